```python
import math
import jax, jax.numpy as jnp
from jax import lax
import numpy as np

D_MODEL = 1024
BATCH = 8
SEQ = 8192
DEPTH = 2

EXPAND = 2
D_INNER = EXPAND * D_MODEL
S5_WIDTH = D_INNER // 2
S5_GROUP = 16
S5_GROUPS = S5_WIDTH // S5_GROUP
S5_STATE = 64
RET_WIDTH = D_INNER - S5_WIDTH
RET_HEADS = 4
RET_DK = RET_WIDTH // RET_HEADS
RET_DV = RET_WIDTH // RET_HEADS
RET_CHUNK = 128
ROPE_BASE = 10000.0
EVEN_SPLITS = [S5_WIDTH, S5_WIDTH, RET_HEADS * RET_DK, RET_HEADS * RET_DK, RET_HEADS * RET_DV, RET_HEADS * RET_DV]
EVEN_IN = sum(EVEN_SPLITS)
SGU_WIDTH = D_INNER
SGU_GROUPS = 4
SGU_GROUP_DIM = SGU_WIDTH // SGU_GROUPS
SGU_CHUNK = 128
ODD_IN = 3 * SGU_WIDTH
N_EVEN = (DEPTH + 1) // 2
N_ODD = DEPTH // 2
NORM_EPS = 1e-6

kernel_name = "hybrid_s5_retention_sgu_block"

F32 = jnp.float32


def rms_norm(x, g):
    xf = x.astype(F32)
    y = xf * lax.rsqrt(jnp.mean(xf * xf, axis=-1, keepdims=True) + NORM_EPS)
    return (y * g.astype(F32)).astype(x.dtype)


def rotary(x, pos):
    half = x.shape[-1] // 2
    inv = ROPE_BASE ** (-jnp.arange(half, dtype=F32) / half)
    ang = pos[:, None] * inv[None, :]
    cos = jnp.cos(ang)[None, :, None, :]
    sin = jnp.sin(ang)[None, :, None, :]
    x1, x2 = x[..., :half], x[..., half:]
    return jnp.concatenate([x1 * cos - x2 * sin, x1 * sin + x2 * cos], axis=-1)


def s5_branch(u, lam_re, lam_im, log_dt, b_re, b_im, c_re, c_im, d_skip, w_glu, b_glu):
    bsz, seq, _ = u.shape
    uf = u.astype(F32).reshape(bsz, seq, S5_GROUPS, S5_GROUP)
    lr = jnp.minimum(lam_re.astype(F32), -1e-4)
    li = lam_im.astype(F32)
    dt = jnp.exp(log_dt.astype(F32))[:, None]
    mag = jnp.exp(lr * dt)
    ab_re = mag * jnp.cos(li * dt)
    ab_im = mag * jnp.sin(li * dt)
    den = lr * lr + li * li
    n_re = ab_re - 1.0
    n_im = ab_im
    z_re = (n_re * lr + n_im * li) / den
    z_im = (n_im * lr - n_re * li) / den
    br = b_re.astype(F32)
    bi = b_im.astype(F32)
    bb_re = z_re[..., None] * br - z_im[..., None] * bi
    bb_im = z_re[..., None] * bi + z_im[..., None] * br
    bu_re = jnp.einsum('gph,blgh->blgp', bb_re, uf)
    bu_im = jnp.einsum('gph,blgh->blgp', bb_im, uf)
    a_re = jnp.broadcast_to(ab_re, bu_re.shape)
    a_im = jnp.broadcast_to(ab_im, bu_im.shape)

    def combine(left, right):
        a1r, a1i, b1r, b1i = left
        a2r, a2i, b2r, b2i = right
        return (a2r * a1r - a2i * a1i,
                a2r * a1i + a2i * a1r,
                a2r * b1r - a2i * b1i + b2r,
                a2r * b1i + a2i * b1r + b2i)

    _, _, s_re, s_im = lax.associative_scan(combine, (a_re, a_im, bu_re, bu_im), axis=1)
    y = (jnp.einsum('ghp,blgp->blgh', c_re.astype(F32), s_re)
         - jnp.einsum('ghp,blgp->blgh', c_im.astype(F32), s_im))
    y = y + d_skip.astype(F32).reshape(S5_GROUPS, S5_GROUP) * uf
    y = jax.nn.gelu(y.reshape(bsz, seq, S5_WIDTH))
    y = y * jax.nn.sigmoid(y @ w_glu.astype(F32) + b_glu.astype(F32))
    return y.astype(u.dtype)


def retention_branch(q, k, v, gn_gain):
    bsz, seq, _ = q.shape
    nc = seq // RET_CHUNK
    pos = jnp.arange(seq, dtype=F32)
    qh = rotary(q.astype(F32).reshape(bsz, seq, RET_HEADS, RET_DK), pos)
    kh = rotary(k.astype(F32).reshape(bsz, seq, RET_HEADS, RET_DK), pos) * (RET_DK ** -0.5)
    vh = v.astype(F32).reshape(bsz, seq, RET_HEADS, RET_DV)
    log_g = jnp.log1p(-jnp.exp2(-5.0 - jnp.arange(RET_HEADS, dtype=F32)))
    idx = jnp.arange(RET_CHUNK, dtype=F32)
    diff = idx[:, None] - idx[None, :]
    decay = jnp.where(diff >= 0, jnp.exp(log_g[:, None, None] * jnp.maximum(diff, 0.0)), 0.0)
    xi = jnp.exp(log_g[None, :] * (idx[:, None] + 1.0))
    zeta = jnp.exp(log_g[None, :] * (RET_CHUNK - 1.0 - idx[:, None]))
    chunk_decay = jnp.exp(log_g * RET_CHUNK)
    qc = qh.reshape(bsz, nc, RET_CHUNK, RET_HEADS, RET_DK)
    kc = kh.reshape(bsz, nc, RET_CHUNK, RET_HEADS, RET_DK)
    vc = vh.reshape(bsz, nc, RET_CHUNK, RET_HEADS, RET_DV)
    scores = jnp.einsum('bcnhk,bcmhk->bchnm', qc, kc) * decay[None, None]
    inner = jnp.einsum('bchnm,bcmhv->bcnhv', scores, vc)
    local = jnp.einsum('bcmhk,bcmhv->bchkv', kc * zeta[None, None, :, :, None], vc)

    def step(state, s_chunk):
        return state * chunk_decay[None, :, None, None] + s_chunk, state

    init = jnp.zeros((bsz, RET_HEADS, RET_DK, RET_DV), F32)
    _, prev = lax.scan(step, init, jnp.moveaxis(local, 1, 0))
    prev = jnp.moveaxis(prev, 0, 1)
    cross = jnp.einsum('bcnhk,bchkv->bcnhv', qc * xi[None, None, :, :, None], prev)
    o = (inner + cross).reshape(bsz, seq, RET_HEADS, RET_DV)
    mu = jnp.mean(o, axis=-1, keepdims=True)
    var = jnp.mean(jnp.square(o - mu), axis=-1, keepdims=True)
    o = (o - mu) * lax.rsqrt(var + NORM_EPS)
    o = o.reshape(bsz, seq, RET_HEADS * RET_DV) * gn_gain.astype(F32)
    return o.astype(q.dtype)


def spatial_gating_branch(h, v_gain, w_s, b_s):
    bsz, seq, _ = h.shape
    nc = seq // SGU_CHUNK
    u, v = h[..., :SGU_WIDTH], h[..., SGU_WIDTH:]
    vf = v.astype(F32)
    mu = jnp.mean(vf, axis=-1, keepdims=True)
    var = jnp.mean(jnp.square(vf - mu), axis=-1, keepdims=True)
    vf = (vf - mu) * lax.rsqrt(var + NORM_EPS) * v_gain.astype(F32)
    vc = vf.reshape(bsz, nc, SGU_CHUNK, SGU_GROUPS, SGU_GROUP_DIM)
    mask = jnp.tril(jnp.ones((SGU_CHUNK, SGU_CHUNK), dtype=bool))
    w = jnp.where(mask[None], w_s.astype(F32), 0.0)
    s = jnp.einsum('gts,bcsgd->bctgd', w, vc) + b_s.astype(F32).T[None, None, :, :, None]
    return (u.astype(F32) * s.reshape(bsz, seq, SGU_WIDTH)).astype(h.dtype)


def _fwd_setup_inputs(seed: int = 0) -> dict:
    key = jax.random.key(seed)
    ks = jax.random.split(key, 24)
    nrm = lambda k, shape, scale: jax.random.normal(k, shape, F32) * scale
    x = jax.random.normal(ks[0], (BATCH, SEQ, D_MODEL), F32)
    norm_even = 1.0 + nrm(ks[1], (N_EVEN, D_MODEL), 0.02)
    w_in_even = nrm(ks[2], (N_EVEN, D_MODEL, EVEN_IN), D_MODEL ** -0.5)
    s5_lam_re = -0.5 + nrm(ks[3], (N_EVEN, S5_GROUPS, S5_STATE), 0.01)
    s5_lam_im = math.pi * jnp.arange(S5_STATE, dtype=F32)[None, None, :] + nrm(ks[4], (N_EVEN, S5_GROUPS, S5_STATE), 0.01)
    s5_log_dt = jax.random.uniform(ks[5], (N_EVEN, S5_GROUPS), F32, math.log(0.001), math.log(0.1))
    s5_b_re = nrm(ks[6], (N_EVEN, S5_GROUPS, S5_STATE, S5_GROUP), (2 * S5_GROUP) ** -0.5)
    s5_b_im = nrm(ks[7], (N_EVEN, S5_GROUPS, S5_STATE, S5_GROUP), (2 * S5_GROUP) ** -0.5)
    s5_c_re = nrm(ks[8], (N_EVEN, S5_GROUPS, S5_GROUP, S5_STATE), (2 * S5_STATE) ** -0.5)
    s5_c_im = nrm(ks[9], (N_EVEN, S5_GROUPS, S5_GROUP, S5_STATE), (2 * S5_STATE) ** -0.5)
    s5_d = nrm(ks[10], (N_EVEN, S5_WIDTH), 1.0)
    s5_w_glu = nrm(ks[11], (N_EVEN, S5_WIDTH, S5_WIDTH), S5_WIDTH ** -0.5)
    s5_b_glu = nrm(ks[12], (N_EVEN, S5_WIDTH), 0.01)
    ret_gn_gain = 1.0 + nrm(ks[13], (N_EVEN, RET_HEADS * RET_DV), 0.02)
    w_out_even = nrm(ks[14], (N_EVEN, D_INNER, D_MODEL), D_INNER ** -0.5)
    norm_odd = 1.0 + nrm(ks[15], (N_ODD, D_MODEL), 0.02)
    w_in_odd = nrm(ks[16], (N_ODD, D_MODEL, ODD_IN), D_MODEL ** -0.5)
    sgu_norm_gain = 1.0 + nrm(ks[17], (N_ODD, SGU_WIDTH), 0.02)
    sgu_w_spatial = nrm(ks[18], (N_ODD, SGU_GROUPS, SGU_CHUNK, SGU_CHUNK), SGU_CHUNK ** -0.5)
    sgu_b_spatial = 1.0 + nrm(ks[19], (N_ODD, SGU_GROUPS, SGU_CHUNK), 0.02)
    w_out_odd = nrm(ks[20], (N_ODD, SGU_WIDTH, D_MODEL), SGU_WIDTH ** -0.5)
    final_norm = 1.0 + nrm(ks[21], (D_MODEL,), 0.02)
    return {"x": x, "norm_even": norm_even, "w_in_even": w_in_even,
            "s5_lam_re": s5_lam_re, "s5_lam_im": s5_lam_im, "s5_log_dt": s5_log_dt,
            "s5_b_re": s5_b_re, "s5_b_im": s5_b_im, "s5_c_re": s5_c_re, "s5_c_im": s5_c_im,
            "s5_d": s5_d, "s5_w_glu": s5_w_glu, "s5_b_glu": s5_b_glu,
            "ret_gn_gain": ret_gn_gain, "w_out_even": w_out_even,
            "norm_odd": norm_odd, "w_in_odd": w_in_odd, "sgu_norm_gain": sgu_norm_gain,
            "sgu_w_spatial": sgu_w_spatial, "sgu_b_spatial": sgu_b_spatial,
            "w_out_odd": w_out_odd, "final_norm": final_norm}


def _fwd_reference(x, norm_even, w_in_even, s5_lam_re, s5_lam_im, s5_log_dt, s5_b_re, s5_b_im,
              s5_c_re, s5_c_im, s5_d, s5_w_glu, s5_b_glu, ret_gn_gain, w_out_even,
              norm_odd, w_in_odd, sgu_norm_gain, sgu_w_spatial, sgu_b_spatial, w_out_odd,
              final_norm):
    split_pts = [int(p) for p in np.cumsum(EVEN_SPLITS)[:-1]]
    for layer in range(DEPTH):
        i = layer // 2
        if layer % 2 == 0:
            h = rms_norm(x, norm_even[i])
            p = h @ w_in_even[i]
            a_u, a_z, q, k, v, b_z = jnp.split(p, split_pts, axis=-1)
            ya = s5_branch(a_u, s5_lam_re[i], s5_lam_im[i], s5_log_dt[i], s5_b_re[i], s5_b_im[i],
                           s5_c_re[i], s5_c_im[i], s5_d[i], s5_w_glu[i], s5_b_glu[i]) * jax.nn.silu(a_z)
            yb = retention_branch(q, k, v, ret_gn_gain[i]) * jax.nn.silu(b_z)
            x = x + jnp.concatenate([ya, yb], axis=-1) @ w_out_even[i]
        else:
            h = rms_norm(x, norm_odd[i])
            p = h @ w_in_odd[i]
            hz = jax.nn.gelu(p[..., :2 * SGU_WIDTH])
            z = p[..., 2 * SGU_WIDTH:]
            y = spatial_gating_branch(hz, sgu_norm_gain[i], sgu_w_spatial[i], sgu_b_spatial[i]) * jax.nn.silu(z)
            x = x + y @ w_out_odd[i]
    return rms_norm(x, final_norm)


import jax as _jax
import jax.numpy as _jnp

TWIN_FORMAT = 'train_step'
FWD_PARAMS = ['x', 'norm_even', 'w_in_even', 's5_lam_re', 's5_lam_im', 's5_log_dt', 's5_b_re', 's5_b_im', 's5_c_re', 's5_c_im', 's5_d', 's5_w_glu', 's5_b_glu', 'ret_gn_gain', 'w_out_even', 'norm_odd', 'w_in_odd', 'sgu_norm_gain', 'sgu_w_spatial', 'sgu_b_spatial', 'w_out_odd', 'final_norm']
TWIN_WEIGHTS = ['norm_even', 'w_in_even', 's5_lam_re', 's5_lam_im', 's5_log_dt', 's5_b_re', 's5_b_im', 's5_c_re', 's5_c_im', 's5_d', 's5_w_glu', 's5_b_glu', 'ret_gn_gain', 'w_out_even', 'norm_odd', 'w_in_odd', 'sgu_norm_gain', 'sgu_w_spatial', 'sgu_b_spatial', 'w_out_odd', 'final_norm']
TWIN_DIFF_INPUT = 'x'
TWIN_INPUTS = ['x', 'norm_even', 'w_in_even', 's5_lam_re', 's5_lam_im', 's5_log_dt', 's5_b_re', 's5_b_im', 's5_c_re', 's5_c_im', 's5_d', 's5_w_glu', 's5_b_glu', 'ret_gn_gain', 'w_out_even', 'norm_odd', 'w_in_odd', 'sgu_norm_gain', 'sgu_w_spatial', 'sgu_b_spatial', 'w_out_odd', 'final_norm', 'loss_target', 'm_norm_even', 'm_w_in_even', 'm_s5_lam_re', 'm_s5_lam_im', 'm_s5_log_dt', 'm_s5_b_re', 'm_s5_b_im', 'm_s5_c_re', 'm_s5_c_im', 'm_s5_d', 'm_s5_w_glu', 'm_s5_b_glu', 'm_ret_gn_gain', 'm_w_out_even', 'm_norm_odd', 'm_w_in_odd', 'm_sgu_norm_gain', 'm_sgu_w_spatial', 'm_sgu_b_spatial', 'm_w_out_odd', 'm_final_norm', 'v_norm_even', 'v_w_in_even', 'v_s5_lam_re', 'v_s5_lam_im', 'v_s5_log_dt', 'v_s5_b_re', 'v_s5_b_im', 'v_s5_c_re', 'v_s5_c_im', 'v_s5_d', 'v_s5_w_glu', 'v_s5_b_glu', 'v_ret_gn_gain', 'v_w_out_even', 'v_norm_odd', 'v_w_in_odd', 'v_sgu_norm_gain', 'v_sgu_w_spatial', 'v_sgu_b_spatial', 'v_w_out_odd', 'v_final_norm']
TWIN_OUTPUTS = ['loss', 'grad_x', 'grad_norm_even', 'grad_w_in_even', 'grad_s5_lam_re', 'grad_s5_lam_im', 'grad_s5_log_dt', 'grad_s5_b_re', 'grad_s5_b_im', 'grad_s5_c_re', 'grad_s5_c_im', 'grad_s5_d', 'grad_s5_w_glu', 'grad_s5_b_glu', 'grad_ret_gn_gain', 'grad_w_out_even', 'grad_norm_odd', 'grad_w_in_odd', 'grad_sgu_norm_gain', 'grad_sgu_w_spatial', 'grad_sgu_b_spatial', 'grad_w_out_odd', 'grad_final_norm', 'delta_norm_even', 'delta_w_in_even', 'delta_s5_lam_re', 'delta_s5_lam_im', 'delta_s5_log_dt', 'delta_s5_b_re', 'delta_s5_b_im', 'delta_s5_c_re', 'delta_s5_c_im', 'delta_s5_d', 'delta_s5_w_glu', 'delta_s5_b_glu', 'delta_ret_gn_gain', 'delta_w_out_even', 'delta_norm_odd', 'delta_w_in_odd', 'delta_sgu_norm_gain', 'delta_sgu_w_spatial', 'delta_sgu_b_spatial', 'delta_w_out_odd', 'delta_final_norm', 'new_m_norm_even', 'new_m_w_in_even', 'new_m_s5_lam_re', 'new_m_s5_lam_im', 'new_m_s5_log_dt', 'new_m_s5_b_re', 'new_m_s5_b_im', 'new_m_s5_c_re', 'new_m_s5_c_im', 'new_m_s5_d', 'new_m_s5_w_glu', 'new_m_s5_b_glu', 'new_m_ret_gn_gain', 'new_m_w_out_even', 'new_m_norm_odd', 'new_m_w_in_odd', 'new_m_sgu_norm_gain', 'new_m_sgu_w_spatial', 'new_m_sgu_b_spatial', 'new_m_w_out_odd', 'new_m_final_norm', 'new_v_norm_even', 'new_v_w_in_even', 'new_v_s5_lam_re', 'new_v_s5_lam_im', 'new_v_s5_log_dt', 'new_v_s5_b_re', 'new_v_s5_b_im', 'new_v_s5_c_re', 'new_v_s5_c_im', 'new_v_s5_d', 'new_v_s5_w_glu', 'new_v_s5_b_glu', 'new_v_ret_gn_gain', 'new_v_w_out_even', 'new_v_norm_odd', 'new_v_w_in_odd', 'new_v_sgu_norm_gain', 'new_v_sgu_w_spatial', 'new_v_sgu_b_spatial', 'new_v_w_out_odd', 'new_v_final_norm']
TWIN_LEAF_KINDS = {'loss': 'loss', 'grad_x': 'grad_x', 'grad_norm_even': 'grad_w', 'grad_w_in_even': 'grad_w', 'grad_s5_lam_re': 'grad_w', 'grad_s5_lam_im': 'grad_w', 'grad_s5_log_dt': 'grad_w', 'grad_s5_b_re': 'grad_w', 'grad_s5_b_im': 'grad_w', 'grad_s5_c_re': 'grad_w', 'grad_s5_c_im': 'grad_w', 'grad_s5_d': 'grad_w', 'grad_s5_w_glu': 'grad_w', 'grad_s5_b_glu': 'grad_w', 'grad_ret_gn_gain': 'grad_w', 'grad_w_out_even': 'grad_w', 'grad_norm_odd': 'grad_w', 'grad_w_in_odd': 'grad_w', 'grad_sgu_norm_gain': 'grad_w', 'grad_sgu_w_spatial': 'grad_w', 'grad_sgu_b_spatial': 'grad_w', 'grad_w_out_odd': 'grad_w', 'grad_final_norm': 'grad_w', 'delta_norm_even': 'delta_w', 'delta_w_in_even': 'delta_w', 'delta_s5_lam_re': 'delta_w', 'delta_s5_lam_im': 'delta_w', 'delta_s5_log_dt': 'delta_w', 'delta_s5_b_re': 'delta_w', 'delta_s5_b_im': 'delta_w', 'delta_s5_c_re': 'delta_w', 'delta_s5_c_im': 'delta_w', 'delta_s5_d': 'delta_w', 'delta_s5_w_glu': 'delta_w', 'delta_s5_b_glu': 'delta_w', 'delta_ret_gn_gain': 'delta_w', 'delta_w_out_even': 'delta_w', 'delta_norm_odd': 'delta_w', 'delta_w_in_odd': 'delta_w', 'delta_sgu_norm_gain': 'delta_w', 'delta_sgu_w_spatial': 'delta_w', 'delta_sgu_b_spatial': 'delta_w', 'delta_w_out_odd': 'delta_w', 'delta_final_norm': 'delta_w', 'new_m_norm_even': 'new_m', 'new_m_w_in_even': 'new_m', 'new_m_s5_lam_re': 'new_m', 'new_m_s5_lam_im': 'new_m', 'new_m_s5_log_dt': 'new_m', 'new_m_s5_b_re': 'new_m', 'new_m_s5_b_im': 'new_m', 'new_m_s5_c_re': 'new_m', 'new_m_s5_c_im': 'new_m', 'new_m_s5_d': 'new_m', 'new_m_s5_w_glu': 'new_m', 'new_m_s5_b_glu': 'new_m', 'new_m_ret_gn_gain': 'new_m', 'new_m_w_out_even': 'new_m', 'new_m_norm_odd': 'new_m', 'new_m_w_in_odd': 'new_m', 'new_m_sgu_norm_gain': 'new_m', 'new_m_sgu_w_spatial': 'new_m', 'new_m_sgu_b_spatial': 'new_m', 'new_m_w_out_odd': 'new_m', 'new_m_final_norm': 'new_m', 'new_v_norm_even': 'new_v', 'new_v_w_in_even': 'new_v', 'new_v_s5_lam_re': 'new_v', 'new_v_s5_lam_im': 'new_v', 'new_v_s5_log_dt': 'new_v', 'new_v_s5_b_re': 'new_v', 'new_v_s5_b_im': 'new_v', 'new_v_s5_c_re': 'new_v', 'new_v_s5_c_im': 'new_v', 'new_v_s5_d': 'new_v', 'new_v_s5_w_glu': 'new_v', 'new_v_s5_b_glu': 'new_v', 'new_v_ret_gn_gain': 'new_v', 'new_v_w_out_even': 'new_v', 'new_v_norm_odd': 'new_v', 'new_v_w_in_odd': 'new_v', 'new_v_sgu_norm_gain': 'new_v', 'new_v_sgu_w_spatial': 'new_v', 'new_v_sgu_b_spatial': 'new_v', 'new_v_w_out_odd': 'new_v', 'new_v_final_norm': 'new_v'}


def _forward(args):
    return _fwd_reference(*[args[k] for k in FWD_PARAMS])


def _output_shape():
    def fwd():
        inp = _fwd_setup_inputs(0)
        return _fwd_reference(*[inp[k] for k in FWD_PARAMS])
    out = _jax.eval_shape(fwd)
    return out.shape, out.dtype

N_MICROBATCH = 1
ADAM_LR = 0.001
ADAM_B1 = 0.9
ADAM_B2 = 0.999
ADAM_EPS = 1e-08
ADAM_WD = 0.01
ADAM_STEP = 10
PER_EXAMPLE_BATCH_AXIS = {'x': 0, 'loss_target': 0}
SHARED_INPUTS = []
_WEIGHT_DTYPES = {'norm_even': _jnp.float32, 'w_in_even': _jnp.float32, 's5_lam_re': _jnp.float32, 's5_lam_im': _jnp.float32, 's5_log_dt': _jnp.float32, 's5_b_re': _jnp.float32, 's5_b_im': _jnp.float32, 's5_c_re': _jnp.float32, 's5_c_im': _jnp.float32, 's5_d': _jnp.float32, 's5_w_glu': _jnp.float32, 's5_b_glu': _jnp.float32, 'ret_gn_gain': _jnp.float32, 'w_out_even': _jnp.float32, 'norm_odd': _jnp.float32, 'w_in_odd': _jnp.float32, 'sgu_norm_gain': _jnp.float32, 'sgu_w_spatial': _jnp.float32, 'sgu_b_spatial': _jnp.float32, 'w_out_odd': _jnp.float32, 'final_norm': _jnp.float32}
MOMENT_SCALE = {'norm_even': 2.259773e-01, 'w_in_even': 9.222763e-02, 's5_lam_re': 2.002740e-03, 's5_lam_im': 2.184232e-03, 's5_log_dt': 1.515309e+00, 's5_b_re': 1.380504e-03, 's5_b_im': 1.366363e-03, 's5_c_re': 2.773142e-03, 's5_c_im': 2.761168e-03, 's5_d': 4.150550e-02, 's5_w_glu': 1.197589e-02, 's5_b_glu': 2.134862e-02, 'ret_gn_gain': 1.100142e-01, 'w_out_even': 1.138304e-01, 'norm_odd': 1.609735e-01, 'w_in_odd': 6.524314e-02, 'sgu_norm_gain': 4.458602e-02, 'sgu_w_spatial': 8.107757e-02, 'sgu_b_spatial': 1.141137e-01, 'w_out_odd': 9.998618e-02, 'final_norm': 6.400533e+01}


def _to_microbatches(a, axis):
    t = _jnp.moveaxis(a, axis, 0)
    t = t.reshape((N_MICROBATCH, t.shape[0] // N_MICROBATCH) + t.shape[1:])
    return _jnp.moveaxis(t, 1, axis + 1)


def setup_inputs(seed: int = 0) -> dict:
    inp = _fwd_setup_inputs(seed)
    key = _jax.random.fold_in(_jax.random.key(seed), 7919)
    shape, _ = _output_shape()
    out = dict(inp)
    out["loss_target"] = _jax.random.normal(_jax.random.fold_in(key, 0), shape, _jnp.float32)
    for i, name in enumerate(TWIN_WEIGHTS):
        w = inp[name].astype(_jnp.float32)
        if MOMENT_SCALE is None:
            s = _jnp.sqrt(_jnp.mean(_jnp.square(w)) + 1e-30)
        else:
            s = MOMENT_SCALE[name]
        km, kv = _jax.random.split(_jax.random.fold_in(key, i + 1))
        out[name] = w
        out["m_" + name] = s * _jax.random.normal(km, w.shape, _jnp.float32)
        out["v_" + name] = (s * s) * _jax.random.uniform(kv, w.shape, _jnp.float32, 0.5, 1.5)
    if N_MICROBATCH > 1:
        for name, axis in PER_EXAMPLE_BATCH_AXIS.items():
            out[name] = _to_microbatches(out[name], axis)
    return {'x': out['x'], 'norm_even': out['norm_even'], 'w_in_even': out['w_in_even'], 's5_lam_re': out['s5_lam_re'], 's5_lam_im': out['s5_lam_im'], 's5_log_dt': out['s5_log_dt'], 's5_b_re': out['s5_b_re'], 's5_b_im': out['s5_b_im'], 's5_c_re': out['s5_c_re'], 's5_c_im': out['s5_c_im'], 's5_d': out['s5_d'], 's5_w_glu': out['s5_w_glu'], 's5_b_glu': out['s5_b_glu'], 'ret_gn_gain': out['ret_gn_gain'], 'w_out_even': out['w_out_even'], 'norm_odd': out['norm_odd'], 'w_in_odd': out['w_in_odd'], 'sgu_norm_gain': out['sgu_norm_gain'], 'sgu_w_spatial': out['sgu_w_spatial'], 'sgu_b_spatial': out['sgu_b_spatial'], 'w_out_odd': out['w_out_odd'], 'final_norm': out['final_norm'], 'loss_target': out['loss_target'], 'm_norm_even': out['m_norm_even'], 'm_w_in_even': out['m_w_in_even'], 'm_s5_lam_re': out['m_s5_lam_re'], 'm_s5_lam_im': out['m_s5_lam_im'], 'm_s5_log_dt': out['m_s5_log_dt'], 'm_s5_b_re': out['m_s5_b_re'], 'm_s5_b_im': out['m_s5_b_im'], 'm_s5_c_re': out['m_s5_c_re'], 'm_s5_c_im': out['m_s5_c_im'], 'm_s5_d': out['m_s5_d'], 'm_s5_w_glu': out['m_s5_w_glu'], 'm_s5_b_glu': out['m_s5_b_glu'], 'm_ret_gn_gain': out['m_ret_gn_gain'], 'm_w_out_even': out['m_w_out_even'], 'm_norm_odd': out['m_norm_odd'], 'm_w_in_odd': out['m_w_in_odd'], 'm_sgu_norm_gain': out['m_sgu_norm_gain'], 'm_sgu_w_spatial': out['m_sgu_w_spatial'], 'm_sgu_b_spatial': out['m_sgu_b_spatial'], 'm_w_out_odd': out['m_w_out_odd'], 'm_final_norm': out['m_final_norm'], 'v_norm_even': out['v_norm_even'], 'v_w_in_even': out['v_w_in_even'], 'v_s5_lam_re': out['v_s5_lam_re'], 'v_s5_lam_im': out['v_s5_lam_im'], 'v_s5_log_dt': out['v_s5_log_dt'], 'v_s5_b_re': out['v_s5_b_re'], 'v_s5_b_im': out['v_s5_b_im'], 'v_s5_c_re': out['v_s5_c_re'], 'v_s5_c_im': out['v_s5_c_im'], 'v_s5_d': out['v_s5_d'], 'v_s5_w_glu': out['v_s5_w_glu'], 'v_s5_b_glu': out['v_s5_b_glu'], 'v_ret_gn_gain': out['v_ret_gn_gain'], 'v_w_out_even': out['v_w_out_even'], 'v_norm_odd': out['v_norm_odd'], 'v_w_in_odd': out['v_w_in_odd'], 'v_sgu_norm_gain': out['v_sgu_norm_gain'], 'v_sgu_w_spatial': out['v_sgu_w_spatial'], 'v_sgu_b_spatial': out['v_sgu_b_spatial'], 'v_w_out_odd': out['v_w_out_odd'], 'v_final_norm': out['v_final_norm']}


def _loss(weights, diff, rest, loss_target):
    with _jax.named_scope("forward"):
        args = {**rest, TWIN_DIFF_INPUT: diff, **{k: w.astype(_WEIGHT_DTYPES[k]) for k, w in weights.items()}}
        y = _forward(args)
    with _jax.named_scope("loss_head"):
        err = _jnp.square(y.astype(_jnp.float32) - loss_target)
        return 0.5 * _jnp.sum(_jnp.mean(err, axis=-1)) if err.ndim else 0.5 * err


def _adamw(w, g, m, v):
    m = ADAM_B1 * m + (1.0 - ADAM_B1) * g
    v = ADAM_B2 * v + (1.0 - ADAM_B2) * _jnp.square(g)
    m_hat = m / (1.0 - ADAM_B1 ** ADAM_STEP)
    v_hat = v / (1.0 - ADAM_B2 ** ADAM_STEP)
    delta = -ADAM_LR * (m_hat / (_jnp.sqrt(v_hat) + ADAM_EPS) + ADAM_WD * w)
    return delta, m, v


def reference(x, norm_even, w_in_even, s5_lam_re, s5_lam_im, s5_log_dt, s5_b_re, s5_b_im, s5_c_re, s5_c_im, s5_d, s5_w_glu, s5_b_glu, ret_gn_gain, w_out_even, norm_odd, w_in_odd, sgu_norm_gain, sgu_w_spatial, sgu_b_spatial, w_out_odd, final_norm, loss_target, m_norm_even, m_w_in_even, m_s5_lam_re, m_s5_lam_im, m_s5_log_dt, m_s5_b_re, m_s5_b_im, m_s5_c_re, m_s5_c_im, m_s5_d, m_s5_w_glu, m_s5_b_glu, m_ret_gn_gain, m_w_out_even, m_norm_odd, m_w_in_odd, m_sgu_norm_gain, m_sgu_w_spatial, m_sgu_b_spatial, m_w_out_odd, m_final_norm, v_norm_even, v_w_in_even, v_s5_lam_re, v_s5_lam_im, v_s5_log_dt, v_s5_b_re, v_s5_b_im, v_s5_c_re, v_s5_c_im, v_s5_d, v_s5_w_glu, v_s5_b_glu, v_ret_gn_gain, v_w_out_even, v_norm_odd, v_w_in_odd, v_sgu_norm_gain, v_sgu_w_spatial, v_sgu_b_spatial, v_w_out_odd, v_final_norm):
    given = dict(x=x, norm_even=norm_even, w_in_even=w_in_even, s5_lam_re=s5_lam_re, s5_lam_im=s5_lam_im, s5_log_dt=s5_log_dt, s5_b_re=s5_b_re, s5_b_im=s5_b_im, s5_c_re=s5_c_re, s5_c_im=s5_c_im, s5_d=s5_d, s5_w_glu=s5_w_glu, s5_b_glu=s5_b_glu, ret_gn_gain=ret_gn_gain, w_out_even=w_out_even, norm_odd=norm_odd, w_in_odd=w_in_odd, sgu_norm_gain=sgu_norm_gain, sgu_w_spatial=sgu_w_spatial, sgu_b_spatial=sgu_b_spatial, w_out_odd=w_out_odd, final_norm=final_norm, loss_target=loss_target, m_norm_even=m_norm_even, m_w_in_even=m_w_in_even, m_s5_lam_re=m_s5_lam_re, m_s5_lam_im=m_s5_lam_im, m_s5_log_dt=m_s5_log_dt, m_s5_b_re=m_s5_b_re, m_s5_b_im=m_s5_b_im, m_s5_c_re=m_s5_c_re, m_s5_c_im=m_s5_c_im, m_s5_d=m_s5_d, m_s5_w_glu=m_s5_w_glu, m_s5_b_glu=m_s5_b_glu, m_ret_gn_gain=m_ret_gn_gain, m_w_out_even=m_w_out_even, m_norm_odd=m_norm_odd, m_w_in_odd=m_w_in_odd, m_sgu_norm_gain=m_sgu_norm_gain, m_sgu_w_spatial=m_sgu_w_spatial, m_sgu_b_spatial=m_sgu_b_spatial, m_w_out_odd=m_w_out_odd, m_final_norm=m_final_norm, v_norm_even=v_norm_even, v_w_in_even=v_w_in_even, v_s5_lam_re=v_s5_lam_re, v_s5_lam_im=v_s5_lam_im, v_s5_log_dt=v_s5_log_dt, v_s5_b_re=v_s5_b_re, v_s5_b_im=v_s5_b_im, v_s5_c_re=v_s5_c_re, v_s5_c_im=v_s5_c_im, v_s5_d=v_s5_d, v_s5_w_glu=v_s5_w_glu, v_s5_b_glu=v_s5_b_glu, v_ret_gn_gain=v_ret_gn_gain, v_w_out_even=v_w_out_even, v_norm_odd=v_norm_odd, v_w_in_odd=v_w_in_odd, v_sgu_norm_gain=v_sgu_norm_gain, v_sgu_w_spatial=v_sgu_w_spatial, v_sgu_b_spatial=v_sgu_b_spatial, v_w_out_odd=v_w_out_odd, v_final_norm=v_final_norm)
    weights = {n: given[n] for n in TWIN_WEIGHTS}
    shared = {n: given[n] for n in SHARED_INPUTS}
    per_example = {n: given[n] for n in ['x']}
    grad_fn = _jax.value_and_grad(_loss, argnums=(0, 1))

    def one_microbatch(ex, loss_target):
        ex = dict(ex)
        diff = ex.pop(TWIN_DIFF_INPUT)
        return grad_fn(weights, diff, {**shared, **ex}, loss_target)

    if N_MICROBATCH == 1:
        loss, (grad_w, grad_x) = one_microbatch(per_example, given["loss_target"])
    else:
        def body(carry, xs):
            loss_sum, grad_sum = carry
            l_k, (gw_k, gx_k) = one_microbatch(xs[0], xs[1])
            with _jax.named_scope("update"):
                return (loss_sum + l_k, _jax.tree.map(_jnp.add, grad_sum, gw_k)), gx_k

        init = (_jnp.zeros((), _jnp.float32), _jax.tree.map(_jnp.zeros_like, weights))
        (loss, grad_w), grad_x = _jax.lax.scan(body, init, (per_example, given["loss_target"]))
    with _jax.named_scope("update"):
        delta_w, new_m, new_v = {}, {}, {}
        for n in TWIN_WEIGHTS:
            delta_w[n], new_m[n], new_v[n] = _adamw(weights[n], grad_w[n], given["m_" + n], given["v_" + n])
    return (loss, grad_x, *[grad_w[n] for n in TWIN_WEIGHTS], *[delta_w[n] for n in TWIN_WEIGHTS],
            *[new_m[n] for n in TWIN_WEIGHTS], *[new_v[n] for n in TWIN_WEIGHTS])
```

```python
import math

import jax
import jax.numpy as jnp
from jax import lax
from jax.experimental import pallas as pl
from jax.experimental.pallas import tpu as pltpu

F32 = jnp.float32
MXU = jnp.bfloat16
AXES = ("x", "y", "c")
NDEV = 8
D = 1024
NIN = 6144
WIN_BLK = NIN // NDEV
DI = 2048
G, P, HG = 64, 64, 16
GB = 8
NJ = G // GB
SW = GB * P
UW = GB * HG
NSTATE = G * P
HEADS, DK = 4, 256
CH = 128
SG, SGD = 4, 512
EPS = 1e-6
ROPE_BASE = 10000.0
VMEM_CAP_V7X = 64 * 1024 * 1024
LOG_G = [math.log1p(-2.0 ** (-5.0 - h)) for h in range(HEADS)]
GELU_C = math.sqrt(2.0 / math.pi)

ADAM_LR, ADAM_B1, ADAM_B2, ADAM_EPS, ADAM_WD, ADAM_STEP = 0.001, 0.9, 0.999, 1e-08, 0.01, 10
BC1 = 1.0 - ADAM_B1 ** ADAM_STEP
BC2 = 1.0 - ADAM_B2 ** ADAM_STEP

SDS = jax.ShapeDtypeStruct
ARB2 = ("arbitrary", "arbitrary")


def _cp(vmem_mib, sem=None):
    kw = dict(vmem_limit_bytes=min(vmem_mib * 1024 * 1024, VMEM_CAP_V7X - 4 * 1024 * 1024))
    if sem is not None:
        kw["dimension_semantics"] = sem
    return pltpu.CompilerParams(**kw)


def _mm(a, b):
    return jnp.dot(a.astype(MXU), b.astype(MXU), preferred_element_type=F32)


def _mm_nt(a, b):
    return lax.dot_general(a.astype(MXU), b.astype(MXU), (((1,), (1,)), ((), ())), preferred_element_type=F32)


def _mm_tn(a, b):
    return lax.dot_general(a.astype(MXU), b.astype(MXU), (((0,), (0,)), ((), ())), preferred_element_type=F32)


def _gelu(x):
    th = jnp.tanh(GELU_C * (x + 0.044715 * x * x * x))
    return 0.5 * x * (1.0 + th)


def _gelu_grad(x):
    th = jnp.tanh(GELU_C * (x + 0.044715 * x * x * x))
    return 0.5 * (1.0 + th) + 0.5 * x * (1.0 - th * th) * GELU_C * (1.0 + 3.0 * 0.044715 * x * x)


def _silu_and_grad(x):
    s = jax.nn.sigmoid(x)
    return x * s, s * (1.0 + x * (1.0 - s))


def _full(shape):
    nd = len(shape)
    return pl.BlockSpec(shape, lambda *_: (0,) * nd)


def _rms(xf):
    r = lax.rsqrt(jnp.mean(xf * xf, axis=-1, keepdims=True) + EPS)
    return xf * r, r


def _in_proj(x, gain, wst, name):
    L = x.shape[0]
    tm = min(512, L)

    def body(x_ref, g_ref, w_ref, o_ref, h_scr):
        @pl.when(pl.program_id(1) == 0)
        def _():
            xhat, _ = _rms(x_ref[...])
            h_scr[...] = (xhat * g_ref[...]).astype(MXU)

        o_ref[...] = jnp.dot(h_scr[...], w_ref[...], preferred_element_type=F32)

    return pl.pallas_call(
        body, name=name, grid=(L // tm, NDEV),
        in_specs=[pl.BlockSpec((tm, D), lambda i, n: (i, 0)), _full((1, D)),
                  pl.BlockSpec((None, D, WIN_BLK), lambda i, n: (n, 0, 0))],
        out_specs=pl.BlockSpec((tm, WIN_BLK), lambda i, n: (i, n)),
        out_shape=SDS((L, NIN), F32),
        scratch_shapes=[pltpu.VMEM((tm, D), MXU)],
        compiler_params=_cp(32, ARB2),
    )(x, gain, wst)


def _in_proj_bwd_x(dp, x, gain, wst, dres, name):
    L = x.shape[0]
    tm = min(512, L)

    def body(dp_ref, x_ref, g_ref, w_ref, dres_ref, dx_ref, gg_ref, acc):
        i, n = pl.program_id(0), pl.program_id(1)

        @pl.when(n == 0)
        def _():
            acc[...] = jnp.zeros_like(acc)

        @pl.when((i == 0) & (n == 0))
        def _():
            gg_ref[...] = jnp.zeros_like(gg_ref)

        acc[...] += _mm_nt(dp_ref[...], w_ref[...])

        @pl.when(n == NDEV - 1)
        def _():
            xhat, r = _rms(x_ref[...])
            dh = acc[...]
            dxhat = dh * g_ref[...]
            dx_ref[...] = dres_ref[...] + r * (dxhat - xhat * jnp.mean(dxhat * xhat, axis=-1, keepdims=True))
            gg_ref[...] += jnp.sum(dh * xhat, axis=0, keepdims=True)

    return pl.pallas_call(
        body, name=name, grid=(L // tm, NDEV),
        in_specs=[pl.BlockSpec((tm, WIN_BLK), lambda i, n: (i, n)),
                  pl.BlockSpec((tm, D), lambda i, n: (i, 0)), _full((1, D)),
                  pl.BlockSpec((None, D, WIN_BLK), lambda i, n: (n, 0, 0)),
                  pl.BlockSpec((tm, D), lambda i, n: (i, 0))],
        out_specs=[pl.BlockSpec((tm, D), lambda i, n: (i, 0)), _full((1, D))],
        out_shape=[SDS((L, D), F32), SDS((1, D), F32)],
        scratch_shapes=[pltpu.VMEM((tm, D), F32)],
        compiler_params=_cp(40, ARB2),
    )(dp, x, gain, wst, dres)


def _wgrad_cols(x, gain, dp, name):
    L = x.shape[0]
    tk = min(512, L)
    nk = L // tk

    def body(x_ref, g_ref, dp_ref, o_ref, acc):
        k = pl.program_id(1)

        @pl.when(k == 0)
        def _():
            acc[...] = jnp.zeros_like(acc)

        xhat, _ = _rms(x_ref[...])
        acc[...] += _mm_tn(xhat * g_ref[...], dp_ref[...])

        @pl.when(k == nk - 1)
        def _():
            o_ref[...] = acc[...].astype(o_ref.dtype)

    return pl.pallas_call(
        body, name=name, grid=(NDEV, nk),
        in_specs=[pl.BlockSpec((tk, D), lambda n, k: (k, 0)), _full((1, D)),
                  pl.BlockSpec((tk, WIN_BLK), lambda n, k: (k, n))],
        out_specs=pl.BlockSpec((None, D, WIN_BLK), lambda n, k: (n, 0, 0)),
        out_shape=SDS((NDEV, D, WIN_BLK), MXU),
        scratch_shapes=[pltpu.VMEM((D, WIN_BLK), F32)],
        compiler_params=_cp(32, ARB2),
    )(x, gain, dp)


def _wgrad_rows(a, b, name):
    L, M = a.shape
    N = b.shape[1]
    tk = min(512, L)
    nk = L // tk
    mb = M // NDEV

    def body(a_ref, b_ref, o_ref, acc):
        k = pl.program_id(1)

        @pl.when(k == 0)
        def _():
            acc[...] = jnp.zeros_like(acc)

        acc[...] += _mm_tn(a_ref[...], b_ref[...])

        @pl.when(k == nk - 1)
        def _():
            o_ref[...] = acc[...].astype(o_ref.dtype)

    return pl.pallas_call(
        body, name=name, grid=(NDEV, nk),
        in_specs=[pl.BlockSpec((tk, mb), lambda m, k: (k, m)), pl.BlockSpec((tk, N), lambda m, k: (k, 0))],
        out_specs=pl.BlockSpec((None, mb, N), lambda m, k: (m, 0, 0)),
        out_shape=SDS((NDEV, mb, N), MXU),
        scratch_shapes=[pltpu.VMEM((mb, N), F32)],
        compiler_params=_cp(32, ARB2),
    )(a, b)


def _s5_disc_fn(lr_raw, li, logdt, br, bi):
    lr = jnp.minimum(lr_raw, -1e-4)
    dt = jnp.exp(logdt)
    mag = jnp.exp(lr * dt)
    abr = mag * jnp.cos(li * dt)
    abi = mag * jnp.sin(li * dt)
    den = lr * lr + li * li
    nre = abr - 1.0
    nim = abi
    zr = (nre * lr + nim * li) / den
    zi = (nim * lr - nre * li) / den
    return abr, abi, zr * br - zi * bi, zr * bi + zi * br


def _s5_disc(lr, li, logdt, br, bi):
    def body(lr_ref, li_ref, dt_ref, br_ref, bi_ref, abr_ref, abi_ref, bbr_ref, bbi_ref):
        abr, abi, bbr, bbi = _s5_disc_fn(lr_ref[...], li_ref[...], dt_ref[...], br_ref[...], bi_ref[...])
        abr_ref[...] = abr
        abi_ref[...] = abi
        bbr_ref[...] = bbr
        bbi_ref[...] = bbi

    s1, s3 = SDS((G, 1, P), F32), SDS((G, HG, P), F32)
    return pl.pallas_call(body, name="s5_disc", out_shape=[s1, s1, s3, s3])(lr, li, logdt, br, bi)


def _s5_disc_bwd(lr, li, logdt, br, bi, dabr, dabi, dbbr, dbbi):
    def body(lr_ref, li_ref, dt_ref, br_ref, bi_ref, c0, c1, c2, c3, o0, o1, o2, o3, o4):
        _, vjp = jax.vjp(_s5_disc_fn, lr_ref[...], li_ref[...], dt_ref[...], br_ref[...], bi_ref[...])
        g = vjp((c0[...], c1[...], c2[...], c3[...]))
        for o, v in zip((o0, o1, o2, o3, o4), g):
            o[...] = v

    s1, s3 = SDS((G, 1, P), F32), SDS((G, HG, P), F32)
    return pl.pallas_call(body, name="s5_disc_bwd", out_shape=[s1, s1, SDS((G, 1, 1), F32), s3, s3])(
        lr, li, logdt, br, bi, dabr, dabi, dbbr, dbbi)


def _s5_tables(abr, abi):
    def body(ar_ref, ai_ref, pfr, pfi, pbr, pbi):
        pfr[0:1, :] = ar_ref[...]
        pfi[0:1, :] = ai_ref[...]
        pbr[CH - 1:CH, :] = ar_ref[...]
        pbi[CH - 1:CH, :] = ai_ref[...]
        n = 1
        while n < CH:
            er, ei = pfr[n - 1:n, :], pfi[n - 1:n, :]
            xr, xi = pfr[0:n, :], pfi[0:n, :]
            pfr[n:2 * n, :] = er * xr - ei * xi
            pfi[n:2 * n, :] = er * xi + ei * xr
            yr, yi = pbr[CH - n:CH, :], pbi[CH - n:CH, :]
            pbr[CH - 2 * n:CH - n, :] = er * yr - ei * yi
            pbi[CH - 2 * n:CH - n, :] = er * yi + ei * yr
            n *= 2

    s = SDS((CH, NSTATE), F32)
    return pl.pallas_call(body, name="s5_tables", out_shape=[s, s, s, s], compiler_params=_cp(40))(abr, abi)


def _cscan(br, bi, pr_ref, pi_ref, reverse):
    T = br.shape[0]
    sign = -1.0 if reverse else 1.0
    row = lax.broadcasted_iota(jnp.int32, br.shape, 0)
    k = 1
    while k < T:
        akr = pr_ref[k - 1:k, :]
        aki = sign * pi_ref[k - 1:k, :]

        def shift(v):
            if k % 8 == 0:
                z = jnp.zeros((k, v.shape[1]), v.dtype)
                return jnp.concatenate([v[k:], z], 0) if reverse else jnp.concatenate([z, v[:T - k]], 0)
            if reverse:
                return jnp.where(row < T - k, pltpu.roll(v, T - k, 0), 0.0)
            return jnp.where(row >= k, pltpu.roll(v, k, 0), 0.0)

        sr, si = shift(br), shift(bi)
        br, bi = br + akr * sr - aki * si, bi + akr * si + aki * sr
        k *= 2
    return br, bi


def _embed(t):
    a, b = t.shape[1], t.shape[2]
    return jnp.einsum("jgab,gh->jgahb", t.reshape(NJ, GB, a, b), jnp.eye(GB, dtype=t.dtype)).reshape(NJ, GB * a, GB * b)


def _diag_blocks(t, a, b):
    return jnp.einsum("jgahb,gh->jgab", t.reshape(NJ, GB, a, GB, b), jnp.eye(GB, dtype=t.dtype)).reshape(G, a, b)


def _s5_scan_fwd(p, bb, cm, pfr, pfi, dskip):
    L = p.shape[0]
    nb = L // CH

    def body(u_ref, bb_ref, cm_ref, pr_ref, pi_ref, d_ref, ypre_ref, st_ref, carry):
        @pl.when(pl.program_id(1) == 0)
        def _():
            carry[...] = jnp.zeros_like(carry)

        c = carry[...]
        st_ref[...] = c
        cr, ci = c[:, :SW], c[:, SW:]
        u = u_ref[...]
        bu = _mm(u, bb_ref[...])
        xr, xi = _cscan(bu[:, :SW], bu[:, SW:], pr_ref, pi_ref, False)
        pr, pi = pr_ref[...], pi_ref[...]
        sr = xr + pr * cr - pi * ci
        si = xi + pr * ci + pi * cr
        carry[...] = jnp.concatenate([sr[CH - 1:CH, :], si[CH - 1:CH, :]], axis=1)
        ypre_ref[...] = _mm(jnp.concatenate([sr, si], axis=1), cm_ref[...]) + d_ref[...] * u

    tab = pl.BlockSpec((CH, SW), lambda j, i: (0, j))
    return pl.pallas_call(
        body, name="s5_scan_fwd", grid=(NJ, nb),
        in_specs=[pl.BlockSpec((CH, UW), lambda j, i: (i, j)),
                  pl.BlockSpec((None, UW, 2 * SW), lambda j, i: (j, 0, 0)),
                  pl.BlockSpec((None, 2 * SW, UW), lambda j, i: (j, 0, 0)),
                  tab, tab, pl.BlockSpec((1, UW), lambda j, i: (0, j))],
        out_specs=[pl.BlockSpec((CH, UW), lambda j, i: (i, j)),
                   pl.BlockSpec((None, None, 1, 2 * SW), lambda j, i: (j, i, 0, 0))],
        out_shape=[SDS((L, D), F32), SDS((NJ, nb, 1, 2 * SW), F32)],
        scratch_shapes=[pltpu.VMEM((1, 2 * SW), F32)],
        compiler_params=_cp(32, ARB2),
    )(p, bb, cm, pfr, pfi, dskip)


def _s5_gate_fwd(ypre, p, wglu, bglu):
    L = ypre.shape[0]
    tm = min(256, L)

    def body(y_ref, az_ref, wg_ref, bg_ref, ya_ref):
        yg = _gelu(y_ref[...])
        t = _mm(yg, wg_ref[...]) + bg_ref[...]
        act, _ = _silu_and_grad(az_ref[...])
        ya_ref[...] = (yg * jax.nn.sigmoid(t) * act).astype(ya_ref.dtype)

    return pl.pallas_call(
        body, name="s5_gate_fwd", grid=(L // tm,),
        in_specs=[pl.BlockSpec((tm, D), lambda i: (i, 0)), pl.BlockSpec((tm, D), lambda i: (i, 1)),
                  _full((D, D)), _full((1, D))],
        out_specs=pl.BlockSpec((tm, D), lambda i: (i, 0)),
        out_shape=SDS((L, D), MXU),
        compiler_params=_cp(32, ("arbitrary",)),
    )(ypre, p, wglu, bglu)


def _s5_gate_bwd(ypre, p, dx1, wout_e, wglu, bglu):
    L = ypre.shape[0]
    tm = min(256, L)

    def body(y_ref, az_ref, dx1_ref, wo_ref, wg_ref, bg_ref, dyp_ref, daz_ref, yg_ref, dt_ref, ya_ref, gbg_ref):
        @pl.when(pl.program_id(0) == 0)
        def _():
            gbg_ref[...] = jnp.zeros_like(gbg_ref)

        ypre_v = y_ref[...]
        yg = _gelu(ypre_v)
        sg = jax.nn.sigmoid(_mm(yg, wg_ref[...]) + bg_ref[...])
        act, dact = _silu_and_grad(az_ref[...])
        y2 = yg * sg
        dya = _mm_nt(dx1_ref[...], wo_ref[...])
        daz_ref[...] = (dya * y2 * dact).astype(daz_ref.dtype)
        dy2 = dya * act
        dt = dy2 * yg * sg * (1.0 - sg)
        dyg = dy2 * sg + _mm_nt(dt, wg_ref[...])
        dyp_ref[...] = dyg * _gelu_grad(ypre_v)
        yg_ref[...] = yg.astype(yg_ref.dtype)
        dt_ref[...] = dt.astype(dt_ref.dtype)
        ya_ref[...] = (y2 * act).astype(ya_ref.dtype)
        gbg_ref[...] += jnp.sum(dt, axis=0, keepdims=True)

    row = pl.BlockSpec((tm, D), lambda i: (i, 0))
    return pl.pallas_call(
        body, name="s5_gate_bwd", grid=(L // tm,),
        in_specs=[row, pl.BlockSpec((tm, D), lambda i: (i, 1)), row,
                  pl.BlockSpec((D, D), lambda i: (0, 0)), _full((D, D)), _full((1, D))],
        out_specs=[row, row, row, row, row, _full((1, D))],
        out_shape=[SDS((L, D), F32), SDS((L, D), MXU), SDS((L, D), MXU), SDS((L, D), MXU), SDS((L, D), MXU),
                   SDS((1, D), F32)],
        compiler_params=_cp(40, ("arbitrary",)),
    )(ypre, p, dx1, wout_e, wglu, bglu)


def _s5_scan_bwd(p, dypre, states, bb, cm, pfr, pfi, pbr, pbi, dskip):
    L = p.shape[0]
    nb = L // CH
    rev = lambda i: nb - 1 - i

    def body(u_ref, dy_ref, st_ref, bb_ref, cm_ref, pr_ref, pi_ref, qr_ref, qi_ref, d_ref,
             du_ref, gd_ref, gcm_ref, gbb_ref, gar_ref, gai_ref, lcarry):
        @pl.when(pl.program_id(1) == 0)
        def _():
            lcarry[...] = jnp.zeros_like(lcarry)
            gd_ref[...] = jnp.zeros_like(gd_ref)
            gcm_ref[...] = jnp.zeros_like(gcm_ref)
            gbb_ref[...] = jnp.zeros_like(gbb_ref)
            gar_ref[...] = jnp.zeros_like(gar_ref)
            gai_ref[...] = jnp.zeros_like(gai_ref)

        u = u_ref[...]
        dyp = dy_ref[...]
        gd_ref[...] += jnp.sum(dyp * u, axis=0, keepdims=True)
        c = st_ref[...]
        cr, ci = c[:, :SW], c[:, SW:]
        bu = _mm(u, bb_ref[...])
        xr, xi = _cscan(bu[:, :SW], bu[:, SW:], pr_ref, pi_ref, False)
        pr, pi = pr_ref[...], pi_ref[...]
        sr = xr + pr * cr - pi * ci
        si = xi + pr * ci + pi * cr
        gcm_ref[...] += _mm_tn(jnp.concatenate([sr, si], axis=1), dyp)
        gs = _mm_nt(dyp, cm_ref[...])
        lr_, li_ = _cscan(gs[:, :SW], gs[:, SW:], pr_ref, pi_ref, True)
        lc = lcarry[...]
        lcr, lci = lc[:, :SW], lc[:, SW:]
        qr, qi = qr_ref[...], -qi_ref[...]
        lr_ = lr_ + qr * lcr - qi * lci
        li_ = li_ + qr * lci + qi * lcr
        lcarry[...] = jnp.concatenate([lr_[0:1, :], li_[0:1, :]], axis=1)
        lam = jnp.concatenate([lr_, li_], axis=1)
        gbb_ref[...] += _mm_tn(u, lam)
        du_ref[...] = (_mm_nt(lam, bb_ref[...]) + dyp * d_ref[...]).astype(du_ref.dtype)
        row = lax.broadcasted_iota(jnp.int32, sr.shape, 0)
        spr = jnp.where(row >= 1, pltpu.roll(sr, 1, 0), cr)
        spi = jnp.where(row >= 1, pltpu.roll(si, 1, 0), ci)
        gar_ref[...] += jnp.sum(lr_ * spr + li_ * spi, axis=0, keepdims=True)
        gai_ref[...] += jnp.sum(li_ * spr - lr_ * spi, axis=0, keepdims=True)

    tab = pl.BlockSpec((CH, SW), lambda j, i: (0, j))
    colblk = pl.BlockSpec((CH, UW), lambda j, i: (rev(i), j))
    vec = lambda w: pl.BlockSpec((1, w), lambda j, i: (0, j))
    return pl.pallas_call(
        body, name="s5_scan_bwd", grid=(NJ, nb),
        in_specs=[colblk, colblk,
                  pl.BlockSpec((None, None, 1, 2 * SW), lambda j, i: (j, rev(i), 0, 0)),
                  pl.BlockSpec((None, UW, 2 * SW), lambda j, i: (j, 0, 0)),
                  pl.BlockSpec((None, 2 * SW, UW), lambda j, i: (j, 0, 0)),
                  tab, tab, tab, tab, vec(UW)],
        out_specs=[colblk, vec(UW),
                   pl.BlockSpec((None, 2 * SW, UW), lambda j, i: (j, 0, 0)),
                   pl.BlockSpec((None, UW, 2 * SW), lambda j, i: (j, 0, 0)),
                   vec(SW), vec(SW)],
        out_shape=[SDS((L, D), MXU), SDS((1, D), F32),
                   SDS((NJ, 2 * SW, UW), F32), SDS((NJ, UW, 2 * SW), F32),
                   SDS((1, NSTATE), F32), SDS((1, NSTATE), F32)],
        scratch_shapes=[pltpu.VMEM((1, 2 * SW), F32)],
        compiler_params=_cp(40, ARB2),
    )(p, dypre, states, bb, cm, pfr, pfi, pbr, pbi, dskip)


def _rope_tables(L, inv):
    tm = min(512, L)

    def body(inv_ref, cos_ref, sin_ref):
        pos = (lax.broadcasted_iota(jnp.int32, (tm, DK // 2), 0) + pl.program_id(0) * tm).astype(F32)
        ang = pos * inv_ref[...]
        cos_ref[...] = jnp.cos(ang)
        sin_ref[...] = jnp.sin(ang)

    blk = pl.BlockSpec((tm, DK // 2), lambda i: (i, 0))
    return pl.pallas_call(body, name="rope_tables", grid=(L // tm,), in_specs=[_full((1, DK // 2))],
                          out_specs=[blk, blk], out_shape=[SDS((L, DK // 2), F32)] * 2)(inv)


def _rot(x, cos, sin):
    x1, x2 = x[:, :DK // 2], x[:, DK // 2:]
    return jnp.concatenate([x1 * cos - x2 * sin, x1 * sin + x2 * cos], axis=1)


def _unrot(d, cos, sin):
    d1, d2 = d[:, :DK // 2], d[:, DK // 2:]
    return jnp.concatenate([d1 * cos + d2 * sin, d2 * cos - d1 * sin], axis=1)


def _head_log_gamma(h):
    lg = jnp.float32(LOG_G[0])
    for hh in range(1, HEADS):
        lg = jnp.where(h == hh, jnp.float32(LOG_G[hh]), lg)
    return lg


def _ret_decays(h):
    lg = _head_log_gamma(h)
    n = lax.broadcasted_iota(jnp.int32, (CH, CH), 0)
    m = lax.broadcasted_iota(jnp.int32, (CH, CH), 1)
    diff = (n - m).astype(F32)
    decay = jnp.where(n >= m, jnp.exp(lg * jnp.maximum(diff, 0.0)), 0.0)
    idx = lax.broadcasted_iota(jnp.int32, (CH, 1), 0).astype(F32)
    xi = jnp.exp(lg * (idx + 1.0))
    zeta = jnp.exp(lg * (CH - 1.0 - idx))
    cd = jnp.exp(jnp.full((1, 1), lg * CH, F32))
    return decay, xi, zeta, cd


def _ret_chunk_fwd(q, k, v, cos, sin, s_prev_b, decay, xi, zeta):
    qr = _rot(q, cos, sin)
    kr = _rot(k, cos, sin) * (DK ** -0.5)
    scores = _mm_nt(qr, kr) * decay
    o = _mm(scores, v) + _mm(qr * xi, s_prev_b)
    local = _mm_tn(kr * zeta, v)
    mu = jnp.mean(o, axis=-1, keepdims=True)
    oc = o - mu
    rstd = lax.rsqrt(jnp.mean(oc * oc, axis=-1, keepdims=True) + EPS)
    return qr, kr, scores, local, oc * rstd, rstd


def _ret_fwd(p, cos, sin, gain):
    L = p.shape[0]
    nb = L // CH

    def body(q_ref, k_ref, v_ref, bz_ref, cos_ref, sin_ref, g_ref, yb_ref, st_ref, state):
        h = pl.program_id(0)

        @pl.when(pl.program_id(1) == 0)
        def _():
            state[...] = jnp.zeros_like(state)

        decay, xi, zeta, cd = _ret_decays(h)
        s_prev = state[...]
        s_prev_b = s_prev.astype(MXU)
        st_ref[...] = s_prev_b
        _, _, _, local, on, _ = _ret_chunk_fwd(q_ref[...], k_ref[...], v_ref[...], cos_ref[...], sin_ref[...],
                                               s_prev_b, decay, xi, zeta)
        state[...] = s_prev * cd + local
        act, _ = _silu_and_grad(bz_ref[...])
        yb_ref[...] = (on * g_ref[...] * act).astype(yb_ref.dtype)

    col = lambda base: pl.BlockSpec((CH, DK), lambda h, i: (i, base + h))
    rope = pl.BlockSpec((CH, DK // 2), lambda h, i: (i, 0))
    return pl.pallas_call(
        body, name="ret_fwd", grid=(HEADS, nb),
        in_specs=[col(8), col(12), col(16), col(20), rope, rope, pl.BlockSpec((1, DK), lambda h, i: (0, h))],
        out_specs=[pl.BlockSpec((CH, DK), lambda h, i: (i, h)),
                   pl.BlockSpec((None, None, DK, DK), lambda h, i: (h, i, 0, 0))],
        out_shape=[SDS((L, D), MXU), SDS((HEADS, nb, DK, DK), MXU)],
        scratch_shapes=[pltpu.VMEM((DK, DK), F32)],
        compiler_params=_cp(32, ARB2),
    )(p, p, p, p, cos, sin, gain)


def _ret_bwd(p, cos, sin, gain, states, dx1, wout_e):
    L = p.shape[0]
    nb = L // CH
    rev = lambda i: nb - 1 - i

    def body(q_ref, k_ref, v_ref, bz_ref, cos_ref, sin_ref, g_ref, st_ref, dx1_ref, wo_ref,
             dq_ref, dk_ref, dv_ref, dbz_ref, yb_ref, gg_ref, gstate):
        h = pl.program_id(0)

        @pl.when(pl.program_id(1) == 0)
        def _():
            gstate[...] = jnp.zeros_like(gstate)
            gg_ref[...] = jnp.zeros_like(gg_ref)

        decay, xi, zeta, cd = _ret_decays(h)
        cos, sin, v = cos_ref[...], sin_ref[...], v_ref[...]
        s_prev_b = st_ref[...]
        qr, kr, scores, _, on, rstd = _ret_chunk_fwd(q_ref[...], k_ref[...], v, cos, sin, s_prev_b, decay, xi, zeta)
        gain_h = g_ref[...]
        bz = bz_ref[...]
        act, dact = _silu_and_grad(bz)
        out = on * gain_h
        yb_ref[...] = (out * act).astype(yb_ref.dtype)
        dyb = _mm_nt(dx1_ref[...], wo_ref[...])
        dbz_ref[...] = (dyb * out * dact).astype(dbz_ref.dtype)
        dout = dyb * act
        gg_ref[...] += jnp.sum(dout * on, axis=0, keepdims=True)
        don = dout * gain_h
        do = rstd * (don - jnp.mean(don, axis=-1, keepdims=True) - on * jnp.mean(don * on, axis=-1, keepdims=True))
        gnext = gstate[...]
        gnext_b = gnext.astype(MXU)
        dscores = _mm_nt(do, v) * decay
        dv_ref[...] = (_mm_tn(scores, do) + _mm(kr * zeta, gnext_b)).astype(dv_ref.dtype)
        dqr = _mm(dscores, kr) + _mm_nt(do, s_prev_b) * xi
        dkr = _mm_tn(dscores, qr) + _mm_nt(v, gnext_b) * zeta
        gstate[...] = gnext * cd + _mm_tn(qr * xi, do)
        dq_ref[...] = _unrot(dqr, cos, sin).astype(dq_ref.dtype)
        dk_ref[...] = (_unrot(dkr, cos, sin) * (DK ** -0.5)).astype(dk_ref.dtype)

    col = lambda base: pl.BlockSpec((CH, DK), lambda h, i: (rev(i), base + h))
    rope = pl.BlockSpec((CH, DK // 2), lambda h, i: (rev(i), 0))
    outc = pl.BlockSpec((CH, DK), lambda h, i: (rev(i), h))
    act_out = SDS((L, D), MXU)
    return pl.pallas_call(
        body, name="ret_bwd", grid=(HEADS, nb),
        in_specs=[col(8), col(12), col(16), col(20), rope, rope, pl.BlockSpec((1, DK), lambda h, i: (0, h)),
                  pl.BlockSpec((None, None, DK, DK), lambda h, i: (h, rev(i), 0, 0)),
                  pl.BlockSpec((CH, D), lambda h, i: (rev(i), 0)),
                  pl.BlockSpec((DK, D), lambda h, i: (4 + h, 0))],
        out_specs=[outc, outc, outc, outc, outc, pl.BlockSpec((1, DK), lambda h, i: (0, h))],
        out_shape=[act_out, act_out, act_out, act_out, act_out, SDS((1, D), F32)],
        scratch_shapes=[pltpu.VMEM((DK, DK), F32)],
        compiler_params=_cp(32, ARB2),
    )(p, p, p, p, cos, sin, gain, states, dx1, wout_e)


def _out_even(x, ya, yb, wout):
    L = x.shape[0]
    tm = min(512, L)

    def body(x_ref, ya_ref, yb_ref, w_ref, o_ref):
        cat = jnp.concatenate([ya_ref[...], yb_ref[...]], axis=1)
        o_ref[...] = x_ref[...] + jnp.dot(cat, w_ref[...], preferred_element_type=F32)

    row = pl.BlockSpec((tm, D), lambda i: (i, 0))
    return pl.pallas_call(
        body, name="out_even", grid=(L // tm,), in_specs=[row, row, row, _full((DI, D))],
        out_specs=row, out_shape=SDS((L, D), F32), compiler_params=_cp(32, ("arbitrary",)),
    )(x, ya, yb, wout)


def _sgu_core(pv, gain, ws_ref, bs_ref):
    pu, pvv, z = pv[:, :DI], pv[:, DI:2 * DI], pv[:, 2 * DI:]
    u = _gelu(pu)
    v = _gelu(pvv)
    mu = jnp.mean(v, axis=-1, keepdims=True)
    vc = v - mu
    rstd = lax.rsqrt(jnp.mean(vc * vc, axis=-1, keepdims=True) + EPS)
    vhat = vc * rstd
    vn = vhat * gain
    t = lax.broadcasted_iota(jnp.int32, (CH, CH), 0)
    s_ = lax.broadcasted_iota(jnp.int32, (CH, CH), 1)
    mask = t >= s_
    wm = [jnp.where(mask, ws_ref[g], 0.0).astype(MXU) for g in range(SG)]
    s = jnp.concatenate([_mm(wm[g], vn[:, g * SGD:(g + 1) * SGD]) + bs_ref[g] for g in range(SG)], axis=1)
    return pu, pvv, z, u, vhat, rstd, vn, mask, wm, s


def _sgu_fwd(p2, x1, gain, wsp, bsp, wout, fnorm, tgt):
    L = p2.shape[0]

    def body(p_ref, x1_ref, g_ref, ws_ref, bs_ref, wo_ref, fn_ref, t_ref, dx2_ref, gfn_ref, loss_ref):
        @pl.when(pl.program_id(0) == 0)
        def _():
            gfn_ref[...] = jnp.zeros_like(gfn_ref)
            loss_ref[...] = jnp.zeros_like(loss_ref)

        _, _, z, u, _, _, _, _, _, s = _sgu_core(p_ref[...], g_ref[...], ws_ref, bs_ref)
        act, _ = _silu_and_grad(z)
        x2 = x1_ref[...] + _mm(u * s * act, wo_ref[...])
        xhat, r = _rms(x2)
        fn = fn_ref[...]
        e = xhat * fn - t_ref[...]
        loss_ref[...] += 0.5 * jnp.sum(jnp.mean(e * e, axis=-1, keepdims=True), axis=0, keepdims=True)
        do = e * (1.0 / D)
        gfn_ref[...] += jnp.sum(do * xhat, axis=0, keepdims=True)
        dxhat = do * fn
        dx2_ref[...] = r * (dxhat - xhat * jnp.mean(dxhat * xhat, axis=-1, keepdims=True))

    row = pl.BlockSpec((CH, D), lambda i: (i, 0))
    return pl.pallas_call(
        body, name="sgu_fwd", grid=(L // CH,),
        in_specs=[pl.BlockSpec((CH, NIN), lambda i: (i, 0)), row, _full((1, DI)), _full((SG, CH, CH)),
                  _full((SG, CH, 1)), _full((DI, D)), _full((1, D)), row],
        out_specs=[row, _full((1, D)), _full((1, 1))],
        out_shape=[SDS((L, D), F32), SDS((1, D), F32), SDS((1, 1), F32)],
        compiler_params=_cp(48, ("arbitrary",)),
    )(p2, x1, gain, wsp, bsp, wout, fnorm, tgt)


def _sgu_bwd(p2, dx2, gain, wsp, bsp, wout):
    L = p2.shape[0]

    def body(p_ref, dx2_ref, g_ref, ws_ref, bs_ref, wo_ref, dp_ref, y_ref, gg_ref, gws_ref, gbs_ref):
        @pl.when(pl.program_id(0) == 0)
        def _():
            gg_ref[...] = jnp.zeros_like(gg_ref)
            gws_ref[...] = jnp.zeros_like(gws_ref)
            gbs_ref[...] = jnp.zeros_like(gbs_ref)

        gain = g_ref[...]
        pu, pvv, z, u, vhat, rstd, vn, mask, wm, s = _sgu_core(p_ref[...], gain, ws_ref, bs_ref)
        act, dact = _silu_and_grad(z)
        y_ref[...] = (u * s * act).astype(y_ref.dtype)
        dy = _mm_nt(dx2_ref[...], wo_ref[...])
        du = dy * s * act
        ds = dy * u * act
        dz = dy * u * s * dact
        dvn = []
        for g in range(SG):
            ds_g = ds[:, g * SGD:(g + 1) * SGD]
            vn_g = vn[:, g * SGD:(g + 1) * SGD]
            gbs_ref[g] += jnp.sum(ds_g, axis=1, keepdims=True)
            gws_ref[g] += jnp.where(mask, _mm_nt(ds_g, vn_g), 0.0)
            dvn.append(_mm_tn(wm[g], ds_g))
        dvn = jnp.concatenate(dvn, axis=1)
        gg_ref[...] += jnp.sum(dvn * vhat, axis=0, keepdims=True)
        dvhat = dvn * gain
        dv = rstd * (dvhat - jnp.mean(dvhat, axis=-1, keepdims=True)
                     - vhat * jnp.mean(dvhat * vhat, axis=-1, keepdims=True))
        dp_ref[...] = jnp.concatenate([du * _gelu_grad(pu), dv * _gelu_grad(pvv), dz], axis=1).astype(dp_ref.dtype)

    return pl.pallas_call(
        body, name="sgu_bwd", grid=(L // CH,),
        in_specs=[pl.BlockSpec((CH, NIN), lambda i: (i, 0)), pl.BlockSpec((CH, D), lambda i: (i, 0)),
                  _full((1, DI)), _full((SG, CH, CH)), _full((SG, CH, 1)), _full((DI, D))],
        out_specs=[pl.BlockSpec((CH, NIN), lambda i: (i, 0)), pl.BlockSpec((CH, DI), lambda i: (i, 0)),
                   _full((1, DI)), _full((SG, CH, CH)), _full((SG, CH, 1))],
        out_shape=[SDS((L, NIN), MXU), SDS((L, DI), MXU), SDS((1, DI), F32), SDS((SG, CH, CH), F32),
                   SDS((SG, CH, 1), F32)],
        compiler_params=_cp(48, ("arbitrary",)),
    )(p2, dx2, gain, wsp, bsp, wout)


def _adamw(w, m, v, gparts, name):
    R, C = w.shape
    tr = R
    for cand in (256, 128, 64, 32, 16, 8):
        if R % cand == 0 and R > cand:
            tr = cand
            break

    def body(w_ref, m_ref, v_ref, gp_ref, g_ref, d_ref, mo_ref, vo_ref):
        g = gp_ref[0].astype(F32)
        for s in range(1, NDEV):
            g = g + gp_ref[s].astype(F32)
        w_ = w_ref[...]
        mn = ADAM_B1 * m_ref[...] + (1.0 - ADAM_B1) * g
        vn = ADAM_B2 * v_ref[...] + (1.0 - ADAM_B2) * (g * g)
        mhat = mn / BC1
        vhat = vn / BC2
        g_ref[...] = g
        d_ref[...] = -ADAM_LR * (mhat / (jnp.sqrt(vhat) + ADAM_EPS) + ADAM_WD * w_)
        mo_ref[...] = mn
        vo_ref[...] = vn

    blk = pl.BlockSpec((tr, C), lambda i: (i, 0))
    out = SDS((R, C), F32)
    return pl.pallas_call(
        body, name=name, grid=(R // tr,),
        in_specs=[blk, blk, blk, pl.BlockSpec((NDEV, tr, C), lambda i: (0, i, 0))],
        out_specs=[blk, blk, blk, blk], out_shape=[out, out, out, out],
        compiler_params=_cp(40, ("arbitrary",)),
    )(w, m, v, gparts)


def _me_and_peers():
    x, y, c = lax.axis_index("x"), lax.axis_index("y"), lax.axis_index("c")
    me = 4 * x + 2 * y + c
    peers = []
    for r in range(1, NDEV):
        px, py, pc = x ^ ((r >> 2) & 1), y ^ ((r >> 1) & 1), c ^ (r & 1)
        peers.append(((px, py, pc), 4 * px + 2 * py + pc))
    return me, peers


def _exchange(arrays, scatter, name):
    n = len(arrays)
    out_shape = [SDS((NDEV,) + (a.shape[1:] if scatter else a.shape), a.dtype) for a in arrays]

    def body(*refs):
        ins, outs = refs[:n], refs[n:2 * n]
        send_sems, recv_sems, loc_sems = refs[2 * n:]
        me, peers = _me_and_peers()
        local = []
        for k in range(n):
            src = ins[k].at[me] if scatter else ins[k]
            cp = pltpu.make_async_copy(src, outs[k].at[me], loc_sems.at[k])
            cp.start()
            local.append(cp)
        sends = []
        for r, (dev, lin) in enumerate(peers):
            for k in range(n):
                src = ins[k].at[lin] if scatter else ins[k]
                cp = pltpu.make_async_remote_copy(
                    src_ref=src, dst_ref=outs[k].at[me], send_sem=send_sems.at[r, k], recv_sem=recv_sems.at[r, k],
                    device_id=dev, device_id_type=pl.DeviceIdType.MESH)
                cp.start()
                sends.append(cp)
        for r, (dev, lin) in enumerate(peers):
            for k in range(n):
                src = ins[k].at[me] if scatter else ins[k]
                pltpu.make_async_remote_copy(
                    src_ref=src, dst_ref=outs[k].at[lin], send_sem=send_sems.at[r, k], recv_sem=recv_sems.at[r, k],
                    device_id=dev, device_id_type=pl.DeviceIdType.MESH).wait_recv()
        for cp in sends:
            cp.wait_send()
        for cp in local:
            cp.wait()

    hbm = pl.BlockSpec(memory_space=pltpu.HBM)
    return pl.pallas_call(
        body, name=name, in_specs=[hbm] * n, out_specs=[hbm] * n, out_shape=out_shape,
        scratch_shapes=[pltpu.SemaphoreType.DMA((NDEV - 1, n)), pltpu.SemaphoreType.DMA((NDEV - 1, n)),
                        pltpu.SemaphoreType.DMA((n,))],
        compiler_params=pltpu.CompilerParams(has_side_effects=True),
    )(*arrays)


def _local_step(x, tgt, norm_even, win_e, lam_re, lam_im, log_dt, b_re, b_im, c_re, c_im, s5_d, wglu, bglu,
                ret_gain, wout_e, norm_odd, win_o, sgu_gain, wsp, bsp, wout_o, fnorm):
    L = x.shape[0]
    lr3, li3 = lam_re.reshape(G, 1, P), lam_im.reshape(G, 1, P)
    dt3 = log_dt.reshape(G, 1, 1)
    br3, bi3 = jnp.swapaxes(b_re, 1, 2), jnp.swapaxes(b_im, 1, 2)
    abr3, abi3, bbr3, bbi3 = _s5_disc(lr3, li3, dt3, br3, bi3)
    bb = jnp.concatenate([_embed(bbr3), _embed(bbi3)], axis=2).astype(MXU)
    cm = jnp.concatenate([_embed(jnp.swapaxes(c_re, 1, 2)), -_embed(jnp.swapaxes(c_im, 1, 2))], axis=1).astype(MXU)
    pfr, pfi, pbr, pbi = _s5_tables(abr3.reshape(1, NSTATE), abi3.reshape(1, NSTATE))
    inv = (ROPE_BASE ** (-jnp.arange(DK // 2, dtype=F32) / (DK // 2))).reshape(1, DK // 2)
    cos, sin = _rope_tables(L, inv)
    bsp3 = bsp.reshape(SG, CH, 1)

    p = _in_proj(x, norm_even, win_e, "in_even")
    ypre, s5_states = _s5_scan_fwd(p, bb, cm, pfr, pfi, s5_d)
    ya = _s5_gate_fwd(ypre, p, wglu, bglu)
    yb, ret_states = _ret_fwd(p, cos, sin, ret_gain)
    x1 = _out_even(x, ya, yb, wout_e)
    p2 = _in_proj(x1, norm_odd, win_o, "in_odd")
    dx2, g_fnorm, loss = _sgu_fwd(p2, x1, sgu_gain, wsp, bsp3, wout_o, fnorm, tgt)

    dp2, y_o, g_sgu_gain, g_wsp, g_bsp = _sgu_bwd(p2, dx2, sgu_gain, wsp, bsp3, wout_o)
    g_wout_o = _wgrad_rows(y_o, dx2.astype(MXU), "wgrad_out_odd")
    dx1, g_norm_odd = _in_proj_bwd_x(dp2, x1, norm_odd, win_o, dx2, "in_odd_bwd")
    g_win_o = _wgrad_cols(x1, norm_odd, dp2, "wgrad_in_odd")

    dypre, daz, yg, dt, ya2, g_bglu = _s5_gate_bwd(ypre, p, dx1, wout_e, wglu, bglu)
    du, g_d, g_cm, g_bb, g_ar, g_ai = _s5_scan_bwd(p, dypre, s5_states, bb, cm, pfr, pfi, pbr, pbi, s5_d)
    dq, dk, dv, dbz, yb2, g_ret_gain = _ret_bwd(p, cos, sin, ret_gain, ret_states, dx1, wout_e)
    dx1b = dx1.astype(MXU)
    g_wout_e = _wgrad_rows(jnp.concatenate([ya2, yb2], axis=1), dx1b, "wgrad_out_even")
    g_wglu = _wgrad_rows(yg, dt, "wgrad_glu")
    dp = jnp.concatenate([du, daz, dq, dk, dv, dbz], axis=1)
    dx, g_norm_even = _in_proj_bwd_x(dp, x, norm_even, win_e, dx1, "in_even_bwd")
    g_win_e = _wgrad_cols(x, norm_even, dp, "wgrad_in_even")

    dbbr3 = _diag_blocks(g_bb[:, :, :SW], HG, P)
    dbbi3 = _diag_blocks(g_bb[:, :, SW:], HG, P)
    g_c_re = jnp.swapaxes(_diag_blocks(g_cm[:, :SW, :], P, HG), 1, 2)
    g_c_im = -jnp.swapaxes(_diag_blocks(g_cm[:, SW:, :], P, HG), 1, 2)
    g_lr3, g_li3, g_dt3, g_br3, g_bi3 = _s5_disc_bwd(
        lr3, li3, dt3, br3, bi3, g_ar.reshape(G, 1, P), g_ai.reshape(G, 1, P), dbbr3, dbbi3)

    big = dict(w_in_even=g_win_e, s5_w_glu=g_wglu, w_out_even=g_wout_e, w_in_odd=g_win_o, w_out_odd=g_wout_o)
    small = dict(
        norm_even=g_norm_even, s5_lam_re=g_lr3.reshape(G, P), s5_lam_im=g_li3.reshape(G, P),
        s5_log_dt=g_dt3.reshape(1, G), s5_b_re=jnp.swapaxes(g_br3, 1, 2), s5_b_im=jnp.swapaxes(g_bi3, 1, 2),
        s5_c_re=g_c_re, s5_c_im=g_c_im, s5_d=g_d, s5_b_glu=g_bglu, ret_gn_gain=g_ret_gain,
        norm_odd=g_norm_odd, sgu_norm_gain=g_sgu_gain, sgu_w_spatial=g_wsp, sgu_b_spatial=g_bsp.reshape(SG, CH),
        final_norm=g_fnorm)
    return loss, dx, big, small


WEIGHTS = ['norm_even', 'w_in_even', 's5_lam_re', 's5_lam_im', 's5_log_dt', 's5_b_re', 's5_b_im', 's5_c_re',
           's5_c_im', 's5_d', 's5_w_glu', 's5_b_glu', 'ret_gn_gain', 'w_out_even', 'norm_odd', 'w_in_odd',
           'sgu_norm_gain', 'sgu_w_spatial', 'sgu_b_spatial', 'w_out_odd', 'final_norm']
BIG = ['w_in_even', 's5_w_glu', 'w_out_even', 'w_in_odd', 'w_out_odd']
SHARDED_SMALL = {'norm_odd': D // NDEV, 'sgu_norm_gain': DI // NDEV}
SMALL = [n for n in WEIGHTS if n not in BIG]
LANES = 128


def kernel(x, norm_even, w_in_even, s5_lam_re, s5_lam_im, s5_log_dt, s5_b_re, s5_b_im, s5_c_re, s5_c_im, s5_d, s5_w_glu, s5_b_glu, ret_gn_gain, w_out_even, norm_odd, w_in_odd, sgu_norm_gain, sgu_w_spatial, sgu_b_spatial, w_out_odd, final_norm, loss_target, m_norm_even, m_w_in_even, m_s5_lam_re, m_s5_lam_im, m_s5_log_dt, m_s5_b_re, m_s5_b_im, m_s5_c_re, m_s5_c_im, m_s5_d, m_s5_w_glu, m_s5_b_glu, m_ret_gn_gain, m_w_out_even, m_norm_odd, m_w_in_odd, m_sgu_norm_gain, m_sgu_w_spatial, m_sgu_b_spatial, m_w_out_odd, m_final_norm, v_norm_even, v_w_in_even, v_s5_lam_re, v_s5_lam_im, v_s5_log_dt, v_s5_b_re, v_s5_b_im, v_s5_c_re, v_s5_c_im, v_s5_d, v_s5_w_glu, v_s5_b_glu, v_ret_gn_gain, v_w_out_even, v_norm_odd, v_w_in_odd, v_sgu_norm_gain, v_sgu_w_spatial, v_sgu_b_spatial, v_w_out_odd, v_final_norm):
    args = dict(locals())
    w = {n: args[n] for n in WEIGHTS}
    m = {n: args["m_" + n] for n in WEIGHTS}
    v = {n: args["v_" + n] for n in WEIGHTS}
    me = 4 * lax.axis_index("x") + 2 * lax.axis_index("y") + lax.axis_index("c")

    shards = [w['w_in_even'][0].astype(MXU), w['s5_w_glu'][0].astype(MXU), w['w_out_even'][0].astype(MXU),
              w['w_in_odd'][0].astype(MXU), w['w_out_odd'][0].astype(MXU), w['norm_odd'], w['sgu_norm_gain']]
    win_e, wglu, wout_e, win_o, wout_o, nodd, sgug = _exchange(shards, False, "gather_weights")
    wglu = wglu.reshape(D, D)
    wout_e = wout_e.reshape(DI, D)
    wout_o = wout_o.reshape(DI, D)
    nodd = nodd.reshape(1, D)
    sgug = sgug.reshape(1, DI)

    loss, dx, big, small = _local_step(
        x[0], loss_target[0], w['norm_even'], win_e, w['s5_lam_re'][0], w['s5_lam_im'][0], w['s5_log_dt'][0],
        w['s5_b_re'][0], w['s5_b_im'][0], w['s5_c_re'][0], w['s5_c_im'][0], w['s5_d'], wglu, w['s5_b_glu'],
        w['ret_gn_gain'], wout_e, nodd, win_o, sgug, w['sgu_w_spatial'][0], w['sgu_b_spatial'][0], wout_o,
        w['final_norm'].reshape(1, D))

    flat = jnp.concatenate([small[n].reshape(-1) for n in SMALL])
    npad = (-flat.shape[0]) % (8 * LANES)
    flat = jnp.pad(flat, (0, npad)).reshape(1, -1)
    bigs = _exchange([big[n] for n in BIG], True, "scatter_grads")
    (gsmall,) = _exchange([flat], False, "gather_small_grads")
    gsmall = gsmall.reshape(NDEV, -1)

    out_g, out_d, out_m, out_v = {}, {}, {}, {}
    for n, parts in zip(BIG, bigs):
        shp = w[n].shape
        r, c = shp[1], shp[2]
        g_, d_, m_, v_ = _adamw(w[n].reshape(r, c), m[n].reshape(r, c), v[n].reshape(r, c), parts, "adamw_" + n)
        out_g[n], out_d[n], out_m[n], out_v[n] = (t.reshape(shp) for t in (g_, d_, m_, v_))
    pieces, off = [], 0
    for n in SMALL:
        size = small[n].size
        seg = gsmall[:, off:off + size]
        if n in SHARDED_SMALL:
            seg = lax.dynamic_slice_in_dim(seg, me * SHARDED_SMALL[n], SHARDED_SMALL[n], axis=1)
        pieces.append(seg)
        off += size
    gp = jnp.concatenate(pieces, axis=1)
    nsw = gp.shape[1]
    wpad = (-nsw) % (8 * LANES)

    def pack(d):
        t = jnp.concatenate([d[n].reshape(-1) for n in SMALL])
        return jnp.pad(t, (0, wpad)).reshape(-1, LANES)

    gp = jnp.pad(gp, ((0, 0), (0, wpad))).reshape(NDEV, -1, LANES)
    sg_, sd_, sm_, sv_ = _adamw(pack(w), pack(m), pack(v), gp, "adamw_small")
    off = 0
    for n in SMALL:
        size = w[n].size
        for dst, src in ((out_g, sg_), (out_d, sd_), (out_m, sm_), (out_v, sv_)):
            dst[n] = src.reshape(-1)[off:off + size].reshape(w[n].shape)
        off += size

    loss_total = lax.psum(loss[0, 0], AXES)
    return (loss_total, dx[None], *[out_g[n] for n in WEIGHTS], *[out_d[n] for n in WEIGHTS],
            *[out_m[n] for n in WEIGHTS], *[out_v[n] for n in WEIGHTS])
```

```python
import math

import jax
import jax.numpy as jnp
from jax import lax
from jax.experimental import pallas as pl
from jax.experimental.pallas import tpu as pltpu

F32 = jnp.float32
MXU = jnp.bfloat16
AXES = ("x", "y", "c")
NDEV = 8
D = 1024
NIN = 6144
WIN_BLK = NIN // NDEV
DI = 2048
G, P, HG = 64, 64, 16
GB = 8
NJ = G // GB
SW = GB * P
UW = GB * HG
NSTATE = G * P
HEADS, DK = 4, 256
CH = 128
SG, SGD = 4, 512
EPS = 1e-6
ROPE_BASE = 10000.0
VMEM_CAP_V7X = 64 * 1024 * 1024
LOG_G = [math.log1p(-2.0 ** (-5.0 - h)) for h in range(HEADS)]
GELU_C = math.sqrt(2.0 / math.pi)

ADAM_LR, ADAM_B1, ADAM_B2, ADAM_EPS, ADAM_WD, ADAM_STEP = 0.001, 0.9, 0.999, 1e-08, 0.01, 10
BC1 = 1.0 - ADAM_B1 ** ADAM_STEP
BC2 = 1.0 - ADAM_B2 ** ADAM_STEP

SDS = jax.ShapeDtypeStruct
ARB2 = ("arbitrary", "arbitrary")


def _cp(vmem_mib, sem=None):
    kw = dict(vmem_limit_bytes=min(vmem_mib * 1024 * 1024, VMEM_CAP_V7X - 4 * 1024 * 1024))
    if sem is not None:
        kw["dimension_semantics"] = sem
    return pltpu.CompilerParams(**kw)


def _mm(a, b):
    return jnp.dot(a.astype(MXU), b.astype(MXU), preferred_element_type=F32)


def _mm_nt(a, b):
    return lax.dot_general(a.astype(MXU), b.astype(MXU), (((1,), (1,)), ((), ())), preferred_element_type=F32)


def _mm_tn(a, b):
    return lax.dot_general(a.astype(MXU), b.astype(MXU), (((0,), (0,)), ((), ())), preferred_element_type=F32)


def _gelu(x):
    th = jnp.tanh(GELU_C * (x + 0.044715 * x * x * x))
    return 0.5 * x * (1.0 + th)


def _gelu_grad(x):
    th = jnp.tanh(GELU_C * (x + 0.044715 * x * x * x))
    return 0.5 * (1.0 + th) + 0.5 * x * (1.0 - th * th) * GELU_C * (1.0 + 3.0 * 0.044715 * x * x)


def _silu_and_grad(x):
    s = jax.nn.sigmoid(x)
    return x * s, s * (1.0 + x * (1.0 - s))


def _full(shape):
    nd = len(shape)
    return pl.BlockSpec(shape, lambda *_: (0,) * nd)


def _rms(xf):
    r = lax.rsqrt(jnp.mean(xf * xf, axis=-1, keepdims=True) + EPS)
    return xf * r, r


def _in_proj(x, gain, wst, name):
    L = x.shape[0]
    tm = min(512, L)

    def body(x_ref, g_ref, w_ref, o_ref, h_scr):
        @pl.when(pl.program_id(1) == 0)
        def _():
            xhat, _ = _rms(x_ref[...])
            h_scr[...] = (xhat * g_ref[...]).astype(MXU)

        o_ref[...] = jnp.dot(h_scr[...], w_ref[...], preferred_element_type=F32)

    return pl.pallas_call(
        body, name=name, grid=(L // tm, NDEV),
        in_specs=[pl.BlockSpec((tm, D), lambda i, n: (i, 0)), _full((1, D)),
                  pl.BlockSpec((None, D, WIN_BLK), lambda i, n: (n, 0, 0))],
        out_specs=pl.BlockSpec((tm, WIN_BLK), lambda i, n: (i, n)),
        out_shape=SDS((L, NIN), F32),
        scratch_shapes=[pltpu.VMEM((tm, D), MXU)],
        compiler_params=_cp(32, ARB2),
    )(x, gain, wst)


def _in_proj_bwd_x(dp, x, gain, wst, dres, name):
    L = x.shape[0]
    tm = min(512, L)

    def body(dp_ref, x_ref, g_ref, w_ref, dres_ref, dx_ref, gg_ref, acc):
        i, n = pl.program_id(0), pl.program_id(1)

        @pl.when(n == 0)
        def _():
            acc[...] = jnp.zeros_like(acc)

        @pl.when((i == 0) & (n == 0))
        def _():
            gg_ref[...] = jnp.zeros_like(gg_ref)

        acc[...] += _mm_nt(dp_ref[...], w_ref[...])

        @pl.when(n == NDEV - 1)
        def _():
            xhat, r = _rms(x_ref[...])
            dh = acc[...]
            dxhat = dh * g_ref[...]
            dx_ref[...] = dres_ref[...] + r * (dxhat - xhat * jnp.mean(dxhat * xhat, axis=-1, keepdims=True))
            gg_ref[...] += jnp.sum(dh * xhat, axis=0, keepdims=True)

    return pl.pallas_call(
        body, name=name, grid=(L // tm, NDEV),
        in_specs=[pl.BlockSpec((tm, WIN_BLK), lambda i, n: (i, n)),
                  pl.BlockSpec((tm, D), lambda i, n: (i, 0)), _full((1, D)),
                  pl.BlockSpec((None, D, WIN_BLK), lambda i, n: (n, 0, 0)),
                  pl.BlockSpec((tm, D), lambda i, n: (i, 0))],
        out_specs=[pl.BlockSpec((tm, D), lambda i, n: (i, 0)), _full((1, D))],
        out_shape=[SDS((L, D), F32), SDS((1, D), F32)],
        scratch_shapes=[pltpu.VMEM((tm, D), F32)],
        compiler_params=_cp(40, ARB2),
    )(dp, x, gain, wst, dres)


def _wgrad_cols(x, gain, dp, name):
    L = x.shape[0]
    tk = min(512, L)
    nk = L // tk

    def body(x_ref, g_ref, dp_ref, o_ref, acc):
        k = pl.program_id(1)

        @pl.when(k == 0)
        def _():
            acc[...] = jnp.zeros_like(acc)

        xhat, _ = _rms(x_ref[...])
        acc[...] += _mm_tn(xhat * g_ref[...], dp_ref[...])

        @pl.when(k == nk - 1)
        def _():
            o_ref[...] = acc[...].astype(o_ref.dtype)

    return pl.pallas_call(
        body, name=name, grid=(NDEV, nk),
        in_specs=[pl.BlockSpec((tk, D), lambda n, k: (k, 0)), _full((1, D)),
                  pl.BlockSpec((tk, WIN_BLK), lambda n, k: (k, n))],
        out_specs=pl.BlockSpec((None, D, WIN_BLK), lambda n, k: (n, 0, 0)),
        out_shape=SDS((NDEV, D, WIN_BLK), MXU),
        scratch_shapes=[pltpu.VMEM((D, WIN_BLK), F32)],
        compiler_params=_cp(32, ARB2),
    )(x, gain, dp)


def _wgrad_rows(a, b, name):
    L, M = a.shape
    N = b.shape[1]
    tk = min(512, L)
    nk = L // tk
    mb = M // NDEV

    def body(a_ref, b_ref, o_ref, acc):
        k = pl.program_id(1)

        @pl.when(k == 0)
        def _():
            acc[...] = jnp.zeros_like(acc)

        acc[...] += _mm_tn(a_ref[...], b_ref[...])

        @pl.when(k == nk - 1)
        def _():
            o_ref[...] = acc[...].astype(o_ref.dtype)

    return pl.pallas_call(
        body, name=name, grid=(NDEV, nk),
        in_specs=[pl.BlockSpec((tk, mb), lambda m, k: (k, m)), pl.BlockSpec((tk, N), lambda m, k: (k, 0))],
        out_specs=pl.BlockSpec((None, mb, N), lambda m, k: (m, 0, 0)),
        out_shape=SDS((NDEV, mb, N), MXU),
        scratch_shapes=[pltpu.VMEM((mb, N), F32)],
        compiler_params=_cp(32, ARB2),
    )(a, b)


def _s5_disc_fn(lr_raw, li, logdt, br, bi):
    lr = jnp.minimum(lr_raw, -1e-4)
    dt = jnp.exp(logdt)
    mag = jnp.exp(lr * dt)
    abr = mag * jnp.cos(li * dt)
    abi = mag * jnp.sin(li * dt)
    den = lr * lr + li * li
    nre = abr - 1.0
    nim = abi
    zr = (nre * lr + nim * li) / den
    zi = (nim * lr - nre * li) / den
    return abr, abi, zr * br - zi * bi, zr * bi + zi * br


def _s5_disc(lr, li, logdt, br, bi):
    def body(lr_ref, li_ref, dt_ref, br_ref, bi_ref, abr_ref, abi_ref, bbr_ref, bbi_ref):
        abr, abi, bbr, bbi = _s5_disc_fn(lr_ref[...], li_ref[...], dt_ref[...], br_ref[...], bi_ref[...])
        abr_ref[...] = abr
        abi_ref[...] = abi
        bbr_ref[...] = bbr
        bbi_ref[...] = bbi

    s1, s3 = SDS((G, 1, P), F32), SDS((G, HG, P), F32)
    return pl.pallas_call(body, name="s5_disc", out_shape=[s1, s1, s3, s3])(lr, li, logdt, br, bi)


def _s5_disc_bwd(lr, li, logdt, br, bi, dabr, dabi, dbbr, dbbi):
    def body(lr_ref, li_ref, dt_ref, br_ref, bi_ref, c0, c1, c2, c3, o0, o1, o2, o3, o4):
        _, vjp = jax.vjp(_s5_disc_fn, lr_ref[...], li_ref[...], dt_ref[...], br_ref[...], bi_ref[...])
        g = vjp((c0[...], c1[...], c2[...], c3[...]))
        for o, v in zip((o0, o1, o2, o3, o4), g):
            o[...] = v

    s1, s3 = SDS((G, 1, P), F32), SDS((G, HG, P), F32)
    return pl.pallas_call(body, name="s5_disc_bwd", out_shape=[s1, s1, SDS((G, 1, 1), F32), s3, s3])(
        lr, li, logdt, br, bi, dabr, dabi, dbbr, dbbi)


def _s5_tables(abr, abi, rows, name):
    def body(ar_ref, ai_ref, pfr, pfi, pbr, pbi):
        pfr[0:1, :] = ar_ref[...]
        pfi[0:1, :] = ai_ref[...]
        pbr[rows - 1:rows, :] = ar_ref[...]
        pbi[rows - 1:rows, :] = ai_ref[...]
        n = 1
        while n < rows:
            er, ei = pfr[n - 1:n, :], pfi[n - 1:n, :]
            xr, xi = pfr[0:n, :], pfi[0:n, :]
            pfr[n:2 * n, :] = er * xr - ei * xi
            pfi[n:2 * n, :] = er * xi + ei * xr
            yr, yi = pbr[rows - n:rows, :], pbi[rows - n:rows, :]
            pbr[rows - 2 * n:rows - n, :] = er * yr - ei * yi
            pbi[rows - 2 * n:rows - n, :] = er * yi + ei * yr
            n *= 2

    s = SDS((rows, NSTATE), F32)
    return pl.pallas_call(body, name=name, out_shape=[s, s, s, s], compiler_params=_cp(40))(abr, abi)


def _cscan(br, bi, pr_ref, pi_ref, reverse):
    T = br.shape[0]
    sign = -1.0 if reverse else 1.0
    row = lax.broadcasted_iota(jnp.int32, br.shape, 0)
    k = 1
    while k < T:
        akr = pr_ref[k - 1:k, :]
        aki = sign * pi_ref[k - 1:k, :]

        def shift(v):
            if k % 8 == 0:
                z = jnp.zeros((k, v.shape[1]), v.dtype)
                return jnp.concatenate([v[k:], z], 0) if reverse else jnp.concatenate([z, v[:T - k]], 0)
            if reverse:
                return jnp.where(row < T - k, pltpu.roll(v, T - k, 0), 0.0)
            return jnp.where(row >= k, pltpu.roll(v, k, 0), 0.0)

        sr, si = shift(br), shift(bi)
        br, bi = br + akr * sr - aki * si, bi + akr * si + aki * sr
        k *= 2
    return br, bi


def _embed(t):
    a, b = t.shape[1], t.shape[2]
    return jnp.einsum("jgab,gh->jgahb", t.reshape(NJ, GB, a, b), jnp.eye(GB, dtype=t.dtype)).reshape(NJ, GB * a, GB * b)


def _diag_blocks(t, a, b):
    return jnp.einsum("jgahb,gh->jgab", t.reshape(NJ, GB, a, GB, b), jnp.eye(GB, dtype=t.dtype)).reshape(G, a, b)


NT = 8


def _chunks(L):
    ncb = min(CH, L // NT)
    return ncb, NT * ncb


def _cmul_add(ar, ai, xr, xi, br, bi):
    return ar * xr - ai * xi + br, ar * xi + ai * xr + bi


def _s5_states(u_ref, bb_ref, ar, ai, par_ref, pai_ref, cr, ci, ncb, bu_scr):
    er = ei = None
    for t in range(NT):
        bu = _mm(u_ref[pl.ds(t, ncb, stride=NT), :], bb_ref[...])
        bu_scr[t] = bu
        if t == 0:
            er, ei = bu[:, :SW], bu[:, SW:]
        else:
            er, ei = _cmul_add(ar, ai, er, ei, bu[:, :SW], bu[:, SW:])
    xr, xi = _cscan(er, ei, par_ref, pai_ref, False)
    fr, fi = _cmul_add(par_ref[0:ncb, :], pai_ref[0:ncb, :], cr, ci, xr, xi)
    row = lax.broadcasted_iota(jnp.int32, fr.shape, 0)
    cinr = jnp.where(row >= 1, pltpu.roll(fr, 1, 0), cr)
    cini = jnp.where(row >= 1, pltpu.roll(fi, 1, 0), ci)
    return cinr, cini, jnp.concatenate([fr[ncb - 1:ncb, :], fi[ncb - 1:ncb, :]], axis=1)


def _s5_scan_fwd(p, bb, cm, abr, abi, par, pai, dskip):
    L = p.shape[0]
    ncb, tb = _chunks(L)
    nb = L // tb

    def body(u_ref, bb_ref, cm_ref, ar_ref, ai_ref, par_ref, pai_ref, d_ref, ypre_ref, st_ref, carry, bu_scr):
        @pl.when(pl.program_id(1) == 0)
        def _():
            carry[...] = jnp.zeros_like(carry)

        c = carry[...]
        st_ref[...] = c
        ar, ai = ar_ref[...], ai_ref[...]
        sr, si, cnext = _s5_states(u_ref, bb_ref, ar, ai, par_ref, pai_ref, c[:, :SW], c[:, SW:], ncb, bu_scr)
        carry[...] = cnext
        for t in range(NT):
            bu = bu_scr[t]
            sr, si = _cmul_add(ar, ai, sr, si, bu[:, :SW], bu[:, SW:])
            rows = pl.ds(t, ncb, stride=NT)
            ypre_ref[rows, :] = _mm(jnp.concatenate([sr, si], axis=1), cm_ref[...]) + d_ref[...] * u_ref[rows, :]

    tab = pl.BlockSpec((CH, SW), lambda j, i: (0, j))
    vec = lambda w: pl.BlockSpec((1, w), lambda j, i: (0, j))
    return pl.pallas_call(
        body, name="s5_scan_fwd", grid=(NJ, nb),
        in_specs=[pl.BlockSpec((tb, UW), lambda j, i: (i, j)),
                  pl.BlockSpec((None, UW, 2 * SW), lambda j, i: (j, 0, 0)),
                  pl.BlockSpec((None, 2 * SW, UW), lambda j, i: (j, 0, 0)),
                  vec(SW), vec(SW), tab, tab, vec(UW)],
        out_specs=[pl.BlockSpec((tb, UW), lambda j, i: (i, j)),
                   pl.BlockSpec((None, None, 1, 2 * SW), lambda j, i: (j, i, 0, 0))],
        out_shape=[SDS((L, D), F32), SDS((NJ, nb, 1, 2 * SW), F32)],
        scratch_shapes=[pltpu.VMEM((1, 2 * SW), F32), pltpu.VMEM((NT, ncb, 2 * SW), F32)],
        compiler_params=_cp(40, ARB2),
    )(p, bb, cm, abr, abi, par, pai, dskip)


def _s5_gate_fwd(ypre, p, wglu, bglu):
    L = ypre.shape[0]
    tm = min(256, L)

    def body(y_ref, az_ref, wg_ref, bg_ref, ya_ref):
        yg = _gelu(y_ref[...])
        t = _mm(yg, wg_ref[...]) + bg_ref[...]
        act, _ = _silu_and_grad(az_ref[...])
        ya_ref[...] = (yg * jax.nn.sigmoid(t) * act).astype(ya_ref.dtype)

    return pl.pallas_call(
        body, name="s5_gate_fwd", grid=(L // tm,),
        in_specs=[pl.BlockSpec((tm, D), lambda i: (i, 0)), pl.BlockSpec((tm, D), lambda i: (i, 1)),
                  _full((D, D)), _full((1, D))],
        out_specs=pl.BlockSpec((tm, D), lambda i: (i, 0)),
        out_shape=SDS((L, D), MXU),
        compiler_params=_cp(32, ("arbitrary",)),
    )(ypre, p, wglu, bglu)


def _s5_gate_bwd(ypre, p, dx1, wout_e, wglu, bglu):
    L = ypre.shape[0]
    tm = min(256, L)

    def body(y_ref, az_ref, dx1_ref, wo_ref, wg_ref, bg_ref, dyp_ref, daz_ref, yg_ref, dt_ref, ya_ref, gbg_ref):
        @pl.when(pl.program_id(0) == 0)
        def _():
            gbg_ref[...] = jnp.zeros_like(gbg_ref)

        ypre_v = y_ref[...]
        yg = _gelu(ypre_v)
        sg = jax.nn.sigmoid(_mm(yg, wg_ref[...]) + bg_ref[...])
        act, dact = _silu_and_grad(az_ref[...])
        y2 = yg * sg
        dya = _mm_nt(dx1_ref[...], wo_ref[...])
        daz_ref[...] = (dya * y2 * dact).astype(daz_ref.dtype)
        dy2 = dya * act
        dt = dy2 * yg * sg * (1.0 - sg)
        dyg = dy2 * sg + _mm_nt(dt, wg_ref[...])
        dyp_ref[...] = dyg * _gelu_grad(ypre_v)
        yg_ref[...] = yg.astype(yg_ref.dtype)
        dt_ref[...] = dt.astype(dt_ref.dtype)
        ya_ref[...] = (y2 * act).astype(ya_ref.dtype)
        gbg_ref[...] += jnp.sum(dt, axis=0, keepdims=True)

    row = pl.BlockSpec((tm, D), lambda i: (i, 0))
    return pl.pallas_call(
        body, name="s5_gate_bwd", grid=(L // tm,),
        in_specs=[row, pl.BlockSpec((tm, D), lambda i: (i, 1)), row,
                  pl.BlockSpec((D, D), lambda i: (0, 0)), _full((D, D)), _full((1, D))],
        out_specs=[row, row, row, row, row, _full((1, D))],
        out_shape=[SDS((L, D), F32), SDS((L, D), MXU), SDS((L, D), MXU), SDS((L, D), MXU), SDS((L, D), MXU),
                   SDS((1, D), F32)],
        compiler_params=_cp(40, ("arbitrary",)),
    )(ypre, p, dx1, wout_e, wglu, bglu)


def _s5_scan_bwd(p, dypre, states, bb, cm, abr, abi, par, pai, pbr, pbi, dskip):
    L = p.shape[0]
    ncb, tb = _chunks(L)
    nb = L // tb
    rev = lambda i: nb - 1 - i

    def body(u_ref, dy_ref, st_ref, bb_ref, cm_ref, ar_ref, ai_ref, par_ref, pai_ref, pbr_ref, pbi_ref, d_ref,
             du_ref, gd_ref, gcm_ref, gbb_ref, gar_ref, gai_ref, lcarry, bu_scr, s_scr, gs_scr):
        @pl.when(pl.program_id(1) == 0)
        def _():
            lcarry[...] = jnp.zeros_like(lcarry)
            gd_ref[...] = jnp.zeros_like(gd_ref)
            gcm_ref[...] = jnp.zeros_like(gcm_ref)
            gbb_ref[...] = jnp.zeros_like(gbb_ref)
            gar_ref[...] = jnp.zeros_like(gar_ref)
            gai_ref[...] = jnp.zeros_like(gai_ref)

        ar, ai = ar_ref[...], ai_ref[...]
        c = st_ref[...]
        sr, si, _ = _s5_states(u_ref, bb_ref, ar, ai, par_ref, pai_ref, c[:, :SW], c[:, SW:], ncb, bu_scr)
        s_scr[0] = jnp.concatenate([sr, si], axis=1)
        for t in range(NT):
            bu = bu_scr[t]
            sr, si = _cmul_add(ar, ai, sr, si, bu[:, :SW], bu[:, SW:])
            s_scr[t + 1] = jnp.concatenate([sr, si], axis=1)
        fr = fi = None
        for t in reversed(range(NT)):
            gs = _mm_nt(dy_ref[pl.ds(t, ncb, stride=NT), :], cm_ref[...])
            gs_scr[t] = gs
            if t == NT - 1:
                fr, fi = gs[:, :SW], gs[:, SW:]
            else:
                fr, fi = _cmul_add(ar, -ai, fr, fi, gs[:, :SW], gs[:, SW:])
        xr, xi = _cscan(fr, fi, par_ref, pai_ref, True)
        lc = lcarry[...]
        lcr, lci = lc[:, :SW], lc[:, SW:]
        hr, hi = _cmul_add(pbr_ref[CH - ncb:CH, :], -pbi_ref[CH - ncb:CH, :], lcr, lci, xr, xi)
        lcarry[...] = jnp.concatenate([hr[0:1, :], hi[0:1, :]], axis=1)
        row = lax.broadcasted_iota(jnp.int32, hr.shape, 0)
        lr_ = jnp.where(row < ncb - 1, pltpu.roll(hr, ncb - 1, 0), lcr)
        li_ = jnp.where(row < ncb - 1, pltpu.roll(hi, ncb - 1, 0), lci)
        gar = jnp.zeros((1, SW), F32)
        gai = jnp.zeros((1, SW), F32)
        for t in reversed(range(NT)):
            gs = gs_scr[t]
            lr_, li_ = _cmul_add(ar, -ai, lr_, li_, gs[:, :SW], gs[:, SW:])
            rows = pl.ds(t, ncb, stride=NT)
            u_t, dy_t = u_ref[rows, :], dy_ref[rows, :]
            lam = jnp.concatenate([lr_, li_], axis=1)
            gbb_ref[...] += _mm_tn(u_t, lam)
            du_ref[rows, :] = _mm_nt(lam, bb_ref[...]) + dy_t * d_ref[...]
            gd_ref[...] += jnp.sum(dy_t * u_t, axis=0, keepdims=True)
            gcm_ref[...] += _mm_tn(s_scr[t + 1], dy_t)
            sp = s_scr[t]
            spr, spi = sp[:, :SW], sp[:, SW:]
            gar += jnp.sum(lr_ * spr + li_ * spi, axis=0, keepdims=True)
            gai += jnp.sum(li_ * spr - lr_ * spi, axis=0, keepdims=True)
        gar_ref[...] += gar
        gai_ref[...] += gai

    tab = pl.BlockSpec((CH, SW), lambda j, i: (0, j))
    colblk = pl.BlockSpec((tb, UW), lambda j, i: (rev(i), j))
    vec = lambda w: pl.BlockSpec((1, w), lambda j, i: (0, j))
    return pl.pallas_call(
        body, name="s5_scan_bwd", grid=(NJ, nb),
        in_specs=[colblk, colblk,
                  pl.BlockSpec((None, None, 1, 2 * SW), lambda j, i: (j, rev(i), 0, 0)),
                  pl.BlockSpec((None, UW, 2 * SW), lambda j, i: (j, 0, 0)),
                  pl.BlockSpec((None, 2 * SW, UW), lambda j, i: (j, 0, 0)),
                  vec(SW), vec(SW), tab, tab, tab, tab, vec(UW)],
        out_specs=[colblk, vec(UW),
                   pl.BlockSpec((None, 2 * SW, UW), lambda j, i: (j, 0, 0)),
                   pl.BlockSpec((None, UW, 2 * SW), lambda j, i: (j, 0, 0)),
                   vec(SW), vec(SW)],
        out_shape=[SDS((L, D), F32), SDS((1, D), F32),
                   SDS((NJ, 2 * SW, UW), F32), SDS((NJ, UW, 2 * SW), F32),
                   SDS((1, NSTATE), F32), SDS((1, NSTATE), F32)],
        scratch_shapes=[pltpu.VMEM((1, 2 * SW), F32), pltpu.VMEM((NT, ncb, 2 * SW), F32),
                        pltpu.VMEM((NT + 1, ncb, 2 * SW), F32), pltpu.VMEM((NT, ncb, 2 * SW), F32)],
        compiler_params=_cp(48, ARB2),
    )(p, dypre, states, bb, cm, abr, abi, par, pai, pbr, pbi, dskip)


def _rope_tables(L, inv):
    tm = min(512, L)

    def body(inv_ref, cos_ref, sin_ref):
        pos = (lax.broadcasted_iota(jnp.int32, (tm, DK // 2), 0) + pl.program_id(0) * tm).astype(F32)
        ang = pos * inv_ref[...]
        cos_ref[...] = jnp.cos(ang)
        sin_ref[...] = jnp.sin(ang)

    blk = pl.BlockSpec((tm, DK // 2), lambda i: (i, 0))
    return pl.pallas_call(body, name="rope_tables", grid=(L // tm,), in_specs=[_full((1, DK // 2))],
                          out_specs=[blk, blk], out_shape=[SDS((L, DK // 2), F32)] * 2)(inv)


def _rot(x, cos, sin):
    x1, x2 = x[:, :DK // 2], x[:, DK // 2:]
    return jnp.concatenate([x1 * cos - x2 * sin, x1 * sin + x2 * cos], axis=1)


def _unrot(d, cos, sin):
    d1, d2 = d[:, :DK // 2], d[:, DK // 2:]
    return jnp.concatenate([d1 * cos + d2 * sin, d2 * cos - d1 * sin], axis=1)


def _head_log_gamma(h):
    lg = jnp.float32(LOG_G[0])
    for hh in range(1, HEADS):
        lg = jnp.where(h == hh, jnp.float32(LOG_G[hh]), lg)
    return lg


def _ret_decays(h):
    lg = _head_log_gamma(h)
    n = lax.broadcasted_iota(jnp.int32, (CH, CH), 0)
    m = lax.broadcasted_iota(jnp.int32, (CH, CH), 1)
    diff = (n - m).astype(F32)
    decay = jnp.where(n >= m, jnp.exp(lg * jnp.maximum(diff, 0.0)), 0.0)
    idx = lax.broadcasted_iota(jnp.int32, (CH, 1), 0).astype(F32)
    xi = jnp.exp(lg * (idx + 1.0))
    zeta = jnp.exp(lg * (CH - 1.0 - idx))
    cd = jnp.exp(jnp.full((1, 1), lg * CH, F32))
    return decay, xi, zeta, cd


def _ret_chunk_fwd(q, k, v, cos, sin, s_prev_b, decay, xi, zeta):
    qr = _rot(q, cos, sin)
    kr = _rot(k, cos, sin) * (DK ** -0.5)
    scores = _mm_nt(qr, kr) * decay
    o = _mm(scores, v) + _mm(qr * xi, s_prev_b)
    local = _mm_tn(kr * zeta, v)
    mu = jnp.mean(o, axis=-1, keepdims=True)
    oc = o - mu
    rstd = lax.rsqrt(jnp.mean(oc * oc, axis=-1, keepdims=True) + EPS)
    return qr, kr, scores, local, oc * rstd, rstd


def _ret_fwd(p, cos, sin, gain):
    L = p.shape[0]
    nb = L // CH

    def body(q_ref, k_ref, v_ref, bz_ref, cos_ref, sin_ref, g_ref, yb_ref, st_ref, state):
        h = pl.program_id(0)

        @pl.when(pl.program_id(1) == 0)
        def _():
            state[...] = jnp.zeros_like(state)

        decay, xi, zeta, cd = _ret_decays(h)
        s_prev = state[...]
        s_prev_b = s_prev.astype(MXU)
        st_ref[...] = s_prev_b
        _, _, _, local, on, _ = _ret_chunk_fwd(q_ref[...], k_ref[...], v_ref[...], cos_ref[...], sin_ref[...],
                                               s_prev_b, decay, xi, zeta)
        state[...] = s_prev * cd + local
        act, _ = _silu_and_grad(bz_ref[...])
        yb_ref[...] = (on * g_ref[...] * act).astype(yb_ref.dtype)

    col = lambda base: pl.BlockSpec((CH, DK), lambda h, i: (i, base + h))
    rope = pl.BlockSpec((CH, DK // 2), lambda h, i: (i, 0))
    return pl.pallas_call(
        body, name="ret_fwd", grid=(HEADS, nb),
        in_specs=[col(8), col(12), col(16), col(20), rope, rope, pl.BlockSpec((1, DK), lambda h, i: (0, h))],
        out_specs=[pl.BlockSpec((CH, DK), lambda h, i: (i, h)),
                   pl.BlockSpec((None, None, DK, DK), lambda h, i: (h, i, 0, 0))],
        out_shape=[SDS((L, D), MXU), SDS((HEADS, nb, DK, DK), MXU)],
        scratch_shapes=[pltpu.VMEM((DK, DK), F32)],
        compiler_params=_cp(32, ARB2),
    )(p, p, p, p, cos, sin, gain)


def _ret_bwd(p, cos, sin, gain, states, dx1, wout_e):
    L = p.shape[0]
    nb = L // CH
    rev = lambda i: nb - 1 - i

    def body(q_ref, k_ref, v_ref, bz_ref, cos_ref, sin_ref, g_ref, st_ref, dx1_ref, wo_ref,
             dq_ref, dk_ref, dv_ref, dbz_ref, yb_ref, gg_ref, gstate):
        h = pl.program_id(0)

        @pl.when(pl.program_id(1) == 0)
        def _():
            gstate[...] = jnp.zeros_like(gstate)
            gg_ref[...] = jnp.zeros_like(gg_ref)

        decay, xi, zeta, cd = _ret_decays(h)
        cos, sin, v = cos_ref[...], sin_ref[...], v_ref[...]
        s_prev_b = st_ref[...]
        qr, kr, scores, _, on, rstd = _ret_chunk_fwd(q_ref[...], k_ref[...], v, cos, sin, s_prev_b, decay, xi, zeta)
        gain_h = g_ref[...]
        bz = bz_ref[...]
        act, dact = _silu_and_grad(bz)
        out = on * gain_h
        yb_ref[...] = (out * act).astype(yb_ref.dtype)
        dyb = _mm_nt(dx1_ref[...], wo_ref[...])
        dbz_ref[...] = (dyb * out * dact).astype(dbz_ref.dtype)
        dout = dyb * act
        gg_ref[...] += jnp.sum(dout * on, axis=0, keepdims=True)
        don = dout * gain_h
        do = rstd * (don - jnp.mean(don, axis=-1, keepdims=True) - on * jnp.mean(don * on, axis=-1, keepdims=True))
        gnext = gstate[...]
        gnext_b = gnext.astype(MXU)
        dscores = _mm_nt(do, v) * decay
        dv_ref[...] = (_mm_tn(scores, do) + _mm(kr * zeta, gnext_b)).astype(dv_ref.dtype)
        dqr = _mm(dscores, kr) + _mm_nt(do, s_prev_b) * xi
        dkr = _mm_tn(dscores, qr) + _mm_nt(v, gnext_b) * zeta
        gstate[...] = gnext * cd + _mm_tn(qr * xi, do)
        dq_ref[...] = _unrot(dqr, cos, sin).astype(dq_ref.dtype)
        dk_ref[...] = (_unrot(dkr, cos, sin) * (DK ** -0.5)).astype(dk_ref.dtype)

    col = lambda base: pl.BlockSpec((CH, DK), lambda h, i: (rev(i), base + h))
    rope = pl.BlockSpec((CH, DK // 2), lambda h, i: (rev(i), 0))
    outc = pl.BlockSpec((CH, DK), lambda h, i: (rev(i), h))
    act_out = SDS((L, D), MXU)
    return pl.pallas_call(
        body, name="ret_bwd", grid=(HEADS, nb),
        in_specs=[col(8), col(12), col(16), col(20), rope, rope, pl.BlockSpec((1, DK), lambda h, i: (0, h)),
                  pl.BlockSpec((None, None, DK, DK), lambda h, i: (h, rev(i), 0, 0)),
                  pl.BlockSpec((CH, D), lambda h, i: (rev(i), 0)),
                  pl.BlockSpec((DK, D), lambda h, i: (4 + h, 0))],
        out_specs=[outc, outc, outc, outc, outc, pl.BlockSpec((1, DK), lambda h, i: (0, h))],
        out_shape=[act_out, act_out, act_out, act_out, act_out, SDS((1, D), F32)],
        scratch_shapes=[pltpu.VMEM((DK, DK), F32)],
        compiler_params=_cp(32, ARB2),
    )(p, p, p, p, cos, sin, gain, states, dx1, wout_e)


def _out_even(x, ya, yb, wout):
    L = x.shape[0]
    tm = min(512, L)

    def body(x_ref, ya_ref, yb_ref, w_ref, o_ref):
        cat = jnp.concatenate([ya_ref[...], yb_ref[...]], axis=1)
        o_ref[...] = x_ref[...] + jnp.dot(cat, w_ref[...], preferred_element_type=F32)

    row = pl.BlockSpec((tm, D), lambda i: (i, 0))
    return pl.pallas_call(
        body, name="out_even", grid=(L // tm,), in_specs=[row, row, row, _full((DI, D))],
        out_specs=row, out_shape=SDS((L, D), F32), compiler_params=_cp(32, ("arbitrary",)),
    )(x, ya, yb, wout)


def _sgu_core(pv, gain, ws_ref, bs_ref):
    pu, pvv, z = pv[:, :DI], pv[:, DI:2 * DI], pv[:, 2 * DI:]
    u = _gelu(pu)
    v = _gelu(pvv)
    mu = jnp.mean(v, axis=-1, keepdims=True)
    vc = v - mu
    rstd = lax.rsqrt(jnp.mean(vc * vc, axis=-1, keepdims=True) + EPS)
    vhat = vc * rstd
    vn = vhat * gain
    t = lax.broadcasted_iota(jnp.int32, (CH, CH), 0)
    s_ = lax.broadcasted_iota(jnp.int32, (CH, CH), 1)
    mask = t >= s_
    wm = [jnp.where(mask, ws_ref[g], 0.0).astype(MXU) for g in range(SG)]
    s = jnp.concatenate([_mm(wm[g], vn[:, g * SGD:(g + 1) * SGD]) + bs_ref[g] for g in range(SG)], axis=1)
    return pu, pvv, z, u, vhat, rstd, vn, mask, wm, s


def _sgu_fwd(p2, x1, gain, wsp, bsp, wout, fnorm, tgt):
    L = p2.shape[0]

    def body(p_ref, x1_ref, g_ref, ws_ref, bs_ref, wo_ref, fn_ref, t_ref, dx2_ref, gfn_ref, loss_ref):
        @pl.when(pl.program_id(0) == 0)
        def _():
            gfn_ref[...] = jnp.zeros_like(gfn_ref)
            loss_ref[...] = jnp.zeros_like(loss_ref)

        _, _, z, u, _, _, _, _, _, s = _sgu_core(p_ref[...], g_ref[...], ws_ref, bs_ref)
        act, _ = _silu_and_grad(z)
        x2 = x1_ref[...] + _mm(u * s * act, wo_ref[...])
        xhat, r = _rms(x2)
        fn = fn_ref[...]
        e = xhat * fn - t_ref[...]
        loss_ref[...] += 0.5 * jnp.sum(jnp.mean(e * e, axis=-1, keepdims=True), axis=0, keepdims=True)
        do = e * (1.0 / D)
        gfn_ref[...] += jnp.sum(do * xhat, axis=0, keepdims=True)
        dxhat = do * fn
        dx2_ref[...] = r * (dxhat - xhat * jnp.mean(dxhat * xhat, axis=-1, keepdims=True))

    row = pl.BlockSpec((CH, D), lambda i: (i, 0))
    return pl.pallas_call(
        body, name="sgu_fwd", grid=(L // CH,),
        in_specs=[pl.BlockSpec((CH, NIN), lambda i: (i, 0)), row, _full((1, DI)), _full((SG, CH, CH)),
                  _full((SG, CH, 1)), _full((DI, D)), _full((1, D)), row],
        out_specs=[row, _full((1, D)), _full((1, 1))],
        out_shape=[SDS((L, D), F32), SDS((1, D), F32), SDS((1, 1), F32)],
        compiler_params=_cp(48, ("arbitrary",)),
    )(p2, x1, gain, wsp, bsp, wout, fnorm, tgt)


def _sgu_bwd(p2, dx2, gain, wsp, bsp, wout):
    L = p2.shape[0]

    def body(p_ref, dx2_ref, g_ref, ws_ref, bs_ref, wo_ref, dp_ref, y_ref, gg_ref, gws_ref, gbs_ref):
        @pl.when(pl.program_id(0) == 0)
        def _():
            gg_ref[...] = jnp.zeros_like(gg_ref)
            gws_ref[...] = jnp.zeros_like(gws_ref)
            gbs_ref[...] = jnp.zeros_like(gbs_ref)

        gain = g_ref[...]
        pu, pvv, z, u, vhat, rstd, vn, mask, wm, s = _sgu_core(p_ref[...], gain, ws_ref, bs_ref)
        act, dact = _silu_and_grad(z)
        y_ref[...] = (u * s * act).astype(y_ref.dtype)
        dy = _mm_nt(dx2_ref[...], wo_ref[...])
        du = dy * s * act
        ds = dy * u * act
        dz = dy * u * s * dact
        dvn = []
        for g in range(SG):
            ds_g = ds[:, g * SGD:(g + 1) * SGD]
            vn_g = vn[:, g * SGD:(g + 1) * SGD]
            gbs_ref[g] += jnp.sum(ds_g, axis=1, keepdims=True)
            gws_ref[g] += jnp.where(mask, _mm_nt(ds_g, vn_g), 0.0)
            dvn.append(_mm_tn(wm[g], ds_g))
        dvn = jnp.concatenate(dvn, axis=1)
        gg_ref[...] += jnp.sum(dvn * vhat, axis=0, keepdims=True)
        dvhat = dvn * gain
        dv = rstd * (dvhat - jnp.mean(dvhat, axis=-1, keepdims=True)
                     - vhat * jnp.mean(dvhat * vhat, axis=-1, keepdims=True))
        dp_ref[...] = jnp.concatenate([du * _gelu_grad(pu), dv * _gelu_grad(pvv), dz], axis=1).astype(dp_ref.dtype)

    return pl.pallas_call(
        body, name="sgu_bwd", grid=(L // CH,),
        in_specs=[pl.BlockSpec((CH, NIN), lambda i: (i, 0)), pl.BlockSpec((CH, D), lambda i: (i, 0)),
                  _full((1, DI)), _full((SG, CH, CH)), _full((SG, CH, 1)), _full((DI, D))],
        out_specs=[pl.BlockSpec((CH, NIN), lambda i: (i, 0)), pl.BlockSpec((CH, DI), lambda i: (i, 0)),
                   _full((1, DI)), _full((SG, CH, CH)), _full((SG, CH, 1))],
        out_shape=[SDS((L, NIN), MXU), SDS((L, DI), MXU), SDS((1, DI), F32), SDS((SG, CH, CH), F32),
                   SDS((SG, CH, 1), F32)],
        compiler_params=_cp(48, ("arbitrary",)),
    )(p2, dx2, gain, wsp, bsp, wout)


def _adamw(w, m, v, gparts, name):
    R, C = w.shape
    tr = R
    for cand in (256, 128, 64, 32, 16, 8):
        if R % cand == 0 and R > cand:
            tr = cand
            break

    def body(w_ref, m_ref, v_ref, gp_ref, g_ref, d_ref, mo_ref, vo_ref):
        g = gp_ref[0].astype(F32)
        for s in range(1, NDEV):
            g = g + gp_ref[s].astype(F32)
        w_ = w_ref[...]
        mn = ADAM_B1 * m_ref[...] + (1.0 - ADAM_B1) * g
        vn = ADAM_B2 * v_ref[...] + (1.0 - ADAM_B2) * (g * g)
        mhat = mn / BC1
        vhat = vn / BC2
        g_ref[...] = g
        d_ref[...] = -ADAM_LR * (mhat / (jnp.sqrt(vhat) + ADAM_EPS) + ADAM_WD * w_)
        mo_ref[...] = mn
        vo_ref[...] = vn

    blk = pl.BlockSpec((tr, C), lambda i: (i, 0))
    out = SDS((R, C), F32)
    return pl.pallas_call(
        body, name=name, grid=(R // tr,),
        in_specs=[blk, blk, blk, pl.BlockSpec((NDEV, tr, C), lambda i: (0, i, 0))],
        out_specs=[blk, blk, blk, blk], out_shape=[out, out, out, out],
        compiler_params=_cp(40, ("arbitrary",)),
    )(w, m, v, gparts)


def _me_and_peers():
    x, y, c = lax.axis_index("x"), lax.axis_index("y"), lax.axis_index("c")
    me = 4 * x + 2 * y + c
    peers = []
    for r in range(1, NDEV):
        px, py, pc = x ^ ((r >> 2) & 1), y ^ ((r >> 1) & 1), c ^ (r & 1)
        peers.append(((px, py, pc), 4 * px + 2 * py + pc))
    return me, peers


def _exchange(arrays, scatter, name):
    n = len(arrays)
    out_shape = [SDS((NDEV,) + (a.shape[1:] if scatter else a.shape), a.dtype) for a in arrays]

    def body(*refs):
        ins, outs = refs[:n], refs[n:2 * n]
        send_sems, recv_sems, loc_sems = refs[2 * n:]
        me, peers = _me_and_peers()
        local = []
        for k in range(n):
            src = ins[k].at[me] if scatter else ins[k]
            cp = pltpu.make_async_copy(src, outs[k].at[me], loc_sems.at[k])
            cp.start()
            local.append(cp)
        sends = []
        for r, (dev, lin) in enumerate(peers):
            for k in range(n):
                src = ins[k].at[lin] if scatter else ins[k]
                cp = pltpu.make_async_remote_copy(
                    src_ref=src, dst_ref=outs[k].at[me], send_sem=send_sems.at[r, k], recv_sem=recv_sems.at[r, k],
                    device_id=dev, device_id_type=pl.DeviceIdType.MESH)
                cp.start()
                sends.append(cp)
        for r, (dev, lin) in enumerate(peers):
            for k in range(n):
                src = ins[k].at[me] if scatter else ins[k]
                pltpu.make_async_remote_copy(
                    src_ref=src, dst_ref=outs[k].at[lin], send_sem=send_sems.at[r, k], recv_sem=recv_sems.at[r, k],
                    device_id=dev, device_id_type=pl.DeviceIdType.MESH).wait_recv()
        for cp in sends:
            cp.wait_send()
        for cp in local:
            cp.wait()

    hbm = pl.BlockSpec(memory_space=pltpu.HBM)
    return pl.pallas_call(
        body, name=name, in_specs=[hbm] * n, out_specs=[hbm] * n, out_shape=out_shape,
        scratch_shapes=[pltpu.SemaphoreType.DMA((NDEV - 1, n)), pltpu.SemaphoreType.DMA((NDEV - 1, n)),
                        pltpu.SemaphoreType.DMA((n,))],
    )(*arrays)


def _local_step(x, tgt, norm_even, win_e, lam_re, lam_im, log_dt, b_re, b_im, c_re, c_im, s5_d, wglu, bglu,
                ret_gain, wout_e, norm_odd, win_o, sgu_gain, wsp, bsp, wout_o, fnorm):
    L = x.shape[0]
    lr3, li3 = lam_re.reshape(G, 1, P), lam_im.reshape(G, 1, P)
    dt3 = log_dt.reshape(G, 1, 1)
    br3, bi3 = jnp.swapaxes(b_re, 1, 2), jnp.swapaxes(b_im, 1, 2)
    abr3, abi3, bbr3, bbi3 = _s5_disc(lr3, li3, dt3, br3, bi3)
    bb = jnp.concatenate([_embed(bbr3), _embed(bbi3)], axis=2).astype(MXU)
    cm = jnp.concatenate([_embed(jnp.swapaxes(c_re, 1, 2)), -_embed(jnp.swapaxes(c_im, 1, 2))], axis=1).astype(MXU)
    abr, abi = abr3.reshape(1, NSTATE), abi3.reshape(1, NSTATE)
    pwr, pwi, _, _ = _s5_tables(abr, abi, NT, "s5_tables_step")
    par, pai, pbr, pbi = _s5_tables(pwr[NT - 1:NT], pwi[NT - 1:NT], CH, "s5_tables_chunk")
    inv = (ROPE_BASE ** (-jnp.arange(DK // 2, dtype=F32) / (DK // 2))).reshape(1, DK // 2)
    cos, sin = _rope_tables(L, inv)
    bsp3 = bsp.reshape(SG, CH, 1)

    p = _in_proj(x, norm_even, win_e, "in_even")
    ypre, s5_states = _s5_scan_fwd(p, bb, cm, abr, abi, par, pai, s5_d)
    ya = _s5_gate_fwd(ypre, p, wglu, bglu)
    yb, ret_states = _ret_fwd(p, cos, sin, ret_gain)
    x1 = _out_even(x, ya, yb, wout_e)
    p2 = _in_proj(x1, norm_odd, win_o, "in_odd")
    dx2, g_fnorm, loss = _sgu_fwd(p2, x1, sgu_gain, wsp, bsp3, wout_o, fnorm, tgt)

    dp2, y_o, g_sgu_gain, g_wsp, g_bsp = _sgu_bwd(p2, dx2, sgu_gain, wsp, bsp3, wout_o)
    g_wout_o = _wgrad_rows(y_o, dx2.astype(MXU), "wgrad_out_odd")
    dx1, g_norm_odd = _in_proj_bwd_x(dp2, x1, norm_odd, win_o, dx2, "in_odd_bwd")
    g_win_o = _wgrad_cols(x1, norm_odd, dp2, "wgrad_in_odd")

    dypre, daz, yg, dt, ya2, g_bglu = _s5_gate_bwd(ypre, p, dx1, wout_e, wglu, bglu)
    du, g_d, g_cm, g_bb, g_ar, g_ai = _s5_scan_bwd(p, dypre, s5_states, bb, cm, abr, abi, par, pai, pbr, pbi, s5_d)
    dq, dk, dv, dbz, yb2, g_ret_gain = _ret_bwd(p, cos, sin, ret_gain, ret_states, dx1, wout_e)
    dx1b = dx1.astype(MXU)
    g_wout_e = _wgrad_rows(jnp.concatenate([ya2, yb2], axis=1), dx1b, "wgrad_out_even")
    g_wglu = _wgrad_rows(yg, dt, "wgrad_glu")
    dp = jnp.concatenate([du.astype(MXU), daz, dq, dk, dv, dbz], axis=1)
    dx, g_norm_even = _in_proj_bwd_x(dp, x, norm_even, win_e, dx1, "in_even_bwd")
    g_win_e = _wgrad_cols(x, norm_even, dp, "wgrad_in_even")

    dbbr3 = _diag_blocks(g_bb[:, :, :SW], HG, P)
    dbbi3 = _diag_blocks(g_bb[:, :, SW:], HG, P)
    g_c_re = jnp.swapaxes(_diag_blocks(g_cm[:, :SW, :], P, HG), 1, 2)
    g_c_im = -jnp.swapaxes(_diag_blocks(g_cm[:, SW:, :], P, HG), 1, 2)
    g_lr3, g_li3, g_dt3, g_br3, g_bi3 = _s5_disc_bwd(
        lr3, li3, dt3, br3, bi3, g_ar.reshape(G, 1, P), g_ai.reshape(G, 1, P), dbbr3, dbbi3)

    big = dict(w_in_even=g_win_e, s5_w_glu=g_wglu, w_out_even=g_wout_e, w_in_odd=g_win_o, w_out_odd=g_wout_o)
    small = dict(
        norm_even=g_norm_even, s5_lam_re=g_lr3.reshape(G, P), s5_lam_im=g_li3.reshape(G, P),
        s5_log_dt=g_dt3.reshape(1, G), s5_b_re=jnp.swapaxes(g_br3, 1, 2), s5_b_im=jnp.swapaxes(g_bi3, 1, 2),
        s5_c_re=g_c_re, s5_c_im=g_c_im, s5_d=g_d, s5_b_glu=g_bglu, ret_gn_gain=g_ret_gain,
        norm_odd=g_norm_odd, sgu_norm_gain=g_sgu_gain, sgu_w_spatial=g_wsp, sgu_b_spatial=g_bsp.reshape(SG, CH),
        final_norm=g_fnorm)
    return loss, dx, big, small


WEIGHTS = ['norm_even', 'w_in_even', 's5_lam_re', 's5_lam_im', 's5_log_dt', 's5_b_re', 's5_b_im', 's5_c_re',
           's5_c_im', 's5_d', 's5_w_glu', 's5_b_glu', 'ret_gn_gain', 'w_out_even', 'norm_odd', 'w_in_odd',
           'sgu_norm_gain', 'sgu_w_spatial', 'sgu_b_spatial', 'w_out_odd', 'final_norm']
BIG = ['w_in_even', 's5_w_glu', 'w_out_even', 'w_in_odd', 'w_out_odd']
SHARDED_SMALL = {'norm_odd': D // NDEV, 'sgu_norm_gain': DI // NDEV}
SMALL = [n for n in WEIGHTS if n not in BIG]
LANES = 128


def kernel(x, norm_even, w_in_even, s5_lam_re, s5_lam_im, s5_log_dt, s5_b_re, s5_b_im, s5_c_re, s5_c_im, s5_d, s5_w_glu, s5_b_glu, ret_gn_gain, w_out_even, norm_odd, w_in_odd, sgu_norm_gain, sgu_w_spatial, sgu_b_spatial, w_out_odd, final_norm, loss_target, m_norm_even, m_w_in_even, m_s5_lam_re, m_s5_lam_im, m_s5_log_dt, m_s5_b_re, m_s5_b_im, m_s5_c_re, m_s5_c_im, m_s5_d, m_s5_w_glu, m_s5_b_glu, m_ret_gn_gain, m_w_out_even, m_norm_odd, m_w_in_odd, m_sgu_norm_gain, m_sgu_w_spatial, m_sgu_b_spatial, m_w_out_odd, m_final_norm, v_norm_even, v_w_in_even, v_s5_lam_re, v_s5_lam_im, v_s5_log_dt, v_s5_b_re, v_s5_b_im, v_s5_c_re, v_s5_c_im, v_s5_d, v_s5_w_glu, v_s5_b_glu, v_ret_gn_gain, v_w_out_even, v_norm_odd, v_w_in_odd, v_sgu_norm_gain, v_sgu_w_spatial, v_sgu_b_spatial, v_w_out_odd, v_final_norm):
    args = dict(locals())
    w = {n: args[n] for n in WEIGHTS}
    m = {n: args["m_" + n] for n in WEIGHTS}
    v = {n: args["v_" + n] for n in WEIGHTS}
    me = 4 * lax.axis_index("x") + 2 * lax.axis_index("y") + lax.axis_index("c")

    shards = [w['w_in_even'][0].astype(MXU), w['s5_w_glu'][0].astype(MXU), w['w_out_even'][0].astype(MXU),
              w['w_in_odd'][0].astype(MXU), w['w_out_odd'][0].astype(MXU), w['norm_odd'], w['sgu_norm_gain']]
    win_e, wglu, wout_e, win_o, wout_o, nodd, sgug = _exchange(shards, False, "gather_weights")
    wglu = wglu.reshape(D, D)
    wout_e = wout_e.reshape(DI, D)
    wout_o = wout_o.reshape(DI, D)
    nodd = nodd.reshape(1, D)
    sgug = sgug.reshape(1, DI)

    loss, dx, big, small = _local_step(
        x[0], loss_target[0], w['norm_even'], win_e, w['s5_lam_re'][0], w['s5_lam_im'][0], w['s5_log_dt'][0],
        w['s5_b_re'][0], w['s5_b_im'][0], w['s5_c_re'][0], w['s5_c_im'][0], w['s5_d'], wglu, w['s5_b_glu'],
        w['ret_gn_gain'], wout_e, nodd, win_o, sgug, w['sgu_w_spatial'][0], w['sgu_b_spatial'][0], wout_o,
        w['final_norm'].reshape(1, D))

    flat = jnp.concatenate([small[n].reshape(-1) for n in SMALL])
    npad = (-flat.shape[0]) % (8 * LANES)
    flat = jnp.pad(flat, (0, npad)).reshape(1, -1)
    bigs = _exchange([big[n] for n in BIG], True, "scatter_grads")
    (gsmall,) = _exchange([flat], False, "gather_small_grads")
    gsmall = gsmall.reshape(NDEV, -1)

    out_g, out_d, out_m, out_v = {}, {}, {}, {}
    for n, parts in zip(BIG, bigs):
        shp = w[n].shape
        r, c = shp[1], shp[2]
        g_, d_, m_, v_ = _adamw(w[n].reshape(r, c), m[n].reshape(r, c), v[n].reshape(r, c), parts, "adamw_" + n)
        out_g[n], out_d[n], out_m[n], out_v[n] = (t.reshape(shp) for t in (g_, d_, m_, v_))
    pieces, off = [], 0
    for n in SMALL:
        size = small[n].size
        seg = gsmall[:, off:off + size]
        if n in SHARDED_SMALL:
            seg = lax.dynamic_slice_in_dim(seg, me * SHARDED_SMALL[n], SHARDED_SMALL[n], axis=1)
        pieces.append(seg)
        off += size
    gp = jnp.concatenate(pieces, axis=1)
    nsw = gp.shape[1]
    wpad = (-nsw) % (8 * LANES)

    def pack(d):
        t = jnp.concatenate([d[n].reshape(-1) for n in SMALL])
        return jnp.pad(t, (0, wpad)).reshape(-1, LANES)

    gp = jnp.pad(gp, ((0, 0), (0, wpad))).reshape(NDEV, -1, LANES)
    sg_, sd_, sm_, sv_ = _adamw(pack(w), pack(m), pack(v), gp, "adamw_small")
    off = 0
    for n in SMALL:
        size = w[n].size
        for dst, src in ((out_g, sg_), (out_d, sd_), (out_m, sm_), (out_v, sv_)):
            dst[n] = src.reshape(-1)[off:off + size].reshape(w[n].shape)
        off += size

    loss_total = lax.psum(loss[0, 0], AXES)
    return (loss_total, dx[None], *[out_g[n] for n in WEIGHTS], *[out_d[n] for n in WEIGHTS],
            *[out_m[n] for n in WEIGHTS], *[out_v[n] for n in WEIGHTS])
```

```python
import math

import jax
import jax.numpy as jnp
from jax import lax
from jax.experimental import pallas as pl
from jax.experimental.pallas import tpu as pltpu

F32 = jnp.float32
MXU = jnp.bfloat16
AXES = ("x", "y", "c")
NDEV = 8
D = 1024
NIN = 6144
WIN_BLK = NIN // NDEV
DI = 2048
G, P, HG = 64, 64, 16
GB = 8
NJ = G // GB
SW = GB * P
UW = GB * HG
NSTATE = G * P
HEADS, DK = 4, 256
CH = 128
SG, SGD = 4, 512
EPS = 1e-6
ROPE_BASE = 10000.0
VMEM_CAP_V7X = 64 * 1024 * 1024
LOG_G = [math.log1p(-2.0 ** (-5.0 - h)) for h in range(HEADS)]
GELU_C = math.sqrt(2.0 / math.pi)

ADAM_LR, ADAM_B1, ADAM_B2, ADAM_EPS, ADAM_WD, ADAM_STEP = 0.001, 0.9, 0.999, 1e-08, 0.01, 10
BC1 = 1.0 - ADAM_B1 ** ADAM_STEP
BC2 = 1.0 - ADAM_B2 ** ADAM_STEP

SDS = jax.ShapeDtypeStruct
ARB2 = ("arbitrary", "arbitrary")


def _cp(vmem_mib, sem=None):
    kw = dict(vmem_limit_bytes=min(vmem_mib * 1024 * 1024, VMEM_CAP_V7X - 4 * 1024 * 1024))
    if sem is not None:
        kw["dimension_semantics"] = sem
    return pltpu.CompilerParams(**kw)


def _mm(a, b):
    return jnp.dot(a.astype(MXU), b.astype(MXU), preferred_element_type=F32)


def _mm_nt(a, b):
    return lax.dot_general(a.astype(MXU), b.astype(MXU), (((1,), (1,)), ((), ())), preferred_element_type=F32)


def _mm_tn(a, b):
    return lax.dot_general(a.astype(MXU), b.astype(MXU), (((0,), (0,)), ((), ())), preferred_element_type=F32)


def _gelu(x):
    th = jnp.tanh(GELU_C * (x + 0.044715 * x * x * x))
    return 0.5 * x * (1.0 + th)


def _gelu_grad(x):
    th = jnp.tanh(GELU_C * (x + 0.044715 * x * x * x))
    return 0.5 * (1.0 + th) + 0.5 * x * (1.0 - th * th) * GELU_C * (1.0 + 3.0 * 0.044715 * x * x)


def _silu_and_grad(x):
    s = jax.nn.sigmoid(x)
    return x * s, s * (1.0 + x * (1.0 - s))


def _full(shape):
    nd = len(shape)
    return pl.BlockSpec(shape, lambda *_: (0,) * nd)


def _rms(xf):
    r = lax.rsqrt(jnp.mean(xf * xf, axis=-1, keepdims=True) + EPS)
    return xf * r, r


def _in_proj(x, gain, wst, name):
    L = x.shape[0]
    tm = min(512, L)

    def body(x_ref, g_ref, w_ref, o_ref, h_scr):
        @pl.when(pl.program_id(1) == 0)
        def _():
            xhat, _ = _rms(x_ref[...])
            h_scr[...] = (xhat * g_ref[...]).astype(MXU)

        o_ref[...] = jnp.dot(h_scr[...], w_ref[...], preferred_element_type=F32)

    return pl.pallas_call(
        body, name=name, grid=(L // tm, NDEV),
        in_specs=[pl.BlockSpec((tm, D), lambda i, n: (i, 0)), _full((1, D)),
                  pl.BlockSpec((None, D, WIN_BLK), lambda i, n: (n, 0, 0))],
        out_specs=pl.BlockSpec((tm, WIN_BLK), lambda i, n: (i, n)),
        out_shape=SDS((L, NIN), F32),
        scratch_shapes=[pltpu.VMEM((tm, D), MXU)],
        compiler_params=_cp(32, ARB2),
    )(x, gain, wst)


def _in_proj_bwd_x(dp, x, gain, wst, dres, name):
    L = x.shape[0]
    tm = min(512, L)

    def body(dp_ref, x_ref, g_ref, w_ref, dres_ref, dx_ref, gg_ref, acc):
        i, n = pl.program_id(0), pl.program_id(1)

        @pl.when(n == 0)
        def _():
            acc[...] = jnp.zeros_like(acc)

        @pl.when((i == 0) & (n == 0))
        def _():
            gg_ref[...] = jnp.zeros_like(gg_ref)

        acc[...] += _mm_nt(dp_ref[...], w_ref[...])

        @pl.when(n == NDEV - 1)
        def _():
            xhat, r = _rms(x_ref[...])
            dh = acc[...]
            dxhat = dh * g_ref[...]
            dx_ref[...] = dres_ref[...] + r * (dxhat - xhat * jnp.mean(dxhat * xhat, axis=-1, keepdims=True))
            gg_ref[...] += jnp.sum(dh * xhat, axis=0, keepdims=True)

    return pl.pallas_call(
        body, name=name, grid=(L // tm, NDEV),
        in_specs=[pl.BlockSpec((tm, WIN_BLK), lambda i, n: (i, n)),
                  pl.BlockSpec((tm, D), lambda i, n: (i, 0)), _full((1, D)),
                  pl.BlockSpec((None, D, WIN_BLK), lambda i, n: (n, 0, 0)),
                  pl.BlockSpec((tm, D), lambda i, n: (i, 0))],
        out_specs=[pl.BlockSpec((tm, D), lambda i, n: (i, 0)), _full((1, D))],
        out_shape=[SDS((L, D), F32), SDS((1, D), F32)],
        scratch_shapes=[pltpu.VMEM((tm, D), F32)],
        compiler_params=_cp(40, ARB2),
    )(dp, x, gain, wst, dres)


def _wgrad_cols(x, gain, dp, name):
    L = x.shape[0]
    tk = min(512, L)
    nk = L // tk

    def body(x_ref, g_ref, dp_ref, o_ref, acc):
        k = pl.program_id(1)

        @pl.when(k == 0)
        def _():
            acc[...] = jnp.zeros_like(acc)

        xhat, _ = _rms(x_ref[...])
        acc[...] += _mm_tn(xhat * g_ref[...], dp_ref[...])

        @pl.when(k == nk - 1)
        def _():
            o_ref[...] = acc[...].astype(o_ref.dtype)

    return pl.pallas_call(
        body, name=name, grid=(NDEV, nk),
        in_specs=[pl.BlockSpec((tk, D), lambda n, k: (k, 0)), _full((1, D)),
                  pl.BlockSpec((tk, WIN_BLK), lambda n, k: (k, n))],
        out_specs=pl.BlockSpec((None, D, WIN_BLK), lambda n, k: (n, 0, 0)),
        out_shape=SDS((NDEV, D, WIN_BLK), MXU),
        scratch_shapes=[pltpu.VMEM((D, WIN_BLK), F32)],
        compiler_params=_cp(32, ARB2),
    )(x, gain, dp)


def _wgrad_rows(a, b, name):
    L, M = a.shape
    N = b.shape[1]
    tk = min(512, L)
    nk = L // tk
    mb = M // NDEV

    def body(a_ref, b_ref, o_ref, acc):
        k = pl.program_id(1)

        @pl.when(k == 0)
        def _():
            acc[...] = jnp.zeros_like(acc)

        acc[...] += _mm_tn(a_ref[...], b_ref[...])

        @pl.when(k == nk - 1)
        def _():
            o_ref[...] = acc[...].astype(o_ref.dtype)

    return pl.pallas_call(
        body, name=name, grid=(NDEV, nk),
        in_specs=[pl.BlockSpec((tk, mb), lambda m, k: (k, m)), pl.BlockSpec((tk, N), lambda m, k: (k, 0))],
        out_specs=pl.BlockSpec((None, mb, N), lambda m, k: (m, 0, 0)),
        out_shape=SDS((NDEV, mb, N), MXU),
        scratch_shapes=[pltpu.VMEM((mb, N), F32)],
        compiler_params=_cp(32, ARB2),
    )(a, b)


def _s5_disc_fn(lr_raw, li, logdt, br, bi):
    lr = jnp.minimum(lr_raw, -1e-4)
    dt = jnp.exp(logdt)
    mag = jnp.exp(lr * dt)
    abr = mag * jnp.cos(li * dt)
    abi = mag * jnp.sin(li * dt)
    den = lr * lr + li * li
    nre = abr - 1.0
    nim = abi
    zr = (nre * lr + nim * li) / den
    zi = (nim * lr - nre * li) / den
    return abr, abi, zr * br - zi * bi, zr * bi + zi * br


def _s5_disc(lr, li, logdt, br, bi):
    def body(lr_ref, li_ref, dt_ref, br_ref, bi_ref, abr_ref, abi_ref, bbr_ref, bbi_ref):
        abr, abi, bbr, bbi = _s5_disc_fn(lr_ref[...], li_ref[...], dt_ref[...], br_ref[...], bi_ref[...])
        abr_ref[...] = abr
        abi_ref[...] = abi
        bbr_ref[...] = bbr
        bbi_ref[...] = bbi

    s1, s3 = SDS((G, 1, P), F32), SDS((G, HG, P), F32)
    return pl.pallas_call(body, name="s5_disc", out_shape=[s1, s1, s3, s3])(lr, li, logdt, br, bi)


def _s5_disc_bwd(lr, li, logdt, br, bi, dabr, dabi, dbbr, dbbi):
    def body(lr_ref, li_ref, dt_ref, br_ref, bi_ref, c0, c1, c2, c3, o0, o1, o2, o3, o4):
        _, vjp = jax.vjp(_s5_disc_fn, lr_ref[...], li_ref[...], dt_ref[...], br_ref[...], bi_ref[...])
        g = vjp((c0[...], c1[...], c2[...], c3[...]))
        for o, v in zip((o0, o1, o2, o3, o4), g):
            o[...] = v

    s1, s3 = SDS((G, 1, P), F32), SDS((G, HG, P), F32)
    return pl.pallas_call(body, name="s5_disc_bwd", out_shape=[s1, s1, SDS((G, 1, 1), F32), s3, s3])(
        lr, li, logdt, br, bi, dabr, dabi, dbbr, dbbi)


def _s5_tables(abr, abi, rows, name):
    def body(ar_ref, ai_ref, pfr, pfi, pbr, pbi):
        pfr[0:1, :] = ar_ref[...]
        pfi[0:1, :] = ai_ref[...]
        pbr[rows - 1:rows, :] = ar_ref[...]
        pbi[rows - 1:rows, :] = ai_ref[...]
        n = 1
        while n < rows:
            er, ei = pfr[n - 1:n, :], pfi[n - 1:n, :]
            xr, xi = pfr[0:n, :], pfi[0:n, :]
            pfr[n:2 * n, :] = er * xr - ei * xi
            pfi[n:2 * n, :] = er * xi + ei * xr
            yr, yi = pbr[rows - n:rows, :], pbi[rows - n:rows, :]
            pbr[rows - 2 * n:rows - n, :] = er * yr - ei * yi
            pbi[rows - 2 * n:rows - n, :] = er * yi + ei * yr
            n *= 2

    s = SDS((rows, NSTATE), F32)
    return pl.pallas_call(body, name=name, out_shape=[s, s, s, s], compiler_params=_cp(40))(abr, abi)


def _cscan(br, bi, pr_ref, pi_ref, reverse):
    T = br.shape[0]
    sign = -1.0 if reverse else 1.0
    row = lax.broadcasted_iota(jnp.int32, br.shape, 0)
    k = 1
    while k < T:
        akr = pr_ref[k - 1:k, :]
        aki = sign * pi_ref[k - 1:k, :]

        def shift(v):
            if k % 8 == 0:
                z = jnp.zeros((k, v.shape[1]), v.dtype)
                return jnp.concatenate([v[k:], z], 0) if reverse else jnp.concatenate([z, v[:T - k]], 0)
            if reverse:
                return jnp.where(row < T - k, pltpu.roll(v, T - k, 0), 0.0)
            return jnp.where(row >= k, pltpu.roll(v, k, 0), 0.0)

        sr, si = shift(br), shift(bi)
        br, bi = br + akr * sr - aki * si, bi + akr * si + aki * sr
        k *= 2
    return br, bi


def _embed(t):
    a, b = t.shape[1], t.shape[2]
    return jnp.einsum("jgab,gh->jgahb", t.reshape(NJ, GB, a, b), jnp.eye(GB, dtype=t.dtype)).reshape(NJ, GB * a, GB * b)


def _diag_blocks(t, a, b):
    return jnp.einsum("jgahb,gh->jgab", t.reshape(NJ, GB, a, GB, b), jnp.eye(GB, dtype=t.dtype)).reshape(G, a, b)


NT = 8


def _chunks(L):
    ncb = min(CH, L // NT)
    return ncb, NT * ncb


def _cmul_add(ar, ai, xr, xi, br, bi):
    return ar * xr - ai * xi + br, ar * xi + ai * xr + bi


def _s5_states(u_ref, bb_ref, ar, ai, par_ref, pai_ref, cr, ci, ncb, bu_scr):
    er = ei = None
    for t in range(NT):
        bu = _mm(u_ref[pl.ds(t, ncb, stride=NT), :], bb_ref[...])
        bu_scr[t] = bu
        if t == 0:
            er, ei = bu[:, :SW], bu[:, SW:]
        else:
            er, ei = _cmul_add(ar, ai, er, ei, bu[:, :SW], bu[:, SW:])
    xr, xi = _cscan(er, ei, par_ref, pai_ref, False)
    fr, fi = _cmul_add(par_ref[0:ncb, :], pai_ref[0:ncb, :], cr, ci, xr, xi)
    row = lax.broadcasted_iota(jnp.int32, fr.shape, 0)
    cinr = jnp.where(row >= 1, pltpu.roll(fr, 1, 0), cr)
    cini = jnp.where(row >= 1, pltpu.roll(fi, 1, 0), ci)
    return cinr, cini, jnp.concatenate([fr[ncb - 1:ncb, :], fi[ncb - 1:ncb, :]], axis=1)


def _s5_scan_fwd(p, bb, cm, abr, abi, par, pai, dskip):
    L = p.shape[0]
    ncb, tb = _chunks(L)
    nb = L // tb

    def body(u_ref, bb_ref, cm_ref, ar_ref, ai_ref, par_ref, pai_ref, d_ref, ypre_ref, st_ref, carry, bu_scr):
        @pl.when(pl.program_id(1) == 0)
        def _():
            carry[...] = jnp.zeros_like(carry)

        c = carry[...]
        st_ref[...] = c
        ar, ai = ar_ref[...], ai_ref[...]
        sr, si, cnext = _s5_states(u_ref, bb_ref, ar, ai, par_ref, pai_ref, c[:, :SW], c[:, SW:], ncb, bu_scr)
        carry[...] = cnext
        for t in range(NT):
            bu = bu_scr[t]
            sr, si = _cmul_add(ar, ai, sr, si, bu[:, :SW], bu[:, SW:])
            rows = pl.ds(t, ncb, stride=NT)
            ypre_ref[rows, :] = _mm(jnp.concatenate([sr, si], axis=1), cm_ref[...]) + d_ref[...] * u_ref[rows, :]

    tab = pl.BlockSpec((CH, SW), lambda j, i: (0, j))
    vec = lambda w: pl.BlockSpec((1, w), lambda j, i: (0, j))
    return pl.pallas_call(
        body, name="s5_scan_fwd", grid=(NJ, nb),
        in_specs=[pl.BlockSpec((tb, UW), lambda j, i: (i, j)),
                  pl.BlockSpec((None, UW, 2 * SW), lambda j, i: (j, 0, 0)),
                  pl.BlockSpec((None, 2 * SW, UW), lambda j, i: (j, 0, 0)),
                  vec(SW), vec(SW), tab, tab, vec(UW)],
        out_specs=[pl.BlockSpec((tb, UW), lambda j, i: (i, j)),
                   pl.BlockSpec((None, None, 1, 2 * SW), lambda j, i: (j, i, 0, 0))],
        out_shape=[SDS((L, D), F32), SDS((NJ, nb, 1, 2 * SW), F32)],
        scratch_shapes=[pltpu.VMEM((1, 2 * SW), F32), pltpu.VMEM((NT, ncb, 2 * SW), F32)],
        compiler_params=_cp(40, ARB2),
    )(p, bb, cm, abr, abi, par, pai, dskip)


def _s5_gate_fwd(ypre, p, wglu, bglu):
    L = ypre.shape[0]
    tm = min(256, L)

    def body(y_ref, az_ref, wg_ref, bg_ref, ya_ref):
        yg = _gelu(y_ref[...])
        t = _mm(yg, wg_ref[...]) + bg_ref[...]
        act, _ = _silu_and_grad(az_ref[...])
        ya_ref[...] = (yg * jax.nn.sigmoid(t) * act).astype(ya_ref.dtype)

    return pl.pallas_call(
        body, name="s5_gate_fwd", grid=(L // tm,),
        in_specs=[pl.BlockSpec((tm, D), lambda i: (i, 0)), pl.BlockSpec((tm, D), lambda i: (i, 1)),
                  _full((D, D)), _full((1, D))],
        out_specs=pl.BlockSpec((tm, D), lambda i: (i, 0)),
        out_shape=SDS((L, D), MXU),
        compiler_params=_cp(32, ("arbitrary",)),
    )(ypre, p, wglu, bglu)


def _s5_gate_bwd(ypre, p, dx1, wout_e, wglu, bglu):
    L = ypre.shape[0]
    tm = min(256, L)

    def body(y_ref, az_ref, dx1_ref, wo_ref, wg_ref, bg_ref, dyp_ref, daz_ref, yg_ref, dt_ref, ya_ref, gbg_ref):
        @pl.when(pl.program_id(0) == 0)
        def _():
            gbg_ref[...] = jnp.zeros_like(gbg_ref)

        ypre_v = y_ref[...]
        yg = _gelu(ypre_v)
        sg = jax.nn.sigmoid(_mm(yg, wg_ref[...]) + bg_ref[...])
        act, dact = _silu_and_grad(az_ref[...])
        y2 = yg * sg
        dya = _mm_nt(dx1_ref[...], wo_ref[...])
        daz_ref[...] = (dya * y2 * dact).astype(daz_ref.dtype)
        dy2 = dya * act
        dt = dy2 * yg * sg * (1.0 - sg)
        dyg = dy2 * sg + _mm_nt(dt, wg_ref[...])
        dyp_ref[...] = dyg * _gelu_grad(ypre_v)
        yg_ref[...] = yg.astype(yg_ref.dtype)
        dt_ref[...] = dt.astype(dt_ref.dtype)
        ya_ref[...] = (y2 * act).astype(ya_ref.dtype)
        gbg_ref[...] += jnp.sum(dt, axis=0, keepdims=True)

    row = pl.BlockSpec((tm, D), lambda i: (i, 0))
    return pl.pallas_call(
        body, name="s5_gate_bwd", grid=(L // tm,),
        in_specs=[row, pl.BlockSpec((tm, D), lambda i: (i, 1)), row,
                  pl.BlockSpec((D, D), lambda i: (0, 0)), _full((D, D)), _full((1, D))],
        out_specs=[row, row, row, row, row, _full((1, D))],
        out_shape=[SDS((L, D), F32), SDS((L, D), MXU), SDS((L, D), MXU), SDS((L, D), MXU), SDS((L, D), MXU),
                   SDS((1, D), F32)],
        compiler_params=_cp(40, ("arbitrary",)),
    )(ypre, p, dx1, wout_e, wglu, bglu)


def _s5_scan_bwd(p, dypre, states, bb, cm, abr, abi, par, pai, pbr, pbi, dskip):
    L = p.shape[0]
    ncb, tb = _chunks(L)
    nb = L // tb
    rev = lambda i: nb - 1 - i

    def body(u_ref, dy_ref, st_ref, bb_ref, cm_ref, ar_ref, ai_ref, par_ref, pai_ref, pbr_ref, pbi_ref, d_ref,
             du_ref, gd_ref, gcm_ref, gbb_ref, gar_ref, gai_ref, lcarry, bu_scr, s_scr, gs_scr):
        @pl.when(pl.program_id(1) == 0)
        def _():
            lcarry[...] = jnp.zeros_like(lcarry)
            gd_ref[...] = jnp.zeros_like(gd_ref)
            gcm_ref[...] = jnp.zeros_like(gcm_ref)
            gbb_ref[...] = jnp.zeros_like(gbb_ref)
            gar_ref[...] = jnp.zeros_like(gar_ref)
            gai_ref[...] = jnp.zeros_like(gai_ref)

        ar, ai = ar_ref[...], ai_ref[...]
        c = st_ref[...]
        sr, si, _ = _s5_states(u_ref, bb_ref, ar, ai, par_ref, pai_ref, c[:, :SW], c[:, SW:], ncb, bu_scr)
        s_scr[0] = jnp.concatenate([sr, si], axis=1)
        for t in range(NT):
            bu = bu_scr[t]
            sr, si = _cmul_add(ar, ai, sr, si, bu[:, :SW], bu[:, SW:])
            s_scr[t + 1] = jnp.concatenate([sr, si], axis=1)
        fr = fi = None
        for t in reversed(range(NT)):
            gs = _mm_nt(dy_ref[pl.ds(t, ncb, stride=NT), :], cm_ref[...])
            gs_scr[t] = gs
            if t == NT - 1:
                fr, fi = gs[:, :SW], gs[:, SW:]
            else:
                fr, fi = _cmul_add(ar, -ai, fr, fi, gs[:, :SW], gs[:, SW:])
        xr, xi = _cscan(fr, fi, par_ref, pai_ref, True)
        lc = lcarry[...]
        lcr, lci = lc[:, :SW], lc[:, SW:]
        hr, hi = _cmul_add(pbr_ref[CH - ncb:CH, :], -pbi_ref[CH - ncb:CH, :], lcr, lci, xr, xi)
        lcarry[...] = jnp.concatenate([hr[0:1, :], hi[0:1, :]], axis=1)
        row = lax.broadcasted_iota(jnp.int32, hr.shape, 0)
        lr_ = jnp.where(row < ncb - 1, pltpu.roll(hr, ncb - 1, 0), lcr)
        li_ = jnp.where(row < ncb - 1, pltpu.roll(hi, ncb - 1, 0), lci)
        gar = jnp.zeros((1, SW), F32)
        gai = jnp.zeros((1, SW), F32)
        for t in reversed(range(NT)):
            gs = gs_scr[t]
            lr_, li_ = _cmul_add(ar, -ai, lr_, li_, gs[:, :SW], gs[:, SW:])
            rows = pl.ds(t, ncb, stride=NT)
            u_t, dy_t = u_ref[rows, :], dy_ref[rows, :]
            lam = jnp.concatenate([lr_, li_], axis=1)
            gbb_ref[...] += _mm_tn(u_t, lam)
            du_ref[rows, :] = _mm_nt(lam, bb_ref[...]) + dy_t * d_ref[...]
            gd_ref[...] += jnp.sum(dy_t * u_t, axis=0, keepdims=True)
            gcm_ref[...] += _mm_tn(s_scr[t + 1], dy_t)
            sp = s_scr[t]
            spr, spi = sp[:, :SW], sp[:, SW:]
            gar += jnp.sum(lr_ * spr + li_ * spi, axis=0, keepdims=True)
            gai += jnp.sum(li_ * spr - lr_ * spi, axis=0, keepdims=True)
        gar_ref[...] += gar
        gai_ref[...] += gai

    tab = pl.BlockSpec((CH, SW), lambda j, i: (0, j))
    colblk = pl.BlockSpec((tb, UW), lambda j, i: (rev(i), j))
    vec = lambda w: pl.BlockSpec((1, w), lambda j, i: (0, j))
    return pl.pallas_call(
        body, name="s5_scan_bwd", grid=(NJ, nb),
        in_specs=[colblk, colblk,
                  pl.BlockSpec((None, None, 1, 2 * SW), lambda j, i: (j, rev(i), 0, 0)),
                  pl.BlockSpec((None, UW, 2 * SW), lambda j, i: (j, 0, 0)),
                  pl.BlockSpec((None, 2 * SW, UW), lambda j, i: (j, 0, 0)),
                  vec(SW), vec(SW), tab, tab, tab, tab, vec(UW)],
        out_specs=[colblk, vec(UW),
                   pl.BlockSpec((None, 2 * SW, UW), lambda j, i: (j, 0, 0)),
                   pl.BlockSpec((None, UW, 2 * SW), lambda j, i: (j, 0, 0)),
                   vec(SW), vec(SW)],
        out_shape=[SDS((L, D), F32), SDS((1, D), F32),
                   SDS((NJ, 2 * SW, UW), F32), SDS((NJ, UW, 2 * SW), F32),
                   SDS((1, NSTATE), F32), SDS((1, NSTATE), F32)],
        scratch_shapes=[pltpu.VMEM((1, 2 * SW), F32), pltpu.VMEM((NT, ncb, 2 * SW), F32),
                        pltpu.VMEM((NT + 1, ncb, 2 * SW), F32), pltpu.VMEM((NT, ncb, 2 * SW), F32)],
        compiler_params=_cp(48, ARB2),
    )(p, dypre, states, bb, cm, abr, abi, par, pai, pbr, pbi, dskip)


def _rope_tables(L, inv):
    tm = min(512, L)

    def body(inv_ref, cos_ref, sin_ref):
        pos = (lax.broadcasted_iota(jnp.int32, (tm, DK // 2), 0) + pl.program_id(0) * tm).astype(F32)
        ang = pos * inv_ref[...]
        cos_ref[...] = jnp.cos(ang)
        sin_ref[...] = jnp.sin(ang)

    blk = pl.BlockSpec((tm, DK // 2), lambda i: (i, 0))
    return pl.pallas_call(body, name="rope_tables", grid=(L // tm,), in_specs=[_full((1, DK // 2))],
                          out_specs=[blk, blk], out_shape=[SDS((L, DK // 2), F32)] * 2)(inv)


def _rot(x, cos, sin):
    x1, x2 = x[:, :DK // 2], x[:, DK // 2:]
    return jnp.concatenate([x1 * cos - x2 * sin, x1 * sin + x2 * cos], axis=1)


def _unrot(d, cos, sin):
    d1, d2 = d[:, :DK // 2], d[:, DK // 2:]
    return jnp.concatenate([d1 * cos + d2 * sin, d2 * cos - d1 * sin], axis=1)


def _head_log_gamma(h):
    lg = jnp.float32(LOG_G[0])
    for hh in range(1, HEADS):
        lg = jnp.where(h == hh, jnp.float32(LOG_G[hh]), lg)
    return lg


def _ret_decays(h):
    lg = _head_log_gamma(h)
    n = lax.broadcasted_iota(jnp.int32, (CH, CH), 0)
    m = lax.broadcasted_iota(jnp.int32, (CH, CH), 1)
    diff = (n - m).astype(F32)
    decay = jnp.where(n >= m, jnp.exp(lg * jnp.maximum(diff, 0.0)), 0.0)
    idx = lax.broadcasted_iota(jnp.int32, (CH, 1), 0).astype(F32)
    xi = jnp.exp(lg * (idx + 1.0))
    zeta = jnp.exp(lg * (CH - 1.0 - idx))
    cd = jnp.exp(jnp.full((1, 1), lg * CH, F32))
    return decay, xi, zeta, cd


def _ret_chunk_fwd(q, k, v, cos, sin, s_prev_b, decay, xi, zeta):
    qr = _rot(q, cos, sin)
    kr = _rot(k, cos, sin) * (DK ** -0.5)
    scores = _mm_nt(qr, kr) * decay
    o = _mm(scores, v) + _mm(qr * xi, s_prev_b)
    local = _mm_tn(kr * zeta, v)
    mu = jnp.mean(o, axis=-1, keepdims=True)
    oc = o - mu
    rstd = lax.rsqrt(jnp.mean(oc * oc, axis=-1, keepdims=True) + EPS)
    return qr, kr, scores, local, oc * rstd, rstd


def _ret_fwd(p, cos, sin, gain):
    L = p.shape[0]
    nb = L // CH

    def body(q_ref, k_ref, v_ref, bz_ref, cos_ref, sin_ref, g_ref, yb_ref, st_ref, state):
        h = pl.program_id(0)

        @pl.when(pl.program_id(1) == 0)
        def _():
            state[...] = jnp.zeros_like(state)

        decay, xi, zeta, cd = _ret_decays(h)
        s_prev = state[...]
        s_prev_b = s_prev.astype(MXU)
        st_ref[...] = s_prev_b
        _, _, _, local, on, _ = _ret_chunk_fwd(q_ref[...], k_ref[...], v_ref[...], cos_ref[...], sin_ref[...],
                                               s_prev_b, decay, xi, zeta)
        state[...] = s_prev * cd + local
        act, _ = _silu_and_grad(bz_ref[...])
        yb_ref[...] = (on * g_ref[...] * act).astype(yb_ref.dtype)

    col = lambda base: pl.BlockSpec((CH, DK), lambda h, i: (i, base + h))
    rope = pl.BlockSpec((CH, DK // 2), lambda h, i: (i, 0))
    return pl.pallas_call(
        body, name="ret_fwd", grid=(HEADS, nb),
        in_specs=[col(8), col(12), col(16), col(20), rope, rope, pl.BlockSpec((1, DK), lambda h, i: (0, h))],
        out_specs=[pl.BlockSpec((CH, DK), lambda h, i: (i, h)),
                   pl.BlockSpec((None, None, DK, DK), lambda h, i: (h, i, 0, 0))],
        out_shape=[SDS((L, D), MXU), SDS((HEADS, nb, DK, DK), MXU)],
        scratch_shapes=[pltpu.VMEM((DK, DK), F32)],
        compiler_params=_cp(32, ARB2),
    )(p, p, p, p, cos, sin, gain)


def _ret_bwd(p, cos, sin, gain, states, dx1, wout_e):
    L = p.shape[0]
    nb = L // CH
    rev = lambda i: nb - 1 - i

    def body(q_ref, k_ref, v_ref, bz_ref, cos_ref, sin_ref, g_ref, st_ref, dx1_ref, wo_ref,
             dq_ref, dk_ref, dv_ref, dbz_ref, yb_ref, gg_ref, gstate):
        h = pl.program_id(0)

        @pl.when(pl.program_id(1) == 0)
        def _():
            gstate[...] = jnp.zeros_like(gstate)
            gg_ref[...] = jnp.zeros_like(gg_ref)

        decay, xi, zeta, cd = _ret_decays(h)
        cos, sin, v = cos_ref[...], sin_ref[...], v_ref[...]
        s_prev_b = st_ref[...]
        qr, kr, scores, _, on, rstd = _ret_chunk_fwd(q_ref[...], k_ref[...], v, cos, sin, s_prev_b, decay, xi, zeta)
        gain_h = g_ref[...]
        bz = bz_ref[...]
        act, dact = _silu_and_grad(bz)
        out = on * gain_h
        yb_ref[...] = (out * act).astype(yb_ref.dtype)
        dyb = _mm_nt(dx1_ref[...], wo_ref[...])
        dbz_ref[...] = (dyb * out * dact).astype(dbz_ref.dtype)
        dout = dyb * act
        gg_ref[...] += jnp.sum(dout * on, axis=0, keepdims=True)
        don = dout * gain_h
        do = rstd * (don - jnp.mean(don, axis=-1, keepdims=True) - on * jnp.mean(don * on, axis=-1, keepdims=True))
        gnext = gstate[...]
        gnext_b = gnext.astype(MXU)
        dscores = _mm_nt(do, v) * decay
        dv_ref[...] = (_mm_tn(scores, do) + _mm(kr * zeta, gnext_b)).astype(dv_ref.dtype)
        dqr = _mm(dscores, kr) + _mm_nt(do, s_prev_b) * xi
        dkr = _mm_tn(dscores, qr) + _mm_nt(v, gnext_b) * zeta
        gstate[...] = gnext * cd + _mm_tn(qr * xi, do)
        dq_ref[...] = _unrot(dqr, cos, sin).astype(dq_ref.dtype)
        dk_ref[...] = (_unrot(dkr, cos, sin) * (DK ** -0.5)).astype(dk_ref.dtype)

    col = lambda base: pl.BlockSpec((CH, DK), lambda h, i: (rev(i), base + h))
    rope = pl.BlockSpec((CH, DK // 2), lambda h, i: (rev(i), 0))
    outc = pl.BlockSpec((CH, DK), lambda h, i: (rev(i), h))
    act_out = SDS((L, D), MXU)
    return pl.pallas_call(
        body, name="ret_bwd", grid=(HEADS, nb),
        in_specs=[col(8), col(12), col(16), col(20), rope, rope, pl.BlockSpec((1, DK), lambda h, i: (0, h)),
                  pl.BlockSpec((None, None, DK, DK), lambda h, i: (h, rev(i), 0, 0)),
                  pl.BlockSpec((CH, D), lambda h, i: (rev(i), 0)),
                  pl.BlockSpec((DK, D), lambda h, i: (4 + h, 0))],
        out_specs=[outc, outc, outc, outc, outc, pl.BlockSpec((1, DK), lambda h, i: (0, h))],
        out_shape=[act_out, act_out, act_out, act_out, act_out, SDS((1, D), F32)],
        scratch_shapes=[pltpu.VMEM((DK, DK), F32)],
        compiler_params=_cp(32, ARB2),
    )(p, p, p, p, cos, sin, gain, states, dx1, wout_e)


def _out_even(x, ya, yb, wout):
    L = x.shape[0]
    tm = min(512, L)

    def body(x_ref, ya_ref, yb_ref, w_ref, o_ref):
        cat = jnp.concatenate([ya_ref[...], yb_ref[...]], axis=1)
        o_ref[...] = x_ref[...] + jnp.dot(cat, w_ref[...], preferred_element_type=F32)

    row = pl.BlockSpec((tm, D), lambda i: (i, 0))
    return pl.pallas_call(
        body, name="out_even", grid=(L // tm,), in_specs=[row, row, row, _full((DI, D))],
        out_specs=row, out_shape=SDS((L, D), F32), compiler_params=_cp(32, ("arbitrary",)),
    )(x, ya, yb, wout)


def _sgu_core(pv, gain, ws_ref, bs_ref):
    pu, pvv, z = pv[:, :DI], pv[:, DI:2 * DI], pv[:, 2 * DI:]
    u = _gelu(pu)
    v = _gelu(pvv)
    mu = jnp.mean(v, axis=-1, keepdims=True)
    vc = v - mu
    rstd = lax.rsqrt(jnp.mean(vc * vc, axis=-1, keepdims=True) + EPS)
    vhat = vc * rstd
    vn = vhat * gain
    t = lax.broadcasted_iota(jnp.int32, (CH, CH), 0)
    s_ = lax.broadcasted_iota(jnp.int32, (CH, CH), 1)
    mask = t >= s_
    wm = [jnp.where(mask, ws_ref[g], 0.0).astype(MXU) for g in range(SG)]
    s = jnp.concatenate([_mm(wm[g], vn[:, g * SGD:(g + 1) * SGD]) + bs_ref[g] for g in range(SG)], axis=1)
    return pu, pvv, z, u, vhat, rstd, vn, mask, wm, s


def _sgu_fwd(p2, x1, gain, wsp, bsp, wout, fnorm, tgt):
    L = p2.shape[0]

    def body(p_ref, x1_ref, g_ref, ws_ref, bs_ref, wo_ref, fn_ref, t_ref, dx2_ref, gfn_ref, loss_ref):
        @pl.when(pl.program_id(0) == 0)
        def _():
            gfn_ref[...] = jnp.zeros_like(gfn_ref)
            loss_ref[...] = jnp.zeros_like(loss_ref)

        _, _, z, u, _, _, _, _, _, s = _sgu_core(p_ref[...], g_ref[...], ws_ref, bs_ref)
        act, _ = _silu_and_grad(z)
        x2 = x1_ref[...] + _mm(u * s * act, wo_ref[...])
        xhat, r = _rms(x2)
        fn = fn_ref[...]
        e = xhat * fn - t_ref[...]
        loss_ref[...] += 0.5 * jnp.sum(jnp.mean(e * e, axis=-1, keepdims=True), axis=0, keepdims=True)
        do = e * (1.0 / D)
        gfn_ref[...] += jnp.sum(do * xhat, axis=0, keepdims=True)
        dxhat = do * fn
        dx2_ref[...] = r * (dxhat - xhat * jnp.mean(dxhat * xhat, axis=-1, keepdims=True))

    row = pl.BlockSpec((CH, D), lambda i: (i, 0))
    return pl.pallas_call(
        body, name="sgu_fwd", grid=(L // CH,),
        in_specs=[pl.BlockSpec((CH, NIN), lambda i: (i, 0)), row, _full((1, DI)), _full((SG, CH, CH)),
                  _full((SG, CH, 1)), _full((DI, D)), _full((1, D)), row],
        out_specs=[row, _full((1, D)), _full((1, 1))],
        out_shape=[SDS((L, D), F32), SDS((1, D), F32), SDS((1, 1), F32)],
        compiler_params=_cp(48, ("arbitrary",)),
    )(p2, x1, gain, wsp, bsp, wout, fnorm, tgt)


def _sgu_bwd(p2, dx2, gain, wsp, bsp, wout):
    L = p2.shape[0]

    def body(p_ref, dx2_ref, g_ref, ws_ref, bs_ref, wo_ref, dp_ref, y_ref, gg_ref, gws_ref, gbs_ref):
        @pl.when(pl.program_id(0) == 0)
        def _():
            gg_ref[...] = jnp.zeros_like(gg_ref)
            gws_ref[...] = jnp.zeros_like(gws_ref)
            gbs_ref[...] = jnp.zeros_like(gbs_ref)

        gain = g_ref[...]
        pu, pvv, z, u, vhat, rstd, vn, mask, wm, s = _sgu_core(p_ref[...], gain, ws_ref, bs_ref)
        act, dact = _silu_and_grad(z)
        y_ref[...] = (u * s * act).astype(y_ref.dtype)
        dy = _mm_nt(dx2_ref[...], wo_ref[...])
        du = dy * s * act
        ds = dy * u * act
        dz = dy * u * s * dact
        dvn = []
        for g in range(SG):
            ds_g = ds[:, g * SGD:(g + 1) * SGD]
            vn_g = vn[:, g * SGD:(g + 1) * SGD]
            gbs_ref[g] += jnp.sum(ds_g, axis=1, keepdims=True)
            gws_ref[g] += jnp.where(mask, _mm_nt(ds_g, vn_g), 0.0)
            dvn.append(_mm_tn(wm[g], ds_g))
        dvn = jnp.concatenate(dvn, axis=1)
        gg_ref[...] += jnp.sum(dvn * vhat, axis=0, keepdims=True)
        dvhat = dvn * gain
        dv = rstd * (dvhat - jnp.mean(dvhat, axis=-1, keepdims=True)
                     - vhat * jnp.mean(dvhat * vhat, axis=-1, keepdims=True))
        dp_ref[...] = jnp.concatenate([du * _gelu_grad(pu), dv * _gelu_grad(pvv), dz], axis=1).astype(dp_ref.dtype)

    return pl.pallas_call(
        body, name="sgu_bwd", grid=(L // CH,),
        in_specs=[pl.BlockSpec((CH, NIN), lambda i: (i, 0)), pl.BlockSpec((CH, D), lambda i: (i, 0)),
                  _full((1, DI)), _full((SG, CH, CH)), _full((SG, CH, 1)), _full((DI, D))],
        out_specs=[pl.BlockSpec((CH, NIN), lambda i: (i, 0)), pl.BlockSpec((CH, DI), lambda i: (i, 0)),
                   _full((1, DI)), _full((SG, CH, CH)), _full((SG, CH, 1))],
        out_shape=[SDS((L, NIN), MXU), SDS((L, DI), MXU), SDS((1, DI), F32), SDS((SG, CH, CH), F32),
                   SDS((SG, CH, 1), F32)],
        compiler_params=_cp(48, ("arbitrary",)),
    )(p2, dx2, gain, wsp, bsp, wout)


def _adamw(w, m, v, gparts, name):
    R, C = w.shape
    tr = R
    for cand in (256, 128, 64, 32, 16, 8):
        if R % cand == 0 and R > cand:
            tr = cand
            break

    def body(w_ref, m_ref, v_ref, gp_ref, g_ref, d_ref, mo_ref, vo_ref):
        g = gp_ref[0].astype(F32)
        for s in range(1, NDEV):
            g = g + gp_ref[s].astype(F32)
        w_ = w_ref[...]
        mn = ADAM_B1 * m_ref[...] + (1.0 - ADAM_B1) * g
        vn = ADAM_B2 * v_ref[...] + (1.0 - ADAM_B2) * (g * g)
        mhat = mn / BC1
        vhat = vn / BC2
        g_ref[...] = g
        d_ref[...] = -ADAM_LR * (mhat / (jnp.sqrt(vhat) + ADAM_EPS) + ADAM_WD * w_)
        mo_ref[...] = mn
        vo_ref[...] = vn

    blk = pl.BlockSpec((tr, C), lambda i: (i, 0))
    out = SDS((R, C), F32)
    return pl.pallas_call(
        body, name=name, grid=(R // tr,),
        in_specs=[blk, blk, blk, pl.BlockSpec((NDEV, tr, C), lambda i: (0, i, 0))],
        out_specs=[blk, blk, blk, blk], out_shape=[out, out, out, out],
        compiler_params=_cp(40, ("arbitrary",)),
    )(w, m, v, gparts)


MESH = pl.DeviceIdType.MESH
HBM_SPEC = pl.BlockSpec(memory_space=pltpu.HBM)
SEM_SPEC = pl.BlockSpec(memory_space=pltpu.SEMAPHORE)
EFFECT = pltpu.SideEffectType.DATAFLOW_SIDE_EFFECTING


def _me_and_peers():
    x, y, c = lax.axis_index("x"), lax.axis_index("y"), lax.axis_index("c")
    me = 4 * x + 2 * y + c
    peers = []
    for r in range(1, NDEV):
        px, py, pc = x ^ ((r >> 2) & 1), y ^ ((r >> 1) & 1), c ^ (r & 1)
        peers.append(((px, py, pc), 4 * px + 2 * py + pc))
    return me, peers


def _land_shape(a, scatter):
    return (NDEV,) + (a.shape[1:] if scatter else a.shape)


def _remote(src, dst, send_sems, recv_sems, r, k, n, dev):
    i = r * n + k
    return pltpu.make_async_remote_copy(src_ref=src, dst_ref=dst, send_sem=send_sems.at[i], recv_sem=recv_sems.at[i],
                                        device_id=dev, device_id_type=MESH)


def _exchange(arrays, scatter, name):
    n = len(arrays)
    out_shape = [SDS(_land_shape(a, scatter), a.dtype) for a in arrays]

    def body(*refs):
        ins, outs = refs[:n], refs[n:2 * n]
        send_sems, recv_sems, loc_sems = refs[2 * n:]
        me, peers = _me_and_peers()
        local = []
        for k in range(n):
            src = ins[k].at[me] if scatter else ins[k]
            cp = pltpu.make_async_copy(src, outs[k].at[me], loc_sems.at[k])
            cp.start()
            local.append(cp)
        sends = []
        for r, (dev, lin) in enumerate(peers):
            for k in range(n):
                src = ins[k].at[lin] if scatter else ins[k]
                cp = _remote(src, outs[k].at[me], send_sems, recv_sems, r, k, n, dev)
                cp.start()
                sends.append(cp)
        for r, (dev, lin) in enumerate(peers):
            for k in range(n):
                src = ins[k].at[me] if scatter else ins[k]
                _remote(src, outs[k].at[lin], send_sems, recv_sems, r, k, n, dev).wait_recv()
        for cp in sends:
            cp.wait_send()
        for cp in local:
            cp.wait()

    return pl.pallas_call(
        body, name=name, in_specs=[HBM_SPEC] * n, out_specs=[HBM_SPEC] * n, out_shape=out_shape,
        scratch_shapes=[pltpu.SemaphoreType.DMA(((NDEV - 1) * n,)), pltpu.SemaphoreType.DMA(((NDEV - 1) * n,)),
                        pltpu.SemaphoreType.DMA((n,))],
    )(*arrays)


def _exchange_start(arrays, scatter, name):
    n = len(arrays)
    lands = [lax.empty(_land_shape(a, scatter), a.dtype) for a in arrays]

    def body(*refs):
        ins, lnd = refs[:n], refs[n:2 * n]
        send_sems, recv_sems = refs[2 * n], refs[2 * n + 1]
        token = refs[-1]
        me, peers = _me_and_peers()
        for r, (dev, lin) in enumerate(peers):
            for k in range(n):
                src = ins[k].at[lin] if scatter else ins[k]
                _remote(src, lnd[k].at[me], send_sems, recv_sems, r, k, n, dev).start()
        token[...] = jnp.zeros_like(token)

    sem = pltpu.SemaphoreType.DMA(((NDEV - 1) * n,))
    outs = pl.pallas_call(
        body, name=name,
        out_shape=(sem, sem, *[pltpu.HBM(a.shape, a.dtype) for a in arrays],
                   *[pltpu.HBM(l.shape, l.dtype) for l in lands], SDS((8, 128), F32)),
        in_specs=[HBM_SPEC] * (2 * n),
        out_specs=(SEM_SPEC, SEM_SPEC, *[HBM_SPEC] * (2 * n), pl.BlockSpec(memory_space=pltpu.VMEM)),
        input_output_aliases={k: 2 + k for k in range(2 * n)},
        compiler_params=pltpu.CompilerParams(has_side_effects=EFFECT),
    )(*[pltpu.with_memory_space_constraint(a, pltpu.HBM) for a in arrays],
      *[pltpu.with_memory_space_constraint(l, pltpu.HBM) for l in lands])
    return (n, scatter, outs[0], outs[1], outs[2:2 + n], outs[2 + n:2 + 2 * n]), outs[-1]


def _exchange_wait(handle, after, name):
    n, scatter, send_sems, recv_sems, thru, lands = handle

    def body(*refs):
        ins, lnd = refs[:n], refs[n:2 * n]
        send_sems, recv_sems = refs[2 * n], refs[2 * n + 1]
        me, peers = _me_and_peers()
        for r, (dev, lin) in enumerate(peers):
            for k in range(n):
                src = ins[k].at[lin] if scatter else ins[k]
                cp = _remote(src, lnd[k].at[lin], send_sems, recv_sems, r, k, n, dev)
                cp.wait_send()
                cp.wait_recv()

    outs = pl.pallas_call(
        body, name=name,
        out_shape=(*[pltpu.HBM(a.shape, a.dtype) for a in thru], *[pltpu.HBM(l.shape, l.dtype) for l in lands]),
        in_specs=[HBM_SPEC] * (2 * n) + [SEM_SPEC, SEM_SPEC, pl.BlockSpec(memory_space=pl.ANY)],
        out_specs=tuple([HBM_SPEC] * (2 * n)),
        input_output_aliases={k: k for k in range(2 * n)},
        compiler_params=pltpu.CompilerParams(has_side_effects=EFFECT),
    )(*thru, *lands, send_sems, recv_sems, after)
    return list(outs[n:])


def _with_own(land, own, me):
    return lax.dynamic_update_slice_in_dim(land, own[None], me, axis=0)


def _local_step(x, tgt, norm_even, win_e, lam_re, lam_im, log_dt, b_re, b_im, c_re, c_im, s5_d, bglu,
                ret_gain, wsp, bsp, fnorm, late_weights, emit):
    L = x.shape[0]
    lr3, li3 = lam_re.reshape(G, 1, P), lam_im.reshape(G, 1, P)
    dt3 = log_dt.reshape(G, 1, 1)
    br3, bi3 = jnp.swapaxes(b_re, 1, 2), jnp.swapaxes(b_im, 1, 2)
    abr3, abi3, bbr3, bbi3 = _s5_disc(lr3, li3, dt3, br3, bi3)
    bb = jnp.concatenate([_embed(bbr3), _embed(bbi3)], axis=2).astype(MXU)
    cm = jnp.concatenate([_embed(jnp.swapaxes(c_re, 1, 2)), -_embed(jnp.swapaxes(c_im, 1, 2))], axis=1).astype(MXU)
    abr, abi = abr3.reshape(1, NSTATE), abi3.reshape(1, NSTATE)
    pwr, pwi, _, _ = _s5_tables(abr, abi, NT, "s5_tables_step")
    par, pai, pbr, pbi = _s5_tables(pwr[NT - 1:NT], pwi[NT - 1:NT], CH, "s5_tables_chunk")
    inv = (ROPE_BASE ** (-jnp.arange(DK // 2, dtype=F32) / (DK // 2))).reshape(1, DK // 2)
    cos, sin = _rope_tables(L, inv)
    bsp3 = bsp.reshape(SG, CH, 1)

    def after_token(v, token):
        return v if token is None else v + token[0:1, 0:1]

    p = _in_proj(x, norm_even, win_e, "in_even")
    wglu, wout_e, norm_odd, win_o, sgu_gain, wout_o = late_weights(p)
    ypre, s5_states = _s5_scan_fwd(p, bb, cm, abr, abi, par, pai, s5_d)
    ya = _s5_gate_fwd(ypre, p, wglu, bglu)
    yb, ret_states = _ret_fwd(p, cos, sin, ret_gain)
    x1 = _out_even(x, ya, yb, wout_e)
    p2 = _in_proj(x1, norm_odd, win_o, "in_odd")
    dx2, g_fnorm, loss = _sgu_fwd(p2, x1, sgu_gain, wsp, bsp3, wout_o, fnorm, tgt)

    dp2, y_o, g_sgu_gain, g_wsp, g_bsp = _sgu_bwd(p2, dx2, sgu_gain, wsp, bsp3, wout_o)
    g_wout_o = _wgrad_rows(y_o, dx2, "wgrad_out_odd")
    g_win_o = _wgrad_cols(x1, norm_odd, dp2, "wgrad_in_odd")
    tok = emit("odd", dict(w_in_odd=g_win_o, w_out_odd=g_wout_o), g_win_o)
    dx1, g_norm_odd = _in_proj_bwd_x(dp2, x1, after_token(norm_odd, tok), win_o, dx2, "in_odd_bwd")

    dypre, daz, yg, dt, ya2, g_bglu = _s5_gate_bwd(ypre, p, dx1, wout_e, wglu, bglu)
    du, g_d, g_cm, g_bb, g_ar, g_ai = _s5_scan_bwd(p, dypre, s5_states, bb, cm, abr, abi, par, pai, pbr, pbi, s5_d)
    dbbr3 = _diag_blocks(g_bb[:, :, :SW], HG, P)
    dbbi3 = _diag_blocks(g_bb[:, :, SW:], HG, P)
    g_c_re = jnp.swapaxes(_diag_blocks(g_cm[:, :SW, :], P, HG), 1, 2)
    g_c_im = -jnp.swapaxes(_diag_blocks(g_cm[:, SW:, :], P, HG), 1, 2)
    g_lr3, g_li3, g_dt3, g_br3, g_bi3 = _s5_disc_bwd(
        lr3, li3, dt3, br3, bi3, g_ar.reshape(G, 1, P), g_ai.reshape(G, 1, P), dbbr3, dbbi3)
    dq, dk, dv, dbz, yb2, g_ret_gain = _ret_bwd(p, cos, sin, ret_gain, ret_states, dx1, wout_e)
    small = dict(
        s5_lam_re=g_lr3.reshape(G, P), s5_lam_im=g_li3.reshape(G, P),
        s5_log_dt=g_dt3.reshape(1, G), s5_b_re=jnp.swapaxes(g_br3, 1, 2), s5_b_im=jnp.swapaxes(g_bi3, 1, 2),
        s5_c_re=g_c_re, s5_c_im=g_c_im, s5_d=g_d, s5_b_glu=g_bglu, ret_gn_gain=g_ret_gain,
        norm_odd=g_norm_odd, sgu_norm_gain=g_sgu_gain, sgu_w_spatial=g_wsp, sgu_b_spatial=g_bsp.reshape(SG, CH),
        final_norm=g_fnorm)
    emit("small", small, g_ret_gain)
    g_wout_e = jnp.concatenate([_wgrad_rows(ya2, dx1, "wgrad_out_even_s5").reshape(NDEV // 2, DI // NDEV, D),
                                _wgrad_rows(yb2, dx1, "wgrad_out_even_ret").reshape(NDEV // 2, DI // NDEV, D)], axis=0)
    g_wglu = _wgrad_rows(yg, dt, "wgrad_glu")
    emit("even_rows", dict(s5_w_glu=g_wglu, w_out_even=g_wout_e), g_wglu)
    dp = jnp.concatenate([du.astype(MXU), daz, dq, dk, dv, dbz], axis=1)
    g_win_e = _wgrad_cols(x, norm_even, dp, "wgrad_in_even")
    tok = emit("even_cols", dict(w_in_even=g_win_e), g_win_e)
    dx, g_norm_even = _in_proj_bwd_x(dp, x, after_token(norm_even, tok), win_e, dx1, "in_even_bwd")
    emit("last", dict(norm_even=g_norm_even), g_norm_even)
    return loss, dx


WEIGHTS = ['norm_even', 'w_in_even', 's5_lam_re', 's5_lam_im', 's5_log_dt', 's5_b_re', 's5_b_im', 's5_c_re',
           's5_c_im', 's5_d', 's5_w_glu', 's5_b_glu', 'ret_gn_gain', 'w_out_even', 'norm_odd', 'w_in_odd',
           'sgu_norm_gain', 'sgu_w_spatial', 'sgu_b_spatial', 'w_out_odd', 'final_norm']
BIG = ['w_in_even', 's5_w_glu', 'w_out_even', 'w_in_odd', 'w_out_odd']
SHARDED_SMALL = {'norm_odd': D // NDEV, 'sgu_norm_gain': DI // NDEV}
SMALL = [n for n in WEIGHTS if n not in BIG and n != 'norm_even']
LANES = 128


def kernel(x, norm_even, w_in_even, s5_lam_re, s5_lam_im, s5_log_dt, s5_b_re, s5_b_im, s5_c_re, s5_c_im, s5_d, s5_w_glu, s5_b_glu, ret_gn_gain, w_out_even, norm_odd, w_in_odd, sgu_norm_gain, sgu_w_spatial, sgu_b_spatial, w_out_odd, final_norm, loss_target, m_norm_even, m_w_in_even, m_s5_lam_re, m_s5_lam_im, m_s5_log_dt, m_s5_b_re, m_s5_b_im, m_s5_c_re, m_s5_c_im, m_s5_d, m_s5_w_glu, m_s5_b_glu, m_ret_gn_gain, m_w_out_even, m_norm_odd, m_w_in_odd, m_sgu_norm_gain, m_sgu_w_spatial, m_sgu_b_spatial, m_w_out_odd, m_final_norm, v_norm_even, v_w_in_even, v_s5_lam_re, v_s5_lam_im, v_s5_log_dt, v_s5_b_re, v_s5_b_im, v_s5_c_re, v_s5_c_im, v_s5_d, v_s5_w_glu, v_s5_b_glu, v_ret_gn_gain, v_w_out_even, v_norm_odd, v_w_in_odd, v_sgu_norm_gain, v_sgu_w_spatial, v_sgu_b_spatial, v_w_out_odd, v_final_norm):
    args = dict(locals())
    w = {n: args[n] for n in WEIGHTS}
    m = {n: args["m_" + n] for n in WEIGHTS}
    v = {n: args["v_" + n] for n in WEIGHTS}
    me = 4 * lax.axis_index("x") + 2 * lax.axis_index("y") + lax.axis_index("c")

    (win_e,) = _exchange([w['w_in_even'][0].astype(MXU)], False, "gather_w_in_even")
    late_own = [w['s5_w_glu'][0].astype(MXU), w['w_out_even'][0].astype(MXU), w['norm_odd'],
                w['w_in_odd'][0].astype(MXU), w['sgu_norm_gain'], w['w_out_odd'][0].astype(MXU)]
    late_handle, _ = _exchange_start(late_own, False, "gather_late_start")

    def late_weights(after):
        lands = _exchange_wait(late_handle, after, "gather_late_wait")
        wglu, wout_e, nodd, win_o, sgug, wout_o = (_with_own(l, o, me) for l, o in zip(lands, late_own))
        return (wglu.reshape(D, D), wout_e.reshape(DI, D), nodd.reshape(1, D), win_o, sgug.reshape(1, DI),
                wout_o.reshape(DI, D))

    pending = {}
    small_last = {}

    def emit(stage, grads, after):
        if stage == "last":
            small_last.update(grads)
            return None
        if stage == "small":
            flat = jnp.concatenate([grads[n].reshape(-1) for n in SMALL])
            flat = jnp.pad(flat, (0, (-flat.shape[0]) % (8 * LANES))).reshape(1, -1)
            handle, token = _exchange_start([flat], False, "small_start")
            pending[stage] = (handle, ["flat"], [flat], {n: grads[n].size for n in SMALL})
            return token
        names = list(grads)
        handle, token = _exchange_start([grads[n] for n in names], True, stage + "_start")
        pending[stage] = (handle, names, [lax.dynamic_index_in_dim(grads[n], me, 0, keepdims=False) for n in names])
        return token

    loss, dx = _local_step(
        x[0], loss_target[0], w['norm_even'], win_e, w['s5_lam_re'][0], w['s5_lam_im'][0], w['s5_log_dt'][0],
        w['s5_b_re'][0], w['s5_b_im'][0], w['s5_c_re'][0], w['s5_c_im'][0], w['s5_d'], w['s5_b_glu'],
        w['ret_gn_gain'], w['sgu_w_spatial'][0], w['sgu_b_spatial'][0], w['final_norm'].reshape(1, D),
        late_weights, emit)

    out_g, out_d, out_m, out_v = {}, {}, {}, {}
    after = dx
    for stage in ("odd", "even_rows", "even_cols"):
        handle, names, own = pending[stage][:3]
        lands = _exchange_wait(handle, after, stage + "_wait")
        for n, land, o in zip(names, lands, own):
            shp = w[n].shape
            r, c = shp[1], shp[2]
            res = _adamw(w[n].reshape(r, c), m[n].reshape(r, c), v[n].reshape(r, c), _with_own(land, o, me),
                         "adamw_" + n)
            out_g[n], out_d[n], out_m[n], out_v[n] = (t.reshape(shp) for t in res)
            after = res[0]

    handle, _, own, sizes = pending["small"]
    (gsmall,) = _exchange_wait(handle, after, "small_wait")
    gsmall = _with_own(gsmall, own[0], me).reshape(NDEV, -1)
    pieces, off = [], 0
    for n in SMALL:
        seg = gsmall[:, off:off + sizes[n]]
        if n in SHARDED_SMALL:
            seg = lax.dynamic_slice_in_dim(seg, me * SHARDED_SMALL[n], SHARDED_SMALL[n], axis=1)
        pieces.append(seg)
        off += sizes[n]
    gp = jnp.concatenate(pieces, axis=1)
    wpad = (-gp.shape[1]) % (8 * LANES)

    def pack(d):
        t = jnp.concatenate([d[n].reshape(-1) for n in SMALL])
        return jnp.pad(t, (0, wpad)).reshape(-1, LANES)

    gp = jnp.pad(gp, ((0, 0), (0, wpad))).reshape(NDEV, -1, LANES)
    res = _adamw(pack(w), pack(m), pack(v), gp, "adamw_small")
    off = 0
    for n in SMALL:
        size = w[n].size
        for dst, src in zip((out_g, out_d, out_m, out_v), res):
            dst[n] = src.reshape(-1)[off:off + size].reshape(w[n].shape)
        off += size
    (gne,) = _exchange([small_last['norm_even']], False, "gather_norm_even")
    n = 'norm_even'
    res = _adamw(w[n].reshape(8, LANES), m[n].reshape(8, LANES), v[n].reshape(8, LANES),
                 gne.reshape(NDEV, 8, LANES), "adamw_norm_even")
    out_g[n], out_d[n], out_m[n], out_v[n] = (t.reshape(w[n].shape) for t in res)

    loss_total = lax.psum(loss[0, 0], AXES)
    return (loss_total, dx[None], *[out_g[n] for n in WEIGHTS], *[out_d[n] for n in WEIGHTS],
            *[out_m[n] for n in WEIGHTS], *[out_v[n] for n in WEIGHTS])
```

```python
import math

import jax
import jax.numpy as jnp
from jax import lax
from jax.experimental import pallas as pl
from jax.experimental.pallas import tpu as pltpu

F32 = jnp.float32
MXU = jnp.bfloat16
AXES = ("x", "y", "c")
NDEV = 8
D = 1024
NIN = 6144
WIN_BLK = NIN // NDEV
DI = 2048
G, P, HG = 64, 64, 16
GB = 8
NJ = G // GB
SW = GB * P
UW = GB * HG
NSTATE = G * P
HEADS, DK = 4, 256
CH = 128
SG, SGD = 4, 512
EPS = 1e-6
ROPE_BASE = 10000.0
VMEM_CAP_V7X = 64 * 1024 * 1024
LOG_G = [math.log1p(-2.0 ** (-5.0 - h)) for h in range(HEADS)]
GELU_C = math.sqrt(2.0 / math.pi)

ADAM_LR, ADAM_B1, ADAM_B2, ADAM_EPS, ADAM_WD, ADAM_STEP = 0.001, 0.9, 0.999, 1e-08, 0.01, 10
BC1 = 1.0 - ADAM_B1 ** ADAM_STEP
BC2 = 1.0 - ADAM_B2 ** ADAM_STEP

SDS = jax.ShapeDtypeStruct
ARB2 = ("arbitrary", "arbitrary")


def _cp(vmem_mib, sem=None):
    kw = dict(vmem_limit_bytes=min(vmem_mib * 1024 * 1024, VMEM_CAP_V7X - 4 * 1024 * 1024))
    if sem is not None:
        kw["dimension_semantics"] = sem
    return pltpu.CompilerParams(**kw)


def _mm(a, b):
    return jnp.dot(a.astype(MXU), b.astype(MXU), preferred_element_type=F32)


def _mm_nt(a, b):
    return lax.dot_general(a.astype(MXU), b.astype(MXU), (((1,), (1,)), ((), ())), preferred_element_type=F32)


def _mm_tn(a, b):
    return lax.dot_general(a.astype(MXU), b.astype(MXU), (((0,), (0,)), ((), ())), preferred_element_type=F32)


def _gelu(x):
    th = jnp.tanh(GELU_C * (x + 0.044715 * x * x * x))
    return 0.5 * x * (1.0 + th)


def _gelu_grad(x):
    th = jnp.tanh(GELU_C * (x + 0.044715 * x * x * x))
    return 0.5 * (1.0 + th) + 0.5 * x * (1.0 - th * th) * GELU_C * (1.0 + 3.0 * 0.044715 * x * x)


def _silu_and_grad(x):
    s = jax.nn.sigmoid(x)
    return x * s, s * (1.0 + x * (1.0 - s))


def _full(shape):
    nd = len(shape)
    return pl.BlockSpec(shape, lambda *_: (0,) * nd)


def _rms(xf):
    r = lax.rsqrt(jnp.mean(xf * xf, axis=-1, keepdims=True) + EPS)
    return xf * r, r


ANY_SPEC = pl.BlockSpec(memory_space=pl.ANY)
NO_DEPS = ()


def _in_proj(x, gain, wst, name, deps=NO_DEPS):
    L = x.shape[0]
    tm = min(512, L)

    def body(x_ref, g_ref, w_ref, *rest):
        o_ref, h_scr = rest[len(deps):]

        @pl.when(pl.program_id(1) == 0)
        def _():
            xhat, _ = _rms(x_ref[...])
            h_scr[...] = (xhat * g_ref[...]).astype(MXU)

        o_ref[...] = jnp.dot(h_scr[...], w_ref[...], preferred_element_type=F32)

    return pl.pallas_call(
        body, name=name, grid=(L // tm, NDEV),
        in_specs=[pl.BlockSpec((tm, D), lambda i, n: (i, 0)), _full((1, D)),
                  pl.BlockSpec((None, D, WIN_BLK), lambda i, n: (n, 0, 0))] + [ANY_SPEC] * len(deps),
        out_specs=pl.BlockSpec((tm, WIN_BLK), lambda i, n: (i, n)),
        out_shape=SDS((L, NIN), F32),
        scratch_shapes=[pltpu.VMEM((tm, D), MXU)],
        compiler_params=_cp(32, ARB2),
    )(x, gain, wst, *deps)


def _in_proj_bwd_x(dp, x, gain, wst, dres, name, deps=NO_DEPS):
    L = x.shape[0]
    tm = min(512, L)

    def body(dp_ref, x_ref, g_ref, w_ref, dres_ref, *rest):
        dx_ref, gg_ref, acc = rest[len(deps):]
        i, n = pl.program_id(0), pl.program_id(1)

        @pl.when(n == 0)
        def _():
            acc[...] = jnp.zeros_like(acc)

        @pl.when((i == 0) & (n == 0))
        def _():
            gg_ref[...] = jnp.zeros_like(gg_ref)

        acc[...] += _mm_nt(dp_ref[...], w_ref[...])

        @pl.when(n == NDEV - 1)
        def _():
            xhat, r = _rms(x_ref[...])
            dh = acc[...]
            dxhat = dh * g_ref[...]
            dx_ref[...] = dres_ref[...] + r * (dxhat - xhat * jnp.mean(dxhat * xhat, axis=-1, keepdims=True))
            gg_ref[...] += jnp.sum(dh * xhat, axis=0, keepdims=True)

    return pl.pallas_call(
        body, name=name, grid=(L // tm, NDEV),
        in_specs=[pl.BlockSpec((tm, WIN_BLK), lambda i, n: (i, n)),
                  pl.BlockSpec((tm, D), lambda i, n: (i, 0)), _full((1, D)),
                  pl.BlockSpec((None, D, WIN_BLK), lambda i, n: (n, 0, 0)),
                  pl.BlockSpec((tm, D), lambda i, n: (i, 0))] + [ANY_SPEC] * len(deps),
        out_specs=[pl.BlockSpec((tm, D), lambda i, n: (i, 0)), _full((1, D))],
        out_shape=[SDS((L, D), F32), SDS((1, D), F32)],
        scratch_shapes=[pltpu.VMEM((tm, D), F32)],
        compiler_params=_cp(40, ARB2),
    )(dp, x, gain, wst, dres, *deps)


def _wgrad_cols(x, gain, dp, name, deps=NO_DEPS):
    L = x.shape[0]
    tk = min(512, L)
    nk = L // tk
    halves = 2
    nh = NDEV // halves

    def body(x_ref, g_ref, dp_ref, *rest):
        o_ref, acc = rest[len(deps):]
        k = pl.program_id(1)

        @pl.when(k == 0)
        def _():
            acc[...] = jnp.zeros_like(acc)

        xhat, _ = _rms(x_ref[...])
        acc[...] += _mm_tn(xhat * g_ref[...], dp_ref[...])

        @pl.when(k == nk - 1)
        def _():
            for c in range(nh):
                o_ref[c] = acc[:, c * WIN_BLK:(c + 1) * WIN_BLK].astype(o_ref.dtype)

    return pl.pallas_call(
        body, name=name, grid=(halves, nk),
        in_specs=[pl.BlockSpec((tk, D), lambda n, k: (k, 0)), _full((1, D)),
                  pl.BlockSpec((tk, nh * WIN_BLK), lambda n, k: (k, n))] + [ANY_SPEC] * len(deps),
        out_specs=pl.BlockSpec((nh, D, WIN_BLK), lambda n, k: (n, 0, 0)),
        out_shape=SDS((NDEV, D, WIN_BLK), MXU),
        scratch_shapes=[pltpu.VMEM((D, nh * WIN_BLK), F32)],
        compiler_params=_cp(56, ARB2),
    )(x, gain, dp, *deps)


def _wgrad_rows(a_parts, b, name, deps=NO_DEPS):
    L, N = b.shape
    na = len(a_parts)
    widths = [a.shape[1] for a in a_parts]
    M = sum(widths)
    tk = min(512, L)
    nk = L // tk

    def body(*refs):
        a_refs, b_ref = refs[:na], refs[na]
        o_ref, acc = refs[na + 1 + len(deps):]
        k = pl.program_id(0)

        @pl.when(k == 0)
        def _():
            acc[...] = jnp.zeros_like(acc)

        bv = b_ref[...].astype(MXU)
        off = 0
        for a_ref, wd in zip(a_refs, widths):
            acc[off:off + wd, :] += _mm_tn(a_ref[...], bv)
            off += wd

        @pl.when(k == nk - 1)
        def _():
            o_ref[...] = acc[...].astype(o_ref.dtype).reshape(o_ref.shape)

    return pl.pallas_call(
        body, name=name, grid=(nk,),
        in_specs=[pl.BlockSpec((tk, wd), lambda k: (k, 0)) for wd in widths]
        + [pl.BlockSpec((tk, N), lambda k: (k, 0))] + [ANY_SPEC] * len(deps),
        out_specs=_full((NDEV, M // NDEV, N)),
        out_shape=SDS((NDEV, M // NDEV, N), MXU),
        scratch_shapes=[pltpu.VMEM((M, N), F32)],
        compiler_params=_cp(48, ("arbitrary",)),
    )(*a_parts, b, *deps)


def _s5_disc_fn(lr_raw, li, logdt, br, bi):
    lr = jnp.minimum(lr_raw, -1e-4)
    dt = jnp.exp(logdt)
    mag = jnp.exp(lr * dt)
    abr = mag * jnp.cos(li * dt)
    abi = mag * jnp.sin(li * dt)
    den = lr * lr + li * li
    nre = abr - 1.0
    nim = abi
    zr = (nre * lr + nim * li) / den
    zi = (nim * lr - nre * li) / den
    return abr, abi, zr * br - zi * bi, zr * bi + zi * br


def _s5_disc(lr, li, logdt, br, bi):
    def body(lr_ref, li_ref, dt_ref, br_ref, bi_ref, abr_ref, abi_ref, bbr_ref, bbi_ref):
        abr, abi, bbr, bbi = _s5_disc_fn(lr_ref[...], li_ref[...], dt_ref[...], br_ref[...], bi_ref[...])
        abr_ref[...] = abr
        abi_ref[...] = abi
        bbr_ref[...] = bbr
        bbi_ref[...] = bbi

    s1, s3 = SDS((G, 1, P), F32), SDS((G, HG, P), F32)
    return pl.pallas_call(body, name="s5_disc", out_shape=[s1, s1, s3, s3])(lr, li, logdt, br, bi)


def _s5_disc_bwd(lr, li, logdt, br, bi, dabr, dabi, dbbr, dbbi):
    def body(lr_ref, li_ref, dt_ref, br_ref, bi_ref, c0, c1, c2, c3, o0, o1, o2, o3, o4):
        _, vjp = jax.vjp(_s5_disc_fn, lr_ref[...], li_ref[...], dt_ref[...], br_ref[...], bi_ref[...])
        g = vjp((c0[...], c1[...], c2[...], c3[...]))
        for o, v in zip((o0, o1, o2, o3, o4), g):
            o[...] = v

    s1, s3 = SDS((G, 1, P), F32), SDS((G, HG, P), F32)
    return pl.pallas_call(body, name="s5_disc_bwd", out_shape=[s1, s1, SDS((G, 1, 1), F32), s3, s3])(
        lr, li, logdt, br, bi, dabr, dabi, dbbr, dbbi)


def _s5_tables(abr, abi, rows, name):
    def body(ar_ref, ai_ref, pfr, pfi, pbr, pbi):
        pfr[0:1, :] = ar_ref[...]
        pfi[0:1, :] = ai_ref[...]
        pbr[rows - 1:rows, :] = ar_ref[...]
        pbi[rows - 1:rows, :] = ai_ref[...]
        n = 1
        while n < rows:
            er, ei = pfr[n - 1:n, :], pfi[n - 1:n, :]
            xr, xi = pfr[0:n, :], pfi[0:n, :]
            pfr[n:2 * n, :] = er * xr - ei * xi
            pfi[n:2 * n, :] = er * xi + ei * xr
            yr, yi = pbr[rows - n:rows, :], pbi[rows - n:rows, :]
            pbr[rows - 2 * n:rows - n, :] = er * yr - ei * yi
            pbi[rows - 2 * n:rows - n, :] = er * yi + ei * yr
            n *= 2

    s = SDS((rows, NSTATE), F32)
    return pl.pallas_call(body, name=name, out_shape=[s, s, s, s], compiler_params=_cp(40))(abr, abi)


def _cscan(br, bi, pr_ref, pi_ref, reverse):
    T = br.shape[0]
    sign = -1.0 if reverse else 1.0
    row = lax.broadcasted_iota(jnp.int32, br.shape, 0)
    k = 1
    while k < T:
        akr = pr_ref[k - 1:k, :]
        aki = sign * pi_ref[k - 1:k, :]

        def shift(v):
            if k % 8 == 0:
                z = jnp.zeros((k, v.shape[1]), v.dtype)
                return jnp.concatenate([v[k:], z], 0) if reverse else jnp.concatenate([z, v[:T - k]], 0)
            if reverse:
                return jnp.where(row < T - k, pltpu.roll(v, T - k, 0), 0.0)
            return jnp.where(row >= k, pltpu.roll(v, k, 0), 0.0)

        sr, si = shift(br), shift(bi)
        br, bi = br + akr * sr - aki * si, bi + akr * si + aki * sr
        k *= 2
    return br, bi


def _embed(t):
    a, b = t.shape[1], t.shape[2]
    return jnp.einsum("jgab,gh->jgahb", t.reshape(NJ, GB, a, b), jnp.eye(GB, dtype=t.dtype)).reshape(NJ, GB * a, GB * b)


def _diag_blocks(t, a, b):
    return jnp.einsum("jgahb,gh->jgab", t.reshape(NJ, GB, a, GB, b), jnp.eye(GB, dtype=t.dtype)).reshape(G, a, b)


NT = 8


def _chunks(L):
    ncb = min(CH, L // NT)
    return ncb, NT * ncb


def _cmul_add(ar, ai, xr, xi, br, bi):
    return ar * xr - ai * xi + br, ar * xi + ai * xr + bi


def _s5_states(u_ref, bb_ref, ar, ai, par_ref, pai_ref, cr, ci, ncb, bu_scr):
    er = ei = None
    for t in range(NT):
        bu = _mm(u_ref[pl.ds(t, ncb, stride=NT), :], bb_ref[...])
        bu_scr[t] = bu
        if t == 0:
            er, ei = bu[:, :SW], bu[:, SW:]
        else:
            er, ei = _cmul_add(ar, ai, er, ei, bu[:, :SW], bu[:, SW:])
    xr, xi = _cscan(er, ei, par_ref, pai_ref, False)
    fr, fi = _cmul_add(par_ref[0:ncb, :], pai_ref[0:ncb, :], cr, ci, xr, xi)
    row = lax.broadcasted_iota(jnp.int32, fr.shape, 0)
    cinr = jnp.where(row >= 1, pltpu.roll(fr, 1, 0), cr)
    cini = jnp.where(row >= 1, pltpu.roll(fi, 1, 0), ci)
    return cinr, cini, jnp.concatenate([fr[ncb - 1:ncb, :], fi[ncb - 1:ncb, :]], axis=1)


def _s5_scan_fwd(p, bb, cm, abr, abi, par, pai, dskip):
    L = p.shape[0]
    ncb, tb = _chunks(L)
    nb = L // tb

    def body(u_ref, bb_ref, cm_ref, ar_ref, ai_ref, par_ref, pai_ref, d_ref, ypre_ref, st_ref, carry, bu_scr):
        @pl.when(pl.program_id(1) == 0)
        def _():
            carry[...] = jnp.zeros_like(carry)

        c = carry[...]
        st_ref[...] = c
        ar, ai = ar_ref[...], ai_ref[...]
        sr, si, cnext = _s5_states(u_ref, bb_ref, ar, ai, par_ref, pai_ref, c[:, :SW], c[:, SW:], ncb, bu_scr)
        carry[...] = cnext
        for t in range(NT):
            bu = bu_scr[t]
            sr, si = _cmul_add(ar, ai, sr, si, bu[:, :SW], bu[:, SW:])
            rows = pl.ds(t, ncb, stride=NT)
            ypre_ref[rows, :] = _mm(jnp.concatenate([sr, si], axis=1), cm_ref[...]) + d_ref[...] * u_ref[rows, :]

    tab = pl.BlockSpec((CH, SW), lambda j, i: (0, j))
    vec = lambda w: pl.BlockSpec((1, w), lambda j, i: (0, j))
    return pl.pallas_call(
        body, name="s5_scan_fwd", grid=(NJ, nb),
        in_specs=[pl.BlockSpec((tb, UW), lambda j, i: (i, j)),
                  pl.BlockSpec((None, UW, 2 * SW), lambda j, i: (j, 0, 0)),
                  pl.BlockSpec((None, 2 * SW, UW), lambda j, i: (j, 0, 0)),
                  vec(SW), vec(SW), tab, tab, vec(UW)],
        out_specs=[pl.BlockSpec((tb, UW), lambda j, i: (i, j)),
                   pl.BlockSpec((None, None, 1, 2 * SW), lambda j, i: (j, i, 0, 0))],
        out_shape=[SDS((L, D), F32), SDS((NJ, nb, 1, 2 * SW), F32)],
        scratch_shapes=[pltpu.VMEM((1, 2 * SW), F32), pltpu.VMEM((NT, ncb, 2 * SW), F32)],
        compiler_params=_cp(40, ARB2),
    )(p, bb, cm, abr, abi, par, pai, dskip)


def _s5_gate_fwd(ypre, p, wglu, bglu):
    L = ypre.shape[0]
    tm = min(256, L)

    def body(y_ref, az_ref, wg_ref, bg_ref, ya_ref):
        yg = _gelu(y_ref[...])
        t = _mm(yg, wg_ref[...]) + bg_ref[...]
        act, _ = _silu_and_grad(az_ref[...])
        ya_ref[...] = (yg * jax.nn.sigmoid(t) * act).astype(ya_ref.dtype)

    return pl.pallas_call(
        body, name="s5_gate_fwd", grid=(L // tm,),
        in_specs=[pl.BlockSpec((tm, D), lambda i: (i, 0)), pl.BlockSpec((tm, D), lambda i: (i, 1)),
                  _full((D, D)), _full((1, D))],
        out_specs=pl.BlockSpec((tm, D), lambda i: (i, 0)),
        out_shape=SDS((L, D), MXU),
        compiler_params=_cp(32, ("arbitrary",)),
    )(ypre, p, wglu, bglu)


def _s5_gate_bwd(ypre, p, dx1, wout_e, wglu, bglu):
    L = ypre.shape[0]
    tm = min(256, L)

    def body(y_ref, az_ref, dx1_ref, wo_ref, wg_ref, bg_ref, dyp_ref, daz_ref, yg_ref, dt_ref, ya_ref, gbg_ref):
        @pl.when(pl.program_id(0) == 0)
        def _():
            gbg_ref[...] = jnp.zeros_like(gbg_ref)

        ypre_v = y_ref[...]
        yg = _gelu(ypre_v)
        sg = jax.nn.sigmoid(_mm(yg, wg_ref[...]) + bg_ref[...])
        act, dact = _silu_and_grad(az_ref[...])
        y2 = yg * sg
        dya = _mm_nt(dx1_ref[...], wo_ref[...])
        daz_ref[...] = (dya * y2 * dact).astype(daz_ref.dtype)
        dy2 = dya * act
        dt = dy2 * yg * sg * (1.0 - sg)
        dyg = dy2 * sg + _mm_nt(dt, wg_ref[...])
        dyp_ref[...] = dyg * _gelu_grad(ypre_v)
        yg_ref[...] = yg.astype(yg_ref.dtype)
        dt_ref[...] = dt.astype(dt_ref.dtype)
        ya_ref[...] = (y2 * act).astype(ya_ref.dtype)
        gbg_ref[...] += jnp.sum(dt, axis=0, keepdims=True)

    row = pl.BlockSpec((tm, D), lambda i: (i, 0))
    return pl.pallas_call(
        body, name="s5_gate_bwd", grid=(L // tm,),
        in_specs=[row, pl.BlockSpec((tm, D), lambda i: (i, 1)), row,
                  pl.BlockSpec((D, D), lambda i: (0, 0)), _full((D, D)), _full((1, D))],
        out_specs=[row, row, row, row, row, _full((1, D))],
        out_shape=[SDS((L, D), F32), SDS((L, D), MXU), SDS((L, D), MXU), SDS((L, D), MXU), SDS((L, D), MXU),
                   SDS((1, D), F32)],
        compiler_params=_cp(40, ("arbitrary",)),
    )(ypre, p, dx1, wout_e, wglu, bglu)


def _s5_scan_bwd(p, dypre, states, bb, cm, abr, abi, par, pai, pbr, pbi, dskip):
    L = p.shape[0]
    ncb, tb = _chunks(L)
    nb = L // tb
    rev = lambda i: nb - 1 - i

    def body(u_ref, dy_ref, st_ref, bb_ref, cm_ref, ar_ref, ai_ref, par_ref, pai_ref, pbr_ref, pbi_ref, d_ref,
             du_ref, gd_ref, gcm_ref, gbb_ref, gar_ref, gai_ref, lcarry, bu_scr, s_scr, gs_scr):
        @pl.when(pl.program_id(1) == 0)
        def _():
            lcarry[...] = jnp.zeros_like(lcarry)
            gd_ref[...] = jnp.zeros_like(gd_ref)
            gcm_ref[...] = jnp.zeros_like(gcm_ref)
            gbb_ref[...] = jnp.zeros_like(gbb_ref)
            gar_ref[...] = jnp.zeros_like(gar_ref)
            gai_ref[...] = jnp.zeros_like(gai_ref)

        ar, ai = ar_ref[...], ai_ref[...]
        c = st_ref[...]
        sr, si, _ = _s5_states(u_ref, bb_ref, ar, ai, par_ref, pai_ref, c[:, :SW], c[:, SW:], ncb, bu_scr)
        s_scr[0] = jnp.concatenate([sr, si], axis=1)
        for t in range(NT):
            bu = bu_scr[t]
            sr, si = _cmul_add(ar, ai, sr, si, bu[:, :SW], bu[:, SW:])
            s_scr[t + 1] = jnp.concatenate([sr, si], axis=1)
        fr = fi = None
        for t in reversed(range(NT)):
            gs = _mm_nt(dy_ref[pl.ds(t, ncb, stride=NT), :], cm_ref[...])
            gs_scr[t] = gs
            if t == NT - 1:
                fr, fi = gs[:, :SW], gs[:, SW:]
            else:
                fr, fi = _cmul_add(ar, -ai, fr, fi, gs[:, :SW], gs[:, SW:])
        xr, xi = _cscan(fr, fi, par_ref, pai_ref, True)
        lc = lcarry[...]
        lcr, lci = lc[:, :SW], lc[:, SW:]
        hr, hi = _cmul_add(pbr_ref[CH - ncb:CH, :], -pbi_ref[CH - ncb:CH, :], lcr, lci, xr, xi)
        lcarry[...] = jnp.concatenate([hr[0:1, :], hi[0:1, :]], axis=1)
        row = lax.broadcasted_iota(jnp.int32, hr.shape, 0)
        lr_ = jnp.where(row < ncb - 1, pltpu.roll(hr, ncb - 1, 0), lcr)
        li_ = jnp.where(row < ncb - 1, pltpu.roll(hi, ncb - 1, 0), lci)
        gar = jnp.zeros((1, SW), F32)
        gai = jnp.zeros((1, SW), F32)
        for t in reversed(range(NT)):
            gs = gs_scr[t]
            lr_, li_ = _cmul_add(ar, -ai, lr_, li_, gs[:, :SW], gs[:, SW:])
            rows = pl.ds(t, ncb, stride=NT)
            u_t, dy_t = u_ref[rows, :], dy_ref[rows, :]
            lam = jnp.concatenate([lr_, li_], axis=1)
            gbb_ref[...] += _mm_tn(u_t, lam)
            du_ref[rows, :] = _mm_nt(lam, bb_ref[...]) + dy_t * d_ref[...]
            gd_ref[...] += jnp.sum(dy_t * u_t, axis=0, keepdims=True)
            gcm_ref[...] += _mm_tn(s_scr[t + 1], dy_t)
            sp = s_scr[t]
            spr, spi = sp[:, :SW], sp[:, SW:]
            gar += jnp.sum(lr_ * spr + li_ * spi, axis=0, keepdims=True)
            gai += jnp.sum(li_ * spr - lr_ * spi, axis=0, keepdims=True)
        gar_ref[...] += gar
        gai_ref[...] += gai

    tab = pl.BlockSpec((CH, SW), lambda j, i: (0, j))
    colblk = pl.BlockSpec((tb, UW), lambda j, i: (rev(i), j))
    vec = lambda w: pl.BlockSpec((1, w), lambda j, i: (0, j))
    return pl.pallas_call(
        body, name="s5_scan_bwd", grid=(NJ, nb),
        in_specs=[colblk, colblk,
                  pl.BlockSpec((None, None, 1, 2 * SW), lambda j, i: (j, rev(i), 0, 0)),
                  pl.BlockSpec((None, UW, 2 * SW), lambda j, i: (j, 0, 0)),
                  pl.BlockSpec((None, 2 * SW, UW), lambda j, i: (j, 0, 0)),
                  vec(SW), vec(SW), tab, tab, tab, tab, vec(UW)],
        out_specs=[colblk, vec(UW),
                   pl.BlockSpec((None, 2 * SW, UW), lambda j, i: (j, 0, 0)),
                   pl.BlockSpec((None, UW, 2 * SW), lambda j, i: (j, 0, 0)),
                   vec(SW), vec(SW)],
        out_shape=[SDS((L, D), F32), SDS((1, D), F32),
                   SDS((NJ, 2 * SW, UW), F32), SDS((NJ, UW, 2 * SW), F32),
                   SDS((1, NSTATE), F32), SDS((1, NSTATE), F32)],
        scratch_shapes=[pltpu.VMEM((1, 2 * SW), F32), pltpu.VMEM((NT, ncb, 2 * SW), F32),
                        pltpu.VMEM((NT + 1, ncb, 2 * SW), F32), pltpu.VMEM((NT, ncb, 2 * SW), F32)],
        compiler_params=_cp(48, ARB2),
    )(p, dypre, states, bb, cm, abr, abi, par, pai, pbr, pbi, dskip)


def _rope_tables(L, inv):
    tm = min(512, L)

    def body(inv_ref, cos_ref, sin_ref):
        pos = (lax.broadcasted_iota(jnp.int32, (tm, DK // 2), 0) + pl.program_id(0) * tm).astype(F32)
        ang = pos * inv_ref[...]
        cos_ref[...] = jnp.cos(ang)
        sin_ref[...] = jnp.sin(ang)

    blk = pl.BlockSpec((tm, DK // 2), lambda i: (i, 0))
    return pl.pallas_call(body, name="rope_tables", grid=(L // tm,), in_specs=[_full((1, DK // 2))],
                          out_specs=[blk, blk], out_shape=[SDS((L, DK // 2), F32)] * 2)(inv)


def _rot(x, cos, sin):
    x1, x2 = x[:, :DK // 2], x[:, DK // 2:]
    return jnp.concatenate([x1 * cos - x2 * sin, x1 * sin + x2 * cos], axis=1)


def _unrot(d, cos, sin):
    d1, d2 = d[:, :DK // 2], d[:, DK // 2:]
    return jnp.concatenate([d1 * cos + d2 * sin, d2 * cos - d1 * sin], axis=1)


def _head_log_gamma(h):
    lg = jnp.float32(LOG_G[0])
    for hh in range(1, HEADS):
        lg = jnp.where(h == hh, jnp.float32(LOG_G[hh]), lg)
    return lg


def _ret_decays(h):
    lg = _head_log_gamma(h)
    n = lax.broadcasted_iota(jnp.int32, (CH, CH), 0)
    m = lax.broadcasted_iota(jnp.int32, (CH, CH), 1)
    diff = (n - m).astype(F32)
    decay = jnp.where(n >= m, jnp.exp(lg * jnp.maximum(diff, 0.0)), 0.0)
    idx = lax.broadcasted_iota(jnp.int32, (CH, 1), 0).astype(F32)
    xi = jnp.exp(lg * (idx + 1.0))
    zeta = jnp.exp(lg * (CH - 1.0 - idx))
    cd = jnp.exp(jnp.full((1, 1), lg * CH, F32))
    return decay, xi, zeta, cd


def _ret_chunk_fwd(q, k, v, cos, sin, s_prev_b, decay, xi, zeta):
    qr = _rot(q, cos, sin)
    kr = _rot(k, cos, sin) * (DK ** -0.5)
    scores = _mm_nt(qr, kr) * decay
    o = _mm(scores, v) + _mm(qr * xi, s_prev_b)
    local = _mm_tn(kr * zeta, v)
    mu = jnp.mean(o, axis=-1, keepdims=True)
    oc = o - mu
    rstd = lax.rsqrt(jnp.mean(oc * oc, axis=-1, keepdims=True) + EPS)
    return qr, kr, scores, local, oc * rstd, rstd


def _ret_fwd(p, cos, sin, gain):
    L = p.shape[0]
    nb = L // CH

    def body(q_ref, k_ref, v_ref, bz_ref, cos_ref, sin_ref, g_ref, yb_ref, st_ref, state):
        h = pl.program_id(0)

        @pl.when(pl.program_id(1) == 0)
        def _():
            state[...] = jnp.zeros_like(state)

        decay, xi, zeta, cd = _ret_decays(h)
        s_prev = state[...]
        s_prev_b = s_prev.astype(MXU)
        st_ref[...] = s_prev_b
        _, _, _, local, on, _ = _ret_chunk_fwd(q_ref[...], k_ref[...], v_ref[...], cos_ref[...], sin_ref[...],
                                               s_prev_b, decay, xi, zeta)
        state[...] = s_prev * cd + local
        act, _ = _silu_and_grad(bz_ref[...])
        yb_ref[...] = (on * g_ref[...] * act).astype(yb_ref.dtype)

    col = lambda base: pl.BlockSpec((CH, DK), lambda h, i: (i, base + h))
    rope = pl.BlockSpec((CH, DK // 2), lambda h, i: (i, 0))
    return pl.pallas_call(
        body, name="ret_fwd", grid=(HEADS, nb),
        in_specs=[col(8), col(12), col(16), col(20), rope, rope, pl.BlockSpec((1, DK), lambda h, i: (0, h))],
        out_specs=[pl.BlockSpec((CH, DK), lambda h, i: (i, h)),
                   pl.BlockSpec((None, None, DK, DK), lambda h, i: (h, i, 0, 0))],
        out_shape=[SDS((L, D), MXU), SDS((HEADS, nb, DK, DK), MXU)],
        scratch_shapes=[pltpu.VMEM((DK, DK), F32)],
        compiler_params=_cp(32, ARB2),
    )(p, p, p, p, cos, sin, gain)


def _ret_bwd(p, cos, sin, gain, states, dx1, wout_e):
    L = p.shape[0]
    nb = L // CH
    rev = lambda i: nb - 1 - i

    def body(q_ref, k_ref, v_ref, bz_ref, cos_ref, sin_ref, g_ref, st_ref, dx1_ref, wo_ref,
             dq_ref, dk_ref, dv_ref, dbz_ref, yb_ref, gg_ref, gstate):
        h = pl.program_id(0)

        @pl.when(pl.program_id(1) == 0)
        def _():
            gstate[...] = jnp.zeros_like(gstate)
            gg_ref[...] = jnp.zeros_like(gg_ref)

        decay, xi, zeta, cd = _ret_decays(h)
        cos, sin, v = cos_ref[...], sin_ref[...], v_ref[...]
        s_prev_b = st_ref[...]
        qr, kr, scores, _, on, rstd = _ret_chunk_fwd(q_ref[...], k_ref[...], v, cos, sin, s_prev_b, decay, xi, zeta)
        gain_h = g_ref[...]
        bz = bz_ref[...]
        act, dact = _silu_and_grad(bz)
        out = on * gain_h
        yb_ref[...] = (out * act).astype(yb_ref.dtype)
        dyb = _mm_nt(dx1_ref[...], wo_ref[...])
        dbz_ref[...] = (dyb * out * dact).astype(dbz_ref.dtype)
        dout = dyb * act
        gg_ref[...] += jnp.sum(dout * on, axis=0, keepdims=True)
        don = dout * gain_h
        do = rstd * (don - jnp.mean(don, axis=-1, keepdims=True) - on * jnp.mean(don * on, axis=-1, keepdims=True))
        gnext = gstate[...]
        gnext_b = gnext.astype(MXU)
        dscores = _mm_nt(do, v) * decay
        dv_ref[...] = (_mm_tn(scores, do) + _mm(kr * zeta, gnext_b)).astype(dv_ref.dtype)
        dqr = _mm(dscores, kr) + _mm_nt(do, s_prev_b) * xi
        dkr = _mm_tn(dscores, qr) + _mm_nt(v, gnext_b) * zeta
        gstate[...] = gnext * cd + _mm_tn(qr * xi, do)
        dq_ref[...] = _unrot(dqr, cos, sin).astype(dq_ref.dtype)
        dk_ref[...] = (_unrot(dkr, cos, sin) * (DK ** -0.5)).astype(dk_ref.dtype)

    col = lambda base: pl.BlockSpec((CH, DK), lambda h, i: (rev(i), base + h))
    rope = pl.BlockSpec((CH, DK // 2), lambda h, i: (rev(i), 0))
    outc = pl.BlockSpec((CH, DK), lambda h, i: (rev(i), h))
    act_out = SDS((L, D), MXU)
    return pl.pallas_call(
        body, name="ret_bwd", grid=(HEADS, nb),
        in_specs=[col(8), col(12), col(16), col(20), rope, rope, pl.BlockSpec((1, DK), lambda h, i: (0, h)),
                  pl.BlockSpec((None, None, DK, DK), lambda h, i: (h, rev(i), 0, 0)),
                  pl.BlockSpec((CH, D), lambda h, i: (rev(i), 0)),
                  pl.BlockSpec((DK, D), lambda h, i: (4 + h, 0))],
        out_specs=[outc, outc, outc, outc, outc, pl.BlockSpec((1, DK), lambda h, i: (0, h))],
        out_shape=[act_out, act_out, act_out, act_out, act_out, SDS((1, D), F32)],
        scratch_shapes=[pltpu.VMEM((DK, DK), F32)],
        compiler_params=_cp(32, ARB2),
    )(p, p, p, p, cos, sin, gain, states, dx1, wout_e)


def _out_even(x, ya, yb, wout):
    L = x.shape[0]
    tm = min(512, L)

    def body(x_ref, ya_ref, yb_ref, w_ref, o_ref):
        cat = jnp.concatenate([ya_ref[...], yb_ref[...]], axis=1)
        o_ref[...] = x_ref[...] + jnp.dot(cat, w_ref[...], preferred_element_type=F32)

    row = pl.BlockSpec((tm, D), lambda i: (i, 0))
    return pl.pallas_call(
        body, name="out_even", grid=(L // tm,), in_specs=[row, row, row, _full((DI, D))],
        out_specs=row, out_shape=SDS((L, D), F32), compiler_params=_cp(32, ("arbitrary",)),
    )(x, ya, yb, wout)


def _sgu_core(pv, gain, ws_ref, bs_ref):
    pu, pvv, z = pv[:, :DI], pv[:, DI:2 * DI], pv[:, 2 * DI:]
    u = _gelu(pu)
    v = _gelu(pvv)
    mu = jnp.mean(v, axis=-1, keepdims=True)
    vc = v - mu
    rstd = lax.rsqrt(jnp.mean(vc * vc, axis=-1, keepdims=True) + EPS)
    vhat = vc * rstd
    vn = vhat * gain
    t = lax.broadcasted_iota(jnp.int32, (CH, CH), 0)
    s_ = lax.broadcasted_iota(jnp.int32, (CH, CH), 1)
    mask = t >= s_
    wm = [jnp.where(mask, ws_ref[g], 0.0).astype(MXU) for g in range(SG)]
    s = jnp.concatenate([_mm(wm[g], vn[:, g * SGD:(g + 1) * SGD]) + bs_ref[g] for g in range(SG)], axis=1)
    return pu, pvv, z, u, vhat, rstd, vn, mask, wm, s


def _sgu_fwd(p2, x1, gain, wsp, bsp, wout, fnorm, tgt):
    L = p2.shape[0]

    def body(p_ref, x1_ref, g_ref, ws_ref, bs_ref, wo_ref, fn_ref, t_ref, dx2_ref, gfn_ref, loss_ref):
        @pl.when(pl.program_id(0) == 0)
        def _():
            gfn_ref[...] = jnp.zeros_like(gfn_ref)
            loss_ref[...] = jnp.zeros_like(loss_ref)

        _, _, z, u, _, _, _, _, _, s = _sgu_core(p_ref[...], g_ref[...], ws_ref, bs_ref)
        act, _ = _silu_and_grad(z)
        x2 = x1_ref[...] + _mm(u * s * act, wo_ref[...])
        xhat, r = _rms(x2)
        fn = fn_ref[...]
        e = xhat * fn - t_ref[...]
        loss_ref[...] += 0.5 * jnp.sum(jnp.mean(e * e, axis=-1, keepdims=True), axis=0, keepdims=True)
        do = e * (1.0 / D)
        gfn_ref[...] += jnp.sum(do * xhat, axis=0, keepdims=True)
        dxhat = do * fn
        dx2_ref[...] = r * (dxhat - xhat * jnp.mean(dxhat * xhat, axis=-1, keepdims=True))

    row = pl.BlockSpec((CH, D), lambda i: (i, 0))
    return pl.pallas_call(
        body, name="sgu_fwd", grid=(L // CH,),
        in_specs=[pl.BlockSpec((CH, NIN), lambda i: (i, 0)), row, _full((1, DI)), _full((SG, CH, CH)),
                  _full((SG, CH, 1)), _full((DI, D)), _full((1, D)), row],
        out_specs=[row, _full((1, D)), _full((1, 1))],
        out_shape=[SDS((L, D), F32), SDS((1, D), F32), SDS((1, 1), F32)],
        compiler_params=_cp(48, ("arbitrary",)),
    )(p2, x1, gain, wsp, bsp, wout, fnorm, tgt)


def _sgu_bwd(p2, dx2, gain, wsp, bsp, wout):
    L = p2.shape[0]

    def body(p_ref, dx2_ref, g_ref, ws_ref, bs_ref, wo_ref, dp_ref, y_ref, gg_ref, gws_ref, gbs_ref):
        @pl.when(pl.program_id(0) == 0)
        def _():
            gg_ref[...] = jnp.zeros_like(gg_ref)
            gws_ref[...] = jnp.zeros_like(gws_ref)
            gbs_ref[...] = jnp.zeros_like(gbs_ref)

        gain = g_ref[...]
        pu, pvv, z, u, vhat, rstd, vn, mask, wm, s = _sgu_core(p_ref[...], gain, ws_ref, bs_ref)
        act, dact = _silu_and_grad(z)
        y_ref[...] = (u * s * act).astype(y_ref.dtype)
        dy = _mm_nt(dx2_ref[...], wo_ref[...])
        du = dy * s * act
        ds = dy * u * act
        dz = dy * u * s * dact
        dvn = []
        for g in range(SG):
            ds_g = ds[:, g * SGD:(g + 1) * SGD]
            vn_g = vn[:, g * SGD:(g + 1) * SGD]
            gbs_ref[g] += jnp.sum(ds_g, axis=1, keepdims=True)
            gws_ref[g] += jnp.where(mask, _mm_nt(ds_g, vn_g), 0.0)
            dvn.append(_mm_tn(wm[g], ds_g))
        dvn = jnp.concatenate(dvn, axis=1)
        gg_ref[...] += jnp.sum(dvn * vhat, axis=0, keepdims=True)
        dvhat = dvn * gain
        dv = rstd * (dvhat - jnp.mean(dvhat, axis=-1, keepdims=True)
                     - vhat * jnp.mean(dvhat * vhat, axis=-1, keepdims=True))
        dp_ref[...] = jnp.concatenate([du * _gelu_grad(pu), dv * _gelu_grad(pvv), dz], axis=1).astype(dp_ref.dtype)

    return pl.pallas_call(
        body, name="sgu_bwd", grid=(L // CH,),
        in_specs=[pl.BlockSpec((CH, NIN), lambda i: (i, 0)), pl.BlockSpec((CH, D), lambda i: (i, 0)),
                  _full((1, DI)), _full((SG, CH, CH)), _full((SG, CH, 1)), _full((DI, D))],
        out_specs=[pl.BlockSpec((CH, NIN), lambda i: (i, 0)), pl.BlockSpec((CH, DI), lambda i: (i, 0)),
                   _full((1, DI)), _full((SG, CH, CH)), _full((SG, CH, 1))],
        out_shape=[SDS((L, NIN), MXU), SDS((L, DI), MXU), SDS((1, DI), F32), SDS((SG, CH, CH), F32),
                   SDS((SG, CH, 1), F32)],
        compiler_params=_cp(48, ("arbitrary",)),
    )(p2, dx2, gain, wsp, bsp, wout)


def _adamw(w, m, v, gparts, name):
    R, C = w.shape
    tr = R
    for cand in (256, 128, 64, 32, 16, 8):
        if R % cand == 0 and R > cand:
            tr = cand
            break

    def body(w_ref, m_ref, v_ref, gp_ref, g_ref, d_ref, mo_ref, vo_ref):
        g = gp_ref[0].astype(F32)
        for s in range(1, NDEV):
            g = g + gp_ref[s].astype(F32)
        w_ = w_ref[...]
        mn = ADAM_B1 * m_ref[...] + (1.0 - ADAM_B1) * g
        vn = ADAM_B2 * v_ref[...] + (1.0 - ADAM_B2) * (g * g)
        mhat = mn / BC1
        vhat = vn / BC2
        g_ref[...] = g
        d_ref[...] = -ADAM_LR * (mhat / (jnp.sqrt(vhat) + ADAM_EPS) + ADAM_WD * w_)
        mo_ref[...] = mn
        vo_ref[...] = vn

    blk = pl.BlockSpec((tr, C), lambda i: (i, 0))
    out = SDS((R, C), F32)
    return pl.pallas_call(
        body, name=name, grid=(R // tr,),
        in_specs=[blk, blk, blk, pl.BlockSpec((NDEV, tr, C), lambda i: (0, i, 0))],
        out_specs=[blk, blk, blk, blk], out_shape=[out, out, out, out],
        compiler_params=_cp(40, ("arbitrary",)),
    )(w, m, v, gparts)


MESH = pl.DeviceIdType.MESH
HBM_SPEC = pl.BlockSpec(memory_space=pltpu.HBM)
SEM_SPEC = pl.BlockSpec(memory_space=pltpu.SEMAPHORE)
EFFECT = pltpu.SideEffectType.DATAFLOW_SIDE_EFFECTING


def _me_and_peers():
    x, y, c = lax.axis_index("x"), lax.axis_index("y"), lax.axis_index("c")
    me = 4 * x + 2 * y + c
    peers = []
    for r in range(1, NDEV):
        px, py, pc = x ^ ((r >> 2) & 1), y ^ ((r >> 1) & 1), c ^ (r & 1)
        peers.append(((px, py, pc), 4 * px + 2 * py + pc))
    return me, peers


def _land_shape(a, scatter):
    return (NDEV,) + (a.shape[1:] if scatter else a.shape)


def _remote(src, dst, send_sems, recv_sems, r, k, n, dev):
    i = r * n + k
    return pltpu.make_async_remote_copy(src_ref=src, dst_ref=dst, send_sem=send_sems.at[i], recv_sem=recv_sems.at[i],
                                        device_id=dev, device_id_type=MESH)


def _exchange(arrays, scatter, name):
    n = len(arrays)
    out_shape = [SDS(_land_shape(a, scatter), a.dtype) for a in arrays]

    def body(*refs):
        ins, outs = refs[:n], refs[n:2 * n]
        send_sems, recv_sems, loc_sems = refs[2 * n:]
        me, peers = _me_and_peers()
        local = []
        for k in range(n):
            src = ins[k].at[me] if scatter else ins[k]
            cp = pltpu.make_async_copy(src, outs[k].at[me], loc_sems.at[k])
            cp.start()
            local.append(cp)
        sends = []
        for r, (dev, lin) in enumerate(peers):
            for k in range(n):
                src = ins[k].at[lin] if scatter else ins[k]
                cp = _remote(src, outs[k].at[me], send_sems, recv_sems, r, k, n, dev)
                cp.start()
                sends.append(cp)
        for r, (dev, lin) in enumerate(peers):
            for k in range(n):
                src = ins[k].at[me] if scatter else ins[k]
                _remote(src, outs[k].at[lin], send_sems, recv_sems, r, k, n, dev).wait_recv()
        for cp in sends:
            cp.wait_send()
        for cp in local:
            cp.wait()

    return pl.pallas_call(
        body, name=name, in_specs=[HBM_SPEC] * n, out_specs=[HBM_SPEC] * n, out_shape=out_shape,
        scratch_shapes=[pltpu.SemaphoreType.DMA(((NDEV - 1) * n,)), pltpu.SemaphoreType.DMA(((NDEV - 1) * n,)),
                        pltpu.SemaphoreType.DMA((n,))],
    )(*arrays)


def _exchange_start(arrays, scatter, name):
    n = len(arrays)
    lands = [lax.empty(_land_shape(a, scatter), a.dtype) for a in arrays]

    def body(*refs):
        ins, lnd = refs[:n], refs[n:2 * n]
        send_sems, recv_sems = refs[2 * n], refs[2 * n + 1]
        token = refs[-1]
        me, peers = _me_and_peers()
        for r, (dev, lin) in enumerate(peers):
            for k in range(n):
                src = ins[k].at[lin] if scatter else ins[k]
                _remote(src, lnd[k].at[me], send_sems, recv_sems, r, k, n, dev).start()
        token[...] = jnp.zeros_like(token)

    sem = pltpu.SemaphoreType.DMA(((NDEV - 1) * n,))
    outs = pl.pallas_call(
        body, name=name,
        out_shape=(sem, sem, *[pltpu.HBM(a.shape, a.dtype) for a in arrays],
                   *[pltpu.HBM(l.shape, l.dtype) for l in lands], SDS((8, 128), F32)),
        in_specs=[HBM_SPEC] * (2 * n),
        out_specs=(SEM_SPEC, SEM_SPEC, *[HBM_SPEC] * (2 * n), pl.BlockSpec(memory_space=pltpu.VMEM)),
        input_output_aliases={k: 2 + k for k in range(2 * n)},
        compiler_params=pltpu.CompilerParams(has_side_effects=EFFECT),
    )(*[pltpu.with_memory_space_constraint(a, pltpu.HBM) for a in arrays],
      *[pltpu.with_memory_space_constraint(l, pltpu.HBM) for l in lands])
    return (n, scatter, outs[0], outs[1], outs[2:2 + n], outs[2 + n:2 + 2 * n]), outs[-1]


def _exchange_wait(handle, after, name):
    n, scatter, send_sems, recv_sems, thru, lands = handle
    after = tuple(after)

    def body(*refs):
        ins, lnd = refs[:n], refs[n:2 * n]
        send_sems, recv_sems = refs[2 * n], refs[2 * n + 1]
        me, peers = _me_and_peers()
        for r, (dev, lin) in enumerate(peers):
            for k in range(n):
                src = ins[k].at[lin] if scatter else ins[k]
                cp = _remote(src, lnd[k].at[lin], send_sems, recv_sems, r, k, n, dev)
                cp.wait_send()
                cp.wait_recv()

    outs = pl.pallas_call(
        body, name=name,
        out_shape=(*[pltpu.HBM(a.shape, a.dtype) for a in thru], *[pltpu.HBM(l.shape, l.dtype) for l in lands]),
        in_specs=[HBM_SPEC] * (2 * n) + [SEM_SPEC, SEM_SPEC] + [ANY_SPEC] * len(after),
        out_specs=tuple([HBM_SPEC] * (2 * n)),
        input_output_aliases={k: k for k in range(2 * n)},
        compiler_params=pltpu.CompilerParams(has_side_effects=EFFECT),
    )(*thru, *lands, send_sems, recv_sems, *after)
    return list(outs[n:])


def _with_own(land, own, me):
    return lax.dynamic_update_slice_in_dim(land, own[None], me, axis=0)


def _local_step(x, tgt, norm_even, win_e, lam_re, lam_im, log_dt, b_re, b_im, c_re, c_im, s5_d, bglu,
                ret_gain, wsp, bsp, fnorm, late_weights, emit, start_token=None):
    L = x.shape[0]
    lr3, li3 = lam_re.reshape(G, 1, P), lam_im.reshape(G, 1, P)
    dt3 = log_dt.reshape(G, 1, 1)
    br3, bi3 = jnp.swapaxes(b_re, 1, 2), jnp.swapaxes(b_im, 1, 2)
    abr3, abi3, bbr3, bbi3 = _s5_disc(lr3, li3, dt3, br3, bi3)
    bb = jnp.concatenate([_embed(bbr3), _embed(bbi3)], axis=2).astype(MXU)
    cm = jnp.concatenate([_embed(jnp.swapaxes(c_re, 1, 2)), -_embed(jnp.swapaxes(c_im, 1, 2))], axis=1).astype(MXU)
    abr, abi = abr3.reshape(1, NSTATE), abi3.reshape(1, NSTATE)
    pwr, pwi, _, _ = _s5_tables(abr, abi, NT, "s5_tables_step")
    par, pai, pbr, pbi = _s5_tables(pwr[NT - 1:NT], pwi[NT - 1:NT], CH, "s5_tables_chunk")
    inv = (ROPE_BASE ** (-jnp.arange(DK // 2, dtype=F32) / (DK // 2))).reshape(1, DK // 2)
    cos, sin = _rope_tables(L, inv)
    bsp3 = bsp.reshape(SG, CH, 1)

    def dep(token):
        return NO_DEPS if token is None else (token,)

    p = _in_proj(x, norm_even, win_e, "in_even", dep(start_token))
    ypre, s5_states = _s5_scan_fwd(p, bb, cm, abr, abi, par, pai, s5_d)
    yb, ret_states = _ret_fwd(p, cos, sin, ret_gain)
    wglu, wout_e, norm_odd, win_o, sgu_gain, wout_o = late_weights((ypre, yb))
    ya = _s5_gate_fwd(ypre, p, wglu, bglu)
    x1 = _out_even(x, ya, yb, wout_e)
    p2 = _in_proj(x1, norm_odd, win_o, "in_odd")
    dx2, g_fnorm, loss = _sgu_fwd(p2, x1, sgu_gain, wsp, bsp3, wout_o, fnorm, tgt)

    dp2, y_o, g_sgu_gain, g_wsp, g_bsp = _sgu_bwd(p2, dx2, sgu_gain, wsp, bsp3, wout_o)
    g_wout_o = _wgrad_rows([y_o], dx2, "wgrad_out_odd")
    g_win_o = _wgrad_cols(x1, norm_odd, dp2, "wgrad_in_odd")
    tok = emit("odd", dict(w_in_odd=g_win_o, w_out_odd=g_wout_o))
    dx1, g_norm_odd = _in_proj_bwd_x(dp2, x1, norm_odd, win_o, dx2, "in_odd_bwd", dep(tok))

    dypre, daz, yg, dt, ya2, g_bglu = _s5_gate_bwd(ypre, p, dx1, wout_e, wglu, bglu)
    du, g_d, g_cm, g_bb, g_ar, g_ai = _s5_scan_bwd(p, dypre, s5_states, bb, cm, abr, abi, par, pai, pbr, pbi, s5_d)
    dbbr3 = _diag_blocks(g_bb[:, :, :SW], HG, P)
    dbbi3 = _diag_blocks(g_bb[:, :, SW:], HG, P)
    g_c_re = jnp.swapaxes(_diag_blocks(g_cm[:, :SW, :], P, HG), 1, 2)
    g_c_im = -jnp.swapaxes(_diag_blocks(g_cm[:, SW:, :], P, HG), 1, 2)
    g_lr3, g_li3, g_dt3, g_br3, g_bi3 = _s5_disc_bwd(
        lr3, li3, dt3, br3, bi3, g_ar.reshape(G, 1, P), g_ai.reshape(G, 1, P), dbbr3, dbbi3)
    dq, dk, dv, dbz, yb2, g_ret_gain = _ret_bwd(p, cos, sin, ret_gain, ret_states, dx1, wout_e)
    small = dict(
        s5_lam_re=g_lr3.reshape(G, P), s5_lam_im=g_li3.reshape(G, P),
        s5_log_dt=g_dt3.reshape(1, G), s5_b_re=jnp.swapaxes(g_br3, 1, 2), s5_b_im=jnp.swapaxes(g_bi3, 1, 2),
        s5_c_re=g_c_re, s5_c_im=g_c_im, s5_d=g_d, s5_b_glu=g_bglu, ret_gn_gain=g_ret_gain,
        norm_odd=g_norm_odd, sgu_norm_gain=g_sgu_gain, sgu_w_spatial=g_wsp, sgu_b_spatial=g_bsp.reshape(SG, CH),
        final_norm=g_fnorm)
    tok = emit("small", small)
    g_wout_e = _wgrad_rows([ya2, yb2], dx1, "wgrad_out_even", dep(tok))
    g_wglu = _wgrad_rows([yg], dt, "wgrad_glu", dep(tok))
    tok = emit("even_rows", dict(s5_w_glu=g_wglu, w_out_even=g_wout_e))
    dp = jnp.concatenate([du.astype(MXU), daz, dq, dk, dv, dbz], axis=1)
    g_win_e = _wgrad_cols(x, norm_even, dp, "wgrad_in_even", dep(tok))
    tok = emit("even_cols", dict(w_in_even=g_win_e))
    dx, g_norm_even = _in_proj_bwd_x(dp, x, norm_even, win_e, dx1, "in_even_bwd", dep(tok))
    emit("last", dict(norm_even=g_norm_even))
    return loss, dx


WEIGHTS = ['norm_even', 'w_in_even', 's5_lam_re', 's5_lam_im', 's5_log_dt', 's5_b_re', 's5_b_im', 's5_c_re',
           's5_c_im', 's5_d', 's5_w_glu', 's5_b_glu', 'ret_gn_gain', 'w_out_even', 'norm_odd', 'w_in_odd',
           'sgu_norm_gain', 'sgu_w_spatial', 'sgu_b_spatial', 'w_out_odd', 'final_norm']
BIG = ['w_in_even', 's5_w_glu', 'w_out_even', 'w_in_odd', 'w_out_odd']
SHARDED_SMALL = {'norm_odd': D // NDEV, 'sgu_norm_gain': DI // NDEV}
SMALL = [n for n in WEIGHTS if n not in BIG and n != 'norm_even']
LANES = 128


def kernel(x, norm_even, w_in_even, s5_lam_re, s5_lam_im, s5_log_dt, s5_b_re, s5_b_im, s5_c_re, s5_c_im, s5_d, s5_w_glu, s5_b_glu, ret_gn_gain, w_out_even, norm_odd, w_in_odd, sgu_norm_gain, sgu_w_spatial, sgu_b_spatial, w_out_odd, final_norm, loss_target, m_norm_even, m_w_in_even, m_s5_lam_re, m_s5_lam_im, m_s5_log_dt, m_s5_b_re, m_s5_b_im, m_s5_c_re, m_s5_c_im, m_s5_d, m_s5_w_glu, m_s5_b_glu, m_ret_gn_gain, m_w_out_even, m_norm_odd, m_w_in_odd, m_sgu_norm_gain, m_sgu_w_spatial, m_sgu_b_spatial, m_w_out_odd, m_final_norm, v_norm_even, v_w_in_even, v_s5_lam_re, v_s5_lam_im, v_s5_log_dt, v_s5_b_re, v_s5_b_im, v_s5_c_re, v_s5_c_im, v_s5_d, v_s5_w_glu, v_s5_b_glu, v_ret_gn_gain, v_w_out_even, v_norm_odd, v_w_in_odd, v_sgu_norm_gain, v_sgu_w_spatial, v_sgu_b_spatial, v_w_out_odd, v_final_norm):
    args = dict(locals())
    w = {n: args[n] for n in WEIGHTS}
    m = {n: args["m_" + n] for n in WEIGHTS}
    v = {n: args["v_" + n] for n in WEIGHTS}
    me = 4 * lax.axis_index("x") + 2 * lax.axis_index("y") + lax.axis_index("c")

    (win_e,) = _exchange([w['w_in_even'][0].astype(MXU)], False, "gather_w_in_even")
    late_own = [w['s5_w_glu'][0].astype(MXU), w['w_out_even'][0].astype(MXU), w['norm_odd'],
                w['w_in_odd'][0].astype(MXU), w['sgu_norm_gain'], w['w_out_odd'][0].astype(MXU)]
    late_handle, start_token = _exchange_start(late_own, False, "gather_late_start")

    def late_weights(after):
        lands = _exchange_wait(late_handle, after, "gather_late_wait")
        wglu, wout_e, nodd, win_o, sgug, wout_o = (_with_own(l, o, me) for l, o in zip(lands, late_own))
        return (wglu.reshape(D, D), wout_e.reshape(DI, D), nodd.reshape(1, D), win_o, sgug.reshape(1, DI),
                wout_o.reshape(DI, D))

    pending = {}
    small_last = {}

    def emit(stage, grads):
        if stage == "last":
            small_last.update(grads)
            return None
        if stage == "small":
            flat = jnp.concatenate([grads[n].reshape(-1) for n in SMALL])
            flat = jnp.pad(flat, (0, (-flat.shape[0]) % (8 * LANES))).reshape(1, -1)
            handle, token = _exchange_start([flat], False, "small_start")
            pending[stage] = (handle, ["flat"], [flat], {n: grads[n].size for n in SMALL})
            return token
        names = list(grads)
        handle, token = _exchange_start([grads[n] for n in names], True, stage + "_start")
        pending[stage] = (handle, names, [lax.dynamic_index_in_dim(grads[n], me, 0, keepdims=False) for n in names])
        return token

    loss, dx = _local_step(
        x[0], loss_target[0], w['norm_even'], win_e, w['s5_lam_re'][0], w['s5_lam_im'][0], w['s5_log_dt'][0],
        w['s5_b_re'][0], w['s5_b_im'][0], w['s5_c_re'][0], w['s5_c_im'][0], w['s5_d'], w['s5_b_glu'],
        w['ret_gn_gain'], w['sgu_w_spatial'][0], w['sgu_b_spatial'][0], w['final_norm'].reshape(1, D),
        late_weights, emit, start_token)

    out_g, out_d, out_m, out_v = {}, {}, {}, {}
    after = dx
    for stage in ("odd", "even_rows", "even_cols"):
        handle, names, own = pending[stage][:3]
        lands = _exchange_wait(handle, (after,), stage + "_wait")
        for n, land, o in zip(names, lands, own):
            shp = w[n].shape
            r, c = shp[1], shp[2]
            res = _adamw(w[n].reshape(r, c), m[n].reshape(r, c), v[n].reshape(r, c), _with_own(land, o, me),
                         "adamw_" + n)
            out_g[n], out_d[n], out_m[n], out_v[n] = (t.reshape(shp) for t in res)
            after = res[0]

    handle, _, own, sizes = pending["small"]
    (gsmall,) = _exchange_wait(handle, (after,), "small_wait")
    gsmall = _with_own(gsmall, own[0], me).reshape(NDEV, -1)
    pieces, off = [], 0
    for n in SMALL:
        seg = gsmall[:, off:off + sizes[n]]
        if n in SHARDED_SMALL:
            seg = lax.dynamic_slice_in_dim(seg, me * SHARDED_SMALL[n], SHARDED_SMALL[n], axis=1)
        pieces.append(seg)
        off += sizes[n]
    gp = jnp.concatenate(pieces, axis=1)
    wpad = (-gp.shape[1]) % (256 * LANES)

    def pack(d):
        t = jnp.concatenate([d[n].reshape(-1) for n in SMALL])
        return jnp.pad(t, (0, wpad)).reshape(-1, LANES)

    gp = jnp.pad(gp, ((0, 0), (0, wpad))).reshape(NDEV, -1, LANES)
    res = _adamw(pack(w), pack(m), pack(v), gp, "adamw_small")
    off = 0
    for n in SMALL:
        size = w[n].size
        for dst, src in zip((out_g, out_d, out_m, out_v), res):
            dst[n] = src.reshape(-1)[off:off + size].reshape(w[n].shape)
        off += size
    (gne,) = _exchange([small_last['norm_even']], False, "gather_norm_even")
    n = 'norm_even'
    res = _adamw(w[n].reshape(8, LANES), m[n].reshape(8, LANES), v[n].reshape(8, LANES),
                 gne.reshape(NDEV, 8, LANES), "adamw_norm_even")
    out_g[n], out_d[n], out_m[n], out_v[n] = (t.reshape(w[n].shape) for t in res)

    loss_total = lax.psum(loss[0, 0], AXES)
    return (loss_total, dx[None], *[out_g[n] for n in WEIGHTS], *[out_d[n] for n in WEIGHTS],
            *[out_m[n] for n in WEIGHTS], *[out_v[n] for n in WEIGHTS])
```

```python
import math

import jax
import jax.numpy as jnp
from jax import lax
from jax.experimental import pallas as pl
from jax.experimental.pallas import tpu as pltpu

F32 = jnp.float32
MXU = jnp.bfloat16
AXES = ("x", "y", "c")
NDEV = 8
D = 1024
NIN = 6144
WIN_BLK = NIN // NDEV
DI = 2048
G, P, HG = 64, 64, 16
GB = 8
NJ = G // GB
SW = GB * P
UW = GB * HG
NSTATE = G * P
HEADS, DK = 4, 256
CH = 128
SG, SGD = 4, 512
EPS = 1e-6
ROPE_BASE = 10000.0
VMEM_CAP_V7X = 64 * 1024 * 1024
LOG_G = [math.log1p(-2.0 ** (-5.0 - h)) for h in range(HEADS)]
GELU_C = math.sqrt(2.0 / math.pi)

ADAM_LR, ADAM_B1, ADAM_B2, ADAM_EPS, ADAM_WD, ADAM_STEP = 0.001, 0.9, 0.999, 1e-08, 0.01, 10
BC1 = 1.0 - ADAM_B1 ** ADAM_STEP
BC2 = 1.0 - ADAM_B2 ** ADAM_STEP

SDS = jax.ShapeDtypeStruct
ARB2 = ("arbitrary", "arbitrary")


def _cp(vmem_mib, sem=None):
    kw = dict(vmem_limit_bytes=min(vmem_mib * 1024 * 1024, VMEM_CAP_V7X - 4 * 1024 * 1024))
    if sem is not None:
        kw["dimension_semantics"] = sem
    return pltpu.CompilerParams(**kw)


def _mm(a, b):
    return jnp.dot(a.astype(MXU), b.astype(MXU), preferred_element_type=F32)


def _mm_nt(a, b):
    return lax.dot_general(a.astype(MXU), b.astype(MXU), (((1,), (1,)), ((), ())), preferred_element_type=F32)


def _mm_tn(a, b):
    return lax.dot_general(a.astype(MXU), b.astype(MXU), (((0,), (0,)), ((), ())), preferred_element_type=F32)


def _gelu(x):
    return _gelu_and_grad(x)[0]


def _gelu_and_grad(x):
    x2 = x * x
    th = jnp.tanh(GELU_C * x * (1.0 + 0.044715 * x2))
    hp = 0.5 * (1.0 + th)
    return x * hp, hp + 0.5 * x * (1.0 - th * th) * GELU_C * (1.0 + 3.0 * 0.044715 * x2)


def _silu_and_grad(x):
    s = jax.nn.sigmoid(x)
    return x * s, s * (1.0 + x * (1.0 - s))


def _full(shape):
    nd = len(shape)
    return pl.BlockSpec(shape, lambda *_: (0,) * nd)


def _rms(xf):
    r = lax.rsqrt(jnp.mean(xf * xf, axis=-1, keepdims=True) + EPS)
    return xf * r, r


ANY_SPEC = pl.BlockSpec(memory_space=pl.ANY)
NO_DEPS = ()


def _in_proj(x, gain, wst, name, deps=NO_DEPS):
    L = x.shape[0]
    tm = min(512, L)

    def body(x_ref, g_ref, w_ref, *rest):
        o_ref, h_scr = rest[len(deps):]

        @pl.when(pl.program_id(1) == 0)
        def _():
            xhat, _ = _rms(x_ref[...])
            h_scr[...] = (xhat * g_ref[...]).astype(MXU)

        o_ref[...] = jnp.dot(h_scr[...], w_ref[...], preferred_element_type=F32)

    return pl.pallas_call(
        body, name=name, grid=(L // tm, NDEV),
        in_specs=[pl.BlockSpec((tm, D), lambda i, n: (i, 0)), _full((1, D)),
                  pl.BlockSpec((None, D, WIN_BLK), lambda i, n: (n, 0, 0))] + [ANY_SPEC] * len(deps),
        out_specs=pl.BlockSpec((tm, WIN_BLK), lambda i, n: (i, n)),
        out_shape=SDS((L, NIN), F32),
        scratch_shapes=[pltpu.VMEM((tm, D), MXU)],
        compiler_params=_cp(32, ARB2),
    )(x, gain, wst, *deps)


def _in_proj_bwd_x(dp, x, gain, wst, dres, name, deps=NO_DEPS):
    L = x.shape[0]
    tm = min(512, L)

    def body(dp_ref, x_ref, g_ref, w_ref, dres_ref, *rest):
        dx_ref, gg_ref, acc = rest[len(deps):]
        i, n = pl.program_id(0), pl.program_id(1)

        @pl.when(n == 0)
        def _():
            acc[...] = jnp.zeros_like(acc)

        @pl.when((i == 0) & (n == 0))
        def _():
            gg_ref[...] = jnp.zeros_like(gg_ref)

        acc[...] += _mm_nt(dp_ref[...], w_ref[...])

        @pl.when(n == NDEV - 1)
        def _():
            xhat, r = _rms(x_ref[...])
            dh = acc[...]
            dxhat = dh * g_ref[...]
            dx_ref[...] = dres_ref[...] + r * (dxhat - xhat * jnp.mean(dxhat * xhat, axis=-1, keepdims=True))
            gg_ref[...] += jnp.sum(dh * xhat, axis=0, keepdims=True)

    return pl.pallas_call(
        body, name=name, grid=(L // tm, NDEV),
        in_specs=[pl.BlockSpec((tm, WIN_BLK), lambda i, n: (i, n)),
                  pl.BlockSpec((tm, D), lambda i, n: (i, 0)), _full((1, D)),
                  pl.BlockSpec((None, D, WIN_BLK), lambda i, n: (n, 0, 0)),
                  pl.BlockSpec((tm, D), lambda i, n: (i, 0))] + [ANY_SPEC] * len(deps),
        out_specs=[pl.BlockSpec((tm, D), lambda i, n: (i, 0)), _full((1, D))],
        out_shape=[SDS((L, D), F32), SDS((1, D), F32)],
        scratch_shapes=[pltpu.VMEM((tm, D), F32)],
        compiler_params=_cp(40, ARB2),
    )(dp, x, gain, wst, dres, *deps)


def _wgrad_cols(x, gain, dp, name, deps=NO_DEPS):
    L = x.shape[0]
    tk = min(512, L)
    nk = L // tk
    halves = 2
    nh = NDEV // halves

    def body(x_ref, g_ref, dp_ref, *rest):
        o_ref, acc = rest[len(deps):]
        k = pl.program_id(1)

        @pl.when(k == 0)
        def _():
            acc[...] = jnp.zeros_like(acc)

        xhat, _ = _rms(x_ref[...])
        acc[...] += _mm_tn(xhat * g_ref[...], dp_ref[...])

        @pl.when(k == nk - 1)
        def _():
            for c in range(nh):
                o_ref[c] = acc[:, c * WIN_BLK:(c + 1) * WIN_BLK].astype(o_ref.dtype)

    return pl.pallas_call(
        body, name=name, grid=(halves, nk),
        in_specs=[pl.BlockSpec((tk, D), lambda n, k: (k, 0)), _full((1, D)),
                  pl.BlockSpec((tk, nh * WIN_BLK), lambda n, k: (k, n))] + [ANY_SPEC] * len(deps),
        out_specs=pl.BlockSpec((nh, D, WIN_BLK), lambda n, k: (n, 0, 0)),
        out_shape=SDS((NDEV, D, WIN_BLK), MXU),
        scratch_shapes=[pltpu.VMEM((D, nh * WIN_BLK), F32)],
        compiler_params=_cp(56, ARB2),
    )(x, gain, dp, *deps)


def _wgrad_rows(a_parts, b, name, deps=NO_DEPS):
    L, N = b.shape
    na = len(a_parts)
    widths = [a.shape[1] for a in a_parts]
    M = sum(widths)
    tk = min(512, L)
    nk = L // tk

    def body(*refs):
        a_refs, b_ref = refs[:na], refs[na]
        o_ref, acc = refs[na + 1 + len(deps):]
        k = pl.program_id(0)

        @pl.when(k == 0)
        def _():
            acc[...] = jnp.zeros_like(acc)

        bv = b_ref[...].astype(MXU)
        off = 0
        for a_ref, wd in zip(a_refs, widths):
            acc[off:off + wd, :] += _mm_tn(a_ref[...], bv)
            off += wd

        @pl.when(k == nk - 1)
        def _():
            o_ref[...] = acc[...].astype(o_ref.dtype).reshape(o_ref.shape)

    return pl.pallas_call(
        body, name=name, grid=(nk,),
        in_specs=[pl.BlockSpec((tk, wd), lambda k: (k, 0)) for wd in widths]
        + [pl.BlockSpec((tk, N), lambda k: (k, 0))] + [ANY_SPEC] * len(deps),
        out_specs=_full((NDEV, M // NDEV, N)),
        out_shape=SDS((NDEV, M // NDEV, N), MXU),
        scratch_shapes=[pltpu.VMEM((M, N), F32)],
        compiler_params=_cp(48, ("arbitrary",)),
    )(*a_parts, b, *deps)


def _s5_disc_fn(lr_raw, li, logdt, br, bi):
    lr = jnp.minimum(lr_raw, -1e-4)
    dt = jnp.exp(logdt)
    mag = jnp.exp(lr * dt)
    abr = mag * jnp.cos(li * dt)
    abi = mag * jnp.sin(li * dt)
    den = lr * lr + li * li
    nre = abr - 1.0
    nim = abi
    zr = (nre * lr + nim * li) / den
    zi = (nim * lr - nre * li) / den
    return abr, abi, zr * br - zi * bi, zr * bi + zi * br


def _s5_disc(lr, li, logdt, br, bi):
    def body(lr_ref, li_ref, dt_ref, br_ref, bi_ref, abr_ref, abi_ref, bbr_ref, bbi_ref):
        abr, abi, bbr, bbi = _s5_disc_fn(lr_ref[...], li_ref[...], dt_ref[...], br_ref[...], bi_ref[...])
        abr_ref[...] = abr
        abi_ref[...] = abi
        bbr_ref[...] = bbr
        bbi_ref[...] = bbi

    s1, s3 = SDS((G, 1, P), F32), SDS((G, HG, P), F32)
    return pl.pallas_call(body, name="s5_disc", out_shape=[s1, s1, s3, s3])(lr, li, logdt, br, bi)


def _s5_disc_bwd(lr, li, logdt, br, bi, dabr, dabi, dbbr, dbbi):
    def body(lr_ref, li_ref, dt_ref, br_ref, bi_ref, c0, c1, c2, c3, o0, o1, o2, o3, o4):
        _, vjp = jax.vjp(_s5_disc_fn, lr_ref[...], li_ref[...], dt_ref[...], br_ref[...], bi_ref[...])
        g = vjp((c0[...], c1[...], c2[...], c3[...]))
        for o, v in zip((o0, o1, o2, o3, o4), g):
            o[...] = v

    s1, s3 = SDS((G, 1, P), F32), SDS((G, HG, P), F32)
    return pl.pallas_call(body, name="s5_disc_bwd", out_shape=[s1, s1, SDS((G, 1, 1), F32), s3, s3])(
        lr, li, logdt, br, bi, dabr, dabi, dbbr, dbbi)


def _s5_tables(abr, abi, rows, name):
    def body(ar_ref, ai_ref, pfr, pfi, pbr, pbi):
        pfr[0:1, :] = ar_ref[...]
        pfi[0:1, :] = ai_ref[...]
        pbr[rows - 1:rows, :] = ar_ref[...]
        pbi[rows - 1:rows, :] = ai_ref[...]
        n = 1
        while n < rows:
            er, ei = pfr[n - 1:n, :], pfi[n - 1:n, :]
            xr, xi = pfr[0:n, :], pfi[0:n, :]
            pfr[n:2 * n, :] = er * xr - ei * xi
            pfi[n:2 * n, :] = er * xi + ei * xr
            yr, yi = pbr[rows - n:rows, :], pbi[rows - n:rows, :]
            pbr[rows - 2 * n:rows - n, :] = er * yr - ei * yi
            pbi[rows - 2 * n:rows - n, :] = er * yi + ei * yr
            n *= 2

    s = SDS((rows, NSTATE), F32)
    return pl.pallas_call(body, name=name, out_shape=[s, s, s, s], compiler_params=_cp(40))(abr, abi)


def _cscan(br, bi, pr_ref, pi_ref, reverse):
    T = br.shape[0]
    sign = -1.0 if reverse else 1.0
    row = lax.broadcasted_iota(jnp.int32, br.shape, 0)
    k = 1
    while k < T:
        akr = pr_ref[k - 1:k, :]
        aki = sign * pi_ref[k - 1:k, :]

        def shift(v):
            if k % 8 == 0:
                z = jnp.zeros((k, v.shape[1]), v.dtype)
                return jnp.concatenate([v[k:], z], 0) if reverse else jnp.concatenate([z, v[:T - k]], 0)
            if reverse:
                return jnp.where(row < T - k, pltpu.roll(v, T - k, 0), 0.0)
            return jnp.where(row >= k, pltpu.roll(v, k, 0), 0.0)

        sr, si = shift(br), shift(bi)
        br, bi = br + akr * sr - aki * si, bi + akr * si + aki * sr
        k *= 2
    return br, bi


def _embed(t):
    a, b = t.shape[1], t.shape[2]
    return jnp.einsum("jgab,gh->jgahb", t.reshape(NJ, GB, a, b), jnp.eye(GB, dtype=t.dtype)).reshape(NJ, GB * a, GB * b)


def _diag_blocks(t, a, b):
    return jnp.einsum("jgahb,gh->jgab", t.reshape(NJ, GB, a, GB, b), jnp.eye(GB, dtype=t.dtype)).reshape(G, a, b)


NT = 16


def _chunks(L):
    ncb = min(CH, L // NT)
    return ncb, NT * ncb


def _cmul_add(ar, ai, xr, xi, br, bi):
    return ar * xr - ai * xi + br, ar * xi + ai * xr + bi


def _s5_states(u_ref, bb_ref, ar, ai, par_ref, pai_ref, cr, ci, ncb, bu_scr):
    er = ei = None
    for t in range(NT):
        bu = _mm(u_ref[pl.ds(t, ncb, stride=NT), :], bb_ref[...])
        bu_scr[t] = bu
        if t == 0:
            er, ei = bu[:, :SW], bu[:, SW:]
        else:
            er, ei = _cmul_add(ar, ai, er, ei, bu[:, :SW], bu[:, SW:])
    xr, xi = _cscan(er, ei, par_ref, pai_ref, False)
    fr, fi = _cmul_add(par_ref[0:ncb, :], pai_ref[0:ncb, :], cr, ci, xr, xi)
    row = lax.broadcasted_iota(jnp.int32, fr.shape, 0)
    cinr = jnp.where(row >= 1, pltpu.roll(fr, 1, 0), cr)
    cini = jnp.where(row >= 1, pltpu.roll(fi, 1, 0), ci)
    return cinr, cini, jnp.concatenate([fr[ncb - 1:ncb, :], fi[ncb - 1:ncb, :]], axis=1)


def _s5_scan_fwd(p, bb, cm, abr, abi, par, pai, dskip):
    L = p.shape[0]
    ncb, tb = _chunks(L)
    nb = L // tb

    def body(u_ref, bb_ref, cm_ref, ar_ref, ai_ref, par_ref, pai_ref, d_ref, ypre_ref, st_ref, carry, bu_scr):
        @pl.when(pl.program_id(1) == 0)
        def _():
            carry[...] = jnp.zeros_like(carry)

        c = carry[...]
        st_ref[...] = c
        ar, ai = ar_ref[...], ai_ref[...]
        sr, si, cnext = _s5_states(u_ref, bb_ref, ar, ai, par_ref, pai_ref, c[:, :SW], c[:, SW:], ncb, bu_scr)
        carry[...] = cnext
        for t in range(NT):
            bu = bu_scr[t]
            sr, si = _cmul_add(ar, ai, sr, si, bu[:, :SW], bu[:, SW:])
            rows = pl.ds(t, ncb, stride=NT)
            ypre_ref[rows, :] = _mm(jnp.concatenate([sr, si], axis=1), cm_ref[...]) + d_ref[...] * u_ref[rows, :]

    tab = pl.BlockSpec((CH, SW), lambda j, i: (0, j))
    vec = lambda w: pl.BlockSpec((1, w), lambda j, i: (0, j))
    return pl.pallas_call(
        body, name="s5_scan_fwd", grid=(NJ, nb),
        in_specs=[pl.BlockSpec((tb, UW), lambda j, i: (i, j)),
                  pl.BlockSpec((None, UW, 2 * SW), lambda j, i: (j, 0, 0)),
                  pl.BlockSpec((None, 2 * SW, UW), lambda j, i: (j, 0, 0)),
                  vec(SW), vec(SW), tab, tab, vec(UW)],
        out_specs=[pl.BlockSpec((tb, UW), lambda j, i: (i, j)),
                   pl.BlockSpec((None, None, 1, 2 * SW), lambda j, i: (j, i, 0, 0))],
        out_shape=[SDS((L, D), F32), SDS((NJ, nb, 1, 2 * SW), F32)],
        scratch_shapes=[pltpu.VMEM((1, 2 * SW), F32), pltpu.VMEM((NT, ncb, 2 * SW), F32)],
        compiler_params=_cp(40, ARB2),
    )(p, bb, cm, abr, abi, par, pai, dskip)


def _s5_gate_fwd(ypre, p, wglu, bglu):
    L = ypre.shape[0]
    tm = min(256, L)

    def body(y_ref, az_ref, wg_ref, bg_ref, ya_ref):
        yg = _gelu(y_ref[...])
        t = _mm(yg, wg_ref[...]) + bg_ref[...]
        act, _ = _silu_and_grad(az_ref[...])
        ya_ref[...] = (yg * jax.nn.sigmoid(t) * act).astype(ya_ref.dtype)

    return pl.pallas_call(
        body, name="s5_gate_fwd", grid=(L // tm,),
        in_specs=[pl.BlockSpec((tm, D), lambda i: (i, 0)), pl.BlockSpec((tm, D), lambda i: (i, 1)),
                  _full((D, D)), _full((1, D))],
        out_specs=pl.BlockSpec((tm, D), lambda i: (i, 0)),
        out_shape=SDS((L, D), MXU),
        compiler_params=_cp(32, ("arbitrary",)),
    )(ypre, p, wglu, bglu)


def _s5_gate_bwd(ypre, p, dx1, wout_e, wglu, bglu):
    L = ypre.shape[0]
    tm = min(256, L)

    def body(y_ref, az_ref, dx1_ref, wo_ref, wg_ref, bg_ref, dyp_ref, daz_ref, yg_ref, dt_ref, ya_ref, gbg_ref):
        @pl.when(pl.program_id(0) == 0)
        def _():
            gbg_ref[...] = jnp.zeros_like(gbg_ref)

        yg, dgelu = _gelu_and_grad(y_ref[...])
        sg = jax.nn.sigmoid(_mm(yg, wg_ref[...]) + bg_ref[...])
        act, dact = _silu_and_grad(az_ref[...])
        y2 = yg * sg
        dya = _mm_nt(dx1_ref[...], wo_ref[...])
        daz_ref[...] = (dya * y2 * dact).astype(daz_ref.dtype)
        dy2 = dya * act
        dt = dy2 * yg * sg * (1.0 - sg)
        dyg = dy2 * sg + _mm_nt(dt, wg_ref[...])
        dyp_ref[...] = dyg * dgelu
        yg_ref[...] = yg.astype(yg_ref.dtype)
        dt_ref[...] = dt.astype(dt_ref.dtype)
        ya_ref[...] = (y2 * act).astype(ya_ref.dtype)
        gbg_ref[...] += jnp.sum(dt, axis=0, keepdims=True)

    row = pl.BlockSpec((tm, D), lambda i: (i, 0))
    return pl.pallas_call(
        body, name="s5_gate_bwd", grid=(L // tm,),
        in_specs=[row, pl.BlockSpec((tm, D), lambda i: (i, 1)), row,
                  pl.BlockSpec((D, D), lambda i: (0, 0)), _full((D, D)), _full((1, D))],
        out_specs=[row, row, row, row, row, _full((1, D))],
        out_shape=[SDS((L, D), F32), SDS((L, D), MXU), SDS((L, D), MXU), SDS((L, D), MXU), SDS((L, D), MXU),
                   SDS((1, D), F32)],
        compiler_params=_cp(40, ("arbitrary",)),
    )(ypre, p, dx1, wout_e, wglu, bglu)


def _s5_scan_bwd(p, dypre, states, bb, cm, abr, abi, par, pai, pbr, pbi, dskip):
    L = p.shape[0]
    ncb, tb = _chunks(L)
    nb = L // tb
    rev = lambda i: nb - 1 - i

    def body(u_ref, dy_ref, st_ref, bb_ref, cm_ref, ar_ref, ai_ref, par_ref, pai_ref, pbr_ref, pbi_ref, d_ref,
             du_ref, gd_ref, gcm_ref, gbb_ref, gar_ref, gai_ref, lcarry, bu_scr, s_scr, gs_scr):
        @pl.when(pl.program_id(1) == 0)
        def _():
            lcarry[...] = jnp.zeros_like(lcarry)
            gd_ref[...] = jnp.zeros_like(gd_ref)
            gcm_ref[...] = jnp.zeros_like(gcm_ref)
            gbb_ref[...] = jnp.zeros_like(gbb_ref)
            gar_ref[...] = jnp.zeros_like(gar_ref)
            gai_ref[...] = jnp.zeros_like(gai_ref)

        ar, ai = ar_ref[...], ai_ref[...]
        c = st_ref[...]
        sr, si, _ = _s5_states(u_ref, bb_ref, ar, ai, par_ref, pai_ref, c[:, :SW], c[:, SW:], ncb, bu_scr)
        s_scr[0] = jnp.concatenate([sr, si], axis=1)
        for t in range(NT):
            bu = bu_scr[t]
            sr, si = _cmul_add(ar, ai, sr, si, bu[:, :SW], bu[:, SW:])
            s_scr[t + 1] = jnp.concatenate([sr, si], axis=1)
        fr = fi = None
        for t in reversed(range(NT)):
            gs = _mm_nt(dy_ref[pl.ds(t, ncb, stride=NT), :], cm_ref[...])
            gs_scr[t] = gs
            if t == NT - 1:
                fr, fi = gs[:, :SW], gs[:, SW:]
            else:
                fr, fi = _cmul_add(ar, -ai, fr, fi, gs[:, :SW], gs[:, SW:])
        xr, xi = _cscan(fr, fi, par_ref, pai_ref, True)
        lc = lcarry[...]
        lcr, lci = lc[:, :SW], lc[:, SW:]
        hr, hi = _cmul_add(pbr_ref[CH - ncb:CH, :], -pbi_ref[CH - ncb:CH, :], lcr, lci, xr, xi)
        lcarry[...] = jnp.concatenate([hr[0:1, :], hi[0:1, :]], axis=1)
        row = lax.broadcasted_iota(jnp.int32, hr.shape, 0)
        lr_ = jnp.where(row < ncb - 1, pltpu.roll(hr, ncb - 1, 0), lcr)
        li_ = jnp.where(row < ncb - 1, pltpu.roll(hi, ncb - 1, 0), lci)
        gar = jnp.zeros((1, SW), F32)
        gai = jnp.zeros((1, SW), F32)
        for t in reversed(range(NT)):
            gs = gs_scr[t]
            lr_, li_ = _cmul_add(ar, -ai, lr_, li_, gs[:, :SW], gs[:, SW:])
            rows = pl.ds(t, ncb, stride=NT)
            u_t, dy_t = u_ref[rows, :], dy_ref[rows, :]
            lam = jnp.concatenate([lr_, li_], axis=1)
            gbb_ref[...] += _mm_tn(u_t, lam)
            du_ref[rows, :] = _mm_nt(lam, bb_ref[...]) + dy_t * d_ref[...]
            gd_ref[...] += jnp.sum(dy_t * u_t, axis=0, keepdims=True)
            gcm_ref[...] += _mm_tn(s_scr[t + 1], dy_t)
            sp = s_scr[t]
            spr, spi = sp[:, :SW], sp[:, SW:]
            gar += jnp.sum(lr_ * spr + li_ * spi, axis=0, keepdims=True)
            gai += jnp.sum(li_ * spr - lr_ * spi, axis=0, keepdims=True)
        gar_ref[...] += gar
        gai_ref[...] += gai

    tab = pl.BlockSpec((CH, SW), lambda j, i: (0, j))
    colblk = pl.BlockSpec((tb, UW), lambda j, i: (rev(i), j))
    vec = lambda w: pl.BlockSpec((1, w), lambda j, i: (0, j))
    return pl.pallas_call(
        body, name="s5_scan_bwd", grid=(NJ, nb),
        in_specs=[colblk, colblk,
                  pl.BlockSpec((None, None, 1, 2 * SW), lambda j, i: (j, rev(i), 0, 0)),
                  pl.BlockSpec((None, UW, 2 * SW), lambda j, i: (j, 0, 0)),
                  pl.BlockSpec((None, 2 * SW, UW), lambda j, i: (j, 0, 0)),
                  vec(SW), vec(SW), tab, tab, tab, tab, vec(UW)],
        out_specs=[colblk, vec(UW),
                   pl.BlockSpec((None, 2 * SW, UW), lambda j, i: (j, 0, 0)),
                   pl.BlockSpec((None, UW, 2 * SW), lambda j, i: (j, 0, 0)),
                   vec(SW), vec(SW)],
        out_shape=[SDS((L, D), F32), SDS((1, D), F32),
                   SDS((NJ, 2 * SW, UW), F32), SDS((NJ, UW, 2 * SW), F32),
                   SDS((1, NSTATE), F32), SDS((1, NSTATE), F32)],
        scratch_shapes=[pltpu.VMEM((1, 2 * SW), F32), pltpu.VMEM((NT, ncb, 2 * SW), F32),
                        pltpu.VMEM((NT + 1, ncb, 2 * SW), F32), pltpu.VMEM((NT, ncb, 2 * SW), F32)],
        compiler_params=_cp(48, ARB2),
    )(p, dypre, states, bb, cm, abr, abi, par, pai, pbr, pbi, dskip)


def _rope_tables(L, inv):
    tm = min(512, L)

    def body(inv_ref, cos_ref, sin_ref):
        pos = (lax.broadcasted_iota(jnp.int32, (tm, DK // 2), 0) + pl.program_id(0) * tm).astype(F32)
        ang = pos * inv_ref[...]
        cos_ref[...] = jnp.cos(ang)
        sin_ref[...] = jnp.sin(ang)

    blk = pl.BlockSpec((tm, DK // 2), lambda i: (i, 0))
    return pl.pallas_call(body, name="rope_tables", grid=(L // tm,), in_specs=[_full((1, DK // 2))],
                          out_specs=[blk, blk], out_shape=[SDS((L, DK // 2), F32)] * 2)(inv)


def _rot(x, cos, sin):
    x1, x2 = x[:, :DK // 2], x[:, DK // 2:]
    return jnp.concatenate([x1 * cos - x2 * sin, x1 * sin + x2 * cos], axis=1)


def _unrot(d, cos, sin):
    d1, d2 = d[:, :DK // 2], d[:, DK // 2:]
    return jnp.concatenate([d1 * cos + d2 * sin, d2 * cos - d1 * sin], axis=1)


def _ret_decays(h):
    lg = LOG_G[h]
    n = lax.broadcasted_iota(jnp.int32, (CH, CH), 0)
    m = lax.broadcasted_iota(jnp.int32, (CH, CH), 1)
    diff = (n - m).astype(F32)
    decay = jnp.where(n >= m, jnp.exp(lg * jnp.maximum(diff, 0.0)), 0.0)
    idx = lax.broadcasted_iota(jnp.int32, (CH, 1), 0).astype(F32)
    xi = jnp.exp(lg * (idx + 1.0))
    zeta = jnp.exp(lg * (CH - 1.0 - idx))
    return decay, xi, zeta, math.exp(lg * CH)


def _ret_tables(dec_scr, vec_scr):
    for h in range(HEADS):
        decay, xi, zeta, _ = _ret_decays(h)
        dec_scr[h] = decay
        vec_scr[h] = jnp.concatenate([jnp.broadcast_to(xi, (CH, 128)), jnp.broadcast_to(zeta, (CH, 128))], axis=1)


def _ret_chunk_fwd(q, k, v, cos, sin, s_prev_b, decay, xi, zeta):
    qr = _rot(q, cos, sin)
    kr = _rot(k, cos, sin) * (DK ** -0.5)
    scores = _mm_nt(qr, kr) * decay
    o = _mm(scores, v) + _mm(qr * xi, s_prev_b)
    local = _mm_tn(kr * zeta, v)
    mu = jnp.mean(o, axis=-1, keepdims=True)
    oc = o - mu
    rstd = lax.rsqrt(jnp.mean(oc * oc, axis=-1, keepdims=True) + EPS)
    return qr, kr, scores, local, oc * rstd, rstd


def _ret_fwd(p, cos, sin, gain):
    L = p.shape[0]
    nb = L // CH

    def body(q_ref, k_ref, v_ref, bz_ref, cos_ref, sin_ref, g_ref, yb_ref, st_ref, state, dec_scr, vec_scr):
        @pl.when(pl.program_id(0) == 0)
        def _():
            state[...] = jnp.zeros_like(state)
            _ret_tables(dec_scr, vec_scr)

        cos, sin = cos_ref[...], sin_ref[...]
        act, _ = _silu_and_grad(bz_ref[...])
        for h in range(HEADS):
            hs = slice(h * DK, (h + 1) * DK)
            xi, zeta = vec_scr[h, :, 0:1], vec_scr[h, :, 128:129]
            s_prev = state[h]
            s_prev_b = s_prev.astype(MXU)
            st_ref[h] = s_prev_b
            _, _, _, local, on, _ = _ret_chunk_fwd(q_ref[:, hs], k_ref[:, hs], v_ref[:, hs], cos, sin,
                                                   s_prev_b, dec_scr[h], xi, zeta)
            state[h] = s_prev * math.exp(LOG_G[h] * CH) + local
            yb_ref[:, hs] = (on * g_ref[:, hs] * act[:, hs]).astype(yb_ref.dtype)

    col = lambda c: pl.BlockSpec((CH, D), lambda i: (i, c))
    rope = pl.BlockSpec((CH, DK // 2), lambda i: (i, 0))
    return pl.pallas_call(
        body, name="ret_fwd", grid=(nb,),
        in_specs=[col(2), col(3), col(4), col(5), rope, rope, _full((1, D))],
        out_specs=[pl.BlockSpec((CH, D), lambda i: (i, 0)),
                   pl.BlockSpec((None, HEADS, DK, DK), lambda i: (i, 0, 0, 0))],
        out_shape=[SDS((L, D), MXU), SDS((nb, HEADS, DK, DK), MXU)],
        scratch_shapes=[pltpu.VMEM((HEADS, DK, DK), F32), pltpu.VMEM((HEADS, CH, CH), F32),
                        pltpu.VMEM((HEADS, CH, 256), F32)],
        compiler_params=_cp(40, ("arbitrary",)),
    )(p, p, p, p, cos, sin, gain)


def _ret_bwd(p, cos, sin, gain, states, dx1, wout_e):
    L = p.shape[0]
    nb = L // CH
    rev = lambda i: nb - 1 - i

    def body(q_ref, k_ref, v_ref, bz_ref, cos_ref, sin_ref, g_ref, st_ref, dx1_ref, wo_ref,
             dq_ref, dk_ref, dv_ref, dbz_ref, yb_ref, gg_ref, gstate, dec_scr, vec_scr):
        @pl.when(pl.program_id(0) == 0)
        def _():
            gstate[...] = jnp.zeros_like(gstate)
            gg_ref[...] = jnp.zeros_like(gg_ref)
            _ret_tables(dec_scr, vec_scr)

        cos, sin = cos_ref[...], sin_ref[...]
        act, dact = _silu_and_grad(bz_ref[...])
        dyb = _mm_nt(dx1_ref[...], wo_ref[...])
        for h in range(HEADS):
            hs = slice(h * DK, (h + 1) * DK)
            decay = dec_scr[h]
            xi, zeta = vec_scr[h, :, 0:1], vec_scr[h, :, 128:129]
            v = v_ref[:, hs]
            s_prev_b = st_ref[h]
            qr, kr, scores, _, on, rstd = _ret_chunk_fwd(q_ref[:, hs], k_ref[:, hs], v, cos, sin, s_prev_b,
                                                         decay, xi, zeta)
            gain_h = g_ref[:, hs]
            out = on * gain_h
            yb_ref[:, hs] = (out * act[:, hs]).astype(yb_ref.dtype)
            dyb_h = dyb[:, hs]
            dbz_ref[:, hs] = (dyb_h * out * dact[:, hs]).astype(dbz_ref.dtype)
            dout = dyb_h * act[:, hs]
            gg_ref[:, hs] += jnp.sum(dout * on, axis=0, keepdims=True)
            don = dout * gain_h
            do = rstd * (don - jnp.mean(don, axis=-1, keepdims=True)
                         - on * jnp.mean(don * on, axis=-1, keepdims=True))
            gnext = gstate[h]
            gnext_b = gnext.astype(MXU)
            dscores = _mm_nt(do, v) * decay
            dv_ref[:, hs] = (_mm_tn(scores, do) + _mm(kr * zeta, gnext_b)).astype(dv_ref.dtype)
            dqr = _mm(dscores, kr) + _mm_nt(do, s_prev_b) * xi
            dkr = _mm_tn(dscores, qr) + _mm_nt(v, gnext_b) * zeta
            gstate[h] = gnext * math.exp(LOG_G[h] * CH) + _mm_tn(qr * xi, do)
            dq_ref[:, hs] = _unrot(dqr, cos, sin).astype(dq_ref.dtype)
            dk_ref[:, hs] = (_unrot(dkr, cos, sin) * (DK ** -0.5)).astype(dk_ref.dtype)

    col = lambda c: pl.BlockSpec((CH, D), lambda i: (rev(i), c))
    rope = pl.BlockSpec((CH, DK // 2), lambda i: (rev(i), 0))
    outc = col(0)
    act_out = SDS((L, D), MXU)
    return pl.pallas_call(
        body, name="ret_bwd", grid=(nb,),
        in_specs=[col(2), col(3), col(4), col(5), rope, rope, _full((1, D)),
                  pl.BlockSpec((None, HEADS, DK, DK), lambda i: (rev(i), 0, 0, 0)),
                  outc, pl.BlockSpec((D, D), lambda i: (1, 0))],
        out_specs=[outc, outc, outc, outc, outc, _full((1, D))],
        out_shape=[act_out, act_out, act_out, act_out, act_out, SDS((1, D), F32)],
        scratch_shapes=[pltpu.VMEM((HEADS, DK, DK), F32), pltpu.VMEM((HEADS, CH, CH), F32),
                        pltpu.VMEM((HEADS, CH, 256), F32)],
        compiler_params=_cp(48, ("arbitrary",)),
    )(p, p, p, p, cos, sin, gain, states, dx1, wout_e)


def _out_even(x, ya, yb, wout):
    L = x.shape[0]
    tm = min(512, L)

    def body(x_ref, ya_ref, yb_ref, w_ref, o_ref):
        cat = jnp.concatenate([ya_ref[...], yb_ref[...]], axis=1)
        o_ref[...] = x_ref[...] + jnp.dot(cat, w_ref[...], preferred_element_type=F32)

    row = pl.BlockSpec((tm, D), lambda i: (i, 0))
    return pl.pallas_call(
        body, name="out_even", grid=(L // tm,), in_specs=[row, row, row, _full((DI, D))],
        out_specs=row, out_shape=SDS((L, D), F32), compiler_params=_cp(32, ("arbitrary",)),
    )(x, ya, yb, wout)


def _sgu_core(pv, gain, ws_ref, bs_ref):
    pu, pvv, z = pv[:, :DI], pv[:, DI:2 * DI], pv[:, 2 * DI:]
    u, gu = _gelu_and_grad(pu)
    v, gv = _gelu_and_grad(pvv)
    mu = jnp.mean(v, axis=-1, keepdims=True)
    vc = v - mu
    rstd = lax.rsqrt(jnp.mean(vc * vc, axis=-1, keepdims=True) + EPS)
    vhat = vc * rstd
    vn = vhat * gain
    t = lax.broadcasted_iota(jnp.int32, (CH, CH), 0)
    s_ = lax.broadcasted_iota(jnp.int32, (CH, CH), 1)
    mask = t >= s_
    wm = [jnp.where(mask, ws_ref[g], 0.0).astype(MXU) for g in range(SG)]
    s = jnp.concatenate([_mm(wm[g], vn[:, g * SGD:(g + 1) * SGD]) + bs_ref[g] for g in range(SG)], axis=1)
    return gu, gv, z, u, vhat, rstd, vn, mask, wm, s


def _sgu_fwd(p2, x1, gain, wsp, bsp, wout, fnorm, tgt):
    L = p2.shape[0]

    def body(p_ref, x1_ref, g_ref, ws_ref, bs_ref, wo_ref, fn_ref, t_ref, dx2_ref, gfn_ref, loss_ref):
        @pl.when(pl.program_id(0) == 0)
        def _():
            gfn_ref[...] = jnp.zeros_like(gfn_ref)
            loss_ref[...] = jnp.zeros_like(loss_ref)

        _, _, z, u, _, _, _, _, _, s = _sgu_core(p_ref[...], g_ref[...], ws_ref, bs_ref)
        act, _ = _silu_and_grad(z)
        x2 = x1_ref[...] + _mm(u * s * act, wo_ref[...])
        xhat, r = _rms(x2)
        fn = fn_ref[...]
        e = xhat * fn - t_ref[...]
        loss_ref[...] += 0.5 * jnp.sum(jnp.mean(e * e, axis=-1, keepdims=True), axis=0, keepdims=True)
        do = e * (1.0 / D)
        gfn_ref[...] += jnp.sum(do * xhat, axis=0, keepdims=True)
        dxhat = do * fn
        dx2_ref[...] = r * (dxhat - xhat * jnp.mean(dxhat * xhat, axis=-1, keepdims=True))

    row = pl.BlockSpec((CH, D), lambda i: (i, 0))
    return pl.pallas_call(
        body, name="sgu_fwd", grid=(L // CH,),
        in_specs=[pl.BlockSpec((CH, NIN), lambda i: (i, 0)), row, _full((1, DI)), _full((SG, CH, CH)),
                  _full((SG, CH, 1)), _full((DI, D)), _full((1, D)), row],
        out_specs=[row, _full((1, D)), _full((1, 1))],
        out_shape=[SDS((L, D), F32), SDS((1, D), F32), SDS((1, 1), F32)],
        compiler_params=_cp(48, ("arbitrary",)),
    )(p2, x1, gain, wsp, bsp, wout, fnorm, tgt)


def _sgu_bwd(p2, dx2, gain, wsp, bsp, wout):
    L = p2.shape[0]

    def body(p_ref, dx2_ref, g_ref, ws_ref, bs_ref, wo_ref, dp_ref, y_ref, gg_ref, gws_ref, gbs_ref):
        @pl.when(pl.program_id(0) == 0)
        def _():
            gg_ref[...] = jnp.zeros_like(gg_ref)
            gws_ref[...] = jnp.zeros_like(gws_ref)
            gbs_ref[...] = jnp.zeros_like(gbs_ref)

        gain = g_ref[...]
        gu, gv, z, u, vhat, rstd, vn, mask, wm, s = _sgu_core(p_ref[...], gain, ws_ref, bs_ref)
        act, dact = _silu_and_grad(z)
        y_ref[...] = (u * s * act).astype(y_ref.dtype)
        dy = _mm_nt(dx2_ref[...], wo_ref[...])
        du = dy * s * act
        ds = dy * u * act
        dz = dy * u * s * dact
        dvn = []
        for g in range(SG):
            ds_g = ds[:, g * SGD:(g + 1) * SGD]
            vn_g = vn[:, g * SGD:(g + 1) * SGD]
            gbs_ref[g] += jnp.sum(ds_g, axis=1, keepdims=True)
            gws_ref[g] += jnp.where(mask, _mm_nt(ds_g, vn_g), 0.0)
            dvn.append(_mm_tn(wm[g], ds_g))
        dvn = jnp.concatenate(dvn, axis=1)
        gg_ref[...] += jnp.sum(dvn * vhat, axis=0, keepdims=True)
        dvhat = dvn * gain
        dv = rstd * (dvhat - jnp.mean(dvhat, axis=-1, keepdims=True)
                     - vhat * jnp.mean(dvhat * vhat, axis=-1, keepdims=True))
        dp_ref[...] = jnp.concatenate([du * gu, dv * gv, dz], axis=1).astype(dp_ref.dtype)

    return pl.pallas_call(
        body, name="sgu_bwd", grid=(L // CH,),
        in_specs=[pl.BlockSpec((CH, NIN), lambda i: (i, 0)), pl.BlockSpec((CH, D), lambda i: (i, 0)),
                  _full((1, DI)), _full((SG, CH, CH)), _full((SG, CH, 1)), _full((DI, D))],
        out_specs=[pl.BlockSpec((CH, NIN), lambda i: (i, 0)), pl.BlockSpec((CH, DI), lambda i: (i, 0)),
                   _full((1, DI)), _full((SG, CH, CH)), _full((SG, CH, 1))],
        out_shape=[SDS((L, NIN), MXU), SDS((L, DI), MXU), SDS((1, DI), F32), SDS((SG, CH, CH), F32),
                   SDS((SG, CH, 1), F32)],
        compiler_params=_cp(48, ("arbitrary",)),
    )(p2, dx2, gain, wsp, bsp, wout)


def _adamw(w, m, v, gparts, name):
    R, C = w.shape
    tr = R
    for cand in (256, 128, 64, 32, 16, 8):
        if R % cand == 0 and R > cand:
            tr = cand
            break

    def body(w_ref, m_ref, v_ref, gp_ref, g_ref, d_ref, mo_ref, vo_ref):
        g = gp_ref[0].astype(F32)
        for s in range(1, NDEV):
            g = g + gp_ref[s].astype(F32)
        w_ = w_ref[...]
        mn = ADAM_B1 * m_ref[...] + (1.0 - ADAM_B1) * g
        vn = ADAM_B2 * v_ref[...] + (1.0 - ADAM_B2) * (g * g)
        mhat = mn / BC1
        vhat = vn / BC2
        g_ref[...] = g
        d_ref[...] = -ADAM_LR * (mhat / (jnp.sqrt(vhat) + ADAM_EPS) + ADAM_WD * w_)
        mo_ref[...] = mn
        vo_ref[...] = vn

    blk = pl.BlockSpec((tr, C), lambda i: (i, 0))
    out = SDS((R, C), F32)
    return pl.pallas_call(
        body, name=name, grid=(R // tr,),
        in_specs=[blk, blk, blk, pl.BlockSpec((NDEV, tr, C), lambda i: (0, i, 0))],
        out_specs=[blk, blk, blk, blk], out_shape=[out, out, out, out],
        compiler_params=_cp(40, ("arbitrary",)),
    )(w, m, v, gparts)


MESH = pl.DeviceIdType.MESH
HBM_SPEC = pl.BlockSpec(memory_space=pltpu.HBM)
SEM_SPEC = pl.BlockSpec(memory_space=pltpu.SEMAPHORE)
EFFECT = pltpu.SideEffectType.DATAFLOW_SIDE_EFFECTING


def _me_and_peers():
    x, y, c = lax.axis_index("x"), lax.axis_index("y"), lax.axis_index("c")
    me = 4 * x + 2 * y + c
    peers = []
    for r in range(1, NDEV):
        px, py, pc = x ^ ((r >> 2) & 1), y ^ ((r >> 1) & 1), c ^ (r & 1)
        peers.append(((px, py, pc), 4 * px + 2 * py + pc))
    return me, peers


def _land_shape(a, scatter):
    return (NDEV,) + (a.shape[1:] if scatter else a.shape)


def _remote(src, dst, send_sems, recv_sems, r, k, n, dev):
    i = r * n + k
    return pltpu.make_async_remote_copy(src_ref=src, dst_ref=dst, send_sem=send_sems.at[i], recv_sem=recv_sems.at[i],
                                        device_id=dev, device_id_type=MESH)


def _exchange(arrays, scatter, name):
    n = len(arrays)
    out_shape = [SDS(_land_shape(a, scatter), a.dtype) for a in arrays]

    def body(*refs):
        ins, outs = refs[:n], refs[n:2 * n]
        send_sems, recv_sems, loc_sems = refs[2 * n:]
        me, peers = _me_and_peers()
        local = []
        for k in range(n):
            src = ins[k].at[me] if scatter else ins[k]
            cp = pltpu.make_async_copy(src, outs[k].at[me], loc_sems.at[k])
            cp.start()
            local.append(cp)
        sends = []
        for r, (dev, lin) in enumerate(peers):
            for k in range(n):
                src = ins[k].at[lin] if scatter else ins[k]
                cp = _remote(src, outs[k].at[me], send_sems, recv_sems, r, k, n, dev)
                cp.start()
                sends.append(cp)
        for r, (dev, lin) in enumerate(peers):
            for k in range(n):
                src = ins[k].at[me] if scatter else ins[k]
                _remote(src, outs[k].at[lin], send_sems, recv_sems, r, k, n, dev).wait_recv()
        for cp in sends:
            cp.wait_send()
        for cp in local:
            cp.wait()

    return pl.pallas_call(
        body, name=name, in_specs=[HBM_SPEC] * n, out_specs=[HBM_SPEC] * n, out_shape=out_shape,
        scratch_shapes=[pltpu.SemaphoreType.DMA(((NDEV - 1) * n,)), pltpu.SemaphoreType.DMA(((NDEV - 1) * n,)),
                        pltpu.SemaphoreType.DMA((n,))],
    )(*arrays)


def _exchange_start(arrays, scatter, name):
    n = len(arrays)
    lands = [lax.empty(_land_shape(a, scatter), a.dtype) for a in arrays]

    def body(*refs):
        ins, lnd = refs[:n], refs[n:2 * n]
        send_sems, recv_sems = refs[2 * n], refs[2 * n + 1]
        token = refs[-1]
        me, peers = _me_and_peers()
        for r, (dev, lin) in enumerate(peers):
            for k in range(n):
                src = ins[k].at[lin] if scatter else ins[k]
                _remote(src, lnd[k].at[me], send_sems, recv_sems, r, k, n, dev).start()
        token[...] = jnp.zeros_like(token)

    sem = pltpu.SemaphoreType.DMA(((NDEV - 1) * n,))
    outs = pl.pallas_call(
        body, name=name,
        out_shape=(sem, sem, *[pltpu.HBM(a.shape, a.dtype) for a in arrays],
                   *[pltpu.HBM(l.shape, l.dtype) for l in lands], SDS((8, 128), F32)),
        in_specs=[HBM_SPEC] * (2 * n),
        out_specs=(SEM_SPEC, SEM_SPEC, *[HBM_SPEC] * (2 * n), pl.BlockSpec(memory_space=pltpu.VMEM)),
        input_output_aliases={k: 2 + k for k in range(2 * n)},
        compiler_params=pltpu.CompilerParams(has_side_effects=EFFECT),
    )(*[pltpu.with_memory_space_constraint(a, pltpu.HBM) for a in arrays],
      *[pltpu.with_memory_space_constraint(l, pltpu.HBM) for l in lands])
    return (n, scatter, outs[0], outs[1], outs[2:2 + n], outs[2 + n:2 + 2 * n]), outs[-1]


def _exchange_wait(handle, after, name):
    n, scatter, send_sems, recv_sems, thru, lands = handle
    after = tuple(after)

    def body(*refs):
        ins, lnd = refs[:n], refs[n:2 * n]
        send_sems, recv_sems = refs[2 * n], refs[2 * n + 1]
        me, peers = _me_and_peers()
        for r, (dev, lin) in enumerate(peers):
            for k in range(n):
                src = ins[k].at[lin] if scatter else ins[k]
                cp = _remote(src, lnd[k].at[lin], send_sems, recv_sems, r, k, n, dev)
                cp.wait_send()
                cp.wait_recv()

    outs = pl.pallas_call(
        body, name=name,
        out_shape=(*[pltpu.HBM(a.shape, a.dtype) for a in thru], *[pltpu.HBM(l.shape, l.dtype) for l in lands]),
        in_specs=[HBM_SPEC] * (2 * n) + [SEM_SPEC, SEM_SPEC] + [ANY_SPEC] * len(after),
        out_specs=tuple([HBM_SPEC] * (2 * n)),
        input_output_aliases={k: k for k in range(2 * n)},
        compiler_params=pltpu.CompilerParams(has_side_effects=EFFECT),
    )(*thru, *lands, send_sems, recv_sems, *after)
    return list(outs[n:])


def _with_own(land, own, me):
    return lax.dynamic_update_slice_in_dim(land, own[None], me, axis=0)


def _local_step(x, tgt, norm_even, win_e, lam_re, lam_im, log_dt, b_re, b_im, c_re, c_im, s5_d, bglu,
                ret_gain, wsp, bsp, fnorm, late_weights, emit, start_token=None):
    L = x.shape[0]
    lr3, li3 = lam_re.reshape(G, 1, P), lam_im.reshape(G, 1, P)
    dt3 = log_dt.reshape(G, 1, 1)
    br3, bi3 = jnp.swapaxes(b_re, 1, 2), jnp.swapaxes(b_im, 1, 2)
    abr3, abi3, bbr3, bbi3 = _s5_disc(lr3, li3, dt3, br3, bi3)
    bb = jnp.concatenate([_embed(bbr3), _embed(bbi3)], axis=2).astype(MXU)
    cm = jnp.concatenate([_embed(jnp.swapaxes(c_re, 1, 2)), -_embed(jnp.swapaxes(c_im, 1, 2))], axis=1).astype(MXU)
    abr, abi = abr3.reshape(1, NSTATE), abi3.reshape(1, NSTATE)
    pwr, pwi, _, _ = _s5_tables(abr, abi, NT, "s5_tables_step")
    par, pai, pbr, pbi = _s5_tables(pwr[NT - 1:NT], pwi[NT - 1:NT], CH, "s5_tables_chunk")
    inv = (ROPE_BASE ** (-jnp.arange(DK // 2, dtype=F32) / (DK // 2))).reshape(1, DK // 2)
    cos, sin = _rope_tables(L, inv)
    bsp3 = bsp.reshape(SG, CH, 1)

    def dep(token):
        return NO_DEPS if token is None else (token,)

    p = _in_proj(x, norm_even, win_e, "in_even", dep(start_token))
    ypre, s5_states = _s5_scan_fwd(p, bb, cm, abr, abi, par, pai, s5_d)
    yb, ret_states = _ret_fwd(p, cos, sin, ret_gain)
    wglu, wout_e, norm_odd, win_o, sgu_gain, wout_o = late_weights((ypre, yb))
    ya = _s5_gate_fwd(ypre, p, wglu, bglu)
    x1 = _out_even(x, ya, yb, wout_e)
    p2 = _in_proj(x1, norm_odd, win_o, "in_odd")
    dx2, g_fnorm, loss = _sgu_fwd(p2, x1, sgu_gain, wsp, bsp3, wout_o, fnorm, tgt)

    dp2, y_o, g_sgu_gain, g_wsp, g_bsp = _sgu_bwd(p2, dx2, sgu_gain, wsp, bsp3, wout_o)
    g_wout_o = _wgrad_rows([y_o], dx2, "wgrad_out_odd")
    g_win_o = _wgrad_cols(x1, norm_odd, dp2, "wgrad_in_odd")
    tok = emit("odd", dict(w_in_odd=g_win_o, w_out_odd=g_wout_o))
    dx1, g_norm_odd = _in_proj_bwd_x(dp2, x1, norm_odd, win_o, dx2, "in_odd_bwd", dep(tok))

    dypre, daz, yg, dt, ya2, g_bglu = _s5_gate_bwd(ypre, p, dx1, wout_e, wglu, bglu)
    du, g_d, g_cm, g_bb, g_ar, g_ai = _s5_scan_bwd(p, dypre, s5_states, bb, cm, abr, abi, par, pai, pbr, pbi, s5_d)
    dbbr3 = _diag_blocks(g_bb[:, :, :SW], HG, P)
    dbbi3 = _diag_blocks(g_bb[:, :, SW:], HG, P)
    g_c_re = jnp.swapaxes(_diag_blocks(g_cm[:, :SW, :], P, HG), 1, 2)
    g_c_im = -jnp.swapaxes(_diag_blocks(g_cm[:, SW:, :], P, HG), 1, 2)
    g_lr3, g_li3, g_dt3, g_br3, g_bi3 = _s5_disc_bwd(
        lr3, li3, dt3, br3, bi3, g_ar.reshape(G, 1, P), g_ai.reshape(G, 1, P), dbbr3, dbbi3)
    dq, dk, dv, dbz, yb2, g_ret_gain = _ret_bwd(p, cos, sin, ret_gain, ret_states, dx1, wout_e)
    small = dict(
        s5_lam_re=g_lr3.reshape(G, P), s5_lam_im=g_li3.reshape(G, P),
        s5_log_dt=g_dt3.reshape(1, G), s5_b_re=jnp.swapaxes(g_br3, 1, 2), s5_b_im=jnp.swapaxes(g_bi3, 1, 2),
        s5_c_re=g_c_re, s5_c_im=g_c_im, s5_d=g_d, s5_b_glu=g_bglu, ret_gn_gain=g_ret_gain,
        norm_odd=g_norm_odd, sgu_norm_gain=g_sgu_gain, sgu_w_spatial=g_wsp, sgu_b_spatial=g_bsp.reshape(SG, CH),
        final_norm=g_fnorm)
    tok = emit("small", small)
    g_wout_e = _wgrad_rows([ya2, yb2], dx1, "wgrad_out_even", dep(tok))
    g_wglu = _wgrad_rows([yg], dt, "wgrad_glu", dep(tok))
    tok = emit("even_rows", dict(s5_w_glu=g_wglu, w_out_even=g_wout_e))
    dp = jnp.concatenate([du.astype(MXU), daz, dq, dk, dv, dbz], axis=1)
    g_win_e = _wgrad_cols(x, norm_even, dp, "wgrad_in_even", dep(tok))
    tok = emit("even_cols", dict(w_in_even=g_win_e))
    dx, g_norm_even = _in_proj_bwd_x(dp, x, norm_even, win_e, dx1, "in_even_bwd", dep(tok))
    emit("last", dict(norm_even=g_norm_even))
    return loss, dx


WEIGHTS = ['norm_even', 'w_in_even', 's5_lam_re', 's5_lam_im', 's5_log_dt', 's5_b_re', 's5_b_im', 's5_c_re',
           's5_c_im', 's5_d', 's5_w_glu', 's5_b_glu', 'ret_gn_gain', 'w_out_even', 'norm_odd', 'w_in_odd',
           'sgu_norm_gain', 'sgu_w_spatial', 'sgu_b_spatial', 'w_out_odd', 'final_norm']
BIG = ['w_in_even', 's5_w_glu', 'w_out_even', 'w_in_odd', 'w_out_odd']
SHARDED_SMALL = {'norm_odd': D // NDEV, 'sgu_norm_gain': DI // NDEV}
SMALL = [n for n in WEIGHTS if n not in BIG and n != 'norm_even']
LANES = 128


def kernel(x, norm_even, w_in_even, s5_lam_re, s5_lam_im, s5_log_dt, s5_b_re, s5_b_im, s5_c_re, s5_c_im, s5_d, s5_w_glu, s5_b_glu, ret_gn_gain, w_out_even, norm_odd, w_in_odd, sgu_norm_gain, sgu_w_spatial, sgu_b_spatial, w_out_odd, final_norm, loss_target, m_norm_even, m_w_in_even, m_s5_lam_re, m_s5_lam_im, m_s5_log_dt, m_s5_b_re, m_s5_b_im, m_s5_c_re, m_s5_c_im, m_s5_d, m_s5_w_glu, m_s5_b_glu, m_ret_gn_gain, m_w_out_even, m_norm_odd, m_w_in_odd, m_sgu_norm_gain, m_sgu_w_spatial, m_sgu_b_spatial, m_w_out_odd, m_final_norm, v_norm_even, v_w_in_even, v_s5_lam_re, v_s5_lam_im, v_s5_log_dt, v_s5_b_re, v_s5_b_im, v_s5_c_re, v_s5_c_im, v_s5_d, v_s5_w_glu, v_s5_b_glu, v_ret_gn_gain, v_w_out_even, v_norm_odd, v_w_in_odd, v_sgu_norm_gain, v_sgu_w_spatial, v_sgu_b_spatial, v_w_out_odd, v_final_norm):
    args = dict(locals())
    w = {n: args[n] for n in WEIGHTS}
    m = {n: args["m_" + n] for n in WEIGHTS}
    v = {n: args["v_" + n] for n in WEIGHTS}
    me = 4 * lax.axis_index("x") + 2 * lax.axis_index("y") + lax.axis_index("c")

    (win_e,) = _exchange([w['w_in_even'][0].astype(MXU)], False, "gather_w_in_even")
    late_own = [w['s5_w_glu'][0].astype(MXU), w['w_out_even'][0].astype(MXU), w['norm_odd'],
                w['w_in_odd'][0].astype(MXU), w['sgu_norm_gain'], w['w_out_odd'][0].astype(MXU)]
    late_handle, start_token = _exchange_start(late_own, False, "gather_late_start")

    def late_weights(after):
        lands = _exchange_wait(late_handle, after, "gather_late_wait")
        wglu, wout_e, nodd, win_o, sgug, wout_o = (_with_own(l, o, me) for l, o in zip(lands, late_own))
        return (wglu.reshape(D, D), wout_e.reshape(DI, D), nodd.reshape(1, D), win_o, sgug.reshape(1, DI),
                wout_o.reshape(DI, D))

    pending = {}
    small_last = {}

    def emit(stage, grads):
        if stage == "last":
            small_last.update(grads)
            return None
        if stage == "small":
            flat = jnp.concatenate([grads[n].reshape(-1) for n in SMALL])
            flat = jnp.pad(flat, (0, (-flat.shape[0]) % (8 * LANES))).reshape(1, -1)
            handle, token = _exchange_start([flat], False, "small_start")
            pending[stage] = (handle, ["flat"], [flat], {n: grads[n].size for n in SMALL})
            return token
        names = list(grads)
        handle, token = _exchange_start([grads[n] for n in names], True, stage + "_start")
        pending[stage] = (handle, names, [lax.dynamic_index_in_dim(grads[n], me, 0, keepdims=False) for n in names])
        return token

    loss, dx = _local_step(
        x[0], loss_target[0], w['norm_even'], win_e, w['s5_lam_re'][0], w['s5_lam_im'][0], w['s5_log_dt'][0],
        w['s5_b_re'][0], w['s5_b_im'][0], w['s5_c_re'][0], w['s5_c_im'][0], w['s5_d'], w['s5_b_glu'],
        w['ret_gn_gain'], w['sgu_w_spatial'][0], w['sgu_b_spatial'][0], w['final_norm'].reshape(1, D),
        late_weights, emit, start_token)

    out_g, out_d, out_m, out_v = {}, {}, {}, {}
    after = dx
    for stage in ("odd", "even_rows", "even_cols"):
        handle, names, own = pending[stage][:3]
        lands = _exchange_wait(handle, (after,), stage + "_wait")
        for n, land, o in zip(names, lands, own):
            shp = w[n].shape
            r, c = shp[1], shp[2]
            res = _adamw(w[n].reshape(r, c), m[n].reshape(r, c), v[n].reshape(r, c), _with_own(land, o, me),
                         "adamw_" + n)
            out_g[n], out_d[n], out_m[n], out_v[n] = (t.reshape(shp) for t in res)
            after = res[0]

    handle, _, own, sizes = pending["small"]
    (gsmall,) = _exchange_wait(handle, (after,), "small_wait")
    gsmall = _with_own(gsmall, own[0], me).reshape(NDEV, -1)
    pieces, off = [], 0
    for n in SMALL:
        seg = gsmall[:, off:off + sizes[n]]
        if n in SHARDED_SMALL:
            seg = lax.dynamic_slice_in_dim(seg, me * SHARDED_SMALL[n], SHARDED_SMALL[n], axis=1)
        pieces.append(seg)
        off += sizes[n]
    gp = jnp.concatenate(pieces, axis=1)
    wpad = (-gp.shape[1]) % (256 * LANES)

    def pack(d):
        t = jnp.concatenate([d[n].reshape(-1) for n in SMALL])
        return jnp.pad(t, (0, wpad)).reshape(-1, LANES)

    gp = jnp.pad(gp, ((0, 0), (0, wpad))).reshape(NDEV, -1, LANES)
    res = _adamw(pack(w), pack(m), pack(v), gp, "adamw_small")
    off = 0
    for n in SMALL:
        size = w[n].size
        for dst, src in zip((out_g, out_d, out_m, out_v), res):
            dst[n] = src.reshape(-1)[off:off + size].reshape(w[n].shape)
        off += size
    (gne,) = _exchange([small_last['norm_even']], False, "gather_norm_even")
    n = 'norm_even'
    res = _adamw(w[n].reshape(8, LANES), m[n].reshape(8, LANES), v[n].reshape(8, LANES),
                 gne.reshape(NDEV, 8, LANES), "adamw_norm_even")
    out_g[n], out_d[n], out_m[n], out_v[n] = (t.reshape(w[n].shape) for t in res)

    loss_total = lax.psum(loss[0, 0], AXES)
    return (loss_total, dx[None], *[out_g[n] for n in WEIGHTS], *[out_d[n] for n in WEIGHTS],
            *[out_m[n] for n in WEIGHTS], *[out_v[n] for n in WEIGHTS])
```

```python
import math

import jax
import jax.numpy as jnp
from jax import lax
from jax.experimental import pallas as pl
from jax.experimental.pallas import tpu as pltpu

F32 = jnp.float32
MXU = jnp.bfloat16
AXES = ("x", "y", "c")
NDEV = 8
D = 1024
NIN = 6144
WIN_BLK = NIN // NDEV
DI = 2048
G, P, HG = 64, 64, 16
GB = 8
NJ = G // GB
SW = GB * P
UW = GB * HG
NSTATE = G * P
HEADS, DK = 4, 256
CH = 128
SG, SGD = 4, 512
EPS = 1e-6
ROPE_BASE = 10000.0
VMEM_CAP_V7X = 64 * 1024 * 1024
LOG_G = [math.log1p(-2.0 ** (-5.0 - h)) for h in range(HEADS)]
GELU_C = math.sqrt(2.0 / math.pi)

ADAM_LR, ADAM_B1, ADAM_B2, ADAM_EPS, ADAM_WD, ADAM_STEP = 0.001, 0.9, 0.999, 1e-08, 0.01, 10
BC1 = 1.0 - ADAM_B1 ** ADAM_STEP
BC2 = 1.0 - ADAM_B2 ** ADAM_STEP

SDS = jax.ShapeDtypeStruct
ARB2 = ("arbitrary", "arbitrary")


def _cp(vmem_mib, sem=None):
    kw = dict(vmem_limit_bytes=min(vmem_mib * 1024 * 1024, VMEM_CAP_V7X - 4 * 1024 * 1024))
    if sem is not None:
        kw["dimension_semantics"] = sem
    return pltpu.CompilerParams(**kw)


def _mm(a, b):
    return jnp.dot(a.astype(MXU), b.astype(MXU), preferred_element_type=F32)


def _mm_nt(a, b):
    return lax.dot_general(a.astype(MXU), b.astype(MXU), (((1,), (1,)), ((), ())), preferred_element_type=F32)


def _mm_tn(a, b):
    return lax.dot_general(a.astype(MXU), b.astype(MXU), (((0,), (0,)), ((), ())), preferred_element_type=F32)


def _gelu(x):
    return _gelu_and_grad(x)[0]


def _gelu_and_grad(x):
    x2 = x * x
    th = jnp.tanh(GELU_C * x * (1.0 + 0.044715 * x2))
    hp = 0.5 * (1.0 + th)
    return x * hp, hp + 0.5 * x * (1.0 - th * th) * GELU_C * (1.0 + 3.0 * 0.044715 * x2)


def _silu_and_grad(x):
    s = jax.nn.sigmoid(x)
    return x * s, s * (1.0 + x * (1.0 - s))


def _full(shape):
    nd = len(shape)
    return pl.BlockSpec(shape, lambda *_: (0,) * nd)


def _rms(xf):
    r = lax.rsqrt(jnp.mean(xf * xf, axis=-1, keepdims=True) + EPS)
    return xf * r, r


ANY_SPEC = pl.BlockSpec(memory_space=pl.ANY)
NO_DEPS = ()


def _in_proj(x, gain, wst, name, deps=NO_DEPS):
    L = x.shape[0]
    tm = min(512, L)

    def body(x_ref, g_ref, w_ref, *rest):
        o_ref, h_scr = rest[len(deps):]

        @pl.when(pl.program_id(1) == 0)
        def _():
            xhat, _ = _rms(x_ref[...])
            h_scr[...] = (xhat * g_ref[...]).astype(MXU)

        o_ref[...] = jnp.dot(h_scr[...], w_ref[...], preferred_element_type=F32)

    return pl.pallas_call(
        body, name=name, grid=(L // tm, NDEV),
        in_specs=[pl.BlockSpec((tm, D), lambda i, n: (i, 0)), _full((1, D)),
                  pl.BlockSpec((None, D, WIN_BLK), lambda i, n: (n, 0, 0))] + [ANY_SPEC] * len(deps),
        out_specs=pl.BlockSpec((tm, WIN_BLK), lambda i, n: (i, n)),
        out_shape=SDS((L, NIN), F32),
        scratch_shapes=[pltpu.VMEM((tm, D), MXU)],
        compiler_params=_cp(32, ARB2),
    )(x, gain, wst, *deps)


def _in_proj_bwd_x(dp, x, gain, wst, dres, name, deps=NO_DEPS):
    L = x.shape[0]
    tm = min(512, L)

    def body(dp_ref, x_ref, g_ref, w_ref, dres_ref, *rest):
        dx_ref, gg_ref, acc = rest[len(deps):]
        i, n = pl.program_id(0), pl.program_id(1)

        @pl.when(n == 0)
        def _():
            acc[...] = jnp.zeros_like(acc)

        @pl.when((i == 0) & (n == 0))
        def _():
            gg_ref[...] = jnp.zeros_like(gg_ref)

        acc[...] += _mm_nt(dp_ref[...], w_ref[...])

        @pl.when(n == NDEV - 1)
        def _():
            xhat, r = _rms(x_ref[...])
            dh = acc[...]
            dxhat = dh * g_ref[...]
            dx_ref[...] = dres_ref[...] + r * (dxhat - xhat * jnp.mean(dxhat * xhat, axis=-1, keepdims=True))
            gg_ref[...] += jnp.sum(dh * xhat, axis=0, keepdims=True)

    return pl.pallas_call(
        body, name=name, grid=(L // tm, NDEV),
        in_specs=[pl.BlockSpec((tm, WIN_BLK), lambda i, n: (i, n)),
                  pl.BlockSpec((tm, D), lambda i, n: (i, 0)), _full((1, D)),
                  pl.BlockSpec((None, D, WIN_BLK), lambda i, n: (n, 0, 0)),
                  pl.BlockSpec((tm, D), lambda i, n: (i, 0))] + [ANY_SPEC] * len(deps),
        out_specs=[pl.BlockSpec((tm, D), lambda i, n: (i, 0)), _full((1, D))],
        out_shape=[SDS((L, D), F32), SDS((1, D), F32)],
        scratch_shapes=[pltpu.VMEM((tm, D), F32)],
        compiler_params=_cp(40, ARB2),
    )(dp, x, gain, wst, dres, *deps)


def _wgrad_cols(x, gain, dp, name, deps=NO_DEPS):
    L = x.shape[0]
    tk = min(512, L)
    nk = L // tk
    halves = 2
    nh = NDEV // halves

    def body(x_ref, g_ref, dp_ref, *rest):
        o_ref, acc = rest[len(deps):]
        k = pl.program_id(1)

        @pl.when(k == 0)
        def _():
            acc[...] = jnp.zeros_like(acc)

        xhat, _ = _rms(x_ref[...])
        acc[...] += _mm_tn(xhat * g_ref[...], dp_ref[...])

        @pl.when(k == nk - 1)
        def _():
            for c in range(nh):
                o_ref[c] = acc[:, c * WIN_BLK:(c + 1) * WIN_BLK].astype(o_ref.dtype)

    return pl.pallas_call(
        body, name=name, grid=(halves, nk),
        in_specs=[pl.BlockSpec((tk, D), lambda n, k: (k, 0)), _full((1, D)),
                  pl.BlockSpec((tk, nh * WIN_BLK), lambda n, k: (k, n))] + [ANY_SPEC] * len(deps),
        out_specs=pl.BlockSpec((nh, D, WIN_BLK), lambda n, k: (n, 0, 0)),
        out_shape=SDS((NDEV, D, WIN_BLK), MXU),
        scratch_shapes=[pltpu.VMEM((D, nh * WIN_BLK), F32)],
        compiler_params=_cp(56, ARB2),
    )(x, gain, dp, *deps)


def _wgrad_rows(a_parts, b, name, deps=NO_DEPS):
    L, N = b.shape
    na = len(a_parts)
    widths = [a.shape[1] for a in a_parts]
    M = sum(widths)
    tk = min(512, L)
    nk = L // tk

    def body(*refs):
        a_refs, b_ref = refs[:na], refs[na]
        o_ref, acc = refs[na + 1 + len(deps):]
        k = pl.program_id(0)

        @pl.when(k == 0)
        def _():
            acc[...] = jnp.zeros_like(acc)

        bv = b_ref[...].astype(MXU)
        off = 0
        for a_ref, wd in zip(a_refs, widths):
            acc[off:off + wd, :] += _mm_tn(a_ref[...], bv)
            off += wd

        @pl.when(k == nk - 1)
        def _():
            o_ref[...] = acc[...].astype(o_ref.dtype).reshape(o_ref.shape)

    return pl.pallas_call(
        body, name=name, grid=(nk,),
        in_specs=[pl.BlockSpec((tk, wd), lambda k: (k, 0)) for wd in widths]
        + [pl.BlockSpec((tk, N), lambda k: (k, 0))] + [ANY_SPEC] * len(deps),
        out_specs=_full((NDEV, M // NDEV, N)),
        out_shape=SDS((NDEV, M // NDEV, N), MXU),
        scratch_shapes=[pltpu.VMEM((M, N), F32)],
        compiler_params=_cp(48, ("arbitrary",)),
    )(*a_parts, b, *deps)


def _s5_disc_fn(lr_raw, li, logdt, br, bi):
    lr = jnp.minimum(lr_raw, -1e-4)
    dt = jnp.exp(logdt)
    mag = jnp.exp(lr * dt)
    abr = mag * jnp.cos(li * dt)
    abi = mag * jnp.sin(li * dt)
    den = lr * lr + li * li
    nre = abr - 1.0
    nim = abi
    zr = (nre * lr + nim * li) / den
    zi = (nim * lr - nre * li) / den
    return abr, abi, zr * br - zi * bi, zr * bi + zi * br


def _s5_disc(lr, li, logdt, br, bi):
    def body(lr_ref, li_ref, dt_ref, br_ref, bi_ref, abr_ref, abi_ref, bbr_ref, bbi_ref):
        abr, abi, bbr, bbi = _s5_disc_fn(lr_ref[...], li_ref[...], dt_ref[...], br_ref[...], bi_ref[...])
        abr_ref[...] = abr
        abi_ref[...] = abi
        bbr_ref[...] = bbr
        bbi_ref[...] = bbi

    s1, s3 = SDS((G, 1, P), F32), SDS((G, HG, P), F32)
    return pl.pallas_call(body, name="s5_disc", out_shape=[s1, s1, s3, s3])(lr, li, logdt, br, bi)


def _s5_disc_bwd(lr, li, logdt, br, bi, dabr, dabi, dbbr, dbbi):
    def body(lr_ref, li_ref, dt_ref, br_ref, bi_ref, c0, c1, c2, c3, o0, o1, o2, o3, o4):
        _, vjp = jax.vjp(_s5_disc_fn, lr_ref[...], li_ref[...], dt_ref[...], br_ref[...], bi_ref[...])
        g = vjp((c0[...], c1[...], c2[...], c3[...]))
        for o, v in zip((o0, o1, o2, o3, o4), g):
            o[...] = v

    s1, s3 = SDS((G, 1, P), F32), SDS((G, HG, P), F32)
    return pl.pallas_call(body, name="s5_disc_bwd", out_shape=[s1, s1, SDS((G, 1, 1), F32), s3, s3])(
        lr, li, logdt, br, bi, dabr, dabi, dbbr, dbbi)


def _s5_tables(abr, abi, rows, name):
    def body(ar_ref, ai_ref, pfr, pfi, pbr, pbi):
        pfr[0:1, :] = ar_ref[...]
        pfi[0:1, :] = ai_ref[...]
        pbr[rows - 1:rows, :] = ar_ref[...]
        pbi[rows - 1:rows, :] = ai_ref[...]
        n = 1
        while n < rows:
            er, ei = pfr[n - 1:n, :], pfi[n - 1:n, :]
            xr, xi = pfr[0:n, :], pfi[0:n, :]
            pfr[n:2 * n, :] = er * xr - ei * xi
            pfi[n:2 * n, :] = er * xi + ei * xr
            yr, yi = pbr[rows - n:rows, :], pbi[rows - n:rows, :]
            pbr[rows - 2 * n:rows - n, :] = er * yr - ei * yi
            pbi[rows - 2 * n:rows - n, :] = er * yi + ei * yr
            n *= 2

    s = SDS((rows, NSTATE), F32)
    return pl.pallas_call(body, name=name, out_shape=[s, s, s, s], compiler_params=_cp(40))(abr, abi)


def _cscan(br, bi, pr_ref, pi_ref, reverse):
    T = br.shape[0]
    sign = -1.0 if reverse else 1.0
    row = lax.broadcasted_iota(jnp.int32, br.shape, 0)
    k = 1
    while k < T:
        akr = pr_ref[k - 1:k, :]
        aki = sign * pi_ref[k - 1:k, :]

        def shift(v):
            if k % 8 == 0:
                z = jnp.zeros((k, v.shape[1]), v.dtype)
                return jnp.concatenate([v[k:], z], 0) if reverse else jnp.concatenate([z, v[:T - k]], 0)
            if reverse:
                return jnp.where(row < T - k, pltpu.roll(v, T - k, 0), 0.0)
            return jnp.where(row >= k, pltpu.roll(v, k, 0), 0.0)

        sr, si = shift(br), shift(bi)
        br, bi = br + akr * sr - aki * si, bi + akr * si + aki * sr
        k *= 2
    return br, bi


def _embed(t):
    a, b = t.shape[1], t.shape[2]
    return jnp.einsum("jgab,gh->jgahb", t.reshape(NJ, GB, a, b), jnp.eye(GB, dtype=t.dtype)).reshape(NJ, GB * a, GB * b)


def _diag_blocks(t, a, b):
    return jnp.einsum("jgahb,gh->jgab", t.reshape(NJ, GB, a, GB, b), jnp.eye(GB, dtype=t.dtype)).reshape(G, a, b)


NT = 16


def _chunks(L):
    ncb = min(CH, L // NT)
    return ncb, NT * ncb


def _cmul_add(ar, ai, xr, xi, br, bi):
    return ar * xr - ai * xi + br, ar * xi + ai * xr + bi


def _s5_states(u_ref, bb_ref, ar, ai, par_ref, pai_ref, cr, ci, ncb, bu_scr):
    er = ei = None
    for t in range(NT):
        bu = _mm(u_ref[pl.ds(t, ncb, stride=NT), :], bb_ref[...])
        bu_scr[t] = bu
        if t == 0:
            er, ei = bu[:, :SW], bu[:, SW:]
        else:
            er, ei = _cmul_add(ar, ai, er, ei, bu[:, :SW], bu[:, SW:])
    xr, xi = _cscan(er, ei, par_ref, pai_ref, False)
    fr, fi = _cmul_add(par_ref[0:ncb, :], pai_ref[0:ncb, :], cr, ci, xr, xi)
    row = lax.broadcasted_iota(jnp.int32, fr.shape, 0)
    cinr = jnp.where(row >= 1, pltpu.roll(fr, 1, 0), cr)
    cini = jnp.where(row >= 1, pltpu.roll(fi, 1, 0), ci)
    return cinr, cini, jnp.concatenate([fr[ncb - 1:ncb, :], fi[ncb - 1:ncb, :]], axis=1)


def _s5_scan_fwd(p, bb, cm, abr, abi, par, pai, dskip):
    L = p.shape[0]
    ncb, tb = _chunks(L)
    nb = L // tb

    def body(u_ref, bb_ref, cm_ref, ar_ref, ai_ref, par_ref, pai_ref, d_ref, ypre_ref, st_ref, carry, bu_scr):
        @pl.when(pl.program_id(1) == 0)
        def _():
            carry[...] = jnp.zeros_like(carry)

        c = carry[...]
        st_ref[...] = c
        ar, ai = ar_ref[...], ai_ref[...]
        sr, si, cnext = _s5_states(u_ref, bb_ref, ar, ai, par_ref, pai_ref, c[:, :SW], c[:, SW:], ncb, bu_scr)
        carry[...] = cnext
        for t in range(NT):
            bu = bu_scr[t]
            sr, si = _cmul_add(ar, ai, sr, si, bu[:, :SW], bu[:, SW:])
            rows = pl.ds(t, ncb, stride=NT)
            ypre_ref[rows, :] = _mm(jnp.concatenate([sr, si], axis=1), cm_ref[...]) + d_ref[...] * u_ref[rows, :]

    tab = pl.BlockSpec((CH, SW), lambda j, i: (0, j))
    vec = lambda w: pl.BlockSpec((1, w), lambda j, i: (0, j))
    return pl.pallas_call(
        body, name="s5_scan_fwd", grid=(NJ, nb),
        in_specs=[pl.BlockSpec((tb, UW), lambda j, i: (i, j)),
                  pl.BlockSpec((None, UW, 2 * SW), lambda j, i: (j, 0, 0)),
                  pl.BlockSpec((None, 2 * SW, UW), lambda j, i: (j, 0, 0)),
                  vec(SW), vec(SW), tab, tab, vec(UW)],
        out_specs=[pl.BlockSpec((tb, UW), lambda j, i: (i, j)),
                   pl.BlockSpec((None, None, 1, 2 * SW), lambda j, i: (j, i, 0, 0))],
        out_shape=[SDS((L, D), F32), SDS((NJ, nb, 1, 2 * SW), F32)],
        scratch_shapes=[pltpu.VMEM((1, 2 * SW), F32), pltpu.VMEM((NT, ncb, 2 * SW), F32)],
        compiler_params=_cp(40, ARB2),
    )(p, bb, cm, abr, abi, par, pai, dskip)


def _s5_gate_fwd(ypre, p, wglu, bglu):
    L = ypre.shape[0]
    tm = min(256, L)

    def body(y_ref, az_ref, wg_ref, bg_ref, ya_ref):
        yg = _gelu(y_ref[...])
        t = _mm(yg, wg_ref[...]) + bg_ref[...]
        act, _ = _silu_and_grad(az_ref[...])
        ya_ref[...] = (yg * jax.nn.sigmoid(t) * act).astype(ya_ref.dtype)

    return pl.pallas_call(
        body, name="s5_gate_fwd", grid=(L // tm,),
        in_specs=[pl.BlockSpec((tm, D), lambda i: (i, 0)), pl.BlockSpec((tm, D), lambda i: (i, 1)),
                  _full((D, D)), _full((1, D))],
        out_specs=pl.BlockSpec((tm, D), lambda i: (i, 0)),
        out_shape=SDS((L, D), MXU),
        compiler_params=_cp(32, ("arbitrary",)),
    )(ypre, p, wglu, bglu)


def _s5_gate_bwd(ypre, p, dx1, wout_e, wglu, bglu):
    L = ypre.shape[0]
    tm = min(256, L)

    def body(y_ref, az_ref, dx1_ref, wo_ref, wg_ref, bg_ref, dyp_ref, daz_ref, yg_ref, dt_ref, ya_ref, gbg_ref):
        @pl.when(pl.program_id(0) == 0)
        def _():
            gbg_ref[...] = jnp.zeros_like(gbg_ref)

        yg, dgelu = _gelu_and_grad(y_ref[...])
        sg = jax.nn.sigmoid(_mm(yg, wg_ref[...]) + bg_ref[...])
        act, dact = _silu_and_grad(az_ref[...])
        y2 = yg * sg
        dya = _mm_nt(dx1_ref[...], wo_ref[...])
        daz_ref[...] = (dya * y2 * dact).astype(daz_ref.dtype)
        dy2 = dya * act
        dt = dy2 * yg * sg * (1.0 - sg)
        dyg = dy2 * sg + _mm_nt(dt, wg_ref[...])
        dyp_ref[...] = dyg * dgelu
        yg_ref[...] = yg.astype(yg_ref.dtype)
        dt_ref[...] = dt.astype(dt_ref.dtype)
        ya_ref[...] = (y2 * act).astype(ya_ref.dtype)
        gbg_ref[...] += jnp.sum(dt, axis=0, keepdims=True)

    row = pl.BlockSpec((tm, D), lambda i: (i, 0))
    return pl.pallas_call(
        body, name="s5_gate_bwd", grid=(L // tm,),
        in_specs=[row, pl.BlockSpec((tm, D), lambda i: (i, 1)), row,
                  pl.BlockSpec((D, D), lambda i: (0, 0)), _full((D, D)), _full((1, D))],
        out_specs=[row, row, row, row, row, _full((1, D))],
        out_shape=[SDS((L, D), F32), SDS((L, D), MXU), SDS((L, D), MXU), SDS((L, D), MXU), SDS((L, D), MXU),
                   SDS((1, D), F32)],
        compiler_params=_cp(40, ("arbitrary",)),
    )(ypre, p, dx1, wout_e, wglu, bglu)


def _s5_scan_bwd(p, dypre, states, bb, cm, abr, abi, par, pai, pbr, pbi, dskip):
    L = p.shape[0]
    ncb, tb = _chunks(L)
    nb = L // tb
    rev = lambda i: nb - 1 - i

    def body(u_ref, dy_ref, st_ref, bb_ref, cm_ref, ar_ref, ai_ref, par_ref, pai_ref, pbr_ref, pbi_ref, d_ref,
             du_ref, gd_ref, gcm_ref, gbb_ref, gar_ref, gai_ref, lcarry, bu_scr, s_scr, gs_scr):
        @pl.when(pl.program_id(1) == 0)
        def _():
            lcarry[...] = jnp.zeros_like(lcarry)
            gd_ref[...] = jnp.zeros_like(gd_ref)
            gcm_ref[...] = jnp.zeros_like(gcm_ref)
            gbb_ref[...] = jnp.zeros_like(gbb_ref)
            gar_ref[...] = jnp.zeros_like(gar_ref)
            gai_ref[...] = jnp.zeros_like(gai_ref)

        ar, ai = ar_ref[...], ai_ref[...]
        c = st_ref[...]
        sr, si, _ = _s5_states(u_ref, bb_ref, ar, ai, par_ref, pai_ref, c[:, :SW], c[:, SW:], ncb, bu_scr)
        s_scr[0] = jnp.concatenate([sr, si], axis=1)
        for t in range(NT):
            bu = bu_scr[t]
            sr, si = _cmul_add(ar, ai, sr, si, bu[:, :SW], bu[:, SW:])
            s_scr[t + 1] = jnp.concatenate([sr, si], axis=1)
        fr = fi = None
        for t in reversed(range(NT)):
            gs = _mm_nt(dy_ref[pl.ds(t, ncb, stride=NT), :], cm_ref[...])
            gs_scr[t] = gs
            if t == NT - 1:
                fr, fi = gs[:, :SW], gs[:, SW:]
            else:
                fr, fi = _cmul_add(ar, -ai, fr, fi, gs[:, :SW], gs[:, SW:])
        xr, xi = _cscan(fr, fi, par_ref, pai_ref, True)
        lc = lcarry[...]
        lcr, lci = lc[:, :SW], lc[:, SW:]
        hr, hi = _cmul_add(pbr_ref[CH - ncb:CH, :], -pbi_ref[CH - ncb:CH, :], lcr, lci, xr, xi)
        lcarry[...] = jnp.concatenate([hr[0:1, :], hi[0:1, :]], axis=1)
        row = lax.broadcasted_iota(jnp.int32, hr.shape, 0)
        lr_ = jnp.where(row < ncb - 1, pltpu.roll(hr, ncb - 1, 0), lcr)
        li_ = jnp.where(row < ncb - 1, pltpu.roll(hi, ncb - 1, 0), lci)
        gar = jnp.zeros((1, SW), F32)
        gai = jnp.zeros((1, SW), F32)
        for t in reversed(range(NT)):
            gs = gs_scr[t]
            lr_, li_ = _cmul_add(ar, -ai, lr_, li_, gs[:, :SW], gs[:, SW:])
            rows = pl.ds(t, ncb, stride=NT)
            u_t, dy_t = u_ref[rows, :], dy_ref[rows, :]
            lam = jnp.concatenate([lr_, li_], axis=1)
            gbb_ref[...] += _mm_tn(u_t, lam)
            du_ref[rows, :] = _mm_nt(lam, bb_ref[...]) + dy_t * d_ref[...]
            gd_ref[...] += jnp.sum(dy_t * u_t, axis=0, keepdims=True)
            gcm_ref[...] += _mm_tn(s_scr[t + 1], dy_t)
            sp = s_scr[t]
            spr, spi = sp[:, :SW], sp[:, SW:]
            gar += jnp.sum(lr_ * spr + li_ * spi, axis=0, keepdims=True)
            gai += jnp.sum(li_ * spr - lr_ * spi, axis=0, keepdims=True)
        gar_ref[...] += gar
        gai_ref[...] += gai

    tab = pl.BlockSpec((CH, SW), lambda j, i: (0, j))
    colblk = pl.BlockSpec((tb, UW), lambda j, i: (rev(i), j))
    vec = lambda w: pl.BlockSpec((1, w), lambda j, i: (0, j))
    return pl.pallas_call(
        body, name="s5_scan_bwd", grid=(NJ, nb),
        in_specs=[colblk, colblk,
                  pl.BlockSpec((None, None, 1, 2 * SW), lambda j, i: (j, rev(i), 0, 0)),
                  pl.BlockSpec((None, UW, 2 * SW), lambda j, i: (j, 0, 0)),
                  pl.BlockSpec((None, 2 * SW, UW), lambda j, i: (j, 0, 0)),
                  vec(SW), vec(SW), tab, tab, tab, tab, vec(UW)],
        out_specs=[colblk, vec(UW),
                   pl.BlockSpec((None, 2 * SW, UW), lambda j, i: (j, 0, 0)),
                   pl.BlockSpec((None, UW, 2 * SW), lambda j, i: (j, 0, 0)),
                   vec(SW), vec(SW)],
        out_shape=[SDS((L, D), F32), SDS((1, D), F32),
                   SDS((NJ, 2 * SW, UW), F32), SDS((NJ, UW, 2 * SW), F32),
                   SDS((1, NSTATE), F32), SDS((1, NSTATE), F32)],
        scratch_shapes=[pltpu.VMEM((1, 2 * SW), F32), pltpu.VMEM((NT, ncb, 2 * SW), F32),
                        pltpu.VMEM((NT + 1, ncb, 2 * SW), F32), pltpu.VMEM((NT, ncb, 2 * SW), F32)],
        compiler_params=_cp(48, ARB2),
    )(p, dypre, states, bb, cm, abr, abi, par, pai, pbr, pbi, dskip)


def _rope_tables(L, inv):
    tm = min(512, L)

    def body(inv_ref, cos_ref, sin_ref):
        pos = (lax.broadcasted_iota(jnp.int32, (tm, DK // 2), 0) + pl.program_id(0) * tm).astype(F32)
        ang = pos * inv_ref[...]
        cos_ref[...] = jnp.cos(ang)
        sin_ref[...] = jnp.sin(ang)

    blk = pl.BlockSpec((tm, DK // 2), lambda i: (i, 0))
    return pl.pallas_call(body, name="rope_tables", grid=(L // tm,), in_specs=[_full((1, DK // 2))],
                          out_specs=[blk, blk], out_shape=[SDS((L, DK // 2), F32)] * 2)(inv)


def _rot(x, cos, sin):
    x1, x2 = x[:, :DK // 2], x[:, DK // 2:]
    return jnp.concatenate([x1 * cos - x2 * sin, x1 * sin + x2 * cos], axis=1)


def _unrot(d, cos, sin):
    d1, d2 = d[:, :DK // 2], d[:, DK // 2:]
    return jnp.concatenate([d1 * cos + d2 * sin, d2 * cos - d1 * sin], axis=1)


def _ret_decays(h):
    lg = LOG_G[h]
    n = lax.broadcasted_iota(jnp.int32, (CH, CH), 0)
    m = lax.broadcasted_iota(jnp.int32, (CH, CH), 1)
    diff = (n - m).astype(F32)
    decay = jnp.where(n >= m, jnp.exp(lg * jnp.maximum(diff, 0.0)), 0.0)
    idx = lax.broadcasted_iota(jnp.int32, (CH, 1), 0).astype(F32)
    xi = jnp.exp(lg * (idx + 1.0))
    zeta = jnp.exp(lg * (CH - 1.0 - idx))
    return decay, xi, zeta, math.exp(lg * CH)


def _ret_tables(dec_scr, vec_scr):
    for h in range(HEADS):
        decay, xi, zeta, _ = _ret_decays(h)
        dec_scr[h] = decay
        vec_scr[h] = jnp.concatenate([jnp.broadcast_to(xi, (CH, 128)), jnp.broadcast_to(zeta, (CH, 128))], axis=1)


def _ret_chunk_fwd(q, k, v, cos, sin, s_prev_b, decay, xi, zeta):
    qr = _rot(q, cos, sin)
    kr = _rot(k, cos, sin) * (DK ** -0.5)
    scores = _mm_nt(qr, kr) * decay
    o = _mm(scores, v) + _mm(qr * xi, s_prev_b)
    local = _mm_tn(kr * zeta, v)
    mu = jnp.mean(o, axis=-1, keepdims=True)
    oc = o - mu
    rstd = lax.rsqrt(jnp.mean(oc * oc, axis=-1, keepdims=True) + EPS)
    return qr, kr, scores, local, oc * rstd, rstd


def _ret_fwd(p, cos, sin, gain):
    L = p.shape[0]
    nb = L // CH

    def body(q_ref, k_ref, v_ref, bz_ref, cos_ref, sin_ref, g_ref, yb_ref, st_ref, state, dec_scr, vec_scr):
        @pl.when(pl.program_id(0) == 0)
        def _():
            state[...] = jnp.zeros_like(state)
            _ret_tables(dec_scr, vec_scr)

        cos, sin = cos_ref[...], sin_ref[...]
        act, _ = _silu_and_grad(bz_ref[...])
        for h in range(HEADS):
            hs = slice(h * DK, (h + 1) * DK)
            xi, zeta = vec_scr[h, :, 0:1], vec_scr[h, :, 128:129]
            s_prev = state[h]
            s_prev_b = s_prev.astype(MXU)
            st_ref[h] = s_prev_b
            _, _, _, local, on, _ = _ret_chunk_fwd(q_ref[:, hs], k_ref[:, hs], v_ref[:, hs], cos, sin,
                                                   s_prev_b, dec_scr[h], xi, zeta)
            state[h] = s_prev * math.exp(LOG_G[h] * CH) + local
            yb_ref[:, hs] = (on * g_ref[:, hs] * act[:, hs]).astype(yb_ref.dtype)

    col = lambda c: pl.BlockSpec((CH, D), lambda i: (i, c))
    rope = pl.BlockSpec((CH, DK // 2), lambda i: (i, 0))
    return pl.pallas_call(
        body, name="ret_fwd", grid=(nb,),
        in_specs=[col(2), col(3), col(4), col(5), rope, rope, _full((1, D))],
        out_specs=[pl.BlockSpec((CH, D), lambda i: (i, 0)),
                   pl.BlockSpec((None, HEADS, DK, DK), lambda i: (i, 0, 0, 0))],
        out_shape=[SDS((L, D), MXU), SDS((nb, HEADS, DK, DK), MXU)],
        scratch_shapes=[pltpu.VMEM((HEADS, DK, DK), F32), pltpu.VMEM((HEADS, CH, CH), F32),
                        pltpu.VMEM((HEADS, CH, 256), F32)],
        compiler_params=_cp(40, ("arbitrary",)),
    )(p, p, p, p, cos, sin, gain)


def _ret_bwd(p, cos, sin, gain, states, dx1, wout_e, du, daz):
    L = p.shape[0]
    nb = L // CH
    rev = lambda i: nb - 1 - i

    def body(q_ref, k_ref, v_ref, bz_ref, cos_ref, sin_ref, g_ref, st_ref, dx1_ref, wo_ref, du_ref, daz_ref,
             dp_ref, yb_ref, gg_ref, gstate, dec_scr, vec_scr):
        @pl.when(pl.program_id(0) == 0)
        def _():
            gstate[...] = jnp.zeros_like(gstate)
            gg_ref[...] = jnp.zeros_like(gg_ref)
            _ret_tables(dec_scr, vec_scr)

        cos, sin = cos_ref[...], sin_ref[...]
        act, dact = _silu_and_grad(bz_ref[...])
        dyb = _mm_nt(dx1_ref[...], wo_ref[...])
        dp_ref[:, 0:D] = du_ref[...].astype(dp_ref.dtype)
        dp_ref[:, D:2 * D] = daz_ref[...]
        for h in range(HEADS):
            hs = slice(h * DK, (h + 1) * DK)
            col = lambda part: slice((2 + part) * D + h * DK, (2 + part) * D + (h + 1) * DK)
            decay = dec_scr[h]
            xi, zeta = vec_scr[h, :, 0:1], vec_scr[h, :, 128:129]
            v = v_ref[:, hs]
            s_prev_b = st_ref[h]
            qr, kr, scores, _, on, rstd = _ret_chunk_fwd(q_ref[:, hs], k_ref[:, hs], v, cos, sin, s_prev_b,
                                                         decay, xi, zeta)
            gain_h = g_ref[:, hs]
            out = on * gain_h
            yb_ref[:, hs] = (out * act[:, hs]).astype(yb_ref.dtype)
            dyb_h = dyb[:, hs]
            dp_ref[:, col(3)] = (dyb_h * out * dact[:, hs]).astype(dp_ref.dtype)
            dout = dyb_h * act[:, hs]
            gg_ref[:, hs] += jnp.sum(dout * on, axis=0, keepdims=True)
            don = dout * gain_h
            do = rstd * (don - jnp.mean(don, axis=-1, keepdims=True)
                         - on * jnp.mean(don * on, axis=-1, keepdims=True))
            gnext = gstate[h]
            gnext_b = gnext.astype(MXU)
            dscores = _mm_nt(do, v) * decay
            dp_ref[:, col(2)] = (_mm_tn(scores, do) + _mm(kr * zeta, gnext_b)).astype(dp_ref.dtype)
            dqr = _mm(dscores, kr) + _mm_nt(do, s_prev_b) * xi
            dkr = _mm_tn(dscores, qr) + _mm_nt(v, gnext_b) * zeta
            gstate[h] = gnext * math.exp(LOG_G[h] * CH) + _mm_tn(qr * xi, do)
            dp_ref[:, col(0)] = _unrot(dqr, cos, sin).astype(dp_ref.dtype)
            dp_ref[:, col(1)] = (_unrot(dkr, cos, sin) * (DK ** -0.5)).astype(dp_ref.dtype)

    col = lambda c: pl.BlockSpec((CH, D), lambda i: (rev(i), c))
    rope = pl.BlockSpec((CH, DK // 2), lambda i: (rev(i), 0))
    outc = col(0)
    act_out = SDS((L, D), MXU)
    return pl.pallas_call(
        body, name="ret_bwd", grid=(nb,),
        in_specs=[col(2), col(3), col(4), col(5), rope, rope, _full((1, D)),
                  pl.BlockSpec((None, HEADS, DK, DK), lambda i: (rev(i), 0, 0, 0)),
                  outc, pl.BlockSpec((D, D), lambda i: (1, 0)), outc, outc],
        out_specs=[pl.BlockSpec((CH, NIN), lambda i: (rev(i), 0)), outc, _full((1, D))],
        out_shape=[SDS((L, NIN), MXU), act_out, SDS((1, D), F32)],
        scratch_shapes=[pltpu.VMEM((HEADS, DK, DK), F32), pltpu.VMEM((HEADS, CH, CH), F32),
                        pltpu.VMEM((HEADS, CH, 256), F32)],
        compiler_params=_cp(48, ("arbitrary",)),
    )(p, p, p, p, cos, sin, gain, states, dx1, wout_e, du, daz)


def _out_even(x, ya, yb, wout):
    L = x.shape[0]
    tm = min(512, L)

    def body(x_ref, ya_ref, yb_ref, w_ref, o_ref):
        cat = jnp.concatenate([ya_ref[...], yb_ref[...]], axis=1)
        o_ref[...] = x_ref[...] + jnp.dot(cat, w_ref[...], preferred_element_type=F32)

    row = pl.BlockSpec((tm, D), lambda i: (i, 0))
    return pl.pallas_call(
        body, name="out_even", grid=(L // tm,), in_specs=[row, row, row, _full((DI, D))],
        out_specs=row, out_shape=SDS((L, D), F32), compiler_params=_cp(32, ("arbitrary",)),
    )(x, ya, yb, wout)


def _sgu_core(pv, gain, ws_ref, bs_ref):
    pu, pvv, z = pv[:, :DI], pv[:, DI:2 * DI], pv[:, 2 * DI:]
    u, gu = _gelu_and_grad(pu)
    v, gv = _gelu_and_grad(pvv)
    mu = jnp.mean(v, axis=-1, keepdims=True)
    vc = v - mu
    rstd = lax.rsqrt(jnp.mean(vc * vc, axis=-1, keepdims=True) + EPS)
    vhat = vc * rstd
    vn = vhat * gain
    t = lax.broadcasted_iota(jnp.int32, (CH, CH), 0)
    s_ = lax.broadcasted_iota(jnp.int32, (CH, CH), 1)
    mask = t >= s_
    wm = [jnp.where(mask, ws_ref[g], 0.0).astype(MXU) for g in range(SG)]
    s = jnp.concatenate([_mm(wm[g], vn[:, g * SGD:(g + 1) * SGD]) + bs_ref[g] for g in range(SG)], axis=1)
    return gu, gv, z, u, vhat, rstd, vn, mask, wm, s


def _sgu_fwd(p2, x1, gain, wsp, bsp, wout, fnorm, tgt):
    L = p2.shape[0]

    def body(p_ref, x1_ref, g_ref, ws_ref, bs_ref, wo_ref, fn_ref, t_ref, dx2_ref, gfn_ref, loss_ref):
        @pl.when(pl.program_id(0) == 0)
        def _():
            gfn_ref[...] = jnp.zeros_like(gfn_ref)
            loss_ref[...] = jnp.zeros_like(loss_ref)

        _, _, z, u, _, _, _, _, _, s = _sgu_core(p_ref[...], g_ref[...], ws_ref, bs_ref)
        act, _ = _silu_and_grad(z)
        x2 = x1_ref[...] + _mm(u * s * act, wo_ref[...])
        xhat, r = _rms(x2)
        fn = fn_ref[...]
        e = xhat * fn - t_ref[...]
        loss_ref[...] += 0.5 * jnp.sum(jnp.mean(e * e, axis=-1, keepdims=True), axis=0, keepdims=True)
        do = e * (1.0 / D)
        gfn_ref[...] += jnp.sum(do * xhat, axis=0, keepdims=True)
        dxhat = do * fn
        dx2_ref[...] = r * (dxhat - xhat * jnp.mean(dxhat * xhat, axis=-1, keepdims=True))

    row = pl.BlockSpec((CH, D), lambda i: (i, 0))
    return pl.pallas_call(
        body, name="sgu_fwd", grid=(L // CH,),
        in_specs=[pl.BlockSpec((CH, NIN), lambda i: (i, 0)), row, _full((1, DI)), _full((SG, CH, CH)),
                  _full((SG, CH, 1)), _full((DI, D)), _full((1, D)), row],
        out_specs=[row, _full((1, D)), _full((1, 1))],
        out_shape=[SDS((L, D), F32), SDS((1, D), F32), SDS((1, 1), F32)],
        compiler_params=_cp(48, ("arbitrary",)),
    )(p2, x1, gain, wsp, bsp, wout, fnorm, tgt)


def _sgu_bwd(p2, dx2, gain, wsp, bsp, wout):
    L = p2.shape[0]

    def body(p_ref, dx2_ref, g_ref, ws_ref, bs_ref, wo_ref, dp_ref, y_ref, gg_ref, gws_ref, gbs_ref):
        @pl.when(pl.program_id(0) == 0)
        def _():
            gg_ref[...] = jnp.zeros_like(gg_ref)
            gws_ref[...] = jnp.zeros_like(gws_ref)
            gbs_ref[...] = jnp.zeros_like(gbs_ref)

        gain = g_ref[...]
        gu, gv, z, u, vhat, rstd, vn, mask, wm, s = _sgu_core(p_ref[...], gain, ws_ref, bs_ref)
        act, dact = _silu_and_grad(z)
        y_ref[...] = (u * s * act).astype(y_ref.dtype)
        dy = _mm_nt(dx2_ref[...], wo_ref[...])
        du = dy * s * act
        ds = dy * u * act
        dz = dy * u * s * dact
        dvn = []
        for g in range(SG):
            ds_g = ds[:, g * SGD:(g + 1) * SGD]
            vn_g = vn[:, g * SGD:(g + 1) * SGD]
            gbs_ref[g] += jnp.sum(ds_g, axis=1, keepdims=True)
            gws_ref[g] += jnp.where(mask, _mm_nt(ds_g, vn_g), 0.0)
            dvn.append(_mm_tn(wm[g], ds_g))
        dvn = jnp.concatenate(dvn, axis=1)
        gg_ref[...] += jnp.sum(dvn * vhat, axis=0, keepdims=True)
        dvhat = dvn * gain
        dv = rstd * (dvhat - jnp.mean(dvhat, axis=-1, keepdims=True)
                     - vhat * jnp.mean(dvhat * vhat, axis=-1, keepdims=True))
        dp_ref[...] = jnp.concatenate([du * gu, dv * gv, dz], axis=1).astype(dp_ref.dtype)

    return pl.pallas_call(
        body, name="sgu_bwd", grid=(L // CH,),
        in_specs=[pl.BlockSpec((CH, NIN), lambda i: (i, 0)), pl.BlockSpec((CH, D), lambda i: (i, 0)),
                  _full((1, DI)), _full((SG, CH, CH)), _full((SG, CH, 1)), _full((DI, D))],
        out_specs=[pl.BlockSpec((CH, NIN), lambda i: (i, 0)), pl.BlockSpec((CH, DI), lambda i: (i, 0)),
                   _full((1, DI)), _full((SG, CH, CH)), _full((SG, CH, 1))],
        out_shape=[SDS((L, NIN), MXU), SDS((L, DI), MXU), SDS((1, DI), F32), SDS((SG, CH, CH), F32),
                   SDS((SG, CH, 1), F32)],
        compiler_params=_cp(48, ("arbitrary",)),
    )(p2, dx2, gain, wsp, bsp, wout)


def _my_index():
    return 4 * lax.axis_index("x") + 2 * lax.axis_index("y") + lax.axis_index("c")


def _ordered_sum(land_ref, own, me):
    g = None
    for s in range(NDEV):
        part = jnp.where(me == s, own, land_ref[s].astype(F32))
        g = part if g is None else g + part
    return g


def _adamw_math(w, m, v, g):
    mn = ADAM_B1 * m + (1.0 - ADAM_B1) * g
    vn = ADAM_B2 * v + (1.0 - ADAM_B2) * (g * g)
    mhat = mn / BC1
    vhat = vn / BC2
    return g, -ADAM_LR * (mhat / (jnp.sqrt(vhat) + ADAM_EPS) + ADAM_WD * w), mn, vn


def _adamw(w, m, v, land, own, name):
    R, C = w.shape
    tr = R
    for cand in (256, 128, 64, 32, 16, 8):
        if R % cand == 0 and R > cand:
            tr = cand
            break

    def body(w_ref, m_ref, v_ref, land_ref, own_ref, g_ref, d_ref, mo_ref, vo_ref):
        g = _ordered_sum(land_ref, own_ref[...].astype(F32), _my_index())
        for o, val in zip((g_ref, d_ref, mo_ref, vo_ref), _adamw_math(w_ref[...], m_ref[...], v_ref[...], g)):
            o[...] = val

    blk = pl.BlockSpec((tr, C), lambda i: (i, 0))
    out = SDS((R, C), F32)
    return pl.pallas_call(
        body, name=name, grid=(R // tr,),
        in_specs=[blk, blk, blk, pl.BlockSpec((NDEV, tr, C), lambda i: (0, i, 0)), blk],
        out_specs=[blk, blk, blk, blk], out_shape=[out, out, out, out],
        compiler_params=_cp(40, ("arbitrary",)),
    )(w, m, v, land, own)


def _adamw_many(ws, ms, vs, lands, owns, name):
    k = len(ws)

    def body(*refs):
        ins, outs = refs[:5 * k], refs[5 * k:]
        me = _my_index()
        for i in range(k):
            w_ref, m_ref, v_ref, land_ref, own_ref = (ins[j * k + i] for j in range(5))
            g = _ordered_sum(land_ref, own_ref[...], me)
            for j, val in enumerate(_adamw_math(w_ref[...], m_ref[...], v_ref[...], g)):
                outs[j * k + i][...] = val

    out_shape = [SDS(w.shape, F32) for _ in range(4) for w in ws]
    res = pl.pallas_call(body, name=name, out_shape=out_shape, compiler_params=_cp(60))(*ws, *ms, *vs, *lands, *owns)
    return [res[j * k:(j + 1) * k] for j in range(4)]


MESH = pl.DeviceIdType.MESH
HBM_SPEC = pl.BlockSpec(memory_space=pltpu.HBM)
SEM_SPEC = pl.BlockSpec(memory_space=pltpu.SEMAPHORE)
EFFECT = pltpu.SideEffectType.DATAFLOW_SIDE_EFFECTING


def _me_and_peers():
    x, y, c = lax.axis_index("x"), lax.axis_index("y"), lax.axis_index("c")
    me = 4 * x + 2 * y + c
    peers = []
    for r in range(1, NDEV):
        px, py, pc = x ^ ((r >> 2) & 1), y ^ ((r >> 1) & 1), c ^ (r & 1)
        peers.append(((px, py, pc), 4 * px + 2 * py + pc))
    return me, peers


def _land_shape(a, scatter):
    return (NDEV,) + (a.shape[1:] if scatter else a.shape)


def _remote(src, dst, send_sems, recv_sems, r, k, n, dev):
    i = r * n + k
    return pltpu.make_async_remote_copy(src_ref=src, dst_ref=dst, send_sem=send_sems.at[i], recv_sem=recv_sems.at[i],
                                        device_id=dev, device_id_type=MESH)


def _exchange(arrays, scatter, name):
    n = len(arrays)
    out_shape = [SDS(_land_shape(a, scatter), a.dtype) for a in arrays]

    def body(*refs):
        ins, outs = refs[:n], refs[n:2 * n]
        send_sems, recv_sems, loc_sems = refs[2 * n:]
        me, peers = _me_and_peers()
        local = []
        for k in range(n):
            src = ins[k].at[me] if scatter else ins[k]
            cp = pltpu.make_async_copy(src, outs[k].at[me], loc_sems.at[k])
            cp.start()
            local.append(cp)
        sends = []
        for r, (dev, lin) in enumerate(peers):
            for k in range(n):
                src = ins[k].at[lin] if scatter else ins[k]
                cp = _remote(src, outs[k].at[me], send_sems, recv_sems, r, k, n, dev)
                cp.start()
                sends.append(cp)
        for r, (dev, lin) in enumerate(peers):
            for k in range(n):
                src = ins[k].at[me] if scatter else ins[k]
                _remote(src, outs[k].at[lin], send_sems, recv_sems, r, k, n, dev).wait_recv()
        for cp in sends:
            cp.wait_send()
        for cp in local:
            cp.wait()

    return pl.pallas_call(
        body, name=name, in_specs=[HBM_SPEC] * n, out_specs=[HBM_SPEC] * n, out_shape=out_shape,
        scratch_shapes=[pltpu.SemaphoreType.DMA(((NDEV - 1) * n,)), pltpu.SemaphoreType.DMA(((NDEV - 1) * n,)),
                        pltpu.SemaphoreType.DMA((n,))],
    )(*arrays)


def _exchange_start(arrays, scatter, name):
    n = len(arrays)
    lands = [lax.empty(_land_shape(a, scatter), a.dtype) for a in arrays]

    def body(*refs):
        ins, lnd = refs[:n], refs[n:2 * n]
        send_sems, recv_sems = refs[2 * n], refs[2 * n + 1]
        token = refs[-1]
        me, peers = _me_and_peers()
        for r, (dev, lin) in enumerate(peers):
            for k in range(n):
                src = ins[k].at[lin] if scatter else ins[k]
                _remote(src, lnd[k].at[me], send_sems, recv_sems, r, k, n, dev).start()
        token[...] = jnp.zeros_like(token)

    sem = pltpu.SemaphoreType.DMA(((NDEV - 1) * n,))
    outs = pl.pallas_call(
        body, name=name,
        out_shape=(sem, sem, *[pltpu.HBM(a.shape, a.dtype) for a in arrays],
                   *[pltpu.HBM(l.shape, l.dtype) for l in lands], SDS((8, 128), F32)),
        in_specs=[HBM_SPEC] * (2 * n),
        out_specs=(SEM_SPEC, SEM_SPEC, *[HBM_SPEC] * (2 * n), pl.BlockSpec(memory_space=pltpu.VMEM)),
        input_output_aliases={k: 2 + k for k in range(2 * n)},
        compiler_params=pltpu.CompilerParams(has_side_effects=EFFECT),
    )(*[pltpu.with_memory_space_constraint(a, pltpu.HBM) for a in arrays],
      *[pltpu.with_memory_space_constraint(l, pltpu.HBM) for l in lands])
    return (n, scatter, outs[0], outs[1], outs[2:2 + n], outs[2 + n:2 + 2 * n]), outs[-1]


def _exchange_wait(handle, after, name):
    n, scatter, send_sems, recv_sems, thru, lands = handle
    after = tuple(after)

    def body(*refs):
        ins, lnd = refs[:n], refs[n:2 * n]
        send_sems, recv_sems = refs[2 * n], refs[2 * n + 1]
        me, peers = _me_and_peers()
        for r, (dev, lin) in enumerate(peers):
            for k in range(n):
                src = ins[k].at[lin] if scatter else ins[k]
                cp = _remote(src, lnd[k].at[lin], send_sems, recv_sems, r, k, n, dev)
                cp.wait_send()
                cp.wait_recv()

    outs = pl.pallas_call(
        body, name=name,
        out_shape=(*[pltpu.HBM(a.shape, a.dtype) for a in thru], *[pltpu.HBM(l.shape, l.dtype) for l in lands]),
        in_specs=[HBM_SPEC] * (2 * n) + [SEM_SPEC, SEM_SPEC] + [ANY_SPEC] * len(after),
        out_specs=tuple([HBM_SPEC] * (2 * n)),
        input_output_aliases={k: k for k in range(2 * n)},
        compiler_params=pltpu.CompilerParams(has_side_effects=EFFECT),
    )(*thru, *lands, send_sems, recv_sems, *after)
    return list(outs[:n]), list(outs[n:])


def _fill_own(lands, owns, name):
    n = len(lands)

    def body(*refs):
        own, outs, sems = refs[n:2 * n], refs[2 * n:3 * n], refs[3 * n]
        me = _my_index()
        cps = [pltpu.make_async_copy(own[k], outs[k].at[me], sems.at[k]) for k in range(n)]
        for cp in cps:
            cp.start()
        for cp in cps:
            cp.wait()

    return pl.pallas_call(
        body, name=name, in_specs=[HBM_SPEC] * (2 * n), out_specs=[HBM_SPEC] * n,
        out_shape=[SDS(l.shape, l.dtype) for l in lands], input_output_aliases={k: k for k in range(n)},
        scratch_shapes=[pltpu.SemaphoreType.DMA((n,))],
    )(*lands, *owns)


def _local_step(x, tgt, norm_even, win_e, lam_re, lam_im, log_dt, b_re, b_im, c_re, c_im, s5_d, bglu,
                ret_gain, wsp, bsp, fnorm, late_weights, emit, start_token=None):
    L = x.shape[0]
    lr3, li3 = lam_re.reshape(G, 1, P), lam_im.reshape(G, 1, P)
    dt3 = log_dt.reshape(G, 1, 1)
    br3, bi3 = jnp.swapaxes(b_re, 1, 2), jnp.swapaxes(b_im, 1, 2)
    abr3, abi3, bbr3, bbi3 = _s5_disc(lr3, li3, dt3, br3, bi3)
    bb = jnp.concatenate([_embed(bbr3), _embed(bbi3)], axis=2).astype(MXU)
    cm = jnp.concatenate([_embed(jnp.swapaxes(c_re, 1, 2)), -_embed(jnp.swapaxes(c_im, 1, 2))], axis=1).astype(MXU)
    abr, abi = abr3.reshape(1, NSTATE), abi3.reshape(1, NSTATE)
    pwr, pwi, _, _ = _s5_tables(abr, abi, NT, "s5_tables_step")
    par, pai, pbr, pbi = _s5_tables(pwr[NT - 1:NT], pwi[NT - 1:NT], CH, "s5_tables_chunk")
    inv = (ROPE_BASE ** (-jnp.arange(DK // 2, dtype=F32) / (DK // 2))).reshape(1, DK // 2)
    cos, sin = _rope_tables(L, inv)
    bsp3 = bsp.reshape(SG, CH, 1)

    def dep(token):
        return NO_DEPS if token is None else (token,)

    p = _in_proj(x, norm_even, win_e, "in_even", dep(start_token))
    ypre, s5_states = _s5_scan_fwd(p, bb, cm, abr, abi, par, pai, s5_d)
    yb, ret_states = _ret_fwd(p, cos, sin, ret_gain)
    wglu, wout_e, norm_odd, win_o, sgu_gain, wout_o = late_weights((ypre, yb))
    ya = _s5_gate_fwd(ypre, p, wglu, bglu)
    x1 = _out_even(x, ya, yb, wout_e)
    p2 = _in_proj(x1, norm_odd, win_o, "in_odd")
    dx2, g_fnorm, loss = _sgu_fwd(p2, x1, sgu_gain, wsp, bsp3, wout_o, fnorm, tgt)

    dp2, y_o, g_sgu_gain, g_wsp, g_bsp = _sgu_bwd(p2, dx2, sgu_gain, wsp, bsp3, wout_o)
    g_wout_o = _wgrad_rows([y_o], dx2, "wgrad_out_odd")
    g_win_o = _wgrad_cols(x1, norm_odd, dp2, "wgrad_in_odd")
    tok = emit("odd", dict(w_in_odd=g_win_o, w_out_odd=g_wout_o))
    dx1, g_norm_odd = _in_proj_bwd_x(dp2, x1, norm_odd, win_o, dx2, "in_odd_bwd", dep(tok))

    dypre, daz, yg, dt, ya2, g_bglu = _s5_gate_bwd(ypre, p, dx1, wout_e, wglu, bglu)
    du, g_d, g_cm, g_bb, g_ar, g_ai = _s5_scan_bwd(p, dypre, s5_states, bb, cm, abr, abi, par, pai, pbr, pbi, s5_d)
    dbbr3 = _diag_blocks(g_bb[:, :, :SW], HG, P)
    dbbi3 = _diag_blocks(g_bb[:, :, SW:], HG, P)
    g_c_re = jnp.swapaxes(_diag_blocks(g_cm[:, :SW, :], P, HG), 1, 2)
    g_c_im = -jnp.swapaxes(_diag_blocks(g_cm[:, SW:, :], P, HG), 1, 2)
    g_lr3, g_li3, g_dt3, g_br3, g_bi3 = _s5_disc_bwd(
        lr3, li3, dt3, br3, bi3, g_ar.reshape(G, 1, P), g_ai.reshape(G, 1, P), dbbr3, dbbi3)
    dp, yb2, g_ret_gain = _ret_bwd(p, cos, sin, ret_gain, ret_states, dx1, wout_e, du, daz)
    small = dict(
        s5_lam_re=g_lr3.reshape(G, P), s5_lam_im=g_li3.reshape(G, P),
        s5_log_dt=g_dt3.reshape(1, G), s5_b_re=g_br3, s5_b_im=g_bi3,
        s5_c_re=g_c_re, s5_c_im=g_c_im, s5_d=g_d, s5_b_glu=g_bglu, ret_gn_gain=g_ret_gain,
        norm_odd=g_norm_odd, sgu_norm_gain=g_sgu_gain, sgu_w_spatial=g_wsp, sgu_b_spatial=g_bsp.reshape(SG, CH),
        final_norm=g_fnorm)
    tok = emit("small", small)
    g_wout_e = _wgrad_rows([ya2, yb2], dx1, "wgrad_out_even", dep(tok))
    g_wglu = _wgrad_rows([yg], dt, "wgrad_glu", dep(tok))
    tok = emit("even_rows", dict(s5_w_glu=g_wglu, w_out_even=g_wout_e))
    g_win_e = _wgrad_cols(x, norm_even, dp, "wgrad_in_even", dep(tok))
    tok = emit("even_cols", dict(w_in_even=g_win_e))
    dx, g_norm_even = _in_proj_bwd_x(dp, x, norm_even, win_e, dx1, "in_even_bwd", dep(tok))
    emit("last", dict(norm_even=g_norm_even))
    return loss, dx


WEIGHTS = ['norm_even', 'w_in_even', 's5_lam_re', 's5_lam_im', 's5_log_dt', 's5_b_re', 's5_b_im', 's5_c_re',
           's5_c_im', 's5_d', 's5_w_glu', 's5_b_glu', 'ret_gn_gain', 'w_out_even', 'norm_odd', 'w_in_odd',
           'sgu_norm_gain', 'sgu_w_spatial', 'sgu_b_spatial', 'w_out_odd', 'final_norm']
BIG = ['w_in_even', 's5_w_glu', 'w_out_even', 'w_in_odd', 'w_out_odd']
SHARDED_SMALL = {'norm_odd': D // NDEV, 'sgu_norm_gain': DI // NDEV}
SMALL = [n for n in WEIGHTS if n not in BIG and n != 'norm_even']


def _view(n, a):
    if n in ('s5_b_re', 's5_b_im'):
        return jnp.swapaxes(a[0], 1, 2)
    if n == 'final_norm':
        return a.reshape(1, D)
    return a[0] if a.ndim >= 3 else a


def _unview(n, t, shape):
    if n in ('s5_b_re', 's5_b_im'):
        return jnp.swapaxes(t, 1, 2)[None]
    return t.reshape(shape)


def kernel(x, norm_even, w_in_even, s5_lam_re, s5_lam_im, s5_log_dt, s5_b_re, s5_b_im, s5_c_re, s5_c_im, s5_d, s5_w_glu, s5_b_glu, ret_gn_gain, w_out_even, norm_odd, w_in_odd, sgu_norm_gain, sgu_w_spatial, sgu_b_spatial, w_out_odd, final_norm, loss_target, m_norm_even, m_w_in_even, m_s5_lam_re, m_s5_lam_im, m_s5_log_dt, m_s5_b_re, m_s5_b_im, m_s5_c_re, m_s5_c_im, m_s5_d, m_s5_w_glu, m_s5_b_glu, m_ret_gn_gain, m_w_out_even, m_norm_odd, m_w_in_odd, m_sgu_norm_gain, m_sgu_w_spatial, m_sgu_b_spatial, m_w_out_odd, m_final_norm, v_norm_even, v_w_in_even, v_s5_lam_re, v_s5_lam_im, v_s5_log_dt, v_s5_b_re, v_s5_b_im, v_s5_c_re, v_s5_c_im, v_s5_d, v_s5_w_glu, v_s5_b_glu, v_ret_gn_gain, v_w_out_even, v_norm_odd, v_w_in_odd, v_sgu_norm_gain, v_sgu_w_spatial, v_sgu_b_spatial, v_w_out_odd, v_final_norm):
    args = dict(locals())
    w = {n: args[n] for n in WEIGHTS}
    m = {n: args["m_" + n] for n in WEIGHTS}
    v = {n: args["v_" + n] for n in WEIGHTS}
    me = _my_index()

    (win_e,) = _exchange([w['w_in_even'][0].astype(MXU)], False, "gather_w_in_even")
    late_own = [w['s5_w_glu'][0].astype(MXU), w['w_out_even'][0].astype(MXU), w['norm_odd'],
                w['w_in_odd'][0].astype(MXU), w['sgu_norm_gain'], w['w_out_odd'][0].astype(MXU)]
    late_handle, start_token = _exchange_start(late_own, False, "gather_late_start")

    def late_weights(after):
        owns, lands = _exchange_wait(late_handle, after, "gather_late_wait")
        wglu, wout_e, nodd, win_o, sgug, wout_o = _fill_own(lands, owns, "gather_late_own")
        return (wglu.reshape(D, D), wout_e.reshape(DI, D), nodd.reshape(1, D), win_o, sgug.reshape(1, DI),
                wout_o.reshape(DI, D))

    pending = {}
    small_last = {}

    def emit(stage, grads):
        if stage == "last":
            small_last.update(grads)
            return None
        names = list(grads) if stage != "small" else SMALL
        handle, token = _exchange_start([grads[n] for n in names], stage != "small", stage + "_start")
        pending[stage] = (handle, names)
        return token

    loss, dx = _local_step(
        x[0], loss_target[0], w['norm_even'], win_e, w['s5_lam_re'][0], w['s5_lam_im'][0], w['s5_log_dt'][0],
        w['s5_b_re'][0], w['s5_b_im'][0], w['s5_c_re'][0], w['s5_c_im'][0], w['s5_d'], w['s5_b_glu'],
        w['ret_gn_gain'], w['sgu_w_spatial'][0], w['sgu_b_spatial'][0], w['final_norm'].reshape(1, D),
        late_weights, emit, start_token)

    out_g, out_d, out_m, out_v = {}, {}, {}, {}
    after = dx
    for stage in ("odd", "even_rows", "even_cols"):
        handle, names = pending[stage]
        sent, lands = _exchange_wait(handle, (after,), stage + "_wait")
        for n, land, stack in zip(names, lands, sent):
            shp = w[n].shape
            r, c = shp[1], shp[2]
            own = lax.dynamic_index_in_dim(stack, me, 0, keepdims=False)
            res = _adamw(w[n].reshape(r, c), m[n].reshape(r, c), v[n].reshape(r, c), land, own, "adamw_" + n)
            out_g[n], out_d[n], out_m[n], out_v[n] = (t.reshape(shp) for t in res)
            after = res[0]

    handle, names = pending["small"]
    owns, lands = _exchange_wait(handle, (after,), "small_wait")
    (gne,) = _exchange([small_last['norm_even']], False, "gather_norm_even")
    names = names + ['norm_even']
    owns = owns + [small_last['norm_even']]
    lands = lands + [gne]
    for i, n in enumerate(names):
        if n in SHARDED_SMALL:
            width = SHARDED_SMALL[n]
            owns[i] = lax.dynamic_slice_in_dim(owns[i], me * width, width, axis=1)
            lands[i] = lax.dynamic_slice_in_dim(lands[i], me * width, width, axis=2)
    res = _adamw_many([_view(n, w[n]) for n in names], [_view(n, m[n]) for n in names],
                      [_view(n, v[n]) for n in names], lands, owns, "adamw_small")
    for dst, vals in zip((out_g, out_d, out_m, out_v), res):
        for n, t in zip(names, vals):
            dst[n] = _unview(n, t, w[n].shape)

    loss_total = lax.psum(loss[0, 0], AXES)
    return (loss_total, dx[None], *[out_g[n] for n in WEIGHTS], *[out_d[n] for n in WEIGHTS],
            *[out_m[n] for n in WEIGHTS], *[out_v[n] for n in WEIGHTS])
```

```python
import math

import jax
import jax.numpy as jnp
from jax import lax
from jax.experimental import pallas as pl
from jax.experimental.pallas import tpu as pltpu

F32 = jnp.float32
MXU = jnp.bfloat16
AXES = ("x", "y", "c")
NDEV = 8
D = 1024
NIN = 6144
WIN_BLK = NIN // NDEV
DI = 2048
G, P, HG = 64, 64, 16
GB = 8
NJ = G // GB
SW = GB * P
UW = GB * HG
NSTATE = G * P
HEADS, DK = 4, 256
CH = 128
SG, SGD = 4, 512
EPS = 1e-6
ROPE_BASE = 10000.0
VMEM_CAP_V7X = 64 * 1024 * 1024
LOG_G = [math.log1p(-2.0 ** (-5.0 - h)) for h in range(HEADS)]
GELU_C = math.sqrt(2.0 / math.pi)

ADAM_LR, ADAM_B1, ADAM_B2, ADAM_EPS, ADAM_WD, ADAM_STEP = 0.001, 0.9, 0.999, 1e-08, 0.01, 10
BC1 = 1.0 - ADAM_B1 ** ADAM_STEP
BC2 = 1.0 - ADAM_B2 ** ADAM_STEP

SDS = jax.ShapeDtypeStruct
ARB2 = ("arbitrary", "arbitrary")


def _cp(vmem_mib, sem=None):
    kw = dict(vmem_limit_bytes=min(vmem_mib * 1024 * 1024, VMEM_CAP_V7X - 4 * 1024 * 1024))
    if sem is not None:
        kw["dimension_semantics"] = sem
    return pltpu.CompilerParams(**kw)


def _mm(a, b):
    return jnp.dot(a.astype(MXU), b.astype(MXU), preferred_element_type=F32)


def _mm_nt(a, b):
    return lax.dot_general(a.astype(MXU), b.astype(MXU), (((1,), (1,)), ((), ())), preferred_element_type=F32)


def _mm_tn(a, b):
    return lax.dot_general(a.astype(MXU), b.astype(MXU), (((0,), (0,)), ((), ())), preferred_element_type=F32)


def _gelu(x):
    return _gelu_and_grad(x)[0]


def _gelu_and_grad(x):
    x2 = x * x
    th = jnp.tanh(GELU_C * x * (1.0 + 0.044715 * x2))
    hp = 0.5 * (1.0 + th)
    return x * hp, hp + 0.5 * x * (1.0 - th * th) * GELU_C * (1.0 + 3.0 * 0.044715 * x2)


def _silu_and_grad(x):
    s = jax.nn.sigmoid(x)
    return x * s, s * (1.0 + x * (1.0 - s))


def _full(shape):
    nd = len(shape)
    return pl.BlockSpec(shape, lambda *_: (0,) * nd)


def _rms(xf):
    r = lax.rsqrt(jnp.mean(xf * xf, axis=-1, keepdims=True) + EPS)
    return xf * r, r


ANY_SPEC = pl.BlockSpec(memory_space=pl.ANY)
NO_DEPS = ()


WIN_PER_STEP = 4


def _in_proj(x, gain, wst, name, deps=NO_DEPS):
    L = x.shape[0]
    tm = min(512, L)
    nc = WIN_PER_STEP

    def body(x_ref, g_ref, w_ref, *rest):
        o_ref, h_scr = rest[len(deps):]

        @pl.when(pl.program_id(1) == 0)
        def _():
            xhat, _ = _rms(x_ref[...])
            h_scr[...] = (xhat * g_ref[...]).astype(MXU)

        h = h_scr[...]
        for c in range(nc):
            o_ref[:, c * WIN_BLK:(c + 1) * WIN_BLK] = jnp.dot(h, w_ref[c], preferred_element_type=F32)

    return pl.pallas_call(
        body, name=name, grid=(L // tm, NDEV // nc),
        in_specs=[pl.BlockSpec((tm, D), lambda i, n: (i, 0)), _full((1, D)),
                  pl.BlockSpec((nc, D, WIN_BLK), lambda i, n: (n, 0, 0))] + [ANY_SPEC] * len(deps),
        out_specs=pl.BlockSpec((tm, nc * WIN_BLK), lambda i, n: (i, n)),
        out_shape=SDS((L, NIN), F32),
        scratch_shapes=[pltpu.VMEM((tm, D), MXU)],
        compiler_params=_cp(48, ARB2),
    )(x, gain, wst, *deps)


def _in_proj_bwd_x(dp, x, gain, wst, dres, name, deps=NO_DEPS):
    L = x.shape[0]
    tm = min(512, L)
    nc = WIN_PER_STEP
    nn = NDEV // nc

    def body(dp_ref, x_ref, g_ref, w_ref, dres_ref, *rest):
        dx_ref, gg_ref, acc = rest[len(deps):]
        i, n = pl.program_id(0), pl.program_id(1)

        @pl.when((i == 0) & (n == 0))
        def _():
            gg_ref[...] = jnp.zeros_like(gg_ref)

        part = _mm_nt(dp_ref[:, 0:WIN_BLK], w_ref[0])
        for c in range(1, nc):
            part += _mm_nt(dp_ref[:, c * WIN_BLK:(c + 1) * WIN_BLK], w_ref[c])

        @pl.when(n == 0)
        def _():
            acc[...] = part

        @pl.when(n > 0)
        def _():
            acc[...] += part

        @pl.when(n == nn - 1)
        def _():
            xhat, r = _rms(x_ref[...])
            dh = acc[...]
            dxhat = dh * g_ref[...]
            dx_ref[...] = dres_ref[...] + r * (dxhat - xhat * jnp.mean(dxhat * xhat, axis=-1, keepdims=True))
            gg_ref[...] += jnp.sum(dh * xhat, axis=0, keepdims=True)

    return pl.pallas_call(
        body, name=name, grid=(L // tm, nn),
        in_specs=[pl.BlockSpec((tm, nc * WIN_BLK), lambda i, n: (i, n)),
                  pl.BlockSpec((tm, D), lambda i, n: (i, 0)), _full((1, D)),
                  pl.BlockSpec((nc, D, WIN_BLK), lambda i, n: (n, 0, 0)),
                  pl.BlockSpec((tm, D), lambda i, n: (i, 0))] + [ANY_SPEC] * len(deps),
        out_specs=[pl.BlockSpec((tm, D), lambda i, n: (i, 0)), _full((1, D))],
        out_shape=[SDS((L, D), F32), SDS((1, D), F32)],
        scratch_shapes=[pltpu.VMEM((tm, D), F32)],
        compiler_params=_cp(48, ARB2),
    )(dp, x, gain, wst, dres, *deps)


def _wgrad_cols(x, gain, dp, name, deps=NO_DEPS):
    L = x.shape[0]
    tk = min(512, L)
    nk = L // tk
    halves = 2
    nh = NDEV // halves

    def body(x_ref, g_ref, dp_ref, *rest):
        o_ref, acc = rest[len(deps):]
        k = pl.program_id(1)

        @pl.when(k == 0)
        def _():
            acc[...] = jnp.zeros_like(acc)

        xhat, _ = _rms(x_ref[...])
        acc[...] += _mm_tn(xhat * g_ref[...], dp_ref[...])

        @pl.when(k == nk - 1)
        def _():
            for c in range(nh):
                o_ref[c] = acc[:, c * WIN_BLK:(c + 1) * WIN_BLK].astype(o_ref.dtype)

    return pl.pallas_call(
        body, name=name, grid=(halves, nk),
        in_specs=[pl.BlockSpec((tk, D), lambda n, k: (k, 0)), _full((1, D)),
                  pl.BlockSpec((tk, nh * WIN_BLK), lambda n, k: (k, n))] + [ANY_SPEC] * len(deps),
        out_specs=pl.BlockSpec((nh, D, WIN_BLK), lambda n, k: (n, 0, 0)),
        out_shape=SDS((NDEV, D, WIN_BLK), MXU),
        scratch_shapes=[pltpu.VMEM((D, nh * WIN_BLK), F32)],
        compiler_params=_cp(56, ARB2),
    )(x, gain, dp, *deps)


def _wgrad_rows(a_parts, b, name, deps=NO_DEPS):
    L, N = b.shape
    na = len(a_parts)
    widths = [a.shape[1] for a in a_parts]
    M = sum(widths)
    tk = min(512, L)
    nk = L // tk

    def body(*refs):
        a_refs, b_ref = refs[:na], refs[na]
        o_ref, acc = refs[na + 1 + len(deps):]
        k = pl.program_id(0)

        @pl.when(k == 0)
        def _():
            acc[...] = jnp.zeros_like(acc)

        bv = b_ref[...].astype(MXU)
        off = 0
        for a_ref, wd in zip(a_refs, widths):
            acc[off:off + wd, :] += _mm_tn(a_ref[...], bv)
            off += wd

        @pl.when(k == nk - 1)
        def _():
            o_ref[...] = acc[...].astype(o_ref.dtype).reshape(o_ref.shape)

    return pl.pallas_call(
        body, name=name, grid=(nk,),
        in_specs=[pl.BlockSpec((tk, wd), lambda k: (k, 0)) for wd in widths]
        + [pl.BlockSpec((tk, N), lambda k: (k, 0))] + [ANY_SPEC] * len(deps),
        out_specs=_full((NDEV, M // NDEV, N)),
        out_shape=SDS((NDEV, M // NDEV, N), MXU),
        scratch_shapes=[pltpu.VMEM((M, N), F32)],
        compiler_params=_cp(48, ("arbitrary",)),
    )(*a_parts, b, *deps)


def _s5_disc_fn(lr_raw, li, logdt, br, bi):
    lr = jnp.minimum(lr_raw, -1e-4)
    dt = jnp.exp(logdt)
    mag = jnp.exp(lr * dt)
    abr = mag * jnp.cos(li * dt)
    abi = mag * jnp.sin(li * dt)
    den = lr * lr + li * li
    nre = abr - 1.0
    nim = abi
    zr = (nre * lr + nim * li) / den
    zi = (nim * lr - nre * li) / den
    return abr, abi, zr * br - zi * bi, zr * bi + zi * br


def _s5_disc(lr, li, logdt, br, bi):
    def body(lr_ref, li_ref, dt_ref, br_ref, bi_ref, abr_ref, abi_ref, bbr_ref, bbi_ref):
        abr, abi, bbr, bbi = _s5_disc_fn(lr_ref[...], li_ref[...], dt_ref[...], br_ref[...], bi_ref[...])
        abr_ref[...] = abr
        abi_ref[...] = abi
        bbr_ref[...] = bbr
        bbi_ref[...] = bbi

    s1, s3 = SDS((G, 1, P), F32), SDS((G, HG, P), F32)
    return pl.pallas_call(body, name="s5_disc", out_shape=[s1, s1, s3, s3])(lr, li, logdt, br, bi)


def _s5_disc_bwd(lr, li, logdt, br, bi, dabr, dabi, dbbr, dbbi):
    def body(lr_ref, li_ref, dt_ref, br_ref, bi_ref, c0, c1, c2, c3, o0, o1, o2, o3, o4):
        _, vjp = jax.vjp(_s5_disc_fn, lr_ref[...], li_ref[...], dt_ref[...], br_ref[...], bi_ref[...])
        g = vjp((c0[...], c1[...], c2[...], c3[...]))
        for o, v in zip((o0, o1, o2, o3, o4), g):
            o[...] = v

    s1, s3 = SDS((G, 1, P), F32), SDS((G, HG, P), F32)
    return pl.pallas_call(body, name="s5_disc_bwd", out_shape=[s1, s1, SDS((G, 1, 1), F32), s3, s3])(
        lr, li, logdt, br, bi, dabr, dabi, dbbr, dbbi)


def _s5_tables(abr, abi, rows, name):
    def body(ar_ref, ai_ref, pfr, pfi, pbr, pbi):
        pfr[0:1, :] = ar_ref[...]
        pfi[0:1, :] = ai_ref[...]
        pbr[rows - 1:rows, :] = ar_ref[...]
        pbi[rows - 1:rows, :] = ai_ref[...]
        n = 1
        while n < rows:
            er, ei = pfr[n - 1:n, :], pfi[n - 1:n, :]
            xr, xi = pfr[0:n, :], pfi[0:n, :]
            pfr[n:2 * n, :] = er * xr - ei * xi
            pfi[n:2 * n, :] = er * xi + ei * xr
            yr, yi = pbr[rows - n:rows, :], pbi[rows - n:rows, :]
            pbr[rows - 2 * n:rows - n, :] = er * yr - ei * yi
            pbi[rows - 2 * n:rows - n, :] = er * yi + ei * yr
            n *= 2

    s = SDS((rows, NSTATE), F32)
    return pl.pallas_call(body, name=name, out_shape=[s, s, s, s], compiler_params=_cp(40))(abr, abi)


def _cscan(br, bi, pr_ref, pi_ref, reverse):
    T = br.shape[0]
    sign = -1.0 if reverse else 1.0
    row = lax.broadcasted_iota(jnp.int32, br.shape, 0)
    k = 1
    while k < T:
        akr = pr_ref[k - 1:k, :]
        aki = sign * pi_ref[k - 1:k, :]

        def shift(v):
            if k % 8 == 0:
                z = jnp.zeros((k, v.shape[1]), v.dtype)
                return jnp.concatenate([v[k:], z], 0) if reverse else jnp.concatenate([z, v[:T - k]], 0)
            if reverse:
                return jnp.where(row < T - k, pltpu.roll(v, T - k, 0), 0.0)
            return jnp.where(row >= k, pltpu.roll(v, k, 0), 0.0)

        sr, si = shift(br), shift(bi)
        br, bi = br + akr * sr - aki * si, bi + akr * si + aki * sr
        k *= 2
    return br, bi


def _embed(t):
    a, b = t.shape[1], t.shape[2]
    return jnp.einsum("jgab,gh->jgahb", t.reshape(NJ, GB, a, b), jnp.eye(GB, dtype=t.dtype)).reshape(NJ, GB * a, GB * b)


def _diag_blocks(t, a, b):
    return jnp.einsum("jgahb,gh->jgab", t.reshape(NJ, GB, a, GB, b), jnp.eye(GB, dtype=t.dtype)).reshape(G, a, b)


NT = 16


def _chunks(L):
    ncb = min(CH, L // NT)
    return ncb, NT * ncb


def _cmul_add(ar, ai, xr, xi, br, bi):
    return ar * xr - ai * xi + br, ar * xi + ai * xr + bi


def _s5_states(u_ref, bb_ref, ar, ai, par_ref, pai_ref, cr, ci, ncb, bu_scr):
    er = ei = None
    for t in range(NT):
        bu = _mm(u_ref[pl.ds(t, ncb, stride=NT), :], bb_ref[...])
        bu_scr[t] = bu
        if t == 0:
            er, ei = bu[:, :SW], bu[:, SW:]
        else:
            er, ei = _cmul_add(ar, ai, er, ei, bu[:, :SW], bu[:, SW:])
    xr, xi = _cscan(er, ei, par_ref, pai_ref, False)
    fr, fi = _cmul_add(par_ref[0:ncb, :], pai_ref[0:ncb, :], cr, ci, xr, xi)
    row = lax.broadcasted_iota(jnp.int32, fr.shape, 0)
    cinr = jnp.where(row >= 1, pltpu.roll(fr, 1, 0), cr)
    cini = jnp.where(row >= 1, pltpu.roll(fi, 1, 0), ci)
    return cinr, cini, jnp.concatenate([fr[ncb - 1:ncb, :], fi[ncb - 1:ncb, :]], axis=1)


def _s5_scan_fwd(p, bb, cm, abr, abi, par, pai, dskip):
    L = p.shape[0]
    ncb, tb = _chunks(L)
    nb = L // tb

    def body(u_ref, bb_ref, cm_ref, ar_ref, ai_ref, par_ref, pai_ref, d_ref, ypre_ref, st_ref, carry, bu_scr):
        @pl.when(pl.program_id(1) == 0)
        def _():
            carry[...] = jnp.zeros_like(carry)

        c = carry[...]
        st_ref[...] = c
        ar, ai = ar_ref[...], ai_ref[...]
        sr, si, cnext = _s5_states(u_ref, bb_ref, ar, ai, par_ref, pai_ref, c[:, :SW], c[:, SW:], ncb, bu_scr)
        carry[...] = cnext
        for t in range(NT):
            bu = bu_scr[t]
            sr, si = _cmul_add(ar, ai, sr, si, bu[:, :SW], bu[:, SW:])
            rows = pl.ds(t, ncb, stride=NT)
            ypre_ref[rows, :] = _mm(jnp.concatenate([sr, si], axis=1), cm_ref[...]) + d_ref[...] * u_ref[rows, :]

    tab = pl.BlockSpec((CH, SW), lambda j, i: (0, j))
    vec = lambda w: pl.BlockSpec((1, w), lambda j, i: (0, j))
    return pl.pallas_call(
        body, name="s5_scan_fwd", grid=(NJ, nb),
        in_specs=[pl.BlockSpec((tb, UW), lambda j, i: (i, j)),
                  pl.BlockSpec((None, UW, 2 * SW), lambda j, i: (j, 0, 0)),
                  pl.BlockSpec((None, 2 * SW, UW), lambda j, i: (j, 0, 0)),
                  vec(SW), vec(SW), tab, tab, vec(UW)],
        out_specs=[pl.BlockSpec((tb, UW), lambda j, i: (i, j)),
                   pl.BlockSpec((None, None, 1, 2 * SW), lambda j, i: (j, i, 0, 0))],
        out_shape=[SDS((L, D), F32), SDS((NJ, nb, 1, 2 * SW), F32)],
        scratch_shapes=[pltpu.VMEM((1, 2 * SW), F32), pltpu.VMEM((NT, ncb, 2 * SW), F32)],
        compiler_params=_cp(40, ARB2),
    )(p, bb, cm, abr, abi, par, pai, dskip)


def _s5_gate_fwd(ypre, p, wglu, bglu):
    L = ypre.shape[0]
    tm = min(256, L)

    def body(y_ref, az_ref, wg_ref, bg_ref, ya_ref):
        yg = _gelu(y_ref[...])
        t = _mm(yg, wg_ref[...]) + bg_ref[...]
        act, _ = _silu_and_grad(az_ref[...])
        ya_ref[...] = (yg * jax.nn.sigmoid(t) * act).astype(ya_ref.dtype)

    return pl.pallas_call(
        body, name="s5_gate_fwd", grid=(L // tm,),
        in_specs=[pl.BlockSpec((tm, D), lambda i: (i, 0)), pl.BlockSpec((tm, D), lambda i: (i, 1)),
                  _full((D, D)), _full((1, D))],
        out_specs=pl.BlockSpec((tm, D), lambda i: (i, 0)),
        out_shape=SDS((L, D), MXU),
        compiler_params=_cp(32, ("arbitrary",)),
    )(ypre, p, wglu, bglu)


def _s5_gate_bwd(ypre, p, dx1, wout_e, wglu, bglu):
    L = ypre.shape[0]
    tm = min(256, L)

    def body(y_ref, az_ref, dx1_ref, wo_ref, wg_ref, bg_ref, dyp_ref, daz_ref, yg_ref, dt_ref, ya_ref, gbg_ref):
        @pl.when(pl.program_id(0) == 0)
        def _():
            gbg_ref[...] = jnp.zeros_like(gbg_ref)

        yg, dgelu = _gelu_and_grad(y_ref[...])
        sg = jax.nn.sigmoid(_mm(yg, wg_ref[...]) + bg_ref[...])
        act, dact = _silu_and_grad(az_ref[...])
        y2 = yg * sg
        dya = _mm_nt(dx1_ref[...], wo_ref[...])
        daz_ref[...] = (dya * y2 * dact).astype(daz_ref.dtype)
        dy2 = dya * act
        dt = dy2 * yg * sg * (1.0 - sg)
        dyg = dy2 * sg + _mm_nt(dt, wg_ref[...])
        dyp_ref[...] = dyg * dgelu
        yg_ref[...] = yg.astype(yg_ref.dtype)
        dt_ref[...] = dt.astype(dt_ref.dtype)
        ya_ref[...] = (y2 * act).astype(ya_ref.dtype)
        gbg_ref[...] += jnp.sum(dt, axis=0, keepdims=True)

    row = pl.BlockSpec((tm, D), lambda i: (i, 0))
    return pl.pallas_call(
        body, name="s5_gate_bwd", grid=(L // tm,),
        in_specs=[row, pl.BlockSpec((tm, D), lambda i: (i, 1)), row,
                  pl.BlockSpec((D, D), lambda i: (0, 0)), _full((D, D)), _full((1, D))],
        out_specs=[row, row, row, row, row, _full((1, D))],
        out_shape=[SDS((L, D), F32), SDS((L, D), MXU), SDS((L, D), MXU), SDS((L, D), MXU), SDS((L, D), MXU),
                   SDS((1, D), F32)],
        compiler_params=_cp(40, ("arbitrary",)),
    )(ypre, p, dx1, wout_e, wglu, bglu)


def _s5_scan_bwd(p, dypre, states, bb, cm, abr, abi, par, pai, pbr, pbi, dskip):
    L = p.shape[0]
    ncb, tb = _chunks(L)
    nb = L // tb
    rev = lambda i: nb - 1 - i

    def body(u_ref, dy_ref, st_ref, bb_ref, cm_ref, ar_ref, ai_ref, par_ref, pai_ref, pbr_ref, pbi_ref, d_ref,
             du_ref, gd_ref, gcm_ref, gbb_ref, gar_ref, gai_ref, lcarry, bu_scr, s_scr, gs_scr):
        @pl.when(pl.program_id(1) == 0)
        def _():
            lcarry[...] = jnp.zeros_like(lcarry)
            gd_ref[...] = jnp.zeros_like(gd_ref)
            gcm_ref[...] = jnp.zeros_like(gcm_ref)
            gbb_ref[...] = jnp.zeros_like(gbb_ref)
            gar_ref[...] = jnp.zeros_like(gar_ref)
            gai_ref[...] = jnp.zeros_like(gai_ref)

        ar, ai = ar_ref[...], ai_ref[...]
        c = st_ref[...]
        sr, si, _ = _s5_states(u_ref, bb_ref, ar, ai, par_ref, pai_ref, c[:, :SW], c[:, SW:], ncb, bu_scr)
        s_scr[0] = jnp.concatenate([sr, si], axis=1)
        for t in range(NT):
            bu = bu_scr[t]
            sr, si = _cmul_add(ar, ai, sr, si, bu[:, :SW], bu[:, SW:])
            s_scr[t + 1] = jnp.concatenate([sr, si], axis=1)
        fr = fi = None
        for t in reversed(range(NT)):
            gs = _mm_nt(dy_ref[pl.ds(t, ncb, stride=NT), :], cm_ref[...])
            gs_scr[t] = gs
            if t == NT - 1:
                fr, fi = gs[:, :SW], gs[:, SW:]
            else:
                fr, fi = _cmul_add(ar, -ai, fr, fi, gs[:, :SW], gs[:, SW:])
        xr, xi = _cscan(fr, fi, par_ref, pai_ref, True)
        lc = lcarry[...]
        lcr, lci = lc[:, :SW], lc[:, SW:]
        hr, hi = _cmul_add(pbr_ref[CH - ncb:CH, :], -pbi_ref[CH - ncb:CH, :], lcr, lci, xr, xi)
        lcarry[...] = jnp.concatenate([hr[0:1, :], hi[0:1, :]], axis=1)
        row = lax.broadcasted_iota(jnp.int32, hr.shape, 0)
        lr_ = jnp.where(row < ncb - 1, pltpu.roll(hr, ncb - 1, 0), lcr)
        li_ = jnp.where(row < ncb - 1, pltpu.roll(hi, ncb - 1, 0), lci)
        gar = jnp.zeros((1, SW), F32)
        gai = jnp.zeros((1, SW), F32)
        for t in reversed(range(NT)):
            gs = gs_scr[t]
            lr_, li_ = _cmul_add(ar, -ai, lr_, li_, gs[:, :SW], gs[:, SW:])
            rows = pl.ds(t, ncb, stride=NT)
            u_t, dy_t = u_ref[rows, :], dy_ref[rows, :]
            lam = jnp.concatenate([lr_, li_], axis=1)
            gbb_ref[...] += _mm_tn(u_t, lam)
            du_ref[rows, :] = _mm_nt(lam, bb_ref[...]) + dy_t * d_ref[...]
            gd_ref[...] += jnp.sum(dy_t * u_t, axis=0, keepdims=True)
            gcm_ref[...] += _mm_tn(s_scr[t + 1], dy_t)
            sp = s_scr[t]
            spr, spi = sp[:, :SW], sp[:, SW:]
            gar += jnp.sum(lr_ * spr + li_ * spi, axis=0, keepdims=True)
            gai += jnp.sum(li_ * spr - lr_ * spi, axis=0, keepdims=True)
        gar_ref[...] += gar
        gai_ref[...] += gai

    tab = pl.BlockSpec((CH, SW), lambda j, i: (0, j))
    colblk = pl.BlockSpec((tb, UW), lambda j, i: (rev(i), j))
    vec = lambda w: pl.BlockSpec((1, w), lambda j, i: (0, j))
    return pl.pallas_call(
        body, name="s5_scan_bwd", grid=(NJ, nb),
        in_specs=[colblk, colblk,
                  pl.BlockSpec((None, None, 1, 2 * SW), lambda j, i: (j, rev(i), 0, 0)),
                  pl.BlockSpec((None, UW, 2 * SW), lambda j, i: (j, 0, 0)),
                  pl.BlockSpec((None, 2 * SW, UW), lambda j, i: (j, 0, 0)),
                  vec(SW), vec(SW), tab, tab, tab, tab, vec(UW)],
        out_specs=[colblk, vec(UW),
                   pl.BlockSpec((None, 2 * SW, UW), lambda j, i: (j, 0, 0)),
                   pl.BlockSpec((None, UW, 2 * SW), lambda j, i: (j, 0, 0)),
                   vec(SW), vec(SW)],
        out_shape=[SDS((L, D), F32), SDS((1, D), F32),
                   SDS((NJ, 2 * SW, UW), F32), SDS((NJ, UW, 2 * SW), F32),
                   SDS((1, NSTATE), F32), SDS((1, NSTATE), F32)],
        scratch_shapes=[pltpu.VMEM((1, 2 * SW), F32), pltpu.VMEM((NT, ncb, 2 * SW), F32),
                        pltpu.VMEM((NT + 1, ncb, 2 * SW), F32), pltpu.VMEM((NT, ncb, 2 * SW), F32)],
        compiler_params=_cp(48, ARB2),
    )(p, dypre, states, bb, cm, abr, abi, par, pai, pbr, pbi, dskip)


def _rope_tables(L, inv):
    tm = min(512, L)

    def body(inv_ref, cos_ref, sin_ref):
        pos = (lax.broadcasted_iota(jnp.int32, (tm, DK // 2), 0) + pl.program_id(0) * tm).astype(F32)
        ang = pos * inv_ref[...]
        cos_ref[...] = jnp.cos(ang)
        sin_ref[...] = jnp.sin(ang)

    blk = pl.BlockSpec((tm, DK // 2), lambda i: (i, 0))
    return pl.pallas_call(body, name="rope_tables", grid=(L // tm,), in_specs=[_full((1, DK // 2))],
                          out_specs=[blk, blk], out_shape=[SDS((L, DK // 2), F32)] * 2)(inv)


def _rot(x, cos, sin):
    x1, x2 = x[:, :DK // 2], x[:, DK // 2:]
    return jnp.concatenate([x1 * cos - x2 * sin, x1 * sin + x2 * cos], axis=1)


def _unrot(d, cos, sin):
    d1, d2 = d[:, :DK // 2], d[:, DK // 2:]
    return jnp.concatenate([d1 * cos + d2 * sin, d2 * cos - d1 * sin], axis=1)


def _ret_decays(h):
    lg = LOG_G[h]
    n = lax.broadcasted_iota(jnp.int32, (CH, CH), 0)
    m = lax.broadcasted_iota(jnp.int32, (CH, CH), 1)
    diff = (n - m).astype(F32)
    decay = jnp.where(n >= m, jnp.exp(lg * jnp.maximum(diff, 0.0)), 0.0)
    idx = lax.broadcasted_iota(jnp.int32, (CH, 1), 0).astype(F32)
    xi = jnp.exp(lg * (idx + 1.0))
    zeta = jnp.exp(lg * (CH - 1.0 - idx))
    return decay, xi, zeta, math.exp(lg * CH)


def _ret_tables(dec_scr, vec_scr):
    for h in range(HEADS):
        decay, xi, zeta, _ = _ret_decays(h)
        dec_scr[h] = decay
        vec_scr[h] = jnp.concatenate([jnp.broadcast_to(xi, (CH, 128)), jnp.broadcast_to(zeta, (CH, 128))], axis=1)


def _ret_chunk_fwd(q, k, v, cos, sin, s_prev_b, decay, xi, zeta):
    qr = _rot(q, cos, sin)
    kr = _rot(k, cos, sin) * (DK ** -0.5)
    scores = _mm_nt(qr, kr) * decay
    o = _mm(scores, v) + _mm(qr * xi, s_prev_b)
    local = _mm_tn(kr * zeta, v)
    mu = jnp.mean(o, axis=-1, keepdims=True)
    oc = o - mu
    rstd = lax.rsqrt(jnp.mean(oc * oc, axis=-1, keepdims=True) + EPS)
    return qr, kr, scores, local, oc * rstd, rstd


def _ret_fwd(p, cos, sin, gain):
    L = p.shape[0]
    nb = L // CH

    def body(q_ref, k_ref, v_ref, bz_ref, cos_ref, sin_ref, g_ref, yb_ref, st_ref, state, dec_scr, vec_scr):
        @pl.when(pl.program_id(0) == 0)
        def _():
            state[...] = jnp.zeros_like(state)
            _ret_tables(dec_scr, vec_scr)

        cos, sin = cos_ref[...], sin_ref[...]
        act, _ = _silu_and_grad(bz_ref[...])
        for h in range(HEADS):
            hs = slice(h * DK, (h + 1) * DK)
            xi, zeta = vec_scr[h, :, 0:1], vec_scr[h, :, 128:129]
            s_prev = state[h]
            s_prev_b = s_prev.astype(MXU)
            st_ref[h] = s_prev_b
            _, _, _, local, on, _ = _ret_chunk_fwd(q_ref[:, hs], k_ref[:, hs], v_ref[:, hs], cos, sin,
                                                   s_prev_b, dec_scr[h], xi, zeta)
            state[h] = s_prev * math.exp(LOG_G[h] * CH) + local
            yb_ref[:, hs] = (on * g_ref[:, hs] * act[:, hs]).astype(yb_ref.dtype)

    col = lambda c: pl.BlockSpec((CH, D), lambda i: (i, c))
    rope = pl.BlockSpec((CH, DK // 2), lambda i: (i, 0))
    return pl.pallas_call(
        body, name="ret_fwd", grid=(nb,),
        in_specs=[col(2), col(3), col(4), col(5), rope, rope, _full((1, D))],
        out_specs=[pl.BlockSpec((CH, D), lambda i: (i, 0)),
                   pl.BlockSpec((None, HEADS, DK, DK), lambda i: (i, 0, 0, 0))],
        out_shape=[SDS((L, D), MXU), SDS((nb, HEADS, DK, DK), MXU)],
        scratch_shapes=[pltpu.VMEM((HEADS, DK, DK), F32), pltpu.VMEM((HEADS, CH, CH), F32),
                        pltpu.VMEM((HEADS, CH, 256), F32)],
        compiler_params=_cp(40, ("arbitrary",)),
    )(p, p, p, p, cos, sin, gain)


def _ret_bwd(p, cos, sin, gain, states, dx1, wout_e, du, daz):
    L = p.shape[0]
    nb = L // CH
    rev = lambda i: nb - 1 - i

    def body(q_ref, k_ref, v_ref, bz_ref, cos_ref, sin_ref, g_ref, st_ref, dx1_ref, wo_ref, du_ref, daz_ref,
             dp_ref, yb_ref, gg_ref, gstate, dec_scr, vec_scr):
        @pl.when(pl.program_id(0) == 0)
        def _():
            gstate[...] = jnp.zeros_like(gstate)
            gg_ref[...] = jnp.zeros_like(gg_ref)
            _ret_tables(dec_scr, vec_scr)

        cos, sin = cos_ref[...], sin_ref[...]
        act, dact = _silu_and_grad(bz_ref[...])
        dyb = _mm_nt(dx1_ref[...], wo_ref[...])
        dp_ref[:, 0:D] = du_ref[...].astype(dp_ref.dtype)
        dp_ref[:, D:2 * D] = daz_ref[...]
        for h in range(HEADS):
            hs = slice(h * DK, (h + 1) * DK)
            col = lambda part: slice((2 + part) * D + h * DK, (2 + part) * D + (h + 1) * DK)
            decay = dec_scr[h]
            xi, zeta = vec_scr[h, :, 0:1], vec_scr[h, :, 128:129]
            v = v_ref[:, hs]
            s_prev_b = st_ref[h]
            qr, kr, scores, _, on, rstd = _ret_chunk_fwd(q_ref[:, hs], k_ref[:, hs], v, cos, sin, s_prev_b,
                                                         decay, xi, zeta)
            gain_h = g_ref[:, hs]
            out = on * gain_h
            yb_ref[:, hs] = (out * act[:, hs]).astype(yb_ref.dtype)
            dyb_h = dyb[:, hs]
            dp_ref[:, col(3)] = (dyb_h * out * dact[:, hs]).astype(dp_ref.dtype)
            dout = dyb_h * act[:, hs]
            gg_ref[:, hs] += jnp.sum(dout * on, axis=0, keepdims=True)
            don = dout * gain_h
            do = rstd * (don - jnp.mean(don, axis=-1, keepdims=True)
                         - on * jnp.mean(don * on, axis=-1, keepdims=True))
            gnext = gstate[h]
            gnext_b = gnext.astype(MXU)
            dscores = _mm_nt(do, v) * decay
            dp_ref[:, col(2)] = (_mm_tn(scores, do) + _mm(kr * zeta, gnext_b)).astype(dp_ref.dtype)
            dqr = _mm(dscores, kr) + _mm_nt(do, s_prev_b) * xi
            dkr = _mm_tn(dscores, qr) + _mm_nt(v, gnext_b) * zeta
            gstate[h] = gnext * math.exp(LOG_G[h] * CH) + _mm_tn(qr * xi, do)
            dp_ref[:, col(0)] = _unrot(dqr, cos, sin).astype(dp_ref.dtype)
            dp_ref[:, col(1)] = (_unrot(dkr, cos, sin) * (DK ** -0.5)).astype(dp_ref.dtype)

    col = lambda c: pl.BlockSpec((CH, D), lambda i: (rev(i), c))
    rope = pl.BlockSpec((CH, DK // 2), lambda i: (rev(i), 0))
    outc = col(0)
    act_out = SDS((L, D), MXU)
    return pl.pallas_call(
        body, name="ret_bwd", grid=(nb,),
        in_specs=[col(2), col(3), col(4), col(5), rope, rope, _full((1, D)),
                  pl.BlockSpec((None, HEADS, DK, DK), lambda i: (rev(i), 0, 0, 0)),
                  outc, pl.BlockSpec((D, D), lambda i: (1, 0)), outc, outc],
        out_specs=[pl.BlockSpec((CH, NIN), lambda i: (rev(i), 0)), outc, _full((1, D))],
        out_shape=[SDS((L, NIN), MXU), act_out, SDS((1, D), F32)],
        scratch_shapes=[pltpu.VMEM((HEADS, DK, DK), F32), pltpu.VMEM((HEADS, CH, CH), F32),
                        pltpu.VMEM((HEADS, CH, 256), F32)],
        compiler_params=_cp(48, ("arbitrary",)),
    )(p, p, p, p, cos, sin, gain, states, dx1, wout_e, du, daz)


def _out_even(x, ya, yb, wout):
    L = x.shape[0]
    tm = min(512, L)

    def body(x_ref, ya_ref, yb_ref, w_ref, o_ref):
        cat = jnp.concatenate([ya_ref[...], yb_ref[...]], axis=1)
        o_ref[...] = x_ref[...] + jnp.dot(cat, w_ref[...], preferred_element_type=F32)

    row = pl.BlockSpec((tm, D), lambda i: (i, 0))
    return pl.pallas_call(
        body, name="out_even", grid=(L // tm,), in_specs=[row, row, row, _full((DI, D))],
        out_specs=row, out_shape=SDS((L, D), F32), compiler_params=_cp(32, ("arbitrary",)),
    )(x, ya, yb, wout)


def _sgu_core(pv, gain, ws_ref, bs_ref):
    pu, pvv, z = pv[:, :DI], pv[:, DI:2 * DI], pv[:, 2 * DI:]
    u, gu = _gelu_and_grad(pu)
    v, gv = _gelu_and_grad(pvv)
    mu = jnp.mean(v, axis=-1, keepdims=True)
    vc = v - mu
    rstd = lax.rsqrt(jnp.mean(vc * vc, axis=-1, keepdims=True) + EPS)
    vhat = vc * rstd
    vn = vhat * gain
    t = lax.broadcasted_iota(jnp.int32, (CH, CH), 0)
    s_ = lax.broadcasted_iota(jnp.int32, (CH, CH), 1)
    mask = t >= s_
    wm = [jnp.where(mask, ws_ref[g], 0.0).astype(MXU) for g in range(SG)]
    s = jnp.concatenate([_mm(wm[g], vn[:, g * SGD:(g + 1) * SGD]) + bs_ref[g] for g in range(SG)], axis=1)
    return gu, gv, z, u, vhat, rstd, vn, mask, wm, s


def _sgu_fwd(p2, x1, gain, wsp, bsp, wout, fnorm, tgt):
    L = p2.shape[0]

    def body(p_ref, x1_ref, g_ref, ws_ref, bs_ref, wo_ref, fn_ref, t_ref, dx2_ref, gfn_ref, loss_ref):
        @pl.when(pl.program_id(0) == 0)
        def _():
            gfn_ref[...] = jnp.zeros_like(gfn_ref)
            loss_ref[...] = jnp.zeros_like(loss_ref)

        _, _, z, u, _, _, _, _, _, s = _sgu_core(p_ref[...], g_ref[...], ws_ref, bs_ref)
        act, _ = _silu_and_grad(z)
        x2 = x1_ref[...] + _mm(u * s * act, wo_ref[...])
        xhat, r = _rms(x2)
        fn = fn_ref[...]
        e = xhat * fn - t_ref[...]
        loss_ref[...] += 0.5 * jnp.sum(jnp.mean(e * e, axis=-1, keepdims=True), axis=0, keepdims=True)
        do = e * (1.0 / D)
        gfn_ref[...] += jnp.sum(do * xhat, axis=0, keepdims=True)
        dxhat = do * fn
        dx2_ref[...] = r * (dxhat - xhat * jnp.mean(dxhat * xhat, axis=-1, keepdims=True))

    row = pl.BlockSpec((CH, D), lambda i: (i, 0))
    return pl.pallas_call(
        body, name="sgu_fwd", grid=(L // CH,),
        in_specs=[pl.BlockSpec((CH, NIN), lambda i: (i, 0)), row, _full((1, DI)), _full((SG, CH, CH)),
                  _full((SG, CH, 1)), _full((DI, D)), _full((1, D)), row],
        out_specs=[row, _full((1, D)), _full((1, 1))],
        out_shape=[SDS((L, D), F32), SDS((1, D), F32), SDS((1, 1), F32)],
        compiler_params=_cp(48, ("arbitrary",)),
    )(p2, x1, gain, wsp, bsp, wout, fnorm, tgt)


def _sgu_bwd(p2, dx2, gain, wsp, bsp, wout):
    L = p2.shape[0]

    def body(p_ref, dx2_ref, g_ref, ws_ref, bs_ref, wo_ref, dp_ref, y_ref, gg_ref, gws_ref, gbs_ref):
        @pl.when(pl.program_id(0) == 0)
        def _():
            gg_ref[...] = jnp.zeros_like(gg_ref)
            gws_ref[...] = jnp.zeros_like(gws_ref)
            gbs_ref[...] = jnp.zeros_like(gbs_ref)

        gain = g_ref[...]
        gu, gv, z, u, vhat, rstd, vn, mask, wm, s = _sgu_core(p_ref[...], gain, ws_ref, bs_ref)
        act, dact = _silu_and_grad(z)
        y_ref[...] = (u * s * act).astype(y_ref.dtype)
        dy = _mm_nt(dx2_ref[...], wo_ref[...])
        du = dy * s * act
        ds = dy * u * act
        dz = dy * u * s * dact
        dvn = []
        for g in range(SG):
            ds_g = ds[:, g * SGD:(g + 1) * SGD]
            vn_g = vn[:, g * SGD:(g + 1) * SGD]
            gbs_ref[g] += jnp.sum(ds_g, axis=1, keepdims=True)
            gws_ref[g] += jnp.where(mask, _mm_nt(ds_g, vn_g), 0.0)
            dvn.append(_mm_tn(wm[g], ds_g))
        dvn = jnp.concatenate(dvn, axis=1)
        gg_ref[...] += jnp.sum(dvn * vhat, axis=0, keepdims=True)
        dvhat = dvn * gain
        dv = rstd * (dvhat - jnp.mean(dvhat, axis=-1, keepdims=True)
                     - vhat * jnp.mean(dvhat * vhat, axis=-1, keepdims=True))
        dp_ref[...] = jnp.concatenate([du * gu, dv * gv, dz], axis=1).astype(dp_ref.dtype)

    return pl.pallas_call(
        body, name="sgu_bwd", grid=(L // CH,),
        in_specs=[pl.BlockSpec((CH, NIN), lambda i: (i, 0)), pl.BlockSpec((CH, D), lambda i: (i, 0)),
                  _full((1, DI)), _full((SG, CH, CH)), _full((SG, CH, 1)), _full((DI, D))],
        out_specs=[pl.BlockSpec((CH, NIN), lambda i: (i, 0)), pl.BlockSpec((CH, DI), lambda i: (i, 0)),
                   _full((1, DI)), _full((SG, CH, CH)), _full((SG, CH, 1))],
        out_shape=[SDS((L, NIN), MXU), SDS((L, DI), MXU), SDS((1, DI), F32), SDS((SG, CH, CH), F32),
                   SDS((SG, CH, 1), F32)],
        compiler_params=_cp(48, ("arbitrary",)),
    )(p2, dx2, gain, wsp, bsp, wout)


def _my_index():
    return 4 * lax.axis_index("x") + 2 * lax.axis_index("y") + lax.axis_index("c")


def _ordered_sum(land_ref, own, me):
    g = None
    for s in range(NDEV):
        part = jnp.where(me == s, own, land_ref[s].astype(F32))
        g = part if g is None else g + part
    return g


def _adamw_math(w, m, v, g):
    mn = ADAM_B1 * m + (1.0 - ADAM_B1) * g
    vn = ADAM_B2 * v + (1.0 - ADAM_B2) * (g * g)
    mhat = mn / BC1
    vhat = vn / BC2
    return g, -ADAM_LR * (mhat / (jnp.sqrt(vhat) + ADAM_EPS) + ADAM_WD * w), mn, vn


def _adamw(w, m, v, land, own, name):
    R, C = w.shape
    tr = R
    for cand in (256, 128, 64, 32, 16, 8):
        if R % cand == 0 and R > cand:
            tr = cand
            break

    def body(w_ref, m_ref, v_ref, land_ref, own_ref, g_ref, d_ref, mo_ref, vo_ref):
        g = _ordered_sum(land_ref, own_ref[...].astype(F32), _my_index())
        for o, val in zip((g_ref, d_ref, mo_ref, vo_ref), _adamw_math(w_ref[...], m_ref[...], v_ref[...], g)):
            o[...] = val

    blk = pl.BlockSpec((tr, C), lambda i: (i, 0))
    out = SDS((R, C), F32)
    return pl.pallas_call(
        body, name=name, grid=(R // tr,),
        in_specs=[blk, blk, blk, pl.BlockSpec((NDEV, tr, C), lambda i: (0, i, 0)), blk],
        out_specs=[blk, blk, blk, blk], out_shape=[out, out, out, out],
        compiler_params=_cp(40, ("arbitrary",)),
    )(w, m, v, land, own)


def _adamw_many(ws, ms, vs, lands, owns, name):
    k = len(ws)

    def body(*refs):
        ins, outs = refs[:5 * k], refs[5 * k:]
        me = _my_index()
        for i in range(k):
            w_ref, m_ref, v_ref, land_ref, own_ref = (ins[j * k + i] for j in range(5))
            g = _ordered_sum(land_ref, own_ref[...], me)
            for j, val in enumerate(_adamw_math(w_ref[...], m_ref[...], v_ref[...], g)):
                outs[j * k + i][...] = val

    out_shape = [SDS(w.shape, F32) for _ in range(4) for w in ws]
    res = pl.pallas_call(body, name=name, out_shape=out_shape, compiler_params=_cp(60))(*ws, *ms, *vs, *lands, *owns)
    return [res[j * k:(j + 1) * k] for j in range(4)]


MESH = pl.DeviceIdType.MESH
HBM_SPEC = pl.BlockSpec(memory_space=pltpu.HBM)
SEM_SPEC = pl.BlockSpec(memory_space=pltpu.SEMAPHORE)
EFFECT = pltpu.SideEffectType.DATAFLOW_SIDE_EFFECTING


def _me_and_peers():
    x, y, c = lax.axis_index("x"), lax.axis_index("y"), lax.axis_index("c")
    me = 4 * x + 2 * y + c
    peers = []
    for r in range(1, NDEV):
        px, py, pc = x ^ ((r >> 2) & 1), y ^ ((r >> 1) & 1), c ^ (r & 1)
        peers.append(((px, py, pc), 4 * px + 2 * py + pc))
    return me, peers


def _land_shape(a, scatter):
    return (NDEV,) + (a.shape[1:] if scatter else a.shape)


def _remote(src, dst, send_sems, recv_sems, r, k, n, dev):
    i = r * n + k
    return pltpu.make_async_remote_copy(src_ref=src, dst_ref=dst, send_sem=send_sems.at[i], recv_sem=recv_sems.at[i],
                                        device_id=dev, device_id_type=MESH)


def _exchange(arrays, scatter, name):
    n = len(arrays)
    out_shape = [SDS(_land_shape(a, scatter), a.dtype) for a in arrays]

    def body(*refs):
        ins, outs = refs[:n], refs[n:2 * n]
        send_sems, recv_sems, loc_sems = refs[2 * n:]
        me, peers = _me_and_peers()
        local = []
        for k in range(n):
            src = ins[k].at[me] if scatter else ins[k]
            cp = pltpu.make_async_copy(src, outs[k].at[me], loc_sems.at[k])
            cp.start()
            local.append(cp)
        sends = []
        for r, (dev, lin) in enumerate(peers):
            for k in range(n):
                src = ins[k].at[lin] if scatter else ins[k]
                cp = _remote(src, outs[k].at[me], send_sems, recv_sems, r, k, n, dev)
                cp.start()
                sends.append(cp)
        for r, (dev, lin) in enumerate(peers):
            for k in range(n):
                src = ins[k].at[me] if scatter else ins[k]
                _remote(src, outs[k].at[lin], send_sems, recv_sems, r, k, n, dev).wait_recv()
        for cp in sends:
            cp.wait_send()
        for cp in local:
            cp.wait()

    return pl.pallas_call(
        body, name=name, in_specs=[HBM_SPEC] * n, out_specs=[HBM_SPEC] * n, out_shape=out_shape,
        scratch_shapes=[pltpu.SemaphoreType.DMA(((NDEV - 1) * n,)), pltpu.SemaphoreType.DMA(((NDEV - 1) * n,)),
                        pltpu.SemaphoreType.DMA((n,))],
    )(*arrays)


def _exchange_start(arrays, scatter, name):
    n = len(arrays)
    lands = [lax.empty(_land_shape(a, scatter), a.dtype) for a in arrays]

    def body(*refs):
        ins, lnd = refs[:n], refs[n:2 * n]
        send_sems, recv_sems, own_sems = refs[2 * n:2 * n + 3]
        token = refs[-1]
        me, peers = _me_and_peers()
        for r, (dev, lin) in enumerate(peers):
            for k in range(n):
                src = ins[k].at[lin] if scatter else ins[k]
                _remote(src, lnd[k].at[me], send_sems, recv_sems, r, k, n, dev).start()
        if not scatter:
            for k in range(n):
                pltpu.make_async_copy(ins[k], lnd[k].at[me], own_sems.at[k]).start()
        token[...] = jnp.zeros_like(token)

    sem = pltpu.SemaphoreType.DMA(((NDEV - 1) * n,))
    outs = pl.pallas_call(
        body, name=name,
        out_shape=(sem, sem, pltpu.SemaphoreType.DMA((n,)), *[pltpu.HBM(a.shape, a.dtype) for a in arrays],
                   *[pltpu.HBM(l.shape, l.dtype) for l in lands], SDS((8, 128), F32)),
        in_specs=[HBM_SPEC] * (2 * n),
        out_specs=(SEM_SPEC, SEM_SPEC, SEM_SPEC, *[HBM_SPEC] * (2 * n), pl.BlockSpec(memory_space=pltpu.VMEM)),
        input_output_aliases={k: 3 + k for k in range(2 * n)},
        compiler_params=pltpu.CompilerParams(has_side_effects=EFFECT),
    )(*[pltpu.with_memory_space_constraint(a, pltpu.HBM) for a in arrays],
      *[pltpu.with_memory_space_constraint(l, pltpu.HBM) for l in lands])
    return (n, scatter, outs[0], outs[1], outs[2], outs[3:3 + n], outs[3 + n:3 + 2 * n]), outs[-1]


def _exchange_wait(handle, after, name):
    n, scatter, send_sems, recv_sems, own_sems, thru, lands = handle
    after = tuple(after)

    def body(*refs):
        ins, lnd = refs[:n], refs[n:2 * n]
        send_sems, recv_sems, own_sems = refs[2 * n:2 * n + 3]
        me, peers = _me_and_peers()
        for r, (dev, lin) in enumerate(peers):
            for k in range(n):
                src = ins[k].at[lin] if scatter else ins[k]
                cp = _remote(src, lnd[k].at[lin], send_sems, recv_sems, r, k, n, dev)
                cp.wait_send()
                cp.wait_recv()
        if not scatter:
            for k in range(n):
                pltpu.make_async_copy(ins[k], lnd[k].at[me], own_sems.at[k]).wait()

    outs = pl.pallas_call(
        body, name=name,
        out_shape=(*[pltpu.HBM(a.shape, a.dtype) for a in thru], *[pltpu.HBM(l.shape, l.dtype) for l in lands]),
        in_specs=[HBM_SPEC] * (2 * n) + [SEM_SPEC, SEM_SPEC, SEM_SPEC] + [ANY_SPEC] * len(after),
        out_specs=tuple([HBM_SPEC] * (2 * n)),
        input_output_aliases={k: k for k in range(2 * n)},
        compiler_params=pltpu.CompilerParams(has_side_effects=EFFECT),
    )(*thru, *lands, send_sems, recv_sems, own_sems, *after)
    return list(outs[:n]), list(outs[n:])


def _local_step(x, tgt, norm_even, first_weight, lam_re, lam_im, log_dt, b_re, b_im, c_re, c_im, s5_d, bglu,
                ret_gain, wsp, bsp, fnorm, late_weights, emit, start_token=None):
    L = x.shape[0]
    lr3, li3 = lam_re.reshape(G, 1, P), lam_im.reshape(G, 1, P)
    dt3 = log_dt.reshape(G, 1, 1)
    br3, bi3 = jnp.swapaxes(b_re, 1, 2), jnp.swapaxes(b_im, 1, 2)
    abr3, abi3, bbr3, bbi3 = _s5_disc(lr3, li3, dt3, br3, bi3)
    bb = jnp.concatenate([_embed(bbr3), _embed(bbi3)], axis=2).astype(MXU)
    cm = jnp.concatenate([_embed(jnp.swapaxes(c_re, 1, 2)), -_embed(jnp.swapaxes(c_im, 1, 2))], axis=1).astype(MXU)
    abr, abi = abr3.reshape(1, NSTATE), abi3.reshape(1, NSTATE)
    pwr, pwi, _, _ = _s5_tables(abr, abi, NT, "s5_tables_step")
    par, pai, pbr, pbi = _s5_tables(pwr[NT - 1:NT], pwi[NT - 1:NT], CH, "s5_tables_chunk")
    inv = (ROPE_BASE ** (-jnp.arange(DK // 2, dtype=F32) / (DK // 2))).reshape(1, DK // 2)
    cos, sin = _rope_tables(L, inv)
    bsp3 = bsp.reshape(SG, CH, 1)

    def dep(token):
        return NO_DEPS if token is None else (token,)

    win_e = first_weight((cos, pbi, cm))
    p = _in_proj(x, norm_even, win_e, "in_even", dep(start_token))
    ypre, s5_states = _s5_scan_fwd(p, bb, cm, abr, abi, par, pai, s5_d)
    yb, ret_states = _ret_fwd(p, cos, sin, ret_gain)
    wglu, wout_e, norm_odd, win_o, sgu_gain, wout_o = late_weights((ypre, yb))
    ya = _s5_gate_fwd(ypre, p, wglu, bglu)
    x1 = _out_even(x, ya, yb, wout_e)
    p2 = _in_proj(x1, norm_odd, win_o, "in_odd")
    dx2, g_fnorm, loss = _sgu_fwd(p2, x1, sgu_gain, wsp, bsp3, wout_o, fnorm, tgt)

    dp2, y_o, g_sgu_gain, g_wsp, g_bsp = _sgu_bwd(p2, dx2, sgu_gain, wsp, bsp3, wout_o)
    g_wout_o = _wgrad_rows([y_o], dx2, "wgrad_out_odd")
    g_win_o = _wgrad_cols(x1, norm_odd, dp2, "wgrad_in_odd")
    tok = emit("odd", dict(w_in_odd=g_win_o, w_out_odd=g_wout_o))
    dx1, g_norm_odd = _in_proj_bwd_x(dp2, x1, norm_odd, win_o, dx2, "in_odd_bwd", dep(tok))

    dypre, daz, yg, dt, ya2, g_bglu = _s5_gate_bwd(ypre, p, dx1, wout_e, wglu, bglu)
    du, g_d, g_cm, g_bb, g_ar, g_ai = _s5_scan_bwd(p, dypre, s5_states, bb, cm, abr, abi, par, pai, pbr, pbi, s5_d)
    dbbr3 = _diag_blocks(g_bb[:, :, :SW], HG, P)
    dbbi3 = _diag_blocks(g_bb[:, :, SW:], HG, P)
    g_c_re = jnp.swapaxes(_diag_blocks(g_cm[:, :SW, :], P, HG), 1, 2)
    g_c_im = -jnp.swapaxes(_diag_blocks(g_cm[:, SW:, :], P, HG), 1, 2)
    g_lr3, g_li3, g_dt3, g_br3, g_bi3 = _s5_disc_bwd(
        lr3, li3, dt3, br3, bi3, g_ar.reshape(G, 1, P), g_ai.reshape(G, 1, P), dbbr3, dbbi3)
    dp, yb2, g_ret_gain = _ret_bwd(p, cos, sin, ret_gain, ret_states, dx1, wout_e, du, daz)
    small = dict(
        s5_lam_re=g_lr3.reshape(G, P), s5_lam_im=g_li3.reshape(G, P),
        s5_log_dt=g_dt3.reshape(1, G), s5_b_re=g_br3, s5_b_im=g_bi3,
        s5_c_re=g_c_re, s5_c_im=g_c_im, s5_d=g_d, s5_b_glu=g_bglu, ret_gn_gain=g_ret_gain,
        norm_odd=g_norm_odd, sgu_norm_gain=g_sgu_gain, sgu_w_spatial=g_wsp, sgu_b_spatial=g_bsp.reshape(SG, CH),
        final_norm=g_fnorm)
    tok = emit("small", small)
    g_wout_e = _wgrad_rows([ya2, yb2], dx1, "wgrad_out_even", dep(tok))
    g_wglu = _wgrad_rows([yg], dt, "wgrad_glu", dep(tok))
    tok = emit("even_rows", dict(s5_w_glu=g_wglu, w_out_even=g_wout_e))
    g_win_e = _wgrad_cols(x, norm_even, dp, "wgrad_in_even", dep(tok))
    tok = emit("even_cols", dict(w_in_even=g_win_e))
    dx, g_norm_even = _in_proj_bwd_x(dp, x, norm_even, win_e, dx1, "in_even_bwd", dep(tok))
    emit("last", dict(norm_even=g_norm_even))
    return loss, dx


WEIGHTS = ['norm_even', 'w_in_even', 's5_lam_re', 's5_lam_im', 's5_log_dt', 's5_b_re', 's5_b_im', 's5_c_re',
           's5_c_im', 's5_d', 's5_w_glu', 's5_b_glu', 'ret_gn_gain', 'w_out_even', 'norm_odd', 'w_in_odd',
           'sgu_norm_gain', 'sgu_w_spatial', 'sgu_b_spatial', 'w_out_odd', 'final_norm']
BIG = ['w_in_even', 's5_w_glu', 'w_out_even', 'w_in_odd', 'w_out_odd']
SHARDED_SMALL = {'norm_odd': D // NDEV, 'sgu_norm_gain': DI // NDEV}
SMALL = [n for n in WEIGHTS if n not in BIG and n != 'norm_even']


def _view(n, a):
    if n in ('s5_b_re', 's5_b_im'):
        return jnp.swapaxes(a[0], 1, 2)
    if n == 'final_norm':
        return a.reshape(1, D)
    return a[0] if a.ndim >= 3 else a


def _unview(n, t, shape):
    if n in ('s5_b_re', 's5_b_im'):
        return jnp.swapaxes(t, 1, 2)[None]
    return t.reshape(shape)


def kernel(x, norm_even, w_in_even, s5_lam_re, s5_lam_im, s5_log_dt, s5_b_re, s5_b_im, s5_c_re, s5_c_im, s5_d, s5_w_glu, s5_b_glu, ret_gn_gain, w_out_even, norm_odd, w_in_odd, sgu_norm_gain, sgu_w_spatial, sgu_b_spatial, w_out_odd, final_norm, loss_target, m_norm_even, m_w_in_even, m_s5_lam_re, m_s5_lam_im, m_s5_log_dt, m_s5_b_re, m_s5_b_im, m_s5_c_re, m_s5_c_im, m_s5_d, m_s5_w_glu, m_s5_b_glu, m_ret_gn_gain, m_w_out_even, m_norm_odd, m_w_in_odd, m_sgu_norm_gain, m_sgu_w_spatial, m_sgu_b_spatial, m_w_out_odd, m_final_norm, v_norm_even, v_w_in_even, v_s5_lam_re, v_s5_lam_im, v_s5_log_dt, v_s5_b_re, v_s5_b_im, v_s5_c_re, v_s5_c_im, v_s5_d, v_s5_w_glu, v_s5_b_glu, v_ret_gn_gain, v_w_out_even, v_norm_odd, v_w_in_odd, v_sgu_norm_gain, v_sgu_w_spatial, v_sgu_b_spatial, v_w_out_odd, v_final_norm):
    args = dict(locals())
    w = {n: args[n] for n in WEIGHTS}
    m = {n: args["m_" + n] for n in WEIGHTS}
    v = {n: args["v_" + n] for n in WEIGHTS}
    me = _my_index()

    first_handle, _ = _exchange_start([w['w_in_even'][0].astype(MXU)], False, "gather_first_start")

    def first_weight(after):
        return _exchange_wait(first_handle, after, "gather_first_wait")[1][0]

    late_own = [w['s5_w_glu'][0].astype(MXU), w['w_out_even'][0].astype(MXU), w['norm_odd'],
                w['w_in_odd'][0].astype(MXU), w['sgu_norm_gain'], w['w_out_odd'][0].astype(MXU)]
    late_handle, start_token = _exchange_start(late_own, False, "gather_late_start")

    def late_weights(after):
        _, (wglu, wout_e, nodd, win_o, sgug, wout_o) = _exchange_wait(late_handle, after, "gather_late_wait")
        return (wglu.reshape(D, D), wout_e.reshape(DI, D), nodd.reshape(1, D), win_o, sgug.reshape(1, DI),
                wout_o.reshape(DI, D))

    pending = {}
    small_last = {}

    def emit(stage, grads):
        if stage == "last":
            small_last.update(grads)
            return None
        names = list(grads) if stage != "small" else SMALL
        handle, token = _exchange_start([grads[n] for n in names], stage != "small", stage + "_start")
        pending[stage] = (handle, names)
        return token

    loss, dx = _local_step(
        x[0], loss_target[0], w['norm_even'], first_weight, w['s5_lam_re'][0], w['s5_lam_im'][0], w['s5_log_dt'][0],
        w['s5_b_re'][0], w['s5_b_im'][0], w['s5_c_re'][0], w['s5_c_im'][0], w['s5_d'], w['s5_b_glu'],
        w['ret_gn_gain'], w['sgu_w_spatial'][0], w['sgu_b_spatial'][0], w['final_norm'].reshape(1, D),
        late_weights, emit, start_token)

    out_g, out_d, out_m, out_v = {}, {}, {}, {}
    after = dx
    for stage in ("odd", "even_rows", "even_cols"):
        handle, names = pending[stage]
        sent, lands = _exchange_wait(handle, (after,), stage + "_wait")
        for n, land, stack in zip(names, lands, sent):
            shp = w[n].shape
            r, c = shp[1], shp[2]
            own = lax.dynamic_index_in_dim(stack, me, 0, keepdims=False)
            res = _adamw(w[n].reshape(r, c), m[n].reshape(r, c), v[n].reshape(r, c), land, own, "adamw_" + n)
            out_g[n], out_d[n], out_m[n], out_v[n] = (t.reshape(shp) for t in res)
            after = res[0]

    handle, names = pending["small"]
    owns, lands = _exchange_wait(handle, (after,), "small_wait")
    (gne,) = _exchange([small_last['norm_even']], False, "gather_norm_even")
    names = names + ['norm_even']
    owns = owns + [small_last['norm_even']]
    lands = lands + [gne]
    for i, n in enumerate(names):
        if n in SHARDED_SMALL:
            width = SHARDED_SMALL[n]
            owns[i] = lax.dynamic_slice_in_dim(owns[i], me * width, width, axis=1)
            lands[i] = lax.dynamic_slice_in_dim(lands[i], me * width, width, axis=2)
    res = _adamw_many([_view(n, w[n]) for n in names], [_view(n, m[n]) for n in names],
                      [_view(n, v[n]) for n in names], lands, owns, "adamw_small")
    for dst, vals in zip((out_g, out_d, out_m, out_v), res):
        for n, t in zip(names, vals):
            dst[n] = _unview(n, t, w[n].shape)

    loss_total = lax.psum(loss[0, 0], AXES)
    return (loss_total, dx[None], *[out_g[n] for n in WEIGHTS], *[out_d[n] for n in WEIGHTS],
            *[out_m[n] for n in WEIGHTS], *[out_v[n] for n in WEIGHTS])
```

```python
import math

import jax
import jax.numpy as jnp
from jax import lax
from jax.experimental import pallas as pl
from jax.experimental.pallas import tpu as pltpu

F32 = jnp.float32
MXU = jnp.bfloat16
AXES = ("x", "y", "c")
NDEV = 8
D = 1024
NIN = 6144
WIN_BLK = NIN // NDEV
DI = 2048
G, P, HG = 64, 64, 16
GB = 8
NJ = G // GB
SW = GB * P
UW = GB * HG
NSTATE = G * P
HEADS, DK = 4, 256
CH = 128
SG, SGD = 4, 512
EPS = 1e-6
ROPE_BASE = 10000.0
VMEM_CAP_V7X = 64 * 1024 * 1024
LOG_G = [math.log1p(-2.0 ** (-5.0 - h)) for h in range(HEADS)]
GELU_C = math.sqrt(2.0 / math.pi)

ADAM_LR, ADAM_B1, ADAM_B2, ADAM_EPS, ADAM_WD, ADAM_STEP = 0.001, 0.9, 0.999, 1e-08, 0.01, 10
BC1 = 1.0 - ADAM_B1 ** ADAM_STEP
BC2 = 1.0 - ADAM_B2 ** ADAM_STEP

SDS = jax.ShapeDtypeStruct
ARB2 = ("arbitrary", "arbitrary")


def _cp(vmem_mib, sem=None):
    kw = dict(vmem_limit_bytes=min(vmem_mib * 1024 * 1024, VMEM_CAP_V7X - 4 * 1024 * 1024))
    if sem is not None:
        kw["dimension_semantics"] = sem
    return pltpu.CompilerParams(**kw)


def _mm(a, b):
    return jnp.dot(a.astype(MXU), b.astype(MXU), preferred_element_type=F32)


def _mm_nt(a, b):
    return lax.dot_general(a.astype(MXU), b.astype(MXU), (((1,), (1,)), ((), ())), preferred_element_type=F32)


def _mm_tn(a, b):
    return lax.dot_general(a.astype(MXU), b.astype(MXU), (((0,), (0,)), ((), ())), preferred_element_type=F32)


def _gelu(x):
    return _gelu_and_grad(x)[0]


def _gelu_and_grad(x):
    x2 = x * x
    th = jnp.tanh(GELU_C * x * (1.0 + 0.044715 * x2))
    hp = 0.5 * (1.0 + th)
    return x * hp, hp + 0.5 * x * (1.0 - th * th) * GELU_C * (1.0 + 3.0 * 0.044715 * x2)


def _silu_and_grad(x):
    s = jax.nn.sigmoid(x)
    return x * s, s * (1.0 + x * (1.0 - s))


def _full(shape):
    nd = len(shape)
    return pl.BlockSpec(shape, lambda *_: (0,) * nd)


def _rms(xf):
    r = lax.rsqrt(jnp.mean(xf * xf, axis=-1, keepdims=True) + EPS)
    return xf * r, r


ANY_SPEC = pl.BlockSpec(memory_space=pl.ANY)
NO_DEPS = ()


def _load_once(src_hbm, dst_vmem, sem):
    @pl.when(pl.program_id(0) == 0)
    def _():
        cp = pltpu.make_async_copy(src_hbm, dst_vmem, sem)
        cp.start()
        cp.wait()


def _in_proj(x, gain, wst, name, deps=NO_DEPS):
    L = x.shape[0]
    tm = min(512, L)

    def body(x_ref, g_ref, w_hbm, *rest):
        o_ref, w_scr, sem = rest[len(deps):]
        _load_once(w_hbm, w_scr, sem)
        xhat, _ = _rms(x_ref[...])
        h = (xhat * g_ref[...]).astype(MXU)
        for c in range(NDEV):
            o_ref[:, c * WIN_BLK:(c + 1) * WIN_BLK] = jnp.dot(h, w_scr[c], preferred_element_type=F32)

    return pl.pallas_call(
        body, name=name, grid=(L // tm,),
        in_specs=[pl.BlockSpec((tm, D), lambda i: (i, 0)), _full((1, D)), ANY_SPEC] + [ANY_SPEC] * len(deps),
        out_specs=pl.BlockSpec((tm, NIN), lambda i: (i, 0)),
        out_shape=SDS((L, NIN), F32),
        scratch_shapes=[pltpu.VMEM((NDEV, D, WIN_BLK), MXU), pltpu.SemaphoreType.DMA(())],
        compiler_params=_cp(56, ("arbitrary",)),
    )(x, gain, wst, *deps)


def _in_proj_bwd_x(dp, x, gain, wst, dres, name, deps=NO_DEPS):
    L = x.shape[0]
    tm = min(512, L)

    def body(dp_ref, x_ref, g_ref, w_hbm, dres_ref, *rest):
        dx_ref, gg_ref, w_scr, sem = rest[len(deps):]
        _load_once(w_hbm, w_scr, sem)

        @pl.when(pl.program_id(0) == 0)
        def _():
            gg_ref[...] = jnp.zeros_like(gg_ref)

        dh = _mm_nt(dp_ref[:, 0:WIN_BLK], w_scr[0])
        for c in range(1, NDEV):
            dh += _mm_nt(dp_ref[:, c * WIN_BLK:(c + 1) * WIN_BLK], w_scr[c])
        xhat, r = _rms(x_ref[...])
        dxhat = dh * g_ref[...]
        dx_ref[...] = dres_ref[...] + r * (dxhat - xhat * jnp.mean(dxhat * xhat, axis=-1, keepdims=True))
        gg_ref[...] += jnp.sum(dh * xhat, axis=0, keepdims=True)

    row = pl.BlockSpec((tm, D), lambda i: (i, 0))
    return pl.pallas_call(
        body, name=name, grid=(L // tm,),
        in_specs=[pl.BlockSpec((tm, NIN), lambda i: (i, 0)), row, _full((1, D)), ANY_SPEC, row]
        + [ANY_SPEC] * len(deps),
        out_specs=[row, _full((1, D))],
        out_shape=[SDS((L, D), F32), SDS((1, D), F32)],
        scratch_shapes=[pltpu.VMEM((NDEV, D, WIN_BLK), MXU), pltpu.SemaphoreType.DMA(())],
        compiler_params=_cp(56, ("arbitrary",)),
    )(dp, x, gain, wst, dres, *deps)


def _wgrad_cols(x, gain, dp, name, deps=NO_DEPS):
    L = x.shape[0]
    tk = min(512, L)
    nk = L // tk
    halves = 2
    nh = NDEV // halves

    def body(x_ref, g_ref, dp_ref, *rest):
        o_ref, acc = rest[len(deps):]
        k = pl.program_id(1)

        @pl.when(k == 0)
        def _():
            acc[...] = jnp.zeros_like(acc)

        xhat, _ = _rms(x_ref[...])
        acc[...] += _mm_tn(xhat * g_ref[...], dp_ref[...])

        @pl.when(k == nk - 1)
        def _():
            for c in range(nh):
                o_ref[c] = acc[:, c * WIN_BLK:(c + 1) * WIN_BLK].astype(o_ref.dtype)

    return pl.pallas_call(
        body, name=name, grid=(halves, nk),
        in_specs=[pl.BlockSpec((tk, D), lambda n, k: (k, 0)), _full((1, D)),
                  pl.BlockSpec((tk, nh * WIN_BLK), lambda n, k: (k, n))] + [ANY_SPEC] * len(deps),
        out_specs=pl.BlockSpec((nh, D, WIN_BLK), lambda n, k: (n, 0, 0)),
        out_shape=SDS((NDEV, D, WIN_BLK), MXU),
        scratch_shapes=[pltpu.VMEM((D, nh * WIN_BLK), F32)],
        compiler_params=_cp(56, ARB2),
    )(x, gain, dp, *deps)


def _wgrad_rows(a_parts, b, name, deps=NO_DEPS):
    L, N = b.shape
    na = len(a_parts)
    widths = [a.shape[1] for a in a_parts]
    M = sum(widths)
    tk = min(512, L)
    nk = L // tk

    def body(*refs):
        a_refs, b_ref = refs[:na], refs[na]
        o_ref, acc = refs[na + 1 + len(deps):]
        k = pl.program_id(0)

        @pl.when(k == 0)
        def _():
            acc[...] = jnp.zeros_like(acc)

        bv = b_ref[...].astype(MXU)
        off = 0
        for a_ref, wd in zip(a_refs, widths):
            acc[off:off + wd, :] += _mm_tn(a_ref[...], bv)
            off += wd

        @pl.when(k == nk - 1)
        def _():
            o_ref[...] = acc[...].astype(o_ref.dtype).reshape(o_ref.shape)

    return pl.pallas_call(
        body, name=name, grid=(nk,),
        in_specs=[pl.BlockSpec((tk, wd), lambda k: (k, 0)) for wd in widths]
        + [pl.BlockSpec((tk, N), lambda k: (k, 0))] + [ANY_SPEC] * len(deps),
        out_specs=_full((NDEV, M // NDEV, N)),
        out_shape=SDS((NDEV, M // NDEV, N), MXU),
        scratch_shapes=[pltpu.VMEM((M, N), F32)],
        compiler_params=_cp(48, ("arbitrary",)),
    )(*a_parts, b, *deps)


def _s5_disc_fn(lr_raw, li, logdt, br, bi):
    lr = jnp.minimum(lr_raw, -1e-4)
    dt = jnp.exp(logdt)
    mag = jnp.exp(lr * dt)
    abr = mag * jnp.cos(li * dt)
    abi = mag * jnp.sin(li * dt)
    den = lr * lr + li * li
    nre = abr - 1.0
    nim = abi
    zr = (nre * lr + nim * li) / den
    zi = (nim * lr - nre * li) / den
    return abr, abi, zr * br - zi * bi, zr * bi + zi * br


def _s5_disc(lr, li, logdt, br, bi):
    def body(lr_ref, li_ref, dt_ref, br_ref, bi_ref, abr_ref, abi_ref, bbr_ref, bbi_ref):
        abr, abi, bbr, bbi = _s5_disc_fn(lr_ref[...], li_ref[...], dt_ref[...], br_ref[...], bi_ref[...])
        abr_ref[...] = abr
        abi_ref[...] = abi
        bbr_ref[...] = bbr
        bbi_ref[...] = bbi

    s1, s3 = SDS((G, 1, P), F32), SDS((G, HG, P), F32)
    return pl.pallas_call(body, name="s5_disc", out_shape=[s1, s1, s3, s3])(lr, li, logdt, br, bi)


def _s5_disc_bwd(lr, li, logdt, br, bi, dabr, dabi, dbbr, dbbi):
    def body(lr_ref, li_ref, dt_ref, br_ref, bi_ref, c0, c1, c2, c3, o0, o1, o2, o3, o4):
        _, vjp = jax.vjp(_s5_disc_fn, lr_ref[...], li_ref[...], dt_ref[...], br_ref[...], bi_ref[...])
        g = vjp((c0[...], c1[...], c2[...], c3[...]))
        for o, v in zip((o0, o1, o2, o3, o4), g):
            o[...] = v

    s1, s3 = SDS((G, 1, P), F32), SDS((G, HG, P), F32)
    return pl.pallas_call(body, name="s5_disc_bwd", out_shape=[s1, s1, SDS((G, 1, 1), F32), s3, s3])(
        lr, li, logdt, br, bi, dabr, dabi, dbbr, dbbi)


def _s5_tables(abr, abi, rows, name):
    def body(ar_ref, ai_ref, pfr, pfi, pbr, pbi):
        pfr[0:1, :] = ar_ref[...]
        pfi[0:1, :] = ai_ref[...]
        pbr[rows - 1:rows, :] = ar_ref[...]
        pbi[rows - 1:rows, :] = ai_ref[...]
        n = 1
        while n < rows:
            er, ei = pfr[n - 1:n, :], pfi[n - 1:n, :]
            xr, xi = pfr[0:n, :], pfi[0:n, :]
            pfr[n:2 * n, :] = er * xr - ei * xi
            pfi[n:2 * n, :] = er * xi + ei * xr
            yr, yi = pbr[rows - n:rows, :], pbi[rows - n:rows, :]
            pbr[rows - 2 * n:rows - n, :] = er * yr - ei * yi
            pbi[rows - 2 * n:rows - n, :] = er * yi + ei * yr
            n *= 2

    s = SDS((rows, NSTATE), F32)
    return pl.pallas_call(body, name=name, out_shape=[s, s, s, s], compiler_params=_cp(40))(abr, abi)


def _cscan(br, bi, pr_ref, pi_ref, reverse):
    T = br.shape[0]
    sign = -1.0 if reverse else 1.0
    row = lax.broadcasted_iota(jnp.int32, br.shape, 0)
    k = 1
    while k < T:
        akr = pr_ref[k - 1:k, :]
        aki = sign * pi_ref[k - 1:k, :]

        def shift(v):
            if k % 8 == 0:
                z = jnp.zeros((k, v.shape[1]), v.dtype)
                return jnp.concatenate([v[k:], z], 0) if reverse else jnp.concatenate([z, v[:T - k]], 0)
            if reverse:
                return jnp.where(row < T - k, pltpu.roll(v, T - k, 0), 0.0)
            return jnp.where(row >= k, pltpu.roll(v, k, 0), 0.0)

        sr, si = shift(br), shift(bi)
        br, bi = br + akr * sr - aki * si, bi + akr * si + aki * sr
        k *= 2
    return br, bi


def _embed(t):
    a, b = t.shape[1], t.shape[2]
    return jnp.einsum("jgab,gh->jgahb", t.reshape(NJ, GB, a, b), jnp.eye(GB, dtype=t.dtype)).reshape(NJ, GB * a, GB * b)


def _diag_blocks(t, a, b):
    return jnp.einsum("jgahb,gh->jgab", t.reshape(NJ, GB, a, GB, b), jnp.eye(GB, dtype=t.dtype)).reshape(G, a, b)


NT = 16


def _chunks(L):
    ncb = min(CH, L // NT)
    return ncb, NT * ncb


def _cmul_add(ar, ai, xr, xi, br, bi):
    return ar * xr - ai * xi + br, ar * xi + ai * xr + bi


def _s5_states(u_ref, bb_ref, ar, ai, par_ref, pai_ref, cr, ci, ncb, bu_scr):
    er = ei = None
    for t in range(NT):
        bu = _mm(u_ref[pl.ds(t, ncb, stride=NT), :], bb_ref[...])
        bu_scr[t] = bu
        if t == 0:
            er, ei = bu[:, :SW], bu[:, SW:]
        else:
            er, ei = _cmul_add(ar, ai, er, ei, bu[:, :SW], bu[:, SW:])
    xr, xi = _cscan(er, ei, par_ref, pai_ref, False)
    fr, fi = _cmul_add(par_ref[0:ncb, :], pai_ref[0:ncb, :], cr, ci, xr, xi)
    row = lax.broadcasted_iota(jnp.int32, fr.shape, 0)
    cinr = jnp.where(row >= 1, pltpu.roll(fr, 1, 0), cr)
    cini = jnp.where(row >= 1, pltpu.roll(fi, 1, 0), ci)
    return cinr, cini, jnp.concatenate([fr[ncb - 1:ncb, :], fi[ncb - 1:ncb, :]], axis=1)


def _s5_scan_fwd(p, bb, cm, abr, abi, par, pai, dskip):
    L = p.shape[0]
    ncb, tb = _chunks(L)
    nb = L // tb

    def body(u_ref, bb_ref, cm_ref, ar_ref, ai_ref, par_ref, pai_ref, d_ref, ypre_ref, st_ref, carry, bu_scr):
        @pl.when(pl.program_id(1) == 0)
        def _():
            carry[...] = jnp.zeros_like(carry)

        c = carry[...]
        st_ref[...] = c
        ar, ai = ar_ref[...], ai_ref[...]
        sr, si, cnext = _s5_states(u_ref, bb_ref, ar, ai, par_ref, pai_ref, c[:, :SW], c[:, SW:], ncb, bu_scr)
        carry[...] = cnext
        for t in range(NT):
            bu = bu_scr[t]
            sr, si = _cmul_add(ar, ai, sr, si, bu[:, :SW], bu[:, SW:])
            rows = pl.ds(t, ncb, stride=NT)
            ypre_ref[rows, :] = _mm(jnp.concatenate([sr, si], axis=1), cm_ref[...]) + d_ref[...] * u_ref[rows, :]

    tab = pl.BlockSpec((CH, SW), lambda j, i: (0, j))
    vec = lambda w: pl.BlockSpec((1, w), lambda j, i: (0, j))
    return pl.pallas_call(
        body, name="s5_scan_fwd", grid=(NJ, nb),
        in_specs=[pl.BlockSpec((tb, UW), lambda j, i: (i, j)),
                  pl.BlockSpec((None, UW, 2 * SW), lambda j, i: (j, 0, 0)),
                  pl.BlockSpec((None, 2 * SW, UW), lambda j, i: (j, 0, 0)),
                  vec(SW), vec(SW), tab, tab, vec(UW)],
        out_specs=[pl.BlockSpec((tb, UW), lambda j, i: (i, j)),
                   pl.BlockSpec((None, None, 1, 2 * SW), lambda j, i: (j, i, 0, 0))],
        out_shape=[SDS((L, D), F32), SDS((NJ, nb, 1, 2 * SW), F32)],
        scratch_shapes=[pltpu.VMEM((1, 2 * SW), F32), pltpu.VMEM((NT, ncb, 2 * SW), F32)],
        compiler_params=_cp(40, ARB2),
    )(p, bb, cm, abr, abi, par, pai, dskip)


def _s5_gate_fwd(ypre, p, wglu, bglu):
    L = ypre.shape[0]
    tm = min(256, L)

    def body(y_ref, az_ref, wg_ref, bg_ref, ya_ref):
        yg = _gelu(y_ref[...])
        t = _mm(yg, wg_ref[...]) + bg_ref[...]
        act, _ = _silu_and_grad(az_ref[...])
        ya_ref[...] = (yg * jax.nn.sigmoid(t) * act).astype(ya_ref.dtype)

    return pl.pallas_call(
        body, name="s5_gate_fwd", grid=(L // tm,),
        in_specs=[pl.BlockSpec((tm, D), lambda i: (i, 0)), pl.BlockSpec((tm, D), lambda i: (i, 1)),
                  _full((D, D)), _full((1, D))],
        out_specs=pl.BlockSpec((tm, D), lambda i: (i, 0)),
        out_shape=SDS((L, D), MXU),
        compiler_params=_cp(32, ("arbitrary",)),
    )(ypre, p, wglu, bglu)


def _s5_gate_bwd(ypre, p, dx1, wout_e, wglu, bglu):
    L = ypre.shape[0]
    tm = min(256, L)

    def body(y_ref, az_ref, dx1_ref, wo_ref, wg_ref, bg_ref, dyp_ref, daz_ref, yg_ref, dt_ref, ya_ref, gbg_ref):
        @pl.when(pl.program_id(0) == 0)
        def _():
            gbg_ref[...] = jnp.zeros_like(gbg_ref)

        yg, dgelu = _gelu_and_grad(y_ref[...])
        sg = jax.nn.sigmoid(_mm(yg, wg_ref[...]) + bg_ref[...])
        act, dact = _silu_and_grad(az_ref[...])
        y2 = yg * sg
        dya = _mm_nt(dx1_ref[...], wo_ref[...])
        daz_ref[...] = (dya * y2 * dact).astype(daz_ref.dtype)
        dy2 = dya * act
        dt = dy2 * yg * sg * (1.0 - sg)
        dyg = dy2 * sg + _mm_nt(dt, wg_ref[...])
        dyp_ref[...] = dyg * dgelu
        yg_ref[...] = yg.astype(yg_ref.dtype)
        dt_ref[...] = dt.astype(dt_ref.dtype)
        ya_ref[...] = (y2 * act).astype(ya_ref.dtype)
        gbg_ref[...] += jnp.sum(dt, axis=0, keepdims=True)

    row = pl.BlockSpec((tm, D), lambda i: (i, 0))
    return pl.pallas_call(
        body, name="s5_gate_bwd", grid=(L // tm,),
        in_specs=[row, pl.BlockSpec((tm, D), lambda i: (i, 1)), row,
                  pl.BlockSpec((D, D), lambda i: (0, 0)), _full((D, D)), _full((1, D))],
        out_specs=[row, row, row, row, row, _full((1, D))],
        out_shape=[SDS((L, D), F32), SDS((L, D), MXU), SDS((L, D), MXU), SDS((L, D), MXU), SDS((L, D), MXU),
                   SDS((1, D), F32)],
        compiler_params=_cp(40, ("arbitrary",)),
    )(ypre, p, dx1, wout_e, wglu, bglu)


def _s5_scan_bwd(p, dypre, states, bb, cm, abr, abi, par, pai, pbr, pbi, dskip):
    L = p.shape[0]
    ncb, tb = _chunks(L)
    nb = L // tb
    rev = lambda i: nb - 1 - i

    def body(u_ref, dy_ref, st_ref, bb_ref, cm_ref, ar_ref, ai_ref, par_ref, pai_ref, pbr_ref, pbi_ref, d_ref,
             du_ref, gd_ref, gcm_ref, gbb_ref, gar_ref, gai_ref, lcarry, bu_scr, s_scr, gs_scr):
        @pl.when(pl.program_id(1) == 0)
        def _():
            lcarry[...] = jnp.zeros_like(lcarry)
            gd_ref[...] = jnp.zeros_like(gd_ref)
            gcm_ref[...] = jnp.zeros_like(gcm_ref)
            gbb_ref[...] = jnp.zeros_like(gbb_ref)
            gar_ref[...] = jnp.zeros_like(gar_ref)
            gai_ref[...] = jnp.zeros_like(gai_ref)

        ar, ai = ar_ref[...], ai_ref[...]
        c = st_ref[...]
        sr, si, _ = _s5_states(u_ref, bb_ref, ar, ai, par_ref, pai_ref, c[:, :SW], c[:, SW:], ncb, bu_scr)
        s_scr[0] = jnp.concatenate([sr, si], axis=1)
        for t in range(NT):
            bu = bu_scr[t]
            sr, si = _cmul_add(ar, ai, sr, si, bu[:, :SW], bu[:, SW:])
            s_scr[t + 1] = jnp.concatenate([sr, si], axis=1)
        fr = fi = None
        for t in reversed(range(NT)):
            gs = _mm_nt(dy_ref[pl.ds(t, ncb, stride=NT), :], cm_ref[...])
            gs_scr[t] = gs
            if t == NT - 1:
                fr, fi = gs[:, :SW], gs[:, SW:]
            else:
                fr, fi = _cmul_add(ar, -ai, fr, fi, gs[:, :SW], gs[:, SW:])
        xr, xi = _cscan(fr, fi, par_ref, pai_ref, True)
        lc = lcarry[...]
        lcr, lci = lc[:, :SW], lc[:, SW:]
        hr, hi = _cmul_add(pbr_ref[CH - ncb:CH, :], -pbi_ref[CH - ncb:CH, :], lcr, lci, xr, xi)
        lcarry[...] = jnp.concatenate([hr[0:1, :], hi[0:1, :]], axis=1)
        row = lax.broadcasted_iota(jnp.int32, hr.shape, 0)
        lr_ = jnp.where(row < ncb - 1, pltpu.roll(hr, ncb - 1, 0), lcr)
        li_ = jnp.where(row < ncb - 1, pltpu.roll(hi, ncb - 1, 0), lci)
        gar = jnp.zeros((1, SW), F32)
        gai = jnp.zeros((1, SW), F32)
        for t in reversed(range(NT)):
            gs = gs_scr[t]
            lr_, li_ = _cmul_add(ar, -ai, lr_, li_, gs[:, :SW], gs[:, SW:])
            rows = pl.ds(t, ncb, stride=NT)
            u_t, dy_t = u_ref[rows, :], dy_ref[rows, :]
            lam = jnp.concatenate([lr_, li_], axis=1)
            gbb_ref[...] += _mm_tn(u_t, lam)
            du_ref[rows, :] = _mm_nt(lam, bb_ref[...]) + dy_t * d_ref[...]
            gd_ref[...] += jnp.sum(dy_t * u_t, axis=0, keepdims=True)
            gcm_ref[...] += _mm_tn(s_scr[t + 1], dy_t)
            sp = s_scr[t]
            spr, spi = sp[:, :SW], sp[:, SW:]
            gar += jnp.sum(lr_ * spr + li_ * spi, axis=0, keepdims=True)
            gai += jnp.sum(li_ * spr - lr_ * spi, axis=0, keepdims=True)
        gar_ref[...] += gar
        gai_ref[...] += gai

    tab = pl.BlockSpec((CH, SW), lambda j, i: (0, j))
    colblk = pl.BlockSpec((tb, UW), lambda j, i: (rev(i), j))
    vec = lambda w: pl.BlockSpec((1, w), lambda j, i: (0, j))
    return pl.pallas_call(
        body, name="s5_scan_bwd", grid=(NJ, nb),
        in_specs=[colblk, colblk,
                  pl.BlockSpec((None, None, 1, 2 * SW), lambda j, i: (j, rev(i), 0, 0)),
                  pl.BlockSpec((None, UW, 2 * SW), lambda j, i: (j, 0, 0)),
                  pl.BlockSpec((None, 2 * SW, UW), lambda j, i: (j, 0, 0)),
                  vec(SW), vec(SW), tab, tab, tab, tab, vec(UW)],
        out_specs=[colblk, vec(UW),
                   pl.BlockSpec((None, 2 * SW, UW), lambda j, i: (j, 0, 0)),
                   pl.BlockSpec((None, UW, 2 * SW), lambda j, i: (j, 0, 0)),
                   vec(SW), vec(SW)],
        out_shape=[SDS((L, D), F32), SDS((1, D), F32),
                   SDS((NJ, 2 * SW, UW), F32), SDS((NJ, UW, 2 * SW), F32),
                   SDS((1, NSTATE), F32), SDS((1, NSTATE), F32)],
        scratch_shapes=[pltpu.VMEM((1, 2 * SW), F32), pltpu.VMEM((NT, ncb, 2 * SW), F32),
                        pltpu.VMEM((NT + 1, ncb, 2 * SW), F32), pltpu.VMEM((NT, ncb, 2 * SW), F32)],
        compiler_params=_cp(48, ARB2),
    )(p, dypre, states, bb, cm, abr, abi, par, pai, pbr, pbi, dskip)


def _rope_tables(L, inv):
    tm = min(512, L)

    def body(inv_ref, cos_ref, sin_ref):
        pos = (lax.broadcasted_iota(jnp.int32, (tm, DK // 2), 0) + pl.program_id(0) * tm).astype(F32)
        ang = pos * inv_ref[...]
        cos_ref[...] = jnp.cos(ang)
        sin_ref[...] = jnp.sin(ang)

    blk = pl.BlockSpec((tm, DK // 2), lambda i: (i, 0))
    return pl.pallas_call(body, name="rope_tables", grid=(L // tm,), in_specs=[_full((1, DK // 2))],
                          out_specs=[blk, blk], out_shape=[SDS((L, DK // 2), F32)] * 2)(inv)


def _rot(x, cos, sin):
    x1, x2 = x[:, :DK // 2], x[:, DK // 2:]
    return jnp.concatenate([x1 * cos - x2 * sin, x1 * sin + x2 * cos], axis=1)


def _unrot(d, cos, sin):
    d1, d2 = d[:, :DK // 2], d[:, DK // 2:]
    return jnp.concatenate([d1 * cos + d2 * sin, d2 * cos - d1 * sin], axis=1)


def _ret_decays(h):
    lg = LOG_G[h]
    n = lax.broadcasted_iota(jnp.int32, (CH, CH), 0)
    m = lax.broadcasted_iota(jnp.int32, (CH, CH), 1)
    diff = (n - m).astype(F32)
    decay = jnp.where(n >= m, jnp.exp(lg * jnp.maximum(diff, 0.0)), 0.0)
    idx = lax.broadcasted_iota(jnp.int32, (CH, 1), 0).astype(F32)
    xi = jnp.exp(lg * (idx + 1.0))
    zeta = jnp.exp(lg * (CH - 1.0 - idx))
    return decay, xi, zeta, math.exp(lg * CH)


def _ret_tables(dec_scr, vec_scr):
    for h in range(HEADS):
        decay, xi, zeta, _ = _ret_decays(h)
        dec_scr[h] = decay
        vec_scr[h] = jnp.concatenate([jnp.broadcast_to(xi, (CH, 128)), jnp.broadcast_to(zeta, (CH, 128))], axis=1)


def _ret_chunk_fwd(q, k, v, cos, sin, s_prev_b, decay, xi, zeta):
    qr = _rot(q, cos, sin)
    kr = _rot(k, cos, sin) * (DK ** -0.5)
    scores = _mm_nt(qr, kr) * decay
    o = _mm(scores, v) + _mm(qr * xi, s_prev_b)
    local = _mm_tn(kr * zeta, v)
    mu = jnp.mean(o, axis=-1, keepdims=True)
    oc = o - mu
    rstd = lax.rsqrt(jnp.mean(oc * oc, axis=-1, keepdims=True) + EPS)
    return qr, kr, scores, local, oc * rstd, rstd


def _ret_fwd(p, cos, sin, gain):
    L = p.shape[0]
    nb = L // CH

    def body(q_ref, k_ref, v_ref, bz_ref, cos_ref, sin_ref, g_ref, yb_ref, st_ref, state, dec_scr, vec_scr):
        @pl.when(pl.program_id(0) == 0)
        def _():
            state[...] = jnp.zeros_like(state)
            _ret_tables(dec_scr, vec_scr)

        cos, sin = cos_ref[...], sin_ref[...]
        act, _ = _silu_and_grad(bz_ref[...])
        for h in range(HEADS):
            hs = slice(h * DK, (h + 1) * DK)
            xi, zeta = vec_scr[h, :, 0:1], vec_scr[h, :, 128:129]
            s_prev = state[h]
            s_prev_b = s_prev.astype(MXU)
            st_ref[h] = s_prev_b
            _, _, _, local, on, _ = _ret_chunk_fwd(q_ref[:, hs], k_ref[:, hs], v_ref[:, hs], cos, sin,
                                                   s_prev_b, dec_scr[h], xi, zeta)
            state[h] = s_prev * math.exp(LOG_G[h] * CH) + local
            yb_ref[:, hs] = (on * g_ref[:, hs] * act[:, hs]).astype(yb_ref.dtype)

    col = lambda c: pl.BlockSpec((CH, D), lambda i: (i, c))
    rope = pl.BlockSpec((CH, DK // 2), lambda i: (i, 0))
    return pl.pallas_call(
        body, name="ret_fwd", grid=(nb,),
        in_specs=[col(2), col(3), col(4), col(5), rope, rope, _full((1, D))],
        out_specs=[pl.BlockSpec((CH, D), lambda i: (i, 0)),
                   pl.BlockSpec((None, HEADS, DK, DK), lambda i: (i, 0, 0, 0))],
        out_shape=[SDS((L, D), MXU), SDS((nb, HEADS, DK, DK), MXU)],
        scratch_shapes=[pltpu.VMEM((HEADS, DK, DK), F32), pltpu.VMEM((HEADS, CH, CH), F32),
                        pltpu.VMEM((HEADS, CH, 256), F32)],
        compiler_params=_cp(40, ("arbitrary",)),
    )(p, p, p, p, cos, sin, gain)


def _ret_bwd(p, cos, sin, gain, states, dx1, wout_e, du, daz):
    L = p.shape[0]
    nb = L // CH
    rev = lambda i: nb - 1 - i

    def body(q_ref, k_ref, v_ref, bz_ref, cos_ref, sin_ref, g_ref, st_ref, dx1_ref, wo_ref, du_ref, daz_ref,
             dp_ref, yb_ref, gg_ref, gstate, dec_scr, vec_scr):
        @pl.when(pl.program_id(0) == 0)
        def _():
            gstate[...] = jnp.zeros_like(gstate)
            gg_ref[...] = jnp.zeros_like(gg_ref)
            _ret_tables(dec_scr, vec_scr)

        cos, sin = cos_ref[...], sin_ref[...]
        act, dact = _silu_and_grad(bz_ref[...])
        dyb = _mm_nt(dx1_ref[...], wo_ref[...])
        dp_ref[:, 0:D] = du_ref[...].astype(dp_ref.dtype)
        dp_ref[:, D:2 * D] = daz_ref[...]
        for h in range(HEADS):
            hs = slice(h * DK, (h + 1) * DK)
            col = lambda part: slice((2 + part) * D + h * DK, (2 + part) * D + (h + 1) * DK)
            decay = dec_scr[h]
            xi, zeta = vec_scr[h, :, 0:1], vec_scr[h, :, 128:129]
            v = v_ref[:, hs]
            s_prev_b = st_ref[h]
            qr, kr, scores, _, on, rstd = _ret_chunk_fwd(q_ref[:, hs], k_ref[:, hs], v, cos, sin, s_prev_b,
                                                         decay, xi, zeta)
            gain_h = g_ref[:, hs]
            out = on * gain_h
            yb_ref[:, hs] = (out * act[:, hs]).astype(yb_ref.dtype)
            dyb_h = dyb[:, hs]
            dp_ref[:, col(3)] = (dyb_h * out * dact[:, hs]).astype(dp_ref.dtype)
            dout = dyb_h * act[:, hs]
            gg_ref[:, hs] += jnp.sum(dout * on, axis=0, keepdims=True)
            don = dout * gain_h
            do = rstd * (don - jnp.mean(don, axis=-1, keepdims=True)
                         - on * jnp.mean(don * on, axis=-1, keepdims=True))
            gnext = gstate[h]
            gnext_b = gnext.astype(MXU)
            dscores = _mm_nt(do, v) * decay
            dp_ref[:, col(2)] = (_mm_tn(scores, do) + _mm(kr * zeta, gnext_b)).astype(dp_ref.dtype)
            dqr = _mm(dscores, kr) + _mm_nt(do, s_prev_b) * xi
            dkr = _mm_tn(dscores, qr) + _mm_nt(v, gnext_b) * zeta
            gstate[h] = gnext * math.exp(LOG_G[h] * CH) + _mm_tn(qr * xi, do)
            dp_ref[:, col(0)] = _unrot(dqr, cos, sin).astype(dp_ref.dtype)
            dp_ref[:, col(1)] = (_unrot(dkr, cos, sin) * (DK ** -0.5)).astype(dp_ref.dtype)

    col = lambda c: pl.BlockSpec((CH, D), lambda i: (rev(i), c))
    rope = pl.BlockSpec((CH, DK // 2), lambda i: (rev(i), 0))
    outc = col(0)
    act_out = SDS((L, D), MXU)
    return pl.pallas_call(
        body, name="ret_bwd", grid=(nb,),
        in_specs=[col(2), col(3), col(4), col(5), rope, rope, _full((1, D)),
                  pl.BlockSpec((None, HEADS, DK, DK), lambda i: (rev(i), 0, 0, 0)),
                  outc, pl.BlockSpec((D, D), lambda i: (1, 0)), outc, outc],
        out_specs=[pl.BlockSpec((CH, NIN), lambda i: (rev(i), 0)), outc, _full((1, D))],
        out_shape=[SDS((L, NIN), MXU), act_out, SDS((1, D), F32)],
        scratch_shapes=[pltpu.VMEM((HEADS, DK, DK), F32), pltpu.VMEM((HEADS, CH, CH), F32),
                        pltpu.VMEM((HEADS, CH, 256), F32)],
        compiler_params=_cp(48, ("arbitrary",)),
    )(p, p, p, p, cos, sin, gain, states, dx1, wout_e, du, daz)


def _out_even(x, ya, yb, wout):
    L = x.shape[0]
    tm = min(512, L)

    def body(x_ref, ya_ref, yb_ref, w_ref, o_ref):
        cat = jnp.concatenate([ya_ref[...], yb_ref[...]], axis=1)
        o_ref[...] = x_ref[...] + jnp.dot(cat, w_ref[...], preferred_element_type=F32)

    row = pl.BlockSpec((tm, D), lambda i: (i, 0))
    return pl.pallas_call(
        body, name="out_even", grid=(L // tm,), in_specs=[row, row, row, _full((DI, D))],
        out_specs=row, out_shape=SDS((L, D), F32), compiler_params=_cp(32, ("arbitrary",)),
    )(x, ya, yb, wout)


def _sgu_core(pv, gain, ws_ref, bs_ref):
    pu, pvv, z = pv[:, :DI], pv[:, DI:2 * DI], pv[:, 2 * DI:]
    u, gu = _gelu_and_grad(pu)
    v, gv = _gelu_and_grad(pvv)
    mu = jnp.mean(v, axis=-1, keepdims=True)
    vc = v - mu
    rstd = lax.rsqrt(jnp.mean(vc * vc, axis=-1, keepdims=True) + EPS)
    vhat = vc * rstd
    vn = vhat * gain
    t = lax.broadcasted_iota(jnp.int32, (CH, CH), 0)
    s_ = lax.broadcasted_iota(jnp.int32, (CH, CH), 1)
    mask = t >= s_
    wm = [jnp.where(mask, ws_ref[g], 0.0).astype(MXU) for g in range(SG)]
    s = jnp.concatenate([_mm(wm[g], vn[:, g * SGD:(g + 1) * SGD]) + bs_ref[g] for g in range(SG)], axis=1)
    return gu, gv, z, u, vhat, rstd, vn, mask, wm, s


def _sgu_fwd(p2, x1, gain, wsp, bsp, wout, fnorm, tgt):
    L = p2.shape[0]

    def body(p_ref, x1_ref, g_ref, ws_ref, bs_ref, wo_ref, fn_ref, t_ref, dx2_ref, gfn_ref, loss_ref):
        @pl.when(pl.program_id(0) == 0)
        def _():
            gfn_ref[...] = jnp.zeros_like(gfn_ref)
            loss_ref[...] = jnp.zeros_like(loss_ref)

        _, _, z, u, _, _, _, _, _, s = _sgu_core(p_ref[...], g_ref[...], ws_ref, bs_ref)
        act, _ = _silu_and_grad(z)
        x2 = x1_ref[...] + _mm(u * s * act, wo_ref[...])
        xhat, r = _rms(x2)
        fn = fn_ref[...]
        e = xhat * fn - t_ref[...]
        loss_ref[...] += 0.5 * jnp.sum(jnp.mean(e * e, axis=-1, keepdims=True), axis=0, keepdims=True)
        do = e * (1.0 / D)
        gfn_ref[...] += jnp.sum(do * xhat, axis=0, keepdims=True)
        dxhat = do * fn
        dx2_ref[...] = r * (dxhat - xhat * jnp.mean(dxhat * xhat, axis=-1, keepdims=True))

    row = pl.BlockSpec((CH, D), lambda i: (i, 0))
    return pl.pallas_call(
        body, name="sgu_fwd", grid=(L // CH,),
        in_specs=[pl.BlockSpec((CH, NIN), lambda i: (i, 0)), row, _full((1, DI)), _full((SG, CH, CH)),
                  _full((SG, CH, 1)), _full((DI, D)), _full((1, D)), row],
        out_specs=[row, _full((1, D)), _full((1, 1))],
        out_shape=[SDS((L, D), F32), SDS((1, D), F32), SDS((1, 1), F32)],
        compiler_params=_cp(48, ("arbitrary",)),
    )(p2, x1, gain, wsp, bsp, wout, fnorm, tgt)


def _sgu_bwd(p2, dx2, gain, wsp, bsp, wout):
    L = p2.shape[0]

    def body(p_ref, dx2_ref, g_ref, ws_ref, bs_ref, wo_ref, dp_ref, y_ref, gg_ref, gws_ref, gbs_ref):
        @pl.when(pl.program_id(0) == 0)
        def _():
            gg_ref[...] = jnp.zeros_like(gg_ref)
            gws_ref[...] = jnp.zeros_like(gws_ref)
            gbs_ref[...] = jnp.zeros_like(gbs_ref)

        gain = g_ref[...]
        gu, gv, z, u, vhat, rstd, vn, mask, wm, s = _sgu_core(p_ref[...], gain, ws_ref, bs_ref)
        act, dact = _silu_and_grad(z)
        y_ref[...] = (u * s * act).astype(y_ref.dtype)
        dy = _mm_nt(dx2_ref[...], wo_ref[...])
        du = dy * s * act
        ds = dy * u * act
        dz = dy * u * s * dact
        dvn = []
        for g in range(SG):
            ds_g = ds[:, g * SGD:(g + 1) * SGD]
            vn_g = vn[:, g * SGD:(g + 1) * SGD]
            gbs_ref[g] += jnp.sum(ds_g, axis=1, keepdims=True)
            gws_ref[g] += jnp.where(mask, _mm_nt(ds_g, vn_g), 0.0)
            dvn.append(_mm_tn(wm[g], ds_g))
        dvn = jnp.concatenate(dvn, axis=1)
        gg_ref[...] += jnp.sum(dvn * vhat, axis=0, keepdims=True)
        dvhat = dvn * gain
        dv = rstd * (dvhat - jnp.mean(dvhat, axis=-1, keepdims=True)
                     - vhat * jnp.mean(dvhat * vhat, axis=-1, keepdims=True))
        dp_ref[...] = jnp.concatenate([du * gu, dv * gv, dz], axis=1).astype(dp_ref.dtype)

    return pl.pallas_call(
        body, name="sgu_bwd", grid=(L // CH,),
        in_specs=[pl.BlockSpec((CH, NIN), lambda i: (i, 0)), pl.BlockSpec((CH, D), lambda i: (i, 0)),
                  _full((1, DI)), _full((SG, CH, CH)), _full((SG, CH, 1)), _full((DI, D))],
        out_specs=[pl.BlockSpec((CH, NIN), lambda i: (i, 0)), pl.BlockSpec((CH, DI), lambda i: (i, 0)),
                   _full((1, DI)), _full((SG, CH, CH)), _full((SG, CH, 1))],
        out_shape=[SDS((L, NIN), MXU), SDS((L, DI), MXU), SDS((1, DI), F32), SDS((SG, CH, CH), F32),
                   SDS((SG, CH, 1), F32)],
        compiler_params=_cp(48, ("arbitrary",)),
    )(p2, dx2, gain, wsp, bsp, wout)


def _my_index():
    return 4 * lax.axis_index("x") + 2 * lax.axis_index("y") + lax.axis_index("c")


def _ordered_sum(land_ref, own, me):
    g = None
    for s in range(NDEV):
        part = jnp.where(me == s, own, land_ref[s].astype(F32))
        g = part if g is None else g + part
    return g


def _adamw_math(w, m, v, g):
    mn = ADAM_B1 * m + (1.0 - ADAM_B1) * g
    vn = ADAM_B2 * v + (1.0 - ADAM_B2) * (g * g)
    mhat = mn / BC1
    vhat = vn / BC2
    return g, -ADAM_LR * (mhat / (jnp.sqrt(vhat) + ADAM_EPS) + ADAM_WD * w), mn, vn


def _adamw(w, m, v, land, own, name):
    R, C = w.shape
    tr = R
    for cand in (256, 128, 64, 32, 16, 8):
        if R % cand == 0 and R > cand:
            tr = cand
            break

    def body(w_ref, m_ref, v_ref, land_ref, own_ref, g_ref, d_ref, mo_ref, vo_ref):
        g = _ordered_sum(land_ref, own_ref[...].astype(F32), _my_index())
        for o, val in zip((g_ref, d_ref, mo_ref, vo_ref), _adamw_math(w_ref[...], m_ref[...], v_ref[...], g)):
            o[...] = val

    blk = pl.BlockSpec((tr, C), lambda i: (i, 0))
    out = SDS((R, C), F32)
    return pl.pallas_call(
        body, name=name, grid=(R // tr,),
        in_specs=[blk, blk, blk, pl.BlockSpec((NDEV, tr, C), lambda i: (0, i, 0)), blk],
        out_specs=[blk, blk, blk, blk], out_shape=[out, out, out, out],
        compiler_params=_cp(40, ("arbitrary",)),
    )(w, m, v, land, own)


def _adamw_many(ws, ms, vs, lands, owns, name):
    k = len(ws)

    def body(*refs):
        ins, outs = refs[:5 * k], refs[5 * k:]
        me = _my_index()
        for i in range(k):
            w_ref, m_ref, v_ref, land_ref, own_ref = (ins[j * k + i] for j in range(5))
            g = _ordered_sum(land_ref, own_ref[...], me)
            for j, val in enumerate(_adamw_math(w_ref[...], m_ref[...], v_ref[...], g)):
                outs[j * k + i][...] = val

    out_shape = [SDS(w.shape, F32) for _ in range(4) for w in ws]
    res = pl.pallas_call(body, name=name, out_shape=out_shape, compiler_params=_cp(60))(*ws, *ms, *vs, *lands, *owns)
    return [res[j * k:(j + 1) * k] for j in range(4)]


MESH = pl.DeviceIdType.MESH
HBM_SPEC = pl.BlockSpec(memory_space=pltpu.HBM)
SEM_SPEC = pl.BlockSpec(memory_space=pltpu.SEMAPHORE)
EFFECT = pltpu.SideEffectType.DATAFLOW_SIDE_EFFECTING


def _me_and_peers():
    x, y, c = lax.axis_index("x"), lax.axis_index("y"), lax.axis_index("c")
    me = 4 * x + 2 * y + c
    peers = []
    for r in range(1, NDEV):
        px, py, pc = x ^ ((r >> 2) & 1), y ^ ((r >> 1) & 1), c ^ (r & 1)
        peers.append(((px, py, pc), 4 * px + 2 * py + pc))
    return me, peers


def _land_shape(a, scatter):
    return (NDEV,) + (a.shape[1:] if scatter else a.shape)


def _remote(src, dst, send_sems, recv_sems, r, k, n, dev):
    i = r * n + k
    return pltpu.make_async_remote_copy(src_ref=src, dst_ref=dst, send_sem=send_sems.at[i], recv_sem=recv_sems.at[i],
                                        device_id=dev, device_id_type=MESH)


def _exchange(arrays, scatter, name):
    n = len(arrays)
    out_shape = [SDS(_land_shape(a, scatter), a.dtype) for a in arrays]

    def body(*refs):
        ins, outs = refs[:n], refs[n:2 * n]
        send_sems, recv_sems, loc_sems = refs[2 * n:]
        me, peers = _me_and_peers()
        local = []
        for k in range(n):
            src = ins[k].at[me] if scatter else ins[k]
            cp = pltpu.make_async_copy(src, outs[k].at[me], loc_sems.at[k])
            cp.start()
            local.append(cp)
        sends = []
        for r, (dev, lin) in enumerate(peers):
            for k in range(n):
                src = ins[k].at[lin] if scatter else ins[k]
                cp = _remote(src, outs[k].at[me], send_sems, recv_sems, r, k, n, dev)
                cp.start()
                sends.append(cp)
        for r, (dev, lin) in enumerate(peers):
            for k in range(n):
                src = ins[k].at[me] if scatter else ins[k]
                _remote(src, outs[k].at[lin], send_sems, recv_sems, r, k, n, dev).wait_recv()
        for cp in sends:
            cp.wait_send()
        for cp in local:
            cp.wait()

    return pl.pallas_call(
        body, name=name, in_specs=[HBM_SPEC] * n, out_specs=[HBM_SPEC] * n, out_shape=out_shape,
        scratch_shapes=[pltpu.SemaphoreType.DMA(((NDEV - 1) * n,)), pltpu.SemaphoreType.DMA(((NDEV - 1) * n,)),
                        pltpu.SemaphoreType.DMA((n,))],
    )(*arrays)


def _exchange_start(arrays, scatter, name):
    n = len(arrays)
    lands = [lax.empty(_land_shape(a, scatter), a.dtype) for a in arrays]

    def body(*refs):
        ins, lnd = refs[:n], refs[n:2 * n]
        send_sems, recv_sems, own_sems = refs[2 * n:2 * n + 3]
        token = refs[-1]
        me, peers = _me_and_peers()
        for r, (dev, lin) in enumerate(peers):
            for k in range(n):
                src = ins[k].at[lin] if scatter else ins[k]
                _remote(src, lnd[k].at[me], send_sems, recv_sems, r, k, n, dev).start()
        if not scatter:
            for k in range(n):
                pltpu.make_async_copy(ins[k], lnd[k].at[me], own_sems.at[k]).start()
        token[...] = jnp.zeros_like(token)

    sem = pltpu.SemaphoreType.DMA(((NDEV - 1) * n,))
    outs = pl.pallas_call(
        body, name=name,
        out_shape=(sem, sem, pltpu.SemaphoreType.DMA((n,)), *[pltpu.HBM(a.shape, a.dtype) for a in arrays],
                   *[pltpu.HBM(l.shape, l.dtype) for l in lands], SDS((8, 128), F32)),
        in_specs=[HBM_SPEC] * (2 * n),
        out_specs=(SEM_SPEC, SEM_SPEC, SEM_SPEC, *[HBM_SPEC] * (2 * n), pl.BlockSpec(memory_space=pltpu.VMEM)),
        input_output_aliases={k: 3 + k for k in range(2 * n)},
        compiler_params=pltpu.CompilerParams(has_side_effects=EFFECT),
    )(*[pltpu.with_memory_space_constraint(a, pltpu.HBM) for a in arrays],
      *[pltpu.with_memory_space_constraint(l, pltpu.HBM) for l in lands])
    return (n, scatter, outs[0], outs[1], outs[2], outs[3:3 + n], outs[3 + n:3 + 2 * n]), outs[-1]


def _exchange_wait(handle, after, name):
    n, scatter, send_sems, recv_sems, own_sems, thru, lands = handle
    after = tuple(after)

    def body(*refs):
        ins, lnd = refs[:n], refs[n:2 * n]
        send_sems, recv_sems, own_sems = refs[2 * n:2 * n + 3]
        me, peers = _me_and_peers()
        for r, (dev, lin) in enumerate(peers):
            for k in range(n):
                src = ins[k].at[lin] if scatter else ins[k]
                cp = _remote(src, lnd[k].at[lin], send_sems, recv_sems, r, k, n, dev)
                cp.wait_send()
                cp.wait_recv()
        if not scatter:
            for k in range(n):
                pltpu.make_async_copy(ins[k], lnd[k].at[me], own_sems.at[k]).wait()

    outs = pl.pallas_call(
        body, name=name,
        out_shape=(*[pltpu.HBM(a.shape, a.dtype) for a in thru], *[pltpu.HBM(l.shape, l.dtype) for l in lands]),
        in_specs=[HBM_SPEC] * (2 * n) + [SEM_SPEC, SEM_SPEC, SEM_SPEC] + [ANY_SPEC] * len(after),
        out_specs=tuple([HBM_SPEC] * (2 * n)),
        input_output_aliases={k: k for k in range(2 * n)},
        compiler_params=pltpu.CompilerParams(has_side_effects=EFFECT),
    )(*thru, *lands, send_sems, recv_sems, own_sems, *after)
    return list(outs[:n]), list(outs[n:])


def _local_step(x, tgt, norm_even, first_weight, lam_re, lam_im, log_dt, b_re, b_im, c_re, c_im, s5_d, bglu,
                ret_gain, wsp, bsp, fnorm, late_weights, emit, start_token=None):
    L = x.shape[0]
    lr3, li3 = lam_re.reshape(G, 1, P), lam_im.reshape(G, 1, P)
    dt3 = log_dt.reshape(G, 1, 1)
    br3, bi3 = jnp.swapaxes(b_re, 1, 2), jnp.swapaxes(b_im, 1, 2)
    abr3, abi3, bbr3, bbi3 = _s5_disc(lr3, li3, dt3, br3, bi3)
    bb = jnp.concatenate([_embed(bbr3), _embed(bbi3)], axis=2).astype(MXU)
    cm = jnp.concatenate([_embed(jnp.swapaxes(c_re, 1, 2)), -_embed(jnp.swapaxes(c_im, 1, 2))], axis=1).astype(MXU)
    abr, abi = abr3.reshape(1, NSTATE), abi3.reshape(1, NSTATE)
    pwr, pwi, _, _ = _s5_tables(abr, abi, NT, "s5_tables_step")
    par, pai, pbr, pbi = _s5_tables(pwr[NT - 1:NT], pwi[NT - 1:NT], CH, "s5_tables_chunk")
    inv = (ROPE_BASE ** (-jnp.arange(DK // 2, dtype=F32) / (DK // 2))).reshape(1, DK // 2)
    cos, sin = _rope_tables(L, inv)
    bsp3 = bsp.reshape(SG, CH, 1)

    def dep(token):
        return NO_DEPS if token is None else (token,)

    win_e = first_weight((cos, pbi, cm))
    p = _in_proj(x, norm_even, win_e, "in_even", dep(start_token))
    ypre, s5_states = _s5_scan_fwd(p, bb, cm, abr, abi, par, pai, s5_d)
    yb, ret_states = _ret_fwd(p, cos, sin, ret_gain)
    wglu, wout_e, norm_odd, win_o, sgu_gain, wout_o = late_weights((ypre, yb))
    ya = _s5_gate_fwd(ypre, p, wglu, bglu)
    x1 = _out_even(x, ya, yb, wout_e)
    p2 = _in_proj(x1, norm_odd, win_o, "in_odd")
    dx2, g_fnorm, loss = _sgu_fwd(p2, x1, sgu_gain, wsp, bsp3, wout_o, fnorm, tgt)

    dp2, y_o, g_sgu_gain, g_wsp, g_bsp = _sgu_bwd(p2, dx2, sgu_gain, wsp, bsp3, wout_o)
    g_wout_o = _wgrad_rows([y_o], dx2, "wgrad_out_odd")
    g_win_o = _wgrad_cols(x1, norm_odd, dp2, "wgrad_in_odd")
    tok = emit("odd", dict(w_in_odd=g_win_o, w_out_odd=g_wout_o))
    dx1, g_norm_odd = _in_proj_bwd_x(dp2, x1, norm_odd, win_o, dx2, "in_odd_bwd", dep(tok))

    dypre, daz, yg, dt, ya2, g_bglu = _s5_gate_bwd(ypre, p, dx1, wout_e, wglu, bglu)
    du, g_d, g_cm, g_bb, g_ar, g_ai = _s5_scan_bwd(p, dypre, s5_states, bb, cm, abr, abi, par, pai, pbr, pbi, s5_d)
    dbbr3 = _diag_blocks(g_bb[:, :, :SW], HG, P)
    dbbi3 = _diag_blocks(g_bb[:, :, SW:], HG, P)
    g_c_re = jnp.swapaxes(_diag_blocks(g_cm[:, :SW, :], P, HG), 1, 2)
    g_c_im = -jnp.swapaxes(_diag_blocks(g_cm[:, SW:, :], P, HG), 1, 2)
    g_lr3, g_li3, g_dt3, g_br3, g_bi3 = _s5_disc_bwd(
        lr3, li3, dt3, br3, bi3, g_ar.reshape(G, 1, P), g_ai.reshape(G, 1, P), dbbr3, dbbi3)
    dp, yb2, g_ret_gain = _ret_bwd(p, cos, sin, ret_gain, ret_states, dx1, wout_e, du, daz)
    small = dict(
        s5_lam_re=g_lr3.reshape(G, P), s5_lam_im=g_li3.reshape(G, P),
        s5_log_dt=g_dt3.reshape(1, G), s5_b_re=g_br3, s5_b_im=g_bi3,
        s5_c_re=g_c_re, s5_c_im=g_c_im, s5_d=g_d, s5_b_glu=g_bglu, ret_gn_gain=g_ret_gain,
        norm_odd=g_norm_odd, sgu_norm_gain=g_sgu_gain, sgu_w_spatial=g_wsp, sgu_b_spatial=g_bsp.reshape(SG, CH),
        final_norm=g_fnorm)
    tok = emit("small", small)
    g_wout_e = _wgrad_rows([ya2, yb2], dx1, "wgrad_out_even", dep(tok))
    g_wglu = _wgrad_rows([yg], dt, "wgrad_glu", dep(tok))
    tok = emit("even_rows", dict(s5_w_glu=g_wglu, w_out_even=g_wout_e))
    g_win_e = _wgrad_cols(x, norm_even, dp, "wgrad_in_even", dep(tok))
    tok = emit("even_cols", dict(w_in_even=g_win_e))
    dx, g_norm_even = _in_proj_bwd_x(dp, x, norm_even, win_e, dx1, "in_even_bwd", dep(tok))
    emit("last", dict(norm_even=g_norm_even))
    return loss, dx


WEIGHTS = ['norm_even', 'w_in_even', 's5_lam_re', 's5_lam_im', 's5_log_dt', 's5_b_re', 's5_b_im', 's5_c_re',
           's5_c_im', 's5_d', 's5_w_glu', 's5_b_glu', 'ret_gn_gain', 'w_out_even', 'norm_odd', 'w_in_odd',
           'sgu_norm_gain', 'sgu_w_spatial', 'sgu_b_spatial', 'w_out_odd', 'final_norm']
BIG = ['w_in_even', 's5_w_glu', 'w_out_even', 'w_in_odd', 'w_out_odd']
SHARDED_SMALL = {'norm_odd': D // NDEV, 'sgu_norm_gain': DI // NDEV}
SMALL = [n for n in WEIGHTS if n not in BIG and n != 'norm_even']


def _view(n, a):
    if n in ('s5_b_re', 's5_b_im'):
        return jnp.swapaxes(a[0], 1, 2)
    if n == 'final_norm':
        return a.reshape(1, D)
    return a[0] if a.ndim >= 3 else a


def _unview(n, t, shape):
    if n in ('s5_b_re', 's5_b_im'):
        return jnp.swapaxes(t, 1, 2)[None]
    return t.reshape(shape)


def kernel(x, norm_even, w_in_even, s5_lam_re, s5_lam_im, s5_log_dt, s5_b_re, s5_b_im, s5_c_re, s5_c_im, s5_d, s5_w_glu, s5_b_glu, ret_gn_gain, w_out_even, norm_odd, w_in_odd, sgu_norm_gain, sgu_w_spatial, sgu_b_spatial, w_out_odd, final_norm, loss_target, m_norm_even, m_w_in_even, m_s5_lam_re, m_s5_lam_im, m_s5_log_dt, m_s5_b_re, m_s5_b_im, m_s5_c_re, m_s5_c_im, m_s5_d, m_s5_w_glu, m_s5_b_glu, m_ret_gn_gain, m_w_out_even, m_norm_odd, m_w_in_odd, m_sgu_norm_gain, m_sgu_w_spatial, m_sgu_b_spatial, m_w_out_odd, m_final_norm, v_norm_even, v_w_in_even, v_s5_lam_re, v_s5_lam_im, v_s5_log_dt, v_s5_b_re, v_s5_b_im, v_s5_c_re, v_s5_c_im, v_s5_d, v_s5_w_glu, v_s5_b_glu, v_ret_gn_gain, v_w_out_even, v_norm_odd, v_w_in_odd, v_sgu_norm_gain, v_sgu_w_spatial, v_sgu_b_spatial, v_w_out_odd, v_final_norm):
    args = dict(locals())
    w = {n: args[n] for n in WEIGHTS}
    m = {n: args["m_" + n] for n in WEIGHTS}
    v = {n: args["v_" + n] for n in WEIGHTS}
    me = _my_index()

    first_handle, _ = _exchange_start([w['w_in_even'][0].astype(MXU)], False, "gather_first_start")

    def first_weight(after):
        return _exchange_wait(first_handle, after, "gather_first_wait")[1][0]

    late_own = [w['s5_w_glu'][0].astype(MXU), w['w_out_even'][0].astype(MXU), w['norm_odd'],
                w['w_in_odd'][0].astype(MXU), w['sgu_norm_gain'], w['w_out_odd'][0].astype(MXU)]
    late_handle, start_token = _exchange_start(late_own, False, "gather_late_start")

    def late_weights(after):
        _, (wglu, wout_e, nodd, win_o, sgug, wout_o) = _exchange_wait(late_handle, after, "gather_late_wait")
        return (wglu.reshape(D, D), wout_e.reshape(DI, D), nodd.reshape(1, D), win_o, sgug.reshape(1, DI),
                wout_o.reshape(DI, D))

    pending = {}
    small_last = {}

    def emit(stage, grads):
        if stage == "last":
            small_last.update(grads)
            return None
        names = list(grads) if stage != "small" else SMALL
        handle, token = _exchange_start([grads[n] for n in names], stage != "small", stage + "_start")
        pending[stage] = (handle, names)
        return token

    loss, dx = _local_step(
        x[0], loss_target[0], w['norm_even'], first_weight, w['s5_lam_re'][0], w['s5_lam_im'][0], w['s5_log_dt'][0],
        w['s5_b_re'][0], w['s5_b_im'][0], w['s5_c_re'][0], w['s5_c_im'][0], w['s5_d'], w['s5_b_glu'],
        w['ret_gn_gain'], w['sgu_w_spatial'][0], w['sgu_b_spatial'][0], w['final_norm'].reshape(1, D),
        late_weights, emit, start_token)

    out_g, out_d, out_m, out_v = {}, {}, {}, {}
    after = dx
    for stage in ("odd", "even_rows", "even_cols"):
        handle, names = pending[stage]
        sent, lands = _exchange_wait(handle, (after,), stage + "_wait")
        for n, land, stack in zip(names, lands, sent):
            shp = w[n].shape
            r, c = shp[1], shp[2]
            own = lax.dynamic_index_in_dim(stack, me, 0, keepdims=False)
            res = _adamw(w[n].reshape(r, c), m[n].reshape(r, c), v[n].reshape(r, c), land, own, "adamw_" + n)
            out_g[n], out_d[n], out_m[n], out_v[n] = (t.reshape(shp) for t in res)
            after = res[0]

    handle, names = pending["small"]
    owns, lands = _exchange_wait(handle, (after,), "small_wait")
    (gne,) = _exchange([small_last['norm_even']], False, "gather_norm_even")
    names = names + ['norm_even']
    owns = owns + [small_last['norm_even']]
    lands = lands + [gne]
    for i, n in enumerate(names):
        if n in SHARDED_SMALL:
            width = SHARDED_SMALL[n]
            owns[i] = lax.dynamic_slice_in_dim(owns[i], me * width, width, axis=1)
            lands[i] = lax.dynamic_slice_in_dim(lands[i], me * width, width, axis=2)
    res = _adamw_many([_view(n, w[n]) for n in names], [_view(n, m[n]) for n in names],
                      [_view(n, v[n]) for n in names], lands, owns, "adamw_small")
    for dst, vals in zip((out_g, out_d, out_m, out_v), res):
        for n, t in zip(names, vals):
            dst[n] = _unview(n, t, w[n].shape)

    loss_total = lax.psum(loss[0, 0], AXES)
    return (loss_total, dx[None], *[out_g[n] for n in WEIGHTS], *[out_d[n] for n in WEIGHTS],
            *[out_m[n] for n in WEIGHTS], *[out_v[n] for n in WEIGHTS])
```

```python
import math

import jax
import jax.numpy as jnp
from jax import lax
from jax.experimental import pallas as pl
from jax.experimental.pallas import tpu as pltpu

F32 = jnp.float32
MXU = jnp.bfloat16
AXES = ("x", "y", "c")
NDEV = 8
D = 1024
NIN = 6144
WIN_BLK = NIN // NDEV
DI = 2048
G, P, HG = 64, 64, 16
GB = 8
NJ = G // GB
SW = GB * P
UW = GB * HG
NSTATE = G * P
HEADS, DK = 4, 256
CH = 128
SG, SGD = 4, 512
EPS = 1e-6
ROPE_BASE = 10000.0
VMEM_CAP_V7X = 64 * 1024 * 1024
LOG_G = [math.log1p(-2.0 ** (-5.0 - h)) for h in range(HEADS)]
GELU_C = math.sqrt(2.0 / math.pi)

ADAM_LR, ADAM_B1, ADAM_B2, ADAM_EPS, ADAM_WD, ADAM_STEP = 0.001, 0.9, 0.999, 1e-08, 0.01, 10
BC1 = 1.0 - ADAM_B1 ** ADAM_STEP
BC2 = 1.0 - ADAM_B2 ** ADAM_STEP

SDS = jax.ShapeDtypeStruct
ARB2 = ("arbitrary", "arbitrary")


def _cp(vmem_mib, sem=None):
    kw = dict(vmem_limit_bytes=min(vmem_mib * 1024 * 1024, VMEM_CAP_V7X - 4 * 1024 * 1024))
    if sem is not None:
        kw["dimension_semantics"] = sem
    return pltpu.CompilerParams(**kw)


def _mm(a, b):
    return jnp.dot(a.astype(MXU), b.astype(MXU), preferred_element_type=F32)


def _mm_nt(a, b):
    return lax.dot_general(a.astype(MXU), b.astype(MXU), (((1,), (1,)), ((), ())), preferred_element_type=F32)


def _mm_tn(a, b):
    return lax.dot_general(a.astype(MXU), b.astype(MXU), (((0,), (0,)), ((), ())), preferred_element_type=F32)


def _gelu(x):
    return _gelu_and_grad(x)[0]


def _gelu_and_grad(x):
    x2 = x * x
    th = jnp.tanh(GELU_C * x * (1.0 + 0.044715 * x2))
    hp = 0.5 * (1.0 + th)
    return x * hp, hp + 0.5 * x * (1.0 - th * th) * GELU_C * (1.0 + 3.0 * 0.044715 * x2)


def _silu_and_grad(x):
    s = jax.nn.sigmoid(x)
    return x * s, s * (1.0 + x * (1.0 - s))


def _full(shape):
    nd = len(shape)
    return pl.BlockSpec(shape, lambda *_: (0,) * nd)


def _rms(xf):
    r = lax.rsqrt(jnp.mean(xf * xf, axis=-1, keepdims=True) + EPS)
    return xf * r, r


ANY_SPEC = pl.BlockSpec(memory_space=pl.ANY)
NO_DEPS = ()


def _load_once(src_hbm, dst_vmem, sem):
    @pl.when(pl.program_id(0) == 0)
    def _():
        cp = pltpu.make_async_copy(src_hbm, dst_vmem, sem)
        cp.start()
        cp.wait()


def _in_proj(x, gain, wst, name, deps=NO_DEPS):
    L = x.shape[0]
    tm = min(512, L)

    def body(x_ref, g_ref, w_hbm, *rest):
        o_ref, w_scr, sem = rest[len(deps):]
        _load_once(w_hbm, w_scr, sem)
        xhat, _ = _rms(x_ref[...])
        h = (xhat * g_ref[...]).astype(MXU)
        for c in range(NDEV):
            o_ref[:, c * WIN_BLK:(c + 1) * WIN_BLK] = jnp.dot(h, w_scr[c], preferred_element_type=F32)

    return pl.pallas_call(
        body, name=name, grid=(L // tm,),
        in_specs=[pl.BlockSpec((tm, D), lambda i: (i, 0)), _full((1, D)), ANY_SPEC] + [ANY_SPEC] * len(deps),
        out_specs=pl.BlockSpec((tm, NIN), lambda i: (i, 0)),
        out_shape=SDS((L, NIN), F32),
        scratch_shapes=[pltpu.VMEM((NDEV, D, WIN_BLK), MXU), pltpu.SemaphoreType.DMA(())],
        compiler_params=_cp(56, ("arbitrary",)),
    )(x, gain, wst, *deps)


def _in_proj_bwd_x(dp, x, gain, wst, dres, name, deps=NO_DEPS):
    L = x.shape[0]
    tm = min(512, L)

    def body(dp_ref, x_ref, g_ref, w_hbm, dres_ref, *rest):
        dx_ref, gg_ref, w_scr, sem = rest[len(deps):]
        _load_once(w_hbm, w_scr, sem)

        @pl.when(pl.program_id(0) == 0)
        def _():
            gg_ref[...] = jnp.zeros_like(gg_ref)

        dh = _mm_nt(dp_ref[:, 0:WIN_BLK], w_scr[0])
        for c in range(1, NDEV):
            dh += _mm_nt(dp_ref[:, c * WIN_BLK:(c + 1) * WIN_BLK], w_scr[c])
        xhat, r = _rms(x_ref[...])
        dxhat = dh * g_ref[...]
        dx_ref[...] = dres_ref[...] + r * (dxhat - xhat * jnp.mean(dxhat * xhat, axis=-1, keepdims=True))
        gg_ref[...] += jnp.sum(dh * xhat, axis=0, keepdims=True)

    row = pl.BlockSpec((tm, D), lambda i: (i, 0))
    return pl.pallas_call(
        body, name=name, grid=(L // tm,),
        in_specs=[pl.BlockSpec((tm, NIN), lambda i: (i, 0)), row, _full((1, D)), ANY_SPEC, row]
        + [ANY_SPEC] * len(deps),
        out_specs=[row, _full((1, D))],
        out_shape=[SDS((L, D), F32), SDS((1, D), F32)],
        scratch_shapes=[pltpu.VMEM((NDEV, D, WIN_BLK), MXU), pltpu.SemaphoreType.DMA(())],
        compiler_params=_cp(56, ("arbitrary",)),
    )(dp, x, gain, wst, dres, *deps)


def _wgrad_cols(x, gain, dp, name, deps=NO_DEPS):
    L = x.shape[0]
    tk = min(512, L)
    nk = L // tk
    halves = 2
    nh = NDEV // halves

    def body(x_ref, g_ref, dp_ref, *rest):
        o_ref, acc = rest[len(deps):]
        k = pl.program_id(1)

        @pl.when(k == 0)
        def _():
            acc[...] = jnp.zeros_like(acc)

        xhat, _ = _rms(x_ref[...])
        acc[...] += _mm_tn(xhat * g_ref[...], dp_ref[...])

        @pl.when(k == nk - 1)
        def _():
            for c in range(nh):
                o_ref[c] = acc[:, c * WIN_BLK:(c + 1) * WIN_BLK].astype(o_ref.dtype)

    return pl.pallas_call(
        body, name=name, grid=(halves, nk),
        in_specs=[pl.BlockSpec((tk, D), lambda n, k: (k, 0)), _full((1, D)),
                  pl.BlockSpec((tk, nh * WIN_BLK), lambda n, k: (k, n))] + [ANY_SPEC] * len(deps),
        out_specs=pl.BlockSpec((nh, D, WIN_BLK), lambda n, k: (n, 0, 0)),
        out_shape=SDS((NDEV, D, WIN_BLK), MXU),
        scratch_shapes=[pltpu.VMEM((D, nh * WIN_BLK), F32)],
        compiler_params=_cp(56, ARB2),
    )(x, gain, dp, *deps)


def _wgrad_rows(a_parts, b, name, deps=NO_DEPS):
    L, N = b.shape
    na = len(a_parts)
    widths = [a.shape[1] for a in a_parts]
    M = sum(widths)
    tk = min(512, L)
    nk = L // tk

    def body(*refs):
        a_refs, b_ref = refs[:na], refs[na]
        o_ref, acc = refs[na + 1 + len(deps):]
        k = pl.program_id(0)

        @pl.when(k == 0)
        def _():
            acc[...] = jnp.zeros_like(acc)

        bv = b_ref[...].astype(MXU)
        off = 0
        for a_ref, wd in zip(a_refs, widths):
            acc[off:off + wd, :] += _mm_tn(a_ref[...], bv)
            off += wd

        @pl.when(k == nk - 1)
        def _():
            o_ref[...] = acc[...].astype(o_ref.dtype).reshape(o_ref.shape)

    return pl.pallas_call(
        body, name=name, grid=(nk,),
        in_specs=[pl.BlockSpec((tk, wd), lambda k: (k, 0)) for wd in widths]
        + [pl.BlockSpec((tk, N), lambda k: (k, 0))] + [ANY_SPEC] * len(deps),
        out_specs=_full((NDEV, M // NDEV, N)),
        out_shape=SDS((NDEV, M // NDEV, N), MXU),
        scratch_shapes=[pltpu.VMEM((M, N), F32)],
        compiler_params=_cp(48, ("arbitrary",)),
    )(*a_parts, b, *deps)


def _s5_disc_fn(lr_raw, li, logdt, br, bi):
    lr = jnp.minimum(lr_raw, -1e-4)
    dt = jnp.exp(logdt)
    mag = jnp.exp(lr * dt)
    abr = mag * jnp.cos(li * dt)
    abi = mag * jnp.sin(li * dt)
    den = lr * lr + li * li
    nre = abr - 1.0
    nim = abi
    zr = (nre * lr + nim * li) / den
    zi = (nim * lr - nre * li) / den
    return abr, abi, zr * br - zi * bi, zr * bi + zi * br


def _s5_disc(lr, li, logdt, br, bi):
    def body(lr_ref, li_ref, dt_ref, br_ref, bi_ref, abr_ref, abi_ref, bbr_ref, bbi_ref):
        abr, abi, bbr, bbi = _s5_disc_fn(lr_ref[...], li_ref[...], dt_ref[...], br_ref[...], bi_ref[...])
        abr_ref[...] = abr
        abi_ref[...] = abi
        bbr_ref[...] = bbr
        bbi_ref[...] = bbi

    s1, s3 = SDS((G, 1, P), F32), SDS((G, HG, P), F32)
    return pl.pallas_call(body, name="s5_disc", out_shape=[s1, s1, s3, s3])(lr, li, logdt, br, bi)


def _s5_disc_bwd(lr, li, logdt, br, bi, dabr, dabi, dbbr, dbbi):
    def body(lr_ref, li_ref, dt_ref, br_ref, bi_ref, c0, c1, c2, c3, o0, o1, o2, o3, o4):
        _, vjp = jax.vjp(_s5_disc_fn, lr_ref[...], li_ref[...], dt_ref[...], br_ref[...], bi_ref[...])
        g = vjp((c0[...], c1[...], c2[...], c3[...]))
        for o, v in zip((o0, o1, o2, o3, o4), g):
            o[...] = v

    s1, s3 = SDS((G, 1, P), F32), SDS((G, HG, P), F32)
    return pl.pallas_call(body, name="s5_disc_bwd", out_shape=[s1, s1, SDS((G, 1, 1), F32), s3, s3])(
        lr, li, logdt, br, bi, dabr, dabi, dbbr, dbbi)


def _s5_tables(abr, abi, rows, name):
    def body(ar_ref, ai_ref, pfr, pfi, pbr, pbi):
        pfr[0:1, :] = ar_ref[...]
        pfi[0:1, :] = ai_ref[...]
        pbr[rows - 1:rows, :] = ar_ref[...]
        pbi[rows - 1:rows, :] = ai_ref[...]
        n = 1
        while n < rows:
            er, ei = pfr[n - 1:n, :], pfi[n - 1:n, :]
            xr, xi = pfr[0:n, :], pfi[0:n, :]
            pfr[n:2 * n, :] = er * xr - ei * xi
            pfi[n:2 * n, :] = er * xi + ei * xr
            yr, yi = pbr[rows - n:rows, :], pbi[rows - n:rows, :]
            pbr[rows - 2 * n:rows - n, :] = er * yr - ei * yi
            pbi[rows - 2 * n:rows - n, :] = er * yi + ei * yr
            n *= 2

    s = SDS((rows, NSTATE), F32)
    return pl.pallas_call(body, name=name, out_shape=[s, s, s, s], compiler_params=_cp(40))(abr, abi)


def _cscan(br, bi, pr_ref, pi_ref, reverse):
    T = br.shape[0]
    sign = -1.0 if reverse else 1.0
    row = lax.broadcasted_iota(jnp.int32, br.shape, 0)
    k = 1
    while k < T:
        akr = pr_ref[k - 1:k, :]
        aki = sign * pi_ref[k - 1:k, :]

        def shift(v):
            if k % 8 == 0:
                z = jnp.zeros((k, v.shape[1]), v.dtype)
                return jnp.concatenate([v[k:], z], 0) if reverse else jnp.concatenate([z, v[:T - k]], 0)
            if reverse:
                return jnp.where(row < T - k, pltpu.roll(v, T - k, 0), 0.0)
            return jnp.where(row >= k, pltpu.roll(v, k, 0), 0.0)

        sr, si = shift(br), shift(bi)
        br, bi = br + akr * sr - aki * si, bi + akr * si + aki * sr
        k *= 2
    return br, bi


def _embed(t):
    a, b = t.shape[1], t.shape[2]
    return jnp.einsum("jgab,gh->jgahb", t.reshape(NJ, GB, a, b), jnp.eye(GB, dtype=t.dtype)).reshape(NJ, GB * a, GB * b)


def _diag_blocks(t, a, b):
    return jnp.einsum("jgahb,gh->jgab", t.reshape(NJ, GB, a, GB, b), jnp.eye(GB, dtype=t.dtype)).reshape(G, a, b)


NT = 16


def _chunks(L):
    ncb = min(CH, L // NT)
    return ncb, NT * ncb


def _cmul_add(ar, ai, xr, xi, br, bi):
    return ar * xr - ai * xi + br, ar * xi + ai * xr + bi


def _pow_weights(w_ref, pwr_ref, pwi_ref, dst, adjoint):
    w = w_ref[...].astype(F32)
    wr, wi = w[:, :SW], w[:, SW:]
    for t in range(NT):
        k = t if adjoint else NT - 1 - t
        if k == 0:
            blk = w
        else:
            pr, pi = pwr_ref[k - 1:k, :], pwi_ref[k - 1:k, :]
            if adjoint:
                blk = jnp.concatenate([pr * wr + pi * wi, pr * wi - pi * wr], axis=1)
            else:
                blk = jnp.concatenate([pr * wr - pi * wi, pr * wi + pi * wr], axis=1)
        dst[t * UW:(t + 1) * UW, :] = blk.astype(dst.dtype)


def _s5_states(u_ref, bb_ref, bbp_scr, par_ref, pai_ref, cr, ci, ncb, bu_scr):
    us = [u_ref[pl.ds(t, ncb, stride=NT), :] for t in range(NT)]
    for t in range(NT):
        bu_scr[t] = _mm(us[t], bb_ref[...])
    e = _mm(jnp.concatenate(us, axis=1), bbp_scr[...])
    xr, xi = _cscan(e[:, :SW], e[:, SW:], par_ref, pai_ref, False)
    fr, fi = _cmul_add(par_ref[0:ncb, :], pai_ref[0:ncb, :], cr, ci, xr, xi)
    row = lax.broadcasted_iota(jnp.int32, fr.shape, 0)
    cinr = jnp.where(row >= 1, pltpu.roll(fr, 1, 0), cr)
    cini = jnp.where(row >= 1, pltpu.roll(fi, 1, 0), ci)
    return cinr, cini, jnp.concatenate([fr[ncb - 1:ncb, :], fi[ncb - 1:ncb, :]], axis=1)


def _s5_scan_fwd(p, bb, cm, abr, abi, pwr, pwi, par, pai, dskip):
    L = p.shape[0]
    ncb, tb = _chunks(L)
    nb = L // tb

    def body(u_ref, bb_ref, cm_ref, ar_ref, ai_ref, pwr_ref, pwi_ref, par_ref, pai_ref, d_ref, ypre_ref, st_ref,
             carry, bu_scr, bbp_scr):
        @pl.when(pl.program_id(1) == 0)
        def _():
            carry[...] = jnp.zeros_like(carry)
            _pow_weights(bb_ref, pwr_ref, pwi_ref, bbp_scr, False)

        c = carry[...]
        st_ref[...] = c
        ar, ai = ar_ref[...], ai_ref[...]
        sr, si, cnext = _s5_states(u_ref, bb_ref, bbp_scr, par_ref, pai_ref, c[:, :SW], c[:, SW:], ncb, bu_scr)
        carry[...] = cnext
        for t in range(NT):
            bu = bu_scr[t]
            sr, si = _cmul_add(ar, ai, sr, si, bu[:, :SW], bu[:, SW:])
            rows = pl.ds(t, ncb, stride=NT)
            ypre_ref[rows, :] = _mm(jnp.concatenate([sr, si], axis=1), cm_ref[...]) + d_ref[...] * u_ref[rows, :]

    tab = pl.BlockSpec((CH, SW), lambda j, i: (0, j))
    stp = pl.BlockSpec((NT, SW), lambda j, i: (0, j))
    vec = lambda w: pl.BlockSpec((1, w), lambda j, i: (0, j))
    return pl.pallas_call(
        body, name="s5_scan_fwd", grid=(NJ, nb),
        in_specs=[pl.BlockSpec((tb, UW), lambda j, i: (i, j)),
                  pl.BlockSpec((None, UW, 2 * SW), lambda j, i: (j, 0, 0)),
                  pl.BlockSpec((None, 2 * SW, UW), lambda j, i: (j, 0, 0)),
                  vec(SW), vec(SW), stp, stp, tab, tab, vec(UW)],
        out_specs=[pl.BlockSpec((tb, UW), lambda j, i: (i, j)),
                   pl.BlockSpec((None, None, 1, 2 * SW), lambda j, i: (j, i, 0, 0))],
        out_shape=[SDS((L, D), F32), SDS((NJ, nb, 1, 2 * SW), F32)],
        scratch_shapes=[pltpu.VMEM((1, 2 * SW), F32), pltpu.VMEM((NT, ncb, 2 * SW), F32),
                        pltpu.VMEM((NT * UW, 2 * SW), MXU)],
        compiler_params=_cp(48, ARB2),
    )(p, bb, cm, abr, abi, pwr, pwi, par, pai, dskip)


def _s5_gate_fwd(ypre, p, wglu, bglu):
    L = ypre.shape[0]
    tm = min(256, L)

    def body(y_ref, az_ref, wg_ref, bg_ref, ya_ref):
        yg = _gelu(y_ref[...])
        t = _mm(yg, wg_ref[...]) + bg_ref[...]
        act, _ = _silu_and_grad(az_ref[...])
        ya_ref[...] = (yg * jax.nn.sigmoid(t) * act).astype(ya_ref.dtype)

    return pl.pallas_call(
        body, name="s5_gate_fwd", grid=(L // tm,),
        in_specs=[pl.BlockSpec((tm, D), lambda i: (i, 0)), pl.BlockSpec((tm, D), lambda i: (i, 1)),
                  _full((D, D)), _full((1, D))],
        out_specs=pl.BlockSpec((tm, D), lambda i: (i, 0)),
        out_shape=SDS((L, D), MXU),
        compiler_params=_cp(32, ("arbitrary",)),
    )(ypre, p, wglu, bglu)


def _s5_gate_bwd(ypre, p, dx1, wout_e, wglu, bglu):
    L = ypre.shape[0]
    tm = min(256, L)

    def body(y_ref, az_ref, dx1_ref, wo_ref, wg_ref, bg_ref, dyp_ref, daz_ref, yg_ref, dt_ref, ya_ref, gbg_ref):
        @pl.when(pl.program_id(0) == 0)
        def _():
            gbg_ref[...] = jnp.zeros_like(gbg_ref)

        yg, dgelu = _gelu_and_grad(y_ref[...])
        sg = jax.nn.sigmoid(_mm(yg, wg_ref[...]) + bg_ref[...])
        act, dact = _silu_and_grad(az_ref[...])
        y2 = yg * sg
        dya = _mm_nt(dx1_ref[...], wo_ref[...])
        daz_ref[...] = (dya * y2 * dact).astype(daz_ref.dtype)
        dy2 = dya * act
        dt = dy2 * yg * sg * (1.0 - sg)
        dyg = dy2 * sg + _mm_nt(dt, wg_ref[...])
        dyp_ref[...] = dyg * dgelu
        yg_ref[...] = yg.astype(yg_ref.dtype)
        dt_ref[...] = dt.astype(dt_ref.dtype)
        ya_ref[...] = (y2 * act).astype(ya_ref.dtype)
        gbg_ref[...] += jnp.sum(dt, axis=0, keepdims=True)

    row = pl.BlockSpec((tm, D), lambda i: (i, 0))
    return pl.pallas_call(
        body, name="s5_gate_bwd", grid=(L // tm,),
        in_specs=[row, pl.BlockSpec((tm, D), lambda i: (i, 1)), row,
                  pl.BlockSpec((D, D), lambda i: (0, 0)), _full((D, D)), _full((1, D))],
        out_specs=[row, row, row, row, row, _full((1, D))],
        out_shape=[SDS((L, D), F32), SDS((L, D), MXU), SDS((L, D), MXU), SDS((L, D), MXU), SDS((L, D), MXU),
                   SDS((1, D), F32)],
        compiler_params=_cp(40, ("arbitrary",)),
    )(ypre, p, dx1, wout_e, wglu, bglu)


def _s5_scan_bwd(p, dypre, states, bb, cm, cmt, abr, abi, pwr, pwi, par, pai, pbr, pbi, dskip):
    L = p.shape[0]
    ncb, tb = _chunks(L)
    nb = L // tb
    rev = lambda i: nb - 1 - i

    def body(u_ref, dy_ref, st_ref, bb_ref, cm_ref, cmt_ref, ar_ref, ai_ref, pwr_ref, pwi_ref, par_ref, pai_ref,
             pbr_ref, pbi_ref, d_ref, du_ref, gd_ref, gcm_ref, gbb_ref, gar_ref, gai_ref,
             lcarry, bu_scr, s_scr, gs_scr, bbp_scr, cmp_scr):
        del cm_ref

        @pl.when(pl.program_id(1) == 0)
        def _():
            _pow_weights(bb_ref, pwr_ref, pwi_ref, bbp_scr, False)
            _pow_weights(cmt_ref, pwr_ref, pwi_ref, cmp_scr, True)
            lcarry[...] = jnp.zeros_like(lcarry)
            gd_ref[...] = jnp.zeros_like(gd_ref)
            gcm_ref[...] = jnp.zeros_like(gcm_ref)
            gbb_ref[...] = jnp.zeros_like(gbb_ref)
            gar_ref[...] = jnp.zeros_like(gar_ref)
            gai_ref[...] = jnp.zeros_like(gai_ref)

        ar, ai = ar_ref[...], ai_ref[...]
        c = st_ref[...]
        sr, si, _ = _s5_states(u_ref, bb_ref, bbp_scr, par_ref, pai_ref, c[:, :SW], c[:, SW:], ncb, bu_scr)
        s_scr[0] = jnp.concatenate([sr, si], axis=1)
        for t in range(NT):
            bu = bu_scr[t]
            sr, si = _cmul_add(ar, ai, sr, si, bu[:, :SW], bu[:, SW:])
            s_scr[t + 1] = jnp.concatenate([sr, si], axis=1)
        dys = [dy_ref[pl.ds(t, ncb, stride=NT), :] for t in range(NT)]
        for t in range(NT):
            gs_scr[t] = _mm(dys[t], cmt_ref[...])
        f = _mm(jnp.concatenate(dys, axis=1), cmp_scr[...])
        xr, xi = _cscan(f[:, :SW], f[:, SW:], par_ref, pai_ref, True)
        lc = lcarry[...]
        lcr, lci = lc[:, :SW], lc[:, SW:]
        hr, hi = _cmul_add(pbr_ref[CH - ncb:CH, :], -pbi_ref[CH - ncb:CH, :], lcr, lci, xr, xi)
        lcarry[...] = jnp.concatenate([hr[0:1, :], hi[0:1, :]], axis=1)
        row = lax.broadcasted_iota(jnp.int32, hr.shape, 0)
        lr_ = jnp.where(row < ncb - 1, pltpu.roll(hr, ncb - 1, 0), lcr)
        li_ = jnp.where(row < ncb - 1, pltpu.roll(hi, ncb - 1, 0), lci)
        gar = jnp.zeros((1, SW), F32)
        gai = jnp.zeros((1, SW), F32)
        for t in reversed(range(NT)):
            gs = gs_scr[t]
            lr_, li_ = _cmul_add(ar, -ai, lr_, li_, gs[:, :SW], gs[:, SW:])
            rows = pl.ds(t, ncb, stride=NT)
            u_t, dy_t = u_ref[rows, :], dy_ref[rows, :]
            lam = jnp.concatenate([lr_, li_], axis=1)
            gbb_ref[...] += _mm_tn(u_t, lam)
            du_ref[rows, :] = _mm_nt(lam, bb_ref[...]) + dy_t * d_ref[...]
            gd_ref[...] += jnp.sum(dy_t * u_t, axis=0, keepdims=True)
            gcm_ref[...] += _mm_tn(s_scr[t + 1], dy_t)
            sp = s_scr[t]
            spr, spi = sp[:, :SW], sp[:, SW:]
            gar += jnp.sum(lr_ * spr + li_ * spi, axis=0, keepdims=True)
            gai += jnp.sum(li_ * spr - lr_ * spi, axis=0, keepdims=True)
        gar_ref[...] += gar
        gai_ref[...] += gai

    tab = pl.BlockSpec((CH, SW), lambda j, i: (0, j))
    stp = pl.BlockSpec((NT, SW), lambda j, i: (0, j))
    colblk = pl.BlockSpec((tb, UW), lambda j, i: (rev(i), j))
    vec = lambda w: pl.BlockSpec((1, w), lambda j, i: (0, j))
    return pl.pallas_call(
        body, name="s5_scan_bwd", grid=(NJ, nb),
        in_specs=[colblk, colblk,
                  pl.BlockSpec((None, None, 1, 2 * SW), lambda j, i: (j, rev(i), 0, 0)),
                  pl.BlockSpec((None, UW, 2 * SW), lambda j, i: (j, 0, 0)),
                  pl.BlockSpec((None, 2 * SW, UW), lambda j, i: (j, 0, 0)),
                  pl.BlockSpec((None, UW, 2 * SW), lambda j, i: (j, 0, 0)),
                  vec(SW), vec(SW), stp, stp, tab, tab, tab, tab, vec(UW)],
        out_specs=[colblk, vec(UW),
                   pl.BlockSpec((None, 2 * SW, UW), lambda j, i: (j, 0, 0)),
                   pl.BlockSpec((None, UW, 2 * SW), lambda j, i: (j, 0, 0)),
                   vec(SW), vec(SW)],
        out_shape=[SDS((L, D), F32), SDS((1, D), F32),
                   SDS((NJ, 2 * SW, UW), F32), SDS((NJ, UW, 2 * SW), F32),
                   SDS((1, NSTATE), F32), SDS((1, NSTATE), F32)],
        scratch_shapes=[pltpu.VMEM((1, 2 * SW), F32), pltpu.VMEM((NT, ncb, 2 * SW), F32),
                        pltpu.VMEM((NT + 1, ncb, 2 * SW), F32), pltpu.VMEM((NT, ncb, 2 * SW), F32),
                        pltpu.VMEM((NT * UW, 2 * SW), MXU), pltpu.VMEM((NT * UW, 2 * SW), MXU)],
        compiler_params=_cp(56, ARB2),
    )(p, dypre, states, bb, cm, cmt, abr, abi, pwr, pwi, par, pai, pbr, pbi, dskip)


def _rope_tables(L, inv):
    tm = min(512, L)

    def body(inv_ref, cos_ref, sin_ref):
        pos = (lax.broadcasted_iota(jnp.int32, (tm, DK // 2), 0) + pl.program_id(0) * tm).astype(F32)
        ang = pos * inv_ref[...]
        cos_ref[...] = jnp.cos(ang)
        sin_ref[...] = jnp.sin(ang)

    blk = pl.BlockSpec((tm, DK // 2), lambda i: (i, 0))
    return pl.pallas_call(body, name="rope_tables", grid=(L // tm,), in_specs=[_full((1, DK // 2))],
                          out_specs=[blk, blk], out_shape=[SDS((L, DK // 2), F32)] * 2)(inv)


def _rot(x, cos, sin):
    x1, x2 = x[:, :DK // 2], x[:, DK // 2:]
    return jnp.concatenate([x1 * cos - x2 * sin, x1 * sin + x2 * cos], axis=1)


def _unrot(d, cos, sin):
    d1, d2 = d[:, :DK // 2], d[:, DK // 2:]
    return jnp.concatenate([d1 * cos + d2 * sin, d2 * cos - d1 * sin], axis=1)


def _ret_decays(h):
    lg = LOG_G[h]
    n = lax.broadcasted_iota(jnp.int32, (CH, CH), 0)
    m = lax.broadcasted_iota(jnp.int32, (CH, CH), 1)
    diff = (n - m).astype(F32)
    decay = jnp.where(n >= m, jnp.exp(lg * jnp.maximum(diff, 0.0)), 0.0)
    idx = lax.broadcasted_iota(jnp.int32, (CH, 1), 0).astype(F32)
    xi = jnp.exp(lg * (idx + 1.0))
    zeta = jnp.exp(lg * (CH - 1.0 - idx))
    return decay, xi, zeta, math.exp(lg * CH)


def _ret_tables(dec_scr, vec_scr):
    for h in range(HEADS):
        decay, xi, zeta, _ = _ret_decays(h)
        dec_scr[h] = decay
        vec_scr[h] = jnp.concatenate([jnp.broadcast_to(xi, (CH, 128)), jnp.broadcast_to(zeta, (CH, 128))], axis=1)


def _ret_chunk_fwd(q, k, v, cos, sin, s_prev_b, decay, xi, zeta):
    qr = _rot(q, cos, sin)
    kr = _rot(k, cos, sin) * (DK ** -0.5)
    scores = _mm_nt(qr, kr) * decay
    o = _mm(scores, v) + _mm(qr * xi, s_prev_b)
    local = _mm_tn(kr * zeta, v)
    mu = jnp.mean(o, axis=-1, keepdims=True)
    oc = o - mu
    rstd = lax.rsqrt(jnp.mean(oc * oc, axis=-1, keepdims=True) + EPS)
    return qr, kr, scores, local, oc * rstd, rstd


def _ret_fwd(p, cos, sin, gain):
    L = p.shape[0]
    nb = L // CH

    def body(q_ref, k_ref, v_ref, bz_ref, cos_ref, sin_ref, g_ref, yb_ref, st_ref, state, dec_scr, vec_scr):
        @pl.when(pl.program_id(0) == 0)
        def _():
            state[...] = jnp.zeros_like(state)
            _ret_tables(dec_scr, vec_scr)

        cos, sin = cos_ref[...], sin_ref[...]
        act, _ = _silu_and_grad(bz_ref[...])
        for h in range(HEADS):
            hs = slice(h * DK, (h + 1) * DK)
            xi, zeta = vec_scr[h, :, 0:1], vec_scr[h, :, 128:129]
            s_prev = state[h]
            s_prev_b = s_prev.astype(MXU)
            st_ref[h] = s_prev_b
            _, _, _, local, on, _ = _ret_chunk_fwd(q_ref[:, hs], k_ref[:, hs], v_ref[:, hs], cos, sin,
                                                   s_prev_b, dec_scr[h], xi, zeta)
            state[h] = s_prev * math.exp(LOG_G[h] * CH) + local
            yb_ref[:, hs] = (on * g_ref[:, hs] * act[:, hs]).astype(yb_ref.dtype)

    col = lambda c: pl.BlockSpec((CH, D), lambda i: (i, c))
    rope = pl.BlockSpec((CH, DK // 2), lambda i: (i, 0))
    return pl.pallas_call(
        body, name="ret_fwd", grid=(nb,),
        in_specs=[col(2), col(3), col(4), col(5), rope, rope, _full((1, D))],
        out_specs=[pl.BlockSpec((CH, D), lambda i: (i, 0)),
                   pl.BlockSpec((None, HEADS, DK, DK), lambda i: (i, 0, 0, 0))],
        out_shape=[SDS((L, D), MXU), SDS((nb, HEADS, DK, DK), MXU)],
        scratch_shapes=[pltpu.VMEM((HEADS, DK, DK), F32), pltpu.VMEM((HEADS, CH, CH), F32),
                        pltpu.VMEM((HEADS, CH, 256), F32)],
        compiler_params=_cp(40, ("arbitrary",)),
    )(p, p, p, p, cos, sin, gain)


def _ret_bwd(p, cos, sin, gain, states, dx1, wout_e, du, daz):
    L = p.shape[0]
    nb = L // CH
    rev = lambda i: nb - 1 - i

    def body(q_ref, k_ref, v_ref, bz_ref, cos_ref, sin_ref, g_ref, st_ref, dx1_ref, wo_ref, du_ref, daz_ref,
             dp_ref, yb_ref, gg_ref, gstate, dec_scr, vec_scr):
        @pl.when(pl.program_id(0) == 0)
        def _():
            gstate[...] = jnp.zeros_like(gstate)
            gg_ref[...] = jnp.zeros_like(gg_ref)
            _ret_tables(dec_scr, vec_scr)

        cos, sin = cos_ref[...], sin_ref[...]
        act, dact = _silu_and_grad(bz_ref[...])
        dyb = _mm_nt(dx1_ref[...], wo_ref[...])
        dp_ref[:, 0:D] = du_ref[...].astype(dp_ref.dtype)
        dp_ref[:, D:2 * D] = daz_ref[...]
        for h in range(HEADS):
            hs = slice(h * DK, (h + 1) * DK)
            col = lambda part: slice((2 + part) * D + h * DK, (2 + part) * D + (h + 1) * DK)
            decay = dec_scr[h]
            xi, zeta = vec_scr[h, :, 0:1], vec_scr[h, :, 128:129]
            v = v_ref[:, hs]
            s_prev_b = st_ref[h]
            qr, kr, scores, _, on, rstd = _ret_chunk_fwd(q_ref[:, hs], k_ref[:, hs], v, cos, sin, s_prev_b,
                                                         decay, xi, zeta)
            gain_h = g_ref[:, hs]
            out = on * gain_h
            yb_ref[:, hs] = (out * act[:, hs]).astype(yb_ref.dtype)
            dyb_h = dyb[:, hs]
            dp_ref[:, col(3)] = (dyb_h * out * dact[:, hs]).astype(dp_ref.dtype)
            dout = dyb_h * act[:, hs]
            gg_ref[:, hs] += jnp.sum(dout * on, axis=0, keepdims=True)
            don = dout * gain_h
            do = rstd * (don - jnp.mean(don, axis=-1, keepdims=True)
                         - on * jnp.mean(don * on, axis=-1, keepdims=True))
            gnext = gstate[h]
            gnext_b = gnext.astype(MXU)
            dscores = _mm_nt(do, v) * decay
            dp_ref[:, col(2)] = (_mm_tn(scores, do) + _mm(kr * zeta, gnext_b)).astype(dp_ref.dtype)
            dqr = _mm(dscores, kr) + _mm_nt(do, s_prev_b) * xi
            dkr = _mm_tn(dscores, qr) + _mm_nt(v, gnext_b) * zeta
            gstate[h] = gnext * math.exp(LOG_G[h] * CH) + _mm_tn(qr * xi, do)
            dp_ref[:, col(0)] = _unrot(dqr, cos, sin).astype(dp_ref.dtype)
            dp_ref[:, col(1)] = (_unrot(dkr, cos, sin) * (DK ** -0.5)).astype(dp_ref.dtype)

    col = lambda c: pl.BlockSpec((CH, D), lambda i: (rev(i), c))
    rope = pl.BlockSpec((CH, DK // 2), lambda i: (rev(i), 0))
    outc = col(0)
    act_out = SDS((L, D), MXU)
    return pl.pallas_call(
        body, name="ret_bwd", grid=(nb,),
        in_specs=[col(2), col(3), col(4), col(5), rope, rope, _full((1, D)),
                  pl.BlockSpec((None, HEADS, DK, DK), lambda i: (rev(i), 0, 0, 0)),
                  outc, pl.BlockSpec((D, D), lambda i: (1, 0)), outc, outc],
        out_specs=[pl.BlockSpec((CH, NIN), lambda i: (rev(i), 0)), outc, _full((1, D))],
        out_shape=[SDS((L, NIN), MXU), act_out, SDS((1, D), F32)],
        scratch_shapes=[pltpu.VMEM((HEADS, DK, DK), F32), pltpu.VMEM((HEADS, CH, CH), F32),
                        pltpu.VMEM((HEADS, CH, 256), F32)],
        compiler_params=_cp(48, ("arbitrary",)),
    )(p, p, p, p, cos, sin, gain, states, dx1, wout_e, du, daz)


def _out_even(x, ya, yb, wout):
    L = x.shape[0]
    tm = min(512, L)

    def body(x_ref, ya_ref, yb_ref, w_ref, o_ref):
        cat = jnp.concatenate([ya_ref[...], yb_ref[...]], axis=1)
        o_ref[...] = x_ref[...] + jnp.dot(cat, w_ref[...], preferred_element_type=F32)

    row = pl.BlockSpec((tm, D), lambda i: (i, 0))
    return pl.pallas_call(
        body, name="out_even", grid=(L // tm,), in_specs=[row, row, row, _full((DI, D))],
        out_specs=row, out_shape=SDS((L, D), F32), compiler_params=_cp(32, ("arbitrary",)),
    )(x, ya, yb, wout)


def _sgu_core(pv, gain, ws_ref, bs_ref):
    pu, pvv, z = pv[:, :DI], pv[:, DI:2 * DI], pv[:, 2 * DI:]
    u, gu = _gelu_and_grad(pu)
    v, gv = _gelu_and_grad(pvv)
    mu = jnp.mean(v, axis=-1, keepdims=True)
    vc = v - mu
    rstd = lax.rsqrt(jnp.mean(vc * vc, axis=-1, keepdims=True) + EPS)
    vhat = vc * rstd
    vn = vhat * gain
    t = lax.broadcasted_iota(jnp.int32, (CH, CH), 0)
    s_ = lax.broadcasted_iota(jnp.int32, (CH, CH), 1)
    mask = t >= s_
    wm = [jnp.where(mask, ws_ref[g], 0.0).astype(MXU) for g in range(SG)]
    s = jnp.concatenate([_mm(wm[g], vn[:, g * SGD:(g + 1) * SGD]) + bs_ref[g] for g in range(SG)], axis=1)
    return gu, gv, z, u, vhat, rstd, vn, mask, wm, s


def _sgu_fwd(p2, x1, gain, wsp, bsp, wout, fnorm, tgt):
    L = p2.shape[0]

    def body(p_ref, x1_ref, g_ref, ws_ref, bs_ref, wo_ref, fn_ref, t_ref, dx2_ref, gfn_ref, loss_ref):
        @pl.when(pl.program_id(0) == 0)
        def _():
            gfn_ref[...] = jnp.zeros_like(gfn_ref)
            loss_ref[...] = jnp.zeros_like(loss_ref)

        _, _, z, u, _, _, _, _, _, s = _sgu_core(p_ref[...], g_ref[...], ws_ref, bs_ref)
        act, _ = _silu_and_grad(z)
        x2 = x1_ref[...] + _mm(u * s * act, wo_ref[...])
        xhat, r = _rms(x2)
        fn = fn_ref[...]
        e = xhat * fn - t_ref[...]
        loss_ref[...] += 0.5 * jnp.sum(jnp.mean(e * e, axis=-1, keepdims=True), axis=0, keepdims=True)
        do = e * (1.0 / D)
        gfn_ref[...] += jnp.sum(do * xhat, axis=0, keepdims=True)
        dxhat = do * fn
        dx2_ref[...] = r * (dxhat - xhat * jnp.mean(dxhat * xhat, axis=-1, keepdims=True))

    row = pl.BlockSpec((CH, D), lambda i: (i, 0))
    return pl.pallas_call(
        body, name="sgu_fwd", grid=(L // CH,),
        in_specs=[pl.BlockSpec((CH, NIN), lambda i: (i, 0)), row, _full((1, DI)), _full((SG, CH, CH)),
                  _full((SG, CH, 1)), _full((DI, D)), _full((1, D)), row],
        out_specs=[row, _full((1, D)), _full((1, 1))],
        out_shape=[SDS((L, D), F32), SDS((1, D), F32), SDS((1, 1), F32)],
        compiler_params=_cp(48, ("arbitrary",)),
    )(p2, x1, gain, wsp, bsp, wout, fnorm, tgt)


def _sgu_bwd(p2, dx2, gain, wsp, bsp, wout):
    L = p2.shape[0]

    def body(p_ref, dx2_ref, g_ref, ws_ref, bs_ref, wo_ref, dp_ref, y_ref, gg_ref, gws_ref, gbs_ref):
        @pl.when(pl.program_id(0) == 0)
        def _():
            gg_ref[...] = jnp.zeros_like(gg_ref)
            gws_ref[...] = jnp.zeros_like(gws_ref)
            gbs_ref[...] = jnp.zeros_like(gbs_ref)

        gain = g_ref[...]
        gu, gv, z, u, vhat, rstd, vn, mask, wm, s = _sgu_core(p_ref[...], gain, ws_ref, bs_ref)
        act, dact = _silu_and_grad(z)
        y_ref[...] = (u * s * act).astype(y_ref.dtype)
        dy = _mm_nt(dx2_ref[...], wo_ref[...])
        du = dy * s * act
        ds = dy * u * act
        dz = dy * u * s * dact
        dvn = []
        for g in range(SG):
            ds_g = ds[:, g * SGD:(g + 1) * SGD]
            vn_g = vn[:, g * SGD:(g + 1) * SGD]
            gbs_ref[g] += jnp.sum(ds_g, axis=1, keepdims=True)
            gws_ref[g] += jnp.where(mask, _mm_nt(ds_g, vn_g), 0.0)
            dvn.append(_mm_tn(wm[g], ds_g))
        dvn = jnp.concatenate(dvn, axis=1)
        gg_ref[...] += jnp.sum(dvn * vhat, axis=0, keepdims=True)
        dvhat = dvn * gain
        dv = rstd * (dvhat - jnp.mean(dvhat, axis=-1, keepdims=True)
                     - vhat * jnp.mean(dvhat * vhat, axis=-1, keepdims=True))
        dp_ref[...] = jnp.concatenate([du * gu, dv * gv, dz], axis=1).astype(dp_ref.dtype)

    return pl.pallas_call(
        body, name="sgu_bwd", grid=(L // CH,),
        in_specs=[pl.BlockSpec((CH, NIN), lambda i: (i, 0)), pl.BlockSpec((CH, D), lambda i: (i, 0)),
                  _full((1, DI)), _full((SG, CH, CH)), _full((SG, CH, 1)), _full((DI, D))],
        out_specs=[pl.BlockSpec((CH, NIN), lambda i: (i, 0)), pl.BlockSpec((CH, DI), lambda i: (i, 0)),
                   _full((1, DI)), _full((SG, CH, CH)), _full((SG, CH, 1))],
        out_shape=[SDS((L, NIN), MXU), SDS((L, DI), MXU), SDS((1, DI), F32), SDS((SG, CH, CH), F32),
                   SDS((SG, CH, 1), F32)],
        compiler_params=_cp(48, ("arbitrary",)),
    )(p2, dx2, gain, wsp, bsp, wout)


def _my_index():
    return 4 * lax.axis_index("x") + 2 * lax.axis_index("y") + lax.axis_index("c")


def _ordered_sum(land_ref, own, me):
    g = None
    for s in range(NDEV):
        part = jnp.where(me == s, own, land_ref[s].astype(F32))
        g = part if g is None else g + part
    return g


def _adamw_math(w, m, v, g):
    mn = ADAM_B1 * m + (1.0 - ADAM_B1) * g
    vn = ADAM_B2 * v + (1.0 - ADAM_B2) * (g * g)
    mhat = mn / BC1
    vhat = vn / BC2
    return g, -ADAM_LR * (mhat / (jnp.sqrt(vhat) + ADAM_EPS) + ADAM_WD * w), mn, vn


def _adamw(w, m, v, land, own, name):
    R, C = w.shape
    tr = R
    for cand in (256, 128, 64, 32, 16, 8):
        if R % cand == 0 and R > cand:
            tr = cand
            break

    def body(w_ref, m_ref, v_ref, land_ref, own_ref, g_ref, d_ref, mo_ref, vo_ref):
        g = _ordered_sum(land_ref, own_ref[...].astype(F32), _my_index())
        for o, val in zip((g_ref, d_ref, mo_ref, vo_ref), _adamw_math(w_ref[...], m_ref[...], v_ref[...], g)):
            o[...] = val

    blk = pl.BlockSpec((tr, C), lambda i: (i, 0))
    out = SDS((R, C), F32)
    return pl.pallas_call(
        body, name=name, grid=(R // tr,),
        in_specs=[blk, blk, blk, pl.BlockSpec((NDEV, tr, C), lambda i: (0, i, 0)), blk],
        out_specs=[blk, blk, blk, blk], out_shape=[out, out, out, out],
        compiler_params=_cp(40, ("arbitrary",)),
    )(w, m, v, land, own)


def _adamw_many(ws, ms, vs, lands, owns, name):
    k = len(ws)

    def body(*refs):
        ins, outs = refs[:5 * k], refs[5 * k:]
        me = _my_index()
        for i in range(k):
            w_ref, m_ref, v_ref, land_ref, own_ref = (ins[j * k + i] for j in range(5))
            g = _ordered_sum(land_ref, own_ref[...], me)
            for j, val in enumerate(_adamw_math(w_ref[...], m_ref[...], v_ref[...], g)):
                outs[j * k + i][...] = val

    out_shape = [SDS(w.shape, F32) for _ in range(4) for w in ws]
    res = pl.pallas_call(body, name=name, out_shape=out_shape, compiler_params=_cp(60))(*ws, *ms, *vs, *lands, *owns)
    return [res[j * k:(j + 1) * k] for j in range(4)]


MESH = pl.DeviceIdType.MESH
HBM_SPEC = pl.BlockSpec(memory_space=pltpu.HBM)
SEM_SPEC = pl.BlockSpec(memory_space=pltpu.SEMAPHORE)
EFFECT = pltpu.SideEffectType.DATAFLOW_SIDE_EFFECTING


def _me_and_peers():
    x, y, c = lax.axis_index("x"), lax.axis_index("y"), lax.axis_index("c")
    me = 4 * x + 2 * y + c
    peers = []
    for r in range(1, NDEV):
        px, py, pc = x ^ ((r >> 2) & 1), y ^ ((r >> 1) & 1), c ^ (r & 1)
        peers.append(((px, py, pc), 4 * px + 2 * py + pc))
    return me, peers


def _land_shape(a, scatter):
    return (NDEV,) + (a.shape[1:] if scatter else a.shape)


def _remote(src, dst, send_sems, recv_sems, r, k, n, dev):
    i = r * n + k
    return pltpu.make_async_remote_copy(src_ref=src, dst_ref=dst, send_sem=send_sems.at[i], recv_sem=recv_sems.at[i],
                                        device_id=dev, device_id_type=MESH)


def _exchange(arrays, scatter, name):
    n = len(arrays)
    out_shape = [SDS(_land_shape(a, scatter), a.dtype) for a in arrays]

    def body(*refs):
        ins, outs = refs[:n], refs[n:2 * n]
        send_sems, recv_sems, loc_sems = refs[2 * n:]
        me, peers = _me_and_peers()
        local = []
        for k in range(n):
            src = ins[k].at[me] if scatter else ins[k]
            cp = pltpu.make_async_copy(src, outs[k].at[me], loc_sems.at[k])
            cp.start()
            local.append(cp)
        sends = []
        for r, (dev, lin) in enumerate(peers):
            for k in range(n):
                src = ins[k].at[lin] if scatter else ins[k]
                cp = _remote(src, outs[k].at[me], send_sems, recv_sems, r, k, n, dev)
                cp.start()
                sends.append(cp)
        for r, (dev, lin) in enumerate(peers):
            for k in range(n):
                src = ins[k].at[me] if scatter else ins[k]
                _remote(src, outs[k].at[lin], send_sems, recv_sems, r, k, n, dev).wait_recv()
        for cp in sends:
            cp.wait_send()
        for cp in local:
            cp.wait()

    return pl.pallas_call(
        body, name=name, in_specs=[HBM_SPEC] * n, out_specs=[HBM_SPEC] * n, out_shape=out_shape,
        scratch_shapes=[pltpu.SemaphoreType.DMA(((NDEV - 1) * n,)), pltpu.SemaphoreType.DMA(((NDEV - 1) * n,)),
                        pltpu.SemaphoreType.DMA((n,))],
    )(*arrays)


def _exchange_start(arrays, scatter, name):
    n = len(arrays)
    lands = [lax.empty(_land_shape(a, scatter), a.dtype) for a in arrays]

    def body(*refs):
        ins, lnd = refs[:n], refs[n:2 * n]
        send_sems, recv_sems, own_sems = refs[2 * n:2 * n + 3]
        token = refs[-1]
        me, peers = _me_and_peers()
        for r, (dev, lin) in enumerate(peers):
            for k in range(n):
                src = ins[k].at[lin] if scatter else ins[k]
                _remote(src, lnd[k].at[me], send_sems, recv_sems, r, k, n, dev).start()
        if not scatter:
            for k in range(n):
                pltpu.make_async_copy(ins[k], lnd[k].at[me], own_sems.at[k]).start()
        token[...] = jnp.zeros_like(token)

    sem = pltpu.SemaphoreType.DMA(((NDEV - 1) * n,))
    outs = pl.pallas_call(
        body, name=name,
        out_shape=(sem, sem, pltpu.SemaphoreType.DMA((n,)), *[pltpu.HBM(a.shape, a.dtype) for a in arrays],
                   *[pltpu.HBM(l.shape, l.dtype) for l in lands], SDS((8, 128), F32)),
        in_specs=[HBM_SPEC] * (2 * n),
        out_specs=(SEM_SPEC, SEM_SPEC, SEM_SPEC, *[HBM_SPEC] * (2 * n), pl.BlockSpec(memory_space=pltpu.VMEM)),
        input_output_aliases={k: 3 + k for k in range(2 * n)},
        compiler_params=pltpu.CompilerParams(has_side_effects=EFFECT),
    )(*[pltpu.with_memory_space_constraint(a, pltpu.HBM) for a in arrays],
      *[pltpu.with_memory_space_constraint(l, pltpu.HBM) for l in lands])
    return (n, scatter, outs[0], outs[1], outs[2], outs[3:3 + n], outs[3 + n:3 + 2 * n]), outs[-1]


def _exchange_wait(handle, after, name):
    n, scatter, send_sems, recv_sems, own_sems, thru, lands = handle
    after = tuple(after)

    def body(*refs):
        ins, lnd = refs[:n], refs[n:2 * n]
        send_sems, recv_sems, own_sems = refs[2 * n:2 * n + 3]
        me, peers = _me_and_peers()
        for r, (dev, lin) in enumerate(peers):
            for k in range(n):
                src = ins[k].at[lin] if scatter else ins[k]
                cp = _remote(src, lnd[k].at[lin], send_sems, recv_sems, r, k, n, dev)
                cp.wait_send()
                cp.wait_recv()
        if not scatter:
            for k in range(n):
                pltpu.make_async_copy(ins[k], lnd[k].at[me], own_sems.at[k]).wait()

    outs = pl.pallas_call(
        body, name=name,
        out_shape=(*[pltpu.HBM(a.shape, a.dtype) for a in thru], *[pltpu.HBM(l.shape, l.dtype) for l in lands]),
        in_specs=[HBM_SPEC] * (2 * n) + [SEM_SPEC, SEM_SPEC, SEM_SPEC] + [ANY_SPEC] * len(after),
        out_specs=tuple([HBM_SPEC] * (2 * n)),
        input_output_aliases={k: k for k in range(2 * n)},
        compiler_params=pltpu.CompilerParams(has_side_effects=EFFECT),
    )(*thru, *lands, send_sems, recv_sems, own_sems, *after)
    return list(outs[:n]), list(outs[n:])


def _local_step(x, tgt, norm_even, first_weight, lam_re, lam_im, log_dt, b_re, b_im, c_re, c_im, s5_d, bglu,
                ret_gain, wsp, bsp, fnorm, late_weights, emit, start_token=None):
    L = x.shape[0]
    lr3, li3 = lam_re.reshape(G, 1, P), lam_im.reshape(G, 1, P)
    dt3 = log_dt.reshape(G, 1, 1)
    br3, bi3 = jnp.swapaxes(b_re, 1, 2), jnp.swapaxes(b_im, 1, 2)
    abr3, abi3, bbr3, bbi3 = _s5_disc(lr3, li3, dt3, br3, bi3)
    bb = jnp.concatenate([_embed(bbr3), _embed(bbi3)], axis=2).astype(MXU)
    cm = jnp.concatenate([_embed(jnp.swapaxes(c_re, 1, 2)), -_embed(jnp.swapaxes(c_im, 1, 2))], axis=1).astype(MXU)
    abr, abi = abr3.reshape(1, NSTATE), abi3.reshape(1, NSTATE)
    pwr, pwi, _, _ = _s5_tables(abr, abi, NT, "s5_tables_step")
    par, pai, pbr, pbi = _s5_tables(pwr[NT - 1:NT], pwi[NT - 1:NT], CH, "s5_tables_chunk")
    inv = (ROPE_BASE ** (-jnp.arange(DK // 2, dtype=F32) / (DK // 2))).reshape(1, DK // 2)
    cos, sin = _rope_tables(L, inv)
    bsp3 = bsp.reshape(SG, CH, 1)

    def dep(token):
        return NO_DEPS if token is None else (token,)

    win_e = first_weight((cos, pbi, cm))
    p = _in_proj(x, norm_even, win_e, "in_even", dep(start_token))
    ypre, s5_states = _s5_scan_fwd(p, bb, cm, abr, abi, pwr, pwi, par, pai, s5_d)
    yb, ret_states = _ret_fwd(p, cos, sin, ret_gain)
    wglu, wout_e, norm_odd, win_o, sgu_gain, wout_o = late_weights((ypre, yb))
    ya = _s5_gate_fwd(ypre, p, wglu, bglu)
    x1 = _out_even(x, ya, yb, wout_e)
    p2 = _in_proj(x1, norm_odd, win_o, "in_odd")
    dx2, g_fnorm, loss = _sgu_fwd(p2, x1, sgu_gain, wsp, bsp3, wout_o, fnorm, tgt)

    dp2, y_o, g_sgu_gain, g_wsp, g_bsp = _sgu_bwd(p2, dx2, sgu_gain, wsp, bsp3, wout_o)
    g_wout_o = _wgrad_rows([y_o], dx2, "wgrad_out_odd")
    g_win_o = _wgrad_cols(x1, norm_odd, dp2, "wgrad_in_odd")
    tok = emit("odd", dict(w_in_odd=g_win_o, w_out_odd=g_wout_o))
    dx1, g_norm_odd = _in_proj_bwd_x(dp2, x1, norm_odd, win_o, dx2, "in_odd_bwd", dep(tok))

    dypre, daz, yg, dt, ya2, g_bglu = _s5_gate_bwd(ypre, p, dx1, wout_e, wglu, bglu)
    du, g_d, g_cm, g_bb, g_ar, g_ai = _s5_scan_bwd(p, dypre, s5_states, bb, cm, jnp.swapaxes(cm, 1, 2), abr, abi,
                                                   pwr, pwi, par, pai, pbr, pbi, s5_d)
    dbbr3 = _diag_blocks(g_bb[:, :, :SW], HG, P)
    dbbi3 = _diag_blocks(g_bb[:, :, SW:], HG, P)
    g_c_re = jnp.swapaxes(_diag_blocks(g_cm[:, :SW, :], P, HG), 1, 2)
    g_c_im = -jnp.swapaxes(_diag_blocks(g_cm[:, SW:, :], P, HG), 1, 2)
    g_lr3, g_li3, g_dt3, g_br3, g_bi3 = _s5_disc_bwd(
        lr3, li3, dt3, br3, bi3, g_ar.reshape(G, 1, P), g_ai.reshape(G, 1, P), dbbr3, dbbi3)
    dp, yb2, g_ret_gain = _ret_bwd(p, cos, sin, ret_gain, ret_states, dx1, wout_e, du, daz)
    small = dict(
        s5_lam_re=g_lr3.reshape(G, P), s5_lam_im=g_li3.reshape(G, P),
        s5_log_dt=g_dt3.reshape(1, G), s5_b_re=g_br3, s5_b_im=g_bi3,
        s5_c_re=g_c_re, s5_c_im=g_c_im, s5_d=g_d, s5_b_glu=g_bglu, ret_gn_gain=g_ret_gain,
        norm_odd=g_norm_odd, sgu_norm_gain=g_sgu_gain, sgu_w_spatial=g_wsp, sgu_b_spatial=g_bsp.reshape(SG, CH),
        final_norm=g_fnorm)
    tok = emit("small", small)
    g_win_e = _wgrad_cols(x, norm_even, dp, "wgrad_in_even", dep(tok))
    tok = emit("even_cols", dict(w_in_even=g_win_e))
    g_wout_e = _wgrad_rows([ya2, yb2], dx1, "wgrad_out_even", dep(tok))
    g_wglu = _wgrad_rows([yg], dt, "wgrad_glu", dep(tok))
    tok = emit("even_rows", dict(s5_w_glu=g_wglu, w_out_even=g_wout_e))
    dx, g_norm_even = _in_proj_bwd_x(dp, x, norm_even, win_e, dx1, "in_even_bwd", dep(tok))
    emit("last", dict(norm_even=g_norm_even))
    return loss, dx


WEIGHTS = ['norm_even', 'w_in_even', 's5_lam_re', 's5_lam_im', 's5_log_dt', 's5_b_re', 's5_b_im', 's5_c_re',
           's5_c_im', 's5_d', 's5_w_glu', 's5_b_glu', 'ret_gn_gain', 'w_out_even', 'norm_odd', 'w_in_odd',
           'sgu_norm_gain', 'sgu_w_spatial', 'sgu_b_spatial', 'w_out_odd', 'final_norm']
BIG = ['w_in_even', 's5_w_glu', 'w_out_even', 'w_in_odd', 'w_out_odd']
SHARDED_SMALL = {'norm_odd': D // NDEV, 'sgu_norm_gain': DI // NDEV}
SMALL = [n for n in WEIGHTS if n not in BIG and n != 'norm_even']


def _view(n, a):
    if n in ('s5_b_re', 's5_b_im'):
        return jnp.swapaxes(a[0], 1, 2)
    if n == 'final_norm':
        return a.reshape(1, D)
    return a[0] if a.ndim >= 3 else a


def _unview(n, t, shape):
    if n in ('s5_b_re', 's5_b_im'):
        return jnp.swapaxes(t, 1, 2)[None]
    return t.reshape(shape)


def kernel(x, norm_even, w_in_even, s5_lam_re, s5_lam_im, s5_log_dt, s5_b_re, s5_b_im, s5_c_re, s5_c_im, s5_d, s5_w_glu, s5_b_glu, ret_gn_gain, w_out_even, norm_odd, w_in_odd, sgu_norm_gain, sgu_w_spatial, sgu_b_spatial, w_out_odd, final_norm, loss_target, m_norm_even, m_w_in_even, m_s5_lam_re, m_s5_lam_im, m_s5_log_dt, m_s5_b_re, m_s5_b_im, m_s5_c_re, m_s5_c_im, m_s5_d, m_s5_w_glu, m_s5_b_glu, m_ret_gn_gain, m_w_out_even, m_norm_odd, m_w_in_odd, m_sgu_norm_gain, m_sgu_w_spatial, m_sgu_b_spatial, m_w_out_odd, m_final_norm, v_norm_even, v_w_in_even, v_s5_lam_re, v_s5_lam_im, v_s5_log_dt, v_s5_b_re, v_s5_b_im, v_s5_c_re, v_s5_c_im, v_s5_d, v_s5_w_glu, v_s5_b_glu, v_ret_gn_gain, v_w_out_even, v_norm_odd, v_w_in_odd, v_sgu_norm_gain, v_sgu_w_spatial, v_sgu_b_spatial, v_w_out_odd, v_final_norm):
    args = dict(locals())
    w = {n: args[n] for n in WEIGHTS}
    m = {n: args["m_" + n] for n in WEIGHTS}
    v = {n: args["v_" + n] for n in WEIGHTS}
    me = _my_index()

    first_handle, _ = _exchange_start([w['w_in_even'][0].astype(MXU)], False, "gather_first_start")

    def first_weight(after):
        return _exchange_wait(first_handle, after, "gather_first_wait")[1][0]

    late_own = [w['s5_w_glu'][0].astype(MXU), w['w_out_even'][0].astype(MXU), w['norm_odd'],
                w['w_in_odd'][0].astype(MXU), w['sgu_norm_gain'], w['w_out_odd'][0].astype(MXU)]
    late_handle, start_token = _exchange_start(late_own, False, "gather_late_start")

    def late_weights(after):
        _, (wglu, wout_e, nodd, win_o, sgug, wout_o) = _exchange_wait(late_handle, after, "gather_late_wait")
        return (wglu.reshape(D, D), wout_e.reshape(DI, D), nodd.reshape(1, D), win_o, sgug.reshape(1, DI),
                wout_o.reshape(DI, D))

    pending = {}
    small_last = {}

    def emit(stage, grads):
        if stage == "last":
            small_last.update(grads)
            return None
        names = list(grads) if stage != "small" else SMALL
        handle, token = _exchange_start([grads[n] for n in names], stage != "small", stage + "_start")
        pending[stage] = (handle, names)
        return token

    loss, dx = _local_step(
        x[0], loss_target[0], w['norm_even'], first_weight, w['s5_lam_re'][0], w['s5_lam_im'][0], w['s5_log_dt'][0],
        w['s5_b_re'][0], w['s5_b_im'][0], w['s5_c_re'][0], w['s5_c_im'][0], w['s5_d'], w['s5_b_glu'],
        w['ret_gn_gain'], w['sgu_w_spatial'][0], w['sgu_b_spatial'][0], w['final_norm'].reshape(1, D),
        late_weights, emit, start_token)

    out_g, out_d, out_m, out_v = {}, {}, {}, {}
    after = dx
    for stage in ("odd", "even_cols", "even_rows"):
        handle, names = pending[stage]
        sent, lands = _exchange_wait(handle, (after,), stage + "_wait")
        for n, land, stack in zip(names, lands, sent):
            shp = w[n].shape
            r, c = shp[1], shp[2]
            own = lax.dynamic_index_in_dim(stack, me, 0, keepdims=False)
            res = _adamw(w[n].reshape(r, c), m[n].reshape(r, c), v[n].reshape(r, c), land, own, "adamw_" + n)
            out_g[n], out_d[n], out_m[n], out_v[n] = (t.reshape(shp) for t in res)
            after = res[0]

    handle, names = pending["small"]
    owns, lands = _exchange_wait(handle, (after,), "small_wait")
    (gne,) = _exchange([small_last['norm_even']], False, "gather_norm_even")
    names = names + ['norm_even']
    owns = owns + [small_last['norm_even']]
    lands = lands + [gne]
    for i, n in enumerate(names):
        if n in SHARDED_SMALL:
            width = SHARDED_SMALL[n]
            owns[i] = lax.dynamic_slice_in_dim(owns[i], me * width, width, axis=1)
            lands[i] = lax.dynamic_slice_in_dim(lands[i], me * width, width, axis=2)
    res = _adamw_many([_view(n, w[n]) for n in names], [_view(n, m[n]) for n in names],
                      [_view(n, v[n]) for n in names], lands, owns, "adamw_small")
    for dst, vals in zip((out_g, out_d, out_m, out_v), res):
        for n, t in zip(names, vals):
            dst[n] = _unview(n, t, w[n].shape)

    loss_total = lax.psum(loss[0, 0], AXES)
    return (loss_total, dx[None], *[out_g[n] for n in WEIGHTS], *[out_d[n] for n in WEIGHTS],
            *[out_m[n] for n in WEIGHTS], *[out_v[n] for n in WEIGHTS])
```

```python
import math

import jax
import jax.numpy as jnp
from jax import lax
from jax.experimental import pallas as pl
from jax.experimental.pallas import tpu as pltpu

F32 = jnp.float32
MXU = jnp.bfloat16
AXES = ("x", "y", "c")
NDEV = 8
D = 1024
NIN = 6144
WIN_BLK = NIN // NDEV
DI = 2048
G, P, HG = 64, 64, 16
GB = 8
NJ = G // GB
SW = GB * P
UW = GB * HG
NSTATE = G * P
HEADS, DK = 4, 256
CH = 128
SG, SGD = 4, 512
EPS = 1e-6
ROPE_BASE = 10000.0
VMEM_CAP_V7X = 64 * 1024 * 1024
LOG_G = [math.log1p(-2.0 ** (-5.0 - h)) for h in range(HEADS)]
GELU_C = math.sqrt(2.0 / math.pi)

ADAM_LR, ADAM_B1, ADAM_B2, ADAM_EPS, ADAM_WD, ADAM_STEP = 0.001, 0.9, 0.999, 1e-08, 0.01, 10
BC1 = 1.0 - ADAM_B1 ** ADAM_STEP
BC2 = 1.0 - ADAM_B2 ** ADAM_STEP

SDS = jax.ShapeDtypeStruct
ARB2 = ("arbitrary", "arbitrary")


def _cp(vmem_mib, sem=None):
    kw = dict(vmem_limit_bytes=min(vmem_mib * 1024 * 1024, VMEM_CAP_V7X - 4 * 1024 * 1024))
    if sem is not None:
        kw["dimension_semantics"] = sem
    return pltpu.CompilerParams(**kw)


def _mm(a, b):
    return jnp.dot(a.astype(MXU), b.astype(MXU), preferred_element_type=F32)


def _mm_nt(a, b):
    return lax.dot_general(a.astype(MXU), b.astype(MXU), (((1,), (1,)), ((), ())), preferred_element_type=F32)


def _mm_tn(a, b):
    return lax.dot_general(a.astype(MXU), b.astype(MXU), (((0,), (0,)), ((), ())), preferred_element_type=F32)


def _gelu(x):
    return _gelu_and_grad(x)[0]


def _gelu_and_grad(x):
    x2 = x * x
    th = jnp.tanh(GELU_C * x * (1.0 + 0.044715 * x2))
    hp = 0.5 * (1.0 + th)
    return x * hp, hp + 0.5 * x * (1.0 - th * th) * GELU_C * (1.0 + 3.0 * 0.044715 * x2)


def _silu_and_grad(x):
    s = jax.nn.sigmoid(x)
    return x * s, s * (1.0 + x * (1.0 - s))


def _full(shape):
    nd = len(shape)
    return pl.BlockSpec(shape, lambda *_: (0,) * nd)


def _rms(xf):
    r = lax.rsqrt(jnp.mean(xf * xf, axis=-1, keepdims=True) + EPS)
    return xf * r, r


ANY_SPEC = pl.BlockSpec(memory_space=pl.ANY)
NO_DEPS = ()


def _load_once(src_hbm, dst_vmem, sem):
    @pl.when(pl.program_id(0) == 0)
    def _():
        cp = pltpu.make_async_copy(src_hbm, dst_vmem, sem)
        cp.start()
        cp.wait()


def _in_proj(x, gain, wst, name, deps=NO_DEPS):
    L = x.shape[0]
    tm = min(512, L)

    def body(x_ref, g_ref, w_hbm, *rest):
        o_ref, w_scr, sem = rest[len(deps):]
        _load_once(w_hbm, w_scr, sem)
        xhat, _ = _rms(x_ref[...])
        h = (xhat * g_ref[...]).astype(MXU)
        for c in range(NDEV):
            o_ref[:, c * WIN_BLK:(c + 1) * WIN_BLK] = jnp.dot(h, w_scr[c], preferred_element_type=F32)

    return pl.pallas_call(
        body, name=name, grid=(L // tm,),
        in_specs=[pl.BlockSpec((tm, D), lambda i: (i, 0)), _full((1, D)), ANY_SPEC] + [ANY_SPEC] * len(deps),
        out_specs=pl.BlockSpec((tm, NIN), lambda i: (i, 0)),
        out_shape=SDS((L, NIN), F32),
        scratch_shapes=[pltpu.VMEM((NDEV, D, WIN_BLK), MXU), pltpu.SemaphoreType.DMA(())],
        compiler_params=_cp(56, ("arbitrary",)),
    )(x, gain, wst, *deps)


def _in_proj_bwd_x(dp, x, gain, wst, dres, name, deps=NO_DEPS):
    L = x.shape[0]
    tm = min(512, L)

    def body(dp_ref, x_ref, g_ref, w_hbm, dres_ref, *rest):
        dx_ref, gg_ref, w_scr, sem = rest[len(deps):]
        _load_once(w_hbm, w_scr, sem)

        @pl.when(pl.program_id(0) == 0)
        def _():
            gg_ref[...] = jnp.zeros_like(gg_ref)

        dh = _mm_nt(dp_ref[:, 0:WIN_BLK], w_scr[0])
        for c in range(1, NDEV):
            dh += _mm_nt(dp_ref[:, c * WIN_BLK:(c + 1) * WIN_BLK], w_scr[c])
        xhat, r = _rms(x_ref[...])
        dxhat = dh * g_ref[...]
        dx_ref[...] = dres_ref[...] + r * (dxhat - xhat * jnp.mean(dxhat * xhat, axis=-1, keepdims=True))
        gg_ref[...] += jnp.sum(dh * xhat, axis=0, keepdims=True)

    row = pl.BlockSpec((tm, D), lambda i: (i, 0))
    return pl.pallas_call(
        body, name=name, grid=(L // tm,),
        in_specs=[pl.BlockSpec((tm, NIN), lambda i: (i, 0)), row, _full((1, D)), ANY_SPEC, row]
        + [ANY_SPEC] * len(deps),
        out_specs=[row, _full((1, D))],
        out_shape=[SDS((L, D), F32), SDS((1, D), F32)],
        scratch_shapes=[pltpu.VMEM((NDEV, D, WIN_BLK), MXU), pltpu.SemaphoreType.DMA(())],
        compiler_params=_cp(56, ("arbitrary",)),
    )(dp, x, gain, wst, dres, *deps)


def _wgrad_cols(x, gain, dp, name, deps=NO_DEPS):
    L = x.shape[0]
    tk = min(512, L)
    nk = L // tk
    halves = 2
    nh = NDEV // halves

    def body(x_ref, g_ref, dp_ref, *rest):
        o_ref, acc = rest[len(deps):]
        k = pl.program_id(1)

        @pl.when(k == 0)
        def _():
            acc[...] = jnp.zeros_like(acc)

        xhat, _ = _rms(x_ref[...])
        acc[...] += _mm_tn(xhat * g_ref[...], dp_ref[...])

        @pl.when(k == nk - 1)
        def _():
            for c in range(nh):
                o_ref[c] = acc[:, c * WIN_BLK:(c + 1) * WIN_BLK].astype(o_ref.dtype)

    return pl.pallas_call(
        body, name=name, grid=(halves, nk),
        in_specs=[pl.BlockSpec((tk, D), lambda n, k: (k, 0)), _full((1, D)),
                  pl.BlockSpec((tk, nh * WIN_BLK), lambda n, k: (k, n))] + [ANY_SPEC] * len(deps),
        out_specs=pl.BlockSpec((nh, D, WIN_BLK), lambda n, k: (n, 0, 0)),
        out_shape=SDS((NDEV, D, WIN_BLK), MXU),
        scratch_shapes=[pltpu.VMEM((D, nh * WIN_BLK), F32)],
        compiler_params=_cp(56, ARB2),
    )(x, gain, dp, *deps)


def _wgrad_rows(a_parts, b, name, deps=NO_DEPS):
    L, N = b.shape
    na = len(a_parts)
    widths = [a.shape[1] for a in a_parts]
    M = sum(widths)
    tk = min(512, L)
    nk = L // tk

    def body(*refs):
        a_refs, b_ref = refs[:na], refs[na]
        o_ref, acc = refs[na + 1 + len(deps):]
        k = pl.program_id(0)

        @pl.when(k == 0)
        def _():
            acc[...] = jnp.zeros_like(acc)

        bv = b_ref[...].astype(MXU)
        off = 0
        for a_ref, wd in zip(a_refs, widths):
            acc[off:off + wd, :] += _mm_tn(a_ref[...], bv)
            off += wd

        @pl.when(k == nk - 1)
        def _():
            o_ref[...] = acc[...].astype(o_ref.dtype).reshape(o_ref.shape)

    return pl.pallas_call(
        body, name=name, grid=(nk,),
        in_specs=[pl.BlockSpec((tk, wd), lambda k: (k, 0)) for wd in widths]
        + [pl.BlockSpec((tk, N), lambda k: (k, 0))] + [ANY_SPEC] * len(deps),
        out_specs=_full((NDEV, M // NDEV, N)),
        out_shape=SDS((NDEV, M // NDEV, N), MXU),
        scratch_shapes=[pltpu.VMEM((M, N), F32)],
        compiler_params=_cp(48, ("arbitrary",)),
    )(*a_parts, b, *deps)


def _s5_disc_fn(lr_raw, li, logdt, br, bi):
    lr = jnp.minimum(lr_raw, -1e-4)
    dt = jnp.exp(logdt)
    mag = jnp.exp(lr * dt)
    abr = mag * jnp.cos(li * dt)
    abi = mag * jnp.sin(li * dt)
    den = lr * lr + li * li
    nre = abr - 1.0
    nim = abi
    zr = (nre * lr + nim * li) / den
    zi = (nim * lr - nre * li) / den
    return abr, abi, zr * br - zi * bi, zr * bi + zi * br


def _s5_disc(lr, li, logdt, br, bi):
    def body(lr_ref, li_ref, dt_ref, br_ref, bi_ref, abr_ref, abi_ref, bbr_ref, bbi_ref):
        abr, abi, bbr, bbi = _s5_disc_fn(lr_ref[...], li_ref[...], dt_ref[...], br_ref[...], bi_ref[...])
        abr_ref[...] = abr
        abi_ref[...] = abi
        bbr_ref[...] = bbr
        bbi_ref[...] = bbi

    s1, s3 = SDS((G, 1, P), F32), SDS((G, HG, P), F32)
    return pl.pallas_call(body, name="s5_disc", out_shape=[s1, s1, s3, s3])(lr, li, logdt, br, bi)


def _s5_disc_bwd(lr, li, logdt, br, bi, dabr, dabi, dbbr, dbbi):
    def body(lr_ref, li_ref, dt_ref, br_ref, bi_ref, c0, c1, c2, c3, o0, o1, o2, o3, o4):
        _, vjp = jax.vjp(_s5_disc_fn, lr_ref[...], li_ref[...], dt_ref[...], br_ref[...], bi_ref[...])
        g = vjp((c0[...], c1[...], c2[...], c3[...]))
        for o, v in zip((o0, o1, o2, o3, o4), g):
            o[...] = v

    s1, s3 = SDS((G, 1, P), F32), SDS((G, HG, P), F32)
    return pl.pallas_call(body, name="s5_disc_bwd", out_shape=[s1, s1, SDS((G, 1, 1), F32), s3, s3])(
        lr, li, logdt, br, bi, dabr, dabi, dbbr, dbbi)


def _s5_tables(abr, abi, rows, name):
    def body(ar_ref, ai_ref, pfr, pfi, pbr, pbi):
        pfr[0:1, :] = ar_ref[...]
        pfi[0:1, :] = ai_ref[...]
        pbr[rows - 1:rows, :] = ar_ref[...]
        pbi[rows - 1:rows, :] = ai_ref[...]
        n = 1
        while n < rows:
            er, ei = pfr[n - 1:n, :], pfi[n - 1:n, :]
            xr, xi = pfr[0:n, :], pfi[0:n, :]
            pfr[n:2 * n, :] = er * xr - ei * xi
            pfi[n:2 * n, :] = er * xi + ei * xr
            yr, yi = pbr[rows - n:rows, :], pbi[rows - n:rows, :]
            pbr[rows - 2 * n:rows - n, :] = er * yr - ei * yi
            pbi[rows - 2 * n:rows - n, :] = er * yi + ei * yr
            n *= 2

    s = SDS((rows, NSTATE), F32)
    return pl.pallas_call(body, name=name, out_shape=[s, s, s, s], compiler_params=_cp(40))(abr, abi)


def _cscan(br, bi, pr_ref, pi_ref, reverse):
    T = br.shape[0]
    sign = -1.0 if reverse else 1.0
    row = lax.broadcasted_iota(jnp.int32, br.shape, 0)
    k = 1
    while k < T:
        akr = pr_ref[k - 1:k, :]
        aki = sign * pi_ref[k - 1:k, :]

        def shift(v):
            if k % 8 == 0:
                z = jnp.zeros((k, v.shape[1]), v.dtype)
                return jnp.concatenate([v[k:], z], 0) if reverse else jnp.concatenate([z, v[:T - k]], 0)
            if reverse:
                return jnp.where(row < T - k, pltpu.roll(v, T - k, 0), 0.0)
            return jnp.where(row >= k, pltpu.roll(v, k, 0), 0.0)

        sr, si = shift(br), shift(bi)
        br, bi = br + akr * sr - aki * si, bi + akr * si + aki * sr
        k *= 2
    return br, bi


def _embed(t):
    a, b = t.shape[1], t.shape[2]
    return jnp.einsum("jgab,gh->jgahb", t.reshape(NJ, GB, a, b), jnp.eye(GB, dtype=t.dtype)).reshape(NJ, GB * a, GB * b)


def _diag_blocks(t, a, b):
    return jnp.einsum("jgahb,gh->jgab", t.reshape(NJ, GB, a, GB, b), jnp.eye(GB, dtype=t.dtype)).reshape(G, a, b)


NT = 16


def _chunks(L):
    ncb = min(CH, L // NT)
    return ncb, NT * ncb


def _cmul_add(ar, ai, xr, xi, br, bi):
    return ar * xr - ai * xi + br, ar * xi + ai * xr + bi


def _pow_weights(w_ref, pwr_ref, pwi_ref, dst, adjoint):
    w = w_ref[...].astype(F32)
    wr, wi = w[:, :SW], w[:, SW:]
    for t in range(NT):
        k = t if adjoint else NT - 1 - t
        if k == 0:
            blk = w
        else:
            pr, pi = pwr_ref[k - 1:k, :], pwi_ref[k - 1:k, :]
            if adjoint:
                blk = jnp.concatenate([pr * wr + pi * wi, pr * wi - pi * wr], axis=1)
            else:
                blk = jnp.concatenate([pr * wr - pi * wi, pr * wi + pi * wr], axis=1)
        dst[t * UW:(t + 1) * UW, :] = blk.astype(dst.dtype)


def _s5_states(u_ref, bb_ref, bbp_scr, par_ref, pai_ref, cr, ci, ncb, bu_scr):
    us = [u_ref[pl.ds(t, ncb, stride=NT), :] for t in range(NT)]
    for t in range(NT):
        bu_scr[t] = _mm(us[t], bb_ref[...])
    e = _mm(jnp.concatenate(us, axis=1), bbp_scr[...])
    xr, xi = _cscan(e[:, :SW], e[:, SW:], par_ref, pai_ref, False)
    fr, fi = _cmul_add(par_ref[0:ncb, :], pai_ref[0:ncb, :], cr, ci, xr, xi)
    row = lax.broadcasted_iota(jnp.int32, fr.shape, 0)
    cinr = jnp.where(row >= 1, pltpu.roll(fr, 1, 0), cr)
    cini = jnp.where(row >= 1, pltpu.roll(fi, 1, 0), ci)
    return cinr, cini, jnp.concatenate([fr[ncb - 1:ncb, :], fi[ncb - 1:ncb, :]], axis=1)


def _s5_scan_fwd(p, bb, cm, abr, abi, pwr, pwi, par, pai, dskip):
    L = p.shape[0]
    ncb, tb = _chunks(L)
    nb = L // tb

    def body(u_ref, bb_ref, cm_ref, ar_ref, ai_ref, pwr_ref, pwi_ref, par_ref, pai_ref, d_ref, ypre_ref, st_ref,
             carry, bu_scr, bbp_scr):
        @pl.when(pl.program_id(1) == 0)
        def _():
            carry[...] = jnp.zeros_like(carry)
            _pow_weights(bb_ref, pwr_ref, pwi_ref, bbp_scr, False)

        c = carry[...]
        st_ref[...] = c
        ar, ai = ar_ref[...], ai_ref[...]
        sr, si, cnext = _s5_states(u_ref, bb_ref, bbp_scr, par_ref, pai_ref, c[:, :SW], c[:, SW:], ncb, bu_scr)
        carry[...] = cnext
        for t in range(NT):
            bu = bu_scr[t]
            sr, si = _cmul_add(ar, ai, sr, si, bu[:, :SW], bu[:, SW:])
            rows = pl.ds(t, ncb, stride=NT)
            ypre_ref[rows, :] = _mm(jnp.concatenate([sr, si], axis=1), cm_ref[...]) + d_ref[...] * u_ref[rows, :]

    tab = pl.BlockSpec((CH, SW), lambda j, i: (0, j))
    stp = pl.BlockSpec((NT, SW), lambda j, i: (0, j))
    vec = lambda w: pl.BlockSpec((1, w), lambda j, i: (0, j))
    return pl.pallas_call(
        body, name="s5_scan_fwd", grid=(NJ, nb),
        in_specs=[pl.BlockSpec((tb, UW), lambda j, i: (i, j)),
                  pl.BlockSpec((None, UW, 2 * SW), lambda j, i: (j, 0, 0)),
                  pl.BlockSpec((None, 2 * SW, UW), lambda j, i: (j, 0, 0)),
                  vec(SW), vec(SW), stp, stp, tab, tab, vec(UW)],
        out_specs=[pl.BlockSpec((tb, UW), lambda j, i: (i, j)),
                   pl.BlockSpec((None, None, 1, 2 * SW), lambda j, i: (j, i, 0, 0))],
        out_shape=[SDS((L, D), F32), SDS((NJ, nb, 1, 2 * SW), F32)],
        scratch_shapes=[pltpu.VMEM((1, 2 * SW), F32), pltpu.VMEM((NT, ncb, 2 * SW), F32),
                        pltpu.VMEM((NT * UW, 2 * SW), MXU)],
        compiler_params=_cp(48, ARB2),
    )(p, bb, cm, abr, abi, pwr, pwi, par, pai, dskip)


def _s5_gate_fwd(ypre, p, wglu, bglu):
    L = ypre.shape[0]
    tm = min(256, L)

    def body(y_ref, az_ref, wg_ref, bg_ref, ya_ref):
        yg = _gelu(y_ref[...])
        t = _mm(yg, wg_ref[...]) + bg_ref[...]
        act, _ = _silu_and_grad(az_ref[...])
        ya_ref[...] = (yg * jax.nn.sigmoid(t) * act).astype(ya_ref.dtype)

    return pl.pallas_call(
        body, name="s5_gate_fwd", grid=(L // tm,),
        in_specs=[pl.BlockSpec((tm, D), lambda i: (i, 0)), pl.BlockSpec((tm, D), lambda i: (i, 1)),
                  _full((D, D)), _full((1, D))],
        out_specs=pl.BlockSpec((tm, D), lambda i: (i, 0)),
        out_shape=SDS((L, D), MXU),
        compiler_params=_cp(32, ("arbitrary",)),
    )(ypre, p, wglu, bglu)


def _s5_gate_bwd(ypre, p, dx1, wout_e, wglu, bglu):
    L = ypre.shape[0]
    tm = min(256, L)

    def body(y_ref, az_ref, dx1_ref, wo_ref, wg_ref, bg_ref, dyp_ref, daz_ref, yg_ref, dt_ref, ya_ref, gbg_ref):
        @pl.when(pl.program_id(0) == 0)
        def _():
            gbg_ref[...] = jnp.zeros_like(gbg_ref)

        yg, dgelu = _gelu_and_grad(y_ref[...])
        sg = jax.nn.sigmoid(_mm(yg, wg_ref[...]) + bg_ref[...])
        act, dact = _silu_and_grad(az_ref[...])
        y2 = yg * sg
        dya = _mm_nt(dx1_ref[...], wo_ref[...])
        daz_ref[...] = (dya * y2 * dact).astype(daz_ref.dtype)
        dy2 = dya * act
        dt = dy2 * yg * sg * (1.0 - sg)
        dyg = dy2 * sg + _mm_nt(dt, wg_ref[...])
        dyp_ref[...] = dyg * dgelu
        yg_ref[...] = yg.astype(yg_ref.dtype)
        dt_ref[...] = dt.astype(dt_ref.dtype)
        ya_ref[...] = (y2 * act).astype(ya_ref.dtype)
        gbg_ref[...] += jnp.sum(dt, axis=0, keepdims=True)

    row = pl.BlockSpec((tm, D), lambda i: (i, 0))
    return pl.pallas_call(
        body, name="s5_gate_bwd", grid=(L // tm,),
        in_specs=[row, pl.BlockSpec((tm, D), lambda i: (i, 1)), row,
                  pl.BlockSpec((D, D), lambda i: (0, 0)), _full((D, D)), _full((1, D))],
        out_specs=[row, row, row, row, row, _full((1, D))],
        out_shape=[SDS((L, D), F32), SDS((L, D), MXU), SDS((L, D), MXU), SDS((L, D), MXU), SDS((L, D), MXU),
                   SDS((1, D), F32)],
        compiler_params=_cp(40, ("arbitrary",)),
    )(ypre, p, dx1, wout_e, wglu, bglu)


def _s5_scan_bwd(p, dypre, states, bb, cm, cmt, abr, abi, pwr, pwi, par, pai, pbr, pbi, dskip):
    L = p.shape[0]
    ncb, tb = _chunks(L)
    nb = L // tb
    rev = lambda i: nb - 1 - i

    def body(u_ref, dy_ref, st_ref, bb_ref, cm_ref, cmt_ref, ar_ref, ai_ref, pwr_ref, pwi_ref, par_ref, pai_ref,
             pbr_ref, pbi_ref, d_ref, du_ref, gd_ref, gcm_ref, gbb_ref, gar_ref, gai_ref,
             lcarry, bu_scr, s_scr, gs_scr, bbp_scr, cmp_scr):
        del cm_ref

        @pl.when(pl.program_id(1) == 0)
        def _():
            _pow_weights(bb_ref, pwr_ref, pwi_ref, bbp_scr, False)
            _pow_weights(cmt_ref, pwr_ref, pwi_ref, cmp_scr, True)
            lcarry[...] = jnp.zeros_like(lcarry)
            gd_ref[...] = jnp.zeros_like(gd_ref)
            gcm_ref[...] = jnp.zeros_like(gcm_ref)
            gbb_ref[...] = jnp.zeros_like(gbb_ref)
            gar_ref[...] = jnp.zeros_like(gar_ref)
            gai_ref[...] = jnp.zeros_like(gai_ref)

        ar, ai = ar_ref[...], ai_ref[...]
        c = st_ref[...]
        sr, si, _ = _s5_states(u_ref, bb_ref, bbp_scr, par_ref, pai_ref, c[:, :SW], c[:, SW:], ncb, bu_scr)
        s_scr[0] = jnp.concatenate([sr, si], axis=1)
        for t in range(NT):
            bu = bu_scr[t]
            sr, si = _cmul_add(ar, ai, sr, si, bu[:, :SW], bu[:, SW:])
            s_scr[t + 1] = jnp.concatenate([sr, si], axis=1)
        dys = [dy_ref[pl.ds(t, ncb, stride=NT), :] for t in range(NT)]
        for t in range(NT):
            gs_scr[t] = _mm(dys[t], cmt_ref[...])
        f = _mm(jnp.concatenate(dys, axis=1), cmp_scr[...])
        xr, xi = _cscan(f[:, :SW], f[:, SW:], par_ref, pai_ref, True)
        lc = lcarry[...]
        lcr, lci = lc[:, :SW], lc[:, SW:]
        hr, hi = _cmul_add(pbr_ref[CH - ncb:CH, :], -pbi_ref[CH - ncb:CH, :], lcr, lci, xr, xi)
        lcarry[...] = jnp.concatenate([hr[0:1, :], hi[0:1, :]], axis=1)
        row = lax.broadcasted_iota(jnp.int32, hr.shape, 0)
        lr_ = jnp.where(row < ncb - 1, pltpu.roll(hr, ncb - 1, 0), lcr)
        li_ = jnp.where(row < ncb - 1, pltpu.roll(hi, ncb - 1, 0), lci)
        gar = jnp.zeros((1, SW), F32)
        gai = jnp.zeros((1, SW), F32)
        for t in reversed(range(NT)):
            gs = gs_scr[t]
            lr_, li_ = _cmul_add(ar, -ai, lr_, li_, gs[:, :SW], gs[:, SW:])
            rows = pl.ds(t, ncb, stride=NT)
            u_t, dy_t = u_ref[rows, :], dy_ref[rows, :]
            lam = jnp.concatenate([lr_, li_], axis=1)
            gbb_ref[...] += _mm_tn(u_t, lam)
            du_ref[rows, :] = _mm_nt(lam, bb_ref[...]) + dy_t * d_ref[...]
            gd_ref[...] += jnp.sum(dy_t * u_t, axis=0, keepdims=True)
            gcm_ref[...] += _mm_tn(s_scr[t + 1], dy_t)
            sp = s_scr[t]
            spr, spi = sp[:, :SW], sp[:, SW:]
            gar += jnp.sum(lr_ * spr + li_ * spi, axis=0, keepdims=True)
            gai += jnp.sum(li_ * spr - lr_ * spi, axis=0, keepdims=True)
        gar_ref[...] += gar
        gai_ref[...] += gai

    tab = pl.BlockSpec((CH, SW), lambda j, i: (0, j))
    stp = pl.BlockSpec((NT, SW), lambda j, i: (0, j))
    colblk = pl.BlockSpec((tb, UW), lambda j, i: (rev(i), j))
    vec = lambda w: pl.BlockSpec((1, w), lambda j, i: (0, j))
    return pl.pallas_call(
        body, name="s5_scan_bwd", grid=(NJ, nb),
        in_specs=[colblk, colblk,
                  pl.BlockSpec((None, None, 1, 2 * SW), lambda j, i: (j, rev(i), 0, 0)),
                  pl.BlockSpec((None, UW, 2 * SW), lambda j, i: (j, 0, 0)),
                  pl.BlockSpec((None, 2 * SW, UW), lambda j, i: (j, 0, 0)),
                  pl.BlockSpec((None, UW, 2 * SW), lambda j, i: (j, 0, 0)),
                  vec(SW), vec(SW), stp, stp, tab, tab, tab, tab, vec(UW)],
        out_specs=[colblk, vec(UW),
                   pl.BlockSpec((None, 2 * SW, UW), lambda j, i: (j, 0, 0)),
                   pl.BlockSpec((None, UW, 2 * SW), lambda j, i: (j, 0, 0)),
                   vec(SW), vec(SW)],
        out_shape=[SDS((L, D), F32), SDS((1, D), F32),
                   SDS((NJ, 2 * SW, UW), F32), SDS((NJ, UW, 2 * SW), F32),
                   SDS((1, NSTATE), F32), SDS((1, NSTATE), F32)],
        scratch_shapes=[pltpu.VMEM((1, 2 * SW), F32), pltpu.VMEM((NT, ncb, 2 * SW), F32),
                        pltpu.VMEM((NT + 1, ncb, 2 * SW), F32), pltpu.VMEM((NT, ncb, 2 * SW), F32),
                        pltpu.VMEM((NT * UW, 2 * SW), MXU), pltpu.VMEM((NT * UW, 2 * SW), MXU)],
        compiler_params=_cp(56, ARB2),
    )(p, dypre, states, bb, cm, cmt, abr, abi, pwr, pwi, par, pai, pbr, pbi, dskip)


def _rope_tables(L, inv):
    tm = min(512, L)

    def body(inv_ref, cos_ref, sin_ref):
        pos = (lax.broadcasted_iota(jnp.int32, (tm, DK // 2), 0) + pl.program_id(0) * tm).astype(F32)
        ang = pos * inv_ref[...]
        cos_ref[...] = jnp.cos(ang)
        sin_ref[...] = jnp.sin(ang)

    blk = pl.BlockSpec((tm, DK // 2), lambda i: (i, 0))
    return pl.pallas_call(body, name="rope_tables", grid=(L // tm,), in_specs=[_full((1, DK // 2))],
                          out_specs=[blk, blk], out_shape=[SDS((L, DK // 2), F32)] * 2)(inv)


def _rot(x, cos, sin):
    x1, x2 = x[:, :DK // 2], x[:, DK // 2:]
    return jnp.concatenate([x1 * cos - x2 * sin, x1 * sin + x2 * cos], axis=1)


def _unrot(d, cos, sin):
    d1, d2 = d[:, :DK // 2], d[:, DK // 2:]
    return jnp.concatenate([d1 * cos + d2 * sin, d2 * cos - d1 * sin], axis=1)


def _ret_decays(h):
    lg = LOG_G[h]
    n = lax.broadcasted_iota(jnp.int32, (CH, CH), 0)
    m = lax.broadcasted_iota(jnp.int32, (CH, CH), 1)
    diff = (n - m).astype(F32)
    decay = jnp.where(n >= m, jnp.exp(lg * jnp.maximum(diff, 0.0)), 0.0)
    idx = lax.broadcasted_iota(jnp.int32, (CH, 1), 0).astype(F32)
    xi = jnp.exp(lg * (idx + 1.0))
    zeta = jnp.exp(lg * (CH - 1.0 - idx))
    return decay, xi, zeta, math.exp(lg * CH)


def _ret_tables(dec_scr, vec_scr):
    for h in range(HEADS):
        decay, xi, zeta, _ = _ret_decays(h)
        dec_scr[h] = decay
        vec_scr[h] = jnp.concatenate([jnp.broadcast_to(xi, (CH, 128)), jnp.broadcast_to(zeta, (CH, 128))], axis=1)


def _ret_chunk_fwd(q, k, v, cos, sin, s_prev_b, decay, xi, zeta):
    qr = _rot(q, cos, sin)
    kr = _rot(k, cos, sin) * (DK ** -0.5)
    scores = _mm_nt(qr, kr) * decay
    o = _mm(scores, v) + _mm(qr * xi, s_prev_b)
    local = _mm_tn(kr * zeta, v)
    mu = jnp.mean(o, axis=-1, keepdims=True)
    oc = o - mu
    rstd = lax.rsqrt(jnp.mean(oc * oc, axis=-1, keepdims=True) + EPS)
    return qr, kr, scores, local, oc * rstd, rstd


def _ret_fwd(p, cos, sin, gain):
    L = p.shape[0]
    nb = L // CH

    def body(q_ref, k_ref, v_ref, bz_ref, cos_ref, sin_ref, g_ref, yb_ref, st_ref, state, dec_scr, vec_scr):
        @pl.when(pl.program_id(0) == 0)
        def _():
            state[...] = jnp.zeros_like(state)
            _ret_tables(dec_scr, vec_scr)

        cos, sin = cos_ref[...], sin_ref[...]
        act, _ = _silu_and_grad(bz_ref[...])
        for h in range(HEADS):
            hs = slice(h * DK, (h + 1) * DK)
            xi, zeta = vec_scr[h, :, 0:1], vec_scr[h, :, 128:129]
            s_prev = state[h]
            s_prev_b = s_prev.astype(MXU)
            st_ref[h] = s_prev_b
            _, _, _, local, on, _ = _ret_chunk_fwd(q_ref[:, hs], k_ref[:, hs], v_ref[:, hs], cos, sin,
                                                   s_prev_b, dec_scr[h], xi, zeta)
            state[h] = s_prev * math.exp(LOG_G[h] * CH) + local
            yb_ref[:, hs] = (on * g_ref[:, hs] * act[:, hs]).astype(yb_ref.dtype)

    col = lambda c: pl.BlockSpec((CH, D), lambda i: (i, c))
    rope = pl.BlockSpec((CH, DK // 2), lambda i: (i, 0))
    return pl.pallas_call(
        body, name="ret_fwd", grid=(nb,),
        in_specs=[col(2), col(3), col(4), col(5), rope, rope, _full((1, D))],
        out_specs=[pl.BlockSpec((CH, D), lambda i: (i, 0)),
                   pl.BlockSpec((None, HEADS, DK, DK), lambda i: (i, 0, 0, 0))],
        out_shape=[SDS((L, D), MXU), SDS((nb, HEADS, DK, DK), MXU)],
        scratch_shapes=[pltpu.VMEM((HEADS, DK, DK), F32), pltpu.VMEM((HEADS, CH, CH), F32),
                        pltpu.VMEM((HEADS, CH, 256), F32)],
        compiler_params=_cp(40, ("arbitrary",)),
    )(p, p, p, p, cos, sin, gain)


def _ret_bwd(p, cos, sin, gain, states, dx1, wout_e, du, daz):
    L = p.shape[0]
    nb = L // CH
    rev = lambda i: nb - 1 - i

    def body(q_ref, k_ref, v_ref, bz_ref, cos_ref, sin_ref, g_ref, st_ref, dx1_ref, wo_ref, du_ref, daz_ref,
             dp_ref, yb_ref, gg_ref, gstate, dec_scr, vec_scr):
        @pl.when(pl.program_id(0) == 0)
        def _():
            gstate[...] = jnp.zeros_like(gstate)
            gg_ref[...] = jnp.zeros_like(gg_ref)
            _ret_tables(dec_scr, vec_scr)

        cos, sin = cos_ref[...], sin_ref[...]
        act, dact = _silu_and_grad(bz_ref[...])
        dyb = _mm_nt(dx1_ref[...], wo_ref[...])
        dp_ref[:, 0:D] = du_ref[...].astype(dp_ref.dtype)
        dp_ref[:, D:2 * D] = daz_ref[...]
        for h in range(HEADS):
            hs = slice(h * DK, (h + 1) * DK)
            col = lambda part: slice((2 + part) * D + h * DK, (2 + part) * D + (h + 1) * DK)
            decay = dec_scr[h]
            xi, zeta = vec_scr[h, :, 0:1], vec_scr[h, :, 128:129]
            v = v_ref[:, hs]
            s_prev_b = st_ref[h]
            qr, kr, scores, _, on, rstd = _ret_chunk_fwd(q_ref[:, hs], k_ref[:, hs], v, cos, sin, s_prev_b,
                                                         decay, xi, zeta)
            gain_h = g_ref[:, hs]
            out = on * gain_h
            yb_ref[:, hs] = (out * act[:, hs]).astype(yb_ref.dtype)
            dyb_h = dyb[:, hs]
            dp_ref[:, col(3)] = (dyb_h * out * dact[:, hs]).astype(dp_ref.dtype)
            dout = dyb_h * act[:, hs]
            gg_ref[:, hs] += jnp.sum(dout * on, axis=0, keepdims=True)
            don = dout * gain_h
            do = rstd * (don - jnp.mean(don, axis=-1, keepdims=True)
                         - on * jnp.mean(don * on, axis=-1, keepdims=True))
            gnext = gstate[h]
            gnext_b = gnext.astype(MXU)
            dscores = _mm_nt(do, v) * decay
            dp_ref[:, col(2)] = (_mm_tn(scores, do) + _mm(kr * zeta, gnext_b)).astype(dp_ref.dtype)
            dqr = _mm(dscores, kr) + _mm_nt(do, s_prev_b) * xi
            dkr = _mm_tn(dscores, qr) + _mm_nt(v, gnext_b) * zeta
            gstate[h] = gnext * math.exp(LOG_G[h] * CH) + _mm_tn(qr * xi, do)
            dp_ref[:, col(0)] = _unrot(dqr, cos, sin).astype(dp_ref.dtype)
            dp_ref[:, col(1)] = (_unrot(dkr, cos, sin) * (DK ** -0.5)).astype(dp_ref.dtype)

    col = lambda c: pl.BlockSpec((CH, D), lambda i: (rev(i), c))
    rope = pl.BlockSpec((CH, DK // 2), lambda i: (rev(i), 0))
    outc = col(0)
    act_out = SDS((L, D), MXU)
    return pl.pallas_call(
        body, name="ret_bwd", grid=(nb,),
        in_specs=[col(2), col(3), col(4), col(5), rope, rope, _full((1, D)),
                  pl.BlockSpec((None, HEADS, DK, DK), lambda i: (rev(i), 0, 0, 0)),
                  outc, pl.BlockSpec((D, D), lambda i: (1, 0)), outc, outc],
        out_specs=[pl.BlockSpec((CH, NIN), lambda i: (rev(i), 0)), outc, _full((1, D))],
        out_shape=[SDS((L, NIN), MXU), act_out, SDS((1, D), F32)],
        scratch_shapes=[pltpu.VMEM((HEADS, DK, DK), F32), pltpu.VMEM((HEADS, CH, CH), F32),
                        pltpu.VMEM((HEADS, CH, 256), F32)],
        compiler_params=_cp(48, ("arbitrary",)),
    )(p, p, p, p, cos, sin, gain, states, dx1, wout_e, du, daz)


def _out_even(x, ya, yb, wout):
    L = x.shape[0]
    tm = min(512, L)

    def body(x_ref, ya_ref, yb_ref, w_ref, o_ref):
        cat = jnp.concatenate([ya_ref[...], yb_ref[...]], axis=1)
        o_ref[...] = x_ref[...] + jnp.dot(cat, w_ref[...], preferred_element_type=F32)

    row = pl.BlockSpec((tm, D), lambda i: (i, 0))
    return pl.pallas_call(
        body, name="out_even", grid=(L // tm,), in_specs=[row, row, row, _full((DI, D))],
        out_specs=row, out_shape=SDS((L, D), F32), compiler_params=_cp(32, ("arbitrary",)),
    )(x, ya, yb, wout)


def _sgu_core(pv, gain, ws_ref, bs_ref):
    pu, pvv, z = pv[:, :DI], pv[:, DI:2 * DI], pv[:, 2 * DI:]
    u, gu = _gelu_and_grad(pu)
    v, gv = _gelu_and_grad(pvv)
    mu = jnp.mean(v, axis=-1, keepdims=True)
    vc = v - mu
    rstd = lax.rsqrt(jnp.mean(vc * vc, axis=-1, keepdims=True) + EPS)
    vhat = vc * rstd
    vn = vhat * gain
    t = lax.broadcasted_iota(jnp.int32, (CH, CH), 0)
    s_ = lax.broadcasted_iota(jnp.int32, (CH, CH), 1)
    mask = t >= s_
    wm = [jnp.where(mask, ws_ref[g], 0.0).astype(MXU) for g in range(SG)]
    s = jnp.concatenate([_mm(wm[g], vn[:, g * SGD:(g + 1) * SGD]) + bs_ref[g] for g in range(SG)], axis=1)
    return gu, gv, z, u, vhat, rstd, vn, mask, wm, s


def _sgu_fwd(p2, x1, gain, wsp, bsp, wout, fnorm, tgt):
    L = p2.shape[0]

    def body(p_ref, x1_ref, g_ref, ws_ref, bs_ref, wo_ref, fn_ref, t_ref, dx2_ref, gfn_ref, loss_ref):
        @pl.when(pl.program_id(0) == 0)
        def _():
            gfn_ref[...] = jnp.zeros_like(gfn_ref)
            loss_ref[...] = jnp.zeros_like(loss_ref)

        _, _, z, u, _, _, _, _, _, s = _sgu_core(p_ref[...], g_ref[...], ws_ref, bs_ref)
        act, _ = _silu_and_grad(z)
        x2 = x1_ref[...] + _mm(u * s * act, wo_ref[...])
        xhat, r = _rms(x2)
        fn = fn_ref[...]
        e = xhat * fn - t_ref[...]
        loss_ref[...] += 0.5 * jnp.sum(jnp.mean(e * e, axis=-1, keepdims=True), axis=0, keepdims=True)
        do = e * (1.0 / D)
        gfn_ref[...] += jnp.sum(do * xhat, axis=0, keepdims=True)
        dxhat = do * fn
        dx2_ref[...] = r * (dxhat - xhat * jnp.mean(dxhat * xhat, axis=-1, keepdims=True))

    row = pl.BlockSpec((CH, D), lambda i: (i, 0))
    return pl.pallas_call(
        body, name="sgu_fwd", grid=(L // CH,),
        in_specs=[pl.BlockSpec((CH, NIN), lambda i: (i, 0)), row, _full((1, DI)), _full((SG, CH, CH)),
                  _full((SG, CH, 1)), _full((DI, D)), _full((1, D)), row],
        out_specs=[row, _full((1, D)), _full((1, 1))],
        out_shape=[SDS((L, D), F32), SDS((1, D), F32), SDS((1, 1), F32)],
        compiler_params=_cp(48, ("arbitrary",)),
    )(p2, x1, gain, wsp, bsp, wout, fnorm, tgt)


def _sgu_bwd(p2, dx2, gain, wsp, bsp, wout):
    L = p2.shape[0]

    def body(p_ref, dx2_ref, g_ref, ws_ref, bs_ref, wo_ref, dp_ref, y_ref, gg_ref, gws_ref, gbs_ref):
        @pl.when(pl.program_id(0) == 0)
        def _():
            gg_ref[...] = jnp.zeros_like(gg_ref)
            gws_ref[...] = jnp.zeros_like(gws_ref)
            gbs_ref[...] = jnp.zeros_like(gbs_ref)

        gain = g_ref[...]
        gu, gv, z, u, vhat, rstd, vn, mask, wm, s = _sgu_core(p_ref[...], gain, ws_ref, bs_ref)
        act, dact = _silu_and_grad(z)
        y_ref[...] = (u * s * act).astype(y_ref.dtype)
        dy = _mm_nt(dx2_ref[...], wo_ref[...])
        du = dy * s * act
        ds = dy * u * act
        dz = dy * u * s * dact
        dvn = []
        for g in range(SG):
            ds_g = ds[:, g * SGD:(g + 1) * SGD]
            vn_g = vn[:, g * SGD:(g + 1) * SGD]
            gbs_ref[g] += jnp.sum(ds_g, axis=1, keepdims=True)
            gws_ref[g] += jnp.where(mask, _mm_nt(ds_g, vn_g), 0.0)
            dvn.append(_mm_tn(wm[g], ds_g))
        dvn = jnp.concatenate(dvn, axis=1)
        gg_ref[...] += jnp.sum(dvn * vhat, axis=0, keepdims=True)
        dvhat = dvn * gain
        dv = rstd * (dvhat - jnp.mean(dvhat, axis=-1, keepdims=True)
                     - vhat * jnp.mean(dvhat * vhat, axis=-1, keepdims=True))
        dp_ref[...] = jnp.concatenate([du * gu, dv * gv, dz], axis=1).astype(dp_ref.dtype)

    return pl.pallas_call(
        body, name="sgu_bwd", grid=(L // CH,),
        in_specs=[pl.BlockSpec((CH, NIN), lambda i: (i, 0)), pl.BlockSpec((CH, D), lambda i: (i, 0)),
                  _full((1, DI)), _full((SG, CH, CH)), _full((SG, CH, 1)), _full((DI, D))],
        out_specs=[pl.BlockSpec((CH, NIN), lambda i: (i, 0)), pl.BlockSpec((CH, DI), lambda i: (i, 0)),
                   _full((1, DI)), _full((SG, CH, CH)), _full((SG, CH, 1))],
        out_shape=[SDS((L, NIN), MXU), SDS((L, DI), MXU), SDS((1, DI), F32), SDS((SG, CH, CH), F32),
                   SDS((SG, CH, 1), F32)],
        compiler_params=_cp(48, ("arbitrary",)),
    )(p2, dx2, gain, wsp, bsp, wout)


def _my_index():
    return 4 * lax.axis_index("x") + 2 * lax.axis_index("y") + lax.axis_index("c")


def _ordered_sum(land_ref, own, me):
    g = None
    for s in range(NDEV):
        part = jnp.where(me == s, own, land_ref[s].astype(F32))
        g = part if g is None else g + part
    return g


def _adamw_math(w, m, v, g):
    mn = ADAM_B1 * m + (1.0 - ADAM_B1) * g
    vn = ADAM_B2 * v + (1.0 - ADAM_B2) * (g * g)
    mhat = mn / BC1
    vhat = vn / BC2
    return g, -ADAM_LR * (mhat / (jnp.sqrt(vhat) + ADAM_EPS) + ADAM_WD * w), mn, vn


def _adamw(w, m, v, land, own, name):
    R, C = w.shape
    tr = R
    for cand in (256, 128, 64, 32, 16, 8):
        if R % cand == 0 and R > cand:
            tr = cand
            break

    def body(w_ref, m_ref, v_ref, land_ref, own_ref, g_ref, d_ref, mo_ref, vo_ref):
        g = _ordered_sum(land_ref, own_ref[...].astype(F32), _my_index())
        for o, val in zip((g_ref, d_ref, mo_ref, vo_ref), _adamw_math(w_ref[...], m_ref[...], v_ref[...], g)):
            o[...] = val

    blk = pl.BlockSpec((tr, C), lambda i: (i, 0))
    out = SDS((R, C), F32)
    return pl.pallas_call(
        body, name=name, grid=(R // tr,),
        in_specs=[blk, blk, blk, pl.BlockSpec((NDEV, tr, C), lambda i: (0, i, 0)), blk],
        out_specs=[blk, blk, blk, blk], out_shape=[out, out, out, out],
        compiler_params=_cp(40, ("arbitrary",)),
    )(w, m, v, land, own)


def _adamw_many(ws, ms, vs, lands, owns, name):
    k = len(ws)

    def body(*refs):
        ins, outs = refs[:5 * k], refs[5 * k:]
        me = _my_index()
        for i in range(k):
            w_ref, m_ref, v_ref, land_ref, own_ref = (ins[j * k + i] for j in range(5))
            g = _ordered_sum(land_ref, own_ref[...], me)
            for j, val in enumerate(_adamw_math(w_ref[...], m_ref[...], v_ref[...], g)):
                outs[j * k + i][...] = val

    out_shape = [SDS(w.shape, F32) for _ in range(4) for w in ws]
    res = pl.pallas_call(body, name=name, out_shape=out_shape, compiler_params=_cp(60))(*ws, *ms, *vs, *lands, *owns)
    return [res[j * k:(j + 1) * k] for j in range(4)]


MESH = pl.DeviceIdType.MESH
HBM_SPEC = pl.BlockSpec(memory_space=pltpu.HBM)
SEM_SPEC = pl.BlockSpec(memory_space=pltpu.SEMAPHORE)
EFFECT = pltpu.SideEffectType.DATAFLOW_SIDE_EFFECTING


def _me_and_peers():
    x, y, c = lax.axis_index("x"), lax.axis_index("y"), lax.axis_index("c")
    me = 4 * x + 2 * y + c
    peers = []
    for r in range(1, NDEV):
        px, py, pc = x ^ ((r >> 2) & 1), y ^ ((r >> 1) & 1), c ^ (r & 1)
        peers.append(((px, py, pc), 4 * px + 2 * py + pc))
    return me, peers


def _land_shape(a, scatter):
    return (NDEV,) + (a.shape[1:] if scatter else a.shape)


def _remote(src, dst, send_sems, recv_sems, r, k, n, dev):
    i = r * n + k
    return pltpu.make_async_remote_copy(src_ref=src, dst_ref=dst, send_sem=send_sems.at[i], recv_sem=recv_sems.at[i],
                                        device_id=dev, device_id_type=MESH)


def _exchange(arrays, scatter, name):
    n = len(arrays)
    out_shape = [SDS(_land_shape(a, scatter), a.dtype) for a in arrays]

    def body(*refs):
        ins, outs = refs[:n], refs[n:2 * n]
        send_sems, recv_sems, loc_sems = refs[2 * n:]
        me, peers = _me_and_peers()
        local = []
        for k in range(n):
            src = ins[k].at[me] if scatter else ins[k]
            cp = pltpu.make_async_copy(src, outs[k].at[me], loc_sems.at[k])
            cp.start()
            local.append(cp)
        sends = []
        for r, (dev, lin) in enumerate(peers):
            for k in range(n):
                src = ins[k].at[lin] if scatter else ins[k]
                cp = _remote(src, outs[k].at[me], send_sems, recv_sems, r, k, n, dev)
                cp.start()
                sends.append(cp)
        for r, (dev, lin) in enumerate(peers):
            for k in range(n):
                src = ins[k].at[me] if scatter else ins[k]
                _remote(src, outs[k].at[lin], send_sems, recv_sems, r, k, n, dev).wait_recv()
        for cp in sends:
            cp.wait_send()
        for cp in local:
            cp.wait()

    return pl.pallas_call(
        body, name=name, in_specs=[HBM_SPEC] * n, out_specs=[HBM_SPEC] * n, out_shape=out_shape,
        scratch_shapes=[pltpu.SemaphoreType.DMA(((NDEV - 1) * n,)), pltpu.SemaphoreType.DMA(((NDEV - 1) * n,)),
                        pltpu.SemaphoreType.DMA((n,))],
    )(*arrays)


def _exchange_start(arrays, scatter, name):
    n = len(arrays)
    lands = [lax.empty(_land_shape(a, scatter), a.dtype) for a in arrays]

    def body(*refs):
        ins, lnd = refs[:n], refs[n:2 * n]
        send_sems, recv_sems, own_sems = refs[2 * n:2 * n + 3]
        token = refs[-1]
        me, peers = _me_and_peers()
        for r, (dev, lin) in enumerate(peers):
            for k in range(n):
                src = ins[k].at[lin] if scatter else ins[k]
                _remote(src, lnd[k].at[me], send_sems, recv_sems, r, k, n, dev).start()
        if not scatter:
            for k in range(n):
                pltpu.make_async_copy(ins[k], lnd[k].at[me], own_sems.at[k]).start()
        token[...] = jnp.zeros_like(token)

    sem = pltpu.SemaphoreType.DMA(((NDEV - 1) * n,))
    outs = pl.pallas_call(
        body, name=name,
        out_shape=(sem, sem, pltpu.SemaphoreType.DMA((n,)), *[pltpu.HBM(a.shape, a.dtype) for a in arrays],
                   *[pltpu.HBM(l.shape, l.dtype) for l in lands], SDS((8, 128), F32)),
        in_specs=[HBM_SPEC] * (2 * n),
        out_specs=(SEM_SPEC, SEM_SPEC, SEM_SPEC, *[HBM_SPEC] * (2 * n), pl.BlockSpec(memory_space=pltpu.VMEM)),
        input_output_aliases={k: 3 + k for k in range(2 * n)},
        compiler_params=pltpu.CompilerParams(has_side_effects=EFFECT),
    )(*[pltpu.with_memory_space_constraint(a, pltpu.HBM) for a in arrays],
      *[pltpu.with_memory_space_constraint(l, pltpu.HBM) for l in lands])
    return (n, scatter, outs[0], outs[1], outs[2], outs[3:3 + n], outs[3 + n:3 + 2 * n]), outs[-1]


def _exchange_wait(handle, after, name):
    n, scatter, send_sems, recv_sems, own_sems, thru, lands = handle
    after = tuple(after)

    def body(*refs):
        ins, lnd = refs[:n], refs[n:2 * n]
        send_sems, recv_sems, own_sems = refs[2 * n:2 * n + 3]
        me, peers = _me_and_peers()
        for r, (dev, lin) in enumerate(peers):
            for k in range(n):
                src = ins[k].at[lin] if scatter else ins[k]
                cp = _remote(src, lnd[k].at[lin], send_sems, recv_sems, r, k, n, dev)
                cp.wait_send()
                cp.wait_recv()
        if not scatter:
            for k in range(n):
                pltpu.make_async_copy(ins[k], lnd[k].at[me], own_sems.at[k]).wait()

    outs = pl.pallas_call(
        body, name=name,
        out_shape=(*[pltpu.HBM(a.shape, a.dtype) for a in thru], *[pltpu.HBM(l.shape, l.dtype) for l in lands]),
        in_specs=[HBM_SPEC] * (2 * n) + [SEM_SPEC, SEM_SPEC, SEM_SPEC] + [ANY_SPEC] * len(after),
        out_specs=tuple([HBM_SPEC] * (2 * n)),
        input_output_aliases={k: k for k in range(2 * n)},
        compiler_params=pltpu.CompilerParams(has_side_effects=EFFECT),
    )(*thru, *lands, send_sems, recv_sems, own_sems, *after)
    return list(outs[:n]), list(outs[n:])


CHIP_RELATIONS = (2, 4, 6)


def _peer(r):
    x, y, c = lax.axis_index("x"), lax.axis_index("y"), lax.axis_index("c")
    px, py, pc = x ^ ((r >> 2) & 1), y ^ ((r >> 1) & 1), c ^ (r & 1)
    return (px, py, pc), 4 * px + 2 * py + pc


def _copy(src, dst, send_sems, recv_sems, i, dev):
    return pltpu.make_async_remote_copy(src_ref=src, dst_ref=dst, send_sem=send_sems.at[i], recv_sem=recv_sems.at[i],
                                        device_id=dev, device_id_type=MESH)


def _gather2_start(a, name):
    land = lax.empty((NDEV,) + a.shape, a.dtype)

    def body(own, lnd, send_sems, recv_sems, own_sem, own_thru, lnd_thru, token):
        me = _my_index()
        for i, r in enumerate((1,) + CHIP_RELATIONS):
            dev, _ = _peer(r)
            _copy(own, lnd.at[me], send_sems, recv_sems, i, dev).start()
        pltpu.make_async_copy(own, lnd.at[me], own_sem.at[0]).start()
        token[...] = jnp.zeros_like(token)

    sem4 = pltpu.SemaphoreType.DMA((4,))
    outs = pl.pallas_call(
        body, name=name,
        out_shape=(sem4, sem4, pltpu.SemaphoreType.DMA((1,)), pltpu.HBM(a.shape, a.dtype),
                   pltpu.HBM(land.shape, land.dtype), SDS((8, 128), F32)),
        in_specs=[HBM_SPEC, HBM_SPEC],
        out_specs=(SEM_SPEC, SEM_SPEC, SEM_SPEC, HBM_SPEC, HBM_SPEC, pl.BlockSpec(memory_space=pltpu.VMEM)),
        input_output_aliases={0: 3, 1: 4},
        compiler_params=pltpu.CompilerParams(has_side_effects=EFFECT),
    )(pltpu.with_memory_space_constraint(a, pltpu.HBM), pltpu.with_memory_space_constraint(land, pltpu.HBM))
    return outs[:5], outs[5]


def _gather2_forward(handle, after, name):
    send_sems, recv_sems, own_sem, own, land = handle
    after = tuple(after)

    def body(lnd, recv_sems, *rest):
        send2, recv2, lnd_thru = rest[len(after):]
        sib, _ = _peer(1)
        for k, r in enumerate(CHIP_RELATIONS):
            dev, lin = _peer(r)
            _copy(lnd.at[lin], lnd.at[lin], recv_sems, recv_sems, 1 + k, dev).wait_recv()
            _copy(lnd.at[lin], lnd.at[lin], send2, recv2, k, sib).start()

    sem3 = pltpu.SemaphoreType.DMA((3,))
    send2, recv2, land = pl.pallas_call(
        body, name=name,
        out_shape=(sem3, sem3, pltpu.HBM(land.shape, land.dtype)),
        in_specs=[HBM_SPEC, SEM_SPEC] + [ANY_SPEC] * len(after),
        out_specs=(SEM_SPEC, SEM_SPEC, HBM_SPEC),
        input_output_aliases={0: 2},
        compiler_params=pltpu.CompilerParams(has_side_effects=EFFECT),
    )(land, recv_sems, *after)
    return send_sems, recv_sems, own_sem, send2, recv2, own, land


def _gather2_wait(handle, name):
    send_sems, recv_sems, own_sem, send2, recv2, own, land = handle

    def body(own_ref, lnd, send_sems, recv_sems, own_sem, send2, recv2, own_thru, lnd_thru):
        me = _my_index()
        sib, sib_lin = _peer(1)
        for i, r in enumerate((1,) + CHIP_RELATIONS):
            dev, _ = _peer(r)
            _copy(own_ref, lnd.at[me], send_sems, recv_sems, i, dev).wait_send()
        _copy(own_ref, lnd.at[sib_lin], send_sems, recv_sems, 0, sib).wait_recv()
        for k, r in enumerate(CHIP_RELATIONS):
            _, lin = _peer(r)
            _, lin_other = _peer(r ^ 1)
            _copy(lnd.at[lin], lnd.at[lin], send2, recv2, k, sib).wait_send()
            _copy(lnd.at[lin_other], lnd.at[lin_other], send2, recv2, k, sib).wait_recv()
        pltpu.make_async_copy(own_ref, lnd.at[me], own_sem.at[0]).wait()

    outs = pl.pallas_call(
        body, name=name,
        out_shape=(pltpu.HBM(own.shape, own.dtype), pltpu.HBM(land.shape, land.dtype)),
        in_specs=[HBM_SPEC, HBM_SPEC] + [SEM_SPEC] * 5,
        out_specs=(HBM_SPEC, HBM_SPEC),
        input_output_aliases={0: 0, 1: 1},
        compiler_params=pltpu.CompilerParams(has_side_effects=EFFECT),
    )(own, land, send_sems, recv_sems, own_sem, send2, recv2)
    return outs[1]


def _local_step(x, tgt, norm_even, first_weight, lam_re, lam_im, log_dt, b_re, b_im, c_re, c_im, s5_d, bglu,
                ret_gain, wsp, bsp, fnorm, late_weights, emit, start_token=None):
    L = x.shape[0]
    lr3, li3 = lam_re.reshape(G, 1, P), lam_im.reshape(G, 1, P)
    dt3 = log_dt.reshape(G, 1, 1)
    br3, bi3 = jnp.swapaxes(b_re, 1, 2), jnp.swapaxes(b_im, 1, 2)
    abr3, abi3, bbr3, bbi3 = _s5_disc(lr3, li3, dt3, br3, bi3)
    bb = jnp.concatenate([_embed(bbr3), _embed(bbi3)], axis=2).astype(MXU)
    cm = jnp.concatenate([_embed(jnp.swapaxes(c_re, 1, 2)), -_embed(jnp.swapaxes(c_im, 1, 2))], axis=1).astype(MXU)
    abr, abi = abr3.reshape(1, NSTATE), abi3.reshape(1, NSTATE)
    pwr, pwi, _, _ = _s5_tables(abr, abi, NT, "s5_tables_step")
    par, pai, pbr, pbi = _s5_tables(pwr[NT - 1:NT], pwi[NT - 1:NT], CH, "s5_tables_chunk")
    inv = (ROPE_BASE ** (-jnp.arange(DK // 2, dtype=F32) / (DK // 2))).reshape(1, DK // 2)
    cos, sin = _rope_tables(L, inv)
    bsp3 = bsp.reshape(SG, CH, 1)

    def dep(token):
        return NO_DEPS if token is None else (token,)

    win_e = first_weight((cos, pbi, cm))
    p = _in_proj(x, norm_even, win_e, "in_even", dep(start_token))
    ypre, s5_states = _s5_scan_fwd(p, bb, cm, abr, abi, pwr, pwi, par, pai, s5_d)
    yb, ret_states = _ret_fwd(p, cos, sin, ret_gain)
    wglu, wout_e, norm_odd, win_o, sgu_gain, wout_o = late_weights((ypre, yb))
    ya = _s5_gate_fwd(ypre, p, wglu, bglu)
    x1 = _out_even(x, ya, yb, wout_e)
    p2 = _in_proj(x1, norm_odd, win_o, "in_odd")
    dx2, g_fnorm, loss = _sgu_fwd(p2, x1, sgu_gain, wsp, bsp3, wout_o, fnorm, tgt)

    dp2, y_o, g_sgu_gain, g_wsp, g_bsp = _sgu_bwd(p2, dx2, sgu_gain, wsp, bsp3, wout_o)
    g_wout_o = _wgrad_rows([y_o], dx2, "wgrad_out_odd")
    g_win_o = _wgrad_cols(x1, norm_odd, dp2, "wgrad_in_odd")
    tok = emit("odd", dict(w_in_odd=g_win_o, w_out_odd=g_wout_o))
    dx1, g_norm_odd = _in_proj_bwd_x(dp2, x1, norm_odd, win_o, dx2, "in_odd_bwd", dep(tok))

    dypre, daz, yg, dt, ya2, g_bglu = _s5_gate_bwd(ypre, p, dx1, wout_e, wglu, bglu)
    du, g_d, g_cm, g_bb, g_ar, g_ai = _s5_scan_bwd(p, dypre, s5_states, bb, cm, jnp.swapaxes(cm, 1, 2), abr, abi,
                                                   pwr, pwi, par, pai, pbr, pbi, s5_d)
    dbbr3 = _diag_blocks(g_bb[:, :, :SW], HG, P)
    dbbi3 = _diag_blocks(g_bb[:, :, SW:], HG, P)
    g_c_re = jnp.swapaxes(_diag_blocks(g_cm[:, :SW, :], P, HG), 1, 2)
    g_c_im = -jnp.swapaxes(_diag_blocks(g_cm[:, SW:, :], P, HG), 1, 2)
    g_lr3, g_li3, g_dt3, g_br3, g_bi3 = _s5_disc_bwd(
        lr3, li3, dt3, br3, bi3, g_ar.reshape(G, 1, P), g_ai.reshape(G, 1, P), dbbr3, dbbi3)
    dp, yb2, g_ret_gain = _ret_bwd(p, cos, sin, ret_gain, ret_states, dx1, wout_e, du, daz)
    small = dict(
        s5_lam_re=g_lr3.reshape(G, P), s5_lam_im=g_li3.reshape(G, P),
        s5_log_dt=g_dt3.reshape(1, G), s5_b_re=g_br3, s5_b_im=g_bi3,
        s5_c_re=g_c_re, s5_c_im=g_c_im, s5_d=g_d, s5_b_glu=g_bglu, ret_gn_gain=g_ret_gain,
        norm_odd=g_norm_odd, sgu_norm_gain=g_sgu_gain, sgu_w_spatial=g_wsp, sgu_b_spatial=g_bsp.reshape(SG, CH),
        final_norm=g_fnorm)
    tok = emit("small", small)
    g_win_e = _wgrad_cols(x, norm_even, dp, "wgrad_in_even", dep(tok))
    tok = emit("even_cols", dict(w_in_even=g_win_e))
    g_wout_e = _wgrad_rows([ya2, yb2], dx1, "wgrad_out_even", dep(tok))
    g_wglu = _wgrad_rows([yg], dt, "wgrad_glu", dep(tok))
    tok = emit("even_rows", dict(s5_w_glu=g_wglu, w_out_even=g_wout_e))
    dx, g_norm_even = _in_proj_bwd_x(dp, x, norm_even, win_e, dx1, "in_even_bwd", dep(tok))
    emit("last", dict(norm_even=g_norm_even))
    return loss, dx


WEIGHTS = ['norm_even', 'w_in_even', 's5_lam_re', 's5_lam_im', 's5_log_dt', 's5_b_re', 's5_b_im', 's5_c_re',
           's5_c_im', 's5_d', 's5_w_glu', 's5_b_glu', 'ret_gn_gain', 'w_out_even', 'norm_odd', 'w_in_odd',
           'sgu_norm_gain', 'sgu_w_spatial', 'sgu_b_spatial', 'w_out_odd', 'final_norm']
BIG = ['w_in_even', 's5_w_glu', 'w_out_even', 'w_in_odd', 'w_out_odd']
SHARDED_SMALL = {'norm_odd': D // NDEV, 'sgu_norm_gain': DI // NDEV}
SMALL = [n for n in WEIGHTS if n not in BIG and n != 'norm_even']


def _view(n, a):
    if n in ('s5_b_re', 's5_b_im'):
        return jnp.swapaxes(a[0], 1, 2)
    if n == 'final_norm':
        return a.reshape(1, D)
    return a[0] if a.ndim >= 3 else a


def _unview(n, t, shape):
    if n in ('s5_b_re', 's5_b_im'):
        return jnp.swapaxes(t, 1, 2)[None]
    return t.reshape(shape)


def kernel(x, norm_even, w_in_even, s5_lam_re, s5_lam_im, s5_log_dt, s5_b_re, s5_b_im, s5_c_re, s5_c_im, s5_d, s5_w_glu, s5_b_glu, ret_gn_gain, w_out_even, norm_odd, w_in_odd, sgu_norm_gain, sgu_w_spatial, sgu_b_spatial, w_out_odd, final_norm, loss_target, m_norm_even, m_w_in_even, m_s5_lam_re, m_s5_lam_im, m_s5_log_dt, m_s5_b_re, m_s5_b_im, m_s5_c_re, m_s5_c_im, m_s5_d, m_s5_w_glu, m_s5_b_glu, m_ret_gn_gain, m_w_out_even, m_norm_odd, m_w_in_odd, m_sgu_norm_gain, m_sgu_w_spatial, m_sgu_b_spatial, m_w_out_odd, m_final_norm, v_norm_even, v_w_in_even, v_s5_lam_re, v_s5_lam_im, v_s5_log_dt, v_s5_b_re, v_s5_b_im, v_s5_c_re, v_s5_c_im, v_s5_d, v_s5_w_glu, v_s5_b_glu, v_ret_gn_gain, v_w_out_even, v_norm_odd, v_w_in_odd, v_sgu_norm_gain, v_sgu_w_spatial, v_sgu_b_spatial, v_w_out_odd, v_final_norm):
    args = dict(locals())
    w = {n: args[n] for n in WEIGHTS}
    m = {n: args["m_" + n] for n in WEIGHTS}
    v = {n: args["v_" + n] for n in WEIGHTS}
    me = _my_index()

    first_handle, _ = _gather2_start(w['w_in_even'][0].astype(MXU), "gather_first_start")

    def first_weight(after):
        return _gather2_wait(_gather2_forward(first_handle, after, "gather_first_forward"), "gather_first_wait")

    late_own = [w['s5_w_glu'][0].astype(MXU), w['w_out_even'][0].astype(MXU), w['norm_odd'],
                w['w_in_odd'][0].astype(MXU), w['sgu_norm_gain'], w['w_out_odd'][0].astype(MXU)]
    late_handle, start_token = _exchange_start(late_own, False, "gather_late_start")

    def late_weights(after):
        _, (wglu, wout_e, nodd, win_o, sgug, wout_o) = _exchange_wait(late_handle, after, "gather_late_wait")
        return (wglu.reshape(D, D), wout_e.reshape(DI, D), nodd.reshape(1, D), win_o, sgug.reshape(1, DI),
                wout_o.reshape(DI, D))

    pending = {}
    small_last = {}

    def emit(stage, grads):
        if stage == "last":
            small_last.update(grads)
            return None
        names = list(grads) if stage != "small" else SMALL
        handle, token = _exchange_start([grads[n] for n in names], stage != "small", stage + "_start")
        pending[stage] = (handle, names)
        return token

    loss, dx = _local_step(
        x[0], loss_target[0], w['norm_even'], first_weight, w['s5_lam_re'][0], w['s5_lam_im'][0], w['s5_log_dt'][0],
        w['s5_b_re'][0], w['s5_b_im'][0], w['s5_c_re'][0], w['s5_c_im'][0], w['s5_d'], w['s5_b_glu'],
        w['ret_gn_gain'], w['sgu_w_spatial'][0], w['sgu_b_spatial'][0], w['final_norm'].reshape(1, D),
        late_weights, emit, start_token)

    out_g, out_d, out_m, out_v = {}, {}, {}, {}
    after = dx
    for stage in ("odd", "even_cols", "even_rows"):
        handle, names = pending[stage]
        sent, lands = _exchange_wait(handle, (after,), stage + "_wait")
        for n, land, stack in zip(names, lands, sent):
            shp = w[n].shape
            r, c = shp[1], shp[2]
            own = lax.dynamic_index_in_dim(stack, me, 0, keepdims=False)
            res = _adamw(w[n].reshape(r, c), m[n].reshape(r, c), v[n].reshape(r, c), land, own, "adamw_" + n)
            out_g[n], out_d[n], out_m[n], out_v[n] = (t.reshape(shp) for t in res)
            after = res[0]

    handle, names = pending["small"]
    owns, lands = _exchange_wait(handle, (after,), "small_wait")
    (gne,) = _exchange([small_last['norm_even']], False, "gather_norm_even")
    names = names + ['norm_even']
    owns = owns + [small_last['norm_even']]
    lands = lands + [gne]
    for i, n in enumerate(names):
        if n in SHARDED_SMALL:
            width = SHARDED_SMALL[n]
            owns[i] = lax.dynamic_slice_in_dim(owns[i], me * width, width, axis=1)
            lands[i] = lax.dynamic_slice_in_dim(lands[i], me * width, width, axis=2)
    res = _adamw_many([_view(n, w[n]) for n in names], [_view(n, m[n]) for n in names],
                      [_view(n, v[n]) for n in names], lands, owns, "adamw_small")
    for dst, vals in zip((out_g, out_d, out_m, out_v), res):
        for n, t in zip(names, vals):
            dst[n] = _unview(n, t, w[n].shape)

    loss_total = lax.psum(loss[0, 0], AXES)
    return (loss_total, dx[None], *[out_g[n] for n in WEIGHTS], *[out_d[n] for n in WEIGHTS],
            *[out_m[n] for n in WEIGHTS], *[out_v[n] for n in WEIGHTS])
```

```python
import math

import jax
import jax.numpy as jnp
from jax import lax
from jax.experimental import pallas as pl
from jax.experimental.pallas import tpu as pltpu

F32 = jnp.float32
MXU = jnp.bfloat16
AXES = ("x", "y", "c")
NDEV = 8
D = 1024
NIN = 6144
WIN_BLK = NIN // NDEV
DI = 2048
G, P, HG = 64, 64, 16
GB = 8
NJ = G // GB
SW = GB * P
UW = GB * HG
NSTATE = G * P
HEADS, DK = 4, 256
CH = 128
SG, SGD = 4, 512
EPS = 1e-6
ROPE_BASE = 10000.0
VMEM_CAP_V7X = 64 * 1024 * 1024
LOG_G = [math.log1p(-2.0 ** (-5.0 - h)) for h in range(HEADS)]
GELU_C = math.sqrt(2.0 / math.pi)

ADAM_LR, ADAM_B1, ADAM_B2, ADAM_EPS, ADAM_WD, ADAM_STEP = 0.001, 0.9, 0.999, 1e-08, 0.01, 10
BC1 = 1.0 - ADAM_B1 ** ADAM_STEP
BC2 = 1.0 - ADAM_B2 ** ADAM_STEP

SDS = jax.ShapeDtypeStruct
ARB2 = ("arbitrary", "arbitrary")


def _cp(vmem_mib, sem=None):
    kw = dict(vmem_limit_bytes=min(vmem_mib * 1024 * 1024, VMEM_CAP_V7X - 4 * 1024 * 1024))
    if sem is not None:
        kw["dimension_semantics"] = sem
    return pltpu.CompilerParams(**kw)


def _mm(a, b):
    return jnp.dot(a.astype(MXU), b.astype(MXU), preferred_element_type=F32)


def _mm_nt(a, b):
    return lax.dot_general(a.astype(MXU), b.astype(MXU), (((1,), (1,)), ((), ())), preferred_element_type=F32)


def _mm_tn(a, b):
    return lax.dot_general(a.astype(MXU), b.astype(MXU), (((0,), (0,)), ((), ())), preferred_element_type=F32)


def _gelu(x):
    return _gelu_and_grad(x)[0]


def _gelu_and_grad(x):
    x2 = x * x
    th = jnp.tanh(GELU_C * x * (1.0 + 0.044715 * x2))
    hp = 0.5 * (1.0 + th)
    return x * hp, hp + 0.5 * x * (1.0 - th * th) * GELU_C * (1.0 + 3.0 * 0.044715 * x2)


def _silu_and_grad(x):
    s = jax.nn.sigmoid(x)
    return x * s, s * (1.0 + x * (1.0 - s))


def _full(shape):
    nd = len(shape)
    return pl.BlockSpec(shape, lambda *_: (0,) * nd)


def _rms(xf):
    r = lax.rsqrt(jnp.mean(xf * xf, axis=-1, keepdims=True) + EPS)
    return xf * r, r


ANY_SPEC = pl.BlockSpec(memory_space=pl.ANY)
NO_DEPS = ()


def _load_once(src_hbm, dst_vmem, sem):
    @pl.when(pl.program_id(0) == 0)
    def _():
        cp = pltpu.make_async_copy(src_hbm, dst_vmem, sem)
        cp.start()
        cp.wait()


def _in_proj(x, gain, wst, name, deps=NO_DEPS):
    L = x.shape[0]
    tm = min(512, L)

    def body(x_ref, g_ref, w_hbm, *rest):
        o_ref, w_scr, sem = rest[len(deps):]
        _load_once(w_hbm, w_scr, sem)
        xhat, _ = _rms(x_ref[...])
        h = (xhat * g_ref[...]).astype(MXU)
        for c in range(NDEV):
            o_ref[:, c * WIN_BLK:(c + 1) * WIN_BLK] = jnp.dot(h, w_scr[c], preferred_element_type=F32)

    return pl.pallas_call(
        body, name=name, grid=(L // tm,),
        in_specs=[pl.BlockSpec((tm, D), lambda i: (i, 0)), _full((1, D)), ANY_SPEC] + [ANY_SPEC] * len(deps),
        out_specs=pl.BlockSpec((tm, NIN), lambda i: (i, 0)),
        out_shape=SDS((L, NIN), F32),
        scratch_shapes=[pltpu.VMEM((NDEV, D, WIN_BLK), MXU), pltpu.SemaphoreType.DMA(())],
        compiler_params=_cp(56, ("arbitrary",)),
    )(x, gain, wst, *deps)


def _in_proj_bwd_x(dp, x, gain, wst, dres, name, deps=NO_DEPS):
    L = x.shape[0]
    tm = min(512, L)

    def body(dp_ref, x_ref, g_ref, w_hbm, dres_ref, *rest):
        dx_ref, gg_ref, w_scr, sem = rest[len(deps):]
        _load_once(w_hbm, w_scr, sem)

        @pl.when(pl.program_id(0) == 0)
        def _():
            gg_ref[...] = jnp.zeros_like(gg_ref)

        dh = _mm_nt(dp_ref[:, 0:WIN_BLK], w_scr[0])
        for c in range(1, NDEV):
            dh += _mm_nt(dp_ref[:, c * WIN_BLK:(c + 1) * WIN_BLK], w_scr[c])
        xhat, r = _rms(x_ref[...])
        dxhat = dh * g_ref[...]
        dx_ref[...] = dres_ref[...] + r * (dxhat - xhat * jnp.mean(dxhat * xhat, axis=-1, keepdims=True))
        gg_ref[...] += jnp.sum(dh * xhat, axis=0, keepdims=True)

    row = pl.BlockSpec((tm, D), lambda i: (i, 0))
    return pl.pallas_call(
        body, name=name, grid=(L // tm,),
        in_specs=[pl.BlockSpec((tm, NIN), lambda i: (i, 0)), row, _full((1, D)), ANY_SPEC, row]
        + [ANY_SPEC] * len(deps),
        out_specs=[row, _full((1, D))],
        out_shape=[SDS((L, D), F32), SDS((1, D), F32)],
        scratch_shapes=[pltpu.VMEM((NDEV, D, WIN_BLK), MXU), pltpu.SemaphoreType.DMA(())],
        compiler_params=_cp(56, ("arbitrary",)),
    )(dp, x, gain, wst, dres, *deps)


def _wgrad_cols(x, gain, dp, name, deps=NO_DEPS):
    L = x.shape[0]
    tk = min(512, L)
    nk = L // tk
    halves = 2
    nh = NDEV // halves

    def body(x_ref, g_ref, dp_ref, *rest):
        o_ref, acc = rest[len(deps):]
        k = pl.program_id(1)

        @pl.when(k == 0)
        def _():
            acc[...] = jnp.zeros_like(acc)

        xhat, _ = _rms(x_ref[...])
        acc[...] += _mm_tn(xhat * g_ref[...], dp_ref[...])

        @pl.when(k == nk - 1)
        def _():
            for c in range(nh):
                o_ref[c] = acc[:, c * WIN_BLK:(c + 1) * WIN_BLK].astype(o_ref.dtype)

    return pl.pallas_call(
        body, name=name, grid=(halves, nk),
        in_specs=[pl.BlockSpec((tk, D), lambda n, k: (k, 0)), _full((1, D)),
                  pl.BlockSpec((tk, nh * WIN_BLK), lambda n, k: (k, n))] + [ANY_SPEC] * len(deps),
        out_specs=pl.BlockSpec((nh, D, WIN_BLK), lambda n, k: (n, 0, 0)),
        out_shape=SDS((NDEV, D, WIN_BLK), MXU),
        scratch_shapes=[pltpu.VMEM((D, nh * WIN_BLK), F32)],
        compiler_params=_cp(56, ARB2),
    )(x, gain, dp, *deps)


def _wgrad_rows(a_parts, b, name, deps=NO_DEPS):
    L, N = b.shape
    na = len(a_parts)
    widths = [a.shape[1] for a in a_parts]
    M = sum(widths)
    tk = min(512, L)
    nk = L // tk

    def body(*refs):
        a_refs, b_ref = refs[:na], refs[na]
        o_ref, acc = refs[na + 1 + len(deps):]
        k = pl.program_id(0)

        @pl.when(k == 0)
        def _():
            acc[...] = jnp.zeros_like(acc)

        bv = b_ref[...].astype(MXU)
        off = 0
        for a_ref, wd in zip(a_refs, widths):
            acc[off:off + wd, :] += _mm_tn(a_ref[...], bv)
            off += wd

        @pl.when(k == nk - 1)
        def _():
            o_ref[...] = acc[...].astype(o_ref.dtype).reshape(o_ref.shape)

    return pl.pallas_call(
        body, name=name, grid=(nk,),
        in_specs=[pl.BlockSpec((tk, wd), lambda k: (k, 0)) for wd in widths]
        + [pl.BlockSpec((tk, N), lambda k: (k, 0))] + [ANY_SPEC] * len(deps),
        out_specs=_full((NDEV, M // NDEV, N)),
        out_shape=SDS((NDEV, M // NDEV, N), MXU),
        scratch_shapes=[pltpu.VMEM((M, N), F32)],
        compiler_params=_cp(48, ("arbitrary",)),
    )(*a_parts, b, *deps)


def _s5_disc_fn(lr_raw, li, logdt, br, bi):
    lr = jnp.minimum(lr_raw, -1e-4)
    dt = jnp.exp(logdt)
    mag = jnp.exp(lr * dt)
    abr = mag * jnp.cos(li * dt)
    abi = mag * jnp.sin(li * dt)
    den = lr * lr + li * li
    nre = abr - 1.0
    nim = abi
    zr = (nre * lr + nim * li) / den
    zi = (nim * lr - nre * li) / den
    return abr, abi, zr * br - zi * bi, zr * bi + zi * br


def _s5_disc(lr, li, logdt, br, bi):
    def body(lr_ref, li_ref, dt_ref, br_ref, bi_ref, abr_ref, abi_ref, bbr_ref, bbi_ref):
        abr, abi, bbr, bbi = _s5_disc_fn(lr_ref[...], li_ref[...], dt_ref[...], br_ref[...], bi_ref[...])
        abr_ref[...] = abr
        abi_ref[...] = abi
        bbr_ref[...] = bbr
        bbi_ref[...] = bbi

    s1, s3 = SDS((G, 1, P), F32), SDS((G, HG, P), F32)
    return pl.pallas_call(body, name="s5_disc", out_shape=[s1, s1, s3, s3])(lr, li, logdt, br, bi)


def _s5_disc_bwd(lr, li, logdt, br, bi, dabr, dabi, dbbr, dbbi):
    def body(lr_ref, li_ref, dt_ref, br_ref, bi_ref, c0, c1, c2, c3, o0, o1, o2, o3, o4):
        _, vjp = jax.vjp(_s5_disc_fn, lr_ref[...], li_ref[...], dt_ref[...], br_ref[...], bi_ref[...])
        g = vjp((c0[...], c1[...], c2[...], c3[...]))
        for o, v in zip((o0, o1, o2, o3, o4), g):
            o[...] = v

    s1, s3 = SDS((G, 1, P), F32), SDS((G, HG, P), F32)
    return pl.pallas_call(body, name="s5_disc_bwd", out_shape=[s1, s1, SDS((G, 1, 1), F32), s3, s3])(
        lr, li, logdt, br, bi, dabr, dabi, dbbr, dbbi)


def _s5_tables(abr, abi, rows, name):
    def body(ar_ref, ai_ref, pfr, pfi, pbr, pbi):
        pfr[0:1, :] = ar_ref[...]
        pfi[0:1, :] = ai_ref[...]
        pbr[rows - 1:rows, :] = ar_ref[...]
        pbi[rows - 1:rows, :] = ai_ref[...]
        n = 1
        while n < rows:
            er, ei = pfr[n - 1:n, :], pfi[n - 1:n, :]
            xr, xi = pfr[0:n, :], pfi[0:n, :]
            pfr[n:2 * n, :] = er * xr - ei * xi
            pfi[n:2 * n, :] = er * xi + ei * xr
            yr, yi = pbr[rows - n:rows, :], pbi[rows - n:rows, :]
            pbr[rows - 2 * n:rows - n, :] = er * yr - ei * yi
            pbi[rows - 2 * n:rows - n, :] = er * yi + ei * yr
            n *= 2

    s = SDS((rows, NSTATE), F32)
    return pl.pallas_call(body, name=name, out_shape=[s, s, s, s], compiler_params=_cp(40))(abr, abi)


def _cscan(br, bi, pr_ref, pi_ref, reverse):
    T = br.shape[0]
    sign = -1.0 if reverse else 1.0
    row = lax.broadcasted_iota(jnp.int32, br.shape, 0)
    k = 1
    while k < T:
        akr = pr_ref[k - 1:k, :]
        aki = sign * pi_ref[k - 1:k, :]

        def shift(v):
            if k % 8 == 0:
                z = jnp.zeros((k, v.shape[1]), v.dtype)
                return jnp.concatenate([v[k:], z], 0) if reverse else jnp.concatenate([z, v[:T - k]], 0)
            if reverse:
                return jnp.where(row < T - k, pltpu.roll(v, T - k, 0), 0.0)
            return jnp.where(row >= k, pltpu.roll(v, k, 0), 0.0)

        sr, si = shift(br), shift(bi)
        br, bi = br + akr * sr - aki * si, bi + akr * si + aki * sr
        k *= 2
    return br, bi


def _embed(t):
    a, b = t.shape[1], t.shape[2]
    return jnp.einsum("jgab,gh->jgahb", t.reshape(NJ, GB, a, b), jnp.eye(GB, dtype=t.dtype)).reshape(NJ, GB * a, GB * b)


def _diag_blocks(t, a, b):
    return jnp.einsum("jgahb,gh->jgab", t.reshape(NJ, GB, a, GB, b), jnp.eye(GB, dtype=t.dtype)).reshape(G, a, b)


NT = 16


def _chunks(L):
    ncb = min(CH, L // NT)
    return ncb, NT * ncb


def _cmul_add(ar, ai, xr, xi, br, bi):
    return ar * xr - ai * xi + br, ar * xi + ai * xr + bi


def _pow_weights(w_ref, pwr_ref, pwi_ref, dst, adjoint):
    w = w_ref[...].astype(F32)
    wr, wi = w[:, :SW], w[:, SW:]
    for t in range(NT):
        k = t if adjoint else NT - 1 - t
        if k == 0:
            blk = w
        else:
            pr, pi = pwr_ref[k - 1:k, :], pwi_ref[k - 1:k, :]
            if adjoint:
                blk = jnp.concatenate([pr * wr + pi * wi, pr * wi - pi * wr], axis=1)
            else:
                blk = jnp.concatenate([pr * wr - pi * wi, pr * wi + pi * wr], axis=1)
        dst[t * UW:(t + 1) * UW, :] = blk.astype(dst.dtype)


def _s5_states(u_ref, bb_ref, bbp_scr, par_ref, pai_ref, cr, ci, ncb, bu_scr):
    us = [u_ref[pl.ds(t, ncb, stride=NT), :] for t in range(NT)]
    for t in range(NT):
        bu_scr[t] = _mm(us[t], bb_ref[...])
    e = _mm(jnp.concatenate(us, axis=1), bbp_scr[...])
    xr, xi = _cscan(e[:, :SW], e[:, SW:], par_ref, pai_ref, False)
    fr, fi = _cmul_add(par_ref[0:ncb, :], pai_ref[0:ncb, :], cr, ci, xr, xi)
    row = lax.broadcasted_iota(jnp.int32, fr.shape, 0)
    cinr = jnp.where(row >= 1, pltpu.roll(fr, 1, 0), cr)
    cini = jnp.where(row >= 1, pltpu.roll(fi, 1, 0), ci)
    return cinr, cini, jnp.concatenate([fr[ncb - 1:ncb, :], fi[ncb - 1:ncb, :]], axis=1)


def _s5_scan_fwd(p, bb, cm, abr, abi, pwr, pwi, par, pai, dskip):
    L = p.shape[0]
    ncb, tb = _chunks(L)
    nb = L // tb

    def body(u_ref, bb_ref, cm_ref, ar_ref, ai_ref, pwr_ref, pwi_ref, par_ref, pai_ref, d_ref, ypre_ref, st_ref,
             carry, bu_scr, bbp_scr):
        @pl.when(pl.program_id(1) == 0)
        def _():
            carry[...] = jnp.zeros_like(carry)
            _pow_weights(bb_ref, pwr_ref, pwi_ref, bbp_scr, False)

        c = carry[...]
        st_ref[...] = c
        ar, ai = ar_ref[...], ai_ref[...]
        sr, si, cnext = _s5_states(u_ref, bb_ref, bbp_scr, par_ref, pai_ref, c[:, :SW], c[:, SW:], ncb, bu_scr)
        carry[...] = cnext
        for t in range(NT):
            bu = bu_scr[t]
            sr, si = _cmul_add(ar, ai, sr, si, bu[:, :SW], bu[:, SW:])
            rows = pl.ds(t, ncb, stride=NT)
            ypre_ref[rows, :] = _mm(jnp.concatenate([sr, si], axis=1), cm_ref[...]) + d_ref[...] * u_ref[rows, :]

    tab = pl.BlockSpec((CH, SW), lambda j, i: (0, j))
    stp = pl.BlockSpec((NT, SW), lambda j, i: (0, j))
    vec = lambda w: pl.BlockSpec((1, w), lambda j, i: (0, j))
    return pl.pallas_call(
        body, name="s5_scan_fwd", grid=(NJ, nb),
        in_specs=[pl.BlockSpec((tb, UW), lambda j, i: (i, j)),
                  pl.BlockSpec((None, UW, 2 * SW), lambda j, i: (j, 0, 0)),
                  pl.BlockSpec((None, 2 * SW, UW), lambda j, i: (j, 0, 0)),
                  vec(SW), vec(SW), stp, stp, tab, tab, vec(UW)],
        out_specs=[pl.BlockSpec((tb, UW), lambda j, i: (i, j)),
                   pl.BlockSpec((None, None, 1, 2 * SW), lambda j, i: (j, i, 0, 0))],
        out_shape=[SDS((L, D), F32), SDS((NJ, nb, 1, 2 * SW), F32)],
        scratch_shapes=[pltpu.VMEM((1, 2 * SW), F32), pltpu.VMEM((NT, ncb, 2 * SW), F32),
                        pltpu.VMEM((NT * UW, 2 * SW), MXU)],
        compiler_params=_cp(48, ARB2),
    )(p, bb, cm, abr, abi, pwr, pwi, par, pai, dskip)


def _s5_gate_fwd(ypre, p, wglu, bglu):
    L = ypre.shape[0]
    tm = min(256, L)

    def body(y_ref, az_ref, wg_ref, bg_ref, ya_ref):
        yg = _gelu(y_ref[...])
        t = _mm(yg, wg_ref[...]) + bg_ref[...]
        act, _ = _silu_and_grad(az_ref[...])
        ya_ref[...] = (yg * jax.nn.sigmoid(t) * act).astype(ya_ref.dtype)

    return pl.pallas_call(
        body, name="s5_gate_fwd", grid=(L // tm,),
        in_specs=[pl.BlockSpec((tm, D), lambda i: (i, 0)), pl.BlockSpec((tm, D), lambda i: (i, 1)),
                  _full((D, D)), _full((1, D))],
        out_specs=pl.BlockSpec((tm, D), lambda i: (i, 0)),
        out_shape=SDS((L, D), MXU),
        compiler_params=_cp(32, ("arbitrary",)),
    )(ypre, p, wglu, bglu)


def _s5_gate_bwd(ypre, p, dx1, wout_e, wglu, bglu, deps=NO_DEPS):
    L = ypre.shape[0]
    tm = min(256, L)

    def body(y_ref, az_ref, dx1_ref, wo_ref, wg_ref, bg_ref, *rest):
        dyp_ref, daz_ref, yg_ref, dt_ref, ya_ref, gbg_ref = rest[len(deps):]

        @pl.when(pl.program_id(0) == 0)
        def _():
            gbg_ref[...] = jnp.zeros_like(gbg_ref)

        yg, dgelu = _gelu_and_grad(y_ref[...])
        sg = jax.nn.sigmoid(_mm(yg, wg_ref[...]) + bg_ref[...])
        act, dact = _silu_and_grad(az_ref[...])
        y2 = yg * sg
        dya = _mm_nt(dx1_ref[...], wo_ref[...])
        daz_ref[...] = (dya * y2 * dact).astype(daz_ref.dtype)
        dy2 = dya * act
        dt = dy2 * yg * sg * (1.0 - sg)
        dyg = dy2 * sg + _mm_nt(dt, wg_ref[...])
        dyp_ref[...] = dyg * dgelu
        yg_ref[...] = yg.astype(yg_ref.dtype)
        dt_ref[...] = dt.astype(dt_ref.dtype)
        ya_ref[...] = (y2 * act).astype(ya_ref.dtype)
        gbg_ref[...] += jnp.sum(dt, axis=0, keepdims=True)

    row = pl.BlockSpec((tm, D), lambda i: (i, 0))
    return pl.pallas_call(
        body, name="s5_gate_bwd", grid=(L // tm,),
        in_specs=[row, pl.BlockSpec((tm, D), lambda i: (i, 1)), row,
                  pl.BlockSpec((D, D), lambda i: (0, 0)), _full((D, D)), _full((1, D))] + [ANY_SPEC] * len(deps),
        out_specs=[row, row, row, row, row, _full((1, D))],
        out_shape=[SDS((L, D), F32), SDS((L, D), MXU), SDS((L, D), MXU), SDS((L, D), MXU), SDS((L, D), MXU),
                   SDS((1, D), F32)],
        compiler_params=_cp(40, ("arbitrary",)),
    )(ypre, p, dx1, wout_e, wglu, bglu, *deps)


def _s5_scan_bwd(p, dypre, states, bb, cm, cmt, abr, abi, pwr, pwi, par, pai, pbr, pbi, dskip, deps=NO_DEPS):
    L = p.shape[0]
    ncb, tb = _chunks(L)
    nb = L // tb
    rev = lambda i: nb - 1 - i

    def body(u_ref, dy_ref, st_ref, bb_ref, cm_ref, cmt_ref, ar_ref, ai_ref, pwr_ref, pwi_ref, par_ref, pai_ref,
             pbr_ref, pbi_ref, d_ref, *rest):
        (du_ref, gd_ref, gcm_ref, gbb_ref, gar_ref, gai_ref,
         lcarry, bu_scr, s_scr, gs_scr, bbp_scr, cmp_scr) = rest[len(deps):]
        del cm_ref

        @pl.when(pl.program_id(1) == 0)
        def _():
            _pow_weights(bb_ref, pwr_ref, pwi_ref, bbp_scr, False)
            _pow_weights(cmt_ref, pwr_ref, pwi_ref, cmp_scr, True)
            lcarry[...] = jnp.zeros_like(lcarry)
            gd_ref[...] = jnp.zeros_like(gd_ref)
            gcm_ref[...] = jnp.zeros_like(gcm_ref)
            gbb_ref[...] = jnp.zeros_like(gbb_ref)
            gar_ref[...] = jnp.zeros_like(gar_ref)
            gai_ref[...] = jnp.zeros_like(gai_ref)

        ar, ai = ar_ref[...], ai_ref[...]
        c = st_ref[...]
        sr, si, _ = _s5_states(u_ref, bb_ref, bbp_scr, par_ref, pai_ref, c[:, :SW], c[:, SW:], ncb, bu_scr)
        s_scr[0] = jnp.concatenate([sr, si], axis=1)
        for t in range(NT):
            bu = bu_scr[t]
            sr, si = _cmul_add(ar, ai, sr, si, bu[:, :SW], bu[:, SW:])
            s_scr[t + 1] = jnp.concatenate([sr, si], axis=1)
        dys = [dy_ref[pl.ds(t, ncb, stride=NT), :] for t in range(NT)]
        for t in range(NT):
            gs_scr[t] = _mm(dys[t], cmt_ref[...])
        f = _mm(jnp.concatenate(dys, axis=1), cmp_scr[...])
        xr, xi = _cscan(f[:, :SW], f[:, SW:], par_ref, pai_ref, True)
        lc = lcarry[...]
        lcr, lci = lc[:, :SW], lc[:, SW:]
        hr, hi = _cmul_add(pbr_ref[CH - ncb:CH, :], -pbi_ref[CH - ncb:CH, :], lcr, lci, xr, xi)
        lcarry[...] = jnp.concatenate([hr[0:1, :], hi[0:1, :]], axis=1)
        row = lax.broadcasted_iota(jnp.int32, hr.shape, 0)
        lr_ = jnp.where(row < ncb - 1, pltpu.roll(hr, ncb - 1, 0), lcr)
        li_ = jnp.where(row < ncb - 1, pltpu.roll(hi, ncb - 1, 0), lci)
        gar = jnp.zeros((1, SW), F32)
        gai = jnp.zeros((1, SW), F32)
        for t in reversed(range(NT)):
            gs = gs_scr[t]
            lr_, li_ = _cmul_add(ar, -ai, lr_, li_, gs[:, :SW], gs[:, SW:])
            rows = pl.ds(t, ncb, stride=NT)
            u_t, dy_t = u_ref[rows, :], dy_ref[rows, :]
            lam = jnp.concatenate([lr_, li_], axis=1)
            gbb_ref[...] += _mm_tn(u_t, lam)
            du_ref[rows, :] = _mm_nt(lam, bb_ref[...]) + dy_t * d_ref[...]
            gd_ref[...] += jnp.sum(dy_t * u_t, axis=0, keepdims=True)
            gcm_ref[...] += _mm_tn(s_scr[t + 1], dy_t)
            sp = s_scr[t]
            spr, spi = sp[:, :SW], sp[:, SW:]
            gar += jnp.sum(lr_ * spr + li_ * spi, axis=0, keepdims=True)
            gai += jnp.sum(li_ * spr - lr_ * spi, axis=0, keepdims=True)
        gar_ref[...] += gar
        gai_ref[...] += gai

    tab = pl.BlockSpec((CH, SW), lambda j, i: (0, j))
    stp = pl.BlockSpec((NT, SW), lambda j, i: (0, j))
    colblk = pl.BlockSpec((tb, UW), lambda j, i: (rev(i), j))
    vec = lambda w: pl.BlockSpec((1, w), lambda j, i: (0, j))
    return pl.pallas_call(
        body, name="s5_scan_bwd", grid=(NJ, nb),
        in_specs=[colblk, colblk,
                  pl.BlockSpec((None, None, 1, 2 * SW), lambda j, i: (j, rev(i), 0, 0)),
                  pl.BlockSpec((None, UW, 2 * SW), lambda j, i: (j, 0, 0)),
                  pl.BlockSpec((None, 2 * SW, UW), lambda j, i: (j, 0, 0)),
                  pl.BlockSpec((None, UW, 2 * SW), lambda j, i: (j, 0, 0)),
                  vec(SW), vec(SW), stp, stp, tab, tab, tab, tab, vec(UW)] + [ANY_SPEC] * len(deps),
        out_specs=[colblk, vec(UW),
                   pl.BlockSpec((None, 2 * SW, UW), lambda j, i: (j, 0, 0)),
                   pl.BlockSpec((None, UW, 2 * SW), lambda j, i: (j, 0, 0)),
                   vec(SW), vec(SW)],
        out_shape=[SDS((L, D), F32), SDS((1, D), F32),
                   SDS((NJ, 2 * SW, UW), F32), SDS((NJ, UW, 2 * SW), F32),
                   SDS((1, NSTATE), F32), SDS((1, NSTATE), F32)],
        scratch_shapes=[pltpu.VMEM((1, 2 * SW), F32), pltpu.VMEM((NT, ncb, 2 * SW), F32),
                        pltpu.VMEM((NT + 1, ncb, 2 * SW), F32), pltpu.VMEM((NT, ncb, 2 * SW), F32),
                        pltpu.VMEM((NT * UW, 2 * SW), MXU), pltpu.VMEM((NT * UW, 2 * SW), MXU)],
        compiler_params=_cp(56, ARB2),
    )(p, dypre, states, bb, cm, cmt, abr, abi, pwr, pwi, par, pai, pbr, pbi, dskip, *deps)


def _rope_tables(L, inv):
    tm = min(512, L)

    def body(inv_ref, cos_ref, sin_ref):
        pos = (lax.broadcasted_iota(jnp.int32, (tm, DK // 2), 0) + pl.program_id(0) * tm).astype(F32)
        ang = pos * inv_ref[...]
        cos_ref[...] = jnp.cos(ang)
        sin_ref[...] = jnp.sin(ang)

    blk = pl.BlockSpec((tm, DK // 2), lambda i: (i, 0))
    return pl.pallas_call(body, name="rope_tables", grid=(L // tm,), in_specs=[_full((1, DK // 2))],
                          out_specs=[blk, blk], out_shape=[SDS((L, DK // 2), F32)] * 2)(inv)


def _rot(x, cos, sin):
    x1, x2 = x[:, :DK // 2], x[:, DK // 2:]
    return jnp.concatenate([x1 * cos - x2 * sin, x1 * sin + x2 * cos], axis=1)


def _unrot(d, cos, sin):
    d1, d2 = d[:, :DK // 2], d[:, DK // 2:]
    return jnp.concatenate([d1 * cos + d2 * sin, d2 * cos - d1 * sin], axis=1)


def _ret_decays(h):
    lg = LOG_G[h]
    n = lax.broadcasted_iota(jnp.int32, (CH, CH), 0)
    m = lax.broadcasted_iota(jnp.int32, (CH, CH), 1)
    diff = (n - m).astype(F32)
    decay = jnp.where(n >= m, jnp.exp(lg * jnp.maximum(diff, 0.0)), 0.0)
    idx = lax.broadcasted_iota(jnp.int32, (CH, 1), 0).astype(F32)
    xi = jnp.exp(lg * (idx + 1.0))
    zeta = jnp.exp(lg * (CH - 1.0 - idx))
    return decay, xi, zeta, math.exp(lg * CH)


def _ret_tables(dec_scr, vec_scr):
    for h in range(HEADS):
        decay, xi, zeta, _ = _ret_decays(h)
        dec_scr[h] = decay
        vec_scr[h] = jnp.concatenate([jnp.broadcast_to(xi, (CH, 128)), jnp.broadcast_to(zeta, (CH, 128))], axis=1)


def _ret_chunk_fwd(q, k, v, cos, sin, s_prev_b, decay, xi, zeta):
    qr = _rot(q, cos, sin)
    kr = _rot(k, cos, sin) * (DK ** -0.5)
    scores = _mm_nt(qr, kr) * decay
    o = _mm(scores, v) + _mm(qr * xi, s_prev_b)
    local = _mm_tn(kr * zeta, v)
    mu = jnp.mean(o, axis=-1, keepdims=True)
    oc = o - mu
    rstd = lax.rsqrt(jnp.mean(oc * oc, axis=-1, keepdims=True) + EPS)
    return qr, kr, scores, local, oc * rstd, rstd


def _ret_fwd(p, cos, sin, gain):
    L = p.shape[0]
    nb = L // CH

    def body(q_ref, k_ref, v_ref, bz_ref, cos_ref, sin_ref, g_ref, yb_ref, st_ref, state, dec_scr, vec_scr):
        @pl.when(pl.program_id(0) == 0)
        def _():
            state[...] = jnp.zeros_like(state)
            _ret_tables(dec_scr, vec_scr)

        cos, sin = cos_ref[...], sin_ref[...]
        act, _ = _silu_and_grad(bz_ref[...])
        for h in range(HEADS):
            hs = slice(h * DK, (h + 1) * DK)
            xi, zeta = vec_scr[h, :, 0:1], vec_scr[h, :, 128:129]
            s_prev = state[h]
            s_prev_b = s_prev.astype(MXU)
            st_ref[h] = s_prev_b
            _, _, _, local, on, _ = _ret_chunk_fwd(q_ref[:, hs], k_ref[:, hs], v_ref[:, hs], cos, sin,
                                                   s_prev_b, dec_scr[h], xi, zeta)
            state[h] = s_prev * math.exp(LOG_G[h] * CH) + local
            yb_ref[:, hs] = (on * g_ref[:, hs] * act[:, hs]).astype(yb_ref.dtype)

    col = lambda c: pl.BlockSpec((CH, D), lambda i: (i, c))
    rope = pl.BlockSpec((CH, DK // 2), lambda i: (i, 0))
    return pl.pallas_call(
        body, name="ret_fwd", grid=(nb,),
        in_specs=[col(2), col(3), col(4), col(5), rope, rope, _full((1, D))],
        out_specs=[pl.BlockSpec((CH, D), lambda i: (i, 0)),
                   pl.BlockSpec((None, HEADS, DK, DK), lambda i: (i, 0, 0, 0))],
        out_shape=[SDS((L, D), MXU), SDS((nb, HEADS, DK, DK), MXU)],
        scratch_shapes=[pltpu.VMEM((HEADS, DK, DK), F32), pltpu.VMEM((HEADS, CH, CH), F32),
                        pltpu.VMEM((HEADS, CH, 256), F32)],
        compiler_params=_cp(40, ("arbitrary",)),
    )(p, p, p, p, cos, sin, gain)


def _ret_bwd(p, cos, sin, gain, states, dx1, wout_e, du, daz, deps=NO_DEPS):
    L = p.shape[0]
    nb = L // CH
    rev = lambda i: nb - 1 - i

    def body(q_ref, k_ref, v_ref, bz_ref, cos_ref, sin_ref, g_ref, st_ref, dx1_ref, wo_ref, du_ref, daz_ref, *rest):
        dp_ref, yb_ref, gg_ref, gstate, dec_scr, vec_scr = rest[len(deps):]

        @pl.when(pl.program_id(0) == 0)
        def _():
            gstate[...] = jnp.zeros_like(gstate)
            gg_ref[...] = jnp.zeros_like(gg_ref)
            _ret_tables(dec_scr, vec_scr)

        cos, sin = cos_ref[...], sin_ref[...]
        act, dact = _silu_and_grad(bz_ref[...])
        dyb = _mm_nt(dx1_ref[...], wo_ref[...])
        dp_ref[:, 0:D] = du_ref[...].astype(dp_ref.dtype)
        dp_ref[:, D:2 * D] = daz_ref[...]
        for h in range(HEADS):
            hs = slice(h * DK, (h + 1) * DK)
            col = lambda part: slice((2 + part) * D + h * DK, (2 + part) * D + (h + 1) * DK)
            decay = dec_scr[h]
            xi, zeta = vec_scr[h, :, 0:1], vec_scr[h, :, 128:129]
            v = v_ref[:, hs]
            s_prev_b = st_ref[h]
            qr, kr, scores, _, on, rstd = _ret_chunk_fwd(q_ref[:, hs], k_ref[:, hs], v, cos, sin, s_prev_b,
                                                         decay, xi, zeta)
            gain_h = g_ref[:, hs]
            out = on * gain_h
            yb_ref[:, hs] = (out * act[:, hs]).astype(yb_ref.dtype)
            dyb_h = dyb[:, hs]
            dp_ref[:, col(3)] = (dyb_h * out * dact[:, hs]).astype(dp_ref.dtype)
            dout = dyb_h * act[:, hs]
            gg_ref[:, hs] += jnp.sum(dout * on, axis=0, keepdims=True)
            don = dout * gain_h
            do = rstd * (don - jnp.mean(don, axis=-1, keepdims=True)
                         - on * jnp.mean(don * on, axis=-1, keepdims=True))
            gnext = gstate[h]
            gnext_b = gnext.astype(MXU)
            dscores = _mm_nt(do, v) * decay
            dp_ref[:, col(2)] = (_mm_tn(scores, do) + _mm(kr * zeta, gnext_b)).astype(dp_ref.dtype)
            dqr = _mm(dscores, kr) + _mm_nt(do, s_prev_b) * xi
            dkr = _mm_tn(dscores, qr) + _mm_nt(v, gnext_b) * zeta
            gstate[h] = gnext * math.exp(LOG_G[h] * CH) + _mm_tn(qr * xi, do)
            dp_ref[:, col(0)] = _unrot(dqr, cos, sin).astype(dp_ref.dtype)
            dp_ref[:, col(1)] = (_unrot(dkr, cos, sin) * (DK ** -0.5)).astype(dp_ref.dtype)

    col = lambda c: pl.BlockSpec((CH, D), lambda i: (rev(i), c))
    rope = pl.BlockSpec((CH, DK // 2), lambda i: (rev(i), 0))
    outc = col(0)
    act_out = SDS((L, D), MXU)
    return pl.pallas_call(
        body, name="ret_bwd", grid=(nb,),
        in_specs=[col(2), col(3), col(4), col(5), rope, rope, _full((1, D)),
                  pl.BlockSpec((None, HEADS, DK, DK), lambda i: (rev(i), 0, 0, 0)),
                  outc, pl.BlockSpec((D, D), lambda i: (1, 0)), outc, outc] + [ANY_SPEC] * len(deps),
        out_specs=[pl.BlockSpec((CH, NIN), lambda i: (rev(i), 0)), outc, _full((1, D))],
        out_shape=[SDS((L, NIN), MXU), act_out, SDS((1, D), F32)],
        scratch_shapes=[pltpu.VMEM((HEADS, DK, DK), F32), pltpu.VMEM((HEADS, CH, CH), F32),
                        pltpu.VMEM((HEADS, CH, 256), F32)],
        compiler_params=_cp(48, ("arbitrary",)),
    )(p, p, p, p, cos, sin, gain, states, dx1, wout_e, du, daz, *deps)


def _out_even(x, ya, yb, wout):
    L = x.shape[0]
    tm = min(512, L)

    def body(x_ref, ya_ref, yb_ref, w_ref, o_ref):
        cat = jnp.concatenate([ya_ref[...], yb_ref[...]], axis=1)
        o_ref[...] = x_ref[...] + jnp.dot(cat, w_ref[...], preferred_element_type=F32)

    row = pl.BlockSpec((tm, D), lambda i: (i, 0))
    return pl.pallas_call(
        body, name="out_even", grid=(L // tm,), in_specs=[row, row, row, _full((DI, D))],
        out_specs=row, out_shape=SDS((L, D), F32), compiler_params=_cp(32, ("arbitrary",)),
    )(x, ya, yb, wout)


def _sgu_core(pv, gain, ws_ref, bs_ref):
    pu, pvv, z = pv[:, :DI], pv[:, DI:2 * DI], pv[:, 2 * DI:]
    u, gu = _gelu_and_grad(pu)
    v, gv = _gelu_and_grad(pvv)
    mu = jnp.mean(v, axis=-1, keepdims=True)
    vc = v - mu
    rstd = lax.rsqrt(jnp.mean(vc * vc, axis=-1, keepdims=True) + EPS)
    vhat = vc * rstd
    vn = vhat * gain
    t = lax.broadcasted_iota(jnp.int32, (CH, CH), 0)
    s_ = lax.broadcasted_iota(jnp.int32, (CH, CH), 1)
    mask = t >= s_
    wm = [jnp.where(mask, ws_ref[g], 0.0).astype(MXU) for g in range(SG)]
    s = jnp.concatenate([_mm(wm[g], vn[:, g * SGD:(g + 1) * SGD]) + bs_ref[g] for g in range(SG)], axis=1)
    return gu, gv, z, u, vhat, rstd, vn, mask, wm, s


def _sgu_fwd(p2, x1, gain, wsp, bsp, wout, fnorm, tgt):
    L = p2.shape[0]

    def body(p_ref, x1_ref, g_ref, ws_ref, bs_ref, wo_ref, fn_ref, t_ref, dx2_ref, gfn_ref, loss_ref):
        @pl.when(pl.program_id(0) == 0)
        def _():
            gfn_ref[...] = jnp.zeros_like(gfn_ref)
            loss_ref[...] = jnp.zeros_like(loss_ref)

        _, _, z, u, _, _, _, _, _, s = _sgu_core(p_ref[...], g_ref[...], ws_ref, bs_ref)
        act, _ = _silu_and_grad(z)
        x2 = x1_ref[...] + _mm(u * s * act, wo_ref[...])
        xhat, r = _rms(x2)
        fn = fn_ref[...]
        e = xhat * fn - t_ref[...]
        loss_ref[...] += 0.5 * jnp.sum(jnp.mean(e * e, axis=-1, keepdims=True), axis=0, keepdims=True)
        do = e * (1.0 / D)
        gfn_ref[...] += jnp.sum(do * xhat, axis=0, keepdims=True)
        dxhat = do * fn
        dx2_ref[...] = r * (dxhat - xhat * jnp.mean(dxhat * xhat, axis=-1, keepdims=True))

    row = pl.BlockSpec((CH, D), lambda i: (i, 0))
    return pl.pallas_call(
        body, name="sgu_fwd", grid=(L // CH,),
        in_specs=[pl.BlockSpec((CH, NIN), lambda i: (i, 0)), row, _full((1, DI)), _full((SG, CH, CH)),
                  _full((SG, CH, 1)), _full((DI, D)), _full((1, D)), row],
        out_specs=[row, _full((1, D)), _full((1, 1))],
        out_shape=[SDS((L, D), F32), SDS((1, D), F32), SDS((1, 1), F32)],
        compiler_params=_cp(48, ("arbitrary",)),
    )(p2, x1, gain, wsp, bsp, wout, fnorm, tgt)


def _sgu_bwd(p2, dx2, gain, wsp, bsp, wout):
    L = p2.shape[0]

    def body(p_ref, dx2_ref, g_ref, ws_ref, bs_ref, wo_ref, dp_ref, y_ref, gg_ref, gws_ref, gbs_ref):
        @pl.when(pl.program_id(0) == 0)
        def _():
            gg_ref[...] = jnp.zeros_like(gg_ref)
            gws_ref[...] = jnp.zeros_like(gws_ref)
            gbs_ref[...] = jnp.zeros_like(gbs_ref)

        gain = g_ref[...]
        gu, gv, z, u, vhat, rstd, vn, mask, wm, s = _sgu_core(p_ref[...], gain, ws_ref, bs_ref)
        act, dact = _silu_and_grad(z)
        y_ref[...] = (u * s * act).astype(y_ref.dtype)
        dy = _mm_nt(dx2_ref[...], wo_ref[...])
        du = dy * s * act
        ds = dy * u * act
        dz = dy * u * s * dact
        dvn = []
        for g in range(SG):
            ds_g = ds[:, g * SGD:(g + 1) * SGD]
            vn_g = vn[:, g * SGD:(g + 1) * SGD]
            gbs_ref[g] += jnp.sum(ds_g, axis=1, keepdims=True)
            gws_ref[g] += jnp.where(mask, _mm_nt(ds_g, vn_g), 0.0)
            dvn.append(_mm_tn(wm[g], ds_g))
        dvn = jnp.concatenate(dvn, axis=1)
        gg_ref[...] += jnp.sum(dvn * vhat, axis=0, keepdims=True)
        dvhat = dvn * gain
        dv = rstd * (dvhat - jnp.mean(dvhat, axis=-1, keepdims=True)
                     - vhat * jnp.mean(dvhat * vhat, axis=-1, keepdims=True))
        dp_ref[...] = jnp.concatenate([du * gu, dv * gv, dz], axis=1).astype(dp_ref.dtype)

    return pl.pallas_call(
        body, name="sgu_bwd", grid=(L // CH,),
        in_specs=[pl.BlockSpec((CH, NIN), lambda i: (i, 0)), pl.BlockSpec((CH, D), lambda i: (i, 0)),
                  _full((1, DI)), _full((SG, CH, CH)), _full((SG, CH, 1)), _full((DI, D))],
        out_specs=[pl.BlockSpec((CH, NIN), lambda i: (i, 0)), pl.BlockSpec((CH, DI), lambda i: (i, 0)),
                   _full((1, DI)), _full((SG, CH, CH)), _full((SG, CH, 1))],
        out_shape=[SDS((L, NIN), MXU), SDS((L, DI), MXU), SDS((1, DI), F32), SDS((SG, CH, CH), F32),
                   SDS((SG, CH, 1), F32)],
        compiler_params=_cp(48, ("arbitrary",)),
    )(p2, dx2, gain, wsp, bsp, wout)


def _my_index():
    return 4 * lax.axis_index("x") + 2 * lax.axis_index("y") + lax.axis_index("c")


def _ordered_sum(land_ref, own, me):
    g = None
    for s in range(NDEV):
        part = jnp.where(me == s, own, land_ref[s].astype(F32))
        g = part if g is None else g + part
    return g


def _adamw_math(w, m, v, g):
    mn = ADAM_B1 * m + (1.0 - ADAM_B1) * g
    vn = ADAM_B2 * v + (1.0 - ADAM_B2) * (g * g)
    mhat = mn / BC1
    vhat = vn / BC2
    return g, -ADAM_LR * (mhat / (jnp.sqrt(vhat) + ADAM_EPS) + ADAM_WD * w), mn, vn


def _adamw(w, m, v, land, own, name):
    R, C = w.shape
    tr = R
    for cand in (256, 128, 64, 32, 16, 8):
        if R % cand == 0 and R > cand:
            tr = cand
            break

    def body(w_ref, m_ref, v_ref, land_ref, own_ref, g_ref, d_ref, mo_ref, vo_ref):
        g = _ordered_sum(land_ref, own_ref[...].astype(F32), _my_index())
        for o, val in zip((g_ref, d_ref, mo_ref, vo_ref), _adamw_math(w_ref[...], m_ref[...], v_ref[...], g)):
            o[...] = val

    blk = pl.BlockSpec((tr, C), lambda i: (i, 0))
    out = SDS((R, C), F32)
    return pl.pallas_call(
        body, name=name, grid=(R // tr,),
        in_specs=[blk, blk, blk, pl.BlockSpec((NDEV, tr, C), lambda i: (0, i, 0)), blk],
        out_specs=[blk, blk, blk, blk], out_shape=[out, out, out, out],
        compiler_params=_cp(40, ("arbitrary",)),
    )(w, m, v, land, own)


def _adamw_many(ws, ms, vs, lands, owns, name):
    k = len(ws)

    def body(*refs):
        ins, outs = refs[:5 * k], refs[5 * k:]
        me = _my_index()
        for i in range(k):
            w_ref, m_ref, v_ref, land_ref, own_ref = (ins[j * k + i] for j in range(5))
            g = _ordered_sum(land_ref, own_ref[...], me)
            for j, val in enumerate(_adamw_math(w_ref[...], m_ref[...], v_ref[...], g)):
                outs[j * k + i][...] = val

    out_shape = [SDS(w.shape, F32) for _ in range(4) for w in ws]
    res = pl.pallas_call(body, name=name, out_shape=out_shape, compiler_params=_cp(60))(*ws, *ms, *vs, *lands, *owns)
    return [res[j * k:(j + 1) * k] for j in range(4)]


MESH = pl.DeviceIdType.MESH
HBM_SPEC = pl.BlockSpec(memory_space=pltpu.HBM)
SEM_SPEC = pl.BlockSpec(memory_space=pltpu.SEMAPHORE)
EFFECT = pltpu.SideEffectType.DATAFLOW_SIDE_EFFECTING


def _me_and_peers():
    x, y, c = lax.axis_index("x"), lax.axis_index("y"), lax.axis_index("c")
    me = 4 * x + 2 * y + c
    peers = []
    for r in range(1, NDEV):
        px, py, pc = x ^ ((r >> 2) & 1), y ^ ((r >> 1) & 1), c ^ (r & 1)
        peers.append(((px, py, pc), 4 * px + 2 * py + pc))
    return me, peers


def _land_shape(a, scatter):
    return (NDEV,) + (a.shape[1:] if scatter else a.shape)


def _remote(src, dst, send_sems, recv_sems, r, k, n, dev):
    i = r * n + k
    return pltpu.make_async_remote_copy(src_ref=src, dst_ref=dst, send_sem=send_sems.at[i], recv_sem=recv_sems.at[i],
                                        device_id=dev, device_id_type=MESH)


def _exchange(arrays, scatter, name):
    n = len(arrays)
    out_shape = [SDS(_land_shape(a, scatter), a.dtype) for a in arrays]

    def body(*refs):
        ins, outs = refs[:n], refs[n:2 * n]
        send_sems, recv_sems, loc_sems = refs[2 * n:]
        me, peers = _me_and_peers()
        local = []
        for k in range(n):
            src = ins[k].at[me] if scatter else ins[k]
            cp = pltpu.make_async_copy(src, outs[k].at[me], loc_sems.at[k])
            cp.start()
            local.append(cp)
        sends = []
        for r, (dev, lin) in enumerate(peers):
            for k in range(n):
                src = ins[k].at[lin] if scatter else ins[k]
                cp = _remote(src, outs[k].at[me], send_sems, recv_sems, r, k, n, dev)
                cp.start()
                sends.append(cp)
        for r, (dev, lin) in enumerate(peers):
            for k in range(n):
                src = ins[k].at[me] if scatter else ins[k]
                _remote(src, outs[k].at[lin], send_sems, recv_sems, r, k, n, dev).wait_recv()
        for cp in sends:
            cp.wait_send()
        for cp in local:
            cp.wait()

    return pl.pallas_call(
        body, name=name, in_specs=[HBM_SPEC] * n, out_specs=[HBM_SPEC] * n, out_shape=out_shape,
        scratch_shapes=[pltpu.SemaphoreType.DMA(((NDEV - 1) * n,)), pltpu.SemaphoreType.DMA(((NDEV - 1) * n,)),
                        pltpu.SemaphoreType.DMA((n,))],
    )(*arrays)


def _exchange_start(arrays, scatter, name):
    n = len(arrays)
    lands = [lax.empty(_land_shape(a, scatter), a.dtype) for a in arrays]

    def body(*refs):
        ins, lnd = refs[:n], refs[n:2 * n]
        send_sems, recv_sems, own_sems = refs[2 * n:2 * n + 3]
        token = refs[-1]
        me, peers = _me_and_peers()
        for r, (dev, lin) in enumerate(peers):
            for k in range(n):
                src = ins[k].at[lin] if scatter else ins[k]
                _remote(src, lnd[k].at[me], send_sems, recv_sems, r, k, n, dev).start()
        if not scatter:
            for k in range(n):
                pltpu.make_async_copy(ins[k], lnd[k].at[me], own_sems.at[k]).start()
        token[...] = jnp.zeros_like(token)

    sem = pltpu.SemaphoreType.DMA(((NDEV - 1) * n,))
    outs = pl.pallas_call(
        body, name=name,
        out_shape=(sem, sem, pltpu.SemaphoreType.DMA((n,)), *[pltpu.HBM(a.shape, a.dtype) for a in arrays],
                   *[pltpu.HBM(l.shape, l.dtype) for l in lands], SDS((8, 128), F32)),
        in_specs=[HBM_SPEC] * (2 * n),
        out_specs=(SEM_SPEC, SEM_SPEC, SEM_SPEC, *[HBM_SPEC] * (2 * n), pl.BlockSpec(memory_space=pltpu.VMEM)),
        input_output_aliases={k: 3 + k for k in range(2 * n)},
        compiler_params=pltpu.CompilerParams(has_side_effects=EFFECT),
    )(*[pltpu.with_memory_space_constraint(a, pltpu.HBM) for a in arrays],
      *[pltpu.with_memory_space_constraint(l, pltpu.HBM) for l in lands])
    return (n, scatter, outs[0], outs[1], outs[2], outs[3:3 + n], outs[3 + n:3 + 2 * n]), outs[-1]


def _exchange_wait(handle, after, name):
    n, scatter, send_sems, recv_sems, own_sems, thru, lands = handle
    after = tuple(after)

    def body(*refs):
        ins, lnd = refs[:n], refs[n:2 * n]
        send_sems, recv_sems, own_sems = refs[2 * n:2 * n + 3]
        me, peers = _me_and_peers()
        for r, (dev, lin) in enumerate(peers):
            for k in range(n):
                src = ins[k].at[lin] if scatter else ins[k]
                cp = _remote(src, lnd[k].at[lin], send_sems, recv_sems, r, k, n, dev)
                cp.wait_send()
                cp.wait_recv()
        if not scatter:
            for k in range(n):
                pltpu.make_async_copy(ins[k], lnd[k].at[me], own_sems.at[k]).wait()

    outs = pl.pallas_call(
        body, name=name,
        out_shape=(*[pltpu.HBM(a.shape, a.dtype) for a in thru], *[pltpu.HBM(l.shape, l.dtype) for l in lands]),
        in_specs=[HBM_SPEC] * (2 * n) + [SEM_SPEC, SEM_SPEC, SEM_SPEC] + [ANY_SPEC] * len(after),
        out_specs=tuple([HBM_SPEC] * (2 * n)),
        input_output_aliases={k: k for k in range(2 * n)},
        compiler_params=pltpu.CompilerParams(has_side_effects=EFFECT),
    )(*thru, *lands, send_sems, recv_sems, own_sems, *after)
    return list(outs[:n]), list(outs[n:])


CHIP_RELATIONS = (2, 4, 6)


def _peer(r):
    x, y, c = lax.axis_index("x"), lax.axis_index("y"), lax.axis_index("c")
    px, py, pc = x ^ ((r >> 2) & 1), y ^ ((r >> 1) & 1), c ^ (r & 1)
    return (px, py, pc), 4 * px + 2 * py + pc


def _copy(src, dst, send_sems, recv_sems, i, dev):
    return pltpu.make_async_remote_copy(src_ref=src, dst_ref=dst, send_sem=send_sems.at[i], recv_sem=recv_sems.at[i],
                                        device_id=dev, device_id_type=MESH)


def _gather2_start(a, name):
    land = lax.empty((NDEV,) + a.shape, a.dtype)

    def body(own, lnd, send_sems, recv_sems, own_sem, own_thru, lnd_thru, token):
        me = _my_index()
        for i, r in enumerate((1,) + CHIP_RELATIONS):
            dev, _ = _peer(r)
            _copy(own, lnd.at[me], send_sems, recv_sems, i, dev).start()
        pltpu.make_async_copy(own, lnd.at[me], own_sem.at[0]).start()
        token[...] = jnp.zeros_like(token)

    sem4 = pltpu.SemaphoreType.DMA((4,))
    outs = pl.pallas_call(
        body, name=name,
        out_shape=(sem4, sem4, pltpu.SemaphoreType.DMA((1,)), pltpu.HBM(a.shape, a.dtype),
                   pltpu.HBM(land.shape, land.dtype), SDS((8, 128), F32)),
        in_specs=[HBM_SPEC, HBM_SPEC],
        out_specs=(SEM_SPEC, SEM_SPEC, SEM_SPEC, HBM_SPEC, HBM_SPEC, pl.BlockSpec(memory_space=pltpu.VMEM)),
        input_output_aliases={0: 3, 1: 4},
        compiler_params=pltpu.CompilerParams(has_side_effects=EFFECT),
    )(pltpu.with_memory_space_constraint(a, pltpu.HBM), pltpu.with_memory_space_constraint(land, pltpu.HBM))
    return outs[:5], outs[5]


def _gather2_forward(handle, after, name):
    send_sems, recv_sems, own_sem, own, land = handle
    after = tuple(after)

    def body(lnd, recv_sems, *rest):
        send2, recv2, lnd_thru = rest[len(after):]
        sib, _ = _peer(1)
        for k, r in enumerate(CHIP_RELATIONS):
            dev, lin = _peer(r)
            _copy(lnd.at[lin], lnd.at[lin], recv_sems, recv_sems, 1 + k, dev).wait_recv()
            _copy(lnd.at[lin], lnd.at[lin], send2, recv2, k, sib).start()

    sem3 = pltpu.SemaphoreType.DMA((3,))
    send2, recv2, land = pl.pallas_call(
        body, name=name,
        out_shape=(sem3, sem3, pltpu.HBM(land.shape, land.dtype)),
        in_specs=[HBM_SPEC, SEM_SPEC] + [ANY_SPEC] * len(after),
        out_specs=(SEM_SPEC, SEM_SPEC, HBM_SPEC),
        input_output_aliases={0: 2},
        compiler_params=pltpu.CompilerParams(has_side_effects=EFFECT),
    )(land, recv_sems, *after)
    return send_sems, recv_sems, own_sem, send2, recv2, own, land


def _gather2_wait(handle, name):
    send_sems, recv_sems, own_sem, send2, recv2, own, land = handle

    def body(own_ref, lnd, send_sems, recv_sems, own_sem, send2, recv2, own_thru, lnd_thru):
        me = _my_index()
        sib, sib_lin = _peer(1)
        for i, r in enumerate((1,) + CHIP_RELATIONS):
            dev, _ = _peer(r)
            _copy(own_ref, lnd.at[me], send_sems, recv_sems, i, dev).wait_send()
        _copy(own_ref, lnd.at[sib_lin], send_sems, recv_sems, 0, sib).wait_recv()
        for k, r in enumerate(CHIP_RELATIONS):
            _, lin = _peer(r)
            _, lin_other = _peer(r ^ 1)
            _copy(lnd.at[lin], lnd.at[lin], send2, recv2, k, sib).wait_send()
            _copy(lnd.at[lin_other], lnd.at[lin_other], send2, recv2, k, sib).wait_recv()
        pltpu.make_async_copy(own_ref, lnd.at[me], own_sem.at[0]).wait()

    outs = pl.pallas_call(
        body, name=name,
        out_shape=(pltpu.HBM(own.shape, own.dtype), pltpu.HBM(land.shape, land.dtype)),
        in_specs=[HBM_SPEC, HBM_SPEC] + [SEM_SPEC] * 5,
        out_specs=(HBM_SPEC, HBM_SPEC),
        input_output_aliases={0: 0, 1: 1},
        compiler_params=pltpu.CompilerParams(has_side_effects=EFFECT),
    )(own, land, send_sems, recv_sems, own_sem, send2, recv2)
    return outs[1]


def _local_step(x, tgt, norm_even, first_weight, lam_re, lam_im, log_dt, b_re, b_im, c_re, c_im, s5_d, bglu,
                ret_gain, wsp, bsp, fnorm, late_weights, emit, start_token=None):
    L = x.shape[0]
    lr3, li3 = lam_re.reshape(G, 1, P), lam_im.reshape(G, 1, P)
    dt3 = log_dt.reshape(G, 1, 1)
    br3, bi3 = jnp.swapaxes(b_re, 1, 2), jnp.swapaxes(b_im, 1, 2)
    abr3, abi3, bbr3, bbi3 = _s5_disc(lr3, li3, dt3, br3, bi3)
    bb = jnp.concatenate([_embed(bbr3), _embed(bbi3)], axis=2).astype(MXU)
    cm = jnp.concatenate([_embed(jnp.swapaxes(c_re, 1, 2)), -_embed(jnp.swapaxes(c_im, 1, 2))], axis=1).astype(MXU)
    abr, abi = abr3.reshape(1, NSTATE), abi3.reshape(1, NSTATE)
    pwr, pwi, _, _ = _s5_tables(abr, abi, NT, "s5_tables_step")
    par, pai, pbr, pbi = _s5_tables(pwr[NT - 1:NT], pwi[NT - 1:NT], CH, "s5_tables_chunk")
    inv = (ROPE_BASE ** (-jnp.arange(DK // 2, dtype=F32) / (DK // 2))).reshape(1, DK // 2)
    cos, sin = _rope_tables(L, inv)
    bsp3 = bsp.reshape(SG, CH, 1)

    def dep(token):
        return NO_DEPS if token is None else (token,)

    win_e = first_weight((cos, pbi, cm))
    p = _in_proj(x, norm_even, win_e, "in_even", dep(start_token))
    ypre, s5_states = _s5_scan_fwd(p, bb, cm, abr, abi, pwr, pwi, par, pai, s5_d)
    yb, ret_states = _ret_fwd(p, cos, sin, ret_gain)
    wglu, wout_e, norm_odd, win_o, sgu_gain, wout_o = late_weights((ypre, yb))
    ya = _s5_gate_fwd(ypre, p, wglu, bglu)
    x1 = _out_even(x, ya, yb, wout_e)
    p2 = _in_proj(x1, norm_odd, win_o, "in_odd")
    dx2, g_fnorm, loss = _sgu_fwd(p2, x1, sgu_gain, wsp, bsp3, wout_o, fnorm, tgt)

    dp2, y_o, g_sgu_gain, g_wsp, g_bsp = _sgu_bwd(p2, dx2, sgu_gain, wsp, bsp3, wout_o)
    g_wout_o = _wgrad_rows([y_o], dx2, "wgrad_out_odd")
    g_win_o = _wgrad_cols(x1, norm_odd, dp2, "wgrad_in_odd")
    tok = emit("odd", dict(w_in_odd=g_win_o, w_out_odd=g_wout_o))
    dx1, g_norm_odd = _in_proj_bwd_x(dp2, x1, norm_odd, win_o, dx2, "in_odd_bwd", dep(tok))
    tok = emit("small_odd", dict(norm_odd=g_norm_odd, sgu_norm_gain=g_sgu_gain, sgu_w_spatial=g_wsp,
                                 sgu_b_spatial=g_bsp.reshape(SG, CH), final_norm=g_fnorm))

    dypre, daz, yg, dt, ya2, g_bglu = _s5_gate_bwd(ypre, p, dx1, wout_e, wglu, bglu, dep(tok))
    g_wglu = _wgrad_rows([yg], dt, "wgrad_glu")
    tok = emit("glu", dict(s5_w_glu=g_wglu))
    du, g_d, g_cm, g_bb, g_ar, g_ai = _s5_scan_bwd(p, dypre, s5_states, bb, cm, jnp.swapaxes(cm, 1, 2), abr, abi,
                                                   pwr, pwi, par, pai, pbr, pbi, s5_d, dep(tok))
    dbbr3 = _diag_blocks(g_bb[:, :, :SW], HG, P)
    dbbi3 = _diag_blocks(g_bb[:, :, SW:], HG, P)
    g_c_re = jnp.swapaxes(_diag_blocks(g_cm[:, :SW, :], P, HG), 1, 2)
    g_c_im = -jnp.swapaxes(_diag_blocks(g_cm[:, SW:, :], P, HG), 1, 2)
    g_lr3, g_li3, g_dt3, g_br3, g_bi3 = _s5_disc_bwd(
        lr3, li3, dt3, br3, bi3, g_ar.reshape(G, 1, P), g_ai.reshape(G, 1, P), dbbr3, dbbi3)
    tok = emit("small_s5", dict(
        s5_lam_re=g_lr3.reshape(G, P), s5_lam_im=g_li3.reshape(G, P), s5_log_dt=g_dt3.reshape(1, G),
        s5_b_re=g_br3, s5_b_im=g_bi3, s5_c_re=g_c_re, s5_c_im=g_c_im, s5_d=g_d, s5_b_glu=g_bglu))
    dp, yb2, g_ret_gain = _ret_bwd(p, cos, sin, ret_gain, ret_states, dx1, wout_e, du, daz, dep(tok))
    g_win_e = _wgrad_cols(x, norm_even, dp, "wgrad_in_even")
    tok = emit("even_cols", dict(w_in_even=g_win_e))
    g_wout_e = _wgrad_rows([ya2, yb2], dx1, "wgrad_out_even", dep(tok))
    tok = emit("even_rows", dict(w_out_even=g_wout_e))
    dx, g_norm_even = _in_proj_bwd_x(dp, x, norm_even, win_e, dx1, "in_even_bwd", dep(tok))
    emit("last", dict(ret_gn_gain=g_ret_gain, norm_even=g_norm_even))
    return loss, dx


WEIGHTS = ['norm_even', 'w_in_even', 's5_lam_re', 's5_lam_im', 's5_log_dt', 's5_b_re', 's5_b_im', 's5_c_re',
           's5_c_im', 's5_d', 's5_w_glu', 's5_b_glu', 'ret_gn_gain', 'w_out_even', 'norm_odd', 'w_in_odd',
           'sgu_norm_gain', 'sgu_w_spatial', 'sgu_b_spatial', 'w_out_odd', 'final_norm']
BIG = ['w_in_even', 's5_w_glu', 'w_out_even', 'w_in_odd', 'w_out_odd']
SHARDED_SMALL = {'norm_odd': D // NDEV, 'sgu_norm_gain': DI // NDEV}
SCATTER_STAGES = ("odd", "glu", "even_cols", "even_rows")
GATHER_STAGES = ("small_odd", "small_s5")


def _view(n, a):
    if n in ('s5_b_re', 's5_b_im'):
        return jnp.swapaxes(a[0], 1, 2)
    if n == 'final_norm':
        return a.reshape(1, D)
    return a[0] if a.ndim >= 3 else a


def _unview(n, t, shape):
    if n in ('s5_b_re', 's5_b_im'):
        return jnp.swapaxes(t, 1, 2)[None]
    return t.reshape(shape)


def kernel(x, norm_even, w_in_even, s5_lam_re, s5_lam_im, s5_log_dt, s5_b_re, s5_b_im, s5_c_re, s5_c_im, s5_d, s5_w_glu, s5_b_glu, ret_gn_gain, w_out_even, norm_odd, w_in_odd, sgu_norm_gain, sgu_w_spatial, sgu_b_spatial, w_out_odd, final_norm, loss_target, m_norm_even, m_w_in_even, m_s5_lam_re, m_s5_lam_im, m_s5_log_dt, m_s5_b_re, m_s5_b_im, m_s5_c_re, m_s5_c_im, m_s5_d, m_s5_w_glu, m_s5_b_glu, m_ret_gn_gain, m_w_out_even, m_norm_odd, m_w_in_odd, m_sgu_norm_gain, m_sgu_w_spatial, m_sgu_b_spatial, m_w_out_odd, m_final_norm, v_norm_even, v_w_in_even, v_s5_lam_re, v_s5_lam_im, v_s5_log_dt, v_s5_b_re, v_s5_b_im, v_s5_c_re, v_s5_c_im, v_s5_d, v_s5_w_glu, v_s5_b_glu, v_ret_gn_gain, v_w_out_even, v_norm_odd, v_w_in_odd, v_sgu_norm_gain, v_sgu_w_spatial, v_sgu_b_spatial, v_w_out_odd, v_final_norm):
    args = dict(locals())
    w = {n: args[n] for n in WEIGHTS}
    m = {n: args["m_" + n] for n in WEIGHTS}
    v = {n: args["v_" + n] for n in WEIGHTS}
    me = _my_index()

    first_handle, _ = _gather2_start(w['w_in_even'][0].astype(MXU), "gather_first_start")

    def first_weight(after):
        return _gather2_wait(_gather2_forward(first_handle, after, "gather_first_forward"), "gather_first_wait")

    late_own = [w['s5_w_glu'][0].astype(MXU), w['w_out_even'][0].astype(MXU), w['norm_odd'],
                w['w_in_odd'][0].astype(MXU), w['sgu_norm_gain'], w['w_out_odd'][0].astype(MXU)]
    late_handle, start_token = _exchange_start(late_own, False, "gather_late_start")

    def late_weights(after):
        _, (wglu, wout_e, nodd, win_o, sgug, wout_o) = _exchange_wait(late_handle, after, "gather_late_wait")
        return (wglu.reshape(D, D), wout_e.reshape(DI, D), nodd.reshape(1, D), win_o, sgug.reshape(1, DI),
                wout_o.reshape(DI, D))

    pending = {}
    small_last = {}

    def emit(stage, grads):
        if stage == "last":
            small_last.update(grads)
            return None
        names = list(grads)
        handle, token = _exchange_start([grads[n] for n in names], stage in SCATTER_STAGES, stage + "_start")
        pending[stage] = (handle, names)
        return token

    loss, dx = _local_step(
        x[0], loss_target[0], w['norm_even'], first_weight, w['s5_lam_re'][0], w['s5_lam_im'][0], w['s5_log_dt'][0],
        w['s5_b_re'][0], w['s5_b_im'][0], w['s5_c_re'][0], w['s5_c_im'][0], w['s5_d'], w['s5_b_glu'],
        w['ret_gn_gain'], w['sgu_w_spatial'][0], w['sgu_b_spatial'][0], w['final_norm'].reshape(1, D),
        late_weights, emit, start_token)

    out_g, out_d, out_m, out_v = {}, {}, {}, {}
    after = dx
    for stage in SCATTER_STAGES:
        handle, names = pending[stage]
        sent, lands = _exchange_wait(handle, (after,), stage + "_wait")
        for n, land, stack in zip(names, lands, sent):
            shp = w[n].shape
            r, c = shp[1], shp[2]
            own = lax.dynamic_index_in_dim(stack, me, 0, keepdims=False)
            res = _adamw(w[n].reshape(r, c), m[n].reshape(r, c), v[n].reshape(r, c), land, own, "adamw_" + n)
            out_g[n], out_d[n], out_m[n], out_v[n] = (t.reshape(shp) for t in res)
            after = res[0]

    names, owns, lands = [], [], []
    for stage in GATHER_STAGES:
        handle, stage_names = pending[stage]
        sent, got = _exchange_wait(handle, (after,), stage + "_wait")
        names, owns, lands = names + stage_names, owns + sent, lands + got
    last_names = list(small_last)
    last = _exchange([small_last[n] for n in last_names], False, "gather_last")
    names, owns, lands = names + last_names, owns + [small_last[n] for n in last_names], lands + list(last)
    for i, n in enumerate(names):
        if n in SHARDED_SMALL:
            width = SHARDED_SMALL[n]
            owns[i] = lax.dynamic_slice_in_dim(owns[i], me * width, width, axis=1)
            lands[i] = lax.dynamic_slice_in_dim(lands[i], me * width, width, axis=2)
    res = _adamw_many([_view(n, w[n]) for n in names], [_view(n, m[n]) for n in names],
                      [_view(n, v[n]) for n in names], lands, owns, "adamw_small")
    for dst, vals in zip((out_g, out_d, out_m, out_v), res):
        for n, t in zip(names, vals):
            dst[n] = _unview(n, t, w[n].shape)

    loss_total = lax.psum(loss[0, 0], AXES)
    return (loss_total, dx[None], *[out_g[n] for n in WEIGHTS], *[out_d[n] for n in WEIGHTS],
            *[out_m[n] for n in WEIGHTS], *[out_v[n] for n in WEIGHTS])
```

```python
import math

import jax
import jax.numpy as jnp
from jax import lax
from jax.experimental import pallas as pl
from jax.experimental.pallas import tpu as pltpu

F32 = jnp.float32
MXU = jnp.bfloat16
AXES = ("x", "y", "c")
NDEV = 8
D = 1024
NIN = 6144
WIN_BLK = NIN // NDEV
DI = 2048
G, P, HG = 64, 64, 16
GB = 8
NJ = G // GB
SW = GB * P
UW = GB * HG
NSTATE = G * P
HEADS, DK = 4, 256
CH = 128
SG, SGD = 4, 512
EPS = 1e-6
ROPE_BASE = 10000.0
VMEM_CAP_V7X = 64 * 1024 * 1024
LOG_G = [math.log1p(-2.0 ** (-5.0 - h)) for h in range(HEADS)]
GELU_C = math.sqrt(2.0 / math.pi)

ADAM_LR, ADAM_B1, ADAM_B2, ADAM_EPS, ADAM_WD, ADAM_STEP = 0.001, 0.9, 0.999, 1e-08, 0.01, 10
BC1 = 1.0 - ADAM_B1 ** ADAM_STEP
BC2 = 1.0 - ADAM_B2 ** ADAM_STEP

SDS = jax.ShapeDtypeStruct
ARB2 = ("arbitrary", "arbitrary")


def _cp(vmem_mib, sem=None):
    kw = dict(vmem_limit_bytes=min(vmem_mib * 1024 * 1024, VMEM_CAP_V7X - 4 * 1024 * 1024))
    if sem is not None:
        kw["dimension_semantics"] = sem
    return pltpu.CompilerParams(**kw)


def _mm(a, b):
    return jnp.dot(a.astype(MXU), b.astype(MXU), preferred_element_type=F32)


def _mm_nt(a, b):
    return lax.dot_general(a.astype(MXU), b.astype(MXU), (((1,), (1,)), ((), ())), preferred_element_type=F32)


def _mm_tn(a, b):
    return lax.dot_general(a.astype(MXU), b.astype(MXU), (((0,), (0,)), ((), ())), preferred_element_type=F32)


def _gelu(x):
    return _gelu_and_grad(x)[0]


def _gelu_and_grad(x):
    x2 = x * x
    th = jnp.tanh(GELU_C * x * (1.0 + 0.044715 * x2))
    hp = 0.5 * (1.0 + th)
    return x * hp, hp + 0.5 * x * (1.0 - th * th) * GELU_C * (1.0 + 3.0 * 0.044715 * x2)


def _silu_and_grad(x):
    s = jax.nn.sigmoid(x)
    return x * s, s * (1.0 + x * (1.0 - s))


def _full(shape):
    nd = len(shape)
    return pl.BlockSpec(shape, lambda *_: (0,) * nd)


def _rms(xf):
    r = lax.rsqrt(jnp.mean(xf * xf, axis=-1, keepdims=True) + EPS)
    return xf * r, r


ANY_SPEC = pl.BlockSpec(memory_space=pl.ANY)
NO_DEPS = ()


def _load_once(src_hbm, dst_vmem, sem):
    @pl.when(pl.program_id(0) == 0)
    def _():
        cp = pltpu.make_async_copy(src_hbm, dst_vmem, sem)
        cp.start()
        cp.wait()


def _lane_blocks(L):
    return SDS((NJ, L, UW), F32)


def _lane_block_spec(rows, index):
    return pl.BlockSpec((NJ, rows, UW), lambda i: (0, index(i), 0))


def _from_lane_blocks(ref):
    return jnp.concatenate([ref[j] for j in range(NJ)], axis=1)


def _to_lane_blocks(ref, v):
    for j in range(NJ):
        ref[j] = v[:, j * UW:(j + 1) * UW].astype(ref.dtype)


def _in_proj(x, gain, wst, name, deps=NO_DEPS, lane_blocks=False):
    L = x.shape[0]
    tm = min(512, L)

    def body(x_ref, g_ref, w_hbm, *rest):
        outs = rest[len(deps):]
        o_ref, w_scr, sem = outs[0], outs[-2], outs[-1]
        _load_once(w_hbm, w_scr, sem)
        xhat, _ = _rms(x_ref[...])
        h = (xhat * g_ref[...]).astype(MXU)
        for c in range(NDEV):
            o_ref[:, c * WIN_BLK:(c + 1) * WIN_BLK] = jnp.dot(h, w_scr[c], preferred_element_type=F32)
        if lane_blocks:
            _to_lane_blocks(outs[1], o_ref[:, 0:D])

    p_spec, p_shape = pl.BlockSpec((tm, NIN), lambda i: (i, 0)), SDS((L, NIN), F32)
    return pl.pallas_call(
        body, name=name, grid=(L // tm,),
        in_specs=[pl.BlockSpec((tm, D), lambda i: (i, 0)), _full((1, D)), ANY_SPEC] + [ANY_SPEC] * len(deps),
        out_specs=[p_spec, _lane_block_spec(tm, lambda i: i)] if lane_blocks else p_spec,
        out_shape=[p_shape, _lane_blocks(L)] if lane_blocks else p_shape,
        scratch_shapes=[pltpu.VMEM((NDEV, D, WIN_BLK), MXU), pltpu.SemaphoreType.DMA(())],
        compiler_params=_cp(58, ("arbitrary",)),
    )(x, gain, wst, *deps)


def _in_proj_bwd_x(dp, x, gain, wst, dres, name, deps=NO_DEPS):
    L = x.shape[0]
    tm = min(512, L)

    def body(dp_ref, x_ref, g_ref, w_hbm, dres_ref, *rest):
        dx_ref, gg_ref, w_scr, sem = rest[len(deps):]
        _load_once(w_hbm, w_scr, sem)

        @pl.when(pl.program_id(0) == 0)
        def _():
            gg_ref[...] = jnp.zeros_like(gg_ref)

        dh = _mm_nt(dp_ref[:, 0:WIN_BLK], w_scr[0])
        for c in range(1, NDEV):
            dh += _mm_nt(dp_ref[:, c * WIN_BLK:(c + 1) * WIN_BLK], w_scr[c])
        xhat, r = _rms(x_ref[...])
        dxhat = dh * g_ref[...]
        dx_ref[...] = dres_ref[...] + r * (dxhat - xhat * jnp.mean(dxhat * xhat, axis=-1, keepdims=True))
        gg_ref[...] += jnp.sum(dh * xhat, axis=0, keepdims=True)

    row = pl.BlockSpec((tm, D), lambda i: (i, 0))
    return pl.pallas_call(
        body, name=name, grid=(L // tm,),
        in_specs=[pl.BlockSpec((tm, NIN), lambda i: (i, 0)), row, _full((1, D)), ANY_SPEC, row]
        + [ANY_SPEC] * len(deps),
        out_specs=[row, _full((1, D))],
        out_shape=[SDS((L, D), F32), SDS((1, D), F32)],
        scratch_shapes=[pltpu.VMEM((NDEV, D, WIN_BLK), MXU), pltpu.SemaphoreType.DMA(())],
        compiler_params=_cp(56, ("arbitrary",)),
    )(dp, x, gain, wst, dres, *deps)


def _wgrad_cols(x, gain, dp, name, deps=NO_DEPS):
    L = x.shape[0]
    tk = min(512, L)
    nk = L // tk
    halves = 2
    nh = NDEV // halves

    def body(x_ref, g_ref, dp_ref, *rest):
        o_ref, acc = rest[len(deps):]
        k = pl.program_id(1)

        @pl.when(k == 0)
        def _():
            acc[...] = jnp.zeros_like(acc)

        xhat, _ = _rms(x_ref[...])
        acc[...] += _mm_tn(xhat * g_ref[...], dp_ref[...])

        @pl.when(k == nk - 1)
        def _():
            for c in range(nh):
                o_ref[c] = acc[:, c * WIN_BLK:(c + 1) * WIN_BLK].astype(o_ref.dtype)

    return pl.pallas_call(
        body, name=name, grid=(halves, nk),
        in_specs=[pl.BlockSpec((tk, D), lambda n, k: (k, 0)), _full((1, D)),
                  pl.BlockSpec((tk, nh * WIN_BLK), lambda n, k: (k, n))] + [ANY_SPEC] * len(deps),
        out_specs=pl.BlockSpec((nh, D, WIN_BLK), lambda n, k: (n, 0, 0)),
        out_shape=SDS((NDEV, D, WIN_BLK), MXU),
        scratch_shapes=[pltpu.VMEM((D, nh * WIN_BLK), F32)],
        compiler_params=_cp(56, ARB2),
    )(x, gain, dp, *deps)


def _wgrad_rows(a_parts, b, name, deps=NO_DEPS):
    L, N = b.shape
    na = len(a_parts)
    widths = [a.shape[1] for a in a_parts]
    M = sum(widths)
    tk = min(512, L)
    nk = L // tk

    def body(*refs):
        a_refs, b_ref = refs[:na], refs[na]
        o_ref, acc = refs[na + 1 + len(deps):]
        k = pl.program_id(0)

        @pl.when(k == 0)
        def _():
            acc[...] = jnp.zeros_like(acc)

        bv = b_ref[...].astype(MXU)
        off = 0
        for a_ref, wd in zip(a_refs, widths):
            acc[off:off + wd, :] += _mm_tn(a_ref[...], bv)
            off += wd

        @pl.when(k == nk - 1)
        def _():
            o_ref[...] = acc[...].astype(o_ref.dtype).reshape(o_ref.shape)

    return pl.pallas_call(
        body, name=name, grid=(nk,),
        in_specs=[pl.BlockSpec((tk, wd), lambda k: (k, 0)) for wd in widths]
        + [pl.BlockSpec((tk, N), lambda k: (k, 0))] + [ANY_SPEC] * len(deps),
        out_specs=_full((NDEV, M // NDEV, N)),
        out_shape=SDS((NDEV, M // NDEV, N), MXU),
        scratch_shapes=[pltpu.VMEM((M, N), F32)],
        compiler_params=_cp(48, ("arbitrary",)),
    )(*a_parts, b, *deps)


def _s5_disc_fn(lr_raw, li, logdt, br, bi):
    lr = jnp.minimum(lr_raw, -1e-4)
    dt = jnp.exp(logdt)
    mag = jnp.exp(lr * dt)
    abr = mag * jnp.cos(li * dt)
    abi = mag * jnp.sin(li * dt)
    den = lr * lr + li * li
    nre = abr - 1.0
    nim = abi
    zr = (nre * lr + nim * li) / den
    zi = (nim * lr - nre * li) / den
    return abr, abi, zr * br - zi * bi, zr * bi + zi * br


def _s5_disc(lr, li, logdt, br, bi):
    def body(lr_ref, li_ref, dt_ref, br_ref, bi_ref, abr_ref, abi_ref, bbr_ref, bbi_ref):
        abr, abi, bbr, bbi = _s5_disc_fn(lr_ref[...], li_ref[...], dt_ref[...], br_ref[...], bi_ref[...])
        abr_ref[...] = abr
        abi_ref[...] = abi
        bbr_ref[...] = bbr
        bbi_ref[...] = bbi

    s1, s3 = SDS((G, 1, P), F32), SDS((G, HG, P), F32)
    return pl.pallas_call(body, name="s5_disc", out_shape=[s1, s1, s3, s3])(lr, li, logdt, br, bi)


def _s5_disc_bwd(lr, li, logdt, br, bi, dabr, dabi, dbbr, dbbi):
    def body(lr_ref, li_ref, dt_ref, br_ref, bi_ref, c0, c1, c2, c3, o0, o1, o2, o3, o4):
        _, vjp = jax.vjp(_s5_disc_fn, lr_ref[...], li_ref[...], dt_ref[...], br_ref[...], bi_ref[...])
        g = vjp((c0[...], c1[...], c2[...], c3[...]))
        for o, v in zip((o0, o1, o2, o3, o4), g):
            o[...] = v

    s1, s3 = SDS((G, 1, P), F32), SDS((G, HG, P), F32)
    return pl.pallas_call(body, name="s5_disc_bwd", out_shape=[s1, s1, SDS((G, 1, 1), F32), s3, s3])(
        lr, li, logdt, br, bi, dabr, dabi, dbbr, dbbi)


def _s5_tables(abr, abi, rows, name):
    def body(ar_ref, ai_ref, pfr, pfi, pbr, pbi):
        pfr[0:1, :] = ar_ref[...]
        pfi[0:1, :] = ai_ref[...]
        pbr[rows - 1:rows, :] = ar_ref[...]
        pbi[rows - 1:rows, :] = ai_ref[...]
        n = 1
        while n < rows:
            er, ei = pfr[n - 1:n, :], pfi[n - 1:n, :]
            xr, xi = pfr[0:n, :], pfi[0:n, :]
            pfr[n:2 * n, :] = er * xr - ei * xi
            pfi[n:2 * n, :] = er * xi + ei * xr
            yr, yi = pbr[rows - n:rows, :], pbi[rows - n:rows, :]
            pbr[rows - 2 * n:rows - n, :] = er * yr - ei * yi
            pbi[rows - 2 * n:rows - n, :] = er * yi + ei * yr
            n *= 2

    s = SDS((rows, NSTATE), F32)
    return pl.pallas_call(body, name=name, out_shape=[s, s, s, s], compiler_params=_cp(40))(abr, abi)


def _cscan(br, bi, pr_ref, pi_ref, reverse):
    T = br.shape[0]
    sign = -1.0 if reverse else 1.0
    row = lax.broadcasted_iota(jnp.int32, br.shape, 0)
    k = 1
    while k < T:
        akr = pr_ref[k - 1:k, :]
        aki = sign * pi_ref[k - 1:k, :]

        def shift(v):
            if k % 8 == 0:
                z = jnp.zeros((k, v.shape[1]), v.dtype)
                return jnp.concatenate([v[k:], z], 0) if reverse else jnp.concatenate([z, v[:T - k]], 0)
            if reverse:
                return jnp.where(row < T - k, pltpu.roll(v, T - k, 0), 0.0)
            return jnp.where(row >= k, pltpu.roll(v, k, 0), 0.0)

        sr, si = shift(br), shift(bi)
        br, bi = br + akr * sr - aki * si, bi + akr * si + aki * sr
        k *= 2
    return br, bi


def _embed(t):
    a, b = t.shape[1], t.shape[2]
    return jnp.einsum("jgab,gh->jgahb", t.reshape(NJ, GB, a, b), jnp.eye(GB, dtype=t.dtype)).reshape(NJ, GB * a, GB * b)


def _diag_blocks(t, a, b):
    return jnp.einsum("jgahb,gh->jgab", t.reshape(NJ, GB, a, GB, b), jnp.eye(GB, dtype=t.dtype)).reshape(G, a, b)


NT = 16


def _chunks(L):
    ncb = min(CH, L // NT)
    return ncb, NT * ncb


def _cmul_add(ar, ai, xr, xi, br, bi):
    return ar * xr - ai * xi + br, ar * xi + ai * xr + bi


def _pow_weights(w_ref, pwr_ref, pwi_ref, dst, adjoint):
    w = w_ref[...].astype(F32)
    wr, wi = w[:, :SW], w[:, SW:]
    for t in range(NT):
        k = t if adjoint else NT - 1 - t
        if k == 0:
            blk = w
        else:
            pr, pi = pwr_ref[k - 1:k, :], pwi_ref[k - 1:k, :]
            if adjoint:
                blk = jnp.concatenate([pr * wr + pi * wi, pr * wi - pi * wr], axis=1)
            else:
                blk = jnp.concatenate([pr * wr - pi * wi, pr * wi + pi * wr], axis=1)
        dst[t * UW:(t + 1) * UW, :] = blk.astype(dst.dtype)


def _s5_states(u_ref, bb_ref, bbp_scr, par_ref, pai_ref, cr, ci, ncb, bu_scr):
    us = [u_ref[pl.ds(t, ncb, stride=NT), :] for t in range(NT)]
    for t in range(NT):
        bu_scr[t] = _mm(us[t], bb_ref[...])
    e = _mm(jnp.concatenate(us, axis=1), bbp_scr[...])
    xr, xi = _cscan(e[:, :SW], e[:, SW:], par_ref, pai_ref, False)
    fr, fi = _cmul_add(par_ref[0:ncb, :], pai_ref[0:ncb, :], cr, ci, xr, xi)
    row = lax.broadcasted_iota(jnp.int32, fr.shape, 0)
    cinr = jnp.where(row >= 1, pltpu.roll(fr, 1, 0), cr)
    cini = jnp.where(row >= 1, pltpu.roll(fi, 1, 0), ci)
    return cinr, cini, jnp.concatenate([fr[ncb - 1:ncb, :], fi[ncb - 1:ncb, :]], axis=1)


def _s5_scan_fwd(u3, bb, cm, abr, abi, pwr, pwi, par, pai, dskip):
    L = u3.shape[1]
    ncb, tb = _chunks(L)
    nb = L // tb

    def body(u_ref, bb_ref, cm_ref, ar_ref, ai_ref, pwr_ref, pwi_ref, par_ref, pai_ref, d_ref, ypre_ref, st_ref,
             carry, bu_scr, bbp_scr):
        @pl.when(pl.program_id(1) == 0)
        def _():
            carry[...] = jnp.zeros_like(carry)
            _pow_weights(bb_ref, pwr_ref, pwi_ref, bbp_scr, False)

        c = carry[...]
        st_ref[...] = c
        ar, ai = ar_ref[...], ai_ref[...]
        sr, si, cnext = _s5_states(u_ref, bb_ref, bbp_scr, par_ref, pai_ref, c[:, :SW], c[:, SW:], ncb, bu_scr)
        carry[...] = cnext
        for t in range(NT):
            bu = bu_scr[t]
            sr, si = _cmul_add(ar, ai, sr, si, bu[:, :SW], bu[:, SW:])
            rows = pl.ds(t, ncb, stride=NT)
            ypre_ref[rows, :] = _mm(jnp.concatenate([sr, si], axis=1), cm_ref[...]) + d_ref[...] * u_ref[rows, :]

    tab = pl.BlockSpec((CH, SW), lambda j, i: (0, j))
    stp = pl.BlockSpec((NT, SW), lambda j, i: (0, j))
    vec = lambda w: pl.BlockSpec((1, w), lambda j, i: (0, j))
    return pl.pallas_call(
        body, name="s5_scan_fwd", grid=(NJ, nb),
        in_specs=[pl.BlockSpec((None, tb, UW), lambda j, i: (j, i, 0)),
                  pl.BlockSpec((None, UW, 2 * SW), lambda j, i: (j, 0, 0)),
                  pl.BlockSpec((None, 2 * SW, UW), lambda j, i: (j, 0, 0)),
                  vec(SW), vec(SW), stp, stp, tab, tab, vec(UW)],
        out_specs=[pl.BlockSpec((None, tb, UW), lambda j, i: (j, i, 0)),
                   pl.BlockSpec((None, None, 1, 2 * SW), lambda j, i: (j, i, 0, 0))],
        out_shape=[_lane_blocks(L), SDS((NJ, nb, 1, 2 * SW), F32)],
        scratch_shapes=[pltpu.VMEM((1, 2 * SW), F32), pltpu.VMEM((NT, ncb, 2 * SW), F32),
                        pltpu.VMEM((NT * UW, 2 * SW), MXU)],
        compiler_params=_cp(48, ARB2),
    )(u3, bb, cm, abr, abi, pwr, pwi, par, pai, dskip)


def _s5_gate_fwd(ypre3, p, wglu, bglu):
    L = p.shape[0]
    tm = min(256, L)

    def body(y_ref, az_ref, wg_ref, bg_ref, ya_ref):
        yg = _gelu(_from_lane_blocks(y_ref))
        t = _mm(yg, wg_ref[...]) + bg_ref[...]
        act, _ = _silu_and_grad(az_ref[...])
        ya_ref[...] = (yg * jax.nn.sigmoid(t) * act).astype(ya_ref.dtype)

    return pl.pallas_call(
        body, name="s5_gate_fwd", grid=(L // tm,),
        in_specs=[_lane_block_spec(tm, lambda i: i), pl.BlockSpec((tm, D), lambda i: (i, 1)),
                  _full((D, D)), _full((1, D))],
        out_specs=pl.BlockSpec((tm, D), lambda i: (i, 0)),
        out_shape=SDS((L, D), MXU),
        compiler_params=_cp(32, ("arbitrary",)),
    )(ypre3, p, wglu, bglu)


def _s5_gate_bwd(ypre3, p, dx1, wout_e, wglu, bglu, deps=NO_DEPS):
    L = p.shape[0]
    tm = min(256, L)

    def body(y_ref, az_ref, dx1_ref, wo_ref, wg_ref, bg_ref, *rest):
        dyp_ref, daz_ref, yg_ref, dt_ref, ya_ref, gbg_ref = rest[len(deps):]

        @pl.when(pl.program_id(0) == 0)
        def _():
            gbg_ref[...] = jnp.zeros_like(gbg_ref)

        yg, dgelu = _gelu_and_grad(_from_lane_blocks(y_ref))
        sg = jax.nn.sigmoid(_mm(yg, wg_ref[...]) + bg_ref[...])
        act, dact = _silu_and_grad(az_ref[...])
        y2 = yg * sg
        dya = _mm_nt(dx1_ref[...], wo_ref[...])
        daz_ref[...] = (dya * y2 * dact).astype(daz_ref.dtype)
        dy2 = dya * act
        dt = dy2 * yg * sg * (1.0 - sg)
        dyg = dy2 * sg + _mm_nt(dt, wg_ref[...])
        _to_lane_blocks(dyp_ref, dyg * dgelu)
        yg_ref[...] = yg.astype(yg_ref.dtype)
        dt_ref[...] = dt.astype(dt_ref.dtype)
        ya_ref[...] = (y2 * act).astype(ya_ref.dtype)
        gbg_ref[...] += jnp.sum(dt, axis=0, keepdims=True)

    row = pl.BlockSpec((tm, D), lambda i: (i, 0))
    return pl.pallas_call(
        body, name="s5_gate_bwd", grid=(L // tm,),
        in_specs=[_lane_block_spec(tm, lambda i: i), pl.BlockSpec((tm, D), lambda i: (i, 1)), row,
                  pl.BlockSpec((D, D), lambda i: (0, 0)), _full((D, D)), _full((1, D))] + [ANY_SPEC] * len(deps),
        out_specs=[_lane_block_spec(tm, lambda i: i), row, row, row, row, _full((1, D))],
        out_shape=[_lane_blocks(L), SDS((L, D), MXU), SDS((L, D), MXU), SDS((L, D), MXU), SDS((L, D), MXU),
                   SDS((1, D), F32)],
        compiler_params=_cp(40, ("arbitrary",)),
    )(ypre3, p, dx1, wout_e, wglu, bglu, *deps)


def _s5_scan_bwd(u3, dypre3, states, bb, cm, cmt, abr, abi, pwr, pwi, par, pai, pbr, pbi, dskip, deps=NO_DEPS):
    L = u3.shape[1]
    ncb, tb = _chunks(L)
    nb = L // tb
    rev = lambda i: nb - 1 - i

    def body(u_ref, dy_ref, st_ref, bb_ref, cm_ref, cmt_ref, ar_ref, ai_ref, pwr_ref, pwi_ref, par_ref, pai_ref,
             pbr_ref, pbi_ref, d_ref, *rest):
        (du_ref, gd_ref, gcm_ref, gbb_ref, gar_ref, gai_ref,
         lcarry, bu_scr, s_scr, gs_scr, bbp_scr, cmp_scr) = rest[len(deps):]
        del cm_ref

        @pl.when(pl.program_id(1) == 0)
        def _():
            _pow_weights(bb_ref, pwr_ref, pwi_ref, bbp_scr, False)
            _pow_weights(cmt_ref, pwr_ref, pwi_ref, cmp_scr, True)
            lcarry[...] = jnp.zeros_like(lcarry)
            gd_ref[...] = jnp.zeros_like(gd_ref)
            gcm_ref[...] = jnp.zeros_like(gcm_ref)
            gbb_ref[...] = jnp.zeros_like(gbb_ref)
            gar_ref[...] = jnp.zeros_like(gar_ref)
            gai_ref[...] = jnp.zeros_like(gai_ref)

        ar, ai = ar_ref[...], ai_ref[...]
        c = st_ref[...]
        sr, si, _ = _s5_states(u_ref, bb_ref, bbp_scr, par_ref, pai_ref, c[:, :SW], c[:, SW:], ncb, bu_scr)
        s_scr[0] = jnp.concatenate([sr, si], axis=1)
        for t in range(NT):
            bu = bu_scr[t]
            sr, si = _cmul_add(ar, ai, sr, si, bu[:, :SW], bu[:, SW:])
            s_scr[t + 1] = jnp.concatenate([sr, si], axis=1)
        dys = [dy_ref[pl.ds(t, ncb, stride=NT), :] for t in range(NT)]
        for t in range(NT):
            gs_scr[t] = _mm(dys[t], cmt_ref[...])
        f = _mm(jnp.concatenate(dys, axis=1), cmp_scr[...])
        xr, xi = _cscan(f[:, :SW], f[:, SW:], par_ref, pai_ref, True)
        lc = lcarry[...]
        lcr, lci = lc[:, :SW], lc[:, SW:]
        hr, hi = _cmul_add(pbr_ref[CH - ncb:CH, :], -pbi_ref[CH - ncb:CH, :], lcr, lci, xr, xi)
        lcarry[...] = jnp.concatenate([hr[0:1, :], hi[0:1, :]], axis=1)
        row = lax.broadcasted_iota(jnp.int32, hr.shape, 0)
        lr_ = jnp.where(row < ncb - 1, pltpu.roll(hr, ncb - 1, 0), lcr)
        li_ = jnp.where(row < ncb - 1, pltpu.roll(hi, ncb - 1, 0), lci)
        gar = jnp.zeros((1, SW), F32)
        gai = jnp.zeros((1, SW), F32)
        for t in reversed(range(NT)):
            gs = gs_scr[t]
            lr_, li_ = _cmul_add(ar, -ai, lr_, li_, gs[:, :SW], gs[:, SW:])
            rows = pl.ds(t, ncb, stride=NT)
            u_t, dy_t = u_ref[rows, :], dy_ref[rows, :]
            lam = jnp.concatenate([lr_, li_], axis=1)
            gbb_ref[...] += _mm_tn(u_t, lam)
            du_ref[rows, :] = _mm_nt(lam, bb_ref[...]) + dy_t * d_ref[...]
            gd_ref[...] += jnp.sum(dy_t * u_t, axis=0, keepdims=True)
            gcm_ref[...] += _mm_tn(s_scr[t + 1], dy_t)
            sp = s_scr[t]
            spr, spi = sp[:, :SW], sp[:, SW:]
            gar += jnp.sum(lr_ * spr + li_ * spi, axis=0, keepdims=True)
            gai += jnp.sum(li_ * spr - lr_ * spi, axis=0, keepdims=True)
        gar_ref[...] += gar
        gai_ref[...] += gai

    tab = pl.BlockSpec((CH, SW), lambda j, i: (0, j))
    stp = pl.BlockSpec((NT, SW), lambda j, i: (0, j))
    colblk = pl.BlockSpec((None, tb, UW), lambda j, i: (j, rev(i), 0))
    vec = lambda w: pl.BlockSpec((1, w), lambda j, i: (0, j))
    return pl.pallas_call(
        body, name="s5_scan_bwd", grid=(NJ, nb),
        in_specs=[colblk, colblk,
                  pl.BlockSpec((None, None, 1, 2 * SW), lambda j, i: (j, rev(i), 0, 0)),
                  pl.BlockSpec((None, UW, 2 * SW), lambda j, i: (j, 0, 0)),
                  pl.BlockSpec((None, 2 * SW, UW), lambda j, i: (j, 0, 0)),
                  pl.BlockSpec((None, UW, 2 * SW), lambda j, i: (j, 0, 0)),
                  vec(SW), vec(SW), stp, stp, tab, tab, tab, tab, vec(UW)] + [ANY_SPEC] * len(deps),
        out_specs=[colblk, vec(UW),
                   pl.BlockSpec((None, 2 * SW, UW), lambda j, i: (j, 0, 0)),
                   pl.BlockSpec((None, UW, 2 * SW), lambda j, i: (j, 0, 0)),
                   vec(SW), vec(SW)],
        out_shape=[_lane_blocks(L), SDS((1, D), F32),
                   SDS((NJ, 2 * SW, UW), F32), SDS((NJ, UW, 2 * SW), F32),
                   SDS((1, NSTATE), F32), SDS((1, NSTATE), F32)],
        scratch_shapes=[pltpu.VMEM((1, 2 * SW), F32), pltpu.VMEM((NT, ncb, 2 * SW), F32),
                        pltpu.VMEM((NT + 1, ncb, 2 * SW), F32), pltpu.VMEM((NT, ncb, 2 * SW), F32),
                        pltpu.VMEM((NT * UW, 2 * SW), MXU), pltpu.VMEM((NT * UW, 2 * SW), MXU)],
        compiler_params=_cp(56, ARB2),
    )(u3, dypre3, states, bb, cm, cmt, abr, abi, pwr, pwi, par, pai, pbr, pbi, dskip, *deps)


def _rope_tables(L, inv):
    tm = min(512, L)

    def body(inv_ref, cos_ref, sin_ref):
        pos = (lax.broadcasted_iota(jnp.int32, (tm, DK // 2), 0) + pl.program_id(0) * tm).astype(F32)
        ang = pos * inv_ref[...]
        cos_ref[...] = jnp.cos(ang)
        sin_ref[...] = jnp.sin(ang)

    blk = pl.BlockSpec((tm, DK // 2), lambda i: (i, 0))
    return pl.pallas_call(body, name="rope_tables", grid=(L // tm,), in_specs=[_full((1, DK // 2))],
                          out_specs=[blk, blk], out_shape=[SDS((L, DK // 2), F32)] * 2)(inv)


def _rot(x, cos, sin):
    x1, x2 = x[:, :DK // 2], x[:, DK // 2:]
    return jnp.concatenate([x1 * cos - x2 * sin, x1 * sin + x2 * cos], axis=1)


def _unrot(d, cos, sin):
    d1, d2 = d[:, :DK // 2], d[:, DK // 2:]
    return jnp.concatenate([d1 * cos + d2 * sin, d2 * cos - d1 * sin], axis=1)


def _ret_decays(h):
    lg = LOG_G[h]
    n = lax.broadcasted_iota(jnp.int32, (CH, CH), 0)
    m = lax.broadcasted_iota(jnp.int32, (CH, CH), 1)
    diff = (n - m).astype(F32)
    decay = jnp.where(n >= m, jnp.exp(lg * jnp.maximum(diff, 0.0)), 0.0)
    idx = lax.broadcasted_iota(jnp.int32, (CH, 1), 0).astype(F32)
    xi = jnp.exp(lg * (idx + 1.0))
    zeta = jnp.exp(lg * (CH - 1.0 - idx))
    return decay, xi, zeta, math.exp(lg * CH)


def _ret_tables(dec_scr, vec_scr):
    for h in range(HEADS):
        decay, xi, zeta, _ = _ret_decays(h)
        dec_scr[h] = decay
        vec_scr[h] = jnp.concatenate([jnp.broadcast_to(xi, (CH, 128)), jnp.broadcast_to(zeta, (CH, 128))], axis=1)


def _ret_chunk_fwd(q, k, v, cos, sin, s_prev_b, decay, xi, zeta):
    qr = _rot(q, cos, sin)
    kr = _rot(k, cos, sin) * (DK ** -0.5)
    scores = _mm_nt(qr, kr) * decay
    o = _mm(scores, v) + _mm(qr * xi, s_prev_b)
    local = _mm_tn(kr * zeta, v)
    mu = jnp.mean(o, axis=-1, keepdims=True)
    oc = o - mu
    rstd = lax.rsqrt(jnp.mean(oc * oc, axis=-1, keepdims=True) + EPS)
    return qr, kr, scores, local, oc * rstd, rstd


def _ret_fwd(p, cos, sin, gain):
    L = p.shape[0]
    nb = L // CH

    def body(q_ref, k_ref, v_ref, bz_ref, cos_ref, sin_ref, g_ref, yb_ref, st_ref, state, dec_scr, vec_scr):
        @pl.when(pl.program_id(0) == 0)
        def _():
            state[...] = jnp.zeros_like(state)
            _ret_tables(dec_scr, vec_scr)

        cos, sin = cos_ref[...], sin_ref[...]
        act, _ = _silu_and_grad(bz_ref[...])
        for h in range(HEADS):
            hs = slice(h * DK, (h + 1) * DK)
            xi, zeta = vec_scr[h, :, 0:1], vec_scr[h, :, 128:129]
            s_prev = state[h]
            s_prev_b = s_prev.astype(MXU)
            st_ref[h] = s_prev_b
            _, _, _, local, on, _ = _ret_chunk_fwd(q_ref[:, hs], k_ref[:, hs], v_ref[:, hs], cos, sin,
                                                   s_prev_b, dec_scr[h], xi, zeta)
            state[h] = s_prev * math.exp(LOG_G[h] * CH) + local
            yb_ref[:, hs] = (on * g_ref[:, hs] * act[:, hs]).astype(yb_ref.dtype)

    col = lambda c: pl.BlockSpec((CH, D), lambda i: (i, c))
    rope = pl.BlockSpec((CH, DK // 2), lambda i: (i, 0))
    return pl.pallas_call(
        body, name="ret_fwd", grid=(nb,),
        in_specs=[col(2), col(3), col(4), col(5), rope, rope, _full((1, D))],
        out_specs=[pl.BlockSpec((CH, D), lambda i: (i, 0)),
                   pl.BlockSpec((None, HEADS, DK, DK), lambda i: (i, 0, 0, 0))],
        out_shape=[SDS((L, D), MXU), SDS((nb, HEADS, DK, DK), MXU)],
        scratch_shapes=[pltpu.VMEM((HEADS, DK, DK), F32), pltpu.VMEM((HEADS, CH, CH), F32),
                        pltpu.VMEM((HEADS, CH, 256), F32)],
        compiler_params=_cp(40, ("arbitrary",)),
    )(p, p, p, p, cos, sin, gain)


def _ret_bwd(p, cos, sin, gain, states, dx1, wout_e, du, daz, deps=NO_DEPS):
    L = p.shape[0]
    nb = L // CH
    rev = lambda i: nb - 1 - i

    def body(q_ref, k_ref, v_ref, bz_ref, cos_ref, sin_ref, g_ref, st_ref, dx1_ref, wo_ref, du_ref, daz_ref, *rest):
        dp_ref, yb_ref, gg_ref, gstate, dec_scr, vec_scr = rest[len(deps):]

        @pl.when(pl.program_id(0) == 0)
        def _():
            gstate[...] = jnp.zeros_like(gstate)
            gg_ref[...] = jnp.zeros_like(gg_ref)
            _ret_tables(dec_scr, vec_scr)

        cos, sin = cos_ref[...], sin_ref[...]
        act, dact = _silu_and_grad(bz_ref[...])
        dyb = _mm_nt(dx1_ref[...], wo_ref[...])
        dp_ref[:, 0:D] = _from_lane_blocks(du_ref).astype(dp_ref.dtype)
        dp_ref[:, D:2 * D] = daz_ref[...]
        for h in range(HEADS):
            hs = slice(h * DK, (h + 1) * DK)
            col = lambda part: slice((2 + part) * D + h * DK, (2 + part) * D + (h + 1) * DK)
            decay = dec_scr[h]
            xi, zeta = vec_scr[h, :, 0:1], vec_scr[h, :, 128:129]
            v = v_ref[:, hs]
            s_prev_b = st_ref[h]
            qr, kr, scores, _, on, rstd = _ret_chunk_fwd(q_ref[:, hs], k_ref[:, hs], v, cos, sin, s_prev_b,
                                                         decay, xi, zeta)
            gain_h = g_ref[:, hs]
            out = on * gain_h
            yb_ref[:, hs] = (out * act[:, hs]).astype(yb_ref.dtype)
            dyb_h = dyb[:, hs]
            dp_ref[:, col(3)] = (dyb_h * out * dact[:, hs]).astype(dp_ref.dtype)
            dout = dyb_h * act[:, hs]
            gg_ref[:, hs] += jnp.sum(dout * on, axis=0, keepdims=True)
            don = dout * gain_h
            do = rstd * (don - jnp.mean(don, axis=-1, keepdims=True)
                         - on * jnp.mean(don * on, axis=-1, keepdims=True))
            gnext = gstate[h]
            gnext_b = gnext.astype(MXU)
            dscores = _mm_nt(do, v) * decay
            dp_ref[:, col(2)] = (_mm_tn(scores, do) + _mm(kr * zeta, gnext_b)).astype(dp_ref.dtype)
            dqr = _mm(dscores, kr) + _mm_nt(do, s_prev_b) * xi
            dkr = _mm_tn(dscores, qr) + _mm_nt(v, gnext_b) * zeta
            gstate[h] = gnext * math.exp(LOG_G[h] * CH) + _mm_tn(qr * xi, do)
            dp_ref[:, col(0)] = _unrot(dqr, cos, sin).astype(dp_ref.dtype)
            dp_ref[:, col(1)] = (_unrot(dkr, cos, sin) * (DK ** -0.5)).astype(dp_ref.dtype)

    col = lambda c: pl.BlockSpec((CH, D), lambda i: (rev(i), c))
    rope = pl.BlockSpec((CH, DK // 2), lambda i: (rev(i), 0))
    outc = col(0)
    act_out = SDS((L, D), MXU)
    return pl.pallas_call(
        body, name="ret_bwd", grid=(nb,),
        in_specs=[col(2), col(3), col(4), col(5), rope, rope, _full((1, D)),
                  pl.BlockSpec((None, HEADS, DK, DK), lambda i: (rev(i), 0, 0, 0)),
                  outc, pl.BlockSpec((D, D), lambda i: (1, 0)), _lane_block_spec(CH, rev), outc]
        + [ANY_SPEC] * len(deps),
        out_specs=[pl.BlockSpec((CH, NIN), lambda i: (rev(i), 0)), outc, _full((1, D))],
        out_shape=[SDS((L, NIN), MXU), act_out, SDS((1, D), F32)],
        scratch_shapes=[pltpu.VMEM((HEADS, DK, DK), F32), pltpu.VMEM((HEADS, CH, CH), F32),
                        pltpu.VMEM((HEADS, CH, 256), F32)],
        compiler_params=_cp(48, ("arbitrary",)),
    )(p, p, p, p, cos, sin, gain, states, dx1, wout_e, du, daz, *deps)


def _out_even(x, ya, yb, wout):
    L = x.shape[0]
    tm = min(512, L)

    def body(x_ref, ya_ref, yb_ref, w_ref, o_ref):
        cat = jnp.concatenate([ya_ref[...], yb_ref[...]], axis=1)
        o_ref[...] = x_ref[...] + jnp.dot(cat, w_ref[...], preferred_element_type=F32)

    row = pl.BlockSpec((tm, D), lambda i: (i, 0))
    return pl.pallas_call(
        body, name="out_even", grid=(L // tm,), in_specs=[row, row, row, _full((DI, D))],
        out_specs=row, out_shape=SDS((L, D), F32), compiler_params=_cp(32, ("arbitrary",)),
    )(x, ya, yb, wout)


def _sgu_core(pv, gain, ws_ref, bs_ref):
    pu, pvv, z = pv[:, :DI], pv[:, DI:2 * DI], pv[:, 2 * DI:]
    u, gu = _gelu_and_grad(pu)
    v, gv = _gelu_and_grad(pvv)
    mu = jnp.mean(v, axis=-1, keepdims=True)
    vc = v - mu
    rstd = lax.rsqrt(jnp.mean(vc * vc, axis=-1, keepdims=True) + EPS)
    vhat = vc * rstd
    vn = vhat * gain
    t = lax.broadcasted_iota(jnp.int32, (CH, CH), 0)
    s_ = lax.broadcasted_iota(jnp.int32, (CH, CH), 1)
    mask = t >= s_
    wm = [jnp.where(mask, ws_ref[g], 0.0).astype(MXU) for g in range(SG)]
    s = jnp.concatenate([_mm(wm[g], vn[:, g * SGD:(g + 1) * SGD]) + bs_ref[g] for g in range(SG)], axis=1)
    return gu, gv, z, u, vhat, rstd, vn, mask, wm, s


def _sgu_fwd(p2, x1, gain, wsp, bsp, wout, fnorm, tgt):
    L = p2.shape[0]

    def body(p_ref, x1_ref, g_ref, ws_ref, bs_ref, wo_ref, fn_ref, t_ref, dx2_ref, gfn_ref, loss_ref):
        @pl.when(pl.program_id(0) == 0)
        def _():
            gfn_ref[...] = jnp.zeros_like(gfn_ref)
            loss_ref[...] = jnp.zeros_like(loss_ref)

        _, _, z, u, _, _, _, _, _, s = _sgu_core(p_ref[...], g_ref[...], ws_ref, bs_ref)
        act, _ = _silu_and_grad(z)
        x2 = x1_ref[...] + _mm(u * s * act, wo_ref[...])
        xhat, r = _rms(x2)
        fn = fn_ref[...]
        e = xhat * fn - t_ref[...]
        loss_ref[...] += 0.5 * jnp.sum(jnp.mean(e * e, axis=-1, keepdims=True), axis=0, keepdims=True)
        do = e * (1.0 / D)
        gfn_ref[...] += jnp.sum(do * xhat, axis=0, keepdims=True)
        dxhat = do * fn
        dx2_ref[...] = r * (dxhat - xhat * jnp.mean(dxhat * xhat, axis=-1, keepdims=True))

    row = pl.BlockSpec((CH, D), lambda i: (i, 0))
    return pl.pallas_call(
        body, name="sgu_fwd", grid=(L // CH,),
        in_specs=[pl.BlockSpec((CH, NIN), lambda i: (i, 0)), row, _full((1, DI)), _full((SG, CH, CH)),
                  _full((SG, CH, 1)), _full((DI, D)), _full((1, D)), row],
        out_specs=[row, _full((1, D)), _full((1, 1))],
        out_shape=[SDS((L, D), F32), SDS((1, D), F32), SDS((1, 1), F32)],
        compiler_params=_cp(48, ("arbitrary",)),
    )(p2, x1, gain, wsp, bsp, wout, fnorm, tgt)


def _sgu_bwd(p2, dx2, gain, wsp, bsp, wout):
    L = p2.shape[0]

    def body(p_ref, dx2_ref, g_ref, ws_ref, bs_ref, wo_ref, dp_ref, y_ref, gg_ref, gws_ref, gbs_ref):
        @pl.when(pl.program_id(0) == 0)
        def _():
            gg_ref[...] = jnp.zeros_like(gg_ref)
            gws_ref[...] = jnp.zeros_like(gws_ref)
            gbs_ref[...] = jnp.zeros_like(gbs_ref)

        gain = g_ref[...]
        gu, gv, z, u, vhat, rstd, vn, mask, wm, s = _sgu_core(p_ref[...], gain, ws_ref, bs_ref)
        act, dact = _silu_and_grad(z)
        y_ref[...] = (u * s * act).astype(y_ref.dtype)
        dy = _mm_nt(dx2_ref[...], wo_ref[...])
        du = dy * s * act
        ds = dy * u * act
        dz = dy * u * s * dact
        dvn = []
        for g in range(SG):
            ds_g = ds[:, g * SGD:(g + 1) * SGD]
            vn_g = vn[:, g * SGD:(g + 1) * SGD]
            gbs_ref[g] += jnp.sum(ds_g, axis=1, keepdims=True)
            gws_ref[g] += jnp.where(mask, _mm_nt(ds_g, vn_g), 0.0)
            dvn.append(_mm_tn(wm[g], ds_g))
        dvn = jnp.concatenate(dvn, axis=1)
        gg_ref[...] += jnp.sum(dvn * vhat, axis=0, keepdims=True)
        dvhat = dvn * gain
        dv = rstd * (dvhat - jnp.mean(dvhat, axis=-1, keepdims=True)
                     - vhat * jnp.mean(dvhat * vhat, axis=-1, keepdims=True))
        dp_ref[...] = jnp.concatenate([du * gu, dv * gv, dz], axis=1).astype(dp_ref.dtype)

    return pl.pallas_call(
        body, name="sgu_bwd", grid=(L // CH,),
        in_specs=[pl.BlockSpec((CH, NIN), lambda i: (i, 0)), pl.BlockSpec((CH, D), lambda i: (i, 0)),
                  _full((1, DI)), _full((SG, CH, CH)), _full((SG, CH, 1)), _full((DI, D))],
        out_specs=[pl.BlockSpec((CH, NIN), lambda i: (i, 0)), pl.BlockSpec((CH, DI), lambda i: (i, 0)),
                   _full((1, DI)), _full((SG, CH, CH)), _full((SG, CH, 1))],
        out_shape=[SDS((L, NIN), MXU), SDS((L, DI), MXU), SDS((1, DI), F32), SDS((SG, CH, CH), F32),
                   SDS((SG, CH, 1), F32)],
        compiler_params=_cp(48, ("arbitrary",)),
    )(p2, dx2, gain, wsp, bsp, wout)


def _my_index():
    return 4 * lax.axis_index("x") + 2 * lax.axis_index("y") + lax.axis_index("c")


def _ordered_sum(land_ref, own, me):
    g = None
    for s in range(NDEV):
        part = jnp.where(me == s, own, land_ref[s].astype(F32))
        g = part if g is None else g + part
    return g


def _adamw_math(w, m, v, g):
    mn = ADAM_B1 * m + (1.0 - ADAM_B1) * g
    vn = ADAM_B2 * v + (1.0 - ADAM_B2) * (g * g)
    mhat = mn / BC1
    vhat = vn / BC2
    return g, -ADAM_LR * (mhat / (jnp.sqrt(vhat) + ADAM_EPS) + ADAM_WD * w), mn, vn


def _adamw(w, m, v, land, own, name):
    R, C = w.shape
    tr = R
    for cand in (256, 128, 64, 32, 16, 8):
        if R % cand == 0 and R > cand:
            tr = cand
            break

    def body(w_ref, m_ref, v_ref, land_ref, own_ref, g_ref, d_ref, mo_ref, vo_ref):
        g = _ordered_sum(land_ref, own_ref[...].astype(F32), _my_index())
        for o, val in zip((g_ref, d_ref, mo_ref, vo_ref), _adamw_math(w_ref[...], m_ref[...], v_ref[...], g)):
            o[...] = val

    blk = pl.BlockSpec((tr, C), lambda i: (i, 0))
    out = SDS((R, C), F32)
    return pl.pallas_call(
        body, name=name, grid=(R // tr,),
        in_specs=[blk, blk, blk, pl.BlockSpec((NDEV, tr, C), lambda i: (0, i, 0)), blk],
        out_specs=[blk, blk, blk, blk], out_shape=[out, out, out, out],
        compiler_params=_cp(40, ("arbitrary",)),
    )(w, m, v, land, own)


def _adamw_many(ws, ms, vs, lands, owns, name):
    k = len(ws)

    def body(*refs):
        ins, outs = refs[:5 * k], refs[5 * k:]
        me = _my_index()
        for i in range(k):
            w_ref, m_ref, v_ref, land_ref, own_ref = (ins[j * k + i] for j in range(5))
            g = _ordered_sum(land_ref, own_ref[...], me)
            for j, val in enumerate(_adamw_math(w_ref[...], m_ref[...], v_ref[...], g)):
                outs[j * k + i][...] = val

    out_shape = [SDS(w.shape, F32) for _ in range(4) for w in ws]
    res = pl.pallas_call(body, name=name, out_shape=out_shape, compiler_params=_cp(60))(*ws, *ms, *vs, *lands, *owns)
    return [res[j * k:(j + 1) * k] for j in range(4)]


MESH = pl.DeviceIdType.MESH
HBM_SPEC = pl.BlockSpec(memory_space=pltpu.HBM)
SEM_SPEC = pl.BlockSpec(memory_space=pltpu.SEMAPHORE)
EFFECT = pltpu.SideEffectType.DATAFLOW_SIDE_EFFECTING


def _me_and_peers():
    x, y, c = lax.axis_index("x"), lax.axis_index("y"), lax.axis_index("c")
    me = 4 * x + 2 * y + c
    peers = []
    for r in range(1, NDEV):
        px, py, pc = x ^ ((r >> 2) & 1), y ^ ((r >> 1) & 1), c ^ (r & 1)
        peers.append(((px, py, pc), 4 * px + 2 * py + pc))
    return me, peers


def _land_shape(a, scatter):
    return (NDEV,) + (a.shape[1:] if scatter else a.shape)


def _remote(src, dst, send_sems, recv_sems, r, k, n, dev):
    i = r * n + k
    return pltpu.make_async_remote_copy(src_ref=src, dst_ref=dst, send_sem=send_sems.at[i], recv_sem=recv_sems.at[i],
                                        device_id=dev, device_id_type=MESH)


def _exchange(arrays, scatter, name):
    n = len(arrays)
    out_shape = [SDS(_land_shape(a, scatter), a.dtype) for a in arrays]

    def body(*refs):
        ins, outs = refs[:n], refs[n:2 * n]
        send_sems, recv_sems, loc_sems = refs[2 * n:]
        me, peers = _me_and_peers()
        local = []
        for k in range(n):
            src = ins[k].at[me] if scatter else ins[k]
            cp = pltpu.make_async_copy(src, outs[k].at[me], loc_sems.at[k])
            cp.start()
            local.append(cp)
        sends = []
        for r, (dev, lin) in enumerate(peers):
            for k in range(n):
                src = ins[k].at[lin] if scatter else ins[k]
                cp = _remote(src, outs[k].at[me], send_sems, recv_sems, r, k, n, dev)
                cp.start()
                sends.append(cp)
        for r, (dev, lin) in enumerate(peers):
            for k in range(n):
                src = ins[k].at[me] if scatter else ins[k]
                _remote(src, outs[k].at[lin], send_sems, recv_sems, r, k, n, dev).wait_recv()
        for cp in sends:
            cp.wait_send()
        for cp in local:
            cp.wait()

    return pl.pallas_call(
        body, name=name, in_specs=[HBM_SPEC] * n, out_specs=[HBM_SPEC] * n, out_shape=out_shape,
        scratch_shapes=[pltpu.SemaphoreType.DMA(((NDEV - 1) * n,)), pltpu.SemaphoreType.DMA(((NDEV - 1) * n,)),
                        pltpu.SemaphoreType.DMA((n,))],
    )(*arrays)


def _exchange_start(arrays, scatter, name):
    n = len(arrays)
    lands = [lax.empty(_land_shape(a, scatter), a.dtype) for a in arrays]

    def body(*refs):
        ins, lnd = refs[:n], refs[n:2 * n]
        send_sems, recv_sems, own_sems = refs[2 * n:2 * n + 3]
        token = refs[-1]
        me, peers = _me_and_peers()
        for r, (dev, lin) in enumerate(peers):
            for k in range(n):
                src = ins[k].at[lin] if scatter else ins[k]
                _remote(src, lnd[k].at[me], send_sems, recv_sems, r, k, n, dev).start()
        if not scatter:
            for k in range(n):
                pltpu.make_async_copy(ins[k], lnd[k].at[me], own_sems.at[k]).start()
        token[...] = jnp.zeros_like(token)

    sem = pltpu.SemaphoreType.DMA(((NDEV - 1) * n,))
    outs = pl.pallas_call(
        body, name=name,
        out_shape=(sem, sem, pltpu.SemaphoreType.DMA((n,)), *[pltpu.HBM(a.shape, a.dtype) for a in arrays],
                   *[pltpu.HBM(l.shape, l.dtype) for l in lands], SDS((8, 128), F32)),
        in_specs=[HBM_SPEC] * (2 * n),
        out_specs=(SEM_SPEC, SEM_SPEC, SEM_SPEC, *[HBM_SPEC] * (2 * n), pl.BlockSpec(memory_space=pltpu.VMEM)),
        input_output_aliases={k: 3 + k for k in range(2 * n)},
        compiler_params=pltpu.CompilerParams(has_side_effects=EFFECT),
    )(*[pltpu.with_memory_space_constraint(a, pltpu.HBM) for a in arrays],
      *[pltpu.with_memory_space_constraint(l, pltpu.HBM) for l in lands])
    return (n, scatter, outs[0], outs[1], outs[2], outs[3:3 + n], outs[3 + n:3 + 2 * n]), outs[-1]


def _exchange_wait(handle, after, name):
    n, scatter, send_sems, recv_sems, own_sems, thru, lands = handle
    after = tuple(after)

    def body(*refs):
        ins, lnd = refs[:n], refs[n:2 * n]
        send_sems, recv_sems, own_sems = refs[2 * n:2 * n + 3]
        me, peers = _me_and_peers()
        for r, (dev, lin) in enumerate(peers):
            for k in range(n):
                src = ins[k].at[lin] if scatter else ins[k]
                cp = _remote(src, lnd[k].at[lin], send_sems, recv_sems, r, k, n, dev)
                cp.wait_send()
                cp.wait_recv()
        if not scatter:
            for k in range(n):
                pltpu.make_async_copy(ins[k], lnd[k].at[me], own_sems.at[k]).wait()

    outs = pl.pallas_call(
        body, name=name,
        out_shape=(*[pltpu.HBM(a.shape, a.dtype) for a in thru], *[pltpu.HBM(l.shape, l.dtype) for l in lands]),
        in_specs=[HBM_SPEC] * (2 * n) + [SEM_SPEC, SEM_SPEC, SEM_SPEC] + [ANY_SPEC] * len(after),
        out_specs=tuple([HBM_SPEC] * (2 * n)),
        input_output_aliases={k: k for k in range(2 * n)},
        compiler_params=pltpu.CompilerParams(has_side_effects=EFFECT),
    )(*thru, *lands, send_sems, recv_sems, own_sems, *after)
    return list(outs[:n]), list(outs[n:])


CHIP_RELATIONS = (2, 4, 6)


def _peer(r):
    x, y, c = lax.axis_index("x"), lax.axis_index("y"), lax.axis_index("c")
    px, py, pc = x ^ ((r >> 2) & 1), y ^ ((r >> 1) & 1), c ^ (r & 1)
    return (px, py, pc), 4 * px + 2 * py + pc


def _copy(src, dst, send_sems, recv_sems, i, dev):
    return pltpu.make_async_remote_copy(src_ref=src, dst_ref=dst, send_sem=send_sems.at[i], recv_sem=recv_sems.at[i],
                                        device_id=dev, device_id_type=MESH)


def _gather2_start(a, name):
    land = lax.empty((NDEV,) + a.shape, a.dtype)

    def body(own, lnd, send_sems, recv_sems, own_sem, own_thru, lnd_thru, token):
        me = _my_index()
        for i, r in enumerate((1,) + CHIP_RELATIONS):
            dev, _ = _peer(r)
            _copy(own, lnd.at[me], send_sems, recv_sems, i, dev).start()
        pltpu.make_async_copy(own, lnd.at[me], own_sem.at[0]).start()
        token[...] = jnp.zeros_like(token)

    sem4 = pltpu.SemaphoreType.DMA((4,))
    outs = pl.pallas_call(
        body, name=name,
        out_shape=(sem4, sem4, pltpu.SemaphoreType.DMA((1,)), pltpu.HBM(a.shape, a.dtype),
                   pltpu.HBM(land.shape, land.dtype), SDS((8, 128), F32)),
        in_specs=[HBM_SPEC, HBM_SPEC],
        out_specs=(SEM_SPEC, SEM_SPEC, SEM_SPEC, HBM_SPEC, HBM_SPEC, pl.BlockSpec(memory_space=pltpu.VMEM)),
        input_output_aliases={0: 3, 1: 4},
        compiler_params=pltpu.CompilerParams(has_side_effects=EFFECT),
    )(pltpu.with_memory_space_constraint(a, pltpu.HBM), pltpu.with_memory_space_constraint(land, pltpu.HBM))
    return outs[:5], outs[5]


def _gather2_forward(handle, after, name):
    send_sems, recv_sems, own_sem, own, land = handle
    after = tuple(after)

    def body(lnd, recv_sems, *rest):
        send2, recv2, lnd_thru = rest[len(after):]
        sib, _ = _peer(1)
        for k, r in enumerate(CHIP_RELATIONS):
            dev, lin = _peer(r)
            _copy(lnd.at[lin], lnd.at[lin], recv_sems, recv_sems, 1 + k, dev).wait_recv()
            _copy(lnd.at[lin], lnd.at[lin], send2, recv2, k, sib).start()

    sem3 = pltpu.SemaphoreType.DMA((3,))
    send2, recv2, land = pl.pallas_call(
        body, name=name,
        out_shape=(sem3, sem3, pltpu.HBM(land.shape, land.dtype)),
        in_specs=[HBM_SPEC, SEM_SPEC] + [ANY_SPEC] * len(after),
        out_specs=(SEM_SPEC, SEM_SPEC, HBM_SPEC),
        input_output_aliases={0: 2},
        compiler_params=pltpu.CompilerParams(has_side_effects=EFFECT),
    )(land, recv_sems, *after)
    return send_sems, recv_sems, own_sem, send2, recv2, own, land


def _gather2_wait(handle, name):
    send_sems, recv_sems, own_sem, send2, recv2, own, land = handle

    def body(own_ref, lnd, send_sems, recv_sems, own_sem, send2, recv2, own_thru, lnd_thru):
        me = _my_index()
        sib, sib_lin = _peer(1)
        for i, r in enumerate((1,) + CHIP_RELATIONS):
            dev, _ = _peer(r)
            _copy(own_ref, lnd.at[me], send_sems, recv_sems, i, dev).wait_send()
        _copy(own_ref, lnd.at[sib_lin], send_sems, recv_sems, 0, sib).wait_recv()
        for k, r in enumerate(CHIP_RELATIONS):
            _, lin = _peer(r)
            _, lin_other = _peer(r ^ 1)
            _copy(lnd.at[lin], lnd.at[lin], send2, recv2, k, sib).wait_send()
            _copy(lnd.at[lin_other], lnd.at[lin_other], send2, recv2, k, sib).wait_recv()
        pltpu.make_async_copy(own_ref, lnd.at[me], own_sem.at[0]).wait()

    outs = pl.pallas_call(
        body, name=name,
        out_shape=(pltpu.HBM(own.shape, own.dtype), pltpu.HBM(land.shape, land.dtype)),
        in_specs=[HBM_SPEC, HBM_SPEC] + [SEM_SPEC] * 5,
        out_specs=(HBM_SPEC, HBM_SPEC),
        input_output_aliases={0: 0, 1: 1},
        compiler_params=pltpu.CompilerParams(has_side_effects=EFFECT),
    )(own, land, send_sems, recv_sems, own_sem, send2, recv2)
    return outs[1]


def _local_step(x, tgt, norm_even, first_weight, lam_re, lam_im, log_dt, b_re, b_im, c_re, c_im, s5_d, bglu,
                ret_gain, wsp, bsp, fnorm, late_weights, emit, start_token=None):
    L = x.shape[0]
    lr3, li3 = lam_re.reshape(G, 1, P), lam_im.reshape(G, 1, P)
    dt3 = log_dt.reshape(G, 1, 1)
    br3, bi3 = jnp.swapaxes(b_re, 1, 2), jnp.swapaxes(b_im, 1, 2)
    abr3, abi3, bbr3, bbi3 = _s5_disc(lr3, li3, dt3, br3, bi3)
    bb = jnp.concatenate([_embed(bbr3), _embed(bbi3)], axis=2).astype(MXU)
    cm = jnp.concatenate([_embed(jnp.swapaxes(c_re, 1, 2)), -_embed(jnp.swapaxes(c_im, 1, 2))], axis=1).astype(MXU)
    abr, abi = abr3.reshape(1, NSTATE), abi3.reshape(1, NSTATE)
    pwr, pwi, _, _ = _s5_tables(abr, abi, NT, "s5_tables_step")
    par, pai, pbr, pbi = _s5_tables(pwr[NT - 1:NT], pwi[NT - 1:NT], CH, "s5_tables_chunk")
    inv = (ROPE_BASE ** (-jnp.arange(DK // 2, dtype=F32) / (DK // 2))).reshape(1, DK // 2)
    cos, sin = _rope_tables(L, inv)
    bsp3 = bsp.reshape(SG, CH, 1)

    def dep(token):
        return NO_DEPS if token is None else (token,)

    win_e = first_weight((cos, pbi, cm))
    p, u3 = _in_proj(x, norm_even, win_e, "in_even", dep(start_token), lane_blocks=True)
    ypre, s5_states = _s5_scan_fwd(u3, bb, cm, abr, abi, pwr, pwi, par, pai, s5_d)
    yb, ret_states = _ret_fwd(p, cos, sin, ret_gain)
    wglu, wout_e, norm_odd, win_o, sgu_gain, wout_o = late_weights((ypre, yb))
    ya = _s5_gate_fwd(ypre, p, wglu, bglu)
    x1 = _out_even(x, ya, yb, wout_e)
    p2 = _in_proj(x1, norm_odd, win_o, "in_odd")
    dx2, g_fnorm, loss = _sgu_fwd(p2, x1, sgu_gain, wsp, bsp3, wout_o, fnorm, tgt)

    dp2, y_o, g_sgu_gain, g_wsp, g_bsp = _sgu_bwd(p2, dx2, sgu_gain, wsp, bsp3, wout_o)
    g_wout_o = _wgrad_rows([y_o], dx2, "wgrad_out_odd")
    g_win_o = _wgrad_cols(x1, norm_odd, dp2, "wgrad_in_odd")
    tok = emit("odd", dict(w_in_odd=g_win_o, w_out_odd=g_wout_o))
    dx1, g_norm_odd = _in_proj_bwd_x(dp2, x1, norm_odd, win_o, dx2, "in_odd_bwd", dep(tok))
    tok = emit("small_odd", dict(norm_odd=g_norm_odd, sgu_norm_gain=g_sgu_gain, sgu_w_spatial=g_wsp,
                                 sgu_b_spatial=g_bsp.reshape(SG, CH), final_norm=g_fnorm))

    dypre, daz, yg, dt, ya2, g_bglu = _s5_gate_bwd(ypre, p, dx1, wout_e, wglu, bglu, dep(tok))
    g_wglu = _wgrad_rows([yg], dt, "wgrad_glu")
    tok = emit("glu", dict(s5_w_glu=g_wglu))
    du, g_d, g_cm, g_bb, g_ar, g_ai = _s5_scan_bwd(u3, dypre, s5_states, bb, cm, jnp.swapaxes(cm, 1, 2), abr, abi,
                                                   pwr, pwi, par, pai, pbr, pbi, s5_d, dep(tok))
    dbbr3 = _diag_blocks(g_bb[:, :, :SW], HG, P)
    dbbi3 = _diag_blocks(g_bb[:, :, SW:], HG, P)
    g_c_re = jnp.swapaxes(_diag_blocks(g_cm[:, :SW, :], P, HG), 1, 2)
    g_c_im = -jnp.swapaxes(_diag_blocks(g_cm[:, SW:, :], P, HG), 1, 2)
    g_lr3, g_li3, g_dt3, g_br3, g_bi3 = _s5_disc_bwd(
        lr3, li3, dt3, br3, bi3, g_ar.reshape(G, 1, P), g_ai.reshape(G, 1, P), dbbr3, dbbi3)
    tok = emit("small_s5", dict(
        s5_lam_re=g_lr3.reshape(G, P), s5_lam_im=g_li3.reshape(G, P), s5_log_dt=g_dt3.reshape(1, G),
        s5_b_re=g_br3, s5_b_im=g_bi3, s5_c_re=g_c_re, s5_c_im=g_c_im, s5_d=g_d, s5_b_glu=g_bglu))
    dp, yb2, g_ret_gain = _ret_bwd(p, cos, sin, ret_gain, ret_states, dx1, wout_e, du, daz, dep(tok))
    g_win_e = _wgrad_cols(x, norm_even, dp, "wgrad_in_even")
    tok = emit("even_cols", dict(w_in_even=g_win_e))
    g_wout_e = _wgrad_rows([ya2, yb2], dx1, "wgrad_out_even", dep(tok))
    tok = emit("even_rows", dict(w_out_even=g_wout_e))
    dx, g_norm_even = _in_proj_bwd_x(dp, x, norm_even, win_e, dx1, "in_even_bwd", dep(tok))
    emit("last", dict(ret_gn_gain=g_ret_gain, norm_even=g_norm_even))
    return loss, dx


WEIGHTS = ['norm_even', 'w_in_even', 's5_lam_re', 's5_lam_im', 's5_log_dt', 's5_b_re', 's5_b_im', 's5_c_re',
           's5_c_im', 's5_d', 's5_w_glu', 's5_b_glu', 'ret_gn_gain', 'w_out_even', 'norm_odd', 'w_in_odd',
           'sgu_norm_gain', 'sgu_w_spatial', 'sgu_b_spatial', 'w_out_odd', 'final_norm']
BIG = ['w_in_even', 's5_w_glu', 'w_out_even', 'w_in_odd', 'w_out_odd']
SHARDED_SMALL = {'norm_odd': D // NDEV, 'sgu_norm_gain': DI // NDEV}
SCATTER_STAGES = ("odd", "glu", "even_cols", "even_rows")
GATHER_STAGES = ("small_odd", "small_s5")


def _view(n, a):
    if n in ('s5_b_re', 's5_b_im'):
        return jnp.swapaxes(a[0], 1, 2)
    if n == 'final_norm':
        return a.reshape(1, D)
    return a[0] if a.ndim >= 3 else a


def _unview(n, t, shape):
    if n in ('s5_b_re', 's5_b_im'):
        return jnp.swapaxes(t, 1, 2)[None]
    return t.reshape(shape)


def kernel(x, norm_even, w_in_even, s5_lam_re, s5_lam_im, s5_log_dt, s5_b_re, s5_b_im, s5_c_re, s5_c_im, s5_d, s5_w_glu, s5_b_glu, ret_gn_gain, w_out_even, norm_odd, w_in_odd, sgu_norm_gain, sgu_w_spatial, sgu_b_spatial, w_out_odd, final_norm, loss_target, m_norm_even, m_w_in_even, m_s5_lam_re, m_s5_lam_im, m_s5_log_dt, m_s5_b_re, m_s5_b_im, m_s5_c_re, m_s5_c_im, m_s5_d, m_s5_w_glu, m_s5_b_glu, m_ret_gn_gain, m_w_out_even, m_norm_odd, m_w_in_odd, m_sgu_norm_gain, m_sgu_w_spatial, m_sgu_b_spatial, m_w_out_odd, m_final_norm, v_norm_even, v_w_in_even, v_s5_lam_re, v_s5_lam_im, v_s5_log_dt, v_s5_b_re, v_s5_b_im, v_s5_c_re, v_s5_c_im, v_s5_d, v_s5_w_glu, v_s5_b_glu, v_ret_gn_gain, v_w_out_even, v_norm_odd, v_w_in_odd, v_sgu_norm_gain, v_sgu_w_spatial, v_sgu_b_spatial, v_w_out_odd, v_final_norm):
    args = dict(locals())
    w = {n: args[n] for n in WEIGHTS}
    m = {n: args["m_" + n] for n in WEIGHTS}
    v = {n: args["v_" + n] for n in WEIGHTS}
    me = _my_index()

    first_handle, _ = _gather2_start(w['w_in_even'][0].astype(MXU), "gather_first_start")

    def first_weight(after):
        return _gather2_wait(_gather2_forward(first_handle, after, "gather_first_forward"), "gather_first_wait")

    late_own = [w['s5_w_glu'][0].astype(MXU), w['w_out_even'][0].astype(MXU), w['norm_odd'],
                w['w_in_odd'][0].astype(MXU), w['sgu_norm_gain'], w['w_out_odd'][0].astype(MXU)]
    late_handle, start_token = _exchange_start(late_own, False, "gather_late_start")

    def late_weights(after):
        _, (wglu, wout_e, nodd, win_o, sgug, wout_o) = _exchange_wait(late_handle, after, "gather_late_wait")
        return (wglu.reshape(D, D), wout_e.reshape(DI, D), nodd.reshape(1, D), win_o, sgug.reshape(1, DI),
                wout_o.reshape(DI, D))

    pending = {}
    small_last = {}

    def emit(stage, grads):
        if stage == "last":
            small_last.update(grads)
            return None
        names = list(grads)
        handle, token = _exchange_start([grads[n] for n in names], stage in SCATTER_STAGES, stage + "_start")
        pending[stage] = (handle, names)
        return token

    loss, dx = _local_step(
        x[0], loss_target[0], w['norm_even'], first_weight, w['s5_lam_re'][0], w['s5_lam_im'][0], w['s5_log_dt'][0],
        w['s5_b_re'][0], w['s5_b_im'][0], w['s5_c_re'][0], w['s5_c_im'][0], w['s5_d'], w['s5_b_glu'],
        w['ret_gn_gain'], w['sgu_w_spatial'][0], w['sgu_b_spatial'][0], w['final_norm'].reshape(1, D),
        late_weights, emit, start_token)

    out_g, out_d, out_m, out_v = {}, {}, {}, {}
    after = dx
    for stage in SCATTER_STAGES:
        handle, names = pending[stage]
        sent, lands = _exchange_wait(handle, (after,), stage + "_wait")
        for n, land, stack in zip(names, lands, sent):
            shp = w[n].shape
            r, c = shp[1], shp[2]
            own = lax.dynamic_index_in_dim(stack, me, 0, keepdims=False)
            res = _adamw(w[n].reshape(r, c), m[n].reshape(r, c), v[n].reshape(r, c), land, own, "adamw_" + n)
            out_g[n], out_d[n], out_m[n], out_v[n] = (t.reshape(shp) for t in res)
            after = res[0]

    names, owns, lands = [], [], []
    for stage in GATHER_STAGES:
        handle, stage_names = pending[stage]
        sent, got = _exchange_wait(handle, (after,), stage + "_wait")
        names, owns, lands = names + stage_names, owns + sent, lands + got
    last_names = list(small_last)
    last = _exchange([small_last[n] for n in last_names], False, "gather_last")
    names, owns, lands = names + last_names, owns + [small_last[n] for n in last_names], lands + list(last)
    for i, n in enumerate(names):
        if n in SHARDED_SMALL:
            width = SHARDED_SMALL[n]
            owns[i] = lax.dynamic_slice_in_dim(owns[i], me * width, width, axis=1)
            lands[i] = lax.dynamic_slice_in_dim(lands[i], me * width, width, axis=2)
    res = _adamw_many([_view(n, w[n]) for n in names], [_view(n, m[n]) for n in names],
                      [_view(n, v[n]) for n in names], lands, owns, "adamw_small")
    for dst, vals in zip((out_g, out_d, out_m, out_v), res):
        for n, t in zip(names, vals):
            dst[n] = _unview(n, t, w[n].shape)

    loss_total = lax.psum(loss[0, 0], AXES)
    return (loss_total, dx[None], *[out_g[n] for n in WEIGHTS], *[out_d[n] for n in WEIGHTS],
            *[out_m[n] for n in WEIGHTS], *[out_v[n] for n in WEIGHTS])
```

```python
import math

import jax
import jax.numpy as jnp
from jax import lax
from jax.experimental import pallas as pl
from jax.experimental.pallas import tpu as pltpu

F32 = jnp.float32
MXU = jnp.bfloat16
AXES = ("x", "y", "c")
NDEV = 8
D = 1024
NIN = 6144
WIN_BLK = NIN // NDEV
DI = 2048
G, P, HG = 64, 64, 16
GB = 8
NJ = G // GB
SW = GB * P
UW = GB * HG
NSTATE = G * P
HEADS, DK = 4, 256
CH = 128
SG, SGD = 4, 512
EPS = 1e-6
ROPE_BASE = 10000.0
VMEM_CAP_V7X = 64 * 1024 * 1024
LOG_G = [math.log1p(-2.0 ** (-5.0 - h)) for h in range(HEADS)]
GELU_C = math.sqrt(2.0 / math.pi)

ADAM_LR, ADAM_B1, ADAM_B2, ADAM_EPS, ADAM_WD, ADAM_STEP = 0.001, 0.9, 0.999, 1e-08, 0.01, 10
BC1 = 1.0 - ADAM_B1 ** ADAM_STEP
BC2 = 1.0 - ADAM_B2 ** ADAM_STEP

SDS = jax.ShapeDtypeStruct
ARB2 = ("arbitrary", "arbitrary")


def _cp(vmem_mib, sem=None):
    kw = dict(vmem_limit_bytes=min(vmem_mib * 1024 * 1024, VMEM_CAP_V7X - 4 * 1024 * 1024))
    if sem is not None:
        kw["dimension_semantics"] = sem
    return pltpu.CompilerParams(**kw)


def _mm(a, b):
    return jnp.dot(a.astype(MXU), b.astype(MXU), preferred_element_type=F32)


def _mm_nt(a, b):
    return lax.dot_general(a.astype(MXU), b.astype(MXU), (((1,), (1,)), ((), ())), preferred_element_type=F32)


def _mm_tn(a, b):
    return lax.dot_general(a.astype(MXU), b.astype(MXU), (((0,), (0,)), ((), ())), preferred_element_type=F32)


def _gelu(x):
    return _gelu_and_grad(x)[0]


def _gelu_and_grad(x):
    x2 = x * x
    th = jnp.tanh(GELU_C * x * (1.0 + 0.044715 * x2))
    hp = 0.5 * (1.0 + th)
    return x * hp, hp + 0.5 * x * (1.0 - th * th) * GELU_C * (1.0 + 3.0 * 0.044715 * x2)


def _silu_and_grad(x):
    s = jax.nn.sigmoid(x)
    return x * s, s * (1.0 + x * (1.0 - s))


def _full(shape):
    nd = len(shape)
    return pl.BlockSpec(shape, lambda *_: (0,) * nd)


def _rms(xf):
    r = lax.rsqrt(jnp.mean(xf * xf, axis=-1, keepdims=True) + EPS)
    return xf * r, r


ANY_SPEC = pl.BlockSpec(memory_space=pl.ANY)
NO_DEPS = ()


def _load_once(src_hbm, dst_vmem, sem):
    @pl.when(pl.program_id(0) == 0)
    def _():
        cp = pltpu.make_async_copy(src_hbm, dst_vmem, sem)
        cp.start()
        cp.wait()


def _lane_blocks(L):
    return SDS((NJ, L, UW), F32)


def _lane_block_spec(rows, index):
    return pl.BlockSpec((NJ, rows, UW), lambda i: (0, index(i), 0))


def _from_lane_blocks(ref):
    return jnp.concatenate([ref[j] for j in range(NJ)], axis=1)


def _to_lane_blocks(ref, v):
    for j in range(NJ):
        ref[j] = v[:, j * UW:(j + 1) * UW].astype(ref.dtype)


def _in_proj(x, gain, wst, name, deps=NO_DEPS, lane_blocks=False):
    L = x.shape[0]
    tm = min(512, L)

    def body(x_ref, g_ref, w_hbm, *rest):
        outs = rest[len(deps):]
        o_ref, w_scr, sem = outs[0], outs[-2], outs[-1]
        _load_once(w_hbm, w_scr, sem)
        xhat, _ = _rms(x_ref[...])
        h = (xhat * g_ref[...]).astype(MXU)
        for c in range(NDEV):
            o_ref[:, c * WIN_BLK:(c + 1) * WIN_BLK] = jnp.dot(h, w_scr[c], preferred_element_type=F32)
        if lane_blocks:
            _to_lane_blocks(outs[1], o_ref[:, 0:D])

    p_spec, p_shape = pl.BlockSpec((tm, NIN), lambda i: (i, 0)), SDS((L, NIN), F32)
    return pl.pallas_call(
        body, name=name, grid=(L // tm,),
        in_specs=[pl.BlockSpec((tm, D), lambda i: (i, 0)), _full((1, D)), ANY_SPEC] + [ANY_SPEC] * len(deps),
        out_specs=[p_spec, _lane_block_spec(tm, lambda i: i)] if lane_blocks else p_spec,
        out_shape=[p_shape, _lane_blocks(L)] if lane_blocks else p_shape,
        scratch_shapes=[pltpu.VMEM((NDEV, D, WIN_BLK), MXU), pltpu.SemaphoreType.DMA(())],
        compiler_params=_cp(58, ("arbitrary",)),
    )(x, gain, wst, *deps)


def _in_proj_bwd_x(dp, x, gain, wst, dres, name, deps=NO_DEPS):
    L = x.shape[0]
    tm = min(512, L)

    def body(dp_ref, x_ref, g_ref, w_hbm, dres_ref, *rest):
        dx_ref, gg_ref, w_scr, sem = rest[len(deps):]
        _load_once(w_hbm, w_scr, sem)

        @pl.when(pl.program_id(0) == 0)
        def _():
            gg_ref[...] = jnp.zeros_like(gg_ref)

        dh = _mm_nt(dp_ref[:, 0:WIN_BLK], w_scr[0])
        for c in range(1, NDEV):
            dh += _mm_nt(dp_ref[:, c * WIN_BLK:(c + 1) * WIN_BLK], w_scr[c])
        xhat, r = _rms(x_ref[...])
        dxhat = dh * g_ref[...]
        dx_ref[...] = dres_ref[...] + r * (dxhat - xhat * jnp.mean(dxhat * xhat, axis=-1, keepdims=True))
        gg_ref[...] += jnp.sum(dh * xhat, axis=0, keepdims=True)

    row = pl.BlockSpec((tm, D), lambda i: (i, 0))
    return pl.pallas_call(
        body, name=name, grid=(L // tm,),
        in_specs=[pl.BlockSpec((tm, NIN), lambda i: (i, 0)), row, _full((1, D)), ANY_SPEC, row]
        + [ANY_SPEC] * len(deps),
        out_specs=[row, _full((1, D))],
        out_shape=[SDS((L, D), F32), SDS((1, D), F32)],
        scratch_shapes=[pltpu.VMEM((NDEV, D, WIN_BLK), MXU), pltpu.SemaphoreType.DMA(())],
        compiler_params=_cp(56, ("arbitrary",)),
    )(dp, x, gain, wst, dres, *deps)


def _wgrad_cols(x, gain, dp, name, deps=NO_DEPS):
    L = x.shape[0]
    tk = min(512, L)
    nk = L // tk
    halves = 2
    nh = NDEV // halves

    def body(x_ref, g_ref, dp_ref, *rest):
        o_ref, acc = rest[len(deps):]
        k = pl.program_id(1)

        @pl.when(k == 0)
        def _():
            acc[...] = jnp.zeros_like(acc)

        xhat, _ = _rms(x_ref[...])
        acc[...] += _mm_tn(xhat * g_ref[...], dp_ref[...])

        @pl.when(k == nk - 1)
        def _():
            for c in range(nh):
                o_ref[c] = acc[:, c * WIN_BLK:(c + 1) * WIN_BLK].astype(o_ref.dtype)

    return pl.pallas_call(
        body, name=name, grid=(halves, nk),
        in_specs=[pl.BlockSpec((tk, D), lambda n, k: (k, 0)), _full((1, D)),
                  pl.BlockSpec((tk, nh * WIN_BLK), lambda n, k: (k, n))] + [ANY_SPEC] * len(deps),
        out_specs=pl.BlockSpec((nh, D, WIN_BLK), lambda n, k: (n, 0, 0)),
        out_shape=SDS((NDEV, D, WIN_BLK), MXU),
        scratch_shapes=[pltpu.VMEM((D, nh * WIN_BLK), F32)],
        compiler_params=_cp(56, ARB2),
    )(x, gain, dp, *deps)


def _wgrad_rows(a_parts, b, name, deps=NO_DEPS):
    L, N = b.shape
    na = len(a_parts)
    widths = [a.shape[1] for a in a_parts]
    M = sum(widths)
    tk = min(512, L)
    nk = L // tk

    def body(*refs):
        a_refs, b_ref = refs[:na], refs[na]
        o_ref, acc = refs[na + 1 + len(deps):]
        k = pl.program_id(0)

        @pl.when(k == 0)
        def _():
            acc[...] = jnp.zeros_like(acc)

        bv = b_ref[...].astype(MXU)
        off = 0
        for a_ref, wd in zip(a_refs, widths):
            acc[off:off + wd, :] += _mm_tn(a_ref[...], bv)
            off += wd

        @pl.when(k == nk - 1)
        def _():
            o_ref[...] = acc[...].astype(o_ref.dtype).reshape(o_ref.shape)

    return pl.pallas_call(
        body, name=name, grid=(nk,),
        in_specs=[pl.BlockSpec((tk, wd), lambda k: (k, 0)) for wd in widths]
        + [pl.BlockSpec((tk, N), lambda k: (k, 0))] + [ANY_SPEC] * len(deps),
        out_specs=_full((NDEV, M // NDEV, N)),
        out_shape=SDS((NDEV, M // NDEV, N), MXU),
        scratch_shapes=[pltpu.VMEM((M, N), F32)],
        compiler_params=_cp(48, ("arbitrary",)),
    )(*a_parts, b, *deps)


def _s5_disc_fn(lr_raw, li, logdt, br, bi):
    lr = jnp.minimum(lr_raw, -1e-4)
    dt = jnp.exp(logdt)
    mag = jnp.exp(lr * dt)
    abr = mag * jnp.cos(li * dt)
    abi = mag * jnp.sin(li * dt)
    den = lr * lr + li * li
    nre = abr - 1.0
    nim = abi
    zr = (nre * lr + nim * li) / den
    zi = (nim * lr - nre * li) / den
    return abr, abi, zr * br - zi * bi, zr * bi + zi * br


def _s5_disc(lr, li, logdt, br, bi):
    def body(lr_ref, li_ref, dt_ref, br_ref, bi_ref, abr_ref, abi_ref, bbr_ref, bbi_ref):
        abr, abi, bbr, bbi = _s5_disc_fn(lr_ref[...], li_ref[...], dt_ref[...], br_ref[...], bi_ref[...])
        abr_ref[...] = abr
        abi_ref[...] = abi
        bbr_ref[...] = bbr
        bbi_ref[...] = bbi

    s1, s3 = SDS((G, 1, P), F32), SDS((G, HG, P), F32)
    return pl.pallas_call(body, name="s5_disc", out_shape=[s1, s1, s3, s3])(lr, li, logdt, br, bi)


def _s5_disc_bwd(lr, li, logdt, br, bi, dabr, dabi, dbbr, dbbi):
    def body(lr_ref, li_ref, dt_ref, br_ref, bi_ref, c0, c1, c2, c3, o0, o1, o2, o3, o4):
        _, vjp = jax.vjp(_s5_disc_fn, lr_ref[...], li_ref[...], dt_ref[...], br_ref[...], bi_ref[...])
        g = vjp((c0[...], c1[...], c2[...], c3[...]))
        for o, v in zip((o0, o1, o2, o3, o4), g):
            o[...] = v

    s1, s3 = SDS((G, 1, P), F32), SDS((G, HG, P), F32)
    return pl.pallas_call(body, name="s5_disc_bwd", out_shape=[s1, s1, SDS((G, 1, 1), F32), s3, s3])(
        lr, li, logdt, br, bi, dabr, dabi, dbbr, dbbi)


def _s5_tables(abr, abi, rows, name):
    def body(ar_ref, ai_ref, pfr, pfi, pbr, pbi):
        pfr[0:1, :] = ar_ref[...]
        pfi[0:1, :] = ai_ref[...]
        pbr[rows - 1:rows, :] = ar_ref[...]
        pbi[rows - 1:rows, :] = ai_ref[...]
        n = 1
        while n < rows:
            er, ei = pfr[n - 1:n, :], pfi[n - 1:n, :]
            xr, xi = pfr[0:n, :], pfi[0:n, :]
            pfr[n:2 * n, :] = er * xr - ei * xi
            pfi[n:2 * n, :] = er * xi + ei * xr
            yr, yi = pbr[rows - n:rows, :], pbi[rows - n:rows, :]
            pbr[rows - 2 * n:rows - n, :] = er * yr - ei * yi
            pbi[rows - 2 * n:rows - n, :] = er * yi + ei * yr
            n *= 2

    s = SDS((rows, NSTATE), F32)
    return pl.pallas_call(body, name=name, out_shape=[s, s, s, s], compiler_params=_cp(40))(abr, abi)


def _cscan(br, bi, pr_ref, pi_ref, reverse):
    T = br.shape[0]
    sign = -1.0 if reverse else 1.0
    row = lax.broadcasted_iota(jnp.int32, br.shape, 0)
    k = 1
    while k < T:
        akr = pr_ref[k - 1:k, :]
        aki = sign * pi_ref[k - 1:k, :]

        def shift(v):
            if k % 8 == 0:
                z = jnp.zeros((k, v.shape[1]), v.dtype)
                return jnp.concatenate([v[k:], z], 0) if reverse else jnp.concatenate([z, v[:T - k]], 0)
            if reverse:
                return jnp.where(row < T - k, pltpu.roll(v, T - k, 0), 0.0)
            return jnp.where(row >= k, pltpu.roll(v, k, 0), 0.0)

        sr, si = shift(br), shift(bi)
        br, bi = br + akr * sr - aki * si, bi + akr * si + aki * sr
        k *= 2
    return br, bi


def _embed(t):
    a, b = t.shape[1], t.shape[2]
    return jnp.einsum("jgab,gh->jgahb", t.reshape(NJ, GB, a, b), jnp.eye(GB, dtype=t.dtype)).reshape(NJ, GB * a, GB * b)


def _diag_blocks(t, a, b):
    return jnp.einsum("jgahb,gh->jgab", t.reshape(NJ, GB, a, GB, b), jnp.eye(GB, dtype=t.dtype)).reshape(G, a, b)


NT = 16


def _chunks(L):
    ncb = min(CH, L // NT)
    return ncb, NT * ncb


def _cmul_add(ar, ai, xr, xi, br, bi):
    return ar * xr - ai * xi + br, ar * xi + ai * xr + bi


def _pow_weights(w_ref, pwr_ref, pwi_ref, dst, adjoint):
    w = w_ref[...].astype(F32)
    wr, wi = w[:, :SW], w[:, SW:]
    for t in range(NT):
        k = t if adjoint else NT - 1 - t
        if k == 0:
            blk = w
        else:
            pr, pi = pwr_ref[k - 1:k, :], pwi_ref[k - 1:k, :]
            if adjoint:
                blk = jnp.concatenate([pr * wr + pi * wi, pr * wi - pi * wr], axis=1)
            else:
                blk = jnp.concatenate([pr * wr - pi * wi, pr * wi + pi * wr], axis=1)
        dst[t * UW:(t + 1) * UW, :] = blk.astype(dst.dtype)


def _s5_states(u_ref, bb_ref, bbp_scr, par_ref, pai_ref, cr, ci, ncb, bu_scr):
    us = [u_ref[pl.ds(t, ncb, stride=NT), :] for t in range(NT)]
    for t in range(NT):
        bu_scr[t] = _mm(us[t], bb_ref[...])
    e = _mm(jnp.concatenate(us, axis=1), bbp_scr[...])
    xr, xi = _cscan(e[:, :SW], e[:, SW:], par_ref, pai_ref, False)
    fr, fi = _cmul_add(par_ref[0:ncb, :], pai_ref[0:ncb, :], cr, ci, xr, xi)
    row = lax.broadcasted_iota(jnp.int32, fr.shape, 0)
    cinr = jnp.where(row >= 1, pltpu.roll(fr, 1, 0), cr)
    cini = jnp.where(row >= 1, pltpu.roll(fi, 1, 0), ci)
    return cinr, cini, jnp.concatenate([fr[ncb - 1:ncb, :], fi[ncb - 1:ncb, :]], axis=1)


def _s5_scan_fwd(u3, bb, cm, abr, abi, pwr, pwi, par, pai, dskip):
    L = u3.shape[1]
    ncb, tb = _chunks(L)
    nb = L // tb

    def body(u_ref, bb_ref, cm_ref, ar_ref, ai_ref, pwr_ref, pwi_ref, par_ref, pai_ref, d_ref, ypre_ref, st_ref,
             carry, bu_scr, bbp_scr):
        @pl.when(pl.program_id(1) == 0)
        def _():
            carry[...] = jnp.zeros_like(carry)
            _pow_weights(bb_ref, pwr_ref, pwi_ref, bbp_scr, False)

        c = carry[...]
        st_ref[...] = c
        ar, ai = ar_ref[...], ai_ref[...]
        sr, si, cnext = _s5_states(u_ref, bb_ref, bbp_scr, par_ref, pai_ref, c[:, :SW], c[:, SW:], ncb, bu_scr)
        carry[...] = cnext
        for t in range(NT):
            bu = bu_scr[t]
            sr, si = _cmul_add(ar, ai, sr, si, bu[:, :SW], bu[:, SW:])
            rows = pl.ds(t, ncb, stride=NT)
            ypre_ref[rows, :] = _mm(jnp.concatenate([sr, si], axis=1), cm_ref[...]) + d_ref[...] * u_ref[rows, :]

    tab = pl.BlockSpec((CH, SW), lambda j, i: (0, j))
    stp = pl.BlockSpec((NT, SW), lambda j, i: (0, j))
    vec = lambda w: pl.BlockSpec((1, w), lambda j, i: (0, j))
    return pl.pallas_call(
        body, name="s5_scan_fwd", grid=(NJ, nb),
        in_specs=[pl.BlockSpec((None, tb, UW), lambda j, i: (j, i, 0)),
                  pl.BlockSpec((None, UW, 2 * SW), lambda j, i: (j, 0, 0)),
                  pl.BlockSpec((None, 2 * SW, UW), lambda j, i: (j, 0, 0)),
                  vec(SW), vec(SW), stp, stp, tab, tab, vec(UW)],
        out_specs=[pl.BlockSpec((None, tb, UW), lambda j, i: (j, i, 0)),
                   pl.BlockSpec((None, None, 1, 2 * SW), lambda j, i: (j, i, 0, 0))],
        out_shape=[_lane_blocks(L), SDS((NJ, nb, 1, 2 * SW), F32)],
        scratch_shapes=[pltpu.VMEM((1, 2 * SW), F32), pltpu.VMEM((NT, ncb, 2 * SW), F32),
                        pltpu.VMEM((NT * UW, 2 * SW), MXU)],
        compiler_params=_cp(48, ARB2),
    )(u3, bb, cm, abr, abi, pwr, pwi, par, pai, dskip)


def _s5_gate_fwd(ypre3, p, wglu, bglu):
    L = p.shape[0]
    tm = min(256, L)

    def body(y_ref, az_ref, wg_ref, bg_ref, ya_ref):
        yg = _gelu(_from_lane_blocks(y_ref))
        t = _mm(yg, wg_ref[...]) + bg_ref[...]
        act, _ = _silu_and_grad(az_ref[...])
        ya_ref[...] = (yg * jax.nn.sigmoid(t) * act).astype(ya_ref.dtype)

    return pl.pallas_call(
        body, name="s5_gate_fwd", grid=(L // tm,),
        in_specs=[_lane_block_spec(tm, lambda i: i), pl.BlockSpec((tm, D), lambda i: (i, 1)),
                  _full((D, D)), _full((1, D))],
        out_specs=pl.BlockSpec((tm, D), lambda i: (i, 0)),
        out_shape=SDS((L, D), MXU),
        compiler_params=_cp(32, ("arbitrary",)),
    )(ypre3, p, wglu, bglu)


def _s5_gate_bwd(ypre3, p, dx1, wout_e, wglu, bglu, deps=NO_DEPS):
    L = p.shape[0]
    tm = min(256, L)

    def body(y_ref, az_ref, dx1_ref, wo_ref, wg_ref, bg_ref, *rest):
        dyp_ref, daz_ref, yg_ref, dt_ref, ya_ref, gbg_ref = rest[len(deps):]

        @pl.when(pl.program_id(0) == 0)
        def _():
            gbg_ref[...] = jnp.zeros_like(gbg_ref)

        yg, dgelu = _gelu_and_grad(_from_lane_blocks(y_ref))
        sg = jax.nn.sigmoid(_mm(yg, wg_ref[...]) + bg_ref[...])
        act, dact = _silu_and_grad(az_ref[...])
        y2 = yg * sg
        dya = _mm_nt(dx1_ref[...], wo_ref[...])
        daz_ref[...] = (dya * y2 * dact).astype(daz_ref.dtype)
        dy2 = dya * act
        dt = dy2 * yg * sg * (1.0 - sg)
        dyg = dy2 * sg + _mm_nt(dt, wg_ref[...])
        _to_lane_blocks(dyp_ref, dyg * dgelu)
        yg_ref[...] = yg.astype(yg_ref.dtype)
        dt_ref[...] = dt.astype(dt_ref.dtype)
        ya_ref[...] = (y2 * act).astype(ya_ref.dtype)
        gbg_ref[...] += jnp.sum(dt, axis=0, keepdims=True)

    row = pl.BlockSpec((tm, D), lambda i: (i, 0))
    return pl.pallas_call(
        body, name="s5_gate_bwd", grid=(L // tm,),
        in_specs=[_lane_block_spec(tm, lambda i: i), pl.BlockSpec((tm, D), lambda i: (i, 1)), row,
                  pl.BlockSpec((D, D), lambda i: (0, 0)), _full((D, D)), _full((1, D))] + [ANY_SPEC] * len(deps),
        out_specs=[_lane_block_spec(tm, lambda i: i), row, row, row, row, _full((1, D))],
        out_shape=[_lane_blocks(L), SDS((L, D), MXU), SDS((L, D), MXU), SDS((L, D), MXU), SDS((L, D), MXU),
                   SDS((1, D), F32)],
        compiler_params=_cp(40, ("arbitrary",)),
    )(ypre3, p, dx1, wout_e, wglu, bglu, *deps)


def _s5_scan_bwd(u3, dypre3, states, bb, cm, cmt, abr, abi, pwr, pwi, par, pai, pbr, pbi, dskip, deps=NO_DEPS):
    L = u3.shape[1]
    ncb, tb = _chunks(L)
    nb = L // tb
    rev = lambda i: nb - 1 - i

    def body(u_ref, dy_ref, st_ref, bb_ref, cm_ref, cmt_ref, ar_ref, ai_ref, pwr_ref, pwi_ref, par_ref, pai_ref,
             pbr_ref, pbi_ref, d_ref, *rest):
        (du_ref, gd_ref, gcm_ref, gbb_ref, gar_ref, gai_ref,
         lcarry, bu_scr, s_scr, gs_scr, bbp_scr, cmp_scr) = rest[len(deps):]
        del cm_ref

        @pl.when(pl.program_id(1) == 0)
        def _():
            _pow_weights(bb_ref, pwr_ref, pwi_ref, bbp_scr, False)
            _pow_weights(cmt_ref, pwr_ref, pwi_ref, cmp_scr, True)
            lcarry[...] = jnp.zeros_like(lcarry)
            gd_ref[...] = jnp.zeros_like(gd_ref)
            gcm_ref[...] = jnp.zeros_like(gcm_ref)
            gbb_ref[...] = jnp.zeros_like(gbb_ref)
            gar_ref[...] = jnp.zeros_like(gar_ref)
            gai_ref[...] = jnp.zeros_like(gai_ref)

        ar, ai = ar_ref[...], ai_ref[...]
        c = st_ref[...]
        sr, si, _ = _s5_states(u_ref, bb_ref, bbp_scr, par_ref, pai_ref, c[:, :SW], c[:, SW:], ncb, bu_scr)
        s_scr[0] = jnp.concatenate([sr, si], axis=1)
        for t in range(NT):
            bu = bu_scr[t]
            sr, si = _cmul_add(ar, ai, sr, si, bu[:, :SW], bu[:, SW:])
            s_scr[t + 1] = jnp.concatenate([sr, si], axis=1)
        dys = [dy_ref[pl.ds(t, ncb, stride=NT), :] for t in range(NT)]
        for t in range(NT):
            gs_scr[t] = _mm(dys[t], cmt_ref[...])
        f = _mm(jnp.concatenate(dys, axis=1), cmp_scr[...])
        xr, xi = _cscan(f[:, :SW], f[:, SW:], par_ref, pai_ref, True)
        lc = lcarry[...]
        lcr, lci = lc[:, :SW], lc[:, SW:]
        hr, hi = _cmul_add(pbr_ref[CH - ncb:CH, :], -pbi_ref[CH - ncb:CH, :], lcr, lci, xr, xi)
        lcarry[...] = jnp.concatenate([hr[0:1, :], hi[0:1, :]], axis=1)
        row = lax.broadcasted_iota(jnp.int32, hr.shape, 0)
        lr_ = jnp.where(row < ncb - 1, pltpu.roll(hr, ncb - 1, 0), lcr)
        li_ = jnp.where(row < ncb - 1, pltpu.roll(hi, ncb - 1, 0), lci)
        gar = jnp.zeros((1, SW), F32)
        gai = jnp.zeros((1, SW), F32)
        for t in reversed(range(NT)):
            gs = gs_scr[t]
            lr_, li_ = _cmul_add(ar, -ai, lr_, li_, gs[:, :SW], gs[:, SW:])
            rows = pl.ds(t, ncb, stride=NT)
            u_t, dy_t = u_ref[rows, :], dy_ref[rows, :]
            lam = jnp.concatenate([lr_, li_], axis=1)
            gbb_ref[...] += _mm_tn(u_t, lam)
            du_ref[rows, :] = _mm_nt(lam, bb_ref[...]) + dy_t * d_ref[...]
            gd_ref[...] += jnp.sum(dy_t * u_t, axis=0, keepdims=True)
            gcm_ref[...] += _mm_tn(s_scr[t + 1], dy_t)
            sp = s_scr[t]
            spr, spi = sp[:, :SW], sp[:, SW:]
            gar += jnp.sum(lr_ * spr + li_ * spi, axis=0, keepdims=True)
            gai += jnp.sum(li_ * spr - lr_ * spi, axis=0, keepdims=True)
        gar_ref[...] += gar
        gai_ref[...] += gai

    tab = pl.BlockSpec((CH, SW), lambda j, i: (0, j))
    stp = pl.BlockSpec((NT, SW), lambda j, i: (0, j))
    colblk = pl.BlockSpec((None, tb, UW), lambda j, i: (j, rev(i), 0))
    vec = lambda w: pl.BlockSpec((1, w), lambda j, i: (0, j))
    return pl.pallas_call(
        body, name="s5_scan_bwd", grid=(NJ, nb),
        in_specs=[colblk, colblk,
                  pl.BlockSpec((None, None, 1, 2 * SW), lambda j, i: (j, rev(i), 0, 0)),
                  pl.BlockSpec((None, UW, 2 * SW), lambda j, i: (j, 0, 0)),
                  pl.BlockSpec((None, 2 * SW, UW), lambda j, i: (j, 0, 0)),
                  pl.BlockSpec((None, UW, 2 * SW), lambda j, i: (j, 0, 0)),
                  vec(SW), vec(SW), stp, stp, tab, tab, tab, tab, vec(UW)] + [ANY_SPEC] * len(deps),
        out_specs=[colblk, vec(UW),
                   pl.BlockSpec((None, 2 * SW, UW), lambda j, i: (j, 0, 0)),
                   pl.BlockSpec((None, UW, 2 * SW), lambda j, i: (j, 0, 0)),
                   vec(SW), vec(SW)],
        out_shape=[_lane_blocks(L), SDS((1, D), F32),
                   SDS((NJ, 2 * SW, UW), F32), SDS((NJ, UW, 2 * SW), F32),
                   SDS((1, NSTATE), F32), SDS((1, NSTATE), F32)],
        scratch_shapes=[pltpu.VMEM((1, 2 * SW), F32), pltpu.VMEM((NT, ncb, 2 * SW), F32),
                        pltpu.VMEM((NT + 1, ncb, 2 * SW), F32), pltpu.VMEM((NT, ncb, 2 * SW), F32),
                        pltpu.VMEM((NT * UW, 2 * SW), MXU), pltpu.VMEM((NT * UW, 2 * SW), MXU)],
        compiler_params=_cp(56, ARB2),
    )(u3, dypre3, states, bb, cm, cmt, abr, abi, pwr, pwi, par, pai, pbr, pbi, dskip, *deps)


def _rope_tables(L, inv):
    tm = min(512, L)

    def body(inv_ref, cos_ref, sin_ref):
        pos = (lax.broadcasted_iota(jnp.int32, (tm, DK // 2), 0) + pl.program_id(0) * tm).astype(F32)
        ang = pos * inv_ref[...]
        cos_ref[...] = jnp.cos(ang)
        sin_ref[...] = jnp.sin(ang)

    blk = pl.BlockSpec((tm, DK // 2), lambda i: (i, 0))
    return pl.pallas_call(body, name="rope_tables", grid=(L // tm,), in_specs=[_full((1, DK // 2))],
                          out_specs=[blk, blk], out_shape=[SDS((L, DK // 2), F32)] * 2)(inv)


def _rot(x, cos, sin):
    x1, x2 = x[:, :DK // 2], x[:, DK // 2:]
    return jnp.concatenate([x1 * cos - x2 * sin, x1 * sin + x2 * cos], axis=1)


def _unrot(d, cos, sin):
    d1, d2 = d[:, :DK // 2], d[:, DK // 2:]
    return jnp.concatenate([d1 * cos + d2 * sin, d2 * cos - d1 * sin], axis=1)


def _ret_decays(h):
    lg = LOG_G[h]
    n = lax.broadcasted_iota(jnp.int32, (CH, CH), 0)
    m = lax.broadcasted_iota(jnp.int32, (CH, CH), 1)
    diff = (n - m).astype(F32)
    decay = jnp.where(n >= m, jnp.exp(lg * jnp.maximum(diff, 0.0)), 0.0)
    idx = lax.broadcasted_iota(jnp.int32, (CH, 1), 0).astype(F32)
    xi = jnp.exp(lg * (idx + 1.0))
    zeta = jnp.exp(lg * (CH - 1.0 - idx))
    return decay, xi, zeta, math.exp(lg * CH)


def _ret_tables(dec_scr, vec_scr):
    for h in range(HEADS):
        decay, xi, zeta, _ = _ret_decays(h)
        dec_scr[h] = decay
        vec_scr[h] = jnp.concatenate([jnp.broadcast_to(xi, (CH, 128)), jnp.broadcast_to(zeta, (CH, 128))], axis=1)


def _ret_chunk_fwd(q, k, v, cos, sin, s_prev_b, decay, xi, zeta):
    qr = _rot(q, cos, sin)
    kr = _rot(k, cos, sin) * (DK ** -0.5)
    scores = _mm_nt(qr, kr) * decay
    o = _mm(scores, v) + _mm(qr * xi, s_prev_b)
    local = _mm_tn(kr * zeta, v)
    mu = jnp.mean(o, axis=-1, keepdims=True)
    oc = o - mu
    rstd = lax.rsqrt(jnp.mean(oc * oc, axis=-1, keepdims=True) + EPS)
    return qr, kr, scores, local, oc * rstd, rstd


def _ret_fwd(p, cos, sin, gain):
    L = p.shape[0]
    nb = L // CH

    def body(q_ref, k_ref, v_ref, bz_ref, cos_ref, sin_ref, g_ref, yb_ref, st_ref, state, dec_scr, vec_scr):
        @pl.when(pl.program_id(0) == 0)
        def _():
            state[...] = jnp.zeros_like(state)
            _ret_tables(dec_scr, vec_scr)

        cos, sin = cos_ref[...], sin_ref[...]
        act, _ = _silu_and_grad(bz_ref[...])
        for h in range(HEADS):
            hs = slice(h * DK, (h + 1) * DK)
            xi, zeta = vec_scr[h, :, 0:1], vec_scr[h, :, 128:129]
            s_prev = state[h]
            s_prev_b = s_prev.astype(MXU)
            st_ref[h] = s_prev_b
            _, _, _, local, on, _ = _ret_chunk_fwd(q_ref[:, hs], k_ref[:, hs], v_ref[:, hs], cos, sin,
                                                   s_prev_b, dec_scr[h], xi, zeta)
            state[h] = s_prev * math.exp(LOG_G[h] * CH) + local
            yb_ref[:, hs] = (on * g_ref[:, hs] * act[:, hs]).astype(yb_ref.dtype)

    col = lambda c: pl.BlockSpec((CH, D), lambda i: (i, c))
    rope = pl.BlockSpec((CH, DK // 2), lambda i: (i, 0))
    return pl.pallas_call(
        body, name="ret_fwd", grid=(nb,),
        in_specs=[col(2), col(3), col(4), col(5), rope, rope, _full((1, D))],
        out_specs=[pl.BlockSpec((CH, D), lambda i: (i, 0)),
                   pl.BlockSpec((None, HEADS, DK, DK), lambda i: (i, 0, 0, 0))],
        out_shape=[SDS((L, D), MXU), SDS((nb, HEADS, DK, DK), MXU)],
        scratch_shapes=[pltpu.VMEM((HEADS, DK, DK), F32), pltpu.VMEM((HEADS, CH, CH), F32),
                        pltpu.VMEM((HEADS, CH, 256), F32)],
        compiler_params=_cp(40, ("arbitrary",)),
    )(p, p, p, p, cos, sin, gain)


def _ret_bwd(p, cos, sin, gain, states, dx1, wout_e, du, daz, deps=NO_DEPS):
    L = p.shape[0]
    nb = L // CH
    rev = lambda i: nb - 1 - i

    def body(q_ref, k_ref, v_ref, bz_ref, cos_ref, sin_ref, g_ref, st_ref, dx1_ref, wo_ref, du_ref, daz_ref, *rest):
        dp_ref, yb_ref, gg_ref, gstate, dec_scr, vec_scr = rest[len(deps):]

        @pl.when(pl.program_id(0) == 0)
        def _():
            gstate[...] = jnp.zeros_like(gstate)
            gg_ref[...] = jnp.zeros_like(gg_ref)
            _ret_tables(dec_scr, vec_scr)

        cos, sin = cos_ref[...], sin_ref[...]
        act, dact = _silu_and_grad(bz_ref[...])
        dyb = _mm_nt(dx1_ref[...], wo_ref[...])
        dp_ref[:, 0:D] = _from_lane_blocks(du_ref).astype(dp_ref.dtype)
        dp_ref[:, D:2 * D] = daz_ref[...]
        for h in range(HEADS):
            hs = slice(h * DK, (h + 1) * DK)
            col = lambda part: slice((2 + part) * D + h * DK, (2 + part) * D + (h + 1) * DK)
            decay = dec_scr[h]
            xi, zeta = vec_scr[h, :, 0:1], vec_scr[h, :, 128:129]
            v = v_ref[:, hs]
            s_prev_b = st_ref[h]
            qr, kr, scores, _, on, rstd = _ret_chunk_fwd(q_ref[:, hs], k_ref[:, hs], v, cos, sin, s_prev_b,
                                                         decay, xi, zeta)
            gain_h = g_ref[:, hs]
            out = on * gain_h
            yb_ref[:, hs] = (out * act[:, hs]).astype(yb_ref.dtype)
            dyb_h = dyb[:, hs]
            dp_ref[:, col(3)] = (dyb_h * out * dact[:, hs]).astype(dp_ref.dtype)
            dout = dyb_h * act[:, hs]
            gg_ref[:, hs] += jnp.sum(dout * on, axis=0, keepdims=True)
            don = dout * gain_h
            do = rstd * (don - jnp.mean(don, axis=-1, keepdims=True)
                         - on * jnp.mean(don * on, axis=-1, keepdims=True))
            gnext = gstate[h]
            gnext_b = gnext.astype(MXU)
            dscores = _mm_nt(do, v) * decay
            dp_ref[:, col(2)] = (_mm_tn(scores, do) + _mm(kr * zeta, gnext_b)).astype(dp_ref.dtype)
            dqr = _mm(dscores, kr) + _mm_nt(do, s_prev_b) * xi
            dkr = _mm_tn(dscores, qr) + _mm_nt(v, gnext_b) * zeta
            gstate[h] = gnext * math.exp(LOG_G[h] * CH) + _mm_tn(qr * xi, do)
            dp_ref[:, col(0)] = _unrot(dqr, cos, sin).astype(dp_ref.dtype)
            dp_ref[:, col(1)] = (_unrot(dkr, cos, sin) * (DK ** -0.5)).astype(dp_ref.dtype)

    col = lambda c: pl.BlockSpec((CH, D), lambda i: (rev(i), c))
    rope = pl.BlockSpec((CH, DK // 2), lambda i: (rev(i), 0))
    outc = col(0)
    act_out = SDS((L, D), MXU)
    return pl.pallas_call(
        body, name="ret_bwd", grid=(nb,),
        in_specs=[col(2), col(3), col(4), col(5), rope, rope, _full((1, D)),
                  pl.BlockSpec((None, HEADS, DK, DK), lambda i: (rev(i), 0, 0, 0)),
                  outc, pl.BlockSpec((D, D), lambda i: (1, 0)), _lane_block_spec(CH, rev), outc]
        + [ANY_SPEC] * len(deps),
        out_specs=[pl.BlockSpec((CH, NIN), lambda i: (rev(i), 0)), outc, _full((1, D))],
        out_shape=[SDS((L, NIN), MXU), act_out, SDS((1, D), F32)],
        scratch_shapes=[pltpu.VMEM((HEADS, DK, DK), F32), pltpu.VMEM((HEADS, CH, CH), F32),
                        pltpu.VMEM((HEADS, CH, 256), F32)],
        compiler_params=_cp(48, ("arbitrary",)),
    )(p, p, p, p, cos, sin, gain, states, dx1, wout_e, du, daz, *deps)


def _out_even(x, ya, yb, wout):
    L = x.shape[0]
    tm = min(512, L)

    def body(x_ref, ya_ref, yb_ref, w_ref, o_ref):
        cat = jnp.concatenate([ya_ref[...], yb_ref[...]], axis=1)
        o_ref[...] = x_ref[...] + jnp.dot(cat, w_ref[...], preferred_element_type=F32)

    row = pl.BlockSpec((tm, D), lambda i: (i, 0))
    return pl.pallas_call(
        body, name="out_even", grid=(L // tm,), in_specs=[row, row, row, _full((DI, D))],
        out_specs=row, out_shape=SDS((L, D), F32), compiler_params=_cp(32, ("arbitrary",)),
    )(x, ya, yb, wout)


def _sgu_core(pv, gain, ws_ref, bs_ref):
    pu, pvv, z = pv[:, :DI], pv[:, DI:2 * DI], pv[:, 2 * DI:]
    u, gu = _gelu_and_grad(pu)
    v, gv = _gelu_and_grad(pvv)
    mu = jnp.mean(v, axis=-1, keepdims=True)
    vc = v - mu
    rstd = lax.rsqrt(jnp.mean(vc * vc, axis=-1, keepdims=True) + EPS)
    vhat = vc * rstd
    vn = vhat * gain
    t = lax.broadcasted_iota(jnp.int32, (CH, CH), 0)
    s_ = lax.broadcasted_iota(jnp.int32, (CH, CH), 1)
    mask = t >= s_
    wm = [jnp.where(mask, ws_ref[g], 0.0).astype(MXU) for g in range(SG)]
    s = jnp.concatenate([_mm(wm[g], vn[:, g * SGD:(g + 1) * SGD]) + bs_ref[g] for g in range(SG)], axis=1)
    return gu, gv, z, u, vhat, rstd, vn, mask, wm, s


def _sgu_fwd_bwd(p2, x1, gain, wsp, bsp, wout, fnorm, tgt):
    L = p2.shape[0]

    def body(p_ref, x1_ref, g_ref, ws_ref, bs_ref, wo_ref, fn_ref, t_ref,
             dp_ref, y_ref, dx2_ref, gg_ref, gws_ref, gbs_ref, gfn_ref, loss_ref):
        @pl.when(pl.program_id(0) == 0)
        def _():
            gg_ref[...] = jnp.zeros_like(gg_ref)
            gws_ref[...] = jnp.zeros_like(gws_ref)
            gbs_ref[...] = jnp.zeros_like(gbs_ref)
            gfn_ref[...] = jnp.zeros_like(gfn_ref)
            loss_ref[...] = jnp.zeros_like(loss_ref)

        gain = g_ref[...]
        gu, gv, z, u, vhat, rstd, vn, mask, wm, s = _sgu_core(p_ref[...], gain, ws_ref, bs_ref)
        act, dact = _silu_and_grad(z)
        y = (u * s * act).astype(MXU)
        y_ref[...] = y
        x2 = x1_ref[...] + jnp.dot(y, wo_ref[...], preferred_element_type=F32)
        xhat, r = _rms(x2)
        fn = fn_ref[...]
        e = xhat * fn - t_ref[...]
        loss_ref[...] += 0.5 * jnp.sum(jnp.mean(e * e, axis=-1, keepdims=True), axis=0, keepdims=True)
        do = e * (1.0 / D)
        gfn_ref[...] += jnp.sum(do * xhat, axis=0, keepdims=True)
        dxhat = do * fn
        dx2 = r * (dxhat - xhat * jnp.mean(dxhat * xhat, axis=-1, keepdims=True))
        dx2_ref[...] = dx2
        dy = _mm_nt(dx2, wo_ref[...])
        du = dy * s * act
        ds = dy * u * act
        dz = dy * u * s * dact
        dvn = []
        for g in range(SG):
            ds_g = ds[:, g * SGD:(g + 1) * SGD]
            vn_g = vn[:, g * SGD:(g + 1) * SGD]
            gbs_ref[g] += jnp.sum(ds_g, axis=1, keepdims=True)
            gws_ref[g] += jnp.where(mask, _mm_nt(ds_g, vn_g), 0.0)
            dvn.append(_mm_tn(wm[g], ds_g))
        dvn = jnp.concatenate(dvn, axis=1)
        gg_ref[...] += jnp.sum(dvn * vhat, axis=0, keepdims=True)
        dvhat = dvn * gain
        dv = rstd * (dvhat - jnp.mean(dvhat, axis=-1, keepdims=True)
                     - vhat * jnp.mean(dvhat * vhat, axis=-1, keepdims=True))
        dp_ref[...] = jnp.concatenate([du * gu, dv * gv, dz], axis=1).astype(dp_ref.dtype)

    row = pl.BlockSpec((CH, D), lambda i: (i, 0))
    wide = pl.BlockSpec((CH, NIN), lambda i: (i, 0))
    return pl.pallas_call(
        body, name="sgu_fwd_bwd", grid=(L // CH,),
        in_specs=[wide, row, _full((1, DI)), _full((SG, CH, CH)), _full((SG, CH, 1)), _full((DI, D)),
                  _full((1, D)), row],
        out_specs=[wide, pl.BlockSpec((CH, DI), lambda i: (i, 0)), row,
                   _full((1, DI)), _full((SG, CH, CH)), _full((SG, CH, 1)), _full((1, D)), _full((1, 1))],
        out_shape=[SDS((L, NIN), MXU), SDS((L, DI), MXU), SDS((L, D), F32), SDS((1, DI), F32),
                   SDS((SG, CH, CH), F32), SDS((SG, CH, 1), F32), SDS((1, D), F32), SDS((1, 1), F32)],
        compiler_params=_cp(48, ("arbitrary",)),
    )(p2, x1, gain, wsp, bsp, wout, fnorm, tgt)


def _my_index():
    return 4 * lax.axis_index("x") + 2 * lax.axis_index("y") + lax.axis_index("c")


def _ordered_sum(land_ref, own, me):
    g = None
    for s in range(NDEV):
        part = jnp.where(me == s, own, land_ref[s].astype(F32))
        g = part if g is None else g + part
    return g


def _adamw_math(w, m, v, g):
    mn = ADAM_B1 * m + (1.0 - ADAM_B1) * g
    vn = ADAM_B2 * v + (1.0 - ADAM_B2) * (g * g)
    mhat = mn / BC1
    vhat = vn / BC2
    return g, -ADAM_LR * (mhat / (jnp.sqrt(vhat) + ADAM_EPS) + ADAM_WD * w), mn, vn


def _adamw(w, m, v, land, own, name):
    R, C = w.shape
    tr = R
    for cand in (256, 128, 64, 32, 16, 8):
        if R % cand == 0 and R > cand:
            tr = cand
            break

    def body(w_ref, m_ref, v_ref, land_ref, own_ref, g_ref, d_ref, mo_ref, vo_ref):
        g = _ordered_sum(land_ref, own_ref[...].astype(F32), _my_index())
        for o, val in zip((g_ref, d_ref, mo_ref, vo_ref), _adamw_math(w_ref[...], m_ref[...], v_ref[...], g)):
            o[...] = val

    blk = pl.BlockSpec((tr, C), lambda i: (i, 0))
    out = SDS((R, C), F32)
    return pl.pallas_call(
        body, name=name, grid=(R // tr,),
        in_specs=[blk, blk, blk, pl.BlockSpec((NDEV, tr, C), lambda i: (0, i, 0)), blk],
        out_specs=[blk, blk, blk, blk], out_shape=[out, out, out, out],
        compiler_params=_cp(40, ("arbitrary",)),
    )(w, m, v, land, own)


def _adamw_many(ws, ms, vs, lands, owns, name):
    k = len(ws)

    def body(*refs):
        ins, outs = refs[:5 * k], refs[5 * k:]
        me = _my_index()
        for i in range(k):
            w_ref, m_ref, v_ref, land_ref, own_ref = (ins[j * k + i] for j in range(5))
            g = _ordered_sum(land_ref, own_ref[...], me)
            for j, val in enumerate(_adamw_math(w_ref[...], m_ref[...], v_ref[...], g)):
                outs[j * k + i][...] = val

    out_shape = [SDS(w.shape, F32) for _ in range(4) for w in ws]
    res = pl.pallas_call(body, name=name, out_shape=out_shape, compiler_params=_cp(60))(*ws, *ms, *vs, *lands, *owns)
    return [res[j * k:(j + 1) * k] for j in range(4)]


MESH = pl.DeviceIdType.MESH
HBM_SPEC = pl.BlockSpec(memory_space=pltpu.HBM)
SEM_SPEC = pl.BlockSpec(memory_space=pltpu.SEMAPHORE)
EFFECT = pltpu.SideEffectType.DATAFLOW_SIDE_EFFECTING


def _me_and_peers():
    x, y, c = lax.axis_index("x"), lax.axis_index("y"), lax.axis_index("c")
    me = 4 * x + 2 * y + c
    peers = []
    for r in range(1, NDEV):
        px, py, pc = x ^ ((r >> 2) & 1), y ^ ((r >> 1) & 1), c ^ (r & 1)
        peers.append(((px, py, pc), 4 * px + 2 * py + pc))
    return me, peers


def _land_shape(a, scatter):
    return (NDEV,) + (a.shape[1:] if scatter else a.shape)


def _remote(src, dst, send_sems, recv_sems, r, k, n, dev):
    i = r * n + k
    return pltpu.make_async_remote_copy(src_ref=src, dst_ref=dst, send_sem=send_sems.at[i], recv_sem=recv_sems.at[i],
                                        device_id=dev, device_id_type=MESH)


def _exchange(arrays, scatter, name):
    n = len(arrays)
    out_shape = [SDS(_land_shape(a, scatter), a.dtype) for a in arrays]

    def body(*refs):
        ins, outs = refs[:n], refs[n:2 * n]
        send_sems, recv_sems, loc_sems = refs[2 * n:]
        me, peers = _me_and_peers()
        local = []
        for k in range(n):
            src = ins[k].at[me] if scatter else ins[k]
            cp = pltpu.make_async_copy(src, outs[k].at[me], loc_sems.at[k])
            cp.start()
            local.append(cp)
        sends = []
        for r, (dev, lin) in enumerate(peers):
            for k in range(n):
                src = ins[k].at[lin] if scatter else ins[k]
                cp = _remote(src, outs[k].at[me], send_sems, recv_sems, r, k, n, dev)
                cp.start()
                sends.append(cp)
        for r, (dev, lin) in enumerate(peers):
            for k in range(n):
                src = ins[k].at[me] if scatter else ins[k]
                _remote(src, outs[k].at[lin], send_sems, recv_sems, r, k, n, dev).wait_recv()
        for cp in sends:
            cp.wait_send()
        for cp in local:
            cp.wait()

    return pl.pallas_call(
        body, name=name, in_specs=[HBM_SPEC] * n, out_specs=[HBM_SPEC] * n, out_shape=out_shape,
        scratch_shapes=[pltpu.SemaphoreType.DMA(((NDEV - 1) * n,)), pltpu.SemaphoreType.DMA(((NDEV - 1) * n,)),
                        pltpu.SemaphoreType.DMA((n,))],
    )(*arrays)


def _exchange_start(arrays, scatter, name):
    n = len(arrays)
    lands = [lax.empty(_land_shape(a, scatter), a.dtype) for a in arrays]

    def body(*refs):
        ins, lnd = refs[:n], refs[n:2 * n]
        send_sems, recv_sems, own_sems = refs[2 * n:2 * n + 3]
        token = refs[-1]
        me, peers = _me_and_peers()
        for r, (dev, lin) in enumerate(peers):
            for k in range(n):
                src = ins[k].at[lin] if scatter else ins[k]
                _remote(src, lnd[k].at[me], send_sems, recv_sems, r, k, n, dev).start()
        if not scatter:
            for k in range(n):
                pltpu.make_async_copy(ins[k], lnd[k].at[me], own_sems.at[k]).start()
        token[...] = jnp.zeros_like(token)

    sem = pltpu.SemaphoreType.DMA(((NDEV - 1) * n,))
    outs = pl.pallas_call(
        body, name=name,
        out_shape=(sem, sem, pltpu.SemaphoreType.DMA((n,)), *[pltpu.HBM(a.shape, a.dtype) for a in arrays],
                   *[pltpu.HBM(l.shape, l.dtype) for l in lands], SDS((8, 128), F32)),
        in_specs=[HBM_SPEC] * (2 * n),
        out_specs=(SEM_SPEC, SEM_SPEC, SEM_SPEC, *[HBM_SPEC] * (2 * n), pl.BlockSpec(memory_space=pltpu.VMEM)),
        input_output_aliases={k: 3 + k for k in range(2 * n)},
        compiler_params=pltpu.CompilerParams(has_side_effects=EFFECT),
    )(*[pltpu.with_memory_space_constraint(a, pltpu.HBM) for a in arrays],
      *[pltpu.with_memory_space_constraint(l, pltpu.HBM) for l in lands])
    return (n, scatter, outs[0], outs[1], outs[2], outs[3:3 + n], outs[3 + n:3 + 2 * n]), outs[-1]


def _exchange_wait(handle, after, name):
    n, scatter, send_sems, recv_sems, own_sems, thru, lands = handle
    after = tuple(after)

    def body(*refs):
        ins, lnd = refs[:n], refs[n:2 * n]
        send_sems, recv_sems, own_sems = refs[2 * n:2 * n + 3]
        me, peers = _me_and_peers()
        for r, (dev, lin) in enumerate(peers):
            for k in range(n):
                src = ins[k].at[lin] if scatter else ins[k]
                cp = _remote(src, lnd[k].at[lin], send_sems, recv_sems, r, k, n, dev)
                cp.wait_send()
                cp.wait_recv()
        if not scatter:
            for k in range(n):
                pltpu.make_async_copy(ins[k], lnd[k].at[me], own_sems.at[k]).wait()

    outs = pl.pallas_call(
        body, name=name,
        out_shape=(*[pltpu.HBM(a.shape, a.dtype) for a in thru], *[pltpu.HBM(l.shape, l.dtype) for l in lands]),
        in_specs=[HBM_SPEC] * (2 * n) + [SEM_SPEC, SEM_SPEC, SEM_SPEC] + [ANY_SPEC] * len(after),
        out_specs=tuple([HBM_SPEC] * (2 * n)),
        input_output_aliases={k: k for k in range(2 * n)},
        compiler_params=pltpu.CompilerParams(has_side_effects=EFFECT),
    )(*thru, *lands, send_sems, recv_sems, own_sems, *after)
    return list(outs[:n]), list(outs[n:])


CHIP_RELATIONS = (2, 4, 6)


def _peer(r):
    x, y, c = lax.axis_index("x"), lax.axis_index("y"), lax.axis_index("c")
    px, py, pc = x ^ ((r >> 2) & 1), y ^ ((r >> 1) & 1), c ^ (r & 1)
    return (px, py, pc), 4 * px + 2 * py + pc


def _copy(src, dst, send_sems, recv_sems, i, dev):
    return pltpu.make_async_remote_copy(src_ref=src, dst_ref=dst, send_sem=send_sems.at[i], recv_sem=recv_sems.at[i],
                                        device_id=dev, device_id_type=MESH)


def _gather2_start(a, name):
    land = lax.empty((NDEV,) + a.shape, a.dtype)

    def body(own, lnd, send_sems, recv_sems, own_sem, own_thru, lnd_thru, token):
        me = _my_index()
        for i, r in enumerate((1,) + CHIP_RELATIONS):
            dev, _ = _peer(r)
            _copy(own, lnd.at[me], send_sems, recv_sems, i, dev).start()
        pltpu.make_async_copy(own, lnd.at[me], own_sem.at[0]).start()
        token[...] = jnp.zeros_like(token)

    sem4 = pltpu.SemaphoreType.DMA((4,))
    outs = pl.pallas_call(
        body, name=name,
        out_shape=(sem4, sem4, pltpu.SemaphoreType.DMA((1,)), pltpu.HBM(a.shape, a.dtype),
                   pltpu.HBM(land.shape, land.dtype), SDS((8, 128), F32)),
        in_specs=[HBM_SPEC, HBM_SPEC],
        out_specs=(SEM_SPEC, SEM_SPEC, SEM_SPEC, HBM_SPEC, HBM_SPEC, pl.BlockSpec(memory_space=pltpu.VMEM)),
        input_output_aliases={0: 3, 1: 4},
        compiler_params=pltpu.CompilerParams(has_side_effects=EFFECT),
    )(pltpu.with_memory_space_constraint(a, pltpu.HBM), pltpu.with_memory_space_constraint(land, pltpu.HBM))
    return outs[:5], outs[5]


def _gather2_forward(handle, after, name):
    send_sems, recv_sems, own_sem, own, land = handle
    after = tuple(after)

    def body(lnd, recv_sems, *rest):
        send2, recv2, lnd_thru = rest[len(after):]
        sib, _ = _peer(1)
        for k, r in enumerate(CHIP_RELATIONS):
            dev, lin = _peer(r)
            _copy(lnd.at[lin], lnd.at[lin], recv_sems, recv_sems, 1 + k, dev).wait_recv()
            _copy(lnd.at[lin], lnd.at[lin], send2, recv2, k, sib).start()

    sem3 = pltpu.SemaphoreType.DMA((3,))
    send2, recv2, land = pl.pallas_call(
        body, name=name,
        out_shape=(sem3, sem3, pltpu.HBM(land.shape, land.dtype)),
        in_specs=[HBM_SPEC, SEM_SPEC] + [ANY_SPEC] * len(after),
        out_specs=(SEM_SPEC, SEM_SPEC, HBM_SPEC),
        input_output_aliases={0: 2},
        compiler_params=pltpu.CompilerParams(has_side_effects=EFFECT),
    )(land, recv_sems, *after)
    return send_sems, recv_sems, own_sem, send2, recv2, own, land


def _gather2_wait(handle, name):
    send_sems, recv_sems, own_sem, send2, recv2, own, land = handle

    def body(own_ref, lnd, send_sems, recv_sems, own_sem, send2, recv2, own_thru, lnd_thru):
        me = _my_index()
        sib, sib_lin = _peer(1)
        for i, r in enumerate((1,) + CHIP_RELATIONS):
            dev, _ = _peer(r)
            _copy(own_ref, lnd.at[me], send_sems, recv_sems, i, dev).wait_send()
        _copy(own_ref, lnd.at[sib_lin], send_sems, recv_sems, 0, sib).wait_recv()
        for k, r in enumerate(CHIP_RELATIONS):
            _, lin = _peer(r)
            _, lin_other = _peer(r ^ 1)
            _copy(lnd.at[lin], lnd.at[lin], send2, recv2, k, sib).wait_send()
            _copy(lnd.at[lin_other], lnd.at[lin_other], send2, recv2, k, sib).wait_recv()
        pltpu.make_async_copy(own_ref, lnd.at[me], own_sem.at[0]).wait()

    outs = pl.pallas_call(
        body, name=name,
        out_shape=(pltpu.HBM(own.shape, own.dtype), pltpu.HBM(land.shape, land.dtype)),
        in_specs=[HBM_SPEC, HBM_SPEC] + [SEM_SPEC] * 5,
        out_specs=(HBM_SPEC, HBM_SPEC),
        input_output_aliases={0: 0, 1: 1},
        compiler_params=pltpu.CompilerParams(has_side_effects=EFFECT),
    )(own, land, send_sems, recv_sems, own_sem, send2, recv2)
    return outs[1]


def _local_step(x, tgt, norm_even, first_weight, lam_re, lam_im, log_dt, b_re, b_im, c_re, c_im, s5_d, bglu,
                ret_gain, wsp, bsp, fnorm, late_weights, emit, start_token=None):
    L = x.shape[0]
    lr3, li3 = lam_re.reshape(G, 1, P), lam_im.reshape(G, 1, P)
    dt3 = log_dt.reshape(G, 1, 1)
    br3, bi3 = jnp.swapaxes(b_re, 1, 2), jnp.swapaxes(b_im, 1, 2)
    abr3, abi3, bbr3, bbi3 = _s5_disc(lr3, li3, dt3, br3, bi3)
    bb = jnp.concatenate([_embed(bbr3), _embed(bbi3)], axis=2).astype(MXU)
    cm = jnp.concatenate([_embed(jnp.swapaxes(c_re, 1, 2)), -_embed(jnp.swapaxes(c_im, 1, 2))], axis=1).astype(MXU)
    abr, abi = abr3.reshape(1, NSTATE), abi3.reshape(1, NSTATE)
    pwr, pwi, _, _ = _s5_tables(abr, abi, NT, "s5_tables_step")
    par, pai, pbr, pbi = _s5_tables(pwr[NT - 1:NT], pwi[NT - 1:NT], CH, "s5_tables_chunk")
    inv = (ROPE_BASE ** (-jnp.arange(DK // 2, dtype=F32) / (DK // 2))).reshape(1, DK // 2)
    cos, sin = _rope_tables(L, inv)
    bsp3 = bsp.reshape(SG, CH, 1)

    def dep(token):
        return NO_DEPS if token is None else (token,)

    win_e = first_weight((cos, pbi, cm))
    p, u3 = _in_proj(x, norm_even, win_e, "in_even", dep(start_token), lane_blocks=True)
    ypre, s5_states = _s5_scan_fwd(u3, bb, cm, abr, abi, pwr, pwi, par, pai, s5_d)
    yb, ret_states = _ret_fwd(p, cos, sin, ret_gain)
    wglu, wout_e, norm_odd, win_o, sgu_gain, wout_o = late_weights((ypre, yb))
    ya = _s5_gate_fwd(ypre, p, wglu, bglu)
    x1 = _out_even(x, ya, yb, wout_e)
    p2 = _in_proj(x1, norm_odd, win_o, "in_odd")
    dp2, y_o, dx2, g_sgu_gain, g_wsp, g_bsp, g_fnorm, loss = _sgu_fwd_bwd(
        p2, x1, sgu_gain, wsp, bsp3, wout_o, fnorm, tgt)

    g_wout_o = _wgrad_rows([y_o], dx2, "wgrad_out_odd")
    g_win_o = _wgrad_cols(x1, norm_odd, dp2, "wgrad_in_odd")
    tok = emit("odd", dict(w_in_odd=g_win_o, w_out_odd=g_wout_o))
    dx1, g_norm_odd = _in_proj_bwd_x(dp2, x1, norm_odd, win_o, dx2, "in_odd_bwd", dep(tok))
    tok = emit("small_odd", dict(norm_odd=g_norm_odd, sgu_norm_gain=g_sgu_gain, sgu_w_spatial=g_wsp,
                                 sgu_b_spatial=g_bsp.reshape(SG, CH), final_norm=g_fnorm))

    dypre, daz, yg, dt, ya2, g_bglu = _s5_gate_bwd(ypre, p, dx1, wout_e, wglu, bglu, dep(tok))
    g_wglu = _wgrad_rows([yg], dt, "wgrad_glu")
    tok = emit("glu", dict(s5_w_glu=g_wglu))
    du, g_d, g_cm, g_bb, g_ar, g_ai = _s5_scan_bwd(u3, dypre, s5_states, bb, cm, jnp.swapaxes(cm, 1, 2), abr, abi,
                                                   pwr, pwi, par, pai, pbr, pbi, s5_d, dep(tok))
    dbbr3 = _diag_blocks(g_bb[:, :, :SW], HG, P)
    dbbi3 = _diag_blocks(g_bb[:, :, SW:], HG, P)
    g_c_re = jnp.swapaxes(_diag_blocks(g_cm[:, :SW, :], P, HG), 1, 2)
    g_c_im = -jnp.swapaxes(_diag_blocks(g_cm[:, SW:, :], P, HG), 1, 2)
    g_lr3, g_li3, g_dt3, g_br3, g_bi3 = _s5_disc_bwd(
        lr3, li3, dt3, br3, bi3, g_ar.reshape(G, 1, P), g_ai.reshape(G, 1, P), dbbr3, dbbi3)
    tok = emit("small_s5", dict(
        s5_lam_re=g_lr3.reshape(G, P), s5_lam_im=g_li3.reshape(G, P), s5_log_dt=g_dt3.reshape(1, G),
        s5_b_re=g_br3, s5_b_im=g_bi3, s5_c_re=g_c_re, s5_c_im=g_c_im, s5_d=g_d, s5_b_glu=g_bglu))
    dp, yb2, g_ret_gain = _ret_bwd(p, cos, sin, ret_gain, ret_states, dx1, wout_e, du, daz, dep(tok))
    g_win_e = _wgrad_cols(x, norm_even, dp, "wgrad_in_even")
    tok = emit("even_cols", dict(w_in_even=g_win_e))
    g_wout_e = _wgrad_rows([ya2, yb2], dx1, "wgrad_out_even", dep(tok))
    tok = emit("even_rows", dict(w_out_even=g_wout_e))
    dx, g_norm_even = _in_proj_bwd_x(dp, x, norm_even, win_e, dx1, "in_even_bwd", dep(tok))
    emit("last", dict(ret_gn_gain=g_ret_gain, norm_even=g_norm_even))
    return loss, dx


WEIGHTS = ['norm_even', 'w_in_even', 's5_lam_re', 's5_lam_im', 's5_log_dt', 's5_b_re', 's5_b_im', 's5_c_re',
           's5_c_im', 's5_d', 's5_w_glu', 's5_b_glu', 'ret_gn_gain', 'w_out_even', 'norm_odd', 'w_in_odd',
           'sgu_norm_gain', 'sgu_w_spatial', 'sgu_b_spatial', 'w_out_odd', 'final_norm']
BIG = ['w_in_even', 's5_w_glu', 'w_out_even', 'w_in_odd', 'w_out_odd']
SHARDED_SMALL = {'norm_odd': D // NDEV, 'sgu_norm_gain': DI // NDEV}
SCATTER_STAGES = ("odd", "glu", "even_cols", "even_rows")
GATHER_STAGES = ("small_odd", "small_s5")


def _view(n, a):
    if n in ('s5_b_re', 's5_b_im'):
        return jnp.swapaxes(a[0], 1, 2)
    if n == 'final_norm':
        return a.reshape(1, D)
    return a[0] if a.ndim >= 3 else a


def _unview(n, t, shape):
    if n in ('s5_b_re', 's5_b_im'):
        return jnp.swapaxes(t, 1, 2)[None]
    return t.reshape(shape)


def kernel(x, norm_even, w_in_even, s5_lam_re, s5_lam_im, s5_log_dt, s5_b_re, s5_b_im, s5_c_re, s5_c_im, s5_d, s5_w_glu, s5_b_glu, ret_gn_gain, w_out_even, norm_odd, w_in_odd, sgu_norm_gain, sgu_w_spatial, sgu_b_spatial, w_out_odd, final_norm, loss_target, m_norm_even, m_w_in_even, m_s5_lam_re, m_s5_lam_im, m_s5_log_dt, m_s5_b_re, m_s5_b_im, m_s5_c_re, m_s5_c_im, m_s5_d, m_s5_w_glu, m_s5_b_glu, m_ret_gn_gain, m_w_out_even, m_norm_odd, m_w_in_odd, m_sgu_norm_gain, m_sgu_w_spatial, m_sgu_b_spatial, m_w_out_odd, m_final_norm, v_norm_even, v_w_in_even, v_s5_lam_re, v_s5_lam_im, v_s5_log_dt, v_s5_b_re, v_s5_b_im, v_s5_c_re, v_s5_c_im, v_s5_d, v_s5_w_glu, v_s5_b_glu, v_ret_gn_gain, v_w_out_even, v_norm_odd, v_w_in_odd, v_sgu_norm_gain, v_sgu_w_spatial, v_sgu_b_spatial, v_w_out_odd, v_final_norm):
    args = dict(locals())
    w = {n: args[n] for n in WEIGHTS}
    m = {n: args["m_" + n] for n in WEIGHTS}
    v = {n: args["v_" + n] for n in WEIGHTS}
    me = _my_index()

    first_handle, _ = _gather2_start(w['w_in_even'][0].astype(MXU), "gather_first_start")

    def first_weight(after):
        return _gather2_wait(_gather2_forward(first_handle, after, "gather_first_forward"), "gather_first_wait")

    late_own = [w['s5_w_glu'][0].astype(MXU), w['w_out_even'][0].astype(MXU), w['norm_odd'],
                w['w_in_odd'][0].astype(MXU), w['sgu_norm_gain'], w['w_out_odd'][0].astype(MXU)]
    late_handle, start_token = _exchange_start(late_own, False, "gather_late_start")

    def late_weights(after):
        _, (wglu, wout_e, nodd, win_o, sgug, wout_o) = _exchange_wait(late_handle, after, "gather_late_wait")
        return (wglu.reshape(D, D), wout_e.reshape(DI, D), nodd.reshape(1, D), win_o, sgug.reshape(1, DI),
                wout_o.reshape(DI, D))

    pending = {}
    small_last = {}

    def emit(stage, grads):
        if stage == "last":
            small_last.update(grads)
            return None
        names = list(grads)
        handle, token = _exchange_start([grads[n] for n in names], stage in SCATTER_STAGES, stage + "_start")
        pending[stage] = (handle, names)
        return token

    loss, dx = _local_step(
        x[0], loss_target[0], w['norm_even'], first_weight, w['s5_lam_re'][0], w['s5_lam_im'][0], w['s5_log_dt'][0],
        w['s5_b_re'][0], w['s5_b_im'][0], w['s5_c_re'][0], w['s5_c_im'][0], w['s5_d'], w['s5_b_glu'],
        w['ret_gn_gain'], w['sgu_w_spatial'][0], w['sgu_b_spatial'][0], w['final_norm'].reshape(1, D),
        late_weights, emit, start_token)

    out_g, out_d, out_m, out_v = {}, {}, {}, {}
    after = dx
    for stage in SCATTER_STAGES:
        handle, names = pending[stage]
        sent, lands = _exchange_wait(handle, (after,), stage + "_wait")
        for n, land, stack in zip(names, lands, sent):
            shp = w[n].shape
            r, c = shp[1], shp[2]
            own = lax.dynamic_index_in_dim(stack, me, 0, keepdims=False)
            res = _adamw(w[n].reshape(r, c), m[n].reshape(r, c), v[n].reshape(r, c), land, own, "adamw_" + n)
            out_g[n], out_d[n], out_m[n], out_v[n] = (t.reshape(shp) for t in res)
            after = res[0]

    names, owns, lands = [], [], []
    for stage in GATHER_STAGES:
        handle, stage_names = pending[stage]
        sent, got = _exchange_wait(handle, (after,), stage + "_wait")
        names, owns, lands = names + stage_names, owns + sent, lands + got
    last_names = list(small_last)
    last = _exchange([small_last[n] for n in last_names], False, "gather_last")
    names, owns, lands = names + last_names, owns + [small_last[n] for n in last_names], lands + list(last)
    for i, n in enumerate(names):
        if n in SHARDED_SMALL:
            width = SHARDED_SMALL[n]
            owns[i] = lax.dynamic_slice_in_dim(owns[i], me * width, width, axis=1)
            lands[i] = lax.dynamic_slice_in_dim(lands[i], me * width, width, axis=2)
    res = _adamw_many([_view(n, w[n]) for n in names], [_view(n, m[n]) for n in names],
                      [_view(n, v[n]) for n in names], lands, owns, "adamw_small")
    for dst, vals in zip((out_g, out_d, out_m, out_v), res):
        for n, t in zip(names, vals):
            dst[n] = _unview(n, t, w[n].shape)

    loss_total = lax.psum(loss[0, 0], AXES)
    return (loss_total, dx[None], *[out_g[n] for n in WEIGHTS], *[out_d[n] for n in WEIGHTS],
            *[out_m[n] for n in WEIGHTS], *[out_v[n] for n in WEIGHTS])
```

```python
import math

import jax
import jax.numpy as jnp
from jax import lax
from jax.experimental import pallas as pl
from jax.experimental.pallas import tpu as pltpu

F32 = jnp.float32
MXU = jnp.bfloat16
AXES = ("x", "y", "c")
NDEV = 8
D = 1024
NIN = 6144
WIN_BLK = NIN // NDEV
DI = 2048
G, P, HG = 64, 64, 16
GB = 8
NJ = G // GB
SW = GB * P
UW = GB * HG
NSTATE = G * P
HEADS, DK = 4, 256
CH = 128
SG, SGD = 4, 512
EPS = 1e-6
ROPE_BASE = 10000.0
VMEM_CAP_V7X = 64 * 1024 * 1024
LOG_G = [math.log1p(-2.0 ** (-5.0 - h)) for h in range(HEADS)]
GELU_C = math.sqrt(2.0 / math.pi)

ADAM_LR, ADAM_B1, ADAM_B2, ADAM_EPS, ADAM_WD, ADAM_STEP = 0.001, 0.9, 0.999, 1e-08, 0.01, 10
BC1 = 1.0 - ADAM_B1 ** ADAM_STEP
BC2 = 1.0 - ADAM_B2 ** ADAM_STEP

SDS = jax.ShapeDtypeStruct
ARB2 = ("arbitrary", "arbitrary")


def _cp(vmem_mib, sem=None):
    kw = dict(vmem_limit_bytes=min(vmem_mib * 1024 * 1024, VMEM_CAP_V7X - 4 * 1024 * 1024))
    if sem is not None:
        kw["dimension_semantics"] = sem
    return pltpu.CompilerParams(**kw)


def _mm(a, b):
    return jnp.dot(a.astype(MXU), b.astype(MXU), preferred_element_type=F32)


def _mm_nt(a, b):
    return lax.dot_general(a.astype(MXU), b.astype(MXU), (((1,), (1,)), ((), ())), preferred_element_type=F32)


def _mm_tn(a, b):
    return lax.dot_general(a.astype(MXU), b.astype(MXU), (((0,), (0,)), ((), ())), preferred_element_type=F32)


def _gelu(x):
    return _gelu_and_grad(x)[0]


def _gelu_and_grad(x):
    x2 = x * x
    th = jnp.tanh(GELU_C * x * (1.0 + 0.044715 * x2))
    hp = 0.5 * (1.0 + th)
    return x * hp, hp + 0.5 * x * (1.0 - th * th) * GELU_C * (1.0 + 3.0 * 0.044715 * x2)


def _silu_and_grad(x):
    s = jax.nn.sigmoid(x)
    return x * s, s * (1.0 + x * (1.0 - s))


def _full(shape):
    nd = len(shape)
    return pl.BlockSpec(shape, lambda *_: (0,) * nd)


def _rms(xf):
    r = lax.rsqrt(jnp.mean(xf * xf, axis=-1, keepdims=True) + EPS)
    return xf * r, r


ANY_SPEC = pl.BlockSpec(memory_space=pl.ANY)
NO_DEPS = ()


def _load_once(src_hbm, dst_vmem, sem):
    @pl.when(pl.program_id(0) == 0)
    def _():
        cp = pltpu.make_async_copy(src_hbm, dst_vmem, sem)
        cp.start()
        cp.wait()


def _lane_blocks(L):
    return SDS((NJ, L, UW), F32)


def _lane_block_spec(rows, index):
    return pl.BlockSpec((NJ, rows, UW), lambda i: (0, index(i), 0))


def _from_lane_blocks(ref):
    return jnp.concatenate([ref[j] for j in range(NJ)], axis=1)


def _to_lane_blocks(ref, v):
    for j in range(NJ):
        ref[j] = v[:, j * UW:(j + 1) * UW].astype(ref.dtype)


def _in_proj(x, gain, wst, name, deps=NO_DEPS, lane_blocks=False):
    L = x.shape[0]
    tm = min(512, L)

    def body(x_ref, g_ref, w_hbm, *rest):
        outs = rest[len(deps):]
        o_ref, w_scr, sem = outs[0], outs[-2], outs[-1]
        _load_once(w_hbm, w_scr, sem)
        xhat, _ = _rms(x_ref[...])
        h = (xhat * g_ref[...]).astype(MXU)
        for c in range(NDEV):
            o_ref[:, c * WIN_BLK:(c + 1) * WIN_BLK] = jnp.dot(h, w_scr[c], preferred_element_type=F32)
        if lane_blocks:
            _to_lane_blocks(outs[1], o_ref[:, 0:D])

    p_spec, p_shape = pl.BlockSpec((tm, NIN), lambda i: (i, 0)), SDS((L, NIN), F32)
    return pl.pallas_call(
        body, name=name, grid=(L // tm,),
        in_specs=[pl.BlockSpec((tm, D), lambda i: (i, 0)), _full((1, D)), ANY_SPEC] + [ANY_SPEC] * len(deps),
        out_specs=[p_spec, _lane_block_spec(tm, lambda i: i)] if lane_blocks else p_spec,
        out_shape=[p_shape, _lane_blocks(L)] if lane_blocks else p_shape,
        scratch_shapes=[pltpu.VMEM((NDEV, D, WIN_BLK), MXU), pltpu.SemaphoreType.DMA(())],
        compiler_params=_cp(58, ("arbitrary",)),
    )(x, gain, wst, *deps)


def _in_proj_bwd_x(dp, x, gain, wst, dres, name, deps=NO_DEPS):
    L = x.shape[0]
    tm = min(512, L)

    def body(dp_ref, x_ref, g_ref, w_hbm, dres_ref, *rest):
        dx_ref, gg_ref, w_scr, sem = rest[len(deps):]
        _load_once(w_hbm, w_scr, sem)

        @pl.when(pl.program_id(0) == 0)
        def _():
            gg_ref[...] = jnp.zeros_like(gg_ref)

        dh = _mm_nt(dp_ref[:, 0:WIN_BLK], w_scr[0])
        for c in range(1, NDEV):
            dh += _mm_nt(dp_ref[:, c * WIN_BLK:(c + 1) * WIN_BLK], w_scr[c])
        xhat, r = _rms(x_ref[...])
        dxhat = dh * g_ref[...]
        dx_ref[...] = dres_ref[...] + r * (dxhat - xhat * jnp.mean(dxhat * xhat, axis=-1, keepdims=True))
        gg_ref[...] += jnp.sum(dh * xhat, axis=0, keepdims=True)

    row = pl.BlockSpec((tm, D), lambda i: (i, 0))
    return pl.pallas_call(
        body, name=name, grid=(L // tm,),
        in_specs=[pl.BlockSpec((tm, NIN), lambda i: (i, 0)), row, _full((1, D)), ANY_SPEC, row]
        + [ANY_SPEC] * len(deps),
        out_specs=[row, _full((1, D))],
        out_shape=[SDS((L, D), F32), SDS((1, D), F32)],
        scratch_shapes=[pltpu.VMEM((NDEV, D, WIN_BLK), MXU), pltpu.SemaphoreType.DMA(())],
        compiler_params=_cp(56, ("arbitrary",)),
    )(dp, x, gain, wst, dres, *deps)


def _wgrad_cols(x, gain, dp, name, deps=NO_DEPS):
    L = x.shape[0]
    tk = min(512, L)
    nk = L // tk
    halves = 2
    nh = NDEV // halves

    def body(x_ref, g_ref, dp_ref, *rest):
        o_ref, acc = rest[len(deps):]
        k = pl.program_id(1)

        @pl.when(k == 0)
        def _():
            acc[...] = jnp.zeros_like(acc)

        xhat, _ = _rms(x_ref[...])
        acc[...] += _mm_tn(xhat * g_ref[...], dp_ref[...])

        @pl.when(k == nk - 1)
        def _():
            for c in range(nh):
                o_ref[c] = acc[:, c * WIN_BLK:(c + 1) * WIN_BLK].astype(o_ref.dtype)

    return pl.pallas_call(
        body, name=name, grid=(halves, nk),
        in_specs=[pl.BlockSpec((tk, D), lambda n, k: (k, 0)), _full((1, D)),
                  pl.BlockSpec((tk, nh * WIN_BLK), lambda n, k: (k, n))] + [ANY_SPEC] * len(deps),
        out_specs=pl.BlockSpec((nh, D, WIN_BLK), lambda n, k: (n, 0, 0)),
        out_shape=SDS((NDEV, D, WIN_BLK), MXU),
        scratch_shapes=[pltpu.VMEM((D, nh * WIN_BLK), F32)],
        compiler_params=_cp(56, ARB2),
    )(x, gain, dp, *deps)


def _wgrad_rows(a_parts, b, name, deps=NO_DEPS):
    L, N = b.shape
    na = len(a_parts)
    widths = [a.shape[1] for a in a_parts]
    M = sum(widths)
    tk = min(512, L)
    nk = L // tk

    def body(*refs):
        a_refs, b_ref = refs[:na], refs[na]
        o_ref, acc = refs[na + 1 + len(deps):]
        k = pl.program_id(0)

        @pl.when(k == 0)
        def _():
            acc[...] = jnp.zeros_like(acc)

        bv = b_ref[...].astype(MXU)
        off = 0
        for a_ref, wd in zip(a_refs, widths):
            acc[off:off + wd, :] += _mm_tn(a_ref[...], bv)
            off += wd

        @pl.when(k == nk - 1)
        def _():
            o_ref[...] = acc[...].astype(o_ref.dtype).reshape(o_ref.shape)

    return pl.pallas_call(
        body, name=name, grid=(nk,),
        in_specs=[pl.BlockSpec((tk, wd), lambda k: (k, 0)) for wd in widths]
        + [pl.BlockSpec((tk, N), lambda k: (k, 0))] + [ANY_SPEC] * len(deps),
        out_specs=_full((NDEV, M // NDEV, N)),
        out_shape=SDS((NDEV, M // NDEV, N), MXU),
        scratch_shapes=[pltpu.VMEM((M, N), F32)],
        compiler_params=_cp(48, ("arbitrary",)),
    )(*a_parts, b, *deps)


def _s5_disc_fn(lr_raw, li, logdt, br, bi):
    lr = jnp.minimum(lr_raw, -1e-4)
    dt = jnp.exp(logdt)
    mag = jnp.exp(lr * dt)
    abr = mag * jnp.cos(li * dt)
    abi = mag * jnp.sin(li * dt)
    den = lr * lr + li * li
    nre = abr - 1.0
    nim = abi
    zr = (nre * lr + nim * li) / den
    zi = (nim * lr - nre * li) / den
    return abr, abi, zr * br - zi * bi, zr * bi + zi * br


def _s5_disc(lr, li, logdt, br, bi):
    def body(lr_ref, li_ref, dt_ref, br_ref, bi_ref, abr_ref, abi_ref, bbr_ref, bbi_ref):
        abr, abi, bbr, bbi = _s5_disc_fn(lr_ref[...], li_ref[...], dt_ref[...], br_ref[...], bi_ref[...])
        abr_ref[...] = abr
        abi_ref[...] = abi
        bbr_ref[...] = bbr
        bbi_ref[...] = bbi

    s1, s3 = SDS((G, 1, P), F32), SDS((G, HG, P), F32)
    return pl.pallas_call(body, name="s5_disc", out_shape=[s1, s1, s3, s3])(lr, li, logdt, br, bi)


def _s5_disc_bwd(lr, li, logdt, br, bi, dabr, dabi, dbbr, dbbi):
    def body(lr_ref, li_ref, dt_ref, br_ref, bi_ref, c0, c1, c2, c3, o0, o1, o2, o3, o4):
        _, vjp = jax.vjp(_s5_disc_fn, lr_ref[...], li_ref[...], dt_ref[...], br_ref[...], bi_ref[...])
        g = vjp((c0[...], c1[...], c2[...], c3[...]))
        for o, v in zip((o0, o1, o2, o3, o4), g):
            o[...] = v

    s1, s3 = SDS((G, 1, P), F32), SDS((G, HG, P), F32)
    return pl.pallas_call(body, name="s5_disc_bwd", out_shape=[s1, s1, SDS((G, 1, 1), F32), s3, s3])(
        lr, li, logdt, br, bi, dabr, dabi, dbbr, dbbi)


def _s5_tables(abr, abi, rows, name):
    def body(ar_ref, ai_ref, pfr, pfi, pbr, pbi):
        pfr[0:1, :] = ar_ref[...]
        pfi[0:1, :] = ai_ref[...]
        pbr[rows - 1:rows, :] = ar_ref[...]
        pbi[rows - 1:rows, :] = ai_ref[...]
        n = 1
        while n < rows:
            er, ei = pfr[n - 1:n, :], pfi[n - 1:n, :]
            xr, xi = pfr[0:n, :], pfi[0:n, :]
            pfr[n:2 * n, :] = er * xr - ei * xi
            pfi[n:2 * n, :] = er * xi + ei * xr
            yr, yi = pbr[rows - n:rows, :], pbi[rows - n:rows, :]
            pbr[rows - 2 * n:rows - n, :] = er * yr - ei * yi
            pbi[rows - 2 * n:rows - n, :] = er * yi + ei * yr
            n *= 2

    s = SDS((rows, NSTATE), F32)
    return pl.pallas_call(body, name=name, out_shape=[s, s, s, s], compiler_params=_cp(40))(abr, abi)


def _cscan(br, bi, pr_ref, pi_ref, reverse):
    T = br.shape[0]
    sign = -1.0 if reverse else 1.0
    row = lax.broadcasted_iota(jnp.int32, br.shape, 0)
    k = 1
    while k < T:
        akr = pr_ref[k - 1:k, :]
        aki = sign * pi_ref[k - 1:k, :]

        def shift(v):
            if k % 8 == 0:
                z = jnp.zeros((k, v.shape[1]), v.dtype)
                return jnp.concatenate([v[k:], z], 0) if reverse else jnp.concatenate([z, v[:T - k]], 0)
            if reverse:
                return jnp.where(row < T - k, pltpu.roll(v, T - k, 0), 0.0)
            return jnp.where(row >= k, pltpu.roll(v, k, 0), 0.0)

        sr, si = shift(br), shift(bi)
        br, bi = br + akr * sr - aki * si, bi + akr * si + aki * sr
        k *= 2
    return br, bi


def _embed(t):
    a, b = t.shape[1], t.shape[2]
    return jnp.einsum("jgab,gh->jgahb", t.reshape(NJ, GB, a, b), jnp.eye(GB, dtype=t.dtype)).reshape(NJ, GB * a, GB * b)


def _diag_blocks(t, a, b):
    return jnp.einsum("jgahb,gh->jgab", t.reshape(NJ, GB, a, GB, b), jnp.eye(GB, dtype=t.dtype)).reshape(G, a, b)


NT = 16


def _chunks(L):
    ncb = min(CH, L // NT)
    return ncb, NT * ncb


def _cmul_add(ar, ai, xr, xi, br, bi):
    return ar * xr - ai * xi + br, ar * xi + ai * xr + bi


def _pow_weights(w_ref, pwr_ref, pwi_ref, dst, adjoint):
    w = w_ref[...].astype(F32)
    wr, wi = w[:, :SW], w[:, SW:]
    for t in range(NT):
        k = t if adjoint else NT - 1 - t
        if k == 0:
            blk = w
        else:
            pr, pi = pwr_ref[k - 1:k, :], pwi_ref[k - 1:k, :]
            if adjoint:
                blk = jnp.concatenate([pr * wr + pi * wi, pr * wi - pi * wr], axis=1)
            else:
                blk = jnp.concatenate([pr * wr - pi * wi, pr * wi + pi * wr], axis=1)
        dst[t * UW:(t + 1) * UW, :] = blk.astype(dst.dtype)


def _s5_states(u_ref, bb_ref, bbp_scr, par_ref, pai_ref, cr, ci, ncb, bu_scr):
    us = [u_ref[pl.ds(t, ncb, stride=NT), :] for t in range(NT)]
    for t in range(NT):
        bu_scr[t] = _mm(us[t], bb_ref[...])
    e = _mm(jnp.concatenate(us, axis=1), bbp_scr[...])
    xr, xi = _cscan(e[:, :SW], e[:, SW:], par_ref, pai_ref, False)
    fr, fi = _cmul_add(par_ref[0:ncb, :], pai_ref[0:ncb, :], cr, ci, xr, xi)
    row = lax.broadcasted_iota(jnp.int32, fr.shape, 0)
    cinr = jnp.where(row >= 1, pltpu.roll(fr, 1, 0), cr)
    cini = jnp.where(row >= 1, pltpu.roll(fi, 1, 0), ci)
    return cinr, cini, jnp.concatenate([fr[ncb - 1:ncb, :], fi[ncb - 1:ncb, :]], axis=1)


def _s5_scan_fwd(u3, bb, cm, abr, abi, pwr, pwi, par, pai, dskip):
    L = u3.shape[1]
    ncb, tb = _chunks(L)
    nb = L // tb

    def body(u_ref, bb_ref, cm_ref, ar_ref, ai_ref, pwr_ref, pwi_ref, par_ref, pai_ref, d_ref, ypre_ref, st_ref,
             carry, bu_scr, bbp_scr):
        @pl.when(pl.program_id(1) == 0)
        def _():
            carry[...] = jnp.zeros_like(carry)
            _pow_weights(bb_ref, pwr_ref, pwi_ref, bbp_scr, False)

        c = carry[...]
        st_ref[...] = c
        ar, ai = ar_ref[...], ai_ref[...]
        sr, si, cnext = _s5_states(u_ref, bb_ref, bbp_scr, par_ref, pai_ref, c[:, :SW], c[:, SW:], ncb, bu_scr)
        carry[...] = cnext
        for t in range(NT):
            bu = bu_scr[t]
            sr, si = _cmul_add(ar, ai, sr, si, bu[:, :SW], bu[:, SW:])
            rows = pl.ds(t, ncb, stride=NT)
            ypre_ref[rows, :] = _mm(jnp.concatenate([sr, si], axis=1), cm_ref[...]) + d_ref[...] * u_ref[rows, :]

    tab = pl.BlockSpec((CH, SW), lambda j, i: (0, j))
    stp = pl.BlockSpec((NT, SW), lambda j, i: (0, j))
    vec = lambda w: pl.BlockSpec((1, w), lambda j, i: (0, j))
    return pl.pallas_call(
        body, name="s5_scan_fwd", grid=(NJ, nb),
        in_specs=[pl.BlockSpec((None, tb, UW), lambda j, i: (j, i, 0)),
                  pl.BlockSpec((None, UW, 2 * SW), lambda j, i: (j, 0, 0)),
                  pl.BlockSpec((None, 2 * SW, UW), lambda j, i: (j, 0, 0)),
                  vec(SW), vec(SW), stp, stp, tab, tab, vec(UW)],
        out_specs=[pl.BlockSpec((None, tb, UW), lambda j, i: (j, i, 0)),
                   pl.BlockSpec((None, None, 1, 2 * SW), lambda j, i: (j, i, 0, 0))],
        out_shape=[_lane_blocks(L), SDS((NJ, nb, 1, 2 * SW), F32)],
        scratch_shapes=[pltpu.VMEM((1, 2 * SW), F32), pltpu.VMEM((NT, ncb, 2 * SW), F32),
                        pltpu.VMEM((NT * UW, 2 * SW), MXU)],
        compiler_params=_cp(48, ARB2),
    )(u3, bb, cm, abr, abi, pwr, pwi, par, pai, dskip)


def _s5_gate_bwd(ypre3, p, dx1, wout_e, wglu, bglu, deps=NO_DEPS):
    L = p.shape[0]
    tm = min(256, L)

    def body(y_ref, az_ref, dx1_ref, wo_ref, wg_ref, bg_ref, *rest):
        dyp_ref, daz_ref, yg_ref, dt_ref, ya_ref, gbg_ref = rest[len(deps):]

        @pl.when(pl.program_id(0) == 0)
        def _():
            gbg_ref[...] = jnp.zeros_like(gbg_ref)

        yg, dgelu = _gelu_and_grad(_from_lane_blocks(y_ref))
        sg = jax.nn.sigmoid(_mm(yg, wg_ref[...]) + bg_ref[...])
        act, dact = _silu_and_grad(az_ref[...])
        y2 = yg * sg
        dya = _mm_nt(dx1_ref[...], wo_ref[...])
        daz_ref[...] = (dya * y2 * dact).astype(daz_ref.dtype)
        dy2 = dya * act
        dt = dy2 * yg * sg * (1.0 - sg)
        dyg = dy2 * sg + _mm_nt(dt, wg_ref[...])
        _to_lane_blocks(dyp_ref, dyg * dgelu)
        yg_ref[...] = yg.astype(yg_ref.dtype)
        dt_ref[...] = dt.astype(dt_ref.dtype)
        ya_ref[...] = (y2 * act).astype(ya_ref.dtype)
        gbg_ref[...] += jnp.sum(dt, axis=0, keepdims=True)

    row = pl.BlockSpec((tm, D), lambda i: (i, 0))
    return pl.pallas_call(
        body, name="s5_gate_bwd", grid=(L // tm,),
        in_specs=[_lane_block_spec(tm, lambda i: i), pl.BlockSpec((tm, D), lambda i: (i, 1)), row,
                  pl.BlockSpec((D, D), lambda i: (0, 0)), _full((D, D)), _full((1, D))] + [ANY_SPEC] * len(deps),
        out_specs=[_lane_block_spec(tm, lambda i: i), row, row, row, row, _full((1, D))],
        out_shape=[_lane_blocks(L), SDS((L, D), MXU), SDS((L, D), MXU), SDS((L, D), MXU), SDS((L, D), MXU),
                   SDS((1, D), F32)],
        compiler_params=_cp(40, ("arbitrary",)),
    )(ypre3, p, dx1, wout_e, wglu, bglu, *deps)


def _s5_scan_bwd(u3, dypre3, states, bb, cm, cmt, abr, abi, pwr, pwi, par, pai, pbr, pbi, dskip, deps=NO_DEPS):
    L = u3.shape[1]
    ncb, tb = _chunks(L)
    nb = L // tb
    rev = lambda i: nb - 1 - i

    def body(u_ref, dy_ref, st_ref, bb_ref, cm_ref, cmt_ref, ar_ref, ai_ref, pwr_ref, pwi_ref, par_ref, pai_ref,
             pbr_ref, pbi_ref, d_ref, *rest):
        (du_ref, gd_ref, gcm_ref, gbb_ref, gar_ref, gai_ref,
         lcarry, bu_scr, s_scr, gs_scr, bbp_scr, cmp_scr) = rest[len(deps):]
        del cm_ref

        @pl.when(pl.program_id(1) == 0)
        def _():
            _pow_weights(bb_ref, pwr_ref, pwi_ref, bbp_scr, False)
            _pow_weights(cmt_ref, pwr_ref, pwi_ref, cmp_scr, True)
            lcarry[...] = jnp.zeros_like(lcarry)
            gd_ref[...] = jnp.zeros_like(gd_ref)
            gcm_ref[...] = jnp.zeros_like(gcm_ref)
            gbb_ref[...] = jnp.zeros_like(gbb_ref)
            gar_ref[...] = jnp.zeros_like(gar_ref)
            gai_ref[...] = jnp.zeros_like(gai_ref)

        ar, ai = ar_ref[...], ai_ref[...]
        c = st_ref[...]
        sr, si, _ = _s5_states(u_ref, bb_ref, bbp_scr, par_ref, pai_ref, c[:, :SW], c[:, SW:], ncb, bu_scr)
        s_scr[0] = jnp.concatenate([sr, si], axis=1)
        for t in range(NT):
            bu = bu_scr[t]
            sr, si = _cmul_add(ar, ai, sr, si, bu[:, :SW], bu[:, SW:])
            s_scr[t + 1] = jnp.concatenate([sr, si], axis=1)
        dys = [dy_ref[pl.ds(t, ncb, stride=NT), :] for t in range(NT)]
        for t in range(NT):
            gs_scr[t] = _mm(dys[t], cmt_ref[...])
        f = _mm(jnp.concatenate(dys, axis=1), cmp_scr[...])
        xr, xi = _cscan(f[:, :SW], f[:, SW:], par_ref, pai_ref, True)
        lc = lcarry[...]
        lcr, lci = lc[:, :SW], lc[:, SW:]
        hr, hi = _cmul_add(pbr_ref[CH - ncb:CH, :], -pbi_ref[CH - ncb:CH, :], lcr, lci, xr, xi)
        lcarry[...] = jnp.concatenate([hr[0:1, :], hi[0:1, :]], axis=1)
        row = lax.broadcasted_iota(jnp.int32, hr.shape, 0)
        lr_ = jnp.where(row < ncb - 1, pltpu.roll(hr, ncb - 1, 0), lcr)
        li_ = jnp.where(row < ncb - 1, pltpu.roll(hi, ncb - 1, 0), lci)
        gar = jnp.zeros((1, SW), F32)
        gai = jnp.zeros((1, SW), F32)
        for t in reversed(range(NT)):
            gs = gs_scr[t]
            lr_, li_ = _cmul_add(ar, -ai, lr_, li_, gs[:, :SW], gs[:, SW:])
            rows = pl.ds(t, ncb, stride=NT)
            u_t, dy_t = u_ref[rows, :], dy_ref[rows, :]
            lam = jnp.concatenate([lr_, li_], axis=1)
            gbb_ref[...] += _mm_tn(u_t, lam)
            du_ref[rows, :] = _mm_nt(lam, bb_ref[...]) + dy_t * d_ref[...]
            gd_ref[...] += jnp.sum(dy_t * u_t, axis=0, keepdims=True)
            gcm_ref[...] += _mm_tn(s_scr[t + 1], dy_t)
            sp = s_scr[t]
            spr, spi = sp[:, :SW], sp[:, SW:]
            gar += jnp.sum(lr_ * spr + li_ * spi, axis=0, keepdims=True)
            gai += jnp.sum(li_ * spr - lr_ * spi, axis=0, keepdims=True)
        gar_ref[...] += gar
        gai_ref[...] += gai

    tab = pl.BlockSpec((CH, SW), lambda j, i: (0, j))
    stp = pl.BlockSpec((NT, SW), lambda j, i: (0, j))
    colblk = pl.BlockSpec((None, tb, UW), lambda j, i: (j, rev(i), 0))
    vec = lambda w: pl.BlockSpec((1, w), lambda j, i: (0, j))
    return pl.pallas_call(
        body, name="s5_scan_bwd", grid=(NJ, nb),
        in_specs=[colblk, colblk,
                  pl.BlockSpec((None, None, 1, 2 * SW), lambda j, i: (j, rev(i), 0, 0)),
                  pl.BlockSpec((None, UW, 2 * SW), lambda j, i: (j, 0, 0)),
                  pl.BlockSpec((None, 2 * SW, UW), lambda j, i: (j, 0, 0)),
                  pl.BlockSpec((None, UW, 2 * SW), lambda j, i: (j, 0, 0)),
                  vec(SW), vec(SW), stp, stp, tab, tab, tab, tab, vec(UW)] + [ANY_SPEC] * len(deps),
        out_specs=[colblk, vec(UW),
                   pl.BlockSpec((None, 2 * SW, UW), lambda j, i: (j, 0, 0)),
                   pl.BlockSpec((None, UW, 2 * SW), lambda j, i: (j, 0, 0)),
                   vec(SW), vec(SW)],
        out_shape=[_lane_blocks(L), SDS((1, D), F32),
                   SDS((NJ, 2 * SW, UW), F32), SDS((NJ, UW, 2 * SW), F32),
                   SDS((1, NSTATE), F32), SDS((1, NSTATE), F32)],
        scratch_shapes=[pltpu.VMEM((1, 2 * SW), F32), pltpu.VMEM((NT, ncb, 2 * SW), F32),
                        pltpu.VMEM((NT + 1, ncb, 2 * SW), F32), pltpu.VMEM((NT, ncb, 2 * SW), F32),
                        pltpu.VMEM((NT * UW, 2 * SW), MXU), pltpu.VMEM((NT * UW, 2 * SW), MXU)],
        compiler_params=_cp(56, ARB2),
    )(u3, dypre3, states, bb, cm, cmt, abr, abi, pwr, pwi, par, pai, pbr, pbi, dskip, *deps)


def _rope_tables(L, inv):
    tm = min(512, L)

    def body(inv_ref, cos_ref, sin_ref):
        pos = (lax.broadcasted_iota(jnp.int32, (tm, DK // 2), 0) + pl.program_id(0) * tm).astype(F32)
        ang = pos * inv_ref[...]
        cos_ref[...] = jnp.cos(ang)
        sin_ref[...] = jnp.sin(ang)

    blk = pl.BlockSpec((tm, DK // 2), lambda i: (i, 0))
    return pl.pallas_call(body, name="rope_tables", grid=(L // tm,), in_specs=[_full((1, DK // 2))],
                          out_specs=[blk, blk], out_shape=[SDS((L, DK // 2), F32)] * 2)(inv)


def _rot(x, cos, sin):
    x1, x2 = x[:, :DK // 2], x[:, DK // 2:]
    return jnp.concatenate([x1 * cos - x2 * sin, x1 * sin + x2 * cos], axis=1)


def _unrot(d, cos, sin):
    d1, d2 = d[:, :DK // 2], d[:, DK // 2:]
    return jnp.concatenate([d1 * cos + d2 * sin, d2 * cos - d1 * sin], axis=1)


def _ret_decays(h):
    lg = LOG_G[h]
    n = lax.broadcasted_iota(jnp.int32, (CH, CH), 0)
    m = lax.broadcasted_iota(jnp.int32, (CH, CH), 1)
    diff = (n - m).astype(F32)
    decay = jnp.where(n >= m, jnp.exp(lg * jnp.maximum(diff, 0.0)), 0.0)
    idx = lax.broadcasted_iota(jnp.int32, (CH, 1), 0).astype(F32)
    xi = jnp.exp(lg * (idx + 1.0))
    zeta = jnp.exp(lg * (CH - 1.0 - idx))
    return decay, xi, zeta, math.exp(lg * CH)


def _ret_tables(dec_scr, vec_scr):
    for h in range(HEADS):
        decay, xi, zeta, _ = _ret_decays(h)
        dec_scr[h] = decay
        vec_scr[h] = jnp.concatenate([jnp.broadcast_to(xi, (CH, 128)), jnp.broadcast_to(zeta, (CH, 128))], axis=1)


def _ret_chunk_fwd(q, k, v, cos, sin, s_prev_b, decay, xi, zeta):
    qr = _rot(q, cos, sin)
    kr = _rot(k, cos, sin) * (DK ** -0.5)
    scores = _mm_nt(qr, kr) * decay
    o = _mm(scores, v) + _mm(qr * xi, s_prev_b)
    local = _mm_tn(kr * zeta, v)
    mu = jnp.mean(o, axis=-1, keepdims=True)
    oc = o - mu
    rstd = lax.rsqrt(jnp.mean(oc * oc, axis=-1, keepdims=True) + EPS)
    return qr, kr, scores, local, oc * rstd, rstd


def _ret_fwd(p, cos, sin, gain):
    L = p.shape[0]
    nb = L // CH

    def body(q_ref, k_ref, v_ref, bz_ref, cos_ref, sin_ref, g_ref, yb_ref, st_ref, state, dec_scr, vec_scr):
        @pl.when(pl.program_id(0) == 0)
        def _():
            state[...] = jnp.zeros_like(state)
            _ret_tables(dec_scr, vec_scr)

        cos, sin = cos_ref[...], sin_ref[...]
        act, _ = _silu_and_grad(bz_ref[...])
        for h in range(HEADS):
            hs = slice(h * DK, (h + 1) * DK)
            xi, zeta = vec_scr[h, :, 0:1], vec_scr[h, :, 128:129]
            s_prev = state[h]
            s_prev_b = s_prev.astype(MXU)
            st_ref[h] = s_prev_b
            _, _, _, local, on, _ = _ret_chunk_fwd(q_ref[:, hs], k_ref[:, hs], v_ref[:, hs], cos, sin,
                                                   s_prev_b, dec_scr[h], xi, zeta)
            state[h] = s_prev * math.exp(LOG_G[h] * CH) + local
            yb_ref[:, hs] = (on * g_ref[:, hs] * act[:, hs]).astype(yb_ref.dtype)

    col = lambda c: pl.BlockSpec((CH, D), lambda i: (i, c))
    rope = pl.BlockSpec((CH, DK // 2), lambda i: (i, 0))
    return pl.pallas_call(
        body, name="ret_fwd", grid=(nb,),
        in_specs=[col(2), col(3), col(4), col(5), rope, rope, _full((1, D))],
        out_specs=[pl.BlockSpec((CH, D), lambda i: (i, 0)),
                   pl.BlockSpec((None, HEADS, DK, DK), lambda i: (i, 0, 0, 0))],
        out_shape=[SDS((L, D), MXU), SDS((nb, HEADS, DK, DK), MXU)],
        scratch_shapes=[pltpu.VMEM((HEADS, DK, DK), F32), pltpu.VMEM((HEADS, CH, CH), F32),
                        pltpu.VMEM((HEADS, CH, 256), F32)],
        compiler_params=_cp(40, ("arbitrary",)),
    )(p, p, p, p, cos, sin, gain)


def _ret_bwd(p, cos, sin, gain, states, dx1, wout_e, du, daz, deps=NO_DEPS):
    L = p.shape[0]
    nb = L // CH
    rev = lambda i: nb - 1 - i

    def body(q_ref, k_ref, v_ref, bz_ref, cos_ref, sin_ref, g_ref, st_ref, dx1_ref, wo_ref, du_ref, daz_ref, *rest):
        dp_ref, yb_ref, gg_ref, gstate, dec_scr, vec_scr = rest[len(deps):]

        @pl.when(pl.program_id(0) == 0)
        def _():
            gstate[...] = jnp.zeros_like(gstate)
            gg_ref[...] = jnp.zeros_like(gg_ref)
            _ret_tables(dec_scr, vec_scr)

        cos, sin = cos_ref[...], sin_ref[...]
        act, dact = _silu_and_grad(bz_ref[...])
        dyb = _mm_nt(dx1_ref[...], wo_ref[...])
        dp_ref[:, 0:D] = _from_lane_blocks(du_ref).astype(dp_ref.dtype)
        dp_ref[:, D:2 * D] = daz_ref[...]
        for h in range(HEADS):
            hs = slice(h * DK, (h + 1) * DK)
            col = lambda part: slice((2 + part) * D + h * DK, (2 + part) * D + (h + 1) * DK)
            decay = dec_scr[h]
            xi, zeta = vec_scr[h, :, 0:1], vec_scr[h, :, 128:129]
            v = v_ref[:, hs]
            s_prev_b = st_ref[h]
            qr, kr, scores, _, on, rstd = _ret_chunk_fwd(q_ref[:, hs], k_ref[:, hs], v, cos, sin, s_prev_b,
                                                         decay, xi, zeta)
            gain_h = g_ref[:, hs]
            out = on * gain_h
            yb_ref[:, hs] = (out * act[:, hs]).astype(yb_ref.dtype)
            dyb_h = dyb[:, hs]
            dp_ref[:, col(3)] = (dyb_h * out * dact[:, hs]).astype(dp_ref.dtype)
            dout = dyb_h * act[:, hs]
            gg_ref[:, hs] += jnp.sum(dout * on, axis=0, keepdims=True)
            don = dout * gain_h
            do = rstd * (don - jnp.mean(don, axis=-1, keepdims=True)
                         - on * jnp.mean(don * on, axis=-1, keepdims=True))
            gnext = gstate[h]
            gnext_b = gnext.astype(MXU)
            dscores = _mm_nt(do, v) * decay
            dp_ref[:, col(2)] = (_mm_tn(scores, do) + _mm(kr * zeta, gnext_b)).astype(dp_ref.dtype)
            dqr = _mm(dscores, kr) + _mm_nt(do, s_prev_b) * xi
            dkr = _mm_tn(dscores, qr) + _mm_nt(v, gnext_b) * zeta
            gstate[h] = gnext * math.exp(LOG_G[h] * CH) + _mm_tn(qr * xi, do)
            dp_ref[:, col(0)] = _unrot(dqr, cos, sin).astype(dp_ref.dtype)
            dp_ref[:, col(1)] = (_unrot(dkr, cos, sin) * (DK ** -0.5)).astype(dp_ref.dtype)

    col = lambda c: pl.BlockSpec((CH, D), lambda i: (rev(i), c))
    rope = pl.BlockSpec((CH, DK // 2), lambda i: (rev(i), 0))
    outc = col(0)
    act_out = SDS((L, D), MXU)
    return pl.pallas_call(
        body, name="ret_bwd", grid=(nb,),
        in_specs=[col(2), col(3), col(4), col(5), rope, rope, _full((1, D)),
                  pl.BlockSpec((None, HEADS, DK, DK), lambda i: (rev(i), 0, 0, 0)),
                  outc, pl.BlockSpec((D, D), lambda i: (1, 0)), _lane_block_spec(CH, rev), outc]
        + [ANY_SPEC] * len(deps),
        out_specs=[pl.BlockSpec((CH, NIN), lambda i: (rev(i), 0)), outc, _full((1, D))],
        out_shape=[SDS((L, NIN), MXU), act_out, SDS((1, D), F32)],
        scratch_shapes=[pltpu.VMEM((HEADS, DK, DK), F32), pltpu.VMEM((HEADS, CH, CH), F32),
                        pltpu.VMEM((HEADS, CH, 256), F32)],
        compiler_params=_cp(48, ("arbitrary",)),
    )(p, p, p, p, cos, sin, gain, states, dx1, wout_e, du, daz, *deps)


def _out_even(x, ypre3, p, yb, wglu, bglu, wout):
    L = x.shape[0]
    tm = min(256, L)

    def body(x_ref, y_ref, az_ref, yb_ref, wg_ref, bg_ref, w_ref, o_ref):
        yg = _gelu(_from_lane_blocks(y_ref))
        t = _mm(yg, wg_ref[...]) + bg_ref[...]
        act, _ = _silu_and_grad(az_ref[...])
        ya = (yg * jax.nn.sigmoid(t) * act).astype(MXU)
        cat = jnp.concatenate([ya, yb_ref[...]], axis=1)
        o_ref[...] = x_ref[...] + jnp.dot(cat, w_ref[...], preferred_element_type=F32)

    row = pl.BlockSpec((tm, D), lambda i: (i, 0))
    return pl.pallas_call(
        body, name="out_even", grid=(L // tm,),
        in_specs=[row, _lane_block_spec(tm, lambda i: i), pl.BlockSpec((tm, D), lambda i: (i, 1)), row,
                  _full((D, D)), _full((1, D)), _full((DI, D))],
        out_specs=row, out_shape=SDS((L, D), F32), compiler_params=_cp(40, ("arbitrary",)),
    )(x, ypre3, p, yb, wglu, bglu, wout)


def _sgu_core(pv, gain, ws_ref, bs_ref):
    pu, pvv, z = pv[:, :DI], pv[:, DI:2 * DI], pv[:, 2 * DI:]
    u, gu = _gelu_and_grad(pu)
    v, gv = _gelu_and_grad(pvv)
    mu = jnp.mean(v, axis=-1, keepdims=True)
    vc = v - mu
    rstd = lax.rsqrt(jnp.mean(vc * vc, axis=-1, keepdims=True) + EPS)
    vhat = vc * rstd
    vn = vhat * gain
    t = lax.broadcasted_iota(jnp.int32, (CH, CH), 0)
    s_ = lax.broadcasted_iota(jnp.int32, (CH, CH), 1)
    mask = t >= s_
    wm = [jnp.where(mask, ws_ref[g], 0.0).astype(MXU) for g in range(SG)]
    s = jnp.concatenate([_mm(wm[g], vn[:, g * SGD:(g + 1) * SGD]) + bs_ref[g] for g in range(SG)], axis=1)
    return gu, gv, z, u, vhat, rstd, vn, mask, wm, s


def _sgu_fwd_bwd(p2, x1, gain, wsp, bsp, wout, fnorm, tgt):
    L = p2.shape[0]

    def body(p_ref, x1_ref, g_ref, ws_ref, bs_ref, wo_ref, fn_ref, t_ref,
             dp_ref, y_ref, dx2_ref, gg_ref, gws_ref, gbs_ref, gfn_ref, loss_ref):
        @pl.when(pl.program_id(0) == 0)
        def _():
            gg_ref[...] = jnp.zeros_like(gg_ref)
            gws_ref[...] = jnp.zeros_like(gws_ref)
            gbs_ref[...] = jnp.zeros_like(gbs_ref)
            gfn_ref[...] = jnp.zeros_like(gfn_ref)
            loss_ref[...] = jnp.zeros_like(loss_ref)

        gain = g_ref[...]
        gu, gv, z, u, vhat, rstd, vn, mask, wm, s = _sgu_core(p_ref[...], gain, ws_ref, bs_ref)
        act, dact = _silu_and_grad(z)
        y = (u * s * act).astype(MXU)
        y_ref[...] = y
        x2 = x1_ref[...] + jnp.dot(y, wo_ref[...], preferred_element_type=F32)
        xhat, r = _rms(x2)
        fn = fn_ref[...]
        e = xhat * fn - t_ref[...]
        loss_ref[...] += 0.5 * jnp.sum(jnp.mean(e * e, axis=-1, keepdims=True), axis=0, keepdims=True)
        do = e * (1.0 / D)
        gfn_ref[...] += jnp.sum(do * xhat, axis=0, keepdims=True)
        dxhat = do * fn
        dx2 = r * (dxhat - xhat * jnp.mean(dxhat * xhat, axis=-1, keepdims=True))
        dx2_ref[...] = dx2
        dy = _mm_nt(dx2, wo_ref[...])
        du = dy * s * act
        ds = dy * u * act
        dz = dy * u * s * dact
        dvn = []
        for g in range(SG):
            ds_g = ds[:, g * SGD:(g + 1) * SGD]
            vn_g = vn[:, g * SGD:(g + 1) * SGD]
            gbs_ref[g] += jnp.sum(ds_g, axis=1, keepdims=True)
            gws_ref[g] += jnp.where(mask, _mm_nt(ds_g, vn_g), 0.0)
            dvn.append(_mm_tn(wm[g], ds_g))
        dvn = jnp.concatenate(dvn, axis=1)
        gg_ref[...] += jnp.sum(dvn * vhat, axis=0, keepdims=True)
        dvhat = dvn * gain
        dv = rstd * (dvhat - jnp.mean(dvhat, axis=-1, keepdims=True)
                     - vhat * jnp.mean(dvhat * vhat, axis=-1, keepdims=True))
        dp_ref[...] = jnp.concatenate([du * gu, dv * gv, dz], axis=1).astype(dp_ref.dtype)

    row = pl.BlockSpec((CH, D), lambda i: (i, 0))
    wide = pl.BlockSpec((CH, NIN), lambda i: (i, 0))
    return pl.pallas_call(
        body, name="sgu_fwd_bwd", grid=(L // CH,),
        in_specs=[wide, row, _full((1, DI)), _full((SG, CH, CH)), _full((SG, CH, 1)), _full((DI, D)),
                  _full((1, D)), row],
        out_specs=[wide, pl.BlockSpec((CH, DI), lambda i: (i, 0)), row,
                   _full((1, DI)), _full((SG, CH, CH)), _full((SG, CH, 1)), _full((1, D)), _full((1, 128))],
        out_shape=[SDS((L, NIN), MXU), SDS((L, DI), MXU), SDS((L, D), F32), SDS((1, DI), F32),
                   SDS((SG, CH, CH), F32), SDS((SG, CH, 1), F32), SDS((1, D), F32), SDS((1, 128), F32)],
        compiler_params=_cp(48, ("arbitrary",)),
    )(p2, x1, gain, wsp, bsp, wout, fnorm, tgt)


def _my_index():
    return 4 * lax.axis_index("x") + 2 * lax.axis_index("y") + lax.axis_index("c")


def _ordered_sum(land_ref, own, me):
    g = None
    for s in range(NDEV):
        part = jnp.where(me == s, own, land_ref[s].astype(F32))
        g = part if g is None else g + part
    return g


def _adamw_math(w, m, v, g):
    mn = ADAM_B1 * m + (1.0 - ADAM_B1) * g
    vn = ADAM_B2 * v + (1.0 - ADAM_B2) * (g * g)
    mhat = mn / BC1
    vhat = vn / BC2
    return g, -ADAM_LR * (mhat / (jnp.sqrt(vhat) + ADAM_EPS) + ADAM_WD * w), mn, vn


def _adamw(w, m, v, land, own, name):
    R, C = w.shape
    tr = R
    for cand in (256, 128, 64, 32, 16, 8):
        if R % cand == 0 and R > cand:
            tr = cand
            break

    def body(w_ref, m_ref, v_ref, land_ref, own_ref, g_ref, d_ref, mo_ref, vo_ref):
        g = _ordered_sum(land_ref, own_ref[...].astype(F32), _my_index())
        for o, val in zip((g_ref, d_ref, mo_ref, vo_ref), _adamw_math(w_ref[...], m_ref[...], v_ref[...], g)):
            o[...] = val

    blk = pl.BlockSpec((tr, C), lambda i: (i, 0))
    out = SDS((R, C), F32)
    return pl.pallas_call(
        body, name=name, grid=(R // tr,),
        in_specs=[blk, blk, blk, pl.BlockSpec((NDEV, tr, C), lambda i: (0, i, 0)), blk],
        out_specs=[blk, blk, blk, blk], out_shape=[out, out, out, out],
        compiler_params=_cp(40, ("arbitrary",)),
    )(w, m, v, land, own)


def _adamw_many(ws, ms, vs, lands, owns, name):
    k = len(ws)

    def body(*refs):
        ins, outs = refs[:5 * k], refs[5 * k:]
        me = _my_index()
        for i in range(k):
            w_ref, m_ref, v_ref, land_ref, own_ref = (ins[j * k + i] for j in range(5))
            g = _ordered_sum(land_ref, own_ref[...], me)
            for j, val in enumerate(_adamw_math(w_ref[...], m_ref[...], v_ref[...], g)):
                outs[j * k + i][...] = val

    out_shape = [SDS(w.shape, F32) for _ in range(4) for w in ws]
    res = pl.pallas_call(body, name=name, out_shape=out_shape, compiler_params=_cp(60))(*ws, *ms, *vs, *lands, *owns)
    return [res[j * k:(j + 1) * k] for j in range(4)]


MESH = pl.DeviceIdType.MESH
HBM_SPEC = pl.BlockSpec(memory_space=pltpu.HBM)
SEM_SPEC = pl.BlockSpec(memory_space=pltpu.SEMAPHORE)
EFFECT = pltpu.SideEffectType.DATAFLOW_SIDE_EFFECTING


def _me_and_peers():
    x, y, c = lax.axis_index("x"), lax.axis_index("y"), lax.axis_index("c")
    me = 4 * x + 2 * y + c
    peers = []
    for r in range(1, NDEV):
        px, py, pc = x ^ ((r >> 2) & 1), y ^ ((r >> 1) & 1), c ^ (r & 1)
        peers.append(((px, py, pc), 4 * px + 2 * py + pc))
    return me, peers


def _land_shape(a, scatter):
    return (NDEV,) + (a.shape[1:] if scatter else a.shape)


def _remote(src, dst, send_sems, recv_sems, r, k, n, dev):
    i = r * n + k
    return pltpu.make_async_remote_copy(src_ref=src, dst_ref=dst, send_sem=send_sems.at[i], recv_sem=recv_sems.at[i],
                                        device_id=dev, device_id_type=MESH)


def _exchange(arrays, scatter, name):
    n = len(arrays)
    out_shape = [SDS(_land_shape(a, scatter), a.dtype) for a in arrays]

    def body(*refs):
        ins, outs = refs[:n], refs[n:2 * n]
        send_sems, recv_sems, loc_sems = refs[2 * n:]
        me, peers = _me_and_peers()
        local = []
        for k in range(n):
            src = ins[k].at[me] if scatter else ins[k]
            cp = pltpu.make_async_copy(src, outs[k].at[me], loc_sems.at[k])
            cp.start()
            local.append(cp)
        sends = []
        for r, (dev, lin) in enumerate(peers):
            for k in range(n):
                src = ins[k].at[lin] if scatter else ins[k]
                cp = _remote(src, outs[k].at[me], send_sems, recv_sems, r, k, n, dev)
                cp.start()
                sends.append(cp)
        for r, (dev, lin) in enumerate(peers):
            for k in range(n):
                src = ins[k].at[me] if scatter else ins[k]
                _remote(src, outs[k].at[lin], send_sems, recv_sems, r, k, n, dev).wait_recv()
        for cp in sends:
            cp.wait_send()
        for cp in local:
            cp.wait()

    return pl.pallas_call(
        body, name=name, in_specs=[HBM_SPEC] * n, out_specs=[HBM_SPEC] * n, out_shape=out_shape,
        scratch_shapes=[pltpu.SemaphoreType.DMA(((NDEV - 1) * n,)), pltpu.SemaphoreType.DMA(((NDEV - 1) * n,)),
                        pltpu.SemaphoreType.DMA((n,))],
    )(*arrays)


def _exchange_start(arrays, scatter, name):
    n = len(arrays)
    lands = [lax.empty(_land_shape(a, scatter), a.dtype) for a in arrays]

    def body(*refs):
        ins, lnd = refs[:n], refs[n:2 * n]
        send_sems, recv_sems, own_sems = refs[2 * n:2 * n + 3]
        token = refs[-1]
        me, peers = _me_and_peers()
        for r, (dev, lin) in enumerate(peers):
            for k in range(n):
                src = ins[k].at[lin] if scatter else ins[k]
                _remote(src, lnd[k].at[me], send_sems, recv_sems, r, k, n, dev).start()
        if not scatter:
            for k in range(n):
                pltpu.make_async_copy(ins[k], lnd[k].at[me], own_sems.at[k]).start()
        token[...] = jnp.zeros_like(token)

    sem = pltpu.SemaphoreType.DMA(((NDEV - 1) * n,))
    outs = pl.pallas_call(
        body, name=name,
        out_shape=(sem, sem, pltpu.SemaphoreType.DMA((n,)), *[pltpu.HBM(a.shape, a.dtype) for a in arrays],
                   *[pltpu.HBM(l.shape, l.dtype) for l in lands], SDS((8, 128), F32)),
        in_specs=[HBM_SPEC] * (2 * n),
        out_specs=(SEM_SPEC, SEM_SPEC, SEM_SPEC, *[HBM_SPEC] * (2 * n), pl.BlockSpec(memory_space=pltpu.VMEM)),
        input_output_aliases={k: 3 + k for k in range(2 * n)},
        compiler_params=pltpu.CompilerParams(has_side_effects=EFFECT),
    )(*[pltpu.with_memory_space_constraint(a, pltpu.HBM) for a in arrays],
      *[pltpu.with_memory_space_constraint(l, pltpu.HBM) for l in lands])
    return (n, scatter, outs[0], outs[1], outs[2], outs[3:3 + n], outs[3 + n:3 + 2 * n]), outs[-1]


def _exchange_wait(handle, after, name):
    n, scatter, send_sems, recv_sems, own_sems, thru, lands = handle
    after = tuple(after)

    def body(*refs):
        ins, lnd = refs[:n], refs[n:2 * n]
        send_sems, recv_sems, own_sems = refs[2 * n:2 * n + 3]
        me, peers = _me_and_peers()
        for r, (dev, lin) in enumerate(peers):
            for k in range(n):
                src = ins[k].at[lin] if scatter else ins[k]
                cp = _remote(src, lnd[k].at[lin], send_sems, recv_sems, r, k, n, dev)
                cp.wait_send()
                cp.wait_recv()
        if not scatter:
            for k in range(n):
                pltpu.make_async_copy(ins[k], lnd[k].at[me], own_sems.at[k]).wait()

    outs = pl.pallas_call(
        body, name=name,
        out_shape=(*[pltpu.HBM(a.shape, a.dtype) for a in thru], *[pltpu.HBM(l.shape, l.dtype) for l in lands]),
        in_specs=[HBM_SPEC] * (2 * n) + [SEM_SPEC, SEM_SPEC, SEM_SPEC] + [ANY_SPEC] * len(after),
        out_specs=tuple([HBM_SPEC] * (2 * n)),
        input_output_aliases={k: k for k in range(2 * n)},
        compiler_params=pltpu.CompilerParams(has_side_effects=EFFECT),
    )(*thru, *lands, send_sems, recv_sems, own_sems, *after)
    return list(outs[:n]), list(outs[n:])


CHIP_RELATIONS = (2, 4, 6)


def _peer(r):
    x, y, c = lax.axis_index("x"), lax.axis_index("y"), lax.axis_index("c")
    px, py, pc = x ^ ((r >> 2) & 1), y ^ ((r >> 1) & 1), c ^ (r & 1)
    return (px, py, pc), 4 * px + 2 * py + pc


def _copy(src, dst, send_sems, recv_sems, i, dev):
    return pltpu.make_async_remote_copy(src_ref=src, dst_ref=dst, send_sem=send_sems.at[i], recv_sem=recv_sems.at[i],
                                        device_id=dev, device_id_type=MESH)


def _gather2_start(a, name):
    land = lax.empty((NDEV,) + a.shape, a.dtype)

    def body(own, lnd, send_sems, recv_sems, own_sem, own_thru, lnd_thru, token):
        me = _my_index()
        for i, r in enumerate((1,) + CHIP_RELATIONS):
            dev, _ = _peer(r)
            _copy(own, lnd.at[me], send_sems, recv_sems, i, dev).start()
        pltpu.make_async_copy(own, lnd.at[me], own_sem.at[0]).start()
        token[...] = jnp.zeros_like(token)

    sem4 = pltpu.SemaphoreType.DMA((4,))
    outs = pl.pallas_call(
        body, name=name,
        out_shape=(sem4, sem4, pltpu.SemaphoreType.DMA((1,)), pltpu.HBM(a.shape, a.dtype),
                   pltpu.HBM(land.shape, land.dtype), SDS((8, 128), F32)),
        in_specs=[HBM_SPEC, HBM_SPEC],
        out_specs=(SEM_SPEC, SEM_SPEC, SEM_SPEC, HBM_SPEC, HBM_SPEC, pl.BlockSpec(memory_space=pltpu.VMEM)),
        input_output_aliases={0: 3, 1: 4},
        compiler_params=pltpu.CompilerParams(has_side_effects=EFFECT),
    )(pltpu.with_memory_space_constraint(a, pltpu.HBM), pltpu.with_memory_space_constraint(land, pltpu.HBM))
    return outs[:5], outs[5]


def _gather2_forward(handle, after, name):
    send_sems, recv_sems, own_sem, own, land = handle
    after = tuple(after)

    def body(lnd, recv_sems, *rest):
        send2, recv2, lnd_thru = rest[len(after):]
        sib, _ = _peer(1)
        for k, r in enumerate(CHIP_RELATIONS):
            dev, lin = _peer(r)
            _copy(lnd.at[lin], lnd.at[lin], recv_sems, recv_sems, 1 + k, dev).wait_recv()
            _copy(lnd.at[lin], lnd.at[lin], send2, recv2, k, sib).start()

    sem3 = pltpu.SemaphoreType.DMA((3,))
    send2, recv2, land = pl.pallas_call(
        body, name=name,
        out_shape=(sem3, sem3, pltpu.HBM(land.shape, land.dtype)),
        in_specs=[HBM_SPEC, SEM_SPEC] + [ANY_SPEC] * len(after),
        out_specs=(SEM_SPEC, SEM_SPEC, HBM_SPEC),
        input_output_aliases={0: 2},
        compiler_params=pltpu.CompilerParams(has_side_effects=EFFECT),
    )(land, recv_sems, *after)
    return send_sems, recv_sems, own_sem, send2, recv2, own, land


def _gather2_wait(handle, name):
    send_sems, recv_sems, own_sem, send2, recv2, own, land = handle

    def body(own_ref, lnd, send_sems, recv_sems, own_sem, send2, recv2, own_thru, lnd_thru):
        me = _my_index()
        sib, sib_lin = _peer(1)
        for i, r in enumerate((1,) + CHIP_RELATIONS):
            dev, _ = _peer(r)
            _copy(own_ref, lnd.at[me], send_sems, recv_sems, i, dev).wait_send()
        _copy(own_ref, lnd.at[sib_lin], send_sems, recv_sems, 0, sib).wait_recv()
        for k, r in enumerate(CHIP_RELATIONS):
            _, lin = _peer(r)
            _, lin_other = _peer(r ^ 1)
            _copy(lnd.at[lin], lnd.at[lin], send2, recv2, k, sib).wait_send()
            _copy(lnd.at[lin_other], lnd.at[lin_other], send2, recv2, k, sib).wait_recv()
        pltpu.make_async_copy(own_ref, lnd.at[me], own_sem.at[0]).wait()

    outs = pl.pallas_call(
        body, name=name,
        out_shape=(pltpu.HBM(own.shape, own.dtype), pltpu.HBM(land.shape, land.dtype)),
        in_specs=[HBM_SPEC, HBM_SPEC] + [SEM_SPEC] * 5,
        out_specs=(HBM_SPEC, HBM_SPEC),
        input_output_aliases={0: 0, 1: 1},
        compiler_params=pltpu.CompilerParams(has_side_effects=EFFECT),
    )(own, land, send_sems, recv_sems, own_sem, send2, recv2)
    return outs[1]


def _local_step(x, tgt, norm_even, first_weight, lam_re, lam_im, log_dt, b_re, b_im, c_re, c_im, s5_d, bglu,
                ret_gain, wsp, bsp, fnorm, late_weights, emit, start_token=None):
    L = x.shape[0]
    lr3, li3 = lam_re.reshape(G, 1, P), lam_im.reshape(G, 1, P)
    dt3 = log_dt.reshape(G, 1, 1)
    br3, bi3 = jnp.swapaxes(b_re, 1, 2), jnp.swapaxes(b_im, 1, 2)
    abr3, abi3, bbr3, bbi3 = _s5_disc(lr3, li3, dt3, br3, bi3)
    bb = jnp.concatenate([_embed(bbr3), _embed(bbi3)], axis=2).astype(MXU)
    cm = jnp.concatenate([_embed(jnp.swapaxes(c_re, 1, 2)), -_embed(jnp.swapaxes(c_im, 1, 2))], axis=1).astype(MXU)
    abr, abi = abr3.reshape(1, NSTATE), abi3.reshape(1, NSTATE)
    pwr, pwi, _, _ = _s5_tables(abr, abi, NT, "s5_tables_step")
    par, pai, pbr, pbi = _s5_tables(pwr[NT - 1:NT], pwi[NT - 1:NT], CH, "s5_tables_chunk")
    inv = (ROPE_BASE ** (-jnp.arange(DK // 2, dtype=F32) / (DK // 2))).reshape(1, DK // 2)
    cos, sin = _rope_tables(L, inv)
    bsp3 = bsp.reshape(SG, CH, 1)

    def dep(token):
        return NO_DEPS if token is None else (token,)

    win_e = first_weight((cos, pbi, cm))
    p, u3 = _in_proj(x, norm_even, win_e, "in_even", dep(start_token), lane_blocks=True)
    ypre, s5_states = _s5_scan_fwd(u3, bb, cm, abr, abi, pwr, pwi, par, pai, s5_d)
    yb, ret_states = _ret_fwd(p, cos, sin, ret_gain)
    wglu, wout_e, norm_odd, win_o, sgu_gain, wout_o = late_weights((ypre, yb))
    x1 = _out_even(x, ypre, p, yb, wglu, bglu, wout_e)
    p2 = _in_proj(x1, norm_odd, win_o, "in_odd")
    dp2, y_o, dx2, g_sgu_gain, g_wsp, g_bsp, g_fnorm, loss = _sgu_fwd_bwd(
        p2, x1, sgu_gain, wsp, bsp3, wout_o, fnorm, tgt)

    g_wout_o = _wgrad_rows([y_o], dx2, "wgrad_out_odd")
    g_win_o = _wgrad_cols(x1, norm_odd, dp2, "wgrad_in_odd")
    tok = emit("odd", dict(w_in_odd=g_win_o, w_out_odd=g_wout_o))
    dx1, g_norm_odd = _in_proj_bwd_x(dp2, x1, norm_odd, win_o, dx2, "in_odd_bwd", dep(tok))
    tok = emit("small_odd", dict(norm_odd=g_norm_odd, sgu_norm_gain=g_sgu_gain, sgu_w_spatial=g_wsp,
                                 sgu_b_spatial=g_bsp.reshape(SG, CH), final_norm=g_fnorm))

    dypre, daz, yg, dt, ya2, g_bglu = _s5_gate_bwd(ypre, p, dx1, wout_e, wglu, bglu, dep(tok))
    g_wglu = _wgrad_rows([yg], dt, "wgrad_glu")
    tok = emit("glu", dict(s5_w_glu=g_wglu))
    du, g_d, g_cm, g_bb, g_ar, g_ai = _s5_scan_bwd(u3, dypre, s5_states, bb, cm, jnp.swapaxes(cm, 1, 2), abr, abi,
                                                   pwr, pwi, par, pai, pbr, pbi, s5_d, dep(tok))
    dbbr3 = _diag_blocks(g_bb[:, :, :SW], HG, P)
    dbbi3 = _diag_blocks(g_bb[:, :, SW:], HG, P)
    g_c_re = jnp.swapaxes(_diag_blocks(g_cm[:, :SW, :], P, HG), 1, 2)
    g_c_im = -jnp.swapaxes(_diag_blocks(g_cm[:, SW:, :], P, HG), 1, 2)
    g_lr3, g_li3, g_dt3, g_br3, g_bi3 = _s5_disc_bwd(
        lr3, li3, dt3, br3, bi3, g_ar.reshape(G, 1, P), g_ai.reshape(G, 1, P), dbbr3, dbbi3)
    tok = emit("small_s5", dict(
        s5_lam_re=g_lr3.reshape(G, P), s5_lam_im=g_li3.reshape(G, P), s5_log_dt=g_dt3.reshape(1, G),
        s5_b_re=g_br3, s5_b_im=g_bi3, s5_c_re=g_c_re, s5_c_im=g_c_im, s5_d=g_d, s5_b_glu=g_bglu))
    dp, yb2, g_ret_gain = _ret_bwd(p, cos, sin, ret_gain, ret_states, dx1, wout_e, du, daz, dep(tok))
    g_win_e = _wgrad_cols(x, norm_even, dp, "wgrad_in_even")
    tok = emit("even_cols", dict(w_in_even=g_win_e))
    g_wout_e = _wgrad_rows([ya2, yb2], dx1, "wgrad_out_even", dep(tok))
    tok = emit("even_rows", dict(w_out_even=g_wout_e))
    dx, g_norm_even = _in_proj_bwd_x(dp, x, norm_even, win_e, dx1, "in_even_bwd", dep(tok))
    emit("last", dict(ret_gn_gain=g_ret_gain, norm_even=g_norm_even, loss=loss))
    return dx


WEIGHTS = ['norm_even', 'w_in_even', 's5_lam_re', 's5_lam_im', 's5_log_dt', 's5_b_re', 's5_b_im', 's5_c_re',
           's5_c_im', 's5_d', 's5_w_glu', 's5_b_glu', 'ret_gn_gain', 'w_out_even', 'norm_odd', 'w_in_odd',
           'sgu_norm_gain', 'sgu_w_spatial', 'sgu_b_spatial', 'w_out_odd', 'final_norm']
BIG = ['w_in_even', 's5_w_glu', 'w_out_even', 'w_in_odd', 'w_out_odd']
SHARDED_SMALL = {'norm_odd': D // NDEV, 'sgu_norm_gain': DI // NDEV}
SCATTER_STAGES = ("odd", "glu", "even_cols", "even_rows")
GATHER_STAGES = ("small_odd", "small_s5")


def _view(n, a):
    if n in ('s5_b_re', 's5_b_im'):
        return jnp.swapaxes(a[0], 1, 2)
    if n == 'final_norm':
        return a.reshape(1, D)
    return a[0] if a.ndim >= 3 else a


def _unview(n, t, shape):
    if n in ('s5_b_re', 's5_b_im'):
        return jnp.swapaxes(t, 1, 2)[None]
    return t.reshape(shape)


def kernel(x, norm_even, w_in_even, s5_lam_re, s5_lam_im, s5_log_dt, s5_b_re, s5_b_im, s5_c_re, s5_c_im, s5_d, s5_w_glu, s5_b_glu, ret_gn_gain, w_out_even, norm_odd, w_in_odd, sgu_norm_gain, sgu_w_spatial, sgu_b_spatial, w_out_odd, final_norm, loss_target, m_norm_even, m_w_in_even, m_s5_lam_re, m_s5_lam_im, m_s5_log_dt, m_s5_b_re, m_s5_b_im, m_s5_c_re, m_s5_c_im, m_s5_d, m_s5_w_glu, m_s5_b_glu, m_ret_gn_gain, m_w_out_even, m_norm_odd, m_w_in_odd, m_sgu_norm_gain, m_sgu_w_spatial, m_sgu_b_spatial, m_w_out_odd, m_final_norm, v_norm_even, v_w_in_even, v_s5_lam_re, v_s5_lam_im, v_s5_log_dt, v_s5_b_re, v_s5_b_im, v_s5_c_re, v_s5_c_im, v_s5_d, v_s5_w_glu, v_s5_b_glu, v_ret_gn_gain, v_w_out_even, v_norm_odd, v_w_in_odd, v_sgu_norm_gain, v_sgu_w_spatial, v_sgu_b_spatial, v_w_out_odd, v_final_norm):
    args = dict(locals())
    w = {n: args[n] for n in WEIGHTS}
    m = {n: args["m_" + n] for n in WEIGHTS}
    v = {n: args["v_" + n] for n in WEIGHTS}
    me = _my_index()

    first_handle, _ = _gather2_start(w['w_in_even'][0].astype(MXU), "gather_first_start")

    def first_weight(after):
        return _gather2_wait(_gather2_forward(first_handle, after, "gather_first_forward"), "gather_first_wait")

    late_own = [w['s5_w_glu'][0].astype(MXU), w['w_out_even'][0].astype(MXU), w['norm_odd'],
                w['w_in_odd'][0].astype(MXU), w['sgu_norm_gain'], w['w_out_odd'][0].astype(MXU)]
    late_handle, start_token = _exchange_start(late_own, False, "gather_late_start")

    def late_weights(after):
        _, (wglu, wout_e, nodd, win_o, sgug, wout_o) = _exchange_wait(late_handle, after, "gather_late_wait")
        return (wglu.reshape(D, D), wout_e.reshape(DI, D), nodd.reshape(1, D), win_o, sgug.reshape(1, DI),
                wout_o.reshape(DI, D))

    pending = {}
    small_last = {}

    def emit(stage, grads):
        if stage == "last":
            small_last.update(grads)
            return None
        names = list(grads)
        handle, token = _exchange_start([grads[n] for n in names], stage in SCATTER_STAGES, stage + "_start")
        pending[stage] = (handle, names)
        return token

    dx = _local_step(
        x[0], loss_target[0], w['norm_even'], first_weight, w['s5_lam_re'][0], w['s5_lam_im'][0], w['s5_log_dt'][0],
        w['s5_b_re'][0], w['s5_b_im'][0], w['s5_c_re'][0], w['s5_c_im'][0], w['s5_d'], w['s5_b_glu'],
        w['ret_gn_gain'], w['sgu_w_spatial'][0], w['sgu_b_spatial'][0], w['final_norm'].reshape(1, D),
        late_weights, emit, start_token)

    out_g, out_d, out_m, out_v = {}, {}, {}, {}
    after = dx
    for stage in SCATTER_STAGES:
        handle, names = pending[stage]
        sent, lands = _exchange_wait(handle, (after,), stage + "_wait")
        for n, land, stack in zip(names, lands, sent):
            shp = w[n].shape
            r, c = shp[1], shp[2]
            own = lax.dynamic_index_in_dim(stack, me, 0, keepdims=False)
            res = _adamw(w[n].reshape(r, c), m[n].reshape(r, c), v[n].reshape(r, c), land, own, "adamw_" + n)
            out_g[n], out_d[n], out_m[n], out_v[n] = (t.reshape(shp) for t in res)
            after = res[0]

    names, owns, lands = [], [], []
    for stage in GATHER_STAGES:
        handle, stage_names = pending[stage]
        sent, got = _exchange_wait(handle, (after,), stage + "_wait")
        names, owns, lands = names + stage_names, owns + sent, lands + got
    last_names = list(small_last)
    last = _exchange([small_last[n] for n in last_names], False, "gather_last")
    loss_parts = last[last_names.index("loss")][:, 0, 0]
    for n, own, land in zip(last_names, (small_last[n] for n in last_names), last):
        if n != "loss":
            names, owns, lands = names + [n], owns + [own], lands + [land]
    for i, n in enumerate(names):
        if n in SHARDED_SMALL:
            width = SHARDED_SMALL[n]
            owns[i] = lax.dynamic_slice_in_dim(owns[i], me * width, width, axis=1)
            lands[i] = lax.dynamic_slice_in_dim(lands[i], me * width, width, axis=2)
    res = _adamw_many([_view(n, w[n]) for n in names], [_view(n, m[n]) for n in names],
                      [_view(n, v[n]) for n in names], lands, owns, "adamw_small")
    for dst, vals in zip((out_g, out_d, out_m, out_v), res):
        for n, t in zip(names, vals):
            dst[n] = _unview(n, t, w[n].shape)

    loss_total = jnp.sum(loss_parts)
    return (loss_total, dx[None], *[out_g[n] for n in WEIGHTS], *[out_d[n] for n in WEIGHTS],
            *[out_m[n] for n in WEIGHTS], *[out_v[n] for n in WEIGHTS])
```

```python
import math

import jax
import jax.numpy as jnp
from jax import lax
from jax.experimental import pallas as pl
from jax.experimental.pallas import tpu as pltpu

F32 = jnp.float32
MXU = jnp.bfloat16
AXES = ("x", "y", "c")
NDEV = 8
D = 1024
NIN = 6144
WIN_BLK = NIN // NDEV
DI = 2048
G, P, HG = 64, 64, 16
GB = 8
NJ = G // GB
SW = GB * P
UW = GB * HG
NSTATE = G * P
HEADS, DK = 4, 256
CH = 128
SG, SGD = 4, 512
EPS = 1e-6
ROPE_BASE = 10000.0
VMEM_CAP_V7X = 64 * 1024 * 1024
LOG_G = [math.log1p(-2.0 ** (-5.0 - h)) for h in range(HEADS)]
GELU_C = math.sqrt(2.0 / math.pi)

ADAM_LR, ADAM_B1, ADAM_B2, ADAM_EPS, ADAM_WD, ADAM_STEP = 0.001, 0.9, 0.999, 1e-08, 0.01, 10
BC1 = 1.0 - ADAM_B1 ** ADAM_STEP
BC2 = 1.0 - ADAM_B2 ** ADAM_STEP

SDS = jax.ShapeDtypeStruct
ARB2 = ("arbitrary", "arbitrary")


def _cp(vmem_mib, sem=None):
    kw = dict(vmem_limit_bytes=min(vmem_mib * 1024 * 1024, VMEM_CAP_V7X - 4 * 1024 * 1024))
    if sem is not None:
        kw["dimension_semantics"] = sem
    return pltpu.CompilerParams(**kw)


def _mm(a, b):
    return jnp.dot(a.astype(MXU), b.astype(MXU), preferred_element_type=F32)


def _mm_nt(a, b):
    return lax.dot_general(a.astype(MXU), b.astype(MXU), (((1,), (1,)), ((), ())), preferred_element_type=F32)


def _mm_tn(a, b):
    return lax.dot_general(a.astype(MXU), b.astype(MXU), (((0,), (0,)), ((), ())), preferred_element_type=F32)


def _gelu(x):
    return _gelu_and_grad(x)[0]


def _gelu_and_grad(x):
    x2 = x * x
    th = jnp.tanh(GELU_C * x * (1.0 + 0.044715 * x2))
    hp = 0.5 * (1.0 + th)
    return x * hp, hp + 0.5 * x * (1.0 - th * th) * GELU_C * (1.0 + 3.0 * 0.044715 * x2)


def _silu_and_grad(x):
    s = jax.nn.sigmoid(x)
    return x * s, s * (1.0 + x * (1.0 - s))


def _full(shape):
    nd = len(shape)
    return pl.BlockSpec(shape, lambda *_: (0,) * nd)


def _rms(xf):
    r = lax.rsqrt(jnp.mean(xf * xf, axis=-1, keepdims=True) + EPS)
    return xf * r, r


ANY_SPEC = pl.BlockSpec(memory_space=pl.ANY)
NO_DEPS = ()


def _load_once(src_hbm, dst_vmem, sem):
    @pl.when(pl.program_id(0) == 0)
    def _():
        cp = pltpu.make_async_copy(src_hbm, dst_vmem, sem)
        cp.start()
        cp.wait()


def _lane_blocks(L):
    return SDS((NJ, L, UW), F32)


def _lane_block_spec(rows, index):
    return pl.BlockSpec((NJ, rows, UW), lambda i: (0, index(i), 0))


def _from_lane_blocks(ref):
    return jnp.concatenate([ref[j] for j in range(NJ)], axis=1)


def _to_lane_blocks(ref, v):
    for j in range(NJ):
        ref[j] = v[:, j * UW:(j + 1) * UW].astype(ref.dtype)


def _in_proj(x, gain, wst, name, deps=NO_DEPS, lane_blocks=False):
    L = x.shape[0]
    tm = min(512, L)

    def body(x_ref, g_ref, w_hbm, *rest):
        outs = rest[len(deps):]
        o_ref, w_scr, sem = outs[0], outs[-2], outs[-1]
        _load_once(w_hbm, w_scr, sem)
        xhat, _ = _rms(x_ref[...])
        h = (xhat * g_ref[...]).astype(MXU)
        for c in range(NDEV):
            o_ref[:, c * WIN_BLK:(c + 1) * WIN_BLK] = jnp.dot(h, w_scr[c], preferred_element_type=F32)
        if lane_blocks:
            _to_lane_blocks(outs[1], o_ref[:, 0:D])

    p_spec, p_shape = pl.BlockSpec((tm, NIN), lambda i: (i, 0)), SDS((L, NIN), F32)
    return pl.pallas_call(
        body, name=name, grid=(L // tm,),
        in_specs=[pl.BlockSpec((tm, D), lambda i: (i, 0)), _full((1, D)), ANY_SPEC] + [ANY_SPEC] * len(deps),
        out_specs=[p_spec, _lane_block_spec(tm, lambda i: i)] if lane_blocks else p_spec,
        out_shape=[p_shape, _lane_blocks(L)] if lane_blocks else p_shape,
        scratch_shapes=[pltpu.VMEM((NDEV, D, WIN_BLK), MXU), pltpu.SemaphoreType.DMA(())],
        compiler_params=_cp(58, ("arbitrary",)),
    )(x, gain, wst, *deps)


def _in_proj_bwd_x(dp, x, gain, wst, dres, name, deps=NO_DEPS):
    L = x.shape[0]
    tm = min(512, L)

    def body(dp_ref, x_ref, g_ref, w_hbm, dres_ref, *rest):
        dx_ref, gg_ref, w_scr, sem = rest[len(deps):]
        _load_once(w_hbm, w_scr, sem)

        @pl.when(pl.program_id(0) == 0)
        def _():
            gg_ref[...] = jnp.zeros_like(gg_ref)

        dh = _mm_nt(dp_ref[:, 0:WIN_BLK], w_scr[0])
        for c in range(1, NDEV):
            dh += _mm_nt(dp_ref[:, c * WIN_BLK:(c + 1) * WIN_BLK], w_scr[c])
        xhat, r = _rms(x_ref[...])
        dxhat = dh * g_ref[...]
        dx_ref[...] = dres_ref[...] + r * (dxhat - xhat * jnp.mean(dxhat * xhat, axis=-1, keepdims=True))
        gg_ref[...] += jnp.sum(dh * xhat, axis=0, keepdims=True)

    row = pl.BlockSpec((tm, D), lambda i: (i, 0))
    return pl.pallas_call(
        body, name=name, grid=(L // tm,),
        in_specs=[pl.BlockSpec((tm, NIN), lambda i: (i, 0)), row, _full((1, D)), ANY_SPEC, row]
        + [ANY_SPEC] * len(deps),
        out_specs=[row, _full((1, D))],
        out_shape=[SDS((L, D), F32), SDS((1, D), F32)],
        scratch_shapes=[pltpu.VMEM((NDEV, D, WIN_BLK), MXU), pltpu.SemaphoreType.DMA(())],
        compiler_params=_cp(56, ("arbitrary",)),
    )(dp, x, gain, wst, dres, *deps)


def _wgrad_cols(x, gain, dp, name, deps=NO_DEPS):
    L = x.shape[0]
    tk = min(512, L)
    nk = L // tk
    halves = 2
    nh = NDEV // halves

    def body(x_ref, g_ref, dp_ref, *rest):
        o_ref, acc = rest[len(deps):]
        k = pl.program_id(1)

        @pl.when(k == 0)
        def _():
            acc[...] = jnp.zeros_like(acc)

        xhat, _ = _rms(x_ref[...])
        acc[...] += _mm_tn(xhat * g_ref[...], dp_ref[...])

        @pl.when(k == nk - 1)
        def _():
            for c in range(nh):
                o_ref[c] = acc[:, c * WIN_BLK:(c + 1) * WIN_BLK].astype(o_ref.dtype)

    return pl.pallas_call(
        body, name=name, grid=(halves, nk),
        in_specs=[pl.BlockSpec((tk, D), lambda n, k: (k, 0)), _full((1, D)),
                  pl.BlockSpec((tk, nh * WIN_BLK), lambda n, k: (k, n))] + [ANY_SPEC] * len(deps),
        out_specs=pl.BlockSpec((nh, D, WIN_BLK), lambda n, k: (n, 0, 0)),
        out_shape=SDS((NDEV, D, WIN_BLK), MXU),
        scratch_shapes=[pltpu.VMEM((D, nh * WIN_BLK), F32)],
        compiler_params=_cp(56, ARB2),
    )(x, gain, dp, *deps)


def _wgrad_rows(a_parts, b, name, deps=NO_DEPS):
    L, N = b.shape
    na = len(a_parts)
    widths = [a.shape[1] for a in a_parts]
    M = sum(widths)
    tk = min(512, L)
    nk = L // tk

    def body(*refs):
        a_refs, b_ref = refs[:na], refs[na]
        o_ref, acc = refs[na + 1 + len(deps):]
        k = pl.program_id(0)

        @pl.when(k == 0)
        def _():
            acc[...] = jnp.zeros_like(acc)

        bv = b_ref[...].astype(MXU)
        off = 0
        for a_ref, wd in zip(a_refs, widths):
            acc[off:off + wd, :] += _mm_tn(a_ref[...], bv)
            off += wd

        @pl.when(k == nk - 1)
        def _():
            o_ref[...] = acc[...].astype(o_ref.dtype).reshape(o_ref.shape)

    return pl.pallas_call(
        body, name=name, grid=(nk,),
        in_specs=[pl.BlockSpec((tk, wd), lambda k: (k, 0)) for wd in widths]
        + [pl.BlockSpec((tk, N), lambda k: (k, 0))] + [ANY_SPEC] * len(deps),
        out_specs=_full((NDEV, M // NDEV, N)),
        out_shape=SDS((NDEV, M // NDEV, N), MXU),
        scratch_shapes=[pltpu.VMEM((M, N), F32)],
        compiler_params=_cp(48, ("arbitrary",)),
    )(*a_parts, b, *deps)


def _s5_disc_fn(lr_raw, li, logdt, br, bi):
    lr = jnp.minimum(lr_raw, -1e-4)
    dt = jnp.exp(logdt)
    mag = jnp.exp(lr * dt)
    abr = mag * jnp.cos(li * dt)
    abi = mag * jnp.sin(li * dt)
    den = lr * lr + li * li
    nre = abr - 1.0
    nim = abi
    zr = (nre * lr + nim * li) / den
    zi = (nim * lr - nre * li) / den
    return abr, abi, zr * br - zi * bi, zr * bi + zi * br


def _s5_disc(lr, li, logdt, br, bi):
    def body(lr_ref, li_ref, dt_ref, br_ref, bi_ref, abr_ref, abi_ref, bbr_ref, bbi_ref):
        abr, abi, bbr, bbi = _s5_disc_fn(lr_ref[...], li_ref[...], dt_ref[...], br_ref[...], bi_ref[...])
        abr_ref[...] = abr
        abi_ref[...] = abi
        bbr_ref[...] = bbr
        bbi_ref[...] = bbi

    s1, s3 = SDS((G, 1, P), F32), SDS((G, HG, P), F32)
    return pl.pallas_call(body, name="s5_disc", out_shape=[s1, s1, s3, s3])(lr, li, logdt, br, bi)


def _s5_disc_bwd(lr, li, logdt, br, bi, dabr, dabi, dbbr, dbbi):
    def body(lr_ref, li_ref, dt_ref, br_ref, bi_ref, c0, c1, c2, c3, o0, o1, o2, o3, o4):
        _, vjp = jax.vjp(_s5_disc_fn, lr_ref[...], li_ref[...], dt_ref[...], br_ref[...], bi_ref[...])
        g = vjp((c0[...], c1[...], c2[...], c3[...]))
        for o, v in zip((o0, o1, o2, o3, o4), g):
            o[...] = v

    s1, s3 = SDS((G, 1, P), F32), SDS((G, HG, P), F32)
    return pl.pallas_call(body, name="s5_disc_bwd", out_shape=[s1, s1, SDS((G, 1, 1), F32), s3, s3])(
        lr, li, logdt, br, bi, dabr, dabi, dbbr, dbbi)


def _s5_tables(abr, abi, rows, name):
    def body(ar_ref, ai_ref, pfr, pfi, pbr, pbi):
        pfr[0:1, :] = ar_ref[...]
        pfi[0:1, :] = ai_ref[...]
        pbr[rows - 1:rows, :] = ar_ref[...]
        pbi[rows - 1:rows, :] = ai_ref[...]
        n = 1
        while n < rows:
            er, ei = pfr[n - 1:n, :], pfi[n - 1:n, :]
            xr, xi = pfr[0:n, :], pfi[0:n, :]
            pfr[n:2 * n, :] = er * xr - ei * xi
            pfi[n:2 * n, :] = er * xi + ei * xr
            yr, yi = pbr[rows - n:rows, :], pbi[rows - n:rows, :]
            pbr[rows - 2 * n:rows - n, :] = er * yr - ei * yi
            pbi[rows - 2 * n:rows - n, :] = er * yi + ei * yr
            n *= 2

    s = SDS((rows, NSTATE), F32)
    return pl.pallas_call(body, name=name, out_shape=[s, s, s, s], compiler_params=_cp(40))(abr, abi)


def _cscan(br, bi, pr_ref, pi_ref, reverse):
    T = br.shape[0]
    sign = -1.0 if reverse else 1.0
    row = lax.broadcasted_iota(jnp.int32, br.shape, 0)
    k = 1
    while k < T:
        akr = pr_ref[k - 1:k, :]
        aki = sign * pi_ref[k - 1:k, :]

        def shift(v):
            if k % 8 == 0:
                z = jnp.zeros((k, v.shape[1]), v.dtype)
                return jnp.concatenate([v[k:], z], 0) if reverse else jnp.concatenate([z, v[:T - k]], 0)
            if reverse:
                return jnp.where(row < T - k, pltpu.roll(v, T - k, 0), 0.0)
            return jnp.where(row >= k, pltpu.roll(v, k, 0), 0.0)

        sr, si = shift(br), shift(bi)
        br, bi = br + akr * sr - aki * si, bi + akr * si + aki * sr
        k *= 2
    return br, bi


def _embed(t):
    a, b = t.shape[1], t.shape[2]
    return jnp.einsum("jgab,gh->jgahb", t.reshape(NJ, GB, a, b), jnp.eye(GB, dtype=t.dtype)).reshape(NJ, GB * a, GB * b)


def _diag_blocks(t, a, b):
    return jnp.einsum("jgahb,gh->jgab", t.reshape(NJ, GB, a, GB, b), jnp.eye(GB, dtype=t.dtype)).reshape(G, a, b)


NT = 16


def _chunks(L):
    ncb = min(CH, L // NT)
    return ncb, NT * ncb


def _cmul_add(ar, ai, xr, xi, br, bi):
    return ar * xr - ai * xi + br, ar * xi + ai * xr + bi


def _pow_weights(w_ref, pwr_ref, pwi_ref, dst, adjoint):
    w = w_ref[...].astype(F32)
    wr, wi = w[:, :SW], w[:, SW:]
    for t in range(NT):
        k = t if adjoint else NT - 1 - t
        if k == 0:
            blk = w
        else:
            pr, pi = pwr_ref[k - 1:k, :], pwi_ref[k - 1:k, :]
            if adjoint:
                blk = jnp.concatenate([pr * wr + pi * wi, pr * wi - pi * wr], axis=1)
            else:
                blk = jnp.concatenate([pr * wr - pi * wi, pr * wi + pi * wr], axis=1)
        dst[t * UW:(t + 1) * UW, :] = blk.astype(dst.dtype)


def _s5_states(u_ref, bb_ref, bbp_scr, par_ref, pai_ref, cr, ci, ncb, bu_scr):
    us = [u_ref[pl.ds(t, ncb, stride=NT), :] for t in range(NT)]
    for t in range(NT):
        bu_scr[t] = _mm(us[t], bb_ref[...])
    e = _mm(jnp.concatenate(us, axis=1), bbp_scr[...])
    xr, xi = _cscan(e[:, :SW], e[:, SW:], par_ref, pai_ref, False)
    fr, fi = _cmul_add(par_ref[0:ncb, :], pai_ref[0:ncb, :], cr, ci, xr, xi)
    row = lax.broadcasted_iota(jnp.int32, fr.shape, 0)
    cinr = jnp.where(row >= 1, pltpu.roll(fr, 1, 0), cr)
    cini = jnp.where(row >= 1, pltpu.roll(fi, 1, 0), ci)
    return cinr, cini, jnp.concatenate([fr[ncb - 1:ncb, :], fi[ncb - 1:ncb, :]], axis=1)


def _slices(ref, ncb, axis):
    return jnp.concatenate([ref[pl.ds(t, ncb, stride=NT), :] for t in range(NT)], axis=axis)


def _s5_scan_fwd(u3, bb, cm, pwr, pwi, pwrt, pwit, par, pai, dskip):
    L = u3.shape[1]
    ncb, tb = _chunks(L)
    nb = L // tb

    def build(bb_ref, cm_ref, pwr_ref, pwi_ref, pct_ref, pit_ref, bbp_scr, ktp_scr, zc_scr):
        w = bb_ref[...].astype(F32)
        wr, wi = w[:, :SW], w[:, SW:]
        cmv = cm_ref[...].astype(F32)
        ct, cb = cmv[:SW, :], cmv[SW:, :]
        zero = jnp.zeros((UW, UW), MXU)
        for tau in range(NT):
            if tau == 0:
                blk = w
            else:
                pr, pi = pwr_ref[tau - 1:tau, :], pwi_ref[tau - 1:tau, :]
                blk = jnp.concatenate([pr * wr - pi * wi, pr * wi + pi * wr], axis=1)
            blk = blk.astype(MXU)
            bbp_scr[(NT - 1 - tau) * UW:(NT - tau) * UW, :] = blk
            k = jnp.dot(blk, cm_ref[...], preferred_element_type=F32).astype(MXU)
            for j in range(NT - tau):
                ktp_scr[j * UW:(j + 1) * UW, (j + tau) * UW:(j + tau + 1) * UW] = k
            pc, pic = pct_ref[:, tau:tau + 1], pit_ref[:, tau:tau + 1]
            zc_scr[:, tau * UW:(tau + 1) * UW] = jnp.concatenate(
                [pc * ct + pic * cb, pc * cb - pic * ct], axis=0).astype(MXU)
        for j in range(NT):
            for t in range(j):
                ktp_scr[j * UW:(j + 1) * UW, t * UW:(t + 1) * UW] = zero

    def body(u_ref, bb_ref, cm_ref, pwr_ref, pwi_ref, pct_ref, pit_ref, par_ref, pai_ref, d_ref, ypre_ref, st_ref,
             carry, cin_scr, bbp_scr, ktp_scr, zc_scr):
        @pl.when(pl.program_id(1) == 0)
        def _():
            carry[...] = jnp.zeros_like(carry)
            build(bb_ref, cm_ref, pwr_ref, pwi_ref, pct_ref, pit_ref, bbp_scr, ktp_scr, zc_scr)

        c = carry[...]
        st_ref[...] = c
        ucat = _slices(u_ref, ncb, 1)
        ub = ucat.astype(MXU)
        e = jnp.dot(ub, bbp_scr[...], preferred_element_type=F32)
        xr, xi = _cscan(e[:, :SW], e[:, SW:], par_ref, pai_ref, False)
        cr, ci = c[:, :SW], c[:, SW:]
        fr, fi = _cmul_add(par_ref[0:ncb, :], pai_ref[0:ncb, :], cr, ci, xr, xi)
        carry[...] = jnp.concatenate([fr[ncb - 1:ncb, :], fi[ncb - 1:ncb, :]], axis=1)
        row = lax.broadcasted_iota(jnp.int32, fr.shape, 0)
        cin_scr[:, :SW] = jnp.where(row >= 1, pltpu.roll(fr, 1, 0), cr).astype(MXU)
        cin_scr[:, SW:] = jnp.where(row >= 1, pltpu.roll(fi, 1, 0), ci).astype(MXU)
        y = (jnp.dot(ub, ktp_scr[...], preferred_element_type=F32)
             + jnp.dot(cin_scr[...], zc_scr[...], preferred_element_type=F32)
             + jnp.tile(d_ref[...], (1, NT)) * ucat)
        for t in range(NT):
            ypre_ref[pl.ds(t, ncb, stride=NT), :] = y[:, t * UW:(t + 1) * UW]

    tab = pl.BlockSpec((CH, SW), lambda j, i: (0, j))
    stp = pl.BlockSpec((NT, SW), lambda j, i: (0, j))
    stt = pl.BlockSpec((SW, NT), lambda j, i: (j, 0))
    vec = lambda w: pl.BlockSpec((1, w), lambda j, i: (0, j))
    return pl.pallas_call(
        body, name="s5_scan_fwd", grid=(NJ, nb),
        in_specs=[pl.BlockSpec((None, tb, UW), lambda j, i: (j, i, 0)),
                  pl.BlockSpec((None, UW, 2 * SW), lambda j, i: (j, 0, 0)),
                  pl.BlockSpec((None, 2 * SW, UW), lambda j, i: (j, 0, 0)),
                  stp, stp, stt, stt, tab, tab, vec(UW)],
        out_specs=[pl.BlockSpec((None, tb, UW), lambda j, i: (j, i, 0)),
                   pl.BlockSpec((None, None, 1, 2 * SW), lambda j, i: (j, i, 0, 0))],
        out_shape=[_lane_blocks(L), SDS((NJ, nb, 1, 2 * SW), F32)],
        scratch_shapes=[pltpu.VMEM((1, 2 * SW), F32), pltpu.VMEM((ncb, 2 * SW), MXU),
                        pltpu.VMEM((NT * UW, 2 * SW), MXU), pltpu.VMEM((NT * UW, NT * UW), MXU),
                        pltpu.VMEM((2 * SW, NT * UW), MXU)],
        compiler_params=_cp(56, ARB2),
    )(u3, bb, cm, pwr, pwi, pwrt, pwit, par, pai, dskip)


def _s5_gate_bwd(ypre3, p, dx1, wout_e, wglu, bglu, deps=NO_DEPS):
    L = p.shape[0]
    tm = min(256, L)

    def body(y_ref, az_ref, dx1_ref, wo_ref, wg_ref, bg_ref, *rest):
        dyp_ref, daz_ref, yg_ref, dt_ref, ya_ref, gbg_ref = rest[len(deps):]

        @pl.when(pl.program_id(0) == 0)
        def _():
            gbg_ref[...] = jnp.zeros_like(gbg_ref)

        yg, dgelu = _gelu_and_grad(_from_lane_blocks(y_ref))
        sg = jax.nn.sigmoid(_mm(yg, wg_ref[...]) + bg_ref[...])
        act, dact = _silu_and_grad(az_ref[...])
        y2 = yg * sg
        dya = _mm_nt(dx1_ref[...], wo_ref[...])
        daz_ref[...] = (dya * y2 * dact).astype(daz_ref.dtype)
        dy2 = dya * act
        dt = dy2 * yg * sg * (1.0 - sg)
        dyg = dy2 * sg + _mm_nt(dt, wg_ref[...])
        _to_lane_blocks(dyp_ref, dyg * dgelu)
        yg_ref[...] = yg.astype(yg_ref.dtype)
        dt_ref[...] = dt.astype(dt_ref.dtype)
        ya_ref[...] = (y2 * act).astype(ya_ref.dtype)
        gbg_ref[...] += jnp.sum(dt, axis=0, keepdims=True)

    row = pl.BlockSpec((tm, D), lambda i: (i, 0))
    return pl.pallas_call(
        body, name="s5_gate_bwd", grid=(L // tm,),
        in_specs=[_lane_block_spec(tm, lambda i: i), pl.BlockSpec((tm, D), lambda i: (i, 1)), row,
                  pl.BlockSpec((D, D), lambda i: (0, 0)), _full((D, D)), _full((1, D))] + [ANY_SPEC] * len(deps),
        out_specs=[_lane_block_spec(tm, lambda i: i), row, row, row, row, _full((1, D))],
        out_shape=[_lane_blocks(L), SDS((L, D), MXU), SDS((L, D), MXU), SDS((L, D), MXU), SDS((L, D), MXU),
                   SDS((1, D), F32)],
        compiler_params=_cp(40, ("arbitrary",)),
    )(ypre3, p, dx1, wout_e, wglu, bglu, *deps)


def _s5_scan_bwd(u3, dypre3, states, bb, cm, cmt, abr, abi, pwr, pwi, par, pai, pbr, pbi, dskip, deps=NO_DEPS):
    L = u3.shape[1]
    ncb, tb = _chunks(L)
    nb = L // tb
    rev = lambda i: nb - 1 - i

    def body(u_ref, dy_ref, st_ref, bb_ref, cm_ref, cmt_ref, ar_ref, ai_ref, pwr_ref, pwi_ref, par_ref, pai_ref,
             pbr_ref, pbi_ref, d_ref, *rest):
        (du_ref, gd_ref, gcm_ref, gbb_ref, gar_ref, gai_ref,
         lcarry, bu_scr, s_scr, gs_scr, bbp_scr, cmp_scr) = rest[len(deps):]
        del cm_ref

        @pl.when(pl.program_id(1) == 0)
        def _():
            _pow_weights(bb_ref, pwr_ref, pwi_ref, bbp_scr, False)
            _pow_weights(cmt_ref, pwr_ref, pwi_ref, cmp_scr, True)
            lcarry[...] = jnp.zeros_like(lcarry)
            gd_ref[...] = jnp.zeros_like(gd_ref)
            gcm_ref[...] = jnp.zeros_like(gcm_ref)
            gbb_ref[...] = jnp.zeros_like(gbb_ref)
            gar_ref[...] = jnp.zeros_like(gar_ref)
            gai_ref[...] = jnp.zeros_like(gai_ref)

        ar, ai = ar_ref[...], ai_ref[...]
        c = st_ref[...]
        sr, si, _ = _s5_states(u_ref, bb_ref, bbp_scr, par_ref, pai_ref, c[:, :SW], c[:, SW:], ncb, bu_scr)
        s_scr[0] = jnp.concatenate([sr, si], axis=1)
        for t in range(NT):
            bu = bu_scr[t]
            sr, si = _cmul_add(ar, ai, sr, si, bu[:, :SW], bu[:, SW:])
            s_scr[t + 1] = jnp.concatenate([sr, si], axis=1)
        dys = [dy_ref[pl.ds(t, ncb, stride=NT), :] for t in range(NT)]
        for t in range(NT):
            gs_scr[t] = _mm(dys[t], cmt_ref[...])
        f = _mm(jnp.concatenate(dys, axis=1), cmp_scr[...])
        xr, xi = _cscan(f[:, :SW], f[:, SW:], par_ref, pai_ref, True)
        lc = lcarry[...]
        lcr, lci = lc[:, :SW], lc[:, SW:]
        hr, hi = _cmul_add(pbr_ref[CH - ncb:CH, :], -pbi_ref[CH - ncb:CH, :], lcr, lci, xr, xi)
        lcarry[...] = jnp.concatenate([hr[0:1, :], hi[0:1, :]], axis=1)
        row = lax.broadcasted_iota(jnp.int32, hr.shape, 0)
        lr_ = jnp.where(row < ncb - 1, pltpu.roll(hr, ncb - 1, 0), lcr)
        li_ = jnp.where(row < ncb - 1, pltpu.roll(hi, ncb - 1, 0), lci)
        gar = jnp.zeros((1, SW), F32)
        gai = jnp.zeros((1, SW), F32)
        for t in reversed(range(NT)):
            gs = gs_scr[t]
            lr_, li_ = _cmul_add(ar, -ai, lr_, li_, gs[:, :SW], gs[:, SW:])
            rows = pl.ds(t, ncb, stride=NT)
            u_t, dy_t = u_ref[rows, :], dy_ref[rows, :]
            lam = jnp.concatenate([lr_, li_], axis=1)
            gbb_ref[...] += _mm_tn(u_t, lam)
            du_ref[rows, :] = _mm_nt(lam, bb_ref[...]) + dy_t * d_ref[...]
            gd_ref[...] += jnp.sum(dy_t * u_t, axis=0, keepdims=True)
            gcm_ref[...] += _mm_tn(s_scr[t + 1], dy_t)
            sp = s_scr[t]
            spr, spi = sp[:, :SW], sp[:, SW:]
            gar += jnp.sum(lr_ * spr + li_ * spi, axis=0, keepdims=True)
            gai += jnp.sum(li_ * spr - lr_ * spi, axis=0, keepdims=True)
        gar_ref[...] += gar
        gai_ref[...] += gai

    tab = pl.BlockSpec((CH, SW), lambda j, i: (0, j))
    stp = pl.BlockSpec((NT, SW), lambda j, i: (0, j))
    colblk = pl.BlockSpec((None, tb, UW), lambda j, i: (j, rev(i), 0))
    vec = lambda w: pl.BlockSpec((1, w), lambda j, i: (0, j))
    return pl.pallas_call(
        body, name="s5_scan_bwd", grid=(NJ, nb),
        in_specs=[colblk, colblk,
                  pl.BlockSpec((None, None, 1, 2 * SW), lambda j, i: (j, rev(i), 0, 0)),
                  pl.BlockSpec((None, UW, 2 * SW), lambda j, i: (j, 0, 0)),
                  pl.BlockSpec((None, 2 * SW, UW), lambda j, i: (j, 0, 0)),
                  pl.BlockSpec((None, UW, 2 * SW), lambda j, i: (j, 0, 0)),
                  vec(SW), vec(SW), stp, stp, tab, tab, tab, tab, vec(UW)] + [ANY_SPEC] * len(deps),
        out_specs=[colblk, vec(UW),
                   pl.BlockSpec((None, 2 * SW, UW), lambda j, i: (j, 0, 0)),
                   pl.BlockSpec((None, UW, 2 * SW), lambda j, i: (j, 0, 0)),
                   vec(SW), vec(SW)],
        out_shape=[_lane_blocks(L), SDS((1, D), F32),
                   SDS((NJ, 2 * SW, UW), F32), SDS((NJ, UW, 2 * SW), F32),
                   SDS((1, NSTATE), F32), SDS((1, NSTATE), F32)],
        scratch_shapes=[pltpu.VMEM((1, 2 * SW), F32), pltpu.VMEM((NT, ncb, 2 * SW), F32),
                        pltpu.VMEM((NT + 1, ncb, 2 * SW), F32), pltpu.VMEM((NT, ncb, 2 * SW), F32),
                        pltpu.VMEM((NT * UW, 2 * SW), MXU), pltpu.VMEM((NT * UW, 2 * SW), MXU)],
        compiler_params=_cp(56, ARB2),
    )(u3, dypre3, states, bb, cm, cmt, abr, abi, pwr, pwi, par, pai, pbr, pbi, dskip, *deps)


def _rope_tables(L, inv):
    tm = min(512, L)

    def body(inv_ref, cos_ref, sin_ref):
        pos = (lax.broadcasted_iota(jnp.int32, (tm, DK // 2), 0) + pl.program_id(0) * tm).astype(F32)
        ang = pos * inv_ref[...]
        cos_ref[...] = jnp.cos(ang)
        sin_ref[...] = jnp.sin(ang)

    blk = pl.BlockSpec((tm, DK // 2), lambda i: (i, 0))
    return pl.pallas_call(body, name="rope_tables", grid=(L // tm,), in_specs=[_full((1, DK // 2))],
                          out_specs=[blk, blk], out_shape=[SDS((L, DK // 2), F32)] * 2)(inv)


def _rot(x, cos, sin):
    x1, x2 = x[:, :DK // 2], x[:, DK // 2:]
    return jnp.concatenate([x1 * cos - x2 * sin, x1 * sin + x2 * cos], axis=1)


def _unrot(d, cos, sin):
    d1, d2 = d[:, :DK // 2], d[:, DK // 2:]
    return jnp.concatenate([d1 * cos + d2 * sin, d2 * cos - d1 * sin], axis=1)


def _ret_decays(h):
    lg = LOG_G[h]
    n = lax.broadcasted_iota(jnp.int32, (CH, CH), 0)
    m = lax.broadcasted_iota(jnp.int32, (CH, CH), 1)
    diff = (n - m).astype(F32)
    decay = jnp.where(n >= m, jnp.exp(lg * jnp.maximum(diff, 0.0)), 0.0)
    idx = lax.broadcasted_iota(jnp.int32, (CH, 1), 0).astype(F32)
    xi = jnp.exp(lg * (idx + 1.0))
    zeta = jnp.exp(lg * (CH - 1.0 - idx))
    return decay, xi, zeta, math.exp(lg * CH)


def _ret_tables(dec_scr, vec_scr):
    for h in range(HEADS):
        decay, xi, zeta, _ = _ret_decays(h)
        dec_scr[h] = decay
        vec_scr[h] = jnp.concatenate([jnp.broadcast_to(xi, (CH, 128)), jnp.broadcast_to(zeta, (CH, 128))], axis=1)


def _ret_chunk_fwd(q, k, v, cos, sin, s_prev_b, decay, xi, zeta):
    qr = _rot(q, cos, sin)
    kr = _rot(k, cos, sin) * (DK ** -0.5)
    scores = _mm_nt(qr, kr) * decay
    o = _mm(scores, v) + _mm(qr * xi, s_prev_b)
    local = _mm_tn(kr * zeta, v)
    mu = jnp.mean(o, axis=-1, keepdims=True)
    oc = o - mu
    rstd = lax.rsqrt(jnp.mean(oc * oc, axis=-1, keepdims=True) + EPS)
    return qr, kr, scores, local, oc * rstd, rstd


def _ret_fwd(p, cos, sin, gain):
    L = p.shape[0]
    nb = L // CH

    def body(q_ref, k_ref, v_ref, bz_ref, cos_ref, sin_ref, g_ref, yb_ref, st_ref, state, dec_scr, vec_scr):
        @pl.when(pl.program_id(0) == 0)
        def _():
            state[...] = jnp.zeros_like(state)
            _ret_tables(dec_scr, vec_scr)

        cos, sin = cos_ref[...], sin_ref[...]
        act, _ = _silu_and_grad(bz_ref[...])
        for h in range(HEADS):
            hs = slice(h * DK, (h + 1) * DK)
            xi, zeta = vec_scr[h, :, 0:1], vec_scr[h, :, 128:129]
            s_prev = state[h]
            s_prev_b = s_prev.astype(MXU)
            st_ref[h] = s_prev_b
            _, _, _, local, on, _ = _ret_chunk_fwd(q_ref[:, hs], k_ref[:, hs], v_ref[:, hs], cos, sin,
                                                   s_prev_b, dec_scr[h], xi, zeta)
            state[h] = s_prev * math.exp(LOG_G[h] * CH) + local
            yb_ref[:, hs] = (on * g_ref[:, hs] * act[:, hs]).astype(yb_ref.dtype)

    col = lambda c: pl.BlockSpec((CH, D), lambda i: (i, c))
    rope = pl.BlockSpec((CH, DK // 2), lambda i: (i, 0))
    return pl.pallas_call(
        body, name="ret_fwd", grid=(nb,),
        in_specs=[col(2), col(3), col(4), col(5), rope, rope, _full((1, D))],
        out_specs=[pl.BlockSpec((CH, D), lambda i: (i, 0)),
                   pl.BlockSpec((None, HEADS, DK, DK), lambda i: (i, 0, 0, 0))],
        out_shape=[SDS((L, D), MXU), SDS((nb, HEADS, DK, DK), MXU)],
        scratch_shapes=[pltpu.VMEM((HEADS, DK, DK), F32), pltpu.VMEM((HEADS, CH, CH), F32),
                        pltpu.VMEM((HEADS, CH, 256), F32)],
        compiler_params=_cp(40, ("arbitrary",)),
    )(p, p, p, p, cos, sin, gain)


def _ret_bwd(p, cos, sin, gain, states, dx1, wout_e, du, daz, deps=NO_DEPS):
    L = p.shape[0]
    nb = L // CH
    rev = lambda i: nb - 1 - i

    def body(q_ref, k_ref, v_ref, bz_ref, cos_ref, sin_ref, g_ref, st_ref, dx1_ref, wo_ref, du_ref, daz_ref, *rest):
        dp_ref, yb_ref, gg_ref, gstate, dec_scr, vec_scr = rest[len(deps):]

        @pl.when(pl.program_id(0) == 0)
        def _():
            gstate[...] = jnp.zeros_like(gstate)
            gg_ref[...] = jnp.zeros_like(gg_ref)
            _ret_tables(dec_scr, vec_scr)

        cos, sin = cos_ref[...], sin_ref[...]
        act, dact = _silu_and_grad(bz_ref[...])
        dyb = _mm_nt(dx1_ref[...], wo_ref[...])
        dp_ref[:, 0:D] = _from_lane_blocks(du_ref).astype(dp_ref.dtype)
        dp_ref[:, D:2 * D] = daz_ref[...]
        for h in range(HEADS):
            hs = slice(h * DK, (h + 1) * DK)
            col = lambda part: slice((2 + part) * D + h * DK, (2 + part) * D + (h + 1) * DK)
            decay = dec_scr[h]
            xi, zeta = vec_scr[h, :, 0:1], vec_scr[h, :, 128:129]
            v = v_ref[:, hs]
            s_prev_b = st_ref[h]
            qr, kr, scores, _, on, rstd = _ret_chunk_fwd(q_ref[:, hs], k_ref[:, hs], v, cos, sin, s_prev_b,
                                                         decay, xi, zeta)
            gain_h = g_ref[:, hs]
            out = on * gain_h
            yb_ref[:, hs] = (out * act[:, hs]).astype(yb_ref.dtype)
            dyb_h = dyb[:, hs]
            dp_ref[:, col(3)] = (dyb_h * out * dact[:, hs]).astype(dp_ref.dtype)
            dout = dyb_h * act[:, hs]
            gg_ref[:, hs] += jnp.sum(dout * on, axis=0, keepdims=True)
            don = dout * gain_h
            do = rstd * (don - jnp.mean(don, axis=-1, keepdims=True)
                         - on * jnp.mean(don * on, axis=-1, keepdims=True))
            gnext = gstate[h]
            gnext_b = gnext.astype(MXU)
            dscores = _mm_nt(do, v) * decay
            dp_ref[:, col(2)] = (_mm_tn(scores, do) + _mm(kr * zeta, gnext_b)).astype(dp_ref.dtype)
            dqr = _mm(dscores, kr) + _mm_nt(do, s_prev_b) * xi
            dkr = _mm_tn(dscores, qr) + _mm_nt(v, gnext_b) * zeta
            gstate[h] = gnext * math.exp(LOG_G[h] * CH) + _mm_tn(qr * xi, do)
            dp_ref[:, col(0)] = _unrot(dqr, cos, sin).astype(dp_ref.dtype)
            dp_ref[:, col(1)] = (_unrot(dkr, cos, sin) * (DK ** -0.5)).astype(dp_ref.dtype)

    col = lambda c: pl.BlockSpec((CH, D), lambda i: (rev(i), c))
    rope = pl.BlockSpec((CH, DK // 2), lambda i: (rev(i), 0))
    outc = col(0)
    act_out = SDS((L, D), MXU)
    return pl.pallas_call(
        body, name="ret_bwd", grid=(nb,),
        in_specs=[col(2), col(3), col(4), col(5), rope, rope, _full((1, D)),
                  pl.BlockSpec((None, HEADS, DK, DK), lambda i: (rev(i), 0, 0, 0)),
                  outc, pl.BlockSpec((D, D), lambda i: (1, 0)), _lane_block_spec(CH, rev), outc]
        + [ANY_SPEC] * len(deps),
        out_specs=[pl.BlockSpec((CH, NIN), lambda i: (rev(i), 0)), outc, _full((1, D))],
        out_shape=[SDS((L, NIN), MXU), act_out, SDS((1, D), F32)],
        scratch_shapes=[pltpu.VMEM((HEADS, DK, DK), F32), pltpu.VMEM((HEADS, CH, CH), F32),
                        pltpu.VMEM((HEADS, CH, 256), F32)],
        compiler_params=_cp(48, ("arbitrary",)),
    )(p, p, p, p, cos, sin, gain, states, dx1, wout_e, du, daz, *deps)


def _out_even(x, ypre3, p, yb, wglu, bglu, wout):
    L = x.shape[0]
    tm = min(256, L)

    def body(x_ref, y_ref, az_ref, yb_ref, wg_ref, bg_ref, w_ref, o_ref):
        yg = _gelu(_from_lane_blocks(y_ref))
        t = _mm(yg, wg_ref[...]) + bg_ref[...]
        act, _ = _silu_and_grad(az_ref[...])
        ya = (yg * jax.nn.sigmoid(t) * act).astype(MXU)
        cat = jnp.concatenate([ya, yb_ref[...]], axis=1)
        o_ref[...] = x_ref[...] + jnp.dot(cat, w_ref[...], preferred_element_type=F32)

    row = pl.BlockSpec((tm, D), lambda i: (i, 0))
    return pl.pallas_call(
        body, name="out_even", grid=(L // tm,),
        in_specs=[row, _lane_block_spec(tm, lambda i: i), pl.BlockSpec((tm, D), lambda i: (i, 1)), row,
                  _full((D, D)), _full((1, D)), _full((DI, D))],
        out_specs=row, out_shape=SDS((L, D), F32), compiler_params=_cp(40, ("arbitrary",)),
    )(x, ypre3, p, yb, wglu, bglu, wout)


def _sgu_core(pv, gain, ws_ref, bs_ref):
    pu, pvv, z = pv[:, :DI], pv[:, DI:2 * DI], pv[:, 2 * DI:]
    u, gu = _gelu_and_grad(pu)
    v, gv = _gelu_and_grad(pvv)
    mu = jnp.mean(v, axis=-1, keepdims=True)
    vc = v - mu
    rstd = lax.rsqrt(jnp.mean(vc * vc, axis=-1, keepdims=True) + EPS)
    vhat = vc * rstd
    vn = vhat * gain
    t = lax.broadcasted_iota(jnp.int32, (CH, CH), 0)
    s_ = lax.broadcasted_iota(jnp.int32, (CH, CH), 1)
    mask = t >= s_
    wm = [jnp.where(mask, ws_ref[g], 0.0).astype(MXU) for g in range(SG)]
    s = jnp.concatenate([_mm(wm[g], vn[:, g * SGD:(g + 1) * SGD]) + bs_ref[g] for g in range(SG)], axis=1)
    return gu, gv, z, u, vhat, rstd, vn, mask, wm, s


def _sgu_fwd_bwd(p2, x1, gain, wsp, bsp, wout, fnorm, tgt):
    L = p2.shape[0]

    def body(p_ref, x1_ref, g_ref, ws_ref, bs_ref, wo_ref, fn_ref, t_ref,
             dp_ref, y_ref, dx2_ref, gg_ref, gws_ref, gbs_ref, gfn_ref, loss_ref):
        @pl.when(pl.program_id(0) == 0)
        def _():
            gg_ref[...] = jnp.zeros_like(gg_ref)
            gws_ref[...] = jnp.zeros_like(gws_ref)
            gbs_ref[...] = jnp.zeros_like(gbs_ref)
            gfn_ref[...] = jnp.zeros_like(gfn_ref)
            loss_ref[...] = jnp.zeros_like(loss_ref)

        gain = g_ref[...]
        gu, gv, z, u, vhat, rstd, vn, mask, wm, s = _sgu_core(p_ref[...], gain, ws_ref, bs_ref)
        act, dact = _silu_and_grad(z)
        y = (u * s * act).astype(MXU)
        y_ref[...] = y
        x2 = x1_ref[...] + jnp.dot(y, wo_ref[...], preferred_element_type=F32)
        xhat, r = _rms(x2)
        fn = fn_ref[...]
        e = xhat * fn - t_ref[...]
        loss_ref[...] += 0.5 * jnp.sum(jnp.mean(e * e, axis=-1, keepdims=True), axis=0, keepdims=True)
        do = e * (1.0 / D)
        gfn_ref[...] += jnp.sum(do * xhat, axis=0, keepdims=True)
        dxhat = do * fn
        dx2 = r * (dxhat - xhat * jnp.mean(dxhat * xhat, axis=-1, keepdims=True))
        dx2_ref[...] = dx2
        dy = _mm_nt(dx2, wo_ref[...])
        du = dy * s * act
        ds = dy * u * act
        dz = dy * u * s * dact
        dvn = []
        for g in range(SG):
            ds_g = ds[:, g * SGD:(g + 1) * SGD]
            vn_g = vn[:, g * SGD:(g + 1) * SGD]
            gbs_ref[g] += jnp.sum(ds_g, axis=1, keepdims=True)
            gws_ref[g] += jnp.where(mask, _mm_nt(ds_g, vn_g), 0.0)
            dvn.append(_mm_tn(wm[g], ds_g))
        dvn = jnp.concatenate(dvn, axis=1)
        gg_ref[...] += jnp.sum(dvn * vhat, axis=0, keepdims=True)
        dvhat = dvn * gain
        dv = rstd * (dvhat - jnp.mean(dvhat, axis=-1, keepdims=True)
                     - vhat * jnp.mean(dvhat * vhat, axis=-1, keepdims=True))
        dp_ref[...] = jnp.concatenate([du * gu, dv * gv, dz], axis=1).astype(dp_ref.dtype)

    row = pl.BlockSpec((CH, D), lambda i: (i, 0))
    wide = pl.BlockSpec((CH, NIN), lambda i: (i, 0))
    return pl.pallas_call(
        body, name="sgu_fwd_bwd", grid=(L // CH,),
        in_specs=[wide, row, _full((1, DI)), _full((SG, CH, CH)), _full((SG, CH, 1)), _full((DI, D)),
                  _full((1, D)), row],
        out_specs=[wide, pl.BlockSpec((CH, DI), lambda i: (i, 0)), row,
                   _full((1, DI)), _full((SG, CH, CH)), _full((SG, CH, 1)), _full((1, D)), _full((1, 128))],
        out_shape=[SDS((L, NIN), MXU), SDS((L, DI), MXU), SDS((L, D), F32), SDS((1, DI), F32),
                   SDS((SG, CH, CH), F32), SDS((SG, CH, 1), F32), SDS((1, D), F32), SDS((1, 128), F32)],
        compiler_params=_cp(48, ("arbitrary",)),
    )(p2, x1, gain, wsp, bsp, wout, fnorm, tgt)


def _my_index():
    return 4 * lax.axis_index("x") + 2 * lax.axis_index("y") + lax.axis_index("c")


def _ordered_sum(land_ref, own, me):
    g = None
    for s in range(NDEV):
        part = jnp.where(me == s, own, land_ref[s].astype(F32))
        g = part if g is None else g + part
    return g


def _adamw_math(w, m, v, g):
    mn = ADAM_B1 * m + (1.0 - ADAM_B1) * g
    vn = ADAM_B2 * v + (1.0 - ADAM_B2) * (g * g)
    mhat = mn / BC1
    vhat = vn / BC2
    return g, -ADAM_LR * (mhat / (jnp.sqrt(vhat) + ADAM_EPS) + ADAM_WD * w), mn, vn


def _adamw(w, m, v, land, own, name):
    R, C = w.shape
    tr = R
    for cand in (256, 128, 64, 32, 16, 8):
        if R % cand == 0 and R > cand:
            tr = cand
            break

    def body(w_ref, m_ref, v_ref, land_ref, own_ref, g_ref, d_ref, mo_ref, vo_ref):
        g = _ordered_sum(land_ref, own_ref[...].astype(F32), _my_index())
        for o, val in zip((g_ref, d_ref, mo_ref, vo_ref), _adamw_math(w_ref[...], m_ref[...], v_ref[...], g)):
            o[...] = val

    blk = pl.BlockSpec((tr, C), lambda i: (i, 0))
    out = SDS((R, C), F32)
    return pl.pallas_call(
        body, name=name, grid=(R // tr,),
        in_specs=[blk, blk, blk, pl.BlockSpec((NDEV, tr, C), lambda i: (0, i, 0)), blk],
        out_specs=[blk, blk, blk, blk], out_shape=[out, out, out, out],
        compiler_params=_cp(40, ("arbitrary",)),
    )(w, m, v, land, own)


def _adamw_many(ws, ms, vs, lands, owns, name):
    k = len(ws)

    def body(*refs):
        ins, outs = refs[:5 * k], refs[5 * k:]
        me = _my_index()
        for i in range(k):
            w_ref, m_ref, v_ref, land_ref, own_ref = (ins[j * k + i] for j in range(5))
            g = _ordered_sum(land_ref, own_ref[...], me)
            for j, val in enumerate(_adamw_math(w_ref[...], m_ref[...], v_ref[...], g)):
                outs[j * k + i][...] = val

    out_shape = [SDS(w.shape, F32) for _ in range(4) for w in ws]
    res = pl.pallas_call(body, name=name, out_shape=out_shape, compiler_params=_cp(60))(*ws, *ms, *vs, *lands, *owns)
    return [res[j * k:(j + 1) * k] for j in range(4)]


MESH = pl.DeviceIdType.MESH
HBM_SPEC = pl.BlockSpec(memory_space=pltpu.HBM)
SEM_SPEC = pl.BlockSpec(memory_space=pltpu.SEMAPHORE)
EFFECT = pltpu.SideEffectType.DATAFLOW_SIDE_EFFECTING


def _me_and_peers():
    x, y, c = lax.axis_index("x"), lax.axis_index("y"), lax.axis_index("c")
    me = 4 * x + 2 * y + c
    peers = []
    for r in range(1, NDEV):
        px, py, pc = x ^ ((r >> 2) & 1), y ^ ((r >> 1) & 1), c ^ (r & 1)
        peers.append(((px, py, pc), 4 * px + 2 * py + pc))
    return me, peers


def _land_shape(a, scatter):
    return (NDEV,) + (a.shape[1:] if scatter else a.shape)


def _remote(src, dst, send_sems, recv_sems, r, k, n, dev):
    i = r * n + k
    return pltpu.make_async_remote_copy(src_ref=src, dst_ref=dst, send_sem=send_sems.at[i], recv_sem=recv_sems.at[i],
                                        device_id=dev, device_id_type=MESH)


def _exchange(arrays, scatter, name):
    n = len(arrays)
    out_shape = [SDS(_land_shape(a, scatter), a.dtype) for a in arrays]

    def body(*refs):
        ins, outs = refs[:n], refs[n:2 * n]
        send_sems, recv_sems, loc_sems = refs[2 * n:]
        me, peers = _me_and_peers()
        local = []
        for k in range(n):
            src = ins[k].at[me] if scatter else ins[k]
            cp = pltpu.make_async_copy(src, outs[k].at[me], loc_sems.at[k])
            cp.start()
            local.append(cp)
        sends = []
        for r, (dev, lin) in enumerate(peers):
            for k in range(n):
                src = ins[k].at[lin] if scatter else ins[k]
                cp = _remote(src, outs[k].at[me], send_sems, recv_sems, r, k, n, dev)
                cp.start()
                sends.append(cp)
        for r, (dev, lin) in enumerate(peers):
            for k in range(n):
                src = ins[k].at[me] if scatter else ins[k]
                _remote(src, outs[k].at[lin], send_sems, recv_sems, r, k, n, dev).wait_recv()
        for cp in sends:
            cp.wait_send()
        for cp in local:
            cp.wait()

    return pl.pallas_call(
        body, name=name, in_specs=[HBM_SPEC] * n, out_specs=[HBM_SPEC] * n, out_shape=out_shape,
        scratch_shapes=[pltpu.SemaphoreType.DMA(((NDEV - 1) * n,)), pltpu.SemaphoreType.DMA(((NDEV - 1) * n,)),
                        pltpu.SemaphoreType.DMA((n,))],
    )(*arrays)


def _exchange_start(arrays, scatter, name):
    n = len(arrays)
    lands = [lax.empty(_land_shape(a, scatter), a.dtype) for a in arrays]

    def body(*refs):
        ins, lnd = refs[:n], refs[n:2 * n]
        send_sems, recv_sems, own_sems = refs[2 * n:2 * n + 3]
        token = refs[-1]
        me, peers = _me_and_peers()
        for r, (dev, lin) in enumerate(peers):
            for k in range(n):
                src = ins[k].at[lin] if scatter else ins[k]
                _remote(src, lnd[k].at[me], send_sems, recv_sems, r, k, n, dev).start()
        if not scatter:
            for k in range(n):
                pltpu.make_async_copy(ins[k], lnd[k].at[me], own_sems.at[k]).start()
        token[...] = jnp.zeros_like(token)

    sem = pltpu.SemaphoreType.DMA(((NDEV - 1) * n,))
    outs = pl.pallas_call(
        body, name=name,
        out_shape=(sem, sem, pltpu.SemaphoreType.DMA((n,)), *[pltpu.HBM(a.shape, a.dtype) for a in arrays],
                   *[pltpu.HBM(l.shape, l.dtype) for l in lands], SDS((8, 128), F32)),
        in_specs=[HBM_SPEC] * (2 * n),
        out_specs=(SEM_SPEC, SEM_SPEC, SEM_SPEC, *[HBM_SPEC] * (2 * n), pl.BlockSpec(memory_space=pltpu.VMEM)),
        input_output_aliases={k: 3 + k for k in range(2 * n)},
        compiler_params=pltpu.CompilerParams(has_side_effects=EFFECT),
    )(*[pltpu.with_memory_space_constraint(a, pltpu.HBM) for a in arrays],
      *[pltpu.with_memory_space_constraint(l, pltpu.HBM) for l in lands])
    return (n, scatter, outs[0], outs[1], outs[2], outs[3:3 + n], outs[3 + n:3 + 2 * n]), outs[-1]


def _exchange_wait(handle, after, name):
    n, scatter, send_sems, recv_sems, own_sems, thru, lands = handle
    after = tuple(after)

    def body(*refs):
        ins, lnd = refs[:n], refs[n:2 * n]
        send_sems, recv_sems, own_sems = refs[2 * n:2 * n + 3]
        me, peers = _me_and_peers()
        for r, (dev, lin) in enumerate(peers):
            for k in range(n):
                src = ins[k].at[lin] if scatter else ins[k]
                cp = _remote(src, lnd[k].at[lin], send_sems, recv_sems, r, k, n, dev)
                cp.wait_send()
                cp.wait_recv()
        if not scatter:
            for k in range(n):
                pltpu.make_async_copy(ins[k], lnd[k].at[me], own_sems.at[k]).wait()

    outs = pl.pallas_call(
        body, name=name,
        out_shape=(*[pltpu.HBM(a.shape, a.dtype) for a in thru], *[pltpu.HBM(l.shape, l.dtype) for l in lands]),
        in_specs=[HBM_SPEC] * (2 * n) + [SEM_SPEC, SEM_SPEC, SEM_SPEC] + [ANY_SPEC] * len(after),
        out_specs=tuple([HBM_SPEC] * (2 * n)),
        input_output_aliases={k: k for k in range(2 * n)},
        compiler_params=pltpu.CompilerParams(has_side_effects=EFFECT),
    )(*thru, *lands, send_sems, recv_sems, own_sems, *after)
    return list(outs[:n]), list(outs[n:])


CHIP_RELATIONS = (2, 4, 6)


def _peer(r):
    x, y, c = lax.axis_index("x"), lax.axis_index("y"), lax.axis_index("c")
    px, py, pc = x ^ ((r >> 2) & 1), y ^ ((r >> 1) & 1), c ^ (r & 1)
    return (px, py, pc), 4 * px + 2 * py + pc


def _copy(src, dst, send_sems, recv_sems, i, dev):
    return pltpu.make_async_remote_copy(src_ref=src, dst_ref=dst, send_sem=send_sems.at[i], recv_sem=recv_sems.at[i],
                                        device_id=dev, device_id_type=MESH)


def _gather2_start(a, name):
    land = lax.empty((NDEV,) + a.shape, a.dtype)

    def body(own, lnd, send_sems, recv_sems, own_sem, own_thru, lnd_thru, token):
        me = _my_index()
        for i, r in enumerate((1,) + CHIP_RELATIONS):
            dev, _ = _peer(r)
            _copy(own, lnd.at[me], send_sems, recv_sems, i, dev).start()
        pltpu.make_async_copy(own, lnd.at[me], own_sem.at[0]).start()
        token[...] = jnp.zeros_like(token)

    sem4 = pltpu.SemaphoreType.DMA((4,))
    outs = pl.pallas_call(
        body, name=name,
        out_shape=(sem4, sem4, pltpu.SemaphoreType.DMA((1,)), pltpu.HBM(a.shape, a.dtype),
                   pltpu.HBM(land.shape, land.dtype), SDS((8, 128), F32)),
        in_specs=[HBM_SPEC, HBM_SPEC],
        out_specs=(SEM_SPEC, SEM_SPEC, SEM_SPEC, HBM_SPEC, HBM_SPEC, pl.BlockSpec(memory_space=pltpu.VMEM)),
        input_output_aliases={0: 3, 1: 4},
        compiler_params=pltpu.CompilerParams(has_side_effects=EFFECT),
    )(pltpu.with_memory_space_constraint(a, pltpu.HBM), pltpu.with_memory_space_constraint(land, pltpu.HBM))
    return outs[:5], outs[5]


def _gather2_forward(handle, after, name):
    send_sems, recv_sems, own_sem, own, land = handle
    after = tuple(after)

    def body(lnd, recv_sems, *rest):
        send2, recv2, lnd_thru = rest[len(after):]
        sib, _ = _peer(1)
        for k, r in enumerate(CHIP_RELATIONS):
            dev, lin = _peer(r)
            _copy(lnd.at[lin], lnd.at[lin], recv_sems, recv_sems, 1 + k, dev).wait_recv()
            _copy(lnd.at[lin], lnd.at[lin], send2, recv2, k, sib).start()

    sem3 = pltpu.SemaphoreType.DMA((3,))
    send2, recv2, land = pl.pallas_call(
        body, name=name,
        out_shape=(sem3, sem3, pltpu.HBM(land.shape, land.dtype)),
        in_specs=[HBM_SPEC, SEM_SPEC] + [ANY_SPEC] * len(after),
        out_specs=(SEM_SPEC, SEM_SPEC, HBM_SPEC),
        input_output_aliases={0: 2},
        compiler_params=pltpu.CompilerParams(has_side_effects=EFFECT),
    )(land, recv_sems, *after)
    return send_sems, recv_sems, own_sem, send2, recv2, own, land


def _gather2_wait(handle, name):
    send_sems, recv_sems, own_sem, send2, recv2, own, land = handle

    def body(own_ref, lnd, send_sems, recv_sems, own_sem, send2, recv2, own_thru, lnd_thru):
        me = _my_index()
        sib, sib_lin = _peer(1)
        for i, r in enumerate((1,) + CHIP_RELATIONS):
            dev, _ = _peer(r)
            _copy(own_ref, lnd.at[me], send_sems, recv_sems, i, dev).wait_send()
        _copy(own_ref, lnd.at[sib_lin], send_sems, recv_sems, 0, sib).wait_recv()
        for k, r in enumerate(CHIP_RELATIONS):
            _, lin = _peer(r)
            _, lin_other = _peer(r ^ 1)
            _copy(lnd.at[lin], lnd.at[lin], send2, recv2, k, sib).wait_send()
            _copy(lnd.at[lin_other], lnd.at[lin_other], send2, recv2, k, sib).wait_recv()
        pltpu.make_async_copy(own_ref, lnd.at[me], own_sem.at[0]).wait()

    outs = pl.pallas_call(
        body, name=name,
        out_shape=(pltpu.HBM(own.shape, own.dtype), pltpu.HBM(land.shape, land.dtype)),
        in_specs=[HBM_SPEC, HBM_SPEC] + [SEM_SPEC] * 5,
        out_specs=(HBM_SPEC, HBM_SPEC),
        input_output_aliases={0: 0, 1: 1},
        compiler_params=pltpu.CompilerParams(has_side_effects=EFFECT),
    )(own, land, send_sems, recv_sems, own_sem, send2, recv2)
    return outs[1]


def _local_step(x, tgt, norm_even, first_weight, lam_re, lam_im, log_dt, b_re, b_im, c_re, c_im, s5_d, bglu,
                ret_gain, wsp, bsp, fnorm, late_weights, emit, start_token=None):
    L = x.shape[0]
    lr3, li3 = lam_re.reshape(G, 1, P), lam_im.reshape(G, 1, P)
    dt3 = log_dt.reshape(G, 1, 1)
    br3, bi3 = jnp.swapaxes(b_re, 1, 2), jnp.swapaxes(b_im, 1, 2)
    abr3, abi3, bbr3, bbi3 = _s5_disc(lr3, li3, dt3, br3, bi3)
    bb = jnp.concatenate([_embed(bbr3), _embed(bbi3)], axis=2).astype(MXU)
    cm = jnp.concatenate([_embed(jnp.swapaxes(c_re, 1, 2)), -_embed(jnp.swapaxes(c_im, 1, 2))], axis=1).astype(MXU)
    abr, abi = abr3.reshape(1, NSTATE), abi3.reshape(1, NSTATE)
    pwr, pwi, _, _ = _s5_tables(abr, abi, NT, "s5_tables_step")
    par, pai, pbr, pbi = _s5_tables(pwr[NT - 1:NT], pwi[NT - 1:NT], CH, "s5_tables_chunk")
    inv = (ROPE_BASE ** (-jnp.arange(DK // 2, dtype=F32) / (DK // 2))).reshape(1, DK // 2)
    cos, sin = _rope_tables(L, inv)
    bsp3 = bsp.reshape(SG, CH, 1)

    def dep(token):
        return NO_DEPS if token is None else (token,)

    win_e = first_weight((cos, pbi, cm))
    p, u3 = _in_proj(x, norm_even, win_e, "in_even", dep(start_token), lane_blocks=True)
    ypre, s5_states = _s5_scan_fwd(u3, bb, cm, pwr, pwi, pwr.T, pwi.T, par, pai, s5_d)
    yb, ret_states = _ret_fwd(p, cos, sin, ret_gain)
    wglu, wout_e, norm_odd, win_o, sgu_gain, wout_o = late_weights((ypre, yb))
    x1 = _out_even(x, ypre, p, yb, wglu, bglu, wout_e)
    p2 = _in_proj(x1, norm_odd, win_o, "in_odd")
    dp2, y_o, dx2, g_sgu_gain, g_wsp, g_bsp, g_fnorm, loss = _sgu_fwd_bwd(
        p2, x1, sgu_gain, wsp, bsp3, wout_o, fnorm, tgt)

    g_wout_o = _wgrad_rows([y_o], dx2, "wgrad_out_odd")
    g_win_o = _wgrad_cols(x1, norm_odd, dp2, "wgrad_in_odd")
    tok = emit("odd", dict(w_in_odd=g_win_o, w_out_odd=g_wout_o))
    dx1, g_norm_odd = _in_proj_bwd_x(dp2, x1, norm_odd, win_o, dx2, "in_odd_bwd", dep(tok))
    tok = emit("small_odd", dict(norm_odd=g_norm_odd, sgu_norm_gain=g_sgu_gain, sgu_w_spatial=g_wsp,
                                 sgu_b_spatial=g_bsp.reshape(SG, CH), final_norm=g_fnorm))

    dypre, daz, yg, dt, ya2, g_bglu = _s5_gate_bwd(ypre, p, dx1, wout_e, wglu, bglu, dep(tok))
    g_wglu = _wgrad_rows([yg], dt, "wgrad_glu")
    tok = emit("glu", dict(s5_w_glu=g_wglu))
    du, g_d, g_cm, g_bb, g_ar, g_ai = _s5_scan_bwd(u3, dypre, s5_states, bb, cm, jnp.swapaxes(cm, 1, 2), abr, abi,
                                                   pwr, pwi, par, pai, pbr, pbi, s5_d, dep(tok))
    dbbr3 = _diag_blocks(g_bb[:, :, :SW], HG, P)
    dbbi3 = _diag_blocks(g_bb[:, :, SW:], HG, P)
    g_c_re = jnp.swapaxes(_diag_blocks(g_cm[:, :SW, :], P, HG), 1, 2)
    g_c_im = -jnp.swapaxes(_diag_blocks(g_cm[:, SW:, :], P, HG), 1, 2)
    g_lr3, g_li3, g_dt3, g_br3, g_bi3 = _s5_disc_bwd(
        lr3, li3, dt3, br3, bi3, g_ar.reshape(G, 1, P), g_ai.reshape(G, 1, P), dbbr3, dbbi3)
    tok = emit("small_s5", dict(
        s5_lam_re=g_lr3.reshape(G, P), s5_lam_im=g_li3.reshape(G, P), s5_log_dt=g_dt3.reshape(1, G),
        s5_b_re=g_br3, s5_b_im=g_bi3, s5_c_re=g_c_re, s5_c_im=g_c_im, s5_d=g_d, s5_b_glu=g_bglu))
    dp, yb2, g_ret_gain = _ret_bwd(p, cos, sin, ret_gain, ret_states, dx1, wout_e, du, daz, dep(tok))
    g_win_e = _wgrad_cols(x, norm_even, dp, "wgrad_in_even")
    tok = emit("even_cols", dict(w_in_even=g_win_e))
    g_wout_e = _wgrad_rows([ya2, yb2], dx1, "wgrad_out_even", dep(tok))
    tok = emit("even_rows", dict(w_out_even=g_wout_e))
    dx, g_norm_even = _in_proj_bwd_x(dp, x, norm_even, win_e, dx1, "in_even_bwd", dep(tok))
    emit("last", dict(ret_gn_gain=g_ret_gain, norm_even=g_norm_even, loss=loss))
    return dx


WEIGHTS = ['norm_even', 'w_in_even', 's5_lam_re', 's5_lam_im', 's5_log_dt', 's5_b_re', 's5_b_im', 's5_c_re',
           's5_c_im', 's5_d', 's5_w_glu', 's5_b_glu', 'ret_gn_gain', 'w_out_even', 'norm_odd', 'w_in_odd',
           'sgu_norm_gain', 'sgu_w_spatial', 'sgu_b_spatial', 'w_out_odd', 'final_norm']
BIG = ['w_in_even', 's5_w_glu', 'w_out_even', 'w_in_odd', 'w_out_odd']
SHARDED_SMALL = {'norm_odd': D // NDEV, 'sgu_norm_gain': DI // NDEV}
SCATTER_STAGES = ("odd", "glu", "even_cols", "even_rows")
GATHER_STAGES = ("small_odd", "small_s5")


def _view(n, a):
    if n in ('s5_b_re', 's5_b_im'):
        return jnp.swapaxes(a[0], 1, 2)
    if n == 'final_norm':
        return a.reshape(1, D)
    return a[0] if a.ndim >= 3 else a


def _unview(n, t, shape):
    if n in ('s5_b_re', 's5_b_im'):
        return jnp.swapaxes(t, 1, 2)[None]
    return t.reshape(shape)


def kernel(x, norm_even, w_in_even, s5_lam_re, s5_lam_im, s5_log_dt, s5_b_re, s5_b_im, s5_c_re, s5_c_im, s5_d, s5_w_glu, s5_b_glu, ret_gn_gain, w_out_even, norm_odd, w_in_odd, sgu_norm_gain, sgu_w_spatial, sgu_b_spatial, w_out_odd, final_norm, loss_target, m_norm_even, m_w_in_even, m_s5_lam_re, m_s5_lam_im, m_s5_log_dt, m_s5_b_re, m_s5_b_im, m_s5_c_re, m_s5_c_im, m_s5_d, m_s5_w_glu, m_s5_b_glu, m_ret_gn_gain, m_w_out_even, m_norm_odd, m_w_in_odd, m_sgu_norm_gain, m_sgu_w_spatial, m_sgu_b_spatial, m_w_out_odd, m_final_norm, v_norm_even, v_w_in_even, v_s5_lam_re, v_s5_lam_im, v_s5_log_dt, v_s5_b_re, v_s5_b_im, v_s5_c_re, v_s5_c_im, v_s5_d, v_s5_w_glu, v_s5_b_glu, v_ret_gn_gain, v_w_out_even, v_norm_odd, v_w_in_odd, v_sgu_norm_gain, v_sgu_w_spatial, v_sgu_b_spatial, v_w_out_odd, v_final_norm):
    args = dict(locals())
    w = {n: args[n] for n in WEIGHTS}
    m = {n: args["m_" + n] for n in WEIGHTS}
    v = {n: args["v_" + n] for n in WEIGHTS}
    me = _my_index()

    first_handle, _ = _gather2_start(w['w_in_even'][0].astype(MXU), "gather_first_start")

    def first_weight(after):
        return _gather2_wait(_gather2_forward(first_handle, after, "gather_first_forward"), "gather_first_wait")

    late_own = [w['s5_w_glu'][0].astype(MXU), w['w_out_even'][0].astype(MXU), w['norm_odd'],
                w['w_in_odd'][0].astype(MXU), w['sgu_norm_gain'], w['w_out_odd'][0].astype(MXU)]
    late_handle, start_token = _exchange_start(late_own, False, "gather_late_start")

    def late_weights(after):
        _, (wglu, wout_e, nodd, win_o, sgug, wout_o) = _exchange_wait(late_handle, after, "gather_late_wait")
        return (wglu.reshape(D, D), wout_e.reshape(DI, D), nodd.reshape(1, D), win_o, sgug.reshape(1, DI),
                wout_o.reshape(DI, D))

    pending = {}
    small_last = {}

    def emit(stage, grads):
        if stage == "last":
            small_last.update(grads)
            return None
        names = list(grads)
        handle, token = _exchange_start([grads[n] for n in names], stage in SCATTER_STAGES, stage + "_start")
        pending[stage] = (handle, names)
        return token

    dx = _local_step(
        x[0], loss_target[0], w['norm_even'], first_weight, w['s5_lam_re'][0], w['s5_lam_im'][0], w['s5_log_dt'][0],
        w['s5_b_re'][0], w['s5_b_im'][0], w['s5_c_re'][0], w['s5_c_im'][0], w['s5_d'], w['s5_b_glu'],
        w['ret_gn_gain'], w['sgu_w_spatial'][0], w['sgu_b_spatial'][0], w['final_norm'].reshape(1, D),
        late_weights, emit, start_token)

    out_g, out_d, out_m, out_v = {}, {}, {}, {}
    after = dx
    for stage in SCATTER_STAGES:
        handle, names = pending[stage]
        sent, lands = _exchange_wait(handle, (after,), stage + "_wait")
        for n, land, stack in zip(names, lands, sent):
            shp = w[n].shape
            r, c = shp[1], shp[2]
            own = lax.dynamic_index_in_dim(stack, me, 0, keepdims=False)
            res = _adamw(w[n].reshape(r, c), m[n].reshape(r, c), v[n].reshape(r, c), land, own, "adamw_" + n)
            out_g[n], out_d[n], out_m[n], out_v[n] = (t.reshape(shp) for t in res)
            after = res[0]

    names, owns, lands = [], [], []
    for stage in GATHER_STAGES:
        handle, stage_names = pending[stage]
        sent, got = _exchange_wait(handle, (after,), stage + "_wait")
        names, owns, lands = names + stage_names, owns + sent, lands + got
    last_names = list(small_last)
    last = _exchange([small_last[n] for n in last_names], False, "gather_last")
    loss_parts = last[last_names.index("loss")][:, 0, 0]
    for n, own, land in zip(last_names, (small_last[n] for n in last_names), last):
        if n != "loss":
            names, owns, lands = names + [n], owns + [own], lands + [land]
    for i, n in enumerate(names):
        if n in SHARDED_SMALL:
            width = SHARDED_SMALL[n]
            owns[i] = lax.dynamic_slice_in_dim(owns[i], me * width, width, axis=1)
            lands[i] = lax.dynamic_slice_in_dim(lands[i], me * width, width, axis=2)
    res = _adamw_many([_view(n, w[n]) for n in names], [_view(n, m[n]) for n in names],
                      [_view(n, v[n]) for n in names], lands, owns, "adamw_small")
    for dst, vals in zip((out_g, out_d, out_m, out_v), res):
        for n, t in zip(names, vals):
            dst[n] = _unview(n, t, w[n].shape)

    loss_total = jnp.sum(loss_parts)
    return (loss_total, dx[None], *[out_g[n] for n in WEIGHTS], *[out_d[n] for n in WEIGHTS],
            *[out_m[n] for n in WEIGHTS], *[out_v[n] for n in WEIGHTS])
```

```python
import math

import jax
import jax.numpy as jnp
from jax import lax
from jax.experimental import pallas as pl
from jax.experimental.pallas import tpu as pltpu

F32 = jnp.float32
MXU = jnp.bfloat16
AXES = ("x", "y", "c")
NDEV = 8
D = 1024
NIN = 6144
WIN_BLK = NIN // NDEV
DI = 2048
G, P, HG = 64, 64, 16
GB = 8
NJ = G // GB
SW = GB * P
UW = GB * HG
NSTATE = G * P
HEADS, DK = 4, 256
CH = 128
SG, SGD = 4, 512
EPS = 1e-6
ROPE_BASE = 10000.0
VMEM_CAP_V7X = 64 * 1024 * 1024
LOG_G = [math.log1p(-2.0 ** (-5.0 - h)) for h in range(HEADS)]
GELU_C = math.sqrt(2.0 / math.pi)

ADAM_LR, ADAM_B1, ADAM_B2, ADAM_EPS, ADAM_WD, ADAM_STEP = 0.001, 0.9, 0.999, 1e-08, 0.01, 10
BC1 = 1.0 - ADAM_B1 ** ADAM_STEP
BC2 = 1.0 - ADAM_B2 ** ADAM_STEP

SDS = jax.ShapeDtypeStruct
ARB2 = ("arbitrary", "arbitrary")


def _cp(vmem_mib, sem=None):
    kw = dict(vmem_limit_bytes=min(vmem_mib * 1024 * 1024, VMEM_CAP_V7X - 4 * 1024 * 1024))
    if sem is not None:
        kw["dimension_semantics"] = sem
    return pltpu.CompilerParams(**kw)


def _mm(a, b):
    return jnp.dot(a.astype(MXU), b.astype(MXU), preferred_element_type=F32)


def _mm_nt(a, b):
    return lax.dot_general(a.astype(MXU), b.astype(MXU), (((1,), (1,)), ((), ())), preferred_element_type=F32)


def _mm_tn(a, b):
    return lax.dot_general(a.astype(MXU), b.astype(MXU), (((0,), (0,)), ((), ())), preferred_element_type=F32)


def _gelu(x):
    return _gelu_and_grad(x)[0]


def _gelu_and_grad(x):
    x2 = x * x
    th = jnp.tanh(GELU_C * x * (1.0 + 0.044715 * x2))
    hp = 0.5 * (1.0 + th)
    return x * hp, hp + 0.5 * x * (1.0 - th * th) * GELU_C * (1.0 + 3.0 * 0.044715 * x2)


def _silu_and_grad(x):
    s = jax.nn.sigmoid(x)
    return x * s, s * (1.0 + x * (1.0 - s))


def _full(shape):
    nd = len(shape)
    return pl.BlockSpec(shape, lambda *_: (0,) * nd)


def _rms(xf):
    r = lax.rsqrt(jnp.mean(xf * xf, axis=-1, keepdims=True) + EPS)
    return xf * r, r


ANY_SPEC = pl.BlockSpec(memory_space=pl.ANY)
NO_DEPS = ()


def _load_once(src_hbm, dst_vmem, sem):
    @pl.when(pl.program_id(0) == 0)
    def _():
        cp = pltpu.make_async_copy(src_hbm, dst_vmem, sem)
        cp.start()
        cp.wait()


def _lane_blocks(L):
    return SDS((NJ, L, UW), F32)


def _lane_block_spec(rows, index):
    return pl.BlockSpec((NJ, rows, UW), lambda i: (0, index(i), 0))


def _from_lane_blocks(ref):
    return jnp.concatenate([ref[j] for j in range(NJ)], axis=1)


def _to_lane_blocks(ref, v):
    for j in range(NJ):
        ref[j] = v[:, j * UW:(j + 1) * UW].astype(ref.dtype)


def _in_proj(x, gain, wst, name, deps=NO_DEPS, lane_blocks=False):
    L = x.shape[0]
    tm = min(512, L)

    def body(x_ref, g_ref, w_hbm, *rest):
        outs = rest[len(deps):]
        o_ref, w_scr, sem = outs[0], outs[-2], outs[-1]
        _load_once(w_hbm, w_scr, sem)
        xhat, _ = _rms(x_ref[...])
        h = (xhat * g_ref[...]).astype(MXU)
        for c in range(NDEV):
            o_ref[:, c * WIN_BLK:(c + 1) * WIN_BLK] = jnp.dot(h, w_scr[c], preferred_element_type=F32)
        if lane_blocks:
            _to_lane_blocks(outs[1], o_ref[:, 0:D])

    p_spec, p_shape = pl.BlockSpec((tm, NIN), lambda i: (i, 0)), SDS((L, NIN), F32)
    return pl.pallas_call(
        body, name=name, grid=(L // tm,),
        in_specs=[pl.BlockSpec((tm, D), lambda i: (i, 0)), _full((1, D)), ANY_SPEC] + [ANY_SPEC] * len(deps),
        out_specs=[p_spec, _lane_block_spec(tm, lambda i: i)] if lane_blocks else p_spec,
        out_shape=[p_shape, _lane_blocks(L)] if lane_blocks else p_shape,
        scratch_shapes=[pltpu.VMEM((NDEV, D, WIN_BLK), MXU), pltpu.SemaphoreType.DMA(())],
        compiler_params=_cp(58, ("arbitrary",)),
    )(x, gain, wst, *deps)


def _in_proj_bwd_x(dp, x, gain, wst, dres, name, deps=NO_DEPS):
    L = x.shape[0]
    tm = min(512, L)

    def body(dp_ref, x_ref, g_ref, w_hbm, dres_ref, *rest):
        dx_ref, gg_ref, w_scr, sem = rest[len(deps):]
        _load_once(w_hbm, w_scr, sem)

        @pl.when(pl.program_id(0) == 0)
        def _():
            gg_ref[...] = jnp.zeros_like(gg_ref)

        dh = _mm_nt(dp_ref[:, 0:WIN_BLK], w_scr[0])
        for c in range(1, NDEV):
            dh += _mm_nt(dp_ref[:, c * WIN_BLK:(c + 1) * WIN_BLK], w_scr[c])
        xhat, r = _rms(x_ref[...])
        dxhat = dh * g_ref[...]
        dx_ref[...] = dres_ref[...] + r * (dxhat - xhat * jnp.mean(dxhat * xhat, axis=-1, keepdims=True))
        gg_ref[...] += jnp.sum(dh * xhat, axis=0, keepdims=True)

    row = pl.BlockSpec((tm, D), lambda i: (i, 0))
    return pl.pallas_call(
        body, name=name, grid=(L // tm,),
        in_specs=[pl.BlockSpec((tm, NIN), lambda i: (i, 0)), row, _full((1, D)), ANY_SPEC, row]
        + [ANY_SPEC] * len(deps),
        out_specs=[row, _full((1, D))],
        out_shape=[SDS((L, D), F32), SDS((1, D), F32)],
        scratch_shapes=[pltpu.VMEM((NDEV, D, WIN_BLK), MXU), pltpu.SemaphoreType.DMA(())],
        compiler_params=_cp(56, ("arbitrary",)),
    )(dp, x, gain, wst, dres, *deps)


def _wgrad_cols(x, gain, dp, name, deps=NO_DEPS):
    L = x.shape[0]
    tk = min(512, L)
    nk = L // tk
    halves = 2
    nh = NDEV // halves

    def body(x_ref, g_ref, dp_ref, *rest):
        o_ref, acc = rest[len(deps):]
        k = pl.program_id(1)

        @pl.when(k == 0)
        def _():
            acc[...] = jnp.zeros_like(acc)

        xhat, _ = _rms(x_ref[...])
        acc[...] += _mm_tn(xhat * g_ref[...], dp_ref[...])

        @pl.when(k == nk - 1)
        def _():
            for c in range(nh):
                o_ref[c] = acc[:, c * WIN_BLK:(c + 1) * WIN_BLK].astype(o_ref.dtype)

    return pl.pallas_call(
        body, name=name, grid=(halves, nk),
        in_specs=[pl.BlockSpec((tk, D), lambda n, k: (k, 0)), _full((1, D)),
                  pl.BlockSpec((tk, nh * WIN_BLK), lambda n, k: (k, n))] + [ANY_SPEC] * len(deps),
        out_specs=pl.BlockSpec((nh, D, WIN_BLK), lambda n, k: (n, 0, 0)),
        out_shape=SDS((NDEV, D, WIN_BLK), MXU),
        scratch_shapes=[pltpu.VMEM((D, nh * WIN_BLK), F32)],
        compiler_params=_cp(56, ARB2),
    )(x, gain, dp, *deps)


def _wgrad_rows(a_parts, b, name, deps=NO_DEPS):
    L, N = b.shape
    na = len(a_parts)
    widths = [a.shape[1] for a in a_parts]
    M = sum(widths)
    tk = min(1024, L)
    nk = L // tk

    def body(*refs):
        a_refs, b_ref = refs[:na], refs[na]
        o_ref, acc = refs[na + 1 + len(deps):]
        k = pl.program_id(0)

        @pl.when(k == 0)
        def _():
            acc[...] = jnp.zeros_like(acc)

        bv = b_ref[...].astype(MXU)
        off = 0
        for a_ref, wd in zip(a_refs, widths):
            acc[off:off + wd, :] += _mm_tn(a_ref[...], bv)
            off += wd

        @pl.when(k == nk - 1)
        def _():
            o_ref[...] = acc[...].astype(o_ref.dtype).reshape(o_ref.shape)

    return pl.pallas_call(
        body, name=name, grid=(nk,),
        in_specs=[pl.BlockSpec((tk, wd), lambda k: (k, 0)) for wd in widths]
        + [pl.BlockSpec((tk, N), lambda k: (k, 0))] + [ANY_SPEC] * len(deps),
        out_specs=_full((NDEV, M // NDEV, N)),
        out_shape=SDS((NDEV, M // NDEV, N), MXU),
        scratch_shapes=[pltpu.VMEM((M, N), F32)],
        compiler_params=_cp(48, ("arbitrary",)),
    )(*a_parts, b, *deps)


def _s5_disc_fn(lr_raw, li, logdt, br, bi):
    lr = jnp.minimum(lr_raw, -1e-4)
    dt = jnp.exp(logdt)
    mag = jnp.exp(lr * dt)
    abr = mag * jnp.cos(li * dt)
    abi = mag * jnp.sin(li * dt)
    den = lr * lr + li * li
    nre = abr - 1.0
    nim = abi
    zr = (nre * lr + nim * li) / den
    zi = (nim * lr - nre * li) / den
    return abr, abi, zr * br - zi * bi, zr * bi + zi * br


def _s5_disc(lr, li, logdt, br, bi):
    def body(lr_ref, li_ref, dt_ref, br_ref, bi_ref, abr_ref, abi_ref, bbr_ref, bbi_ref):
        abr, abi, bbr, bbi = _s5_disc_fn(lr_ref[...], li_ref[...], dt_ref[...], br_ref[...], bi_ref[...])
        abr_ref[...] = abr
        abi_ref[...] = abi
        bbr_ref[...] = bbr
        bbi_ref[...] = bbi

    s1, s3 = SDS((G, 1, P), F32), SDS((G, HG, P), F32)
    return pl.pallas_call(body, name="s5_disc", out_shape=[s1, s1, s3, s3])(lr, li, logdt, br, bi)


def _s5_disc_bwd(lr, li, logdt, br, bi, dabr, dabi, dbbr, dbbi):
    def body(lr_ref, li_ref, dt_ref, br_ref, bi_ref, c0, c1, c2, c3, o0, o1, o2, o3, o4):
        _, vjp = jax.vjp(_s5_disc_fn, lr_ref[...], li_ref[...], dt_ref[...], br_ref[...], bi_ref[...])
        g = vjp((c0[...], c1[...], c2[...], c3[...]))
        for o, v in zip((o0, o1, o2, o3, o4), g):
            o[...] = v

    s1, s3 = SDS((G, 1, P), F32), SDS((G, HG, P), F32)
    return pl.pallas_call(body, name="s5_disc_bwd", out_shape=[s1, s1, SDS((G, 1, 1), F32), s3, s3])(
        lr, li, logdt, br, bi, dabr, dabi, dbbr, dbbi)


def _s5_tables(abr, abi, rows, name):
    def body(ar_ref, ai_ref, pfr, pfi, pbr, pbi):
        pfr[0:1, :] = ar_ref[...]
        pfi[0:1, :] = ai_ref[...]
        pbr[rows - 1:rows, :] = ar_ref[...]
        pbi[rows - 1:rows, :] = ai_ref[...]
        n = 1
        while n < rows:
            er, ei = pfr[n - 1:n, :], pfi[n - 1:n, :]
            xr, xi = pfr[0:n, :], pfi[0:n, :]
            pfr[n:2 * n, :] = er * xr - ei * xi
            pfi[n:2 * n, :] = er * xi + ei * xr
            yr, yi = pbr[rows - n:rows, :], pbi[rows - n:rows, :]
            pbr[rows - 2 * n:rows - n, :] = er * yr - ei * yi
            pbi[rows - 2 * n:rows - n, :] = er * yi + ei * yr
            n *= 2

    s = SDS((rows, NSTATE), F32)
    return pl.pallas_call(body, name=name, out_shape=[s, s, s, s], compiler_params=_cp(40))(abr, abi)


def _cscan(br, bi, pr_ref, pi_ref, reverse):
    T = br.shape[0]
    sign = -1.0 if reverse else 1.0
    row = lax.broadcasted_iota(jnp.int32, br.shape, 0)
    k = 1
    while k < T:
        akr = pr_ref[k - 1:k, :]
        aki = sign * pi_ref[k - 1:k, :]

        def shift(v):
            if k % 8 == 0:
                z = jnp.zeros((k, v.shape[1]), v.dtype)
                return jnp.concatenate([v[k:], z], 0) if reverse else jnp.concatenate([z, v[:T - k]], 0)
            if reverse:
                return jnp.where(row < T - k, pltpu.roll(v, T - k, 0), 0.0)
            return jnp.where(row >= k, pltpu.roll(v, k, 0), 0.0)

        sr, si = shift(br), shift(bi)
        br, bi = br + akr * sr - aki * si, bi + akr * si + aki * sr
        k *= 2
    return br, bi


def _embed(t):
    a, b = t.shape[1], t.shape[2]
    return jnp.einsum("jgab,gh->jgahb", t.reshape(NJ, GB, a, b), jnp.eye(GB, dtype=t.dtype)).reshape(NJ, GB * a, GB * b)


def _diag_blocks(t, a, b):
    return jnp.einsum("jgahb,gh->jgab", t.reshape(NJ, GB, a, GB, b), jnp.eye(GB, dtype=t.dtype)).reshape(G, a, b)


NT = 16


def _chunks(L):
    ncb = min(CH, L // NT)
    return ncb, NT * ncb


def _cmul_add(ar, ai, xr, xi, br, bi):
    return ar * xr - ai * xi + br, ar * xi + ai * xr + bi


def _pow_weights(w_ref, pwr_ref, pwi_ref, dst, adjoint):
    w = w_ref[...].astype(F32)
    wr, wi = w[:, :SW], w[:, SW:]
    for t in range(NT):
        k = t if adjoint else NT - 1 - t
        if k == 0:
            blk = w
        else:
            pr, pi = pwr_ref[k - 1:k, :], pwi_ref[k - 1:k, :]
            if adjoint:
                blk = jnp.concatenate([pr * wr + pi * wi, pr * wi - pi * wr], axis=1)
            else:
                blk = jnp.concatenate([pr * wr - pi * wi, pr * wi + pi * wr], axis=1)
        dst[t * UW:(t + 1) * UW, :] = blk.astype(dst.dtype)


def _s5_states(u_ref, bb_ref, bbp_scr, par_ref, pai_ref, cr, ci, ncb, bu_scr):
    us = [u_ref[pl.ds(t, ncb, stride=NT), :] for t in range(NT)]
    for t in range(NT):
        bu_scr[t] = _mm(us[t], bb_ref[...])
    e = _mm(jnp.concatenate(us, axis=1), bbp_scr[...])
    xr, xi = _cscan(e[:, :SW], e[:, SW:], par_ref, pai_ref, False)
    fr, fi = _cmul_add(par_ref[0:ncb, :], pai_ref[0:ncb, :], cr, ci, xr, xi)
    row = lax.broadcasted_iota(jnp.int32, fr.shape, 0)
    cinr = jnp.where(row >= 1, pltpu.roll(fr, 1, 0), cr)
    cini = jnp.where(row >= 1, pltpu.roll(fi, 1, 0), ci)
    return cinr, cini, jnp.concatenate([fr[ncb - 1:ncb, :], fi[ncb - 1:ncb, :]], axis=1)


def _slices(ref, ncb, axis):
    return jnp.concatenate([ref[pl.ds(t, ncb, stride=NT), :] for t in range(NT)], axis=axis)


def _s5_scan_fwd(u3, bb, cm, pwr, pwi, pwrt, pwit, par, pai, dskip):
    L = u3.shape[1]
    ncb, tb = _chunks(L)
    nb = L // tb

    def build(bb_ref, cm_ref, pwr_ref, pwi_ref, pct_ref, pit_ref, bbp_scr, ktp_scr, zc_scr):
        w = bb_ref[...].astype(F32)
        wr, wi = w[:, :SW], w[:, SW:]
        cmv = cm_ref[...].astype(F32)
        ct, cb = cmv[:SW, :], cmv[SW:, :]
        zero = jnp.zeros((UW, UW), MXU)
        for tau in range(NT):
            if tau == 0:
                blk = w
            else:
                pr, pi = pwr_ref[tau - 1:tau, :], pwi_ref[tau - 1:tau, :]
                blk = jnp.concatenate([pr * wr - pi * wi, pr * wi + pi * wr], axis=1)
            blk = blk.astype(MXU)
            bbp_scr[(NT - 1 - tau) * UW:(NT - tau) * UW, :] = blk
            k = jnp.dot(blk, cm_ref[...], preferred_element_type=F32).astype(MXU)
            for j in range(NT - tau):
                ktp_scr[j * UW:(j + 1) * UW, (j + tau) * UW:(j + tau + 1) * UW] = k
            pc, pic = pct_ref[:, tau:tau + 1], pit_ref[:, tau:tau + 1]
            zc_scr[:, tau * UW:(tau + 1) * UW] = jnp.concatenate(
                [pc * ct + pic * cb, pc * cb - pic * ct], axis=0).astype(MXU)
        for j in range(NT):
            for t in range(j):
                ktp_scr[j * UW:(j + 1) * UW, t * UW:(t + 1) * UW] = zero

    def body(u_ref, bb_ref, cm_ref, pwr_ref, pwi_ref, pct_ref, pit_ref, par_ref, pai_ref, d_ref, ypre_ref, st_ref,
             carry, cin_scr, bbp_scr, ktp_scr, zc_scr):
        @pl.when(pl.program_id(1) == 0)
        def _():
            carry[...] = jnp.zeros_like(carry)
            build(bb_ref, cm_ref, pwr_ref, pwi_ref, pct_ref, pit_ref, bbp_scr, ktp_scr, zc_scr)

        c = carry[...]
        st_ref[...] = c
        ucat = _slices(u_ref, ncb, 1)
        ub = ucat.astype(MXU)
        e = jnp.dot(ub, bbp_scr[...], preferred_element_type=F32)
        xr, xi = _cscan(e[:, :SW], e[:, SW:], par_ref, pai_ref, False)
        cr, ci = c[:, :SW], c[:, SW:]
        fr, fi = _cmul_add(par_ref[0:ncb, :], pai_ref[0:ncb, :], cr, ci, xr, xi)
        carry[...] = jnp.concatenate([fr[ncb - 1:ncb, :], fi[ncb - 1:ncb, :]], axis=1)
        row = lax.broadcasted_iota(jnp.int32, fr.shape, 0)
        cin_scr[:, :SW] = jnp.where(row >= 1, pltpu.roll(fr, 1, 0), cr).astype(MXU)
        cin_scr[:, SW:] = jnp.where(row >= 1, pltpu.roll(fi, 1, 0), ci).astype(MXU)
        y = (jnp.dot(ub, ktp_scr[...], preferred_element_type=F32)
             + jnp.dot(cin_scr[...], zc_scr[...], preferred_element_type=F32)
             + jnp.tile(d_ref[...], (1, NT)) * ucat)
        for t in range(NT):
            ypre_ref[pl.ds(t, ncb, stride=NT), :] = y[:, t * UW:(t + 1) * UW]

    tab = pl.BlockSpec((CH, SW), lambda j, i: (0, j))
    stp = pl.BlockSpec((NT, SW), lambda j, i: (0, j))
    stt = pl.BlockSpec((SW, NT), lambda j, i: (j, 0))
    vec = lambda w: pl.BlockSpec((1, w), lambda j, i: (0, j))
    return pl.pallas_call(
        body, name="s5_scan_fwd", grid=(NJ, nb),
        in_specs=[pl.BlockSpec((None, tb, UW), lambda j, i: (j, i, 0)),
                  pl.BlockSpec((None, UW, 2 * SW), lambda j, i: (j, 0, 0)),
                  pl.BlockSpec((None, 2 * SW, UW), lambda j, i: (j, 0, 0)),
                  stp, stp, stt, stt, tab, tab, vec(UW)],
        out_specs=[pl.BlockSpec((None, tb, UW), lambda j, i: (j, i, 0)),
                   pl.BlockSpec((None, None, 1, 2 * SW), lambda j, i: (j, i, 0, 0))],
        out_shape=[_lane_blocks(L), SDS((NJ, nb, 1, 2 * SW), F32)],
        scratch_shapes=[pltpu.VMEM((1, 2 * SW), F32), pltpu.VMEM((ncb, 2 * SW), MXU),
                        pltpu.VMEM((NT * UW, 2 * SW), MXU), pltpu.VMEM((NT * UW, NT * UW), MXU),
                        pltpu.VMEM((2 * SW, NT * UW), MXU)],
        compiler_params=_cp(56, ARB2),
    )(u3, bb, cm, pwr, pwi, pwrt, pwit, par, pai, dskip)


def _s5_gate_bwd(ypre3, p, dx1, wout_e, wglu, bglu, deps=NO_DEPS):
    L = p.shape[0]
    tm = min(256, L)

    def body(y_ref, az_ref, dx1_ref, wo_ref, wg_ref, bg_ref, *rest):
        dyp_ref, daz_ref, yg_ref, dt_ref, ya_ref, gbg_ref = rest[len(deps):]

        @pl.when(pl.program_id(0) == 0)
        def _():
            gbg_ref[...] = jnp.zeros_like(gbg_ref)

        yg, dgelu = _gelu_and_grad(_from_lane_blocks(y_ref))
        sg = jax.nn.sigmoid(_mm(yg, wg_ref[...]) + bg_ref[...])
        act, dact = _silu_and_grad(az_ref[...])
        y2 = yg * sg
        dya = _mm_nt(dx1_ref[...], wo_ref[...])
        daz_ref[...] = (dya * y2 * dact).astype(daz_ref.dtype)
        dy2 = dya * act
        dt = dy2 * yg * sg * (1.0 - sg)
        dyg = dy2 * sg + _mm_nt(dt, wg_ref[...])
        _to_lane_blocks(dyp_ref, dyg * dgelu)
        yg_ref[...] = yg.astype(yg_ref.dtype)
        dt_ref[...] = dt.astype(dt_ref.dtype)
        ya_ref[...] = (y2 * act).astype(ya_ref.dtype)
        gbg_ref[...] += jnp.sum(dt, axis=0, keepdims=True)

    row = pl.BlockSpec((tm, D), lambda i: (i, 0))
    return pl.pallas_call(
        body, name="s5_gate_bwd", grid=(L // tm,),
        in_specs=[_lane_block_spec(tm, lambda i: i), pl.BlockSpec((tm, D), lambda i: (i, 1)), row,
                  pl.BlockSpec((D, D), lambda i: (0, 0)), _full((D, D)), _full((1, D))] + [ANY_SPEC] * len(deps),
        out_specs=[_lane_block_spec(tm, lambda i: i), row, row, row, row, _full((1, D))],
        out_shape=[_lane_blocks(L), SDS((L, D), MXU), SDS((L, D), MXU), SDS((L, D), MXU), SDS((L, D), MXU),
                   SDS((1, D), F32)],
        compiler_params=_cp(40, ("arbitrary",)),
    )(ypre3, p, dx1, wout_e, wglu, bglu, *deps)


def _s5_scan_bwd(u3, dypre3, states, bb, cm, cmt, abr, abi, pwr, pwi, par, pai, pbr, pbi, dskip, deps=NO_DEPS):
    L = u3.shape[1]
    ncb, tb = _chunks(L)
    nb = L // tb
    rev = lambda i: nb - 1 - i

    def body(u_ref, dy_ref, st_ref, bb_ref, cm_ref, cmt_ref, ar_ref, ai_ref, pwr_ref, pwi_ref, par_ref, pai_ref,
             pbr_ref, pbi_ref, d_ref, *rest):
        (du_ref, gd_ref, gcm_ref, gbb_ref, gar_ref, gai_ref,
         lcarry, bu_scr, s_scr, gs_scr, bbp_scr, cmp_scr) = rest[len(deps):]
        del cm_ref

        @pl.when(pl.program_id(1) == 0)
        def _():
            _pow_weights(bb_ref, pwr_ref, pwi_ref, bbp_scr, False)
            _pow_weights(cmt_ref, pwr_ref, pwi_ref, cmp_scr, True)
            lcarry[...] = jnp.zeros_like(lcarry)
            gd_ref[...] = jnp.zeros_like(gd_ref)
            gcm_ref[...] = jnp.zeros_like(gcm_ref)
            gbb_ref[...] = jnp.zeros_like(gbb_ref)
            gar_ref[...] = jnp.zeros_like(gar_ref)
            gai_ref[...] = jnp.zeros_like(gai_ref)

        ar, ai = ar_ref[...], ai_ref[...]
        c = st_ref[...]
        sr, si, _ = _s5_states(u_ref, bb_ref, bbp_scr, par_ref, pai_ref, c[:, :SW], c[:, SW:], ncb, bu_scr)
        s_scr[0] = jnp.concatenate([sr, si], axis=1)
        for t in range(NT):
            bu = bu_scr[t]
            sr, si = _cmul_add(ar, ai, sr, si, bu[:, :SW], bu[:, SW:])
            s_scr[t + 1] = jnp.concatenate([sr, si], axis=1)
        dys = [dy_ref[pl.ds(t, ncb, stride=NT), :] for t in range(NT)]
        for t in range(NT):
            gs_scr[t] = _mm(dys[t], cmt_ref[...])
        f = _mm(jnp.concatenate(dys, axis=1), cmp_scr[...])
        xr, xi = _cscan(f[:, :SW], f[:, SW:], par_ref, pai_ref, True)
        lc = lcarry[...]
        lcr, lci = lc[:, :SW], lc[:, SW:]
        hr, hi = _cmul_add(pbr_ref[CH - ncb:CH, :], -pbi_ref[CH - ncb:CH, :], lcr, lci, xr, xi)
        lcarry[...] = jnp.concatenate([hr[0:1, :], hi[0:1, :]], axis=1)
        row = lax.broadcasted_iota(jnp.int32, hr.shape, 0)
        lr_ = jnp.where(row < ncb - 1, pltpu.roll(hr, ncb - 1, 0), lcr)
        li_ = jnp.where(row < ncb - 1, pltpu.roll(hi, ncb - 1, 0), lci)
        gar = jnp.zeros((1, SW), F32)
        gai = jnp.zeros((1, SW), F32)
        for t in reversed(range(NT)):
            gs = gs_scr[t]
            lr_, li_ = _cmul_add(ar, -ai, lr_, li_, gs[:, :SW], gs[:, SW:])
            rows = pl.ds(t, ncb, stride=NT)
            u_t, dy_t = u_ref[rows, :], dy_ref[rows, :]
            lam = jnp.concatenate([lr_, li_], axis=1)
            gbb_ref[...] += _mm_tn(u_t, lam)
            du_ref[rows, :] = _mm_nt(lam, bb_ref[...]) + dy_t * d_ref[...]
            gd_ref[...] += jnp.sum(dy_t * u_t, axis=0, keepdims=True)
            gcm_ref[...] += _mm_tn(s_scr[t + 1], dy_t)
            sp = s_scr[t]
            spr, spi = sp[:, :SW], sp[:, SW:]
            gar += jnp.sum(lr_ * spr + li_ * spi, axis=0, keepdims=True)
            gai += jnp.sum(li_ * spr - lr_ * spi, axis=0, keepdims=True)
        gar_ref[...] += gar
        gai_ref[...] += gai

    tab = pl.BlockSpec((CH, SW), lambda j, i: (0, j))
    stp = pl.BlockSpec((NT, SW), lambda j, i: (0, j))
    colblk = pl.BlockSpec((None, tb, UW), lambda j, i: (j, rev(i), 0))
    vec = lambda w: pl.BlockSpec((1, w), lambda j, i: (0, j))
    return pl.pallas_call(
        body, name="s5_scan_bwd", grid=(NJ, nb),
        in_specs=[colblk, colblk,
                  pl.BlockSpec((None, None, 1, 2 * SW), lambda j, i: (j, rev(i), 0, 0)),
                  pl.BlockSpec((None, UW, 2 * SW), lambda j, i: (j, 0, 0)),
                  pl.BlockSpec((None, 2 * SW, UW), lambda j, i: (j, 0, 0)),
                  pl.BlockSpec((None, UW, 2 * SW), lambda j, i: (j, 0, 0)),
                  vec(SW), vec(SW), stp, stp, tab, tab, tab, tab, vec(UW)] + [ANY_SPEC] * len(deps),
        out_specs=[colblk, vec(UW),
                   pl.BlockSpec((None, 2 * SW, UW), lambda j, i: (j, 0, 0)),
                   pl.BlockSpec((None, UW, 2 * SW), lambda j, i: (j, 0, 0)),
                   vec(SW), vec(SW)],
        out_shape=[_lane_blocks(L), SDS((1, D), F32),
                   SDS((NJ, 2 * SW, UW), F32), SDS((NJ, UW, 2 * SW), F32),
                   SDS((1, NSTATE), F32), SDS((1, NSTATE), F32)],
        scratch_shapes=[pltpu.VMEM((1, 2 * SW), F32), pltpu.VMEM((NT, ncb, 2 * SW), F32),
                        pltpu.VMEM((NT + 1, ncb, 2 * SW), F32), pltpu.VMEM((NT, ncb, 2 * SW), F32),
                        pltpu.VMEM((NT * UW, 2 * SW), MXU), pltpu.VMEM((NT * UW, 2 * SW), MXU)],
        compiler_params=_cp(56, ARB2),
    )(u3, dypre3, states, bb, cm, cmt, abr, abi, pwr, pwi, par, pai, pbr, pbi, dskip, *deps)


def _rope_tables(L, inv):
    tm = min(512, L)

    def body(inv_ref, cos_ref, sin_ref):
        pos = (lax.broadcasted_iota(jnp.int32, (tm, DK // 2), 0) + pl.program_id(0) * tm).astype(F32)
        ang = pos * inv_ref[...]
        cos_ref[...] = jnp.cos(ang)
        sin_ref[...] = jnp.sin(ang)

    blk = pl.BlockSpec((tm, DK // 2), lambda i: (i, 0))
    return pl.pallas_call(body, name="rope_tables", grid=(L // tm,), in_specs=[_full((1, DK // 2))],
                          out_specs=[blk, blk], out_shape=[SDS((L, DK // 2), F32)] * 2)(inv)


def _rot(x, cos, sin):
    x1, x2 = x[:, :DK // 2], x[:, DK // 2:]
    return jnp.concatenate([x1 * cos - x2 * sin, x1 * sin + x2 * cos], axis=1)


def _unrot(d, cos, sin):
    d1, d2 = d[:, :DK // 2], d[:, DK // 2:]
    return jnp.concatenate([d1 * cos + d2 * sin, d2 * cos - d1 * sin], axis=1)


def _ret_decays(h):
    lg = LOG_G[h]
    n = lax.broadcasted_iota(jnp.int32, (CH, CH), 0)
    m = lax.broadcasted_iota(jnp.int32, (CH, CH), 1)
    diff = (n - m).astype(F32)
    decay = jnp.where(n >= m, jnp.exp(lg * jnp.maximum(diff, 0.0)), 0.0)
    idx = lax.broadcasted_iota(jnp.int32, (CH, 1), 0).astype(F32)
    xi = jnp.exp(lg * (idx + 1.0))
    zeta = jnp.exp(lg * (CH - 1.0 - idx))
    return decay, xi, zeta, math.exp(lg * CH)


def _ret_tables(dec_scr, vec_scr):
    for h in range(HEADS):
        decay, xi, zeta, _ = _ret_decays(h)
        dec_scr[h] = decay
        vec_scr[h] = jnp.concatenate([jnp.broadcast_to(xi, (CH, 128)), jnp.broadcast_to(zeta, (CH, 128))], axis=1)


def _ret_chunk_fwd(q, k, v, cos, sin, s_prev_b, decay, xi, zeta):
    qr = _rot(q, cos, sin)
    kr = _rot(k, cos, sin) * (DK ** -0.5)
    scores = _mm_nt(qr, kr) * decay
    o = _mm(scores, v) + _mm(qr * xi, s_prev_b)
    local = _mm_tn(kr * zeta, v)
    mu = jnp.mean(o, axis=-1, keepdims=True)
    oc = o - mu
    rstd = lax.rsqrt(jnp.mean(oc * oc, axis=-1, keepdims=True) + EPS)
    return qr, kr, scores, local, oc * rstd, rstd


def _ret_fwd(p, cos, sin, gain):
    L = p.shape[0]
    nb = L // CH

    def body(q_ref, k_ref, v_ref, bz_ref, cos_ref, sin_ref, g_ref, yb_ref, st_ref, state, dec_scr, vec_scr):
        @pl.when(pl.program_id(0) == 0)
        def _():
            state[...] = jnp.zeros_like(state)
            _ret_tables(dec_scr, vec_scr)

        cos, sin = cos_ref[...], sin_ref[...]
        act, _ = _silu_and_grad(bz_ref[...])
        for h in range(HEADS):
            hs = slice(h * DK, (h + 1) * DK)
            xi, zeta = vec_scr[h, :, 0:1], vec_scr[h, :, 128:129]
            s_prev = state[h]
            s_prev_b = s_prev.astype(MXU)
            st_ref[h] = s_prev_b
            _, _, _, local, on, _ = _ret_chunk_fwd(q_ref[:, hs], k_ref[:, hs], v_ref[:, hs], cos, sin,
                                                   s_prev_b, dec_scr[h], xi, zeta)
            state[h] = s_prev * math.exp(LOG_G[h] * CH) + local
            yb_ref[:, hs] = (on * g_ref[:, hs] * act[:, hs]).astype(yb_ref.dtype)

    col = lambda c: pl.BlockSpec((CH, D), lambda i: (i, c))
    rope = pl.BlockSpec((CH, DK // 2), lambda i: (i, 0))
    return pl.pallas_call(
        body, name="ret_fwd", grid=(nb,),
        in_specs=[col(2), col(3), col(4), col(5), rope, rope, _full((1, D))],
        out_specs=[pl.BlockSpec((CH, D), lambda i: (i, 0)),
                   pl.BlockSpec((None, HEADS, DK, DK), lambda i: (i, 0, 0, 0))],
        out_shape=[SDS((L, D), MXU), SDS((nb, HEADS, DK, DK), MXU)],
        scratch_shapes=[pltpu.VMEM((HEADS, DK, DK), F32), pltpu.VMEM((HEADS, CH, CH), F32),
                        pltpu.VMEM((HEADS, CH, 256), F32)],
        compiler_params=_cp(40, ("arbitrary",)),
    )(p, p, p, p, cos, sin, gain)


def _ret_bwd(p, cos, sin, gain, states, dx1, wout_e, du, daz, deps=NO_DEPS):
    L = p.shape[0]
    nb = L // CH
    rev = lambda i: nb - 1 - i

    def body(q_ref, k_ref, v_ref, bz_ref, cos_ref, sin_ref, g_ref, st_ref, dx1_ref, wo_ref, du_ref, daz_ref, *rest):
        dp_ref, yb_ref, gg_ref, gstate, dec_scr, vec_scr = rest[len(deps):]

        @pl.when(pl.program_id(0) == 0)
        def _():
            gstate[...] = jnp.zeros_like(gstate)
            gg_ref[...] = jnp.zeros_like(gg_ref)
            _ret_tables(dec_scr, vec_scr)

        cos, sin = cos_ref[...], sin_ref[...]
        act, dact = _silu_and_grad(bz_ref[...])
        dyb = _mm_nt(dx1_ref[...], wo_ref[...])
        dp_ref[:, 0:D] = _from_lane_blocks(du_ref).astype(dp_ref.dtype)
        dp_ref[:, D:2 * D] = daz_ref[...]
        for h in range(HEADS):
            hs = slice(h * DK, (h + 1) * DK)
            col = lambda part: slice((2 + part) * D + h * DK, (2 + part) * D + (h + 1) * DK)
            decay = dec_scr[h]
            xi, zeta = vec_scr[h, :, 0:1], vec_scr[h, :, 128:129]
            v = v_ref[:, hs]
            s_prev_b = st_ref[h]
            qr, kr, scores, _, on, rstd = _ret_chunk_fwd(q_ref[:, hs], k_ref[:, hs], v, cos, sin, s_prev_b,
                                                         decay, xi, zeta)
            gain_h = g_ref[:, hs]
            out = on * gain_h
            yb_ref[:, hs] = (out * act[:, hs]).astype(yb_ref.dtype)
            dyb_h = dyb[:, hs]
            dp_ref[:, col(3)] = (dyb_h * out * dact[:, hs]).astype(dp_ref.dtype)
            dout = dyb_h * act[:, hs]
            gg_ref[:, hs] += jnp.sum(dout * on, axis=0, keepdims=True)
            don = dout * gain_h
            do = rstd * (don - jnp.mean(don, axis=-1, keepdims=True)
                         - on * jnp.mean(don * on, axis=-1, keepdims=True))
            gnext = gstate[h]
            gnext_b = gnext.astype(MXU)
            dscores = _mm_nt(do, v) * decay
            dp_ref[:, col(2)] = (_mm_tn(scores, do) + _mm(kr * zeta, gnext_b)).astype(dp_ref.dtype)
            dqr = _mm(dscores, kr) + _mm_nt(do, s_prev_b) * xi
            dkr = _mm_tn(dscores, qr) + _mm_nt(v, gnext_b) * zeta
            gstate[h] = gnext * math.exp(LOG_G[h] * CH) + _mm_tn(qr * xi, do)
            dp_ref[:, col(0)] = _unrot(dqr, cos, sin).astype(dp_ref.dtype)
            dp_ref[:, col(1)] = (_unrot(dkr, cos, sin) * (DK ** -0.5)).astype(dp_ref.dtype)

    col = lambda c: pl.BlockSpec((CH, D), lambda i: (rev(i), c))
    rope = pl.BlockSpec((CH, DK // 2), lambda i: (rev(i), 0))
    outc = col(0)
    act_out = SDS((L, D), MXU)
    return pl.pallas_call(
        body, name="ret_bwd", grid=(nb,),
        in_specs=[col(2), col(3), col(4), col(5), rope, rope, _full((1, D)),
                  pl.BlockSpec((None, HEADS, DK, DK), lambda i: (rev(i), 0, 0, 0)),
                  outc, pl.BlockSpec((D, D), lambda i: (1, 0)), _lane_block_spec(CH, rev), outc]
        + [ANY_SPEC] * len(deps),
        out_specs=[pl.BlockSpec((CH, NIN), lambda i: (rev(i), 0)), outc, _full((1, D))],
        out_shape=[SDS((L, NIN), MXU), act_out, SDS((1, D), F32)],
        scratch_shapes=[pltpu.VMEM((HEADS, DK, DK), F32), pltpu.VMEM((HEADS, CH, CH), F32),
                        pltpu.VMEM((HEADS, CH, 256), F32)],
        compiler_params=_cp(48, ("arbitrary",)),
    )(p, p, p, p, cos, sin, gain, states, dx1, wout_e, du, daz, *deps)


def _out_even(x, ypre3, p, yb, wglu, bglu, wout):
    L = x.shape[0]
    tm = min(512, L)

    def body(x_ref, y_ref, az_ref, yb_ref, wg_ref, bg_ref, w_ref, o_ref):
        yg = _gelu(_from_lane_blocks(y_ref))
        t = _mm(yg, wg_ref[...]) + bg_ref[...]
        act, _ = _silu_and_grad(az_ref[...])
        ya = (yg * jax.nn.sigmoid(t) * act).astype(MXU)
        cat = jnp.concatenate([ya, yb_ref[...]], axis=1)
        o_ref[...] = x_ref[...] + jnp.dot(cat, w_ref[...], preferred_element_type=F32)

    row = pl.BlockSpec((tm, D), lambda i: (i, 0))
    return pl.pallas_call(
        body, name="out_even", grid=(L // tm,),
        in_specs=[row, _lane_block_spec(tm, lambda i: i), pl.BlockSpec((tm, D), lambda i: (i, 1)), row,
                  _full((D, D)), _full((1, D)), _full((DI, D))],
        out_specs=row, out_shape=SDS((L, D), F32), compiler_params=_cp(48, ("arbitrary",)),
    )(x, ypre3, p, yb, wglu, bglu, wout)


def _sgu_core(pv, gain, ws_ref, bs_ref):
    pu, pvv, z = pv[:, :DI], pv[:, DI:2 * DI], pv[:, 2 * DI:]
    u, gu = _gelu_and_grad(pu)
    v, gv = _gelu_and_grad(pvv)
    mu = jnp.mean(v, axis=-1, keepdims=True)
    vc = v - mu
    rstd = lax.rsqrt(jnp.mean(vc * vc, axis=-1, keepdims=True) + EPS)
    vhat = vc * rstd
    vn = vhat * gain
    t = lax.broadcasted_iota(jnp.int32, (CH, CH), 0)
    s_ = lax.broadcasted_iota(jnp.int32, (CH, CH), 1)
    mask = t >= s_
    wm = [jnp.where(mask, ws_ref[g], 0.0).astype(MXU) for g in range(SG)]
    s = jnp.concatenate([_mm(wm[g], vn[:, g * SGD:(g + 1) * SGD]) + bs_ref[g] for g in range(SG)], axis=1)
    return gu, gv, z, u, vhat, rstd, vn, mask, wm, s


def _sgu_fwd_bwd(p2, x1, gain, wsp, bsp, wout, fnorm, tgt):
    L = p2.shape[0]

    def body(p_ref, x1_ref, g_ref, ws_ref, bs_ref, wo_ref, fn_ref, t_ref,
             dp_ref, y_ref, dx2_ref, gg_ref, gws_ref, gbs_ref, gfn_ref, loss_ref):
        @pl.when(pl.program_id(0) == 0)
        def _():
            gg_ref[...] = jnp.zeros_like(gg_ref)
            gws_ref[...] = jnp.zeros_like(gws_ref)
            gbs_ref[...] = jnp.zeros_like(gbs_ref)
            gfn_ref[...] = jnp.zeros_like(gfn_ref)
            loss_ref[...] = jnp.zeros_like(loss_ref)

        gain = g_ref[...]
        gu, gv, z, u, vhat, rstd, vn, mask, wm, s = _sgu_core(p_ref[...], gain, ws_ref, bs_ref)
        act, dact = _silu_and_grad(z)
        y = (u * s * act).astype(MXU)
        y_ref[...] = y
        x2 = x1_ref[...] + jnp.dot(y, wo_ref[...], preferred_element_type=F32)
        xhat, r = _rms(x2)
        fn = fn_ref[...]
        e = xhat * fn - t_ref[...]
        loss_ref[...] += 0.5 * jnp.sum(jnp.mean(e * e, axis=-1, keepdims=True), axis=0, keepdims=True)
        do = e * (1.0 / D)
        gfn_ref[...] += jnp.sum(do * xhat, axis=0, keepdims=True)
        dxhat = do * fn
        dx2 = r * (dxhat - xhat * jnp.mean(dxhat * xhat, axis=-1, keepdims=True))
        dx2_ref[...] = dx2
        dy = _mm_nt(dx2, wo_ref[...])
        du = dy * s * act
        ds = dy * u * act
        dz = dy * u * s * dact
        dvn = []
        for g in range(SG):
            ds_g = ds[:, g * SGD:(g + 1) * SGD]
            vn_g = vn[:, g * SGD:(g + 1) * SGD]
            gbs_ref[g] += jnp.sum(ds_g, axis=1, keepdims=True)
            gws_ref[g] += jnp.where(mask, _mm_nt(ds_g, vn_g), 0.0)
            dvn.append(_mm_tn(wm[g], ds_g))
        dvn = jnp.concatenate(dvn, axis=1)
        gg_ref[...] += jnp.sum(dvn * vhat, axis=0, keepdims=True)
        dvhat = dvn * gain
        dv = rstd * (dvhat - jnp.mean(dvhat, axis=-1, keepdims=True)
                     - vhat * jnp.mean(dvhat * vhat, axis=-1, keepdims=True))
        dp_ref[...] = jnp.concatenate([du * gu, dv * gv, dz], axis=1).astype(dp_ref.dtype)

    row = pl.BlockSpec((CH, D), lambda i: (i, 0))
    wide = pl.BlockSpec((CH, NIN), lambda i: (i, 0))
    return pl.pallas_call(
        body, name="sgu_fwd_bwd", grid=(L // CH,),
        in_specs=[wide, row, _full((1, DI)), _full((SG, CH, CH)), _full((SG, CH, 1)), _full((DI, D)),
                  _full((1, D)), row],
        out_specs=[wide, pl.BlockSpec((CH, DI), lambda i: (i, 0)), row,
                   _full((1, DI)), _full((SG, CH, CH)), _full((SG, CH, 1)), _full((1, D)), _full((1, 128))],
        out_shape=[SDS((L, NIN), MXU), SDS((L, DI), MXU), SDS((L, D), F32), SDS((1, DI), F32),
                   SDS((SG, CH, CH), F32), SDS((SG, CH, 1), F32), SDS((1, D), F32), SDS((1, 128), F32)],
        compiler_params=_cp(48, ("arbitrary",)),
    )(p2, x1, gain, wsp, bsp, wout, fnorm, tgt)


def _my_index():
    return 4 * lax.axis_index("x") + 2 * lax.axis_index("y") + lax.axis_index("c")


def _ordered_sum(land_ref, own, me):
    g = None
    for s in range(NDEV):
        part = jnp.where(me == s, own, land_ref[s].astype(F32))
        g = part if g is None else g + part
    return g


def _adamw_math(w, m, v, g):
    mn = ADAM_B1 * m + (1.0 - ADAM_B1) * g
    vn = ADAM_B2 * v + (1.0 - ADAM_B2) * (g * g)
    mhat = mn / BC1
    vhat = vn / BC2
    return g, -ADAM_LR * (mhat / (jnp.sqrt(vhat) + ADAM_EPS) + ADAM_WD * w), mn, vn


def _adamw(w, m, v, land, own, name):
    R, C = w.shape
    tr = R
    for cand in (256, 128, 64, 32, 16, 8):
        if R % cand == 0 and R > cand:
            tr = cand
            break

    def body(w_ref, m_ref, v_ref, land_ref, own_ref, g_ref, d_ref, mo_ref, vo_ref):
        g = _ordered_sum(land_ref, own_ref[...].astype(F32), _my_index())
        for o, val in zip((g_ref, d_ref, mo_ref, vo_ref), _adamw_math(w_ref[...], m_ref[...], v_ref[...], g)):
            o[...] = val

    blk = pl.BlockSpec((tr, C), lambda i: (i, 0))
    out = SDS((R, C), F32)
    return pl.pallas_call(
        body, name=name, grid=(R // tr,),
        in_specs=[blk, blk, blk, pl.BlockSpec((NDEV, tr, C), lambda i: (0, i, 0)), blk],
        out_specs=[blk, blk, blk, blk], out_shape=[out, out, out, out],
        compiler_params=_cp(40, ("arbitrary",)),
    )(w, m, v, land, own)


def _adamw_many(ws, ms, vs, lands, owns, name):
    k = len(ws)

    def body(*refs):
        ins, outs = refs[:5 * k], refs[5 * k:]
        me = _my_index()
        for i in range(k):
            w_ref, m_ref, v_ref, land_ref, own_ref = (ins[j * k + i] for j in range(5))
            g = _ordered_sum(land_ref, own_ref[...], me)
            for j, val in enumerate(_adamw_math(w_ref[...], m_ref[...], v_ref[...], g)):
                outs[j * k + i][...] = val

    out_shape = [SDS(w.shape, F32) for _ in range(4) for w in ws]
    res = pl.pallas_call(body, name=name, out_shape=out_shape, compiler_params=_cp(60))(*ws, *ms, *vs, *lands, *owns)
    return [res[j * k:(j + 1) * k] for j in range(4)]


MESH = pl.DeviceIdType.MESH
HBM_SPEC = pl.BlockSpec(memory_space=pltpu.HBM)
SEM_SPEC = pl.BlockSpec(memory_space=pltpu.SEMAPHORE)
EFFECT = pltpu.SideEffectType.DATAFLOW_SIDE_EFFECTING


def _me_and_peers():
    x, y, c = lax.axis_index("x"), lax.axis_index("y"), lax.axis_index("c")
    me = 4 * x + 2 * y + c
    peers = []
    for r in range(1, NDEV):
        px, py, pc = x ^ ((r >> 2) & 1), y ^ ((r >> 1) & 1), c ^ (r & 1)
        peers.append(((px, py, pc), 4 * px + 2 * py + pc))
    return me, peers


def _land_shape(a, scatter):
    return (NDEV,) + (a.shape[1:] if scatter else a.shape)


def _remote(src, dst, send_sems, recv_sems, r, k, n, dev):
    i = r * n + k
    return pltpu.make_async_remote_copy(src_ref=src, dst_ref=dst, send_sem=send_sems.at[i], recv_sem=recv_sems.at[i],
                                        device_id=dev, device_id_type=MESH)


def _exchange(arrays, scatter, name):
    n = len(arrays)
    out_shape = [SDS(_land_shape(a, scatter), a.dtype) for a in arrays]

    def body(*refs):
        ins, outs = refs[:n], refs[n:2 * n]
        send_sems, recv_sems, loc_sems = refs[2 * n:]
        me, peers = _me_and_peers()
        local = []
        for k in range(n):
            src = ins[k].at[me] if scatter else ins[k]
            cp = pltpu.make_async_copy(src, outs[k].at[me], loc_sems.at[k])
            cp.start()
            local.append(cp)
        sends = []
        for r, (dev, lin) in enumerate(peers):
            for k in range(n):
                src = ins[k].at[lin] if scatter else ins[k]
                cp = _remote(src, outs[k].at[me], send_sems, recv_sems, r, k, n, dev)
                cp.start()
                sends.append(cp)
        for r, (dev, lin) in enumerate(peers):
            for k in range(n):
                src = ins[k].at[me] if scatter else ins[k]
                _remote(src, outs[k].at[lin], send_sems, recv_sems, r, k, n, dev).wait_recv()
        for cp in sends:
            cp.wait_send()
        for cp in local:
            cp.wait()

    return pl.pallas_call(
        body, name=name, in_specs=[HBM_SPEC] * n, out_specs=[HBM_SPEC] * n, out_shape=out_shape,
        scratch_shapes=[pltpu.SemaphoreType.DMA(((NDEV - 1) * n,)), pltpu.SemaphoreType.DMA(((NDEV - 1) * n,)),
                        pltpu.SemaphoreType.DMA((n,))],
    )(*arrays)


def _exchange_start(arrays, scatter, name):
    n = len(arrays)
    lands = [lax.empty(_land_shape(a, scatter), a.dtype) for a in arrays]

    def body(*refs):
        ins, lnd = refs[:n], refs[n:2 * n]
        send_sems, recv_sems, own_sems = refs[2 * n:2 * n + 3]
        token = refs[-1]
        me, peers = _me_and_peers()
        for r, (dev, lin) in enumerate(peers):
            for k in range(n):
                src = ins[k].at[lin] if scatter else ins[k]
                _remote(src, lnd[k].at[me], send_sems, recv_sems, r, k, n, dev).start()
        if not scatter:
            for k in range(n):
                pltpu.make_async_copy(ins[k], lnd[k].at[me], own_sems.at[k]).start()
        token[...] = jnp.zeros_like(token)

    sem = pltpu.SemaphoreType.DMA(((NDEV - 1) * n,))
    outs = pl.pallas_call(
        body, name=name,
        out_shape=(sem, sem, pltpu.SemaphoreType.DMA((n,)), *[pltpu.HBM(a.shape, a.dtype) for a in arrays],
                   *[pltpu.HBM(l.shape, l.dtype) for l in lands], SDS((8, 128), F32)),
        in_specs=[HBM_SPEC] * (2 * n),
        out_specs=(SEM_SPEC, SEM_SPEC, SEM_SPEC, *[HBM_SPEC] * (2 * n), pl.BlockSpec(memory_space=pltpu.VMEM)),
        input_output_aliases={k: 3 + k for k in range(2 * n)},
        compiler_params=pltpu.CompilerParams(has_side_effects=EFFECT),
    )(*[pltpu.with_memory_space_constraint(a, pltpu.HBM) for a in arrays],
      *[pltpu.with_memory_space_constraint(l, pltpu.HBM) for l in lands])
    return (n, scatter, outs[0], outs[1], outs[2], outs[3:3 + n], outs[3 + n:3 + 2 * n]), outs[-1]


def _exchange_wait(handle, after, name):
    n, scatter, send_sems, recv_sems, own_sems, thru, lands = handle
    after = tuple(after)

    def body(*refs):
        ins, lnd = refs[:n], refs[n:2 * n]
        send_sems, recv_sems, own_sems = refs[2 * n:2 * n + 3]
        me, peers = _me_and_peers()
        for r, (dev, lin) in enumerate(peers):
            for k in range(n):
                src = ins[k].at[lin] if scatter else ins[k]
                cp = _remote(src, lnd[k].at[lin], send_sems, recv_sems, r, k, n, dev)
                cp.wait_send()
                cp.wait_recv()
        if not scatter:
            for k in range(n):
                pltpu.make_async_copy(ins[k], lnd[k].at[me], own_sems.at[k]).wait()

    outs = pl.pallas_call(
        body, name=name,
        out_shape=(*[pltpu.HBM(a.shape, a.dtype) for a in thru], *[pltpu.HBM(l.shape, l.dtype) for l in lands]),
        in_specs=[HBM_SPEC] * (2 * n) + [SEM_SPEC, SEM_SPEC, SEM_SPEC] + [ANY_SPEC] * len(after),
        out_specs=tuple([HBM_SPEC] * (2 * n)),
        input_output_aliases={k: k for k in range(2 * n)},
        compiler_params=pltpu.CompilerParams(has_side_effects=EFFECT),
    )(*thru, *lands, send_sems, recv_sems, own_sems, *after)
    return list(outs[:n]), list(outs[n:])


CHIP_RELATIONS = (2, 4, 6)


def _peer(r):
    x, y, c = lax.axis_index("x"), lax.axis_index("y"), lax.axis_index("c")
    px, py, pc = x ^ ((r >> 2) & 1), y ^ ((r >> 1) & 1), c ^ (r & 1)
    return (px, py, pc), 4 * px + 2 * py + pc


def _copy(src, dst, send_sems, recv_sems, i, dev):
    return pltpu.make_async_remote_copy(src_ref=src, dst_ref=dst, send_sem=send_sems.at[i], recv_sem=recv_sems.at[i],
                                        device_id=dev, device_id_type=MESH)


def _gather2_start(a, name):
    land = lax.empty((NDEV,) + a.shape, a.dtype)

    def body(own, lnd, send_sems, recv_sems, own_sem, own_thru, lnd_thru, token):
        me = _my_index()
        for i, r in enumerate((1,) + CHIP_RELATIONS):
            dev, _ = _peer(r)
            _copy(own, lnd.at[me], send_sems, recv_sems, i, dev).start()
        pltpu.make_async_copy(own, lnd.at[me], own_sem.at[0]).start()
        token[...] = jnp.zeros_like(token)

    sem4 = pltpu.SemaphoreType.DMA((4,))
    outs = pl.pallas_call(
        body, name=name,
        out_shape=(sem4, sem4, pltpu.SemaphoreType.DMA((1,)), pltpu.HBM(a.shape, a.dtype),
                   pltpu.HBM(land.shape, land.dtype), SDS((8, 128), F32)),
        in_specs=[HBM_SPEC, HBM_SPEC],
        out_specs=(SEM_SPEC, SEM_SPEC, SEM_SPEC, HBM_SPEC, HBM_SPEC, pl.BlockSpec(memory_space=pltpu.VMEM)),
        input_output_aliases={0: 3, 1: 4},
        compiler_params=pltpu.CompilerParams(has_side_effects=EFFECT),
    )(pltpu.with_memory_space_constraint(a, pltpu.HBM), pltpu.with_memory_space_constraint(land, pltpu.HBM))
    return outs[:5], outs[5]


def _gather2_forward(handle, after, name):
    send_sems, recv_sems, own_sem, own, land = handle
    after = tuple(after)

    def body(lnd, recv_sems, *rest):
        send2, recv2, lnd_thru = rest[len(after):]
        sib, _ = _peer(1)
        for k, r in enumerate(CHIP_RELATIONS):
            dev, lin = _peer(r)
            _copy(lnd.at[lin], lnd.at[lin], recv_sems, recv_sems, 1 + k, dev).wait_recv()
            _copy(lnd.at[lin], lnd.at[lin], send2, recv2, k, sib).start()

    sem3 = pltpu.SemaphoreType.DMA((3,))
    send2, recv2, land = pl.pallas_call(
        body, name=name,
        out_shape=(sem3, sem3, pltpu.HBM(land.shape, land.dtype)),
        in_specs=[HBM_SPEC, SEM_SPEC] + [ANY_SPEC] * len(after),
        out_specs=(SEM_SPEC, SEM_SPEC, HBM_SPEC),
        input_output_aliases={0: 2},
        compiler_params=pltpu.CompilerParams(has_side_effects=EFFECT),
    )(land, recv_sems, *after)
    return send_sems, recv_sems, own_sem, send2, recv2, own, land


def _gather2_wait(handle, name):
    send_sems, recv_sems, own_sem, send2, recv2, own, land = handle

    def body(own_ref, lnd, send_sems, recv_sems, own_sem, send2, recv2, own_thru, lnd_thru):
        me = _my_index()
        sib, sib_lin = _peer(1)
        for i, r in enumerate((1,) + CHIP_RELATIONS):
            dev, _ = _peer(r)
            _copy(own_ref, lnd.at[me], send_sems, recv_sems, i, dev).wait_send()
        _copy(own_ref, lnd.at[sib_lin], send_sems, recv_sems, 0, sib).wait_recv()
        for k, r in enumerate(CHIP_RELATIONS):
            _, lin = _peer(r)
            _, lin_other = _peer(r ^ 1)
            _copy(lnd.at[lin], lnd.at[lin], send2, recv2, k, sib).wait_send()
            _copy(lnd.at[lin_other], lnd.at[lin_other], send2, recv2, k, sib).wait_recv()
        pltpu.make_async_copy(own_ref, lnd.at[me], own_sem.at[0]).wait()

    outs = pl.pallas_call(
        body, name=name,
        out_shape=(pltpu.HBM(own.shape, own.dtype), pltpu.HBM(land.shape, land.dtype)),
        in_specs=[HBM_SPEC, HBM_SPEC] + [SEM_SPEC] * 5,
        out_specs=(HBM_SPEC, HBM_SPEC),
        input_output_aliases={0: 0, 1: 1},
        compiler_params=pltpu.CompilerParams(has_side_effects=EFFECT),
    )(own, land, send_sems, recv_sems, own_sem, send2, recv2)
    return outs[1]


def _local_step(x, tgt, norm_even, first_weight, lam_re, lam_im, log_dt, b_re, b_im, c_re, c_im, s5_d, bglu,
                ret_gain, wsp, bsp, fnorm, late_weights, emit, start_token=None):
    L = x.shape[0]
    lr3, li3 = lam_re.reshape(G, 1, P), lam_im.reshape(G, 1, P)
    dt3 = log_dt.reshape(G, 1, 1)
    br3, bi3 = jnp.swapaxes(b_re, 1, 2), jnp.swapaxes(b_im, 1, 2)
    abr3, abi3, bbr3, bbi3 = _s5_disc(lr3, li3, dt3, br3, bi3)
    bb = jnp.concatenate([_embed(bbr3), _embed(bbi3)], axis=2).astype(MXU)
    cm = jnp.concatenate([_embed(jnp.swapaxes(c_re, 1, 2)), -_embed(jnp.swapaxes(c_im, 1, 2))], axis=1).astype(MXU)
    abr, abi = abr3.reshape(1, NSTATE), abi3.reshape(1, NSTATE)
    pwr, pwi, _, _ = _s5_tables(abr, abi, NT, "s5_tables_step")
    par, pai, pbr, pbi = _s5_tables(pwr[NT - 1:NT], pwi[NT - 1:NT], CH, "s5_tables_chunk")
    inv = (ROPE_BASE ** (-jnp.arange(DK // 2, dtype=F32) / (DK // 2))).reshape(1, DK // 2)
    cos, sin = _rope_tables(L, inv)
    bsp3 = bsp.reshape(SG, CH, 1)

    def dep(token):
        return NO_DEPS if token is None else (token,)

    win_e = first_weight((cos, pbi, cm))
    p, u3 = _in_proj(x, norm_even, win_e, "in_even", dep(start_token), lane_blocks=True)
    ypre, s5_states = _s5_scan_fwd(u3, bb, cm, pwr, pwi, pwr.T, pwi.T, par, pai, s5_d)
    yb, ret_states = _ret_fwd(p, cos, sin, ret_gain)
    wglu, wout_e, norm_odd, win_o, sgu_gain, wout_o = late_weights((ypre, yb))
    x1 = _out_even(x, ypre, p, yb, wglu, bglu, wout_e)
    p2 = _in_proj(x1, norm_odd, win_o, "in_odd")
    dp2, y_o, dx2, g_sgu_gain, g_wsp, g_bsp, g_fnorm, loss = _sgu_fwd_bwd(
        p2, x1, sgu_gain, wsp, bsp3, wout_o, fnorm, tgt)

    g_wout_o = _wgrad_rows([y_o], dx2, "wgrad_out_odd")
    g_win_o = _wgrad_cols(x1, norm_odd, dp2, "wgrad_in_odd")
    tok = emit("odd", dict(w_in_odd=g_win_o, w_out_odd=g_wout_o))
    dx1, g_norm_odd = _in_proj_bwd_x(dp2, x1, norm_odd, win_o, dx2, "in_odd_bwd", dep(tok))
    tok = emit("small_odd", dict(norm_odd=g_norm_odd, sgu_norm_gain=g_sgu_gain, sgu_w_spatial=g_wsp,
                                 sgu_b_spatial=g_bsp.reshape(SG, CH), final_norm=g_fnorm))

    dypre, daz, yg, dt, ya2, g_bglu = _s5_gate_bwd(ypre, p, dx1, wout_e, wglu, bglu, dep(tok))
    g_wglu = _wgrad_rows([yg], dt, "wgrad_glu")
    tok = emit("glu", dict(s5_w_glu=g_wglu))
    du, g_d, g_cm, g_bb, g_ar, g_ai = _s5_scan_bwd(u3, dypre, s5_states, bb, cm, jnp.swapaxes(cm, 1, 2), abr, abi,
                                                   pwr, pwi, par, pai, pbr, pbi, s5_d, dep(tok))
    dbbr3 = _diag_blocks(g_bb[:, :, :SW], HG, P)
    dbbi3 = _diag_blocks(g_bb[:, :, SW:], HG, P)
    g_c_re = jnp.swapaxes(_diag_blocks(g_cm[:, :SW, :], P, HG), 1, 2)
    g_c_im = -jnp.swapaxes(_diag_blocks(g_cm[:, SW:, :], P, HG), 1, 2)
    g_lr3, g_li3, g_dt3, g_br3, g_bi3 = _s5_disc_bwd(
        lr3, li3, dt3, br3, bi3, g_ar.reshape(G, 1, P), g_ai.reshape(G, 1, P), dbbr3, dbbi3)
    tok = emit("small_s5", dict(
        s5_lam_re=g_lr3.reshape(G, P), s5_lam_im=g_li3.reshape(G, P), s5_log_dt=g_dt3.reshape(1, G),
        s5_b_re=g_br3, s5_b_im=g_bi3, s5_c_re=g_c_re, s5_c_im=g_c_im, s5_d=g_d, s5_b_glu=g_bglu))
    dp, yb2, g_ret_gain = _ret_bwd(p, cos, sin, ret_gain, ret_states, dx1, wout_e, du, daz, dep(tok))
    g_win_e = _wgrad_cols(x, norm_even, dp, "wgrad_in_even")
    tok = emit("even_cols", dict(w_in_even=g_win_e))
    g_wout_e = _wgrad_rows([ya2, yb2], dx1, "wgrad_out_even", dep(tok))
    tok = emit("even_rows", dict(w_out_even=g_wout_e))
    dx, g_norm_even = _in_proj_bwd_x(dp, x, norm_even, win_e, dx1, "in_even_bwd", dep(tok))
    emit("last", dict(ret_gn_gain=g_ret_gain, norm_even=g_norm_even, loss=loss))
    return dx


WEIGHTS = ['norm_even', 'w_in_even', 's5_lam_re', 's5_lam_im', 's5_log_dt', 's5_b_re', 's5_b_im', 's5_c_re',
           's5_c_im', 's5_d', 's5_w_glu', 's5_b_glu', 'ret_gn_gain', 'w_out_even', 'norm_odd', 'w_in_odd',
           'sgu_norm_gain', 'sgu_w_spatial', 'sgu_b_spatial', 'w_out_odd', 'final_norm']
BIG = ['w_in_even', 's5_w_glu', 'w_out_even', 'w_in_odd', 'w_out_odd']
SHARDED_SMALL = {'norm_odd': D // NDEV, 'sgu_norm_gain': DI // NDEV}
SCATTER_STAGES = ("odd", "glu", "even_cols", "even_rows")
GATHER_STAGES = ("small_odd", "small_s5")


def _view(n, a):
    if n in ('s5_b_re', 's5_b_im'):
        return jnp.swapaxes(a[0], 1, 2)
    if n == 'final_norm':
        return a.reshape(1, D)
    return a[0] if a.ndim >= 3 else a


def _unview(n, t, shape):
    if n in ('s5_b_re', 's5_b_im'):
        return jnp.swapaxes(t, 1, 2)[None]
    return t.reshape(shape)


def kernel(x, norm_even, w_in_even, s5_lam_re, s5_lam_im, s5_log_dt, s5_b_re, s5_b_im, s5_c_re, s5_c_im, s5_d, s5_w_glu, s5_b_glu, ret_gn_gain, w_out_even, norm_odd, w_in_odd, sgu_norm_gain, sgu_w_spatial, sgu_b_spatial, w_out_odd, final_norm, loss_target, m_norm_even, m_w_in_even, m_s5_lam_re, m_s5_lam_im, m_s5_log_dt, m_s5_b_re, m_s5_b_im, m_s5_c_re, m_s5_c_im, m_s5_d, m_s5_w_glu, m_s5_b_glu, m_ret_gn_gain, m_w_out_even, m_norm_odd, m_w_in_odd, m_sgu_norm_gain, m_sgu_w_spatial, m_sgu_b_spatial, m_w_out_odd, m_final_norm, v_norm_even, v_w_in_even, v_s5_lam_re, v_s5_lam_im, v_s5_log_dt, v_s5_b_re, v_s5_b_im, v_s5_c_re, v_s5_c_im, v_s5_d, v_s5_w_glu, v_s5_b_glu, v_ret_gn_gain, v_w_out_even, v_norm_odd, v_w_in_odd, v_sgu_norm_gain, v_sgu_w_spatial, v_sgu_b_spatial, v_w_out_odd, v_final_norm):
    args = dict(locals())
    w = {n: args[n] for n in WEIGHTS}
    m = {n: args["m_" + n] for n in WEIGHTS}
    v = {n: args["v_" + n] for n in WEIGHTS}
    me = _my_index()

    first_handle, _ = _gather2_start(w['w_in_even'][0].astype(MXU), "gather_first_start")

    def first_weight(after):
        return _gather2_wait(_gather2_forward(first_handle, after, "gather_first_forward"), "gather_first_wait")

    late_own = [w['s5_w_glu'][0].astype(MXU), w['w_out_even'][0].astype(MXU), w['norm_odd'],
                w['w_in_odd'][0].astype(MXU), w['sgu_norm_gain'], w['w_out_odd'][0].astype(MXU)]
    late_handle, start_token = _exchange_start(late_own, False, "gather_late_start")

    def late_weights(after):
        _, (wglu, wout_e, nodd, win_o, sgug, wout_o) = _exchange_wait(late_handle, after, "gather_late_wait")
        return (wglu.reshape(D, D), wout_e.reshape(DI, D), nodd.reshape(1, D), win_o, sgug.reshape(1, DI),
                wout_o.reshape(DI, D))

    pending = {}
    small_last = {}

    def emit(stage, grads):
        if stage == "last":
            small_last.update(grads)
            return None
        names = list(grads)
        handle, token = _exchange_start([grads[n] for n in names], stage in SCATTER_STAGES, stage + "_start")
        pending[stage] = (handle, names)
        return token

    dx = _local_step(
        x[0], loss_target[0], w['norm_even'], first_weight, w['s5_lam_re'][0], w['s5_lam_im'][0], w['s5_log_dt'][0],
        w['s5_b_re'][0], w['s5_b_im'][0], w['s5_c_re'][0], w['s5_c_im'][0], w['s5_d'], w['s5_b_glu'],
        w['ret_gn_gain'], w['sgu_w_spatial'][0], w['sgu_b_spatial'][0], w['final_norm'].reshape(1, D),
        late_weights, emit, start_token)

    out_g, out_d, out_m, out_v = {}, {}, {}, {}
    after = dx
    for stage in SCATTER_STAGES:
        handle, names = pending[stage]
        sent, lands = _exchange_wait(handle, (after,), stage + "_wait")
        for n, land, stack in zip(names, lands, sent):
            shp = w[n].shape
            r, c = shp[1], shp[2]
            own = lax.dynamic_index_in_dim(stack, me, 0, keepdims=False)
            res = _adamw(w[n].reshape(r, c), m[n].reshape(r, c), v[n].reshape(r, c), land, own, "adamw_" + n)
            out_g[n], out_d[n], out_m[n], out_v[n] = (t.reshape(shp) for t in res)
            after = res[0]

    names, owns, lands = [], [], []
    for stage in GATHER_STAGES:
        handle, stage_names = pending[stage]
        sent, got = _exchange_wait(handle, (after,), stage + "_wait")
        names, owns, lands = names + stage_names, owns + sent, lands + got
    last_names = list(small_last)
    last = _exchange([small_last[n] for n in last_names], False, "gather_last")
    loss_parts = last[last_names.index("loss")][:, 0, 0]
    for n, own, land in zip(last_names, (small_last[n] for n in last_names), last):
        if n != "loss":
            names, owns, lands = names + [n], owns + [own], lands + [land]
    for i, n in enumerate(names):
        if n in SHARDED_SMALL:
            width = SHARDED_SMALL[n]
            owns[i] = lax.dynamic_slice_in_dim(owns[i], me * width, width, axis=1)
            lands[i] = lax.dynamic_slice_in_dim(lands[i], me * width, width, axis=2)
    res = _adamw_many([_view(n, w[n]) for n in names], [_view(n, m[n]) for n in names],
                      [_view(n, v[n]) for n in names], lands, owns, "adamw_small")
    for dst, vals in zip((out_g, out_d, out_m, out_v), res):
        for n, t in zip(names, vals):
            dst[n] = _unview(n, t, w[n].shape)

    loss_total = jnp.sum(loss_parts)
    return (loss_total, dx[None], *[out_g[n] for n in WEIGHTS], *[out_d[n] for n in WEIGHTS],
            *[out_m[n] for n in WEIGHTS], *[out_v[n] for n in WEIGHTS])
```

```python
import math

import jax
import jax.numpy as jnp
from jax import lax
from jax.experimental import pallas as pl
from jax.experimental.pallas import tpu as pltpu

F32 = jnp.float32
MXU = jnp.bfloat16
AXES = ("x", "y", "c")
NDEV = 8
D = 1024
NIN = 6144
WIN_BLK = NIN // NDEV
DI = 2048
G, P, HG = 64, 64, 16
GB = 8
NJ = G // GB
SW = GB * P
UW = GB * HG
NSTATE = G * P
HEADS, DK = 4, 256
CH = 128
SG, SGD = 4, 512
EPS = 1e-6
ROPE_BASE = 10000.0
VMEM_CAP_V7X = 64 * 1024 * 1024
LOG_G = [math.log1p(-2.0 ** (-5.0 - h)) for h in range(HEADS)]
GELU_C = math.sqrt(2.0 / math.pi)

ADAM_LR, ADAM_B1, ADAM_B2, ADAM_EPS, ADAM_WD, ADAM_STEP = 0.001, 0.9, 0.999, 1e-08, 0.01, 10
BC1 = 1.0 - ADAM_B1 ** ADAM_STEP
BC2 = 1.0 - ADAM_B2 ** ADAM_STEP

SDS = jax.ShapeDtypeStruct
ARB2 = ("arbitrary", "arbitrary")


def _cp(vmem_mib, sem=None):
    kw = dict(vmem_limit_bytes=min(vmem_mib * 1024 * 1024, VMEM_CAP_V7X - 4 * 1024 * 1024))
    if sem is not None:
        kw["dimension_semantics"] = sem
    return pltpu.CompilerParams(**kw)


def _mm(a, b):
    return jnp.dot(a.astype(MXU), b.astype(MXU), preferred_element_type=F32)


def _mm_nt(a, b):
    return lax.dot_general(a.astype(MXU), b.astype(MXU), (((1,), (1,)), ((), ())), preferred_element_type=F32)


def _mm_tn(a, b):
    return lax.dot_general(a.astype(MXU), b.astype(MXU), (((0,), (0,)), ((), ())), preferred_element_type=F32)


def _gelu(x):
    return _gelu_and_grad(x)[0]


def _gelu_and_grad(x):
    x2 = x * x
    th = jnp.tanh(GELU_C * x * (1.0 + 0.044715 * x2))
    hp = 0.5 * (1.0 + th)
    return x * hp, hp + 0.5 * x * (1.0 - th * th) * GELU_C * (1.0 + 3.0 * 0.044715 * x2)


def _silu_and_grad(x):
    s = jax.nn.sigmoid(x)
    return x * s, s * (1.0 + x * (1.0 - s))


def _full(shape):
    nd = len(shape)
    return pl.BlockSpec(shape, lambda *_: (0,) * nd)


def _rms(xf):
    r = lax.rsqrt(jnp.mean(xf * xf, axis=-1, keepdims=True) + EPS)
    return xf * r, r


ANY_SPEC = pl.BlockSpec(memory_space=pl.ANY)
NO_DEPS = ()


def _load_once(src_hbm, dst_vmem, sem):
    @pl.when(pl.program_id(0) == 0)
    def _():
        cp = pltpu.make_async_copy(src_hbm, dst_vmem, sem)
        cp.start()
        cp.wait()


def _lane_blocks(L):
    return SDS((NJ, L, UW), F32)


def _lane_block_spec(rows, index):
    return pl.BlockSpec((NJ, rows, UW), lambda i: (0, index(i), 0))


def _from_lane_blocks(ref):
    return jnp.concatenate([ref[j] for j in range(NJ)], axis=1)


def _to_lane_blocks(ref, v):
    for j in range(NJ):
        ref[j] = v[:, j * UW:(j + 1) * UW].astype(ref.dtype)


def _in_proj(x, gain, wst, name, deps=NO_DEPS, lane_blocks=False):
    L = x.shape[0]
    tm = min(512, L)

    def body(x_ref, g_ref, w_hbm, *rest):
        outs = rest[len(deps):]
        o_ref, w_scr, sem = outs[0], outs[-2], outs[-1]
        _load_once(w_hbm, w_scr, sem)
        xhat, _ = _rms(x_ref[...])
        h = (xhat * g_ref[...]).astype(MXU)
        for c in range(NDEV):
            o_ref[:, c * WIN_BLK:(c + 1) * WIN_BLK] = jnp.dot(h, w_scr[c], preferred_element_type=F32)
        if lane_blocks:
            _to_lane_blocks(outs[1], o_ref[:, 0:D])

    p_spec, p_shape = pl.BlockSpec((tm, NIN), lambda i: (i, 0)), SDS((L, NIN), F32)
    return pl.pallas_call(
        body, name=name, grid=(L // tm,),
        in_specs=[pl.BlockSpec((tm, D), lambda i: (i, 0)), _full((1, D)), ANY_SPEC] + [ANY_SPEC] * len(deps),
        out_specs=[p_spec, _lane_block_spec(tm, lambda i: i)] if lane_blocks else p_spec,
        out_shape=[p_shape, _lane_blocks(L)] if lane_blocks else p_shape,
        scratch_shapes=[pltpu.VMEM((NDEV, D, WIN_BLK), MXU), pltpu.SemaphoreType.DMA(())],
        compiler_params=_cp(58, ("arbitrary",)),
    )(x, gain, wst, *deps)


def _in_proj_bwd_x(dp, x, gain, wst, dres, name, deps=NO_DEPS):
    L = x.shape[0]
    tm = min(512, L)

    def body(dp_ref, x_ref, g_ref, w_hbm, dres_ref, *rest):
        dx_ref, gg_ref, w_scr, sem = rest[len(deps):]
        _load_once(w_hbm, w_scr, sem)

        @pl.when(pl.program_id(0) == 0)
        def _():
            gg_ref[...] = jnp.zeros_like(gg_ref)

        dh = _mm_nt(dp_ref[:, 0:WIN_BLK], w_scr[0])
        for c in range(1, NDEV):
            dh += _mm_nt(dp_ref[:, c * WIN_BLK:(c + 1) * WIN_BLK], w_scr[c])
        xhat, r = _rms(x_ref[...])
        dxhat = dh * g_ref[...]
        dx_ref[...] = dres_ref[...] + r * (dxhat - xhat * jnp.mean(dxhat * xhat, axis=-1, keepdims=True))
        gg_ref[...] += jnp.sum(dh * xhat, axis=0, keepdims=True)

    row = pl.BlockSpec((tm, D), lambda i: (i, 0))
    return pl.pallas_call(
        body, name=name, grid=(L // tm,),
        in_specs=[pl.BlockSpec((tm, NIN), lambda i: (i, 0)), row, _full((1, D)), ANY_SPEC, row]
        + [ANY_SPEC] * len(deps),
        out_specs=[row, _full((1, D))],
        out_shape=[SDS((L, D), F32), SDS((1, D), F32)],
        scratch_shapes=[pltpu.VMEM((NDEV, D, WIN_BLK), MXU), pltpu.SemaphoreType.DMA(())],
        compiler_params=_cp(56, ("arbitrary",)),
    )(dp, x, gain, wst, dres, *deps)


def _wgrad_cols(x, gain, dp, name, deps=NO_DEPS):
    L = x.shape[0]
    tk = min(1024, L)
    nk = L // tk
    halves = 2
    nh = NDEV // halves

    def body(x_ref, g_ref, dp_ref, *rest):
        o_ref, acc = rest[len(deps):]
        k = pl.program_id(1)

        @pl.when(k == 0)
        def _():
            acc[...] = jnp.zeros_like(acc)

        xhat, _ = _rms(x_ref[...])
        acc[...] += _mm_tn(xhat * g_ref[...], dp_ref[...])

        @pl.when(k == nk - 1)
        def _():
            for c in range(nh):
                o_ref[c] = acc[:, c * WIN_BLK:(c + 1) * WIN_BLK].astype(o_ref.dtype)

    return pl.pallas_call(
        body, name=name, grid=(halves, nk),
        in_specs=[pl.BlockSpec((tk, D), lambda n, k: (k, 0)), _full((1, D)),
                  pl.BlockSpec((tk, nh * WIN_BLK), lambda n, k: (k, n))] + [ANY_SPEC] * len(deps),
        out_specs=pl.BlockSpec((nh, D, WIN_BLK), lambda n, k: (n, 0, 0)),
        out_shape=SDS((NDEV, D, WIN_BLK), MXU),
        scratch_shapes=[pltpu.VMEM((D, nh * WIN_BLK), F32)],
        compiler_params=_cp(56, ARB2),
    )(x, gain, dp, *deps)


def _wgrad_rows(a_parts, b, name, deps=NO_DEPS):
    L, N = b.shape
    na = len(a_parts)
    widths = [a.shape[1] for a in a_parts]
    M = sum(widths)
    tk = min(1024, L)
    nk = L // tk

    def body(*refs):
        a_refs, b_ref = refs[:na], refs[na]
        o_ref, acc = refs[na + 1 + len(deps):]
        k = pl.program_id(0)

        @pl.when(k == 0)
        def _():
            acc[...] = jnp.zeros_like(acc)

        bv = b_ref[...].astype(MXU)
        off = 0
        for a_ref, wd in zip(a_refs, widths):
            acc[off:off + wd, :] += _mm_tn(a_ref[...], bv)
            off += wd

        @pl.when(k == nk - 1)
        def _():
            o_ref[...] = acc[...].astype(o_ref.dtype).reshape(o_ref.shape)

    return pl.pallas_call(
        body, name=name, grid=(nk,),
        in_specs=[pl.BlockSpec((tk, wd), lambda k: (k, 0)) for wd in widths]
        + [pl.BlockSpec((tk, N), lambda k: (k, 0))] + [ANY_SPEC] * len(deps),
        out_specs=_full((NDEV, M // NDEV, N)),
        out_shape=SDS((NDEV, M // NDEV, N), MXU),
        scratch_shapes=[pltpu.VMEM((M, N), F32)],
        compiler_params=_cp(48, ("arbitrary",)),
    )(*a_parts, b, *deps)


def _s5_disc_fn(lr_raw, li, logdt, br, bi):
    lr = jnp.minimum(lr_raw, -1e-4)
    dt = jnp.exp(logdt)
    mag = jnp.exp(lr * dt)
    abr = mag * jnp.cos(li * dt)
    abi = mag * jnp.sin(li * dt)
    den = lr * lr + li * li
    nre = abr - 1.0
    nim = abi
    zr = (nre * lr + nim * li) / den
    zi = (nim * lr - nre * li) / den
    return abr, abi, zr * br - zi * bi, zr * bi + zi * br


def _s5_disc(lr, li, logdt, br, bi):
    def body(lr_ref, li_ref, dt_ref, br_ref, bi_ref, abr_ref, abi_ref, bbr_ref, bbi_ref):
        abr, abi, bbr, bbi = _s5_disc_fn(lr_ref[...], li_ref[...], dt_ref[...], br_ref[...], bi_ref[...])
        abr_ref[...] = abr
        abi_ref[...] = abi
        bbr_ref[...] = bbr
        bbi_ref[...] = bbi

    s1, s3 = SDS((G, 1, P), F32), SDS((G, HG, P), F32)
    return pl.pallas_call(body, name="s5_disc", out_shape=[s1, s1, s3, s3])(lr, li, logdt, br, bi)


def _s5_disc_bwd(lr, li, logdt, br, bi, dabr, dabi, dbbr, dbbi):
    def body(lr_ref, li_ref, dt_ref, br_ref, bi_ref, c0, c1, c2, c3, o0, o1, o2, o3, o4):
        _, vjp = jax.vjp(_s5_disc_fn, lr_ref[...], li_ref[...], dt_ref[...], br_ref[...], bi_ref[...])
        g = vjp((c0[...], c1[...], c2[...], c3[...]))
        for o, v in zip((o0, o1, o2, o3, o4), g):
            o[...] = v

    s1, s3 = SDS((G, 1, P), F32), SDS((G, HG, P), F32)
    return pl.pallas_call(body, name="s5_disc_bwd", out_shape=[s1, s1, SDS((G, 1, 1), F32), s3, s3])(
        lr, li, logdt, br, bi, dabr, dabi, dbbr, dbbi)


def _s5_tables(abr, abi, rows, name):
    def body(ar_ref, ai_ref, pfr, pfi, pbr, pbi):
        pfr[0:1, :] = ar_ref[...]
        pfi[0:1, :] = ai_ref[...]
        pbr[rows - 1:rows, :] = ar_ref[...]
        pbi[rows - 1:rows, :] = ai_ref[...]
        n = 1
        while n < rows:
            er, ei = pfr[n - 1:n, :], pfi[n - 1:n, :]
            xr, xi = pfr[0:n, :], pfi[0:n, :]
            pfr[n:2 * n, :] = er * xr - ei * xi
            pfi[n:2 * n, :] = er * xi + ei * xr
            yr, yi = pbr[rows - n:rows, :], pbi[rows - n:rows, :]
            pbr[rows - 2 * n:rows - n, :] = er * yr - ei * yi
            pbi[rows - 2 * n:rows - n, :] = er * yi + ei * yr
            n *= 2

    s = SDS((rows, NSTATE), F32)
    return pl.pallas_call(body, name=name, out_shape=[s, s, s, s], compiler_params=_cp(40))(abr, abi)


def _cscan(br, bi, pr_ref, pi_ref, reverse):
    T = br.shape[0]
    sign = -1.0 if reverse else 1.0
    row = lax.broadcasted_iota(jnp.int32, br.shape, 0)
    k = 1
    while k < T:
        akr = pr_ref[k - 1:k, :]
        aki = sign * pi_ref[k - 1:k, :]

        def shift(v):
            if k % 8 == 0:
                z = jnp.zeros((k, v.shape[1]), v.dtype)
                return jnp.concatenate([v[k:], z], 0) if reverse else jnp.concatenate([z, v[:T - k]], 0)
            if reverse:
                return jnp.where(row < T - k, pltpu.roll(v, T - k, 0), 0.0)
            return jnp.where(row >= k, pltpu.roll(v, k, 0), 0.0)

        sr, si = shift(br), shift(bi)
        br, bi = br + akr * sr - aki * si, bi + akr * si + aki * sr
        k *= 2
    return br, bi


def _embed(t):
    a, b = t.shape[1], t.shape[2]
    return jnp.einsum("jgab,gh->jgahb", t.reshape(NJ, GB, a, b), jnp.eye(GB, dtype=t.dtype)).reshape(NJ, GB * a, GB * b)


def _diag_blocks(t, a, b):
    return jnp.einsum("jgahb,gh->jgab", t.reshape(NJ, GB, a, GB, b), jnp.eye(GB, dtype=t.dtype)).reshape(G, a, b)


NT = 16


def _chunks(L):
    ncb = min(CH, L // NT)
    return ncb, NT * ncb


def _cmul_add(ar, ai, xr, xi, br, bi):
    return ar * xr - ai * xi + br, ar * xi + ai * xr + bi


def _pow_weights(w_ref, pwr_ref, pwi_ref, dst, adjoint):
    w = w_ref[...].astype(F32)
    wr, wi = w[:, :SW], w[:, SW:]
    for t in range(NT):
        k = t if adjoint else NT - 1 - t
        if k == 0:
            blk = w
        else:
            pr, pi = pwr_ref[k - 1:k, :], pwi_ref[k - 1:k, :]
            if adjoint:
                blk = jnp.concatenate([pr * wr + pi * wi, pr * wi - pi * wr], axis=1)
            else:
                blk = jnp.concatenate([pr * wr - pi * wi, pr * wi + pi * wr], axis=1)
        dst[t * UW:(t + 1) * UW, :] = blk.astype(dst.dtype)


def _s5_states(u_ref, bb_ref, bbp_scr, par_ref, pai_ref, cr, ci, ncb, bu_scr):
    us = [u_ref[pl.ds(t, ncb, stride=NT), :] for t in range(NT)]
    for t in range(NT):
        bu_scr[t] = _mm(us[t], bb_ref[...])
    e = _mm(jnp.concatenate(us, axis=1), bbp_scr[...])
    xr, xi = _cscan(e[:, :SW], e[:, SW:], par_ref, pai_ref, False)
    fr, fi = _cmul_add(par_ref[0:ncb, :], pai_ref[0:ncb, :], cr, ci, xr, xi)
    row = lax.broadcasted_iota(jnp.int32, fr.shape, 0)
    cinr = jnp.where(row >= 1, pltpu.roll(fr, 1, 0), cr)
    cini = jnp.where(row >= 1, pltpu.roll(fi, 1, 0), ci)
    return cinr, cini, jnp.concatenate([fr[ncb - 1:ncb, :], fi[ncb - 1:ncb, :]], axis=1)


def _slices(ref, ncb, axis):
    return jnp.concatenate([ref[pl.ds(t, ncb, stride=NT), :] for t in range(NT)], axis=axis)


def _s5_scan_fwd(u3, bb, cm, pwr, pwi, pwrt, pwit, par, pai, dskip):
    L = u3.shape[1]
    ncb, tb = _chunks(L)
    nb = L // tb

    def build(bb_ref, cm_ref, pwr_ref, pwi_ref, pct_ref, pit_ref, bbp_scr, ktp_scr, zc_scr):
        w = bb_ref[...].astype(F32)
        wr, wi = w[:, :SW], w[:, SW:]
        cmv = cm_ref[...].astype(F32)
        ct, cb = cmv[:SW, :], cmv[SW:, :]
        zero = jnp.zeros((UW, UW), MXU)
        for tau in range(NT):
            if tau == 0:
                blk = w
            else:
                pr, pi = pwr_ref[tau - 1:tau, :], pwi_ref[tau - 1:tau, :]
                blk = jnp.concatenate([pr * wr - pi * wi, pr * wi + pi * wr], axis=1)
            blk = blk.astype(MXU)
            bbp_scr[(NT - 1 - tau) * UW:(NT - tau) * UW, :] = blk
            k = jnp.dot(blk, cm_ref[...], preferred_element_type=F32).astype(MXU)
            for j in range(NT - tau):
                ktp_scr[j * UW:(j + 1) * UW, (j + tau) * UW:(j + tau + 1) * UW] = k
            pc, pic = pct_ref[:, tau:tau + 1], pit_ref[:, tau:tau + 1]
            zc_scr[:, tau * UW:(tau + 1) * UW] = jnp.concatenate(
                [pc * ct + pic * cb, pc * cb - pic * ct], axis=0).astype(MXU)
        for j in range(NT):
            for t in range(j):
                ktp_scr[j * UW:(j + 1) * UW, t * UW:(t + 1) * UW] = zero

    def body(u_ref, bb_ref, cm_ref, pwr_ref, pwi_ref, pct_ref, pit_ref, par_ref, pai_ref, d_ref, ypre_ref, st_ref,
             carry, cin_scr, bbp_scr, ktp_scr, zc_scr):
        @pl.when(pl.program_id(1) == 0)
        def _():
            carry[...] = jnp.zeros_like(carry)
            build(bb_ref, cm_ref, pwr_ref, pwi_ref, pct_ref, pit_ref, bbp_scr, ktp_scr, zc_scr)

        c = carry[...]
        st_ref[...] = c
        ucat = _slices(u_ref, ncb, 1)
        ub = ucat.astype(MXU)
        e = jnp.dot(ub, bbp_scr[...], preferred_element_type=F32)
        xr, xi = _cscan(e[:, :SW], e[:, SW:], par_ref, pai_ref, False)
        cr, ci = c[:, :SW], c[:, SW:]
        fr, fi = _cmul_add(par_ref[0:ncb, :], pai_ref[0:ncb, :], cr, ci, xr, xi)
        carry[...] = jnp.concatenate([fr[ncb - 1:ncb, :], fi[ncb - 1:ncb, :]], axis=1)
        row = lax.broadcasted_iota(jnp.int32, fr.shape, 0)
        cin_scr[:, :SW] = jnp.where(row >= 1, pltpu.roll(fr, 1, 0), cr).astype(MXU)
        cin_scr[:, SW:] = jnp.where(row >= 1, pltpu.roll(fi, 1, 0), ci).astype(MXU)
        y = (jnp.dot(ub, ktp_scr[...], preferred_element_type=F32)
             + jnp.dot(cin_scr[...], zc_scr[...], preferred_element_type=F32)
             + jnp.tile(d_ref[...], (1, NT)) * ucat)
        for t in range(NT):
            ypre_ref[pl.ds(t, ncb, stride=NT), :] = y[:, t * UW:(t + 1) * UW]

    tab = pl.BlockSpec((CH, SW), lambda j, i: (0, j))
    stp = pl.BlockSpec((NT, SW), lambda j, i: (0, j))
    stt = pl.BlockSpec((SW, NT), lambda j, i: (j, 0))
    vec = lambda w: pl.BlockSpec((1, w), lambda j, i: (0, j))
    return pl.pallas_call(
        body, name="s5_scan_fwd", grid=(NJ, nb),
        in_specs=[pl.BlockSpec((None, tb, UW), lambda j, i: (j, i, 0)),
                  pl.BlockSpec((None, UW, 2 * SW), lambda j, i: (j, 0, 0)),
                  pl.BlockSpec((None, 2 * SW, UW), lambda j, i: (j, 0, 0)),
                  stp, stp, stt, stt, tab, tab, vec(UW)],
        out_specs=[pl.BlockSpec((None, tb, UW), lambda j, i: (j, i, 0)),
                   pl.BlockSpec((None, None, 1, 2 * SW), lambda j, i: (j, i, 0, 0))],
        out_shape=[_lane_blocks(L), SDS((NJ, nb, 1, 2 * SW), F32)],
        scratch_shapes=[pltpu.VMEM((1, 2 * SW), F32), pltpu.VMEM((ncb, 2 * SW), MXU),
                        pltpu.VMEM((NT * UW, 2 * SW), MXU), pltpu.VMEM((NT * UW, NT * UW), MXU),
                        pltpu.VMEM((2 * SW, NT * UW), MXU)],
        compiler_params=_cp(56, ARB2),
    )(u3, bb, cm, pwr, pwi, pwrt, pwit, par, pai, dskip)


def _s5_gate_bwd(ypre3, p, dx1, wout_e, wglu, bglu, deps=NO_DEPS):
    L = p.shape[0]
    tm = min(256, L)

    def body(y_ref, az_ref, dx1_ref, wo_ref, wg_ref, bg_ref, *rest):
        dyp_ref, daz_ref, yg_ref, dt_ref, ya_ref, gbg_ref = rest[len(deps):]

        @pl.when(pl.program_id(0) == 0)
        def _():
            gbg_ref[...] = jnp.zeros_like(gbg_ref)

        yg, dgelu = _gelu_and_grad(_from_lane_blocks(y_ref))
        sg = jax.nn.sigmoid(_mm(yg, wg_ref[...]) + bg_ref[...])
        act, dact = _silu_and_grad(az_ref[...])
        y2 = yg * sg
        dya = _mm_nt(dx1_ref[...], wo_ref[...])
        daz_ref[...] = (dya * y2 * dact).astype(daz_ref.dtype)
        dy2 = dya * act
        dt = dy2 * yg * sg * (1.0 - sg)
        dyg = dy2 * sg + _mm_nt(dt, wg_ref[...])
        _to_lane_blocks(dyp_ref, dyg * dgelu)
        yg_ref[...] = yg.astype(yg_ref.dtype)
        dt_ref[...] = dt.astype(dt_ref.dtype)
        ya_ref[...] = (y2 * act).astype(ya_ref.dtype)
        gbg_ref[...] += jnp.sum(dt, axis=0, keepdims=True)

    row = pl.BlockSpec((tm, D), lambda i: (i, 0))
    return pl.pallas_call(
        body, name="s5_gate_bwd", grid=(L // tm,),
        in_specs=[_lane_block_spec(tm, lambda i: i), pl.BlockSpec((tm, D), lambda i: (i, 1)), row,
                  pl.BlockSpec((D, D), lambda i: (0, 0)), _full((D, D)), _full((1, D))] + [ANY_SPEC] * len(deps),
        out_specs=[_lane_block_spec(tm, lambda i: i), row, row, row, row, _full((1, D))],
        out_shape=[_lane_blocks(L), SDS((L, D), MXU), SDS((L, D), MXU), SDS((L, D), MXU), SDS((L, D), MXU),
                   SDS((1, D), F32)],
        compiler_params=_cp(40, ("arbitrary",)),
    )(ypre3, p, dx1, wout_e, wglu, bglu, *deps)


def _s5_scan_bwd(u3, dypre3, states, bb, cm, cmt, abr, abi, pwr, pwi, par, pai, pbr, pbi, dskip, deps=NO_DEPS):
    L = u3.shape[1]
    ncb, tb = _chunks(L)
    nb = L // tb
    rev = lambda i: nb - 1 - i

    def body(u_ref, dy_ref, st_ref, bb_ref, cm_ref, cmt_ref, ar_ref, ai_ref, pwr_ref, pwi_ref, par_ref, pai_ref,
             pbr_ref, pbi_ref, d_ref, *rest):
        (du_ref, gd_ref, gcm_ref, gbb_ref, gar_ref, gai_ref,
         lcarry, bu_scr, s_scr, gs_scr, bbp_scr, cmp_scr) = rest[len(deps):]
        del cm_ref

        @pl.when(pl.program_id(1) == 0)
        def _():
            _pow_weights(bb_ref, pwr_ref, pwi_ref, bbp_scr, False)
            _pow_weights(cmt_ref, pwr_ref, pwi_ref, cmp_scr, True)
            lcarry[...] = jnp.zeros_like(lcarry)
            gd_ref[...] = jnp.zeros_like(gd_ref)
            gcm_ref[...] = jnp.zeros_like(gcm_ref)
            gbb_ref[...] = jnp.zeros_like(gbb_ref)
            gar_ref[...] = jnp.zeros_like(gar_ref)
            gai_ref[...] = jnp.zeros_like(gai_ref)

        ar, ai = ar_ref[...], ai_ref[...]
        c = st_ref[...]
        sr, si, _ = _s5_states(u_ref, bb_ref, bbp_scr, par_ref, pai_ref, c[:, :SW], c[:, SW:], ncb, bu_scr)
        s_scr[0] = jnp.concatenate([sr, si], axis=1)
        for t in range(NT):
            bu = bu_scr[t]
            sr, si = _cmul_add(ar, ai, sr, si, bu[:, :SW], bu[:, SW:])
            s_scr[t + 1] = jnp.concatenate([sr, si], axis=1)
        dys = [dy_ref[pl.ds(t, ncb, stride=NT), :] for t in range(NT)]
        for t in range(NT):
            gs_scr[t] = _mm(dys[t], cmt_ref[...])
        f = _mm(jnp.concatenate(dys, axis=1), cmp_scr[...])
        xr, xi = _cscan(f[:, :SW], f[:, SW:], par_ref, pai_ref, True)
        lc = lcarry[...]
        lcr, lci = lc[:, :SW], lc[:, SW:]
        hr, hi = _cmul_add(pbr_ref[CH - ncb:CH, :], -pbi_ref[CH - ncb:CH, :], lcr, lci, xr, xi)
        lcarry[...] = jnp.concatenate([hr[0:1, :], hi[0:1, :]], axis=1)
        row = lax.broadcasted_iota(jnp.int32, hr.shape, 0)
        lr_ = jnp.where(row < ncb - 1, pltpu.roll(hr, ncb - 1, 0), lcr)
        li_ = jnp.where(row < ncb - 1, pltpu.roll(hi, ncb - 1, 0), lci)
        gar = jnp.zeros((1, SW), F32)
        gai = jnp.zeros((1, SW), F32)
        for t in reversed(range(NT)):
            gs = gs_scr[t]
            lr_, li_ = _cmul_add(ar, -ai, lr_, li_, gs[:, :SW], gs[:, SW:])
            rows = pl.ds(t, ncb, stride=NT)
            u_t, dy_t = u_ref[rows, :], dy_ref[rows, :]
            lam = jnp.concatenate([lr_, li_], axis=1)
            gbb_ref[...] += _mm_tn(u_t, lam)
            du_ref[rows, :] = _mm_nt(lam, bb_ref[...]) + dy_t * d_ref[...]
            gd_ref[...] += jnp.sum(dy_t * u_t, axis=0, keepdims=True)
            gcm_ref[...] += _mm_tn(s_scr[t + 1], dy_t)
            sp = s_scr[t]
            spr, spi = sp[:, :SW], sp[:, SW:]
            gar += jnp.sum(lr_ * spr + li_ * spi, axis=0, keepdims=True)
            gai += jnp.sum(li_ * spr - lr_ * spi, axis=0, keepdims=True)
        gar_ref[...] += gar
        gai_ref[...] += gai

    tab = pl.BlockSpec((CH, SW), lambda j, i: (0, j))
    stp = pl.BlockSpec((NT, SW), lambda j, i: (0, j))
    colblk = pl.BlockSpec((None, tb, UW), lambda j, i: (j, rev(i), 0))
    vec = lambda w: pl.BlockSpec((1, w), lambda j, i: (0, j))
    return pl.pallas_call(
        body, name="s5_scan_bwd", grid=(NJ, nb),
        in_specs=[colblk, colblk,
                  pl.BlockSpec((None, None, 1, 2 * SW), lambda j, i: (j, rev(i), 0, 0)),
                  pl.BlockSpec((None, UW, 2 * SW), lambda j, i: (j, 0, 0)),
                  pl.BlockSpec((None, 2 * SW, UW), lambda j, i: (j, 0, 0)),
                  pl.BlockSpec((None, UW, 2 * SW), lambda j, i: (j, 0, 0)),
                  vec(SW), vec(SW), stp, stp, tab, tab, tab, tab, vec(UW)] + [ANY_SPEC] * len(deps),
        out_specs=[colblk, vec(UW),
                   pl.BlockSpec((None, 2 * SW, UW), lambda j, i: (j, 0, 0)),
                   pl.BlockSpec((None, UW, 2 * SW), lambda j, i: (j, 0, 0)),
                   vec(SW), vec(SW)],
        out_shape=[_lane_blocks(L), SDS((1, D), F32),
                   SDS((NJ, 2 * SW, UW), F32), SDS((NJ, UW, 2 * SW), F32),
                   SDS((1, NSTATE), F32), SDS((1, NSTATE), F32)],
        scratch_shapes=[pltpu.VMEM((1, 2 * SW), F32), pltpu.VMEM((NT, ncb, 2 * SW), F32),
                        pltpu.VMEM((NT + 1, ncb, 2 * SW), F32), pltpu.VMEM((NT, ncb, 2 * SW), F32),
                        pltpu.VMEM((NT * UW, 2 * SW), MXU), pltpu.VMEM((NT * UW, 2 * SW), MXU)],
        compiler_params=_cp(56, ARB2),
    )(u3, dypre3, states, bb, cm, cmt, abr, abi, pwr, pwi, par, pai, pbr, pbi, dskip, *deps)


def _rope_tables(L, inv):
    tm = min(512, L)

    def body(inv_ref, cos_ref, sin_ref):
        pos = (lax.broadcasted_iota(jnp.int32, (tm, DK // 2), 0) + pl.program_id(0) * tm).astype(F32)
        ang = pos * inv_ref[...]
        cos_ref[...] = jnp.cos(ang)
        sin_ref[...] = jnp.sin(ang)

    blk = pl.BlockSpec((tm, DK // 2), lambda i: (i, 0))
    return pl.pallas_call(body, name="rope_tables", grid=(L // tm,), in_specs=[_full((1, DK // 2))],
                          out_specs=[blk, blk], out_shape=[SDS((L, DK // 2), F32)] * 2)(inv)


def _rot(x, cos, sin):
    x1, x2 = x[:, :DK // 2], x[:, DK // 2:]
    return jnp.concatenate([x1 * cos - x2 * sin, x1 * sin + x2 * cos], axis=1)


def _unrot(d, cos, sin):
    d1, d2 = d[:, :DK // 2], d[:, DK // 2:]
    return jnp.concatenate([d1 * cos + d2 * sin, d2 * cos - d1 * sin], axis=1)


def _ret_decays(h):
    lg = LOG_G[h]
    n = lax.broadcasted_iota(jnp.int32, (CH, CH), 0)
    m = lax.broadcasted_iota(jnp.int32, (CH, CH), 1)
    diff = (n - m).astype(F32)
    decay = jnp.where(n >= m, jnp.exp(lg * jnp.maximum(diff, 0.0)), 0.0)
    idx = lax.broadcasted_iota(jnp.int32, (CH, 1), 0).astype(F32)
    xi = jnp.exp(lg * (idx + 1.0))
    zeta = jnp.exp(lg * (CH - 1.0 - idx))
    return decay, xi, zeta, math.exp(lg * CH)


def _ret_tables(dec_scr, vec_scr):
    for h in range(HEADS):
        decay, xi, zeta, _ = _ret_decays(h)
        dec_scr[h] = decay
        vec_scr[h] = jnp.concatenate([jnp.broadcast_to(xi, (CH, 128)), jnp.broadcast_to(zeta, (CH, 128))], axis=1)


def _ret_chunk_fwd(q, k, v, cos, sin, s_prev_b, decay, xi, zeta):
    qr = _rot(q, cos, sin)
    kr = _rot(k, cos, sin) * (DK ** -0.5)
    scores = _mm_nt(qr, kr) * decay
    o = _mm(scores, v) + _mm(qr * xi, s_prev_b)
    local = _mm_tn(kr * zeta, v)
    mu = jnp.mean(o, axis=-1, keepdims=True)
    oc = o - mu
    rstd = lax.rsqrt(jnp.mean(oc * oc, axis=-1, keepdims=True) + EPS)
    return qr, kr, scores, local, oc * rstd, rstd


def _ret_fwd(p, cos, sin, gain):
    L = p.shape[0]
    nb = L // CH

    def body(q_ref, k_ref, v_ref, bz_ref, cos_ref, sin_ref, g_ref, yb_ref, st_ref, state, dec_scr, vec_scr):
        @pl.when(pl.program_id(0) == 0)
        def _():
            state[...] = jnp.zeros_like(state)
            _ret_tables(dec_scr, vec_scr)

        cos, sin = cos_ref[...], sin_ref[...]
        act, _ = _silu_and_grad(bz_ref[...])
        for h in range(HEADS):
            hs = slice(h * DK, (h + 1) * DK)
            xi, zeta = vec_scr[h, :, 0:1], vec_scr[h, :, 128:129]
            s_prev = state[h]
            s_prev_b = s_prev.astype(MXU)
            st_ref[h] = s_prev_b
            _, _, _, local, on, _ = _ret_chunk_fwd(q_ref[:, hs], k_ref[:, hs], v_ref[:, hs], cos, sin,
                                                   s_prev_b, dec_scr[h], xi, zeta)
            state[h] = s_prev * math.exp(LOG_G[h] * CH) + local
            yb_ref[:, hs] = (on * g_ref[:, hs] * act[:, hs]).astype(yb_ref.dtype)

    col = lambda c: pl.BlockSpec((CH, D), lambda i: (i, c))
    rope = pl.BlockSpec((CH, DK // 2), lambda i: (i, 0))
    return pl.pallas_call(
        body, name="ret_fwd", grid=(nb,),
        in_specs=[col(2), col(3), col(4), col(5), rope, rope, _full((1, D))],
        out_specs=[pl.BlockSpec((CH, D), lambda i: (i, 0)),
                   pl.BlockSpec((None, HEADS, DK, DK), lambda i: (i, 0, 0, 0))],
        out_shape=[SDS((L, D), MXU), SDS((nb, HEADS, DK, DK), MXU)],
        scratch_shapes=[pltpu.VMEM((HEADS, DK, DK), F32), pltpu.VMEM((HEADS, CH, CH), F32),
                        pltpu.VMEM((HEADS, CH, 256), F32)],
        compiler_params=_cp(40, ("arbitrary",)),
    )(p, p, p, p, cos, sin, gain)


def _ret_bwd(p, cos, sin, gain, states, dx1, wout_e, du, daz, deps=NO_DEPS):
    L = p.shape[0]
    nb = L // CH
    rev = lambda i: nb - 1 - i

    def body(q_ref, k_ref, v_ref, bz_ref, cos_ref, sin_ref, g_ref, st_ref, dx1_ref, wo_ref, du_ref, daz_ref, *rest):
        dp_ref, yb_ref, gg_ref, gstate, dec_scr, vec_scr = rest[len(deps):]

        @pl.when(pl.program_id(0) == 0)
        def _():
            gstate[...] = jnp.zeros_like(gstate)
            gg_ref[...] = jnp.zeros_like(gg_ref)
            _ret_tables(dec_scr, vec_scr)

        cos, sin = cos_ref[...], sin_ref[...]
        act, dact = _silu_and_grad(bz_ref[...])
        dyb = _mm_nt(dx1_ref[...], wo_ref[...])
        dp_ref[:, 0:D] = _from_lane_blocks(du_ref).astype(dp_ref.dtype)
        dp_ref[:, D:2 * D] = daz_ref[...]
        for h in range(HEADS):
            hs = slice(h * DK, (h + 1) * DK)
            col = lambda part: slice((2 + part) * D + h * DK, (2 + part) * D + (h + 1) * DK)
            decay = dec_scr[h]
            xi, zeta = vec_scr[h, :, 0:1], vec_scr[h, :, 128:129]
            v = v_ref[:, hs]
            s_prev_b = st_ref[h]
            qr, kr, scores, _, on, rstd = _ret_chunk_fwd(q_ref[:, hs], k_ref[:, hs], v, cos, sin, s_prev_b,
                                                         decay, xi, zeta)
            gain_h = g_ref[:, hs]
            out = on * gain_h
            yb_ref[:, hs] = (out * act[:, hs]).astype(yb_ref.dtype)
            dyb_h = dyb[:, hs]
            dp_ref[:, col(3)] = (dyb_h * out * dact[:, hs]).astype(dp_ref.dtype)
            dout = dyb_h * act[:, hs]
            gg_ref[:, hs] += jnp.sum(dout * on, axis=0, keepdims=True)
            don = dout * gain_h
            do = rstd * (don - jnp.mean(don, axis=-1, keepdims=True)
                         - on * jnp.mean(don * on, axis=-1, keepdims=True))
            gnext = gstate[h]
            gnext_b = gnext.astype(MXU)
            dscores = _mm_nt(do, v) * decay
            dp_ref[:, col(2)] = (_mm_tn(scores, do) + _mm(kr * zeta, gnext_b)).astype(dp_ref.dtype)
            dqr = _mm(dscores, kr) + _mm_nt(do, s_prev_b) * xi
            dkr = _mm_tn(dscores, qr) + _mm_nt(v, gnext_b) * zeta
            gstate[h] = gnext * math.exp(LOG_G[h] * CH) + _mm_tn(qr * xi, do)
            dp_ref[:, col(0)] = _unrot(dqr, cos, sin).astype(dp_ref.dtype)
            dp_ref[:, col(1)] = (_unrot(dkr, cos, sin) * (DK ** -0.5)).astype(dp_ref.dtype)

    col = lambda c: pl.BlockSpec((CH, D), lambda i: (rev(i), c))
    rope = pl.BlockSpec((CH, DK // 2), lambda i: (rev(i), 0))
    outc = col(0)
    act_out = SDS((L, D), MXU)
    return pl.pallas_call(
        body, name="ret_bwd", grid=(nb,),
        in_specs=[col(2), col(3), col(4), col(5), rope, rope, _full((1, D)),
                  pl.BlockSpec((None, HEADS, DK, DK), lambda i: (rev(i), 0, 0, 0)),
                  outc, pl.BlockSpec((D, D), lambda i: (1, 0)), _lane_block_spec(CH, rev), outc]
        + [ANY_SPEC] * len(deps),
        out_specs=[pl.BlockSpec((CH, NIN), lambda i: (rev(i), 0)), outc, _full((1, D))],
        out_shape=[SDS((L, NIN), MXU), act_out, SDS((1, D), F32)],
        scratch_shapes=[pltpu.VMEM((HEADS, DK, DK), F32), pltpu.VMEM((HEADS, CH, CH), F32),
                        pltpu.VMEM((HEADS, CH, 256), F32)],
        compiler_params=_cp(48, ("arbitrary",)),
    )(p, p, p, p, cos, sin, gain, states, dx1, wout_e, du, daz, *deps)


def _out_even(x, ypre3, p, yb, wglu, bglu, wout):
    L = x.shape[0]
    tm = min(512, L)

    def body(x_ref, y_ref, az_ref, yb_ref, wg_ref, bg_ref, w_ref, o_ref):
        yg = _gelu(_from_lane_blocks(y_ref))
        t = _mm(yg, wg_ref[...]) + bg_ref[...]
        act, _ = _silu_and_grad(az_ref[...])
        ya = (yg * jax.nn.sigmoid(t) * act).astype(MXU)
        cat = jnp.concatenate([ya, yb_ref[...]], axis=1)
        o_ref[...] = x_ref[...] + jnp.dot(cat, w_ref[...], preferred_element_type=F32)

    row = pl.BlockSpec((tm, D), lambda i: (i, 0))
    return pl.pallas_call(
        body, name="out_even", grid=(L // tm,),
        in_specs=[row, _lane_block_spec(tm, lambda i: i), pl.BlockSpec((tm, D), lambda i: (i, 1)), row,
                  _full((D, D)), _full((1, D)), _full((DI, D))],
        out_specs=row, out_shape=SDS((L, D), F32), compiler_params=_cp(48, ("arbitrary",)),
    )(x, ypre3, p, yb, wglu, bglu, wout)


def _sgu_core(pv, gain, ws_ref, bs_ref):
    pu, pvv, z = pv[:, :DI], pv[:, DI:2 * DI], pv[:, 2 * DI:]
    u, gu = _gelu_and_grad(pu)
    v, gv = _gelu_and_grad(pvv)
    mu = jnp.mean(v, axis=-1, keepdims=True)
    vc = v - mu
    rstd = lax.rsqrt(jnp.mean(vc * vc, axis=-1, keepdims=True) + EPS)
    vhat = vc * rstd
    vn = vhat * gain
    t = lax.broadcasted_iota(jnp.int32, (CH, CH), 0)
    s_ = lax.broadcasted_iota(jnp.int32, (CH, CH), 1)
    mask = t >= s_
    wm = [jnp.where(mask, ws_ref[g], 0.0).astype(MXU) for g in range(SG)]
    s = jnp.concatenate([_mm(wm[g], vn[:, g * SGD:(g + 1) * SGD]) + bs_ref[g] for g in range(SG)], axis=1)
    return gu, gv, z, u, vhat, rstd, vn, mask, wm, s


def _sgu_fwd_bwd(p2, x1, gain, wsp, bsp, wout, fnorm, tgt):
    L = p2.shape[0]

    def body(p_ref, x1_ref, g_ref, ws_ref, bs_ref, wo_ref, fn_ref, t_ref,
             dp_ref, y_ref, dx2_ref, gg_ref, gws_ref, gbs_ref, gfn_ref, loss_ref):
        @pl.when(pl.program_id(0) == 0)
        def _():
            gg_ref[...] = jnp.zeros_like(gg_ref)
            gws_ref[...] = jnp.zeros_like(gws_ref)
            gbs_ref[...] = jnp.zeros_like(gbs_ref)
            gfn_ref[...] = jnp.zeros_like(gfn_ref)
            loss_ref[...] = jnp.zeros_like(loss_ref)

        gain = g_ref[...]
        gu, gv, z, u, vhat, rstd, vn, mask, wm, s = _sgu_core(p_ref[...], gain, ws_ref, bs_ref)
        act, dact = _silu_and_grad(z)
        y = (u * s * act).astype(MXU)
        y_ref[...] = y
        x2 = x1_ref[...] + jnp.dot(y, wo_ref[...], preferred_element_type=F32)
        xhat, r = _rms(x2)
        fn = fn_ref[...]
        e = xhat * fn - t_ref[...]
        loss_ref[...] += 0.5 * jnp.sum(jnp.mean(e * e, axis=-1, keepdims=True), axis=0, keepdims=True)
        do = e * (1.0 / D)
        gfn_ref[...] += jnp.sum(do * xhat, axis=0, keepdims=True)
        dxhat = do * fn
        dx2 = r * (dxhat - xhat * jnp.mean(dxhat * xhat, axis=-1, keepdims=True))
        dx2_ref[...] = dx2
        dy = _mm_nt(dx2, wo_ref[...])
        du = dy * s * act
        ds = dy * u * act
        dz = dy * u * s * dact
        dvn = []
        for g in range(SG):
            ds_g = ds[:, g * SGD:(g + 1) * SGD]
            vn_g = vn[:, g * SGD:(g + 1) * SGD]
            gbs_ref[g] += jnp.sum(ds_g, axis=1, keepdims=True)
            gws_ref[g] += jnp.where(mask, _mm_nt(ds_g, vn_g), 0.0)
            dvn.append(_mm_tn(wm[g], ds_g))
        dvn = jnp.concatenate(dvn, axis=1)
        gg_ref[...] += jnp.sum(dvn * vhat, axis=0, keepdims=True)
        dvhat = dvn * gain
        dv = rstd * (dvhat - jnp.mean(dvhat, axis=-1, keepdims=True)
                     - vhat * jnp.mean(dvhat * vhat, axis=-1, keepdims=True))
        dp_ref[...] = jnp.concatenate([du * gu, dv * gv, dz], axis=1).astype(dp_ref.dtype)

    row = pl.BlockSpec((CH, D), lambda i: (i, 0))
    wide = pl.BlockSpec((CH, NIN), lambda i: (i, 0))
    return pl.pallas_call(
        body, name="sgu_fwd_bwd", grid=(L // CH,),
        in_specs=[wide, row, _full((1, DI)), _full((SG, CH, CH)), _full((SG, CH, 1)), _full((DI, D)),
                  _full((1, D)), row],
        out_specs=[wide, pl.BlockSpec((CH, DI), lambda i: (i, 0)), row,
                   _full((1, DI)), _full((SG, CH, CH)), _full((SG, CH, 1)), _full((1, D)), _full((1, 128))],
        out_shape=[SDS((L, NIN), MXU), SDS((L, DI), MXU), SDS((L, D), F32), SDS((1, DI), F32),
                   SDS((SG, CH, CH), F32), SDS((SG, CH, 1), F32), SDS((1, D), F32), SDS((1, 128), F32)],
        compiler_params=_cp(48, ("arbitrary",)),
    )(p2, x1, gain, wsp, bsp, wout, fnorm, tgt)


def _my_index():
    return 4 * lax.axis_index("x") + 2 * lax.axis_index("y") + lax.axis_index("c")


def _ordered_sum(land_ref, own, me):
    g = None
    for s in range(NDEV):
        part = jnp.where(me == s, own, land_ref[s].astype(F32))
        g = part if g is None else g + part
    return g


def _adamw_math(w, m, v, g):
    mn = ADAM_B1 * m + (1.0 - ADAM_B1) * g
    vn = ADAM_B2 * v + (1.0 - ADAM_B2) * (g * g)
    mhat = mn / BC1
    vhat = vn / BC2
    return g, -ADAM_LR * (mhat / (jnp.sqrt(vhat) + ADAM_EPS) + ADAM_WD * w), mn, vn


def _adamw(w, m, v, land, own, name):
    R, C = w.shape
    tr = R
    for cand in (256, 128, 64, 32, 16, 8):
        if R % cand == 0 and R > cand:
            tr = cand
            break

    def body(w_ref, m_ref, v_ref, land_ref, own_ref, g_ref, d_ref, mo_ref, vo_ref):
        g = _ordered_sum(land_ref, own_ref[...].astype(F32), _my_index())
        for o, val in zip((g_ref, d_ref, mo_ref, vo_ref), _adamw_math(w_ref[...], m_ref[...], v_ref[...], g)):
            o[...] = val

    blk = pl.BlockSpec((tr, C), lambda i: (i, 0))
    out = SDS((R, C), F32)
    return pl.pallas_call(
        body, name=name, grid=(R // tr,),
        in_specs=[blk, blk, blk, pl.BlockSpec((NDEV, tr, C), lambda i: (0, i, 0)), blk],
        out_specs=[blk, blk, blk, blk], out_shape=[out, out, out, out],
        compiler_params=_cp(40, ("arbitrary",)),
    )(w, m, v, land, own)


def _adamw_many(ws, ms, vs, lands, owns, name):
    k = len(ws)

    def body(*refs):
        ins, outs = refs[:5 * k], refs[5 * k:]
        me = _my_index()
        for i in range(k):
            w_ref, m_ref, v_ref, land_ref, own_ref = (ins[j * k + i] for j in range(5))
            g = _ordered_sum(land_ref, own_ref[...], me)
            for j, val in enumerate(_adamw_math(w_ref[...], m_ref[...], v_ref[...], g)):
                outs[j * k + i][...] = val

    out_shape = [SDS(w.shape, F32) for _ in range(4) for w in ws]
    res = pl.pallas_call(body, name=name, out_shape=out_shape, compiler_params=_cp(60))(*ws, *ms, *vs, *lands, *owns)
    return [res[j * k:(j + 1) * k] for j in range(4)]


MESH = pl.DeviceIdType.MESH
HBM_SPEC = pl.BlockSpec(memory_space=pltpu.HBM)
SEM_SPEC = pl.BlockSpec(memory_space=pltpu.SEMAPHORE)
EFFECT = pltpu.SideEffectType.DATAFLOW_SIDE_EFFECTING


def _me_and_peers():
    x, y, c = lax.axis_index("x"), lax.axis_index("y"), lax.axis_index("c")
    me = 4 * x + 2 * y + c
    peers = []
    for r in range(1, NDEV):
        px, py, pc = x ^ ((r >> 2) & 1), y ^ ((r >> 1) & 1), c ^ (r & 1)
        peers.append(((px, py, pc), 4 * px + 2 * py + pc))
    return me, peers


def _land_shape(a, scatter):
    return (NDEV,) + (a.shape[1:] if scatter else a.shape)


def _remote(src, dst, send_sems, recv_sems, r, k, n, dev):
    i = r * n + k
    return pltpu.make_async_remote_copy(src_ref=src, dst_ref=dst, send_sem=send_sems.at[i], recv_sem=recv_sems.at[i],
                                        device_id=dev, device_id_type=MESH)


def _exchange(arrays, scatter, name):
    n = len(arrays)
    out_shape = [SDS(_land_shape(a, scatter), a.dtype) for a in arrays]

    def body(*refs):
        ins, outs = refs[:n], refs[n:2 * n]
        send_sems, recv_sems, loc_sems = refs[2 * n:]
        me, peers = _me_and_peers()
        local = []
        for k in range(n):
            src = ins[k].at[me] if scatter else ins[k]
            cp = pltpu.make_async_copy(src, outs[k].at[me], loc_sems.at[k])
            cp.start()
            local.append(cp)
        sends = []
        for r, (dev, lin) in enumerate(peers):
            for k in range(n):
                src = ins[k].at[lin] if scatter else ins[k]
                cp = _remote(src, outs[k].at[me], send_sems, recv_sems, r, k, n, dev)
                cp.start()
                sends.append(cp)
        for r, (dev, lin) in enumerate(peers):
            for k in range(n):
                src = ins[k].at[me] if scatter else ins[k]
                _remote(src, outs[k].at[lin], send_sems, recv_sems, r, k, n, dev).wait_recv()
        for cp in sends:
            cp.wait_send()
        for cp in local:
            cp.wait()

    return pl.pallas_call(
        body, name=name, in_specs=[HBM_SPEC] * n, out_specs=[HBM_SPEC] * n, out_shape=out_shape,
        scratch_shapes=[pltpu.SemaphoreType.DMA(((NDEV - 1) * n,)), pltpu.SemaphoreType.DMA(((NDEV - 1) * n,)),
                        pltpu.SemaphoreType.DMA((n,))],
    )(*arrays)


def _exchange_start(arrays, scatter, name):
    n = len(arrays)
    lands = [lax.empty(_land_shape(a, scatter), a.dtype) for a in arrays]

    def body(*refs):
        ins, lnd = refs[:n], refs[n:2 * n]
        send_sems, recv_sems, own_sems = refs[2 * n:2 * n + 3]
        token = refs[-1]
        me, peers = _me_and_peers()
        for r, (dev, lin) in enumerate(peers):
            for k in range(n):
                src = ins[k].at[lin] if scatter else ins[k]
                _remote(src, lnd[k].at[me], send_sems, recv_sems, r, k, n, dev).start()
        if not scatter:
            for k in range(n):
                pltpu.make_async_copy(ins[k], lnd[k].at[me], own_sems.at[k]).start()
        token[...] = jnp.zeros_like(token)

    sem = pltpu.SemaphoreType.DMA(((NDEV - 1) * n,))
    outs = pl.pallas_call(
        body, name=name,
        out_shape=(sem, sem, pltpu.SemaphoreType.DMA((n,)), *[pltpu.HBM(a.shape, a.dtype) for a in arrays],
                   *[pltpu.HBM(l.shape, l.dtype) for l in lands], SDS((8, 128), F32)),
        in_specs=[HBM_SPEC] * (2 * n),
        out_specs=(SEM_SPEC, SEM_SPEC, SEM_SPEC, *[HBM_SPEC] * (2 * n), pl.BlockSpec(memory_space=pltpu.VMEM)),
        input_output_aliases={k: 3 + k for k in range(2 * n)},
        compiler_params=pltpu.CompilerParams(has_side_effects=EFFECT),
    )(*[pltpu.with_memory_space_constraint(a, pltpu.HBM) for a in arrays],
      *[pltpu.with_memory_space_constraint(l, pltpu.HBM) for l in lands])
    return (n, scatter, outs[0], outs[1], outs[2], outs[3:3 + n], outs[3 + n:3 + 2 * n]), outs[-1]


def _exchange_wait(handle, after, name):
    n, scatter, send_sems, recv_sems, own_sems, thru, lands = handle
    after = tuple(after)

    def body(*refs):
        ins, lnd = refs[:n], refs[n:2 * n]
        send_sems, recv_sems, own_sems = refs[2 * n:2 * n + 3]
        me, peers = _me_and_peers()
        for r, (dev, lin) in enumerate(peers):
            for k in range(n):
                src = ins[k].at[lin] if scatter else ins[k]
                cp = _remote(src, lnd[k].at[lin], send_sems, recv_sems, r, k, n, dev)
                cp.wait_send()
                cp.wait_recv()
        if not scatter:
            for k in range(n):
                pltpu.make_async_copy(ins[k], lnd[k].at[me], own_sems.at[k]).wait()

    outs = pl.pallas_call(
        body, name=name,
        out_shape=(*[pltpu.HBM(a.shape, a.dtype) for a in thru], *[pltpu.HBM(l.shape, l.dtype) for l in lands]),
        in_specs=[HBM_SPEC] * (2 * n) + [SEM_SPEC, SEM_SPEC, SEM_SPEC] + [ANY_SPEC] * len(after),
        out_specs=tuple([HBM_SPEC] * (2 * n)),
        input_output_aliases={k: k for k in range(2 * n)},
        compiler_params=pltpu.CompilerParams(has_side_effects=EFFECT),
    )(*thru, *lands, send_sems, recv_sems, own_sems, *after)
    return list(outs[:n]), list(outs[n:])


CHIP_RELATIONS = (2, 4, 6)


def _peer(r):
    x, y, c = lax.axis_index("x"), lax.axis_index("y"), lax.axis_index("c")
    px, py, pc = x ^ ((r >> 2) & 1), y ^ ((r >> 1) & 1), c ^ (r & 1)
    return (px, py, pc), 4 * px + 2 * py + pc


def _copy(src, dst, send_sems, recv_sems, i, dev):
    return pltpu.make_async_remote_copy(src_ref=src, dst_ref=dst, send_sem=send_sems.at[i], recv_sem=recv_sems.at[i],
                                        device_id=dev, device_id_type=MESH)


def _gather2_start(a, name):
    land = lax.empty((NDEV,) + a.shape, a.dtype)

    def body(own, lnd, send_sems, recv_sems, own_sem, own_thru, lnd_thru, token):
        me = _my_index()
        for i, r in enumerate((1,) + CHIP_RELATIONS):
            dev, _ = _peer(r)
            _copy(own, lnd.at[me], send_sems, recv_sems, i, dev).start()
        pltpu.make_async_copy(own, lnd.at[me], own_sem.at[0]).start()
        token[...] = jnp.zeros_like(token)

    sem4 = pltpu.SemaphoreType.DMA((4,))
    outs = pl.pallas_call(
        body, name=name,
        out_shape=(sem4, sem4, pltpu.SemaphoreType.DMA((1,)), pltpu.HBM(a.shape, a.dtype),
                   pltpu.HBM(land.shape, land.dtype), SDS((8, 128), F32)),
        in_specs=[HBM_SPEC, HBM_SPEC],
        out_specs=(SEM_SPEC, SEM_SPEC, SEM_SPEC, HBM_SPEC, HBM_SPEC, pl.BlockSpec(memory_space=pltpu.VMEM)),
        input_output_aliases={0: 3, 1: 4},
        compiler_params=pltpu.CompilerParams(has_side_effects=EFFECT),
    )(pltpu.with_memory_space_constraint(a, pltpu.HBM), pltpu.with_memory_space_constraint(land, pltpu.HBM))
    return outs[:5], outs[5]


def _gather2_forward(handle, after, name):
    send_sems, recv_sems, own_sem, own, land = handle
    after = tuple(after)

    def body(lnd, recv_sems, *rest):
        send2, recv2, lnd_thru = rest[len(after):]
        sib, _ = _peer(1)
        for k, r in enumerate(CHIP_RELATIONS):
            dev, lin = _peer(r)
            _copy(lnd.at[lin], lnd.at[lin], recv_sems, recv_sems, 1 + k, dev).wait_recv()
            _copy(lnd.at[lin], lnd.at[lin], send2, recv2, k, sib).start()

    sem3 = pltpu.SemaphoreType.DMA((3,))
    send2, recv2, land = pl.pallas_call(
        body, name=name,
        out_shape=(sem3, sem3, pltpu.HBM(land.shape, land.dtype)),
        in_specs=[HBM_SPEC, SEM_SPEC] + [ANY_SPEC] * len(after),
        out_specs=(SEM_SPEC, SEM_SPEC, HBM_SPEC),
        input_output_aliases={0: 2},
        compiler_params=pltpu.CompilerParams(has_side_effects=EFFECT),
    )(land, recv_sems, *after)
    return send_sems, recv_sems, own_sem, send2, recv2, own, land


def _gather2_wait(handle, name):
    send_sems, recv_sems, own_sem, send2, recv2, own, land = handle

    def body(own_ref, lnd, send_sems, recv_sems, own_sem, send2, recv2, own_thru, lnd_thru):
        me = _my_index()
        sib, sib_lin = _peer(1)
        for i, r in enumerate((1,) + CHIP_RELATIONS):
            dev, _ = _peer(r)
            _copy(own_ref, lnd.at[me], send_sems, recv_sems, i, dev).wait_send()
        _copy(own_ref, lnd.at[sib_lin], send_sems, recv_sems, 0, sib).wait_recv()
        for k, r in enumerate(CHIP_RELATIONS):
            _, lin = _peer(r)
            _, lin_other = _peer(r ^ 1)
            _copy(lnd.at[lin], lnd.at[lin], send2, recv2, k, sib).wait_send()
            _copy(lnd.at[lin_other], lnd.at[lin_other], send2, recv2, k, sib).wait_recv()
        pltpu.make_async_copy(own_ref, lnd.at[me], own_sem.at[0]).wait()

    outs = pl.pallas_call(
        body, name=name,
        out_shape=(pltpu.HBM(own.shape, own.dtype), pltpu.HBM(land.shape, land.dtype)),
        in_specs=[HBM_SPEC, HBM_SPEC] + [SEM_SPEC] * 5,
        out_specs=(HBM_SPEC, HBM_SPEC),
        input_output_aliases={0: 0, 1: 1},
        compiler_params=pltpu.CompilerParams(has_side_effects=EFFECT),
    )(own, land, send_sems, recv_sems, own_sem, send2, recv2)
    return outs[1]


def _local_step(x, tgt, norm_even, first_weight, lam_re, lam_im, log_dt, b_re, b_im, c_re, c_im, s5_d, bglu,
                ret_gain, wsp, bsp, fnorm, late_weights, emit, start_token=None):
    L = x.shape[0]
    lr3, li3 = lam_re.reshape(G, 1, P), lam_im.reshape(G, 1, P)
    dt3 = log_dt.reshape(G, 1, 1)
    br3, bi3 = jnp.swapaxes(b_re, 1, 2), jnp.swapaxes(b_im, 1, 2)
    abr3, abi3, bbr3, bbi3 = _s5_disc(lr3, li3, dt3, br3, bi3)
    bb = jnp.concatenate([_embed(bbr3), _embed(bbi3)], axis=2).astype(MXU)
    cm = jnp.concatenate([_embed(jnp.swapaxes(c_re, 1, 2)), -_embed(jnp.swapaxes(c_im, 1, 2))], axis=1).astype(MXU)
    abr, abi = abr3.reshape(1, NSTATE), abi3.reshape(1, NSTATE)
    pwr, pwi, _, _ = _s5_tables(abr, abi, NT, "s5_tables_step")
    par, pai, pbr, pbi = _s5_tables(pwr[NT - 1:NT], pwi[NT - 1:NT], CH, "s5_tables_chunk")
    inv = (ROPE_BASE ** (-jnp.arange(DK // 2, dtype=F32) / (DK // 2))).reshape(1, DK // 2)
    cos, sin = _rope_tables(L, inv)
    bsp3 = bsp.reshape(SG, CH, 1)

    def dep(token):
        return NO_DEPS if token is None else (token,)

    win_e = first_weight((cos, pbi, cm))
    p, u3 = _in_proj(x, norm_even, win_e, "in_even", dep(start_token), lane_blocks=True)
    ypre, s5_states = _s5_scan_fwd(u3, bb, cm, pwr, pwi, pwr.T, pwi.T, par, pai, s5_d)
    yb, ret_states = _ret_fwd(p, cos, sin, ret_gain)
    wglu, wout_e, norm_odd, win_o, sgu_gain, wout_o = late_weights((ypre, yb))
    x1 = _out_even(x, ypre, p, yb, wglu, bglu, wout_e)
    p2 = _in_proj(x1, norm_odd, win_o, "in_odd")
    dp2, y_o, dx2, g_sgu_gain, g_wsp, g_bsp, g_fnorm, loss = _sgu_fwd_bwd(
        p2, x1, sgu_gain, wsp, bsp3, wout_o, fnorm, tgt)

    g_wout_o = _wgrad_rows([y_o], dx2, "wgrad_out_odd")
    g_win_o = _wgrad_cols(x1, norm_odd, dp2, "wgrad_in_odd")
    tok = emit("odd", dict(w_in_odd=g_win_o, w_out_odd=g_wout_o))
    dx1, g_norm_odd = _in_proj_bwd_x(dp2, x1, norm_odd, win_o, dx2, "in_odd_bwd", dep(tok))
    tok = emit("small_odd", dict(norm_odd=g_norm_odd, sgu_norm_gain=g_sgu_gain, sgu_w_spatial=g_wsp,
                                 sgu_b_spatial=g_bsp.reshape(SG, CH), final_norm=g_fnorm))

    dypre, daz, yg, dt, ya2, g_bglu = _s5_gate_bwd(ypre, p, dx1, wout_e, wglu, bglu, dep(tok))
    g_wglu = _wgrad_rows([yg], dt, "wgrad_glu")
    tok = emit("glu", dict(s5_w_glu=g_wglu))
    du, g_d, g_cm, g_bb, g_ar, g_ai = _s5_scan_bwd(u3, dypre, s5_states, bb, cm, jnp.swapaxes(cm, 1, 2), abr, abi,
                                                   pwr, pwi, par, pai, pbr, pbi, s5_d, dep(tok))
    dbbr3 = _diag_blocks(g_bb[:, :, :SW], HG, P)
    dbbi3 = _diag_blocks(g_bb[:, :, SW:], HG, P)
    g_c_re = jnp.swapaxes(_diag_blocks(g_cm[:, :SW, :], P, HG), 1, 2)
    g_c_im = -jnp.swapaxes(_diag_blocks(g_cm[:, SW:, :], P, HG), 1, 2)
    g_lr3, g_li3, g_dt3, g_br3, g_bi3 = _s5_disc_bwd(
        lr3, li3, dt3, br3, bi3, g_ar.reshape(G, 1, P), g_ai.reshape(G, 1, P), dbbr3, dbbi3)
    tok = emit("small_s5", dict(
        s5_lam_re=g_lr3.reshape(G, P), s5_lam_im=g_li3.reshape(G, P), s5_log_dt=g_dt3.reshape(1, G),
        s5_b_re=g_br3, s5_b_im=g_bi3, s5_c_re=g_c_re, s5_c_im=g_c_im, s5_d=g_d, s5_b_glu=g_bglu))
    dp, yb2, g_ret_gain = _ret_bwd(p, cos, sin, ret_gain, ret_states, dx1, wout_e, du, daz, dep(tok))
    g_win_e = _wgrad_cols(x, norm_even, dp, "wgrad_in_even")
    tok = emit("even_cols", dict(w_in_even=g_win_e))
    g_wout_e = _wgrad_rows([ya2, yb2], dx1, "wgrad_out_even", dep(tok))
    tok = emit("even_rows", dict(w_out_even=g_wout_e))
    dx, g_norm_even = _in_proj_bwd_x(dp, x, norm_even, win_e, dx1, "in_even_bwd", dep(tok))
    emit("last", dict(ret_gn_gain=g_ret_gain, norm_even=g_norm_even, loss=loss))
    return dx


WEIGHTS = ['norm_even', 'w_in_even', 's5_lam_re', 's5_lam_im', 's5_log_dt', 's5_b_re', 's5_b_im', 's5_c_re',
           's5_c_im', 's5_d', 's5_w_glu', 's5_b_glu', 'ret_gn_gain', 'w_out_even', 'norm_odd', 'w_in_odd',
           'sgu_norm_gain', 'sgu_w_spatial', 'sgu_b_spatial', 'w_out_odd', 'final_norm']
BIG = ['w_in_even', 's5_w_glu', 'w_out_even', 'w_in_odd', 'w_out_odd']
SHARDED_SMALL = {'norm_odd': D // NDEV, 'sgu_norm_gain': DI // NDEV}
SCATTER_STAGES = ("odd", "glu", "even_cols", "even_rows")
GATHER_STAGES = ("small_odd", "small_s5")


def _view(n, a):
    if n in ('s5_b_re', 's5_b_im'):
        return jnp.swapaxes(a[0], 1, 2)
    if n == 'final_norm':
        return a.reshape(1, D)
    return a[0] if a.ndim >= 3 else a


def _unview(n, t, shape):
    if n in ('s5_b_re', 's5_b_im'):
        return jnp.swapaxes(t, 1, 2)[None]
    return t.reshape(shape)


def kernel(x, norm_even, w_in_even, s5_lam_re, s5_lam_im, s5_log_dt, s5_b_re, s5_b_im, s5_c_re, s5_c_im, s5_d, s5_w_glu, s5_b_glu, ret_gn_gain, w_out_even, norm_odd, w_in_odd, sgu_norm_gain, sgu_w_spatial, sgu_b_spatial, w_out_odd, final_norm, loss_target, m_norm_even, m_w_in_even, m_s5_lam_re, m_s5_lam_im, m_s5_log_dt, m_s5_b_re, m_s5_b_im, m_s5_c_re, m_s5_c_im, m_s5_d, m_s5_w_glu, m_s5_b_glu, m_ret_gn_gain, m_w_out_even, m_norm_odd, m_w_in_odd, m_sgu_norm_gain, m_sgu_w_spatial, m_sgu_b_spatial, m_w_out_odd, m_final_norm, v_norm_even, v_w_in_even, v_s5_lam_re, v_s5_lam_im, v_s5_log_dt, v_s5_b_re, v_s5_b_im, v_s5_c_re, v_s5_c_im, v_s5_d, v_s5_w_glu, v_s5_b_glu, v_ret_gn_gain, v_w_out_even, v_norm_odd, v_w_in_odd, v_sgu_norm_gain, v_sgu_w_spatial, v_sgu_b_spatial, v_w_out_odd, v_final_norm):
    args = dict(locals())
    w = {n: args[n] for n in WEIGHTS}
    m = {n: args["m_" + n] for n in WEIGHTS}
    v = {n: args["v_" + n] for n in WEIGHTS}
    me = _my_index()

    first_handle, _ = _gather2_start(w['w_in_even'][0].astype(MXU), "gather_first_start")

    def first_weight(after):
        return _gather2_wait(_gather2_forward(first_handle, after, "gather_first_forward"), "gather_first_wait")

    late_own = [w['s5_w_glu'][0].astype(MXU), w['w_out_even'][0].astype(MXU), w['norm_odd'],
                w['w_in_odd'][0].astype(MXU), w['sgu_norm_gain'], w['w_out_odd'][0].astype(MXU)]
    late_handle, start_token = _exchange_start(late_own, False, "gather_late_start")

    def late_weights(after):
        _, (wglu, wout_e, nodd, win_o, sgug, wout_o) = _exchange_wait(late_handle, after, "gather_late_wait")
        return (wglu.reshape(D, D), wout_e.reshape(DI, D), nodd.reshape(1, D), win_o, sgug.reshape(1, DI),
                wout_o.reshape(DI, D))

    pending = {}
    small_last = {}

    def emit(stage, grads):
        if stage == "last":
            small_last.update(grads)
            return None
        names = list(grads)
        handle, token = _exchange_start([grads[n] for n in names], stage in SCATTER_STAGES, stage + "_start")
        pending[stage] = (handle, names)
        return token

    dx = _local_step(
        x[0], loss_target[0], w['norm_even'], first_weight, w['s5_lam_re'][0], w['s5_lam_im'][0], w['s5_log_dt'][0],
        w['s5_b_re'][0], w['s5_b_im'][0], w['s5_c_re'][0], w['s5_c_im'][0], w['s5_d'], w['s5_b_glu'],
        w['ret_gn_gain'], w['sgu_w_spatial'][0], w['sgu_b_spatial'][0], w['final_norm'].reshape(1, D),
        late_weights, emit, start_token)

    out_g, out_d, out_m, out_v = {}, {}, {}, {}
    after = dx
    for stage in SCATTER_STAGES:
        handle, names = pending[stage]
        sent, lands = _exchange_wait(handle, (after,), stage + "_wait")
        for n, land, stack in zip(names, lands, sent):
            shp = w[n].shape
            r, c = shp[1], shp[2]
            own = lax.dynamic_index_in_dim(stack, me, 0, keepdims=False)
            res = _adamw(w[n].reshape(r, c), m[n].reshape(r, c), v[n].reshape(r, c), land, own, "adamw_" + n)
            out_g[n], out_d[n], out_m[n], out_v[n] = (t.reshape(shp) for t in res)
            after = res[0]

    names, owns, lands = [], [], []
    for stage in GATHER_STAGES:
        handle, stage_names = pending[stage]
        sent, got = _exchange_wait(handle, (after,), stage + "_wait")
        names, owns, lands = names + stage_names, owns + sent, lands + got
    last_names = list(small_last)
    last = _exchange([small_last[n] for n in last_names], False, "gather_last")
    loss_parts = last[last_names.index("loss")][:, 0, 0]
    for n, own, land in zip(last_names, (small_last[n] for n in last_names), last):
        if n != "loss":
            names, owns, lands = names + [n], owns + [own], lands + [land]
    for i, n in enumerate(names):
        if n in SHARDED_SMALL:
            width = SHARDED_SMALL[n]
            owns[i] = lax.dynamic_slice_in_dim(owns[i], me * width, width, axis=1)
            lands[i] = lax.dynamic_slice_in_dim(lands[i], me * width, width, axis=2)
    res = _adamw_many([_view(n, w[n]) for n in names], [_view(n, m[n]) for n in names],
                      [_view(n, v[n]) for n in names], lands, owns, "adamw_small")
    for dst, vals in zip((out_g, out_d, out_m, out_v), res):
        for n, t in zip(names, vals):
            dst[n] = _unview(n, t, w[n].shape)

    loss_total = jnp.sum(loss_parts)
    return (loss_total, dx[None], *[out_g[n] for n in WEIGHTS], *[out_d[n] for n in WEIGHTS],
            *[out_m[n] for n in WEIGHTS], *[out_v[n] for n in WEIGHTS])
```

```python
import math

import jax
import jax.numpy as jnp
from jax import lax
from jax.experimental import pallas as pl
from jax.experimental.pallas import tpu as pltpu

F32 = jnp.float32
MXU = jnp.bfloat16
AXES = ("x", "y", "c")
NDEV = 8
D = 1024
NIN = 6144
WIN_BLK = NIN // NDEV
DI = 2048
G, P, HG = 64, 64, 16
GB = 8
NJ = G // GB
SW = GB * P
UW = GB * HG
NSTATE = G * P
HEADS, DK = 4, 256
CH = 128
SG, SGD = 4, 512
EPS = 1e-6
ROPE_BASE = 10000.0
VMEM_CAP_V7X = 64 * 1024 * 1024
LOG_G = [math.log1p(-2.0 ** (-5.0 - h)) for h in range(HEADS)]
GELU_C = math.sqrt(2.0 / math.pi)

ADAM_LR, ADAM_B1, ADAM_B2, ADAM_EPS, ADAM_WD, ADAM_STEP = 0.001, 0.9, 0.999, 1e-08, 0.01, 10
BC1 = 1.0 - ADAM_B1 ** ADAM_STEP
BC2 = 1.0 - ADAM_B2 ** ADAM_STEP

SDS = jax.ShapeDtypeStruct
ARB2 = ("arbitrary", "arbitrary")


def _cp(vmem_mib, sem=None):
    kw = dict(vmem_limit_bytes=min(vmem_mib * 1024 * 1024, VMEM_CAP_V7X - 4 * 1024 * 1024))
    if sem is not None:
        kw["dimension_semantics"] = sem
    return pltpu.CompilerParams(**kw)


def _mm(a, b):
    return jnp.dot(a.astype(MXU), b.astype(MXU), preferred_element_type=F32)


def _mm_nt(a, b):
    return lax.dot_general(a.astype(MXU), b.astype(MXU), (((1,), (1,)), ((), ())), preferred_element_type=F32)


def _mm_tn(a, b):
    return lax.dot_general(a.astype(MXU), b.astype(MXU), (((0,), (0,)), ((), ())), preferred_element_type=F32)


def _gelu(x):
    return _gelu_and_grad(x)[0]


def _gelu_and_grad(x):
    x2 = x * x
    th = jnp.tanh(GELU_C * x * (1.0 + 0.044715 * x2))
    hp = 0.5 * (1.0 + th)
    return x * hp, hp + 0.5 * x * (1.0 - th * th) * GELU_C * (1.0 + 3.0 * 0.044715 * x2)


def _silu_and_grad(x):
    s = jax.nn.sigmoid(x)
    return x * s, s * (1.0 + x * (1.0 - s))


def _full(shape):
    nd = len(shape)
    return pl.BlockSpec(shape, lambda *_: (0,) * nd)


def _rms(xf):
    r = lax.rsqrt(jnp.mean(xf * xf, axis=-1, keepdims=True) + EPS)
    return xf * r, r


ANY_SPEC = pl.BlockSpec(memory_space=pl.ANY)
NO_DEPS = ()


def _load_once(src_hbm, dst_vmem, sem):
    @pl.when(pl.program_id(0) == 0)
    def _():
        cp = pltpu.make_async_copy(src_hbm, dst_vmem, sem)
        cp.start()
        cp.wait()


def _lane_blocks(L):
    return SDS((NJ, L, UW), F32)


def _lane_block_spec(rows, index):
    return pl.BlockSpec((NJ, rows, UW), lambda i: (0, index(i), 0))


def _from_lane_blocks(ref):
    return jnp.concatenate([ref[j] for j in range(NJ)], axis=1)


def _to_lane_blocks(ref, v):
    for j in range(NJ):
        ref[j] = v[:, j * UW:(j + 1) * UW].astype(ref.dtype)


def _in_proj(x, gain, wst, name, deps=NO_DEPS, lane_blocks=False):
    L = x.shape[0]
    tm = min(512, L)

    def body(x_ref, g_ref, w_hbm, *rest):
        outs = rest[len(deps):]
        o_ref, w_scr, sem = outs[0], outs[-2], outs[-1]
        _load_once(w_hbm, w_scr, sem)
        xhat, _ = _rms(x_ref[...])
        h = (xhat * g_ref[...]).astype(MXU)
        for c in range(NDEV):
            o_ref[:, c * WIN_BLK:(c + 1) * WIN_BLK] = jnp.dot(h, w_scr[c], preferred_element_type=F32)
        if lane_blocks:
            _to_lane_blocks(outs[1], o_ref[:, 0:D])

    p_spec, p_shape = pl.BlockSpec((tm, NIN), lambda i: (i, 0)), SDS((L, NIN), F32)
    return pl.pallas_call(
        body, name=name, grid=(L // tm,),
        in_specs=[pl.BlockSpec((tm, D), lambda i: (i, 0)), _full((1, D)), ANY_SPEC] + [ANY_SPEC] * len(deps),
        out_specs=[p_spec, _lane_block_spec(tm, lambda i: i)] if lane_blocks else p_spec,
        out_shape=[p_shape, _lane_blocks(L)] if lane_blocks else p_shape,
        scratch_shapes=[pltpu.VMEM((NDEV, D, WIN_BLK), MXU), pltpu.SemaphoreType.DMA(())],
        compiler_params=_cp(58, ("arbitrary",)),
    )(x, gain, wst, *deps)


def _in_proj_bwd_x(dp, x, gain, wst, dres, name, deps=NO_DEPS):
    L = x.shape[0]
    tm = min(512, L)

    def body(dp_ref, x_ref, g_ref, w_hbm, dres_ref, *rest):
        dx_ref, gg_ref, w_scr, sem = rest[len(deps):]
        _load_once(w_hbm, w_scr, sem)

        @pl.when(pl.program_id(0) == 0)
        def _():
            gg_ref[...] = jnp.zeros_like(gg_ref)

        dh = _mm_nt(dp_ref[:, 0:WIN_BLK], w_scr[0])
        for c in range(1, NDEV):
            dh += _mm_nt(dp_ref[:, c * WIN_BLK:(c + 1) * WIN_BLK], w_scr[c])
        xhat, r = _rms(x_ref[...])
        dxhat = dh * g_ref[...]
        dx_ref[...] = dres_ref[...] + r * (dxhat - xhat * jnp.mean(dxhat * xhat, axis=-1, keepdims=True))
        gg_ref[...] += jnp.sum(dh * xhat, axis=0, keepdims=True)

    row = pl.BlockSpec((tm, D), lambda i: (i, 0))
    return pl.pallas_call(
        body, name=name, grid=(L // tm,),
        in_specs=[pl.BlockSpec((tm, NIN), lambda i: (i, 0)), row, _full((1, D)), ANY_SPEC, row]
        + [ANY_SPEC] * len(deps),
        out_specs=[row, _full((1, D))],
        out_shape=[SDS((L, D), F32), SDS((1, D), F32)],
        scratch_shapes=[pltpu.VMEM((NDEV, D, WIN_BLK), MXU), pltpu.SemaphoreType.DMA(())],
        compiler_params=_cp(56, ("arbitrary",)),
    )(dp, x, gain, wst, dres, *deps)


def _wgrad_cols(x, gain, dp, name, deps=NO_DEPS):
    L = x.shape[0]
    tk = min(1024, L)
    nk = L // tk
    halves = 2
    nh = NDEV // halves

    def body(x_ref, g_ref, dp_ref, *rest):
        o_ref, acc = rest[len(deps):]
        k = pl.program_id(1)

        @pl.when(k == 0)
        def _():
            acc[...] = jnp.zeros_like(acc)

        xhat, _ = _rms(x_ref[...])
        acc[...] += _mm_tn(xhat * g_ref[...], dp_ref[...])

        @pl.when(k == nk - 1)
        def _():
            for c in range(nh):
                o_ref[c] = acc[:, c * WIN_BLK:(c + 1) * WIN_BLK].astype(o_ref.dtype)

    return pl.pallas_call(
        body, name=name, grid=(halves, nk),
        in_specs=[pl.BlockSpec((tk, D), lambda n, k: (k, 0)), _full((1, D)),
                  pl.BlockSpec((tk, nh * WIN_BLK), lambda n, k: (k, n))] + [ANY_SPEC] * len(deps),
        out_specs=pl.BlockSpec((nh, D, WIN_BLK), lambda n, k: (n, 0, 0)),
        out_shape=SDS((NDEV, D, WIN_BLK), MXU),
        scratch_shapes=[pltpu.VMEM((D, nh * WIN_BLK), F32)],
        compiler_params=_cp(56, ARB2),
    )(x, gain, dp, *deps)


def _wgrad_rows(a_parts, b, name, deps=NO_DEPS):
    L, N = b.shape
    na = len(a_parts)
    widths = [a.shape[1] for a in a_parts]
    M = sum(widths)
    tk = min(1024, L)
    nk = L // tk

    def body(*refs):
        a_refs, b_ref = refs[:na], refs[na]
        o_ref, acc = refs[na + 1 + len(deps):]
        k = pl.program_id(0)

        @pl.when(k == 0)
        def _():
            acc[...] = jnp.zeros_like(acc)

        bv = b_ref[...].astype(MXU)
        off = 0
        for a_ref, wd in zip(a_refs, widths):
            acc[off:off + wd, :] += _mm_tn(a_ref[...], bv)
            off += wd

        @pl.when(k == nk - 1)
        def _():
            o_ref[...] = acc[...].astype(o_ref.dtype).reshape(o_ref.shape)

    return pl.pallas_call(
        body, name=name, grid=(nk,),
        in_specs=[pl.BlockSpec((tk, wd), lambda k: (k, 0)) for wd in widths]
        + [pl.BlockSpec((tk, N), lambda k: (k, 0))] + [ANY_SPEC] * len(deps),
        out_specs=_full((NDEV, M // NDEV, N)),
        out_shape=SDS((NDEV, M // NDEV, N), MXU),
        scratch_shapes=[pltpu.VMEM((M, N), F32)],
        compiler_params=_cp(48, ("arbitrary",)),
    )(*a_parts, b, *deps)


def _s5_disc_fn(lr_raw, li, logdt, br, bi):
    lr = jnp.minimum(lr_raw, -1e-4)
    dt = jnp.exp(logdt)
    mag = jnp.exp(lr * dt)
    abr = mag * jnp.cos(li * dt)
    abi = mag * jnp.sin(li * dt)
    den = lr * lr + li * li
    nre = abr - 1.0
    nim = abi
    zr = (nre * lr + nim * li) / den
    zi = (nim * lr - nre * li) / den
    return abr, abi, zr * br - zi * bi, zr * bi + zi * br


def _s5_disc(lr, li, logdt, br, bi):
    def body(lr_ref, li_ref, dt_ref, br_ref, bi_ref, abr_ref, abi_ref, bbr_ref, bbi_ref):
        abr, abi, bbr, bbi = _s5_disc_fn(lr_ref[...], li_ref[...], dt_ref[...], br_ref[...], bi_ref[...])
        abr_ref[...] = abr
        abi_ref[...] = abi
        bbr_ref[...] = bbr
        bbi_ref[...] = bbi

    s1, s3 = SDS((G, 1, P), F32), SDS((G, HG, P), F32)
    return pl.pallas_call(body, name="s5_disc", out_shape=[s1, s1, s3, s3])(lr, li, logdt, br, bi)


def _s5_disc_bwd(lr, li, logdt, br, bi, dabr, dabi, dbbr, dbbi):
    def body(lr_ref, li_ref, dt_ref, br_ref, bi_ref, c0, c1, c2, c3, o0, o1, o2, o3, o4):
        _, vjp = jax.vjp(_s5_disc_fn, lr_ref[...], li_ref[...], dt_ref[...], br_ref[...], bi_ref[...])
        g = vjp((c0[...], c1[...], c2[...], c3[...]))
        for o, v in zip((o0, o1, o2, o3, o4), g):
            o[...] = v

    s1, s3 = SDS((G, 1, P), F32), SDS((G, HG, P), F32)
    return pl.pallas_call(body, name="s5_disc_bwd", out_shape=[s1, s1, SDS((G, 1, 1), F32), s3, s3])(
        lr, li, logdt, br, bi, dabr, dabi, dbbr, dbbi)


def _s5_tables(abr, abi, rows, name):
    def body(ar_ref, ai_ref, pfr, pfi, pbr, pbi):
        pfr[0:1, :] = ar_ref[...]
        pfi[0:1, :] = ai_ref[...]
        pbr[rows - 1:rows, :] = ar_ref[...]
        pbi[rows - 1:rows, :] = ai_ref[...]
        n = 1
        while n < rows:
            er, ei = pfr[n - 1:n, :], pfi[n - 1:n, :]
            xr, xi = pfr[0:n, :], pfi[0:n, :]
            pfr[n:2 * n, :] = er * xr - ei * xi
            pfi[n:2 * n, :] = er * xi + ei * xr
            yr, yi = pbr[rows - n:rows, :], pbi[rows - n:rows, :]
            pbr[rows - 2 * n:rows - n, :] = er * yr - ei * yi
            pbi[rows - 2 * n:rows - n, :] = er * yi + ei * yr
            n *= 2

    s = SDS((rows, NSTATE), F32)
    return pl.pallas_call(body, name=name, out_shape=[s, s, s, s], compiler_params=_cp(40))(abr, abi)


def _cscan(br, bi, pr_ref, pi_ref, reverse):
    T = br.shape[0]
    sign = -1.0 if reverse else 1.0
    row = lax.broadcasted_iota(jnp.int32, br.shape, 0)
    k = 1
    while k < T:
        akr = pr_ref[k - 1:k, :]
        aki = sign * pi_ref[k - 1:k, :]

        def shift(v):
            if k % 8 == 0:
                z = jnp.zeros((k, v.shape[1]), v.dtype)
                return jnp.concatenate([v[k:], z], 0) if reverse else jnp.concatenate([z, v[:T - k]], 0)
            if reverse:
                return jnp.where(row < T - k, pltpu.roll(v, T - k, 0), 0.0)
            return jnp.where(row >= k, pltpu.roll(v, k, 0), 0.0)

        sr, si = shift(br), shift(bi)
        br, bi = br + akr * sr - aki * si, bi + akr * si + aki * sr
        k *= 2
    return br, bi


def _embed(t):
    a, b = t.shape[1], t.shape[2]
    return jnp.einsum("jgab,gh->jgahb", t.reshape(NJ, GB, a, b), jnp.eye(GB, dtype=t.dtype)).reshape(NJ, GB * a, GB * b)


def _diag_blocks(t, a, b):
    return jnp.einsum("jgahb,gh->jgab", t.reshape(NJ, GB, a, GB, b), jnp.eye(GB, dtype=t.dtype)).reshape(G, a, b)


NT = 16


def _chunks(L):
    ncb = min(CH, L // NT)
    return ncb, NT * ncb


def _cmul_add(ar, ai, xr, xi, br, bi):
    return ar * xr - ai * xi + br, ar * xi + ai * xr + bi


def _pow_weights(w_ref, pwr_ref, pwi_ref, dst, adjoint):
    w = w_ref[...].astype(F32)
    wr, wi = w[:, :SW], w[:, SW:]
    for t in range(NT):
        k = t if adjoint else NT - 1 - t
        if k == 0:
            blk = w
        else:
            pr, pi = pwr_ref[k - 1:k, :], pwi_ref[k - 1:k, :]
            if adjoint:
                blk = jnp.concatenate([pr * wr + pi * wi, pr * wi - pi * wr], axis=1)
            else:
                blk = jnp.concatenate([pr * wr - pi * wi, pr * wi + pi * wr], axis=1)
        dst[t * UW:(t + 1) * UW, :] = blk.astype(dst.dtype)


def _s5_states(u_ref, bb_ref, bbp_scr, par_ref, pai_ref, cr, ci, ncb, bu_scr):
    us = [u_ref[pl.ds(t, ncb, stride=NT), :] for t in range(NT)]
    for t in range(NT):
        bu_scr[t] = _mm(us[t], bb_ref[...])
    e = _mm(jnp.concatenate(us, axis=1), bbp_scr[...])
    xr, xi = _cscan(e[:, :SW], e[:, SW:], par_ref, pai_ref, False)
    fr, fi = _cmul_add(par_ref[0:ncb, :], pai_ref[0:ncb, :], cr, ci, xr, xi)
    row = lax.broadcasted_iota(jnp.int32, fr.shape, 0)
    cinr = jnp.where(row >= 1, pltpu.roll(fr, 1, 0), cr)
    cini = jnp.where(row >= 1, pltpu.roll(fi, 1, 0), ci)
    return cinr, cini, jnp.concatenate([fr[ncb - 1:ncb, :], fi[ncb - 1:ncb, :]], axis=1)


def _slices(ref, ncb, axis):
    return jnp.concatenate([ref[pl.ds(t, ncb, stride=NT), :] for t in range(NT)], axis=axis)


def _s5_scan_fwd(u3, bb, cm, pwr, pwi, pwrt, pwit, par, pai, dskip):
    L = u3.shape[1]
    ncb, tb = _chunks(L)
    nb = L // tb

    def build(bb_ref, cm_ref, pwr_ref, pwi_ref, pct_ref, pit_ref, bbp_scr, ktp_scr, zc_scr):
        w = bb_ref[...].astype(F32)
        wr, wi = w[:, :SW], w[:, SW:]
        cmv = cm_ref[...].astype(F32)
        ct, cb = cmv[:SW, :], cmv[SW:, :]
        zero = jnp.zeros((UW, UW), MXU)
        for tau in range(NT):
            if tau == 0:
                blk = w
            else:
                pr, pi = pwr_ref[tau - 1:tau, :], pwi_ref[tau - 1:tau, :]
                blk = jnp.concatenate([pr * wr - pi * wi, pr * wi + pi * wr], axis=1)
            blk = blk.astype(MXU)
            bbp_scr[(NT - 1 - tau) * UW:(NT - tau) * UW, :] = blk
            k = jnp.dot(blk, cm_ref[...], preferred_element_type=F32).astype(MXU)
            for j in range(NT - tau):
                ktp_scr[j * UW:(j + 1) * UW, (j + tau) * UW:(j + tau + 1) * UW] = k
            pc, pic = pct_ref[:, tau:tau + 1], pit_ref[:, tau:tau + 1]
            zc_scr[:, tau * UW:(tau + 1) * UW] = jnp.concatenate(
                [pc * ct + pic * cb, pc * cb - pic * ct], axis=0).astype(MXU)
        for j in range(NT):
            for t in range(j):
                ktp_scr[j * UW:(j + 1) * UW, t * UW:(t + 1) * UW] = zero

    def body(u_ref, bb_ref, cm_ref, pwr_ref, pwi_ref, pct_ref, pit_ref, par_ref, pai_ref, d_ref, ypre_ref, st_ref,
             carry, cin_scr, bbp_scr, ktp_scr, zc_scr):
        @pl.when(pl.program_id(1) == 0)
        def _():
            carry[...] = jnp.zeros_like(carry)
            build(bb_ref, cm_ref, pwr_ref, pwi_ref, pct_ref, pit_ref, bbp_scr, ktp_scr, zc_scr)

        c = carry[...]
        st_ref[...] = c
        ucat = _slices(u_ref, ncb, 1)
        ub = ucat.astype(MXU)
        e = jnp.dot(ub, bbp_scr[...], preferred_element_type=F32)
        xr, xi = _cscan(e[:, :SW], e[:, SW:], par_ref, pai_ref, False)
        cr, ci = c[:, :SW], c[:, SW:]
        fr, fi = _cmul_add(par_ref[0:ncb, :], pai_ref[0:ncb, :], cr, ci, xr, xi)
        carry[...] = jnp.concatenate([fr[ncb - 1:ncb, :], fi[ncb - 1:ncb, :]], axis=1)
        row = lax.broadcasted_iota(jnp.int32, fr.shape, 0)
        cin_scr[:, :SW] = jnp.where(row >= 1, pltpu.roll(fr, 1, 0), cr).astype(MXU)
        cin_scr[:, SW:] = jnp.where(row >= 1, pltpu.roll(fi, 1, 0), ci).astype(MXU)
        y = (jnp.dot(ub, ktp_scr[...], preferred_element_type=F32)
             + jnp.dot(cin_scr[...], zc_scr[...], preferred_element_type=F32)
             + jnp.tile(d_ref[...], (1, NT)) * ucat)
        for t in range(NT):
            ypre_ref[pl.ds(t, ncb, stride=NT), :] = y[:, t * UW:(t + 1) * UW]

    tab = pl.BlockSpec((CH, SW), lambda j, i: (0, j))
    stp = pl.BlockSpec((NT, SW), lambda j, i: (0, j))
    stt = pl.BlockSpec((SW, NT), lambda j, i: (j, 0))
    vec = lambda w: pl.BlockSpec((1, w), lambda j, i: (0, j))
    return pl.pallas_call(
        body, name="s5_scan_fwd", grid=(NJ, nb),
        in_specs=[pl.BlockSpec((None, tb, UW), lambda j, i: (j, i, 0)),
                  pl.BlockSpec((None, UW, 2 * SW), lambda j, i: (j, 0, 0)),
                  pl.BlockSpec((None, 2 * SW, UW), lambda j, i: (j, 0, 0)),
                  stp, stp, stt, stt, tab, tab, vec(UW)],
        out_specs=[pl.BlockSpec((None, tb, UW), lambda j, i: (j, i, 0)),
                   pl.BlockSpec((None, None, 1, 2 * SW), lambda j, i: (j, i, 0, 0))],
        out_shape=[_lane_blocks(L), SDS((NJ, nb, 1, 2 * SW), F32)],
        scratch_shapes=[pltpu.VMEM((1, 2 * SW), F32), pltpu.VMEM((ncb, 2 * SW), MXU),
                        pltpu.VMEM((NT * UW, 2 * SW), MXU), pltpu.VMEM((NT * UW, NT * UW), MXU),
                        pltpu.VMEM((2 * SW, NT * UW), MXU)],
        compiler_params=_cp(56, ARB2),
    )(u3, bb, cm, pwr, pwi, pwrt, pwit, par, pai, dskip)


def _s5_gate_bwd(ypre3, p, dx1, wout_e, wglu, bglu, deps=NO_DEPS):
    L = p.shape[0]
    tm = min(256, L)

    def body(y_ref, az_ref, dx1_ref, wo_ref, wg_ref, bg_ref, *rest):
        dyp_ref, daz_ref, yg_ref, dt_ref, ya_ref, gbg_ref = rest[len(deps):]

        @pl.when(pl.program_id(0) == 0)
        def _():
            gbg_ref[...] = jnp.zeros_like(gbg_ref)

        yg, dgelu = _gelu_and_grad(_from_lane_blocks(y_ref))
        sg = jax.nn.sigmoid(_mm(yg, wg_ref[...]) + bg_ref[...])
        act, dact = _silu_and_grad(az_ref[...])
        y2 = yg * sg
        dya = _mm_nt(dx1_ref[...], wo_ref[...])
        daz_ref[...] = (dya * y2 * dact).astype(daz_ref.dtype)
        dy2 = dya * act
        dt = dy2 * yg * sg * (1.0 - sg)
        dyg = dy2 * sg + _mm_nt(dt, wg_ref[...])
        _to_lane_blocks(dyp_ref, dyg * dgelu)
        yg_ref[...] = yg.astype(yg_ref.dtype)
        dt_ref[...] = dt.astype(dt_ref.dtype)
        ya_ref[...] = (y2 * act).astype(ya_ref.dtype)
        gbg_ref[...] += jnp.sum(dt, axis=0, keepdims=True)

    row = pl.BlockSpec((tm, D), lambda i: (i, 0))
    return pl.pallas_call(
        body, name="s5_gate_bwd", grid=(L // tm,),
        in_specs=[_lane_block_spec(tm, lambda i: i), pl.BlockSpec((tm, D), lambda i: (i, 1)), row,
                  pl.BlockSpec((D, D), lambda i: (0, 0)), _full((D, D)), _full((1, D))] + [ANY_SPEC] * len(deps),
        out_specs=[_lane_block_spec(tm, lambda i: i), row, row, row, row, _full((1, D))],
        out_shape=[_lane_blocks(L), SDS((L, D), MXU), SDS((L, D), MXU), SDS((L, D), MXU), SDS((L, D), MXU),
                   SDS((1, D), F32)],
        compiler_params=_cp(40, ("arbitrary",)),
    )(ypre3, p, dx1, wout_e, wglu, bglu, *deps)


def _s5_scan_bwd(u3, dypre3, states, bb, cm, cmt, abr, abi, pwr, pwi, par, pai, pbr, pbi, dskip, deps=NO_DEPS):
    L = u3.shape[1]
    ncb, tb = _chunks(L)
    nb = L // tb
    rev = lambda i: nb - 1 - i

    def body(u_ref, dy_ref, st_ref, bb_ref, cm_ref, cmt_ref, ar_ref, ai_ref, pwr_ref, pwi_ref, par_ref, pai_ref,
             pbr_ref, pbi_ref, d_ref, *rest):
        (du_ref, gd_ref, gcm_ref, gbb_ref, gar_ref, gai_ref,
         lcarry, bu_scr, s_scr, gs_scr, bbp_scr, cmp_scr) = rest[len(deps):]
        del cm_ref

        @pl.when(pl.program_id(1) == 0)
        def _():
            _pow_weights(bb_ref, pwr_ref, pwi_ref, bbp_scr, False)
            _pow_weights(cmt_ref, pwr_ref, pwi_ref, cmp_scr, True)
            lcarry[...] = jnp.zeros_like(lcarry)
            gd_ref[...] = jnp.zeros_like(gd_ref)
            gcm_ref[...] = jnp.zeros_like(gcm_ref)
            gbb_ref[...] = jnp.zeros_like(gbb_ref)
            gar_ref[...] = jnp.zeros_like(gar_ref)
            gai_ref[...] = jnp.zeros_like(gai_ref)

        ar, ai = ar_ref[...], ai_ref[...]
        c = st_ref[...]
        sr, si, _ = _s5_states(u_ref, bb_ref, bbp_scr, par_ref, pai_ref, c[:, :SW], c[:, SW:], ncb, bu_scr)
        s_scr[0] = jnp.concatenate([sr, si], axis=1)
        for t in range(NT):
            bu = bu_scr[t]
            sr, si = _cmul_add(ar, ai, sr, si, bu[:, :SW], bu[:, SW:])
            s_scr[t + 1] = jnp.concatenate([sr, si], axis=1)
        dys = [dy_ref[pl.ds(t, ncb, stride=NT), :] for t in range(NT)]
        for t in range(NT):
            gs_scr[t] = _mm(dys[t], cmt_ref[...])
        f = _mm(jnp.concatenate(dys, axis=1), cmp_scr[...])
        xr, xi = _cscan(f[:, :SW], f[:, SW:], par_ref, pai_ref, True)
        lc = lcarry[...]
        lcr, lci = lc[:, :SW], lc[:, SW:]
        hr, hi = _cmul_add(pbr_ref[CH - ncb:CH, :], -pbi_ref[CH - ncb:CH, :], lcr, lci, xr, xi)
        lcarry[...] = jnp.concatenate([hr[0:1, :], hi[0:1, :]], axis=1)
        row = lax.broadcasted_iota(jnp.int32, hr.shape, 0)
        lr_ = jnp.where(row < ncb - 1, pltpu.roll(hr, ncb - 1, 0), lcr)
        li_ = jnp.where(row < ncb - 1, pltpu.roll(hi, ncb - 1, 0), lci)
        gar = jnp.zeros((1, SW), F32)
        gai = jnp.zeros((1, SW), F32)
        for t in reversed(range(NT)):
            gs = gs_scr[t]
            lr_, li_ = _cmul_add(ar, -ai, lr_, li_, gs[:, :SW], gs[:, SW:])
            rows = pl.ds(t, ncb, stride=NT)
            u_t, dy_t = u_ref[rows, :], dy_ref[rows, :]
            lam = jnp.concatenate([lr_, li_], axis=1)
            gbb_ref[...] += _mm_tn(u_t, lam)
            du_ref[rows, :] = _mm_nt(lam, bb_ref[...]) + dy_t * d_ref[...]
            gd_ref[...] += jnp.sum(dy_t * u_t, axis=0, keepdims=True)
            gcm_ref[...] += _mm_tn(s_scr[t + 1], dy_t)
            sp = s_scr[t]
            spr, spi = sp[:, :SW], sp[:, SW:]
            gar += jnp.sum(lr_ * spr + li_ * spi, axis=0, keepdims=True)
            gai += jnp.sum(li_ * spr - lr_ * spi, axis=0, keepdims=True)
        gar_ref[...] += gar
        gai_ref[...] += gai

    tab = pl.BlockSpec((CH, SW), lambda j, i: (0, j))
    stp = pl.BlockSpec((NT, SW), lambda j, i: (0, j))
    colblk = pl.BlockSpec((None, tb, UW), lambda j, i: (j, rev(i), 0))
    vec = lambda w: pl.BlockSpec((1, w), lambda j, i: (0, j))
    return pl.pallas_call(
        body, name="s5_scan_bwd", grid=(NJ, nb),
        in_specs=[colblk, colblk,
                  pl.BlockSpec((None, None, 1, 2 * SW), lambda j, i: (j, rev(i), 0, 0)),
                  pl.BlockSpec((None, UW, 2 * SW), lambda j, i: (j, 0, 0)),
                  pl.BlockSpec((None, 2 * SW, UW), lambda j, i: (j, 0, 0)),
                  pl.BlockSpec((None, UW, 2 * SW), lambda j, i: (j, 0, 0)),
                  vec(SW), vec(SW), stp, stp, tab, tab, tab, tab, vec(UW)] + [ANY_SPEC] * len(deps),
        out_specs=[colblk, vec(UW),
                   pl.BlockSpec((None, 2 * SW, UW), lambda j, i: (j, 0, 0)),
                   pl.BlockSpec((None, UW, 2 * SW), lambda j, i: (j, 0, 0)),
                   vec(SW), vec(SW)],
        out_shape=[_lane_blocks(L), SDS((1, D), F32),
                   SDS((NJ, 2 * SW, UW), F32), SDS((NJ, UW, 2 * SW), F32),
                   SDS((1, NSTATE), F32), SDS((1, NSTATE), F32)],
        scratch_shapes=[pltpu.VMEM((1, 2 * SW), F32), pltpu.VMEM((NT, ncb, 2 * SW), F32),
                        pltpu.VMEM((NT + 1, ncb, 2 * SW), F32), pltpu.VMEM((NT, ncb, 2 * SW), F32),
                        pltpu.VMEM((NT * UW, 2 * SW), MXU), pltpu.VMEM((NT * UW, 2 * SW), MXU)],
        compiler_params=_cp(56, ARB2),
    )(u3, dypre3, states, bb, cm, cmt, abr, abi, pwr, pwi, par, pai, pbr, pbi, dskip, *deps)


def _rope_tables(L, inv):
    tm = min(512, L)

    def body(inv_ref, cos_ref, sin_ref):
        pos = (lax.broadcasted_iota(jnp.int32, (tm, DK // 2), 0) + pl.program_id(0) * tm).astype(F32)
        ang = pos * inv_ref[...]
        cos_ref[...] = jnp.cos(ang)
        sin_ref[...] = jnp.sin(ang)

    blk = pl.BlockSpec((tm, DK // 2), lambda i: (i, 0))
    return pl.pallas_call(body, name="rope_tables", grid=(L // tm,), in_specs=[_full((1, DK // 2))],
                          out_specs=[blk, blk], out_shape=[SDS((L, DK // 2), F32)] * 2)(inv)


def _rot(x, cos, sin):
    x1, x2 = x[:, :DK // 2], x[:, DK // 2:]
    return jnp.concatenate([x1 * cos - x2 * sin, x1 * sin + x2 * cos], axis=1)


def _unrot(d, cos, sin):
    d1, d2 = d[:, :DK // 2], d[:, DK // 2:]
    return jnp.concatenate([d1 * cos + d2 * sin, d2 * cos - d1 * sin], axis=1)


def _ret_decays(h):
    lg = LOG_G[h]
    n = lax.broadcasted_iota(jnp.int32, (CH, CH), 0)
    m = lax.broadcasted_iota(jnp.int32, (CH, CH), 1)
    diff = (n - m).astype(F32)
    decay = jnp.where(n >= m, jnp.exp(lg * jnp.maximum(diff, 0.0)), 0.0)
    idx = lax.broadcasted_iota(jnp.int32, (CH, 1), 0).astype(F32)
    xi = jnp.exp(lg * (idx + 1.0))
    zeta = jnp.exp(lg * (CH - 1.0 - idx))
    return decay, xi, zeta, math.exp(lg * CH)


def _ret_tables(dec_scr, vec_scr):
    for h in range(HEADS):
        decay, xi, zeta, _ = _ret_decays(h)
        dec_scr[h] = decay
        vec_scr[h] = jnp.concatenate([jnp.broadcast_to(xi, (CH, 128)), jnp.broadcast_to(zeta, (CH, 128))], axis=1)


def _group_norm(o):
    mu = jnp.mean(o, axis=-1, keepdims=True)
    oc = o - mu
    rstd = lax.rsqrt(jnp.mean(oc * oc, axis=-1, keepdims=True) + EPS)
    return oc * rstd, rstd


def _ret_fwd(p, cos, sin, gain):
    L = p.shape[0]
    nb = L // CH

    def body(q_ref, k_ref, v_ref, bz_ref, cos_ref, sin_ref, g_ref, yb_ref, st_ref, sc_ref, o_ref,
             state, dec_scr, vec_scr):
        @pl.when(pl.program_id(0) == 0)
        def _():
            state[...] = jnp.zeros_like(state)
            _ret_tables(dec_scr, vec_scr)

        cos, sin = cos_ref[...], sin_ref[...]
        act, _ = _silu_and_grad(bz_ref[...])
        for h in range(HEADS):
            hs = slice(h * DK, (h + 1) * DK)
            xi, zeta = vec_scr[h, :, 0:1], vec_scr[h, :, 128:129]
            s_prev = state[h]
            s_prev_b = s_prev.astype(MXU)
            st_ref[h] = s_prev_b
            v = v_ref[:, hs]
            qr = _rot(q_ref[:, hs], cos, sin)
            kr = _rot(k_ref[:, hs], cos, sin) * (DK ** -0.5)
            scores = (_mm_nt(qr, kr) * dec_scr[h]).astype(MXU)
            o = _mm(scores, v) + _mm(qr * xi, s_prev_b)
            sc_ref[:, h * CH:(h + 1) * CH] = scores
            o_ref[:, hs] = o
            state[h] = s_prev * math.exp(LOG_G[h] * CH) + _mm_tn(kr * zeta, v)
            on, _ = _group_norm(o)
            yb_ref[:, hs] = (on * g_ref[:, hs] * act[:, hs]).astype(yb_ref.dtype)

    col = lambda c: pl.BlockSpec((CH, D), lambda i: (i, c))
    rope = pl.BlockSpec((CH, DK // 2), lambda i: (i, 0))
    return pl.pallas_call(
        body, name="ret_fwd", grid=(nb,),
        in_specs=[col(2), col(3), col(4), col(5), rope, rope, _full((1, D))],
        out_specs=[pl.BlockSpec((CH, D), lambda i: (i, 0)),
                   pl.BlockSpec((None, HEADS, DK, DK), lambda i: (i, 0, 0, 0)),
                   pl.BlockSpec((CH, HEADS * CH), lambda i: (i, 0)), pl.BlockSpec((CH, D), lambda i: (i, 0))],
        out_shape=[SDS((L, D), MXU), SDS((nb, HEADS, DK, DK), MXU), SDS((L, HEADS * CH), MXU), SDS((L, D), F32)],
        scratch_shapes=[pltpu.VMEM((HEADS, DK, DK), F32), pltpu.VMEM((HEADS, CH, CH), F32),
                        pltpu.VMEM((HEADS, CH, 256), F32)],
        compiler_params=_cp(40, ("arbitrary",)),
    )(p, p, p, p, cos, sin, gain)


def _ret_bwd(p, cos, sin, gain, states, scores, o, dx1, wout_e, du, daz, deps=NO_DEPS):
    L = p.shape[0]
    nb = L // CH
    rev = lambda i: nb - 1 - i

    def body(q_ref, k_ref, v_ref, bz_ref, cos_ref, sin_ref, g_ref, st_ref, sc_ref, o_ref, dx1_ref, wo_ref, du_ref,
             daz_ref, *rest):
        dp_ref, yb_ref, gg_ref, gstate, dec_scr, vec_scr = rest[len(deps):]

        @pl.when(pl.program_id(0) == 0)
        def _():
            gstate[...] = jnp.zeros_like(gstate)
            gg_ref[...] = jnp.zeros_like(gg_ref)
            _ret_tables(dec_scr, vec_scr)

        cos, sin = cos_ref[...], sin_ref[...]
        act, dact = _silu_and_grad(bz_ref[...])
        dyb = _mm_nt(dx1_ref[...], wo_ref[...])
        dp_ref[:, 0:D] = _from_lane_blocks(du_ref).astype(dp_ref.dtype)
        dp_ref[:, D:2 * D] = daz_ref[...]
        for h in range(HEADS):
            hs = slice(h * DK, (h + 1) * DK)
            col = lambda part: slice((2 + part) * D + h * DK, (2 + part) * D + (h + 1) * DK)
            decay = dec_scr[h]
            xi, zeta = vec_scr[h, :, 0:1], vec_scr[h, :, 128:129]
            v = v_ref[:, hs]
            s_prev_b = st_ref[h]
            qr = _rot(q_ref[:, hs], cos, sin)
            kr = _rot(k_ref[:, hs], cos, sin) * (DK ** -0.5)
            scores = sc_ref[:, h * CH:(h + 1) * CH]
            on, rstd = _group_norm(o_ref[:, hs])
            gain_h = g_ref[:, hs]
            out = on * gain_h
            yb_ref[:, hs] = (out * act[:, hs]).astype(yb_ref.dtype)
            dyb_h = dyb[:, hs]
            dp_ref[:, col(3)] = (dyb_h * out * dact[:, hs]).astype(dp_ref.dtype)
            dout = dyb_h * act[:, hs]
            gg_ref[:, hs] += jnp.sum(dout * on, axis=0, keepdims=True)
            don = dout * gain_h
            do = rstd * (don - jnp.mean(don, axis=-1, keepdims=True)
                         - on * jnp.mean(don * on, axis=-1, keepdims=True))
            gnext = gstate[h]
            gnext_b = gnext.astype(MXU)
            dscores = _mm_nt(do, v) * decay
            dp_ref[:, col(2)] = (_mm_tn(scores, do) + _mm(kr * zeta, gnext_b)).astype(dp_ref.dtype)
            dqr = _mm(dscores, kr) + _mm_nt(do, s_prev_b) * xi
            dkr = _mm_tn(dscores, qr) + _mm_nt(v, gnext_b) * zeta
            gstate[h] = gnext * math.exp(LOG_G[h] * CH) + _mm_tn(qr * xi, do)
            dp_ref[:, col(0)] = _unrot(dqr, cos, sin).astype(dp_ref.dtype)
            dp_ref[:, col(1)] = (_unrot(dkr, cos, sin) * (DK ** -0.5)).astype(dp_ref.dtype)

    col = lambda c: pl.BlockSpec((CH, D), lambda i: (rev(i), c))
    rope = pl.BlockSpec((CH, DK // 2), lambda i: (rev(i), 0))
    outc = col(0)
    act_out = SDS((L, D), MXU)
    return pl.pallas_call(
        body, name="ret_bwd", grid=(nb,),
        in_specs=[col(2), col(3), col(4), col(5), rope, rope, _full((1, D)),
                  pl.BlockSpec((None, HEADS, DK, DK), lambda i: (rev(i), 0, 0, 0)),
                  pl.BlockSpec((CH, HEADS * CH), lambda i: (rev(i), 0)), outc,
                  outc, pl.BlockSpec((D, D), lambda i: (1, 0)), _lane_block_spec(CH, rev), outc]
        + [ANY_SPEC] * len(deps),
        out_specs=[pl.BlockSpec((CH, NIN), lambda i: (rev(i), 0)), outc, _full((1, D))],
        out_shape=[SDS((L, NIN), MXU), act_out, SDS((1, D), F32)],
        scratch_shapes=[pltpu.VMEM((HEADS, DK, DK), F32), pltpu.VMEM((HEADS, CH, CH), F32),
                        pltpu.VMEM((HEADS, CH, 256), F32)],
        compiler_params=_cp(48, ("arbitrary",)),
    )(p, p, p, p, cos, sin, gain, states, scores, o, dx1, wout_e, du, daz, *deps)


def _out_even(x, ypre3, p, yb, wglu, bglu, wout):
    L = x.shape[0]
    tm = min(512, L)

    def body(x_ref, y_ref, az_ref, yb_ref, wg_ref, bg_ref, w_ref, o_ref):
        yg = _gelu(_from_lane_blocks(y_ref))
        t = _mm(yg, wg_ref[...]) + bg_ref[...]
        act, _ = _silu_and_grad(az_ref[...])
        ya = (yg * jax.nn.sigmoid(t) * act).astype(MXU)
        cat = jnp.concatenate([ya, yb_ref[...]], axis=1)
        o_ref[...] = x_ref[...] + jnp.dot(cat, w_ref[...], preferred_element_type=F32)

    row = pl.BlockSpec((tm, D), lambda i: (i, 0))
    return pl.pallas_call(
        body, name="out_even", grid=(L // tm,),
        in_specs=[row, _lane_block_spec(tm, lambda i: i), pl.BlockSpec((tm, D), lambda i: (i, 1)), row,
                  _full((D, D)), _full((1, D)), _full((DI, D))],
        out_specs=row, out_shape=SDS((L, D), F32), compiler_params=_cp(48, ("arbitrary",)),
    )(x, ypre3, p, yb, wglu, bglu, wout)


def _sgu_core(pv, gain, ws_ref, bs_ref):
    pu, pvv, z = pv[:, :DI], pv[:, DI:2 * DI], pv[:, 2 * DI:]
    u, gu = _gelu_and_grad(pu)
    v, gv = _gelu_and_grad(pvv)
    mu = jnp.mean(v, axis=-1, keepdims=True)
    vc = v - mu
    rstd = lax.rsqrt(jnp.mean(vc * vc, axis=-1, keepdims=True) + EPS)
    vhat = vc * rstd
    vn = vhat * gain
    t = lax.broadcasted_iota(jnp.int32, (CH, CH), 0)
    s_ = lax.broadcasted_iota(jnp.int32, (CH, CH), 1)
    mask = t >= s_
    wm = [jnp.where(mask, ws_ref[g], 0.0).astype(MXU) for g in range(SG)]
    s = jnp.concatenate([_mm(wm[g], vn[:, g * SGD:(g + 1) * SGD]) + bs_ref[g] for g in range(SG)], axis=1)
    return gu, gv, z, u, vhat, rstd, vn, mask, wm, s


def _sgu_fwd_bwd(p2, x1, gain, wsp, bsp, wout, fnorm, tgt):
    L = p2.shape[0]

    def body(p_ref, x1_ref, g_ref, ws_ref, bs_ref, wo_ref, fn_ref, t_ref,
             dp_ref, y_ref, dx2_ref, gg_ref, gws_ref, gbs_ref, gfn_ref, loss_ref):
        @pl.when(pl.program_id(0) == 0)
        def _():
            gg_ref[...] = jnp.zeros_like(gg_ref)
            gws_ref[...] = jnp.zeros_like(gws_ref)
            gbs_ref[...] = jnp.zeros_like(gbs_ref)
            gfn_ref[...] = jnp.zeros_like(gfn_ref)
            loss_ref[...] = jnp.zeros_like(loss_ref)

        gain = g_ref[...]
        gu, gv, z, u, vhat, rstd, vn, mask, wm, s = _sgu_core(p_ref[...], gain, ws_ref, bs_ref)
        act, dact = _silu_and_grad(z)
        y = (u * s * act).astype(MXU)
        y_ref[...] = y
        x2 = x1_ref[...] + jnp.dot(y, wo_ref[...], preferred_element_type=F32)
        xhat, r = _rms(x2)
        fn = fn_ref[...]
        e = xhat * fn - t_ref[...]
        loss_ref[...] += 0.5 * jnp.sum(jnp.mean(e * e, axis=-1, keepdims=True), axis=0, keepdims=True)
        do = e * (1.0 / D)
        gfn_ref[...] += jnp.sum(do * xhat, axis=0, keepdims=True)
        dxhat = do * fn
        dx2 = r * (dxhat - xhat * jnp.mean(dxhat * xhat, axis=-1, keepdims=True))
        dx2_ref[...] = dx2
        dy = _mm_nt(dx2, wo_ref[...])
        du = dy * s * act
        ds = dy * u * act
        dz = dy * u * s * dact
        dvn = []
        for g in range(SG):
            ds_g = ds[:, g * SGD:(g + 1) * SGD]
            vn_g = vn[:, g * SGD:(g + 1) * SGD]
            gbs_ref[g] += jnp.sum(ds_g, axis=1, keepdims=True)
            gws_ref[g] += jnp.where(mask, _mm_nt(ds_g, vn_g), 0.0)
            dvn.append(_mm_tn(wm[g], ds_g))
        dvn = jnp.concatenate(dvn, axis=1)
        gg_ref[...] += jnp.sum(dvn * vhat, axis=0, keepdims=True)
        dvhat = dvn * gain
        dv = rstd * (dvhat - jnp.mean(dvhat, axis=-1, keepdims=True)
                     - vhat * jnp.mean(dvhat * vhat, axis=-1, keepdims=True))
        dp_ref[...] = jnp.concatenate([du * gu, dv * gv, dz], axis=1).astype(dp_ref.dtype)

    row = pl.BlockSpec((CH, D), lambda i: (i, 0))
    wide = pl.BlockSpec((CH, NIN), lambda i: (i, 0))
    return pl.pallas_call(
        body, name="sgu_fwd_bwd", grid=(L // CH,),
        in_specs=[wide, row, _full((1, DI)), _full((SG, CH, CH)), _full((SG, CH, 1)), _full((DI, D)),
                  _full((1, D)), row],
        out_specs=[wide, pl.BlockSpec((CH, DI), lambda i: (i, 0)), row,
                   _full((1, DI)), _full((SG, CH, CH)), _full((SG, CH, 1)), _full((1, D)), _full((1, 128))],
        out_shape=[SDS((L, NIN), MXU), SDS((L, DI), MXU), SDS((L, D), F32), SDS((1, DI), F32),
                   SDS((SG, CH, CH), F32), SDS((SG, CH, 1), F32), SDS((1, D), F32), SDS((1, 128), F32)],
        compiler_params=_cp(48, ("arbitrary",)),
    )(p2, x1, gain, wsp, bsp, wout, fnorm, tgt)


def _my_index():
    return 4 * lax.axis_index("x") + 2 * lax.axis_index("y") + lax.axis_index("c")


def _ordered_sum(land_ref, own, me):
    g = None
    for s in range(NDEV):
        part = jnp.where(me == s, own, land_ref[s].astype(F32))
        g = part if g is None else g + part
    return g


def _adamw_math(w, m, v, g):
    mn = ADAM_B1 * m + (1.0 - ADAM_B1) * g
    vn = ADAM_B2 * v + (1.0 - ADAM_B2) * (g * g)
    mhat = mn / BC1
    vhat = vn / BC2
    return g, -ADAM_LR * (mhat / (jnp.sqrt(vhat) + ADAM_EPS) + ADAM_WD * w), mn, vn


def _adamw(w, m, v, land, own, name):
    R, C = w.shape
    tr = R
    for cand in (256, 128, 64, 32, 16, 8):
        if R % cand == 0 and R > cand:
            tr = cand
            break

    def body(w_ref, m_ref, v_ref, land_ref, own_ref, g_ref, d_ref, mo_ref, vo_ref):
        g = _ordered_sum(land_ref, own_ref[...].astype(F32), _my_index())
        for o, val in zip((g_ref, d_ref, mo_ref, vo_ref), _adamw_math(w_ref[...], m_ref[...], v_ref[...], g)):
            o[...] = val

    blk = pl.BlockSpec((tr, C), lambda i: (i, 0))
    out = SDS((R, C), F32)
    return pl.pallas_call(
        body, name=name, grid=(R // tr,),
        in_specs=[blk, blk, blk, pl.BlockSpec((NDEV, tr, C), lambda i: (0, i, 0)), blk],
        out_specs=[blk, blk, blk, blk], out_shape=[out, out, out, out],
        compiler_params=_cp(40, ("arbitrary",)),
    )(w, m, v, land, own)


def _adamw_many(ws, ms, vs, lands, owns, name):
    k = len(ws)

    def body(*refs):
        ins, outs = refs[:5 * k], refs[5 * k:]
        me = _my_index()
        for i in range(k):
            w_ref, m_ref, v_ref, land_ref, own_ref = (ins[j * k + i] for j in range(5))
            g = _ordered_sum(land_ref, own_ref[...], me)
            for j, val in enumerate(_adamw_math(w_ref[...], m_ref[...], v_ref[...], g)):
                outs[j * k + i][...] = val

    out_shape = [SDS(w.shape, F32) for _ in range(4) for w in ws]
    res = pl.pallas_call(body, name=name, out_shape=out_shape, compiler_params=_cp(60))(*ws, *ms, *vs, *lands, *owns)
    return [res[j * k:(j + 1) * k] for j in range(4)]


MESH = pl.DeviceIdType.MESH
HBM_SPEC = pl.BlockSpec(memory_space=pltpu.HBM)
SEM_SPEC = pl.BlockSpec(memory_space=pltpu.SEMAPHORE)
EFFECT = pltpu.SideEffectType.DATAFLOW_SIDE_EFFECTING


def _me_and_peers():
    x, y, c = lax.axis_index("x"), lax.axis_index("y"), lax.axis_index("c")
    me = 4 * x + 2 * y + c
    peers = []
    for r in range(1, NDEV):
        px, py, pc = x ^ ((r >> 2) & 1), y ^ ((r >> 1) & 1), c ^ (r & 1)
        peers.append(((px, py, pc), 4 * px + 2 * py + pc))
    return me, peers


def _land_shape(a, scatter):
    return (NDEV,) + (a.shape[1:] if scatter else a.shape)


def _remote(src, dst, send_sems, recv_sems, r, k, n, dev):
    i = r * n + k
    return pltpu.make_async_remote_copy(src_ref=src, dst_ref=dst, send_sem=send_sems.at[i], recv_sem=recv_sems.at[i],
                                        device_id=dev, device_id_type=MESH)


def _exchange(arrays, scatter, name):
    n = len(arrays)
    out_shape = [SDS(_land_shape(a, scatter), a.dtype) for a in arrays]

    def body(*refs):
        ins, outs = refs[:n], refs[n:2 * n]
        send_sems, recv_sems, loc_sems = refs[2 * n:]
        me, peers = _me_and_peers()
        local = []
        for k in range(n):
            src = ins[k].at[me] if scatter else ins[k]
            cp = pltpu.make_async_copy(src, outs[k].at[me], loc_sems.at[k])
            cp.start()
            local.append(cp)
        sends = []
        for r, (dev, lin) in enumerate(peers):
            for k in range(n):
                src = ins[k].at[lin] if scatter else ins[k]
                cp = _remote(src, outs[k].at[me], send_sems, recv_sems, r, k, n, dev)
                cp.start()
                sends.append(cp)
        for r, (dev, lin) in enumerate(peers):
            for k in range(n):
                src = ins[k].at[me] if scatter else ins[k]
                _remote(src, outs[k].at[lin], send_sems, recv_sems, r, k, n, dev).wait_recv()
        for cp in sends:
            cp.wait_send()
        for cp in local:
            cp.wait()

    return pl.pallas_call(
        body, name=name, in_specs=[HBM_SPEC] * n, out_specs=[HBM_SPEC] * n, out_shape=out_shape,
        scratch_shapes=[pltpu.SemaphoreType.DMA(((NDEV - 1) * n,)), pltpu.SemaphoreType.DMA(((NDEV - 1) * n,)),
                        pltpu.SemaphoreType.DMA((n,))],
    )(*arrays)


def _exchange_start(arrays, scatter, name):
    n = len(arrays)
    lands = [lax.empty(_land_shape(a, scatter), a.dtype) for a in arrays]

    def body(*refs):
        ins, lnd = refs[:n], refs[n:2 * n]
        send_sems, recv_sems, own_sems = refs[2 * n:2 * n + 3]
        token = refs[-1]
        me, peers = _me_and_peers()
        for r, (dev, lin) in enumerate(peers):
            for k in range(n):
                src = ins[k].at[lin] if scatter else ins[k]
                _remote(src, lnd[k].at[me], send_sems, recv_sems, r, k, n, dev).start()
        if not scatter:
            for k in range(n):
                pltpu.make_async_copy(ins[k], lnd[k].at[me], own_sems.at[k]).start()
        token[...] = jnp.zeros_like(token)

    sem = pltpu.SemaphoreType.DMA(((NDEV - 1) * n,))
    outs = pl.pallas_call(
        body, name=name,
        out_shape=(sem, sem, pltpu.SemaphoreType.DMA((n,)), *[pltpu.HBM(a.shape, a.dtype) for a in arrays],
                   *[pltpu.HBM(l.shape, l.dtype) for l in lands], SDS((8, 128), F32)),
        in_specs=[HBM_SPEC] * (2 * n),
        out_specs=(SEM_SPEC, SEM_SPEC, SEM_SPEC, *[HBM_SPEC] * (2 * n), pl.BlockSpec(memory_space=pltpu.VMEM)),
        input_output_aliases={k: 3 + k for k in range(2 * n)},
        compiler_params=pltpu.CompilerParams(has_side_effects=EFFECT),
    )(*[pltpu.with_memory_space_constraint(a, pltpu.HBM) for a in arrays],
      *[pltpu.with_memory_space_constraint(l, pltpu.HBM) for l in lands])
    return (n, scatter, outs[0], outs[1], outs[2], outs[3:3 + n], outs[3 + n:3 + 2 * n]), outs[-1]


def _exchange_wait(handle, after, name):
    n, scatter, send_sems, recv_sems, own_sems, thru, lands = handle
    after = tuple(after)

    def body(*refs):
        ins, lnd = refs[:n], refs[n:2 * n]
        send_sems, recv_sems, own_sems = refs[2 * n:2 * n + 3]
        me, peers = _me_and_peers()
        for r, (dev, lin) in enumerate(peers):
            for k in range(n):
                src = ins[k].at[lin] if scatter else ins[k]
                cp = _remote(src, lnd[k].at[lin], send_sems, recv_sems, r, k, n, dev)
                cp.wait_send()
                cp.wait_recv()
        if not scatter:
            for k in range(n):
                pltpu.make_async_copy(ins[k], lnd[k].at[me], own_sems.at[k]).wait()

    outs = pl.pallas_call(
        body, name=name,
        out_shape=(*[pltpu.HBM(a.shape, a.dtype) for a in thru], *[pltpu.HBM(l.shape, l.dtype) for l in lands]),
        in_specs=[HBM_SPEC] * (2 * n) + [SEM_SPEC, SEM_SPEC, SEM_SPEC] + [ANY_SPEC] * len(after),
        out_specs=tuple([HBM_SPEC] * (2 * n)),
        input_output_aliases={k: k for k in range(2 * n)},
        compiler_params=pltpu.CompilerParams(has_side_effects=EFFECT),
    )(*thru, *lands, send_sems, recv_sems, own_sems, *after)
    return list(outs[:n]), list(outs[n:])


CHIP_RELATIONS = (2, 4, 6)


def _peer(r):
    x, y, c = lax.axis_index("x"), lax.axis_index("y"), lax.axis_index("c")
    px, py, pc = x ^ ((r >> 2) & 1), y ^ ((r >> 1) & 1), c ^ (r & 1)
    return (px, py, pc), 4 * px + 2 * py + pc


def _copy(src, dst, send_sems, recv_sems, i, dev):
    return pltpu.make_async_remote_copy(src_ref=src, dst_ref=dst, send_sem=send_sems.at[i], recv_sem=recv_sems.at[i],
                                        device_id=dev, device_id_type=MESH)


def _gather2_start(a, name):
    land = lax.empty((NDEV,) + a.shape, a.dtype)

    def body(own, lnd, send_sems, recv_sems, own_sem, own_thru, lnd_thru, token):
        me = _my_index()
        for i, r in enumerate((1,) + CHIP_RELATIONS):
            dev, _ = _peer(r)
            _copy(own, lnd.at[me], send_sems, recv_sems, i, dev).start()
        pltpu.make_async_copy(own, lnd.at[me], own_sem.at[0]).start()
        token[...] = jnp.zeros_like(token)

    sem4 = pltpu.SemaphoreType.DMA((4,))
    outs = pl.pallas_call(
        body, name=name,
        out_shape=(sem4, sem4, pltpu.SemaphoreType.DMA((1,)), pltpu.HBM(a.shape, a.dtype),
                   pltpu.HBM(land.shape, land.dtype), SDS((8, 128), F32)),
        in_specs=[HBM_SPEC, HBM_SPEC],
        out_specs=(SEM_SPEC, SEM_SPEC, SEM_SPEC, HBM_SPEC, HBM_SPEC, pl.BlockSpec(memory_space=pltpu.VMEM)),
        input_output_aliases={0: 3, 1: 4},
        compiler_params=pltpu.CompilerParams(has_side_effects=EFFECT),
    )(pltpu.with_memory_space_constraint(a, pltpu.HBM), pltpu.with_memory_space_constraint(land, pltpu.HBM))
    return outs[:5], outs[5]


def _gather2_forward(handle, after, name):
    send_sems, recv_sems, own_sem, own, land = handle
    after = tuple(after)

    def body(lnd, recv_sems, *rest):
        send2, recv2, lnd_thru = rest[len(after):]
        sib, _ = _peer(1)
        for k, r in enumerate(CHIP_RELATIONS):
            dev, lin = _peer(r)
            _copy(lnd.at[lin], lnd.at[lin], recv_sems, recv_sems, 1 + k, dev).wait_recv()
            _copy(lnd.at[lin], lnd.at[lin], send2, recv2, k, sib).start()

    sem3 = pltpu.SemaphoreType.DMA((3,))
    send2, recv2, land = pl.pallas_call(
        body, name=name,
        out_shape=(sem3, sem3, pltpu.HBM(land.shape, land.dtype)),
        in_specs=[HBM_SPEC, SEM_SPEC] + [ANY_SPEC] * len(after),
        out_specs=(SEM_SPEC, SEM_SPEC, HBM_SPEC),
        input_output_aliases={0: 2},
        compiler_params=pltpu.CompilerParams(has_side_effects=EFFECT),
    )(land, recv_sems, *after)
    return send_sems, recv_sems, own_sem, send2, recv2, own, land


def _gather2_wait(handle, name):
    send_sems, recv_sems, own_sem, send2, recv2, own, land = handle

    def body(own_ref, lnd, send_sems, recv_sems, own_sem, send2, recv2, own_thru, lnd_thru):
        me = _my_index()
        sib, sib_lin = _peer(1)
        for i, r in enumerate((1,) + CHIP_RELATIONS):
            dev, _ = _peer(r)
            _copy(own_ref, lnd.at[me], send_sems, recv_sems, i, dev).wait_send()
        _copy(own_ref, lnd.at[sib_lin], send_sems, recv_sems, 0, sib).wait_recv()
        for k, r in enumerate(CHIP_RELATIONS):
            _, lin = _peer(r)
            _, lin_other = _peer(r ^ 1)
            _copy(lnd.at[lin], lnd.at[lin], send2, recv2, k, sib).wait_send()
            _copy(lnd.at[lin_other], lnd.at[lin_other], send2, recv2, k, sib).wait_recv()
        pltpu.make_async_copy(own_ref, lnd.at[me], own_sem.at[0]).wait()

    outs = pl.pallas_call(
        body, name=name,
        out_shape=(pltpu.HBM(own.shape, own.dtype), pltpu.HBM(land.shape, land.dtype)),
        in_specs=[HBM_SPEC, HBM_SPEC] + [SEM_SPEC] * 5,
        out_specs=(HBM_SPEC, HBM_SPEC),
        input_output_aliases={0: 0, 1: 1},
        compiler_params=pltpu.CompilerParams(has_side_effects=EFFECT),
    )(own, land, send_sems, recv_sems, own_sem, send2, recv2)
    return outs[1]


def _local_step(x, tgt, norm_even, first_weight, lam_re, lam_im, log_dt, b_re, b_im, c_re, c_im, s5_d, bglu,
                ret_gain, wsp, bsp, fnorm, late_weights, emit, start_token=None):
    L = x.shape[0]
    lr3, li3 = lam_re.reshape(G, 1, P), lam_im.reshape(G, 1, P)
    dt3 = log_dt.reshape(G, 1, 1)
    br3, bi3 = jnp.swapaxes(b_re, 1, 2), jnp.swapaxes(b_im, 1, 2)
    abr3, abi3, bbr3, bbi3 = _s5_disc(lr3, li3, dt3, br3, bi3)
    bb = jnp.concatenate([_embed(bbr3), _embed(bbi3)], axis=2).astype(MXU)
    cm = jnp.concatenate([_embed(jnp.swapaxes(c_re, 1, 2)), -_embed(jnp.swapaxes(c_im, 1, 2))], axis=1).astype(MXU)
    abr, abi = abr3.reshape(1, NSTATE), abi3.reshape(1, NSTATE)
    pwr, pwi, _, _ = _s5_tables(abr, abi, NT, "s5_tables_step")
    par, pai, pbr, pbi = _s5_tables(pwr[NT - 1:NT], pwi[NT - 1:NT], CH, "s5_tables_chunk")
    inv = (ROPE_BASE ** (-jnp.arange(DK // 2, dtype=F32) / (DK // 2))).reshape(1, DK // 2)
    cos, sin = _rope_tables(L, inv)
    bsp3 = bsp.reshape(SG, CH, 1)

    def dep(token):
        return NO_DEPS if token is None else (token,)

    win_e = first_weight((cos, pbi, cm))
    p, u3 = _in_proj(x, norm_even, win_e, "in_even", dep(start_token), lane_blocks=True)
    ypre, s5_states = _s5_scan_fwd(u3, bb, cm, pwr, pwi, pwr.T, pwi.T, par, pai, s5_d)
    yb, ret_states, ret_scores, ret_o = _ret_fwd(p, cos, sin, ret_gain)
    wglu, wout_e, norm_odd, win_o, sgu_gain, wout_o = late_weights((ypre, yb))
    x1 = _out_even(x, ypre, p, yb, wglu, bglu, wout_e)
    p2 = _in_proj(x1, norm_odd, win_o, "in_odd")
    dp2, y_o, dx2, g_sgu_gain, g_wsp, g_bsp, g_fnorm, loss = _sgu_fwd_bwd(
        p2, x1, sgu_gain, wsp, bsp3, wout_o, fnorm, tgt)

    g_wout_o = _wgrad_rows([y_o], dx2, "wgrad_out_odd")
    g_win_o = _wgrad_cols(x1, norm_odd, dp2, "wgrad_in_odd")
    tok = emit("odd", dict(w_in_odd=g_win_o, w_out_odd=g_wout_o))
    dx1, g_norm_odd = _in_proj_bwd_x(dp2, x1, norm_odd, win_o, dx2, "in_odd_bwd", dep(tok))
    tok = emit("small_odd", dict(norm_odd=g_norm_odd, sgu_norm_gain=g_sgu_gain, sgu_w_spatial=g_wsp,
                                 sgu_b_spatial=g_bsp.reshape(SG, CH), final_norm=g_fnorm))

    dypre, daz, yg, dt, ya2, g_bglu = _s5_gate_bwd(ypre, p, dx1, wout_e, wglu, bglu, dep(tok))
    g_wglu = _wgrad_rows([yg], dt, "wgrad_glu")
    tok = emit("glu", dict(s5_w_glu=g_wglu))
    du, g_d, g_cm, g_bb, g_ar, g_ai = _s5_scan_bwd(u3, dypre, s5_states, bb, cm, jnp.swapaxes(cm, 1, 2), abr, abi,
                                                   pwr, pwi, par, pai, pbr, pbi, s5_d, dep(tok))
    dbbr3 = _diag_blocks(g_bb[:, :, :SW], HG, P)
    dbbi3 = _diag_blocks(g_bb[:, :, SW:], HG, P)
    g_c_re = jnp.swapaxes(_diag_blocks(g_cm[:, :SW, :], P, HG), 1, 2)
    g_c_im = -jnp.swapaxes(_diag_blocks(g_cm[:, SW:, :], P, HG), 1, 2)
    g_lr3, g_li3, g_dt3, g_br3, g_bi3 = _s5_disc_bwd(
        lr3, li3, dt3, br3, bi3, g_ar.reshape(G, 1, P), g_ai.reshape(G, 1, P), dbbr3, dbbi3)
    tok = emit("small_s5", dict(
        s5_lam_re=g_lr3.reshape(G, P), s5_lam_im=g_li3.reshape(G, P), s5_log_dt=g_dt3.reshape(1, G),
        s5_b_re=g_br3, s5_b_im=g_bi3, s5_c_re=g_c_re, s5_c_im=g_c_im, s5_d=g_d, s5_b_glu=g_bglu))
    dp, yb2, g_ret_gain = _ret_bwd(p, cos, sin, ret_gain, ret_states, ret_scores, ret_o, dx1, wout_e, du, daz,
                                   dep(tok))
    g_win_e = _wgrad_cols(x, norm_even, dp, "wgrad_in_even")
    tok = emit("even_cols", dict(w_in_even=g_win_e))
    g_wout_e = _wgrad_rows([ya2, yb2], dx1, "wgrad_out_even", dep(tok))
    tok = emit("even_rows", dict(w_out_even=g_wout_e))
    dx, g_norm_even = _in_proj_bwd_x(dp, x, norm_even, win_e, dx1, "in_even_bwd", dep(tok))
    emit("last", dict(ret_gn_gain=g_ret_gain, norm_even=g_norm_even, loss=loss))
    return dx


WEIGHTS = ['norm_even', 'w_in_even', 's5_lam_re', 's5_lam_im', 's5_log_dt', 's5_b_re', 's5_b_im', 's5_c_re',
           's5_c_im', 's5_d', 's5_w_glu', 's5_b_glu', 'ret_gn_gain', 'w_out_even', 'norm_odd', 'w_in_odd',
           'sgu_norm_gain', 'sgu_w_spatial', 'sgu_b_spatial', 'w_out_odd', 'final_norm']
BIG = ['w_in_even', 's5_w_glu', 'w_out_even', 'w_in_odd', 'w_out_odd']
SHARDED_SMALL = {'norm_odd': D // NDEV, 'sgu_norm_gain': DI // NDEV}
SCATTER_STAGES = ("odd", "glu", "even_cols", "even_rows")
GATHER_STAGES = ("small_odd", "small_s5")


def _view(n, a):
    if n in ('s5_b_re', 's5_b_im'):
        return jnp.swapaxes(a[0], 1, 2)
    if n == 'final_norm':
        return a.reshape(1, D)
    return a[0] if a.ndim >= 3 else a


def _unview(n, t, shape):
    if n in ('s5_b_re', 's5_b_im'):
        return jnp.swapaxes(t, 1, 2)[None]
    return t.reshape(shape)


def kernel(x, norm_even, w_in_even, s5_lam_re, s5_lam_im, s5_log_dt, s5_b_re, s5_b_im, s5_c_re, s5_c_im, s5_d, s5_w_glu, s5_b_glu, ret_gn_gain, w_out_even, norm_odd, w_in_odd, sgu_norm_gain, sgu_w_spatial, sgu_b_spatial, w_out_odd, final_norm, loss_target, m_norm_even, m_w_in_even, m_s5_lam_re, m_s5_lam_im, m_s5_log_dt, m_s5_b_re, m_s5_b_im, m_s5_c_re, m_s5_c_im, m_s5_d, m_s5_w_glu, m_s5_b_glu, m_ret_gn_gain, m_w_out_even, m_norm_odd, m_w_in_odd, m_sgu_norm_gain, m_sgu_w_spatial, m_sgu_b_spatial, m_w_out_odd, m_final_norm, v_norm_even, v_w_in_even, v_s5_lam_re, v_s5_lam_im, v_s5_log_dt, v_s5_b_re, v_s5_b_im, v_s5_c_re, v_s5_c_im, v_s5_d, v_s5_w_glu, v_s5_b_glu, v_ret_gn_gain, v_w_out_even, v_norm_odd, v_w_in_odd, v_sgu_norm_gain, v_sgu_w_spatial, v_sgu_b_spatial, v_w_out_odd, v_final_norm):
    args = dict(locals())
    w = {n: args[n] for n in WEIGHTS}
    m = {n: args["m_" + n] for n in WEIGHTS}
    v = {n: args["v_" + n] for n in WEIGHTS}
    me = _my_index()

    first_handle, _ = _gather2_start(w['w_in_even'][0].astype(MXU), "gather_first_start")

    def first_weight(after):
        return _gather2_wait(_gather2_forward(first_handle, after, "gather_first_forward"), "gather_first_wait")

    late_own = [w['s5_w_glu'][0].astype(MXU), w['w_out_even'][0].astype(MXU), w['norm_odd'],
                w['w_in_odd'][0].astype(MXU), w['sgu_norm_gain'], w['w_out_odd'][0].astype(MXU)]
    late_handle, start_token = _exchange_start(late_own, False, "gather_late_start")

    def late_weights(after):
        _, (wglu, wout_e, nodd, win_o, sgug, wout_o) = _exchange_wait(late_handle, after, "gather_late_wait")
        return (wglu.reshape(D, D), wout_e.reshape(DI, D), nodd.reshape(1, D), win_o, sgug.reshape(1, DI),
                wout_o.reshape(DI, D))

    pending = {}
    small_last = {}

    def emit(stage, grads):
        if stage == "last":
            small_last.update(grads)
            return None
        names = list(grads)
        handle, token = _exchange_start([grads[n] for n in names], stage in SCATTER_STAGES, stage + "_start")
        pending[stage] = (handle, names)
        return token

    dx = _local_step(
        x[0], loss_target[0], w['norm_even'], first_weight, w['s5_lam_re'][0], w['s5_lam_im'][0], w['s5_log_dt'][0],
        w['s5_b_re'][0], w['s5_b_im'][0], w['s5_c_re'][0], w['s5_c_im'][0], w['s5_d'], w['s5_b_glu'],
        w['ret_gn_gain'], w['sgu_w_spatial'][0], w['sgu_b_spatial'][0], w['final_norm'].reshape(1, D),
        late_weights, emit, start_token)

    out_g, out_d, out_m, out_v = {}, {}, {}, {}
    after = dx
    for stage in SCATTER_STAGES:
        handle, names = pending[stage]
        sent, lands = _exchange_wait(handle, (after,), stage + "_wait")
        for n, land, stack in zip(names, lands, sent):
            shp = w[n].shape
            r, c = shp[1], shp[2]
            own = lax.dynamic_index_in_dim(stack, me, 0, keepdims=False)
            res = _adamw(w[n].reshape(r, c), m[n].reshape(r, c), v[n].reshape(r, c), land, own, "adamw_" + n)
            out_g[n], out_d[n], out_m[n], out_v[n] = (t.reshape(shp) for t in res)
            after = res[0]

    names, owns, lands = [], [], []
    for stage in GATHER_STAGES:
        handle, stage_names = pending[stage]
        sent, got = _exchange_wait(handle, (after,), stage + "_wait")
        names, owns, lands = names + stage_names, owns + sent, lands + got
    last_names = list(small_last)
    last = _exchange([small_last[n] for n in last_names], False, "gather_last")
    loss_parts = last[last_names.index("loss")][:, 0, 0]
    for n, own, land in zip(last_names, (small_last[n] for n in last_names), last):
        if n != "loss":
            names, owns, lands = names + [n], owns + [own], lands + [land]
    for i, n in enumerate(names):
        if n in SHARDED_SMALL:
            width = SHARDED_SMALL[n]
            owns[i] = lax.dynamic_slice_in_dim(owns[i], me * width, width, axis=1)
            lands[i] = lax.dynamic_slice_in_dim(lands[i], me * width, width, axis=2)
    res = _adamw_many([_view(n, w[n]) for n in names], [_view(n, m[n]) for n in names],
                      [_view(n, v[n]) for n in names], lands, owns, "adamw_small")
    for dst, vals in zip((out_g, out_d, out_m, out_v), res):
        for n, t in zip(names, vals):
            dst[n] = _unview(n, t, w[n].shape)

    loss_total = jnp.sum(loss_parts)
    return (loss_total, dx[None], *[out_g[n] for n in WEIGHTS], *[out_d[n] for n in WEIGHTS],
            *[out_m[n] for n in WEIGHTS], *[out_v[n] for n in WEIGHTS])
```

```python
import math

import jax
import jax.numpy as jnp
from jax import lax
from jax.experimental import pallas as pl
from jax.experimental.pallas import tpu as pltpu

F32 = jnp.float32
MXU = jnp.bfloat16
AXES = ("x", "y", "c")
NDEV = 8
D = 1024
NIN = 6144
WIN_BLK = NIN // NDEV
DI = 2048
G, P, HG = 64, 64, 16
GB = 8
NJ = G // GB
SW = GB * P
UW = GB * HG
NSTATE = G * P
HEADS, DK = 4, 256
CH = 128
SG, SGD = 4, 512
EPS = 1e-6
ROPE_BASE = 10000.0
VMEM_CAP_V7X = 64 * 1024 * 1024
LOG_G = [math.log1p(-2.0 ** (-5.0 - h)) for h in range(HEADS)]
GELU_C = math.sqrt(2.0 / math.pi)

ADAM_LR, ADAM_B1, ADAM_B2, ADAM_EPS, ADAM_WD, ADAM_STEP = 0.001, 0.9, 0.999, 1e-08, 0.01, 10
BC1 = 1.0 - ADAM_B1 ** ADAM_STEP
BC2 = 1.0 - ADAM_B2 ** ADAM_STEP

SDS = jax.ShapeDtypeStruct
ARB2 = ("arbitrary", "arbitrary")


def _cp(vmem_mib, sem=None):
    kw = dict(vmem_limit_bytes=min(vmem_mib * 1024 * 1024, VMEM_CAP_V7X - 4 * 1024 * 1024))
    if sem is not None:
        kw["dimension_semantics"] = sem
    return pltpu.CompilerParams(**kw)


def _mm(a, b):
    return jnp.dot(a.astype(MXU), b.astype(MXU), preferred_element_type=F32)


def _mm_nt(a, b):
    return lax.dot_general(a.astype(MXU), b.astype(MXU), (((1,), (1,)), ((), ())), preferred_element_type=F32)


def _mm_tn(a, b):
    return lax.dot_general(a.astype(MXU), b.astype(MXU), (((0,), (0,)), ((), ())), preferred_element_type=F32)


def _gelu(x):
    return _gelu_and_grad(x)[0]


def _gelu_and_grad(x):
    x2 = x * x
    th = jnp.tanh(GELU_C * x * (1.0 + 0.044715 * x2))
    hp = 0.5 * (1.0 + th)
    return x * hp, hp + 0.5 * x * (1.0 - th * th) * GELU_C * (1.0 + 3.0 * 0.044715 * x2)


def _silu_and_grad(x):
    s = jax.nn.sigmoid(x)
    return x * s, s * (1.0 + x * (1.0 - s))


def _full(shape):
    nd = len(shape)
    return pl.BlockSpec(shape, lambda *_: (0,) * nd)


def _rms(xf):
    r = lax.rsqrt(jnp.mean(xf * xf, axis=-1, keepdims=True) + EPS)
    return xf * r, r


ANY_SPEC = pl.BlockSpec(memory_space=pl.ANY)
NO_DEPS = ()


def _load_once(src_hbm, dst_vmem, sem):
    @pl.when(pl.program_id(0) == 0)
    def _():
        cp = pltpu.make_async_copy(src_hbm, dst_vmem, sem)
        cp.start()
        cp.wait()


def _lane_blocks(L):
    return SDS((NJ, L, UW), F32)


def _lane_block_spec(rows, index):
    return pl.BlockSpec((NJ, rows, UW), lambda i: (0, index(i), 0))


def _from_lane_blocks(ref):
    return jnp.concatenate([ref[j] for j in range(NJ)], axis=1)


def _to_lane_blocks(ref, v):
    for j in range(NJ):
        ref[j] = v[:, j * UW:(j + 1) * UW].astype(ref.dtype)


def _in_proj(x, gain, wst, name, deps=NO_DEPS, lane_blocks=False):
    L = x.shape[0]
    tm = min(512, L)

    def body(x_ref, g_ref, w_hbm, *rest):
        outs = rest[len(deps):]
        o_ref, w_scr, sem = outs[0], outs[-2], outs[-1]
        _load_once(w_hbm, w_scr, sem)
        xhat, _ = _rms(x_ref[...])
        h = (xhat * g_ref[...]).astype(MXU)
        for c in range(NDEV):
            o_ref[:, c * WIN_BLK:(c + 1) * WIN_BLK] = jnp.dot(h, w_scr[c], preferred_element_type=F32)
        if lane_blocks:
            _to_lane_blocks(outs[1], o_ref[:, 0:D])

    p_spec, p_shape = pl.BlockSpec((tm, NIN), lambda i: (i, 0)), SDS((L, NIN), F32)
    return pl.pallas_call(
        body, name=name, grid=(L // tm,),
        in_specs=[pl.BlockSpec((tm, D), lambda i: (i, 0)), _full((1, D)), ANY_SPEC] + [ANY_SPEC] * len(deps),
        out_specs=[p_spec, _lane_block_spec(tm, lambda i: i)] if lane_blocks else p_spec,
        out_shape=[p_shape, _lane_blocks(L)] if lane_blocks else p_shape,
        scratch_shapes=[pltpu.VMEM((NDEV, D, WIN_BLK), MXU), pltpu.SemaphoreType.DMA(())],
        compiler_params=_cp(58, ("arbitrary",)),
    )(x, gain, wst, *deps)


def _in_proj_bwd_x(dp, x, gain, wst, dres, name, deps=NO_DEPS):
    L = x.shape[0]
    tm = min(512, L)

    def body(dp_ref, x_ref, g_ref, w_hbm, dres_ref, *rest):
        dx_ref, gg_ref, w_scr, sem = rest[len(deps):]
        _load_once(w_hbm, w_scr, sem)

        @pl.when(pl.program_id(0) == 0)
        def _():
            gg_ref[...] = jnp.zeros_like(gg_ref)

        dh = _mm_nt(dp_ref[:, 0:WIN_BLK], w_scr[0])
        for c in range(1, NDEV):
            dh += _mm_nt(dp_ref[:, c * WIN_BLK:(c + 1) * WIN_BLK], w_scr[c])
        xhat, r = _rms(x_ref[...])
        dxhat = dh * g_ref[...]
        dx_ref[...] = dres_ref[...] + r * (dxhat - xhat * jnp.mean(dxhat * xhat, axis=-1, keepdims=True))
        gg_ref[...] += jnp.sum(dh * xhat, axis=0, keepdims=True)

    row = pl.BlockSpec((tm, D), lambda i: (i, 0))
    return pl.pallas_call(
        body, name=name, grid=(L // tm,),
        in_specs=[pl.BlockSpec((tm, NIN), lambda i: (i, 0)), row, _full((1, D)), ANY_SPEC, row]
        + [ANY_SPEC] * len(deps),
        out_specs=[row, _full((1, D))],
        out_shape=[SDS((L, D), F32), SDS((1, D), F32)],
        scratch_shapes=[pltpu.VMEM((NDEV, D, WIN_BLK), MXU), pltpu.SemaphoreType.DMA(())],
        compiler_params=_cp(56, ("arbitrary",)),
    )(dp, x, gain, wst, dres, *deps)


def _wgrad_cols(x, gain, dp, name, deps=NO_DEPS):
    L = x.shape[0]
    tk = min(1024, L)
    nk = L // tk
    halves = 2
    nh = NDEV // halves

    def body(x_ref, g_ref, dp_ref, *rest):
        o_ref, acc = rest[len(deps):]
        k = pl.program_id(1)

        @pl.when(k == 0)
        def _():
            acc[...] = jnp.zeros_like(acc)

        xhat, _ = _rms(x_ref[...])
        acc[...] += _mm_tn(xhat * g_ref[...], dp_ref[...])

        @pl.when(k == nk - 1)
        def _():
            for c in range(nh):
                o_ref[c] = acc[:, c * WIN_BLK:(c + 1) * WIN_BLK].astype(o_ref.dtype)

    return pl.pallas_call(
        body, name=name, grid=(halves, nk),
        in_specs=[pl.BlockSpec((tk, D), lambda n, k: (k, 0)), _full((1, D)),
                  pl.BlockSpec((tk, nh * WIN_BLK), lambda n, k: (k, n))] + [ANY_SPEC] * len(deps),
        out_specs=pl.BlockSpec((nh, D, WIN_BLK), lambda n, k: (n, 0, 0)),
        out_shape=SDS((NDEV, D, WIN_BLK), MXU),
        scratch_shapes=[pltpu.VMEM((D, nh * WIN_BLK), F32)],
        compiler_params=_cp(56, ARB2),
    )(x, gain, dp, *deps)


def _wgrad_rows(a_parts, b, name, deps=NO_DEPS):
    L, N = b.shape
    na = len(a_parts)
    widths = [a.shape[1] for a in a_parts]
    M = sum(widths)
    tk = min(1024, L)
    nk = L // tk

    def body(*refs):
        a_refs, b_ref = refs[:na], refs[na]
        o_ref, acc = refs[na + 1 + len(deps):]
        k = pl.program_id(0)

        @pl.when(k == 0)
        def _():
            acc[...] = jnp.zeros_like(acc)

        bv = b_ref[...].astype(MXU)
        off = 0
        for a_ref, wd in zip(a_refs, widths):
            acc[off:off + wd, :] += _mm_tn(a_ref[...], bv)
            off += wd

        @pl.when(k == nk - 1)
        def _():
            o_ref[...] = acc[...].astype(o_ref.dtype).reshape(o_ref.shape)

    return pl.pallas_call(
        body, name=name, grid=(nk,),
        in_specs=[pl.BlockSpec((tk, wd), lambda k: (k, 0)) for wd in widths]
        + [pl.BlockSpec((tk, N), lambda k: (k, 0))] + [ANY_SPEC] * len(deps),
        out_specs=_full((NDEV, M // NDEV, N)),
        out_shape=SDS((NDEV, M // NDEV, N), MXU),
        scratch_shapes=[pltpu.VMEM((M, N), F32)],
        compiler_params=_cp(48, ("arbitrary",)),
    )(*a_parts, b, *deps)


def _s5_disc_fn(lr_raw, li, logdt, br, bi):
    lr = jnp.minimum(lr_raw, -1e-4)
    dt = jnp.exp(logdt)
    mag = jnp.exp(lr * dt)
    abr = mag * jnp.cos(li * dt)
    abi = mag * jnp.sin(li * dt)
    den = lr * lr + li * li
    nre = abr - 1.0
    nim = abi
    zr = (nre * lr + nim * li) / den
    zi = (nim * lr - nre * li) / den
    return abr, abi, zr * br - zi * bi, zr * bi + zi * br


def _s5_disc(lr, li, logdt, br, bi):
    def body(lr_ref, li_ref, dt_ref, br_ref, bi_ref, abr_ref, abi_ref, bbr_ref, bbi_ref):
        abr, abi, bbr, bbi = _s5_disc_fn(lr_ref[...], li_ref[...], dt_ref[...], br_ref[...], bi_ref[...])
        abr_ref[...] = abr
        abi_ref[...] = abi
        bbr_ref[...] = bbr
        bbi_ref[...] = bbi

    s1, s3 = SDS((G, 1, P), F32), SDS((G, HG, P), F32)
    return pl.pallas_call(body, name="s5_disc", out_shape=[s1, s1, s3, s3])(lr, li, logdt, br, bi)


def _s5_disc_bwd(lr, li, logdt, br, bi, dabr, dabi, dbbr, dbbi):
    def body(lr_ref, li_ref, dt_ref, br_ref, bi_ref, c0, c1, c2, c3, o0, o1, o2, o3, o4):
        _, vjp = jax.vjp(_s5_disc_fn, lr_ref[...], li_ref[...], dt_ref[...], br_ref[...], bi_ref[...])
        g = vjp((c0[...], c1[...], c2[...], c3[...]))
        for o, v in zip((o0, o1, o2, o3, o4), g):
            o[...] = v

    s1, s3 = SDS((G, 1, P), F32), SDS((G, HG, P), F32)
    return pl.pallas_call(body, name="s5_disc_bwd", out_shape=[s1, s1, SDS((G, 1, 1), F32), s3, s3])(
        lr, li, logdt, br, bi, dabr, dabi, dbbr, dbbi)


def _s5_tables(abr, abi, rows, name):
    def body(ar_ref, ai_ref, pfr, pfi, pbr, pbi):
        pfr[0:1, :] = ar_ref[...]
        pfi[0:1, :] = ai_ref[...]
        pbr[rows - 1:rows, :] = ar_ref[...]
        pbi[rows - 1:rows, :] = ai_ref[...]
        n = 1
        while n < rows:
            er, ei = pfr[n - 1:n, :], pfi[n - 1:n, :]
            xr, xi = pfr[0:n, :], pfi[0:n, :]
            pfr[n:2 * n, :] = er * xr - ei * xi
            pfi[n:2 * n, :] = er * xi + ei * xr
            yr, yi = pbr[rows - n:rows, :], pbi[rows - n:rows, :]
            pbr[rows - 2 * n:rows - n, :] = er * yr - ei * yi
            pbi[rows - 2 * n:rows - n, :] = er * yi + ei * yr
            n *= 2

    s = SDS((rows, NSTATE), F32)
    return pl.pallas_call(body, name=name, out_shape=[s, s, s, s], compiler_params=_cp(40))(abr, abi)


def _cscan(br, bi, pr_ref, pi_ref, reverse):
    T = br.shape[0]
    sign = -1.0 if reverse else 1.0
    row = lax.broadcasted_iota(jnp.int32, br.shape, 0)
    k = 1
    while k < T:
        akr = pr_ref[k - 1:k, :]
        aki = sign * pi_ref[k - 1:k, :]

        def shift(v):
            if k % 8 == 0:
                z = jnp.zeros((k, v.shape[1]), v.dtype)
                return jnp.concatenate([v[k:], z], 0) if reverse else jnp.concatenate([z, v[:T - k]], 0)
            if reverse:
                return jnp.where(row < T - k, pltpu.roll(v, T - k, 0), 0.0)
            return jnp.where(row >= k, pltpu.roll(v, k, 0), 0.0)

        sr, si = shift(br), shift(bi)
        br, bi = br + akr * sr - aki * si, bi + akr * si + aki * sr
        k *= 2
    return br, bi


def _embed(t):
    a, b = t.shape[1], t.shape[2]
    return jnp.einsum("jgab,gh->jgahb", t.reshape(NJ, GB, a, b), jnp.eye(GB, dtype=t.dtype)).reshape(NJ, GB * a, GB * b)


def _diag_blocks(t, a, b):
    return jnp.einsum("jgahb,gh->jgab", t.reshape(NJ, GB, a, GB, b), jnp.eye(GB, dtype=t.dtype)).reshape(G, a, b)


NT = 16


def _chunks(L):
    ncb = min(CH, L // NT)
    return ncb, NT * ncb


def _cmul_add(ar, ai, xr, xi, br, bi):
    return ar * xr - ai * xi + br, ar * xi + ai * xr + bi


def _pow_weights(w_ref, pwr_ref, pwi_ref, dst, adjoint):
    w = w_ref[...].astype(F32)
    wr, wi = w[:, :SW], w[:, SW:]
    for t in range(NT):
        k = t if adjoint else NT - 1 - t
        if k == 0:
            blk = w
        else:
            pr, pi = pwr_ref[k - 1:k, :], pwi_ref[k - 1:k, :]
            if adjoint:
                blk = jnp.concatenate([pr * wr + pi * wi, pr * wi - pi * wr], axis=1)
            else:
                blk = jnp.concatenate([pr * wr - pi * wi, pr * wi + pi * wr], axis=1)
        dst[t * UW:(t + 1) * UW, :] = blk.astype(dst.dtype)


def _slices(ref, ncb, axis):
    return jnp.concatenate([ref[pl.ds(t, ncb, stride=NT), :] for t in range(NT)], axis=axis)


def _s5_scan_fwd(u3, bb, cm, pwr, pwi, pwrt, pwit, par, pai, dskip):
    L = u3.shape[1]
    ncb, tb = _chunks(L)
    nb = L // tb

    def build(bb_ref, cm_ref, pwr_ref, pwi_ref, pct_ref, pit_ref, bbp_scr, ktp_scr, zc_scr):
        w = bb_ref[...].astype(F32)
        wr, wi = w[:, :SW], w[:, SW:]
        cmv = cm_ref[...].astype(F32)
        ct, cb = cmv[:SW, :], cmv[SW:, :]
        zero = jnp.zeros((UW, UW), MXU)
        for tau in range(NT):
            if tau == 0:
                blk = w
            else:
                pr, pi = pwr_ref[tau - 1:tau, :], pwi_ref[tau - 1:tau, :]
                blk = jnp.concatenate([pr * wr - pi * wi, pr * wi + pi * wr], axis=1)
            blk = blk.astype(MXU)
            bbp_scr[(NT - 1 - tau) * UW:(NT - tau) * UW, :] = blk
            k = jnp.dot(blk, cm_ref[...], preferred_element_type=F32).astype(MXU)
            for j in range(NT - tau):
                ktp_scr[j * UW:(j + 1) * UW, (j + tau) * UW:(j + tau + 1) * UW] = k
            pc, pic = pct_ref[:, tau:tau + 1], pit_ref[:, tau:tau + 1]
            zc_scr[:, tau * UW:(tau + 1) * UW] = jnp.concatenate(
                [pc * ct + pic * cb, pc * cb - pic * ct], axis=0).astype(MXU)
        for j in range(NT):
            for t in range(j):
                ktp_scr[j * UW:(j + 1) * UW, t * UW:(t + 1) * UW] = zero

    def body(u_ref, bb_ref, cm_ref, pwr_ref, pwi_ref, pct_ref, pit_ref, par_ref, pai_ref, d_ref, ypre_ref, cin_ref,
             carry, cin_scr, bbp_scr, ktp_scr, zc_scr):
        @pl.when(pl.program_id(1) == 0)
        def _():
            carry[...] = jnp.zeros_like(carry)
            build(bb_ref, cm_ref, pwr_ref, pwi_ref, pct_ref, pit_ref, bbp_scr, ktp_scr, zc_scr)

        c = carry[...]
        ucat = _slices(u_ref, ncb, 1)
        ub = ucat.astype(MXU)
        e = jnp.dot(ub, bbp_scr[...], preferred_element_type=F32)
        xr, xi = _cscan(e[:, :SW], e[:, SW:], par_ref, pai_ref, False)
        cr, ci = c[:, :SW], c[:, SW:]
        fr, fi = _cmul_add(par_ref[0:ncb, :], pai_ref[0:ncb, :], cr, ci, xr, xi)
        carry[...] = jnp.concatenate([fr[ncb - 1:ncb, :], fi[ncb - 1:ncb, :]], axis=1)
        row = lax.broadcasted_iota(jnp.int32, fr.shape, 0)
        cin_ref[:, :SW] = jnp.where(row >= 1, pltpu.roll(fr, 1, 0), cr)
        cin_ref[:, SW:] = jnp.where(row >= 1, pltpu.roll(fi, 1, 0), ci)
        cin_scr[...] = cin_ref[...].astype(MXU)
        y = (jnp.dot(ub, ktp_scr[...], preferred_element_type=F32)
             + jnp.dot(cin_scr[...], zc_scr[...], preferred_element_type=F32)
             + jnp.tile(d_ref[...], (1, NT)) * ucat)
        for t in range(NT):
            ypre_ref[pl.ds(t, ncb, stride=NT), :] = y[:, t * UW:(t + 1) * UW]

    tab = pl.BlockSpec((CH, SW), lambda j, i: (0, j))
    stp = pl.BlockSpec((NT, SW), lambda j, i: (0, j))
    stt = pl.BlockSpec((SW, NT), lambda j, i: (j, 0))
    vec = lambda w: pl.BlockSpec((1, w), lambda j, i: (0, j))
    return pl.pallas_call(
        body, name="s5_scan_fwd", grid=(NJ, nb),
        in_specs=[pl.BlockSpec((None, tb, UW), lambda j, i: (j, i, 0)),
                  pl.BlockSpec((None, UW, 2 * SW), lambda j, i: (j, 0, 0)),
                  pl.BlockSpec((None, 2 * SW, UW), lambda j, i: (j, 0, 0)),
                  stp, stp, stt, stt, tab, tab, vec(UW)],
        out_specs=[pl.BlockSpec((None, tb, UW), lambda j, i: (j, i, 0)),
                   pl.BlockSpec((None, ncb, 2 * SW), lambda j, i: (j, i, 0))],
        out_shape=[_lane_blocks(L), SDS((NJ, L // NT, 2 * SW), F32)],
        scratch_shapes=[pltpu.VMEM((1, 2 * SW), F32), pltpu.VMEM((ncb, 2 * SW), MXU),
                        pltpu.VMEM((NT * UW, 2 * SW), MXU), pltpu.VMEM((NT * UW, NT * UW), MXU),
                        pltpu.VMEM((2 * SW, NT * UW), MXU)],
        compiler_params=_cp(56, ARB2),
    )(u3, bb, cm, pwr, pwi, pwrt, pwit, par, pai, dskip)


def _s5_gate_bwd(ypre3, p, dx1, wout_e, wglu, bglu, deps=NO_DEPS):
    L = p.shape[0]
    tm = min(256, L)

    def body(y_ref, az_ref, dx1_ref, wo_ref, wg_ref, bg_ref, *rest):
        dyp_ref, daz_ref, yg_ref, dt_ref, ya_ref, gbg_ref = rest[len(deps):]

        @pl.when(pl.program_id(0) == 0)
        def _():
            gbg_ref[...] = jnp.zeros_like(gbg_ref)

        yg, dgelu = _gelu_and_grad(_from_lane_blocks(y_ref))
        sg = jax.nn.sigmoid(_mm(yg, wg_ref[...]) + bg_ref[...])
        act, dact = _silu_and_grad(az_ref[...])
        y2 = yg * sg
        dya = _mm_nt(dx1_ref[...], wo_ref[...])
        daz_ref[...] = (dya * y2 * dact).astype(daz_ref.dtype)
        dy2 = dya * act
        dt = dy2 * yg * sg * (1.0 - sg)
        dyg = dy2 * sg + _mm_nt(dt, wg_ref[...])
        _to_lane_blocks(dyp_ref, dyg * dgelu)
        yg_ref[...] = yg.astype(yg_ref.dtype)
        dt_ref[...] = dt.astype(dt_ref.dtype)
        ya_ref[...] = (y2 * act).astype(ya_ref.dtype)
        gbg_ref[...] += jnp.sum(dt, axis=0, keepdims=True)

    row = pl.BlockSpec((tm, D), lambda i: (i, 0))
    return pl.pallas_call(
        body, name="s5_gate_bwd", grid=(L // tm,),
        in_specs=[_lane_block_spec(tm, lambda i: i), pl.BlockSpec((tm, D), lambda i: (i, 1)), row,
                  pl.BlockSpec((D, D), lambda i: (0, 0)), _full((D, D)), _full((1, D))] + [ANY_SPEC] * len(deps),
        out_specs=[_lane_block_spec(tm, lambda i: i), row, row, row, row, _full((1, D))],
        out_shape=[_lane_blocks(L), SDS((L, D), MXU), SDS((L, D), MXU), SDS((L, D), MXU), SDS((L, D), MXU),
                   SDS((1, D), F32)],
        compiler_params=_cp(40, ("arbitrary",)),
    )(ypre3, p, dx1, wout_e, wglu, bglu, *deps)


def _s5_scan_bwd(u3, dypre3, states, bb, cm, cmt, abr, abi, pwr, pwi, par, pai, pbr, pbi, dskip, deps=NO_DEPS):
    L = u3.shape[1]
    ncb, tb = _chunks(L)
    nb = L // tb
    rev = lambda i: nb - 1 - i

    def body(u_ref, dy_ref, st_ref, bb_ref, cm_ref, cmt_ref, ar_ref, ai_ref, pwr_ref, pwi_ref, par_ref, pai_ref,
             pbr_ref, pbi_ref, d_ref, *rest):
        (du_ref, gd_ref, gcm_ref, gbb_ref, gar_ref, gai_ref,
         lcarry, s_scr, gs_scr, cmp_scr) = rest[len(deps):]
        del cm_ref

        @pl.when(pl.program_id(1) == 0)
        def _():
            _pow_weights(cmt_ref, pwr_ref, pwi_ref, cmp_scr, True)
            lcarry[...] = jnp.zeros_like(lcarry)
            gd_ref[...] = jnp.zeros_like(gd_ref)
            gcm_ref[...] = jnp.zeros_like(gcm_ref)
            gbb_ref[...] = jnp.zeros_like(gbb_ref)
            gar_ref[...] = jnp.zeros_like(gar_ref)
            gai_ref[...] = jnp.zeros_like(gai_ref)

        ar, ai = ar_ref[...], ai_ref[...]
        c = st_ref[...]
        s_scr[0] = c
        sr, si = c[:, :SW], c[:, SW:]
        for t in range(NT):
            bu = _mm(u_ref[pl.ds(t, ncb, stride=NT), :], bb_ref[...])
            sr, si = _cmul_add(ar, ai, sr, si, bu[:, :SW], bu[:, SW:])
            s_scr[t + 1] = jnp.concatenate([sr, si], axis=1)
        dys = [dy_ref[pl.ds(t, ncb, stride=NT), :] for t in range(NT)]
        for t in range(NT):
            gs_scr[t] = _mm(dys[t], cmt_ref[...])
        f = _mm(jnp.concatenate(dys, axis=1), cmp_scr[...])
        xr, xi = _cscan(f[:, :SW], f[:, SW:], par_ref, pai_ref, True)
        lc = lcarry[...]
        lcr, lci = lc[:, :SW], lc[:, SW:]
        hr, hi = _cmul_add(pbr_ref[CH - ncb:CH, :], -pbi_ref[CH - ncb:CH, :], lcr, lci, xr, xi)
        lcarry[...] = jnp.concatenate([hr[0:1, :], hi[0:1, :]], axis=1)
        row = lax.broadcasted_iota(jnp.int32, hr.shape, 0)
        lr_ = jnp.where(row < ncb - 1, pltpu.roll(hr, ncb - 1, 0), lcr)
        li_ = jnp.where(row < ncb - 1, pltpu.roll(hi, ncb - 1, 0), lci)
        gar = jnp.zeros((1, SW), F32)
        gai = jnp.zeros((1, SW), F32)
        for t in reversed(range(NT)):
            gs = gs_scr[t]
            lr_, li_ = _cmul_add(ar, -ai, lr_, li_, gs[:, :SW], gs[:, SW:])
            rows = pl.ds(t, ncb, stride=NT)
            u_t, dy_t = u_ref[rows, :], dy_ref[rows, :]
            lam = jnp.concatenate([lr_, li_], axis=1)
            gbb_ref[...] += _mm_tn(u_t, lam)
            du_ref[rows, :] = _mm_nt(lam, bb_ref[...]) + dy_t * d_ref[...]
            gd_ref[...] += jnp.sum(dy_t * u_t, axis=0, keepdims=True)
            gcm_ref[...] += _mm_tn(s_scr[t + 1], dy_t)
            sp = s_scr[t]
            spr, spi = sp[:, :SW], sp[:, SW:]
            gar += jnp.sum(lr_ * spr + li_ * spi, axis=0, keepdims=True)
            gai += jnp.sum(li_ * spr - lr_ * spi, axis=0, keepdims=True)
        gar_ref[...] += gar
        gai_ref[...] += gai

    tab = pl.BlockSpec((CH, SW), lambda j, i: (0, j))
    stp = pl.BlockSpec((NT, SW), lambda j, i: (0, j))
    colblk = pl.BlockSpec((None, tb, UW), lambda j, i: (j, rev(i), 0))
    vec = lambda w: pl.BlockSpec((1, w), lambda j, i: (0, j))
    return pl.pallas_call(
        body, name="s5_scan_bwd", grid=(NJ, nb),
        in_specs=[colblk, colblk,
                  pl.BlockSpec((None, ncb, 2 * SW), lambda j, i: (j, rev(i), 0)),
                  pl.BlockSpec((None, UW, 2 * SW), lambda j, i: (j, 0, 0)),
                  pl.BlockSpec((None, 2 * SW, UW), lambda j, i: (j, 0, 0)),
                  pl.BlockSpec((None, UW, 2 * SW), lambda j, i: (j, 0, 0)),
                  vec(SW), vec(SW), stp, stp, tab, tab, tab, tab, vec(UW)] + [ANY_SPEC] * len(deps),
        out_specs=[colblk, vec(UW),
                   pl.BlockSpec((None, 2 * SW, UW), lambda j, i: (j, 0, 0)),
                   pl.BlockSpec((None, UW, 2 * SW), lambda j, i: (j, 0, 0)),
                   vec(SW), vec(SW)],
        out_shape=[_lane_blocks(L), SDS((1, D), F32),
                   SDS((NJ, 2 * SW, UW), F32), SDS((NJ, UW, 2 * SW), F32),
                   SDS((1, NSTATE), F32), SDS((1, NSTATE), F32)],
        scratch_shapes=[pltpu.VMEM((1, 2 * SW), F32), pltpu.VMEM((NT + 1, ncb, 2 * SW), F32),
                        pltpu.VMEM((NT, ncb, 2 * SW), F32), pltpu.VMEM((NT * UW, 2 * SW), MXU)],
        compiler_params=_cp(56, ARB2),
    )(u3, dypre3, states, bb, cm, cmt, abr, abi, pwr, pwi, par, pai, pbr, pbi, dskip, *deps)


def _rope_tables(L, inv):
    tm = min(512, L)

    def body(inv_ref, cos_ref, sin_ref):
        pos = (lax.broadcasted_iota(jnp.int32, (tm, DK // 2), 0) + pl.program_id(0) * tm).astype(F32)
        ang = pos * inv_ref[...]
        cos_ref[...] = jnp.cos(ang)
        sin_ref[...] = jnp.sin(ang)

    blk = pl.BlockSpec((tm, DK // 2), lambda i: (i, 0))
    return pl.pallas_call(body, name="rope_tables", grid=(L // tm,), in_specs=[_full((1, DK // 2))],
                          out_specs=[blk, blk], out_shape=[SDS((L, DK // 2), F32)] * 2)(inv)


def _rot(x, cos, sin):
    x1, x2 = x[:, :DK // 2], x[:, DK // 2:]
    return jnp.concatenate([x1 * cos - x2 * sin, x1 * sin + x2 * cos], axis=1)


def _unrot(d, cos, sin):
    d1, d2 = d[:, :DK // 2], d[:, DK // 2:]
    return jnp.concatenate([d1 * cos + d2 * sin, d2 * cos - d1 * sin], axis=1)


def _ret_decays(h):
    lg = LOG_G[h]
    n = lax.broadcasted_iota(jnp.int32, (CH, CH), 0)
    m = lax.broadcasted_iota(jnp.int32, (CH, CH), 1)
    diff = (n - m).astype(F32)
    decay = jnp.where(n >= m, jnp.exp(lg * jnp.maximum(diff, 0.0)), 0.0)
    idx = lax.broadcasted_iota(jnp.int32, (CH, 1), 0).astype(F32)
    xi = jnp.exp(lg * (idx + 1.0))
    zeta = jnp.exp(lg * (CH - 1.0 - idx))
    return decay, xi, zeta, math.exp(lg * CH)


def _ret_tables(dec_scr, vec_scr):
    for h in range(HEADS):
        decay, xi, zeta, _ = _ret_decays(h)
        dec_scr[h] = decay
        vec_scr[h] = jnp.concatenate([jnp.broadcast_to(xi, (CH, 128)), jnp.broadcast_to(zeta, (CH, 128))], axis=1)


def _group_norm(o):
    mu = jnp.mean(o, axis=-1, keepdims=True)
    oc = o - mu
    rstd = lax.rsqrt(jnp.mean(oc * oc, axis=-1, keepdims=True) + EPS)
    return oc * rstd, rstd


def _ret_fwd(p, cos, sin, gain):
    L = p.shape[0]
    nb = L // CH

    def body(q_ref, k_ref, v_ref, bz_ref, cos_ref, sin_ref, g_ref, yb_ref, st_ref, sc_ref, o_ref,
             state, dec_scr, vec_scr):
        @pl.when(pl.program_id(0) == 0)
        def _():
            state[...] = jnp.zeros_like(state)
            _ret_tables(dec_scr, vec_scr)

        cos, sin = cos_ref[...], sin_ref[...]
        act, _ = _silu_and_grad(bz_ref[...])
        for h in range(HEADS):
            hs = slice(h * DK, (h + 1) * DK)
            xi, zeta = vec_scr[h, :, 0:1], vec_scr[h, :, 128:129]
            s_prev = state[h]
            s_prev_b = s_prev.astype(MXU)
            st_ref[h] = s_prev_b
            v = v_ref[:, hs]
            qr = _rot(q_ref[:, hs], cos, sin)
            kr = _rot(k_ref[:, hs], cos, sin) * (DK ** -0.5)
            scores = (_mm_nt(qr, kr) * dec_scr[h]).astype(MXU)
            o = _mm(scores, v) + _mm(qr * xi, s_prev_b)
            sc_ref[:, h * CH:(h + 1) * CH] = scores
            o_ref[:, hs] = o
            state[h] = s_prev * math.exp(LOG_G[h] * CH) + _mm_tn(kr * zeta, v)
            on, _ = _group_norm(o)
            yb_ref[:, hs] = (on * g_ref[:, hs] * act[:, hs]).astype(yb_ref.dtype)

    col = lambda c: pl.BlockSpec((CH, D), lambda i: (i, c))
    rope = pl.BlockSpec((CH, DK // 2), lambda i: (i, 0))
    return pl.pallas_call(
        body, name="ret_fwd", grid=(nb,),
        in_specs=[col(2), col(3), col(4), col(5), rope, rope, _full((1, D))],
        out_specs=[pl.BlockSpec((CH, D), lambda i: (i, 0)),
                   pl.BlockSpec((None, HEADS, DK, DK), lambda i: (i, 0, 0, 0)),
                   pl.BlockSpec((CH, HEADS * CH), lambda i: (i, 0)), pl.BlockSpec((CH, D), lambda i: (i, 0))],
        out_shape=[SDS((L, D), MXU), SDS((nb, HEADS, DK, DK), MXU), SDS((L, HEADS * CH), MXU), SDS((L, D), F32)],
        scratch_shapes=[pltpu.VMEM((HEADS, DK, DK), F32), pltpu.VMEM((HEADS, CH, CH), F32),
                        pltpu.VMEM((HEADS, CH, 256), F32)],
        compiler_params=_cp(40, ("arbitrary",)),
    )(p, p, p, p, cos, sin, gain)


def _ret_bwd(p, cos, sin, gain, states, scores, o, dx1, wout_e, du, daz, deps=NO_DEPS):
    L = p.shape[0]
    nb = L // CH
    rev = lambda i: nb - 1 - i

    def body(q_ref, k_ref, v_ref, bz_ref, cos_ref, sin_ref, g_ref, st_ref, sc_ref, o_ref, dx1_ref, wo_ref, du_ref,
             daz_ref, *rest):
        dp_ref, yb_ref, gg_ref, gstate, dec_scr, vec_scr = rest[len(deps):]

        @pl.when(pl.program_id(0) == 0)
        def _():
            gstate[...] = jnp.zeros_like(gstate)
            gg_ref[...] = jnp.zeros_like(gg_ref)
            _ret_tables(dec_scr, vec_scr)

        cos, sin = cos_ref[...], sin_ref[...]
        act, dact = _silu_and_grad(bz_ref[...])
        dyb = _mm_nt(dx1_ref[...], wo_ref[...])
        dp_ref[:, 0:D] = _from_lane_blocks(du_ref).astype(dp_ref.dtype)
        dp_ref[:, D:2 * D] = daz_ref[...]
        for h in range(HEADS):
            hs = slice(h * DK, (h + 1) * DK)
            col = lambda part: slice((2 + part) * D + h * DK, (2 + part) * D + (h + 1) * DK)
            decay = dec_scr[h]
            xi, zeta = vec_scr[h, :, 0:1], vec_scr[h, :, 128:129]
            v = v_ref[:, hs]
            s_prev_b = st_ref[h]
            qr = _rot(q_ref[:, hs], cos, sin)
            kr = _rot(k_ref[:, hs], cos, sin) * (DK ** -0.5)
            scores = sc_ref[:, h * CH:(h + 1) * CH]
            on, rstd = _group_norm(o_ref[:, hs])
            gain_h = g_ref[:, hs]
            out = on * gain_h
            yb_ref[:, hs] = (out * act[:, hs]).astype(yb_ref.dtype)
            dyb_h = dyb[:, hs]
            dp_ref[:, col(3)] = (dyb_h * out * dact[:, hs]).astype(dp_ref.dtype)
            dout = dyb_h * act[:, hs]
            gg_ref[:, hs] += jnp.sum(dout * on, axis=0, keepdims=True)
            don = dout * gain_h
            do = rstd * (don - jnp.mean(don, axis=-1, keepdims=True)
                         - on * jnp.mean(don * on, axis=-1, keepdims=True))
            gnext = gstate[h]
            gnext_b = gnext.astype(MXU)
            dscores = _mm_nt(do, v) * decay
            dp_ref[:, col(2)] = (_mm_tn(scores, do) + _mm(kr * zeta, gnext_b)).astype(dp_ref.dtype)
            dqr = _mm(dscores, kr) + _mm_nt(do, s_prev_b) * xi
            dkr = _mm_tn(dscores, qr) + _mm_nt(v, gnext_b) * zeta
            gstate[h] = gnext * math.exp(LOG_G[h] * CH) + _mm_tn(qr * xi, do)
            dp_ref[:, col(0)] = _unrot(dqr, cos, sin).astype(dp_ref.dtype)
            dp_ref[:, col(1)] = (_unrot(dkr, cos, sin) * (DK ** -0.5)).astype(dp_ref.dtype)

    col = lambda c: pl.BlockSpec((CH, D), lambda i: (rev(i), c))
    rope = pl.BlockSpec((CH, DK // 2), lambda i: (rev(i), 0))
    outc = col(0)
    act_out = SDS((L, D), MXU)
    return pl.pallas_call(
        body, name="ret_bwd", grid=(nb,),
        in_specs=[col(2), col(3), col(4), col(5), rope, rope, _full((1, D)),
                  pl.BlockSpec((None, HEADS, DK, DK), lambda i: (rev(i), 0, 0, 0)),
                  pl.BlockSpec((CH, HEADS * CH), lambda i: (rev(i), 0)), outc,
                  outc, pl.BlockSpec((D, D), lambda i: (1, 0)), _lane_block_spec(CH, rev), outc]
        + [ANY_SPEC] * len(deps),
        out_specs=[pl.BlockSpec((CH, NIN), lambda i: (rev(i), 0)), outc, _full((1, D))],
        out_shape=[SDS((L, NIN), MXU), act_out, SDS((1, D), F32)],
        scratch_shapes=[pltpu.VMEM((HEADS, DK, DK), F32), pltpu.VMEM((HEADS, CH, CH), F32),
                        pltpu.VMEM((HEADS, CH, 256), F32)],
        compiler_params=_cp(48, ("arbitrary",)),
    )(p, p, p, p, cos, sin, gain, states, scores, o, dx1, wout_e, du, daz, *deps)


def _out_even(x, ypre3, p, yb, wglu, bglu, wout):
    L = x.shape[0]
    tm = min(512, L)

    def body(x_ref, y_ref, az_ref, yb_ref, wg_ref, bg_ref, w_ref, o_ref):
        yg = _gelu(_from_lane_blocks(y_ref))
        t = _mm(yg, wg_ref[...]) + bg_ref[...]
        act, _ = _silu_and_grad(az_ref[...])
        ya = (yg * jax.nn.sigmoid(t) * act).astype(MXU)
        cat = jnp.concatenate([ya, yb_ref[...]], axis=1)
        o_ref[...] = x_ref[...] + jnp.dot(cat, w_ref[...], preferred_element_type=F32)

    row = pl.BlockSpec((tm, D), lambda i: (i, 0))
    return pl.pallas_call(
        body, name="out_even", grid=(L // tm,),
        in_specs=[row, _lane_block_spec(tm, lambda i: i), pl.BlockSpec((tm, D), lambda i: (i, 1)), row,
                  _full((D, D)), _full((1, D)), _full((DI, D))],
        out_specs=row, out_shape=SDS((L, D), F32), compiler_params=_cp(48, ("arbitrary",)),
    )(x, ypre3, p, yb, wglu, bglu, wout)


def _sgu_core(pv, gain, ws_ref, bs_ref):
    pu, pvv, z = pv[:, :DI], pv[:, DI:2 * DI], pv[:, 2 * DI:]
    u, gu = _gelu_and_grad(pu)
    v, gv = _gelu_and_grad(pvv)
    mu = jnp.mean(v, axis=-1, keepdims=True)
    vc = v - mu
    rstd = lax.rsqrt(jnp.mean(vc * vc, axis=-1, keepdims=True) + EPS)
    vhat = vc * rstd
    vn = vhat * gain
    t = lax.broadcasted_iota(jnp.int32, (CH, CH), 0)
    s_ = lax.broadcasted_iota(jnp.int32, (CH, CH), 1)
    mask = t >= s_
    wm = [jnp.where(mask, ws_ref[g], 0.0).astype(MXU) for g in range(SG)]
    s = jnp.concatenate([_mm(wm[g], vn[:, g * SGD:(g + 1) * SGD]) + bs_ref[g] for g in range(SG)], axis=1)
    return gu, gv, z, u, vhat, rstd, vn, mask, wm, s


def _sgu_fwd_bwd(p2, x1, gain, wsp, bsp, wout, fnorm, tgt):
    L = p2.shape[0]

    def body(p_ref, x1_ref, g_ref, ws_ref, bs_ref, wo_ref, fn_ref, t_ref,
             dp_ref, y_ref, dx2_ref, gg_ref, gws_ref, gbs_ref, gfn_ref, loss_ref):
        @pl.when(pl.program_id(0) == 0)
        def _():
            gg_ref[...] = jnp.zeros_like(gg_ref)
            gws_ref[...] = jnp.zeros_like(gws_ref)
            gbs_ref[...] = jnp.zeros_like(gbs_ref)
            gfn_ref[...] = jnp.zeros_like(gfn_ref)
            loss_ref[...] = jnp.zeros_like(loss_ref)

        gain = g_ref[...]
        gu, gv, z, u, vhat, rstd, vn, mask, wm, s = _sgu_core(p_ref[...], gain, ws_ref, bs_ref)
        act, dact = _silu_and_grad(z)
        y = (u * s * act).astype(MXU)
        y_ref[...] = y
        x2 = x1_ref[...] + jnp.dot(y, wo_ref[...], preferred_element_type=F32)
        xhat, r = _rms(x2)
        fn = fn_ref[...]
        e = xhat * fn - t_ref[...]
        loss_ref[...] += 0.5 * jnp.sum(jnp.mean(e * e, axis=-1, keepdims=True), axis=0, keepdims=True)
        do = e * (1.0 / D)
        gfn_ref[...] += jnp.sum(do * xhat, axis=0, keepdims=True)
        dxhat = do * fn
        dx2 = r * (dxhat - xhat * jnp.mean(dxhat * xhat, axis=-1, keepdims=True))
        dx2_ref[...] = dx2
        dy = _mm_nt(dx2, wo_ref[...])
        du = dy * s * act
        ds = dy * u * act
        dz = dy * u * s * dact
        dvn = []
        for g in range(SG):
            ds_g = ds[:, g * SGD:(g + 1) * SGD]
            vn_g = vn[:, g * SGD:(g + 1) * SGD]
            gbs_ref[g] += jnp.sum(ds_g, axis=1, keepdims=True)
            gws_ref[g] += jnp.where(mask, _mm_nt(ds_g, vn_g), 0.0)
            dvn.append(_mm_tn(wm[g], ds_g))
        dvn = jnp.concatenate(dvn, axis=1)
        gg_ref[...] += jnp.sum(dvn * vhat, axis=0, keepdims=True)
        dvhat = dvn * gain
        dv = rstd * (dvhat - jnp.mean(dvhat, axis=-1, keepdims=True)
                     - vhat * jnp.mean(dvhat * vhat, axis=-1, keepdims=True))
        dp_ref[...] = jnp.concatenate([du * gu, dv * gv, dz], axis=1).astype(dp_ref.dtype)

    row = pl.BlockSpec((CH, D), lambda i: (i, 0))
    wide = pl.BlockSpec((CH, NIN), lambda i: (i, 0))
    return pl.pallas_call(
        body, name="sgu_fwd_bwd", grid=(L // CH,),
        in_specs=[wide, row, _full((1, DI)), _full((SG, CH, CH)), _full((SG, CH, 1)), _full((DI, D)),
                  _full((1, D)), row],
        out_specs=[wide, pl.BlockSpec((CH, DI), lambda i: (i, 0)), row,
                   _full((1, DI)), _full((SG, CH, CH)), _full((SG, CH, 1)), _full((1, D)), _full((1, 128))],
        out_shape=[SDS((L, NIN), MXU), SDS((L, DI), MXU), SDS((L, D), F32), SDS((1, DI), F32),
                   SDS((SG, CH, CH), F32), SDS((SG, CH, 1), F32), SDS((1, D), F32), SDS((1, 128), F32)],
        compiler_params=_cp(48, ("arbitrary",)),
    )(p2, x1, gain, wsp, bsp, wout, fnorm, tgt)


def _my_index():
    return 4 * lax.axis_index("x") + 2 * lax.axis_index("y") + lax.axis_index("c")


def _ordered_sum(land_ref, own, me):
    g = None
    for s in range(NDEV):
        part = jnp.where(me == s, own, land_ref[s].astype(F32))
        g = part if g is None else g + part
    return g


def _adamw_math(w, m, v, g):
    mn = ADAM_B1 * m + (1.0 - ADAM_B1) * g
    vn = ADAM_B2 * v + (1.0 - ADAM_B2) * (g * g)
    mhat = mn / BC1
    vhat = vn / BC2
    return g, -ADAM_LR * (mhat / (jnp.sqrt(vhat) + ADAM_EPS) + ADAM_WD * w), mn, vn


def _adamw(w, m, v, land, own, name):
    R, C = w.shape
    tr = R
    for cand in (256, 128, 64, 32, 16, 8):
        if R % cand == 0 and R > cand:
            tr = cand
            break

    def body(w_ref, m_ref, v_ref, land_ref, own_ref, g_ref, d_ref, mo_ref, vo_ref):
        g = _ordered_sum(land_ref, own_ref[...].astype(F32), _my_index())
        for o, val in zip((g_ref, d_ref, mo_ref, vo_ref), _adamw_math(w_ref[...], m_ref[...], v_ref[...], g)):
            o[...] = val

    blk = pl.BlockSpec((tr, C), lambda i: (i, 0))
    out = SDS((R, C), F32)
    return pl.pallas_call(
        body, name=name, grid=(R // tr,),
        in_specs=[blk, blk, blk, pl.BlockSpec((NDEV, tr, C), lambda i: (0, i, 0)), blk],
        out_specs=[blk, blk, blk, blk], out_shape=[out, out, out, out],
        compiler_params=_cp(40, ("arbitrary",)),
    )(w, m, v, land, own)


def _adamw_many(ws, ms, vs, lands, owns, name):
    k = len(ws)

    def body(*refs):
        ins, outs = refs[:5 * k], refs[5 * k:]
        me = _my_index()
        for i in range(k):
            w_ref, m_ref, v_ref, land_ref, own_ref = (ins[j * k + i] for j in range(5))
            g = _ordered_sum(land_ref, own_ref[...], me)
            for j, val in enumerate(_adamw_math(w_ref[...], m_ref[...], v_ref[...], g)):
                outs[j * k + i][...] = val

    out_shape = [SDS(w.shape, F32) for _ in range(4) for w in ws]
    res = pl.pallas_call(body, name=name, out_shape=out_shape, compiler_params=_cp(60))(*ws, *ms, *vs, *lands, *owns)
    return [res[j * k:(j + 1) * k] for j in range(4)]


MESH = pl.DeviceIdType.MESH
HBM_SPEC = pl.BlockSpec(memory_space=pltpu.HBM)
SEM_SPEC = pl.BlockSpec(memory_space=pltpu.SEMAPHORE)
EFFECT = pltpu.SideEffectType.DATAFLOW_SIDE_EFFECTING


def _me_and_peers():
    x, y, c = lax.axis_index("x"), lax.axis_index("y"), lax.axis_index("c")
    me = 4 * x + 2 * y + c
    peers = []
    for r in range(1, NDEV):
        px, py, pc = x ^ ((r >> 2) & 1), y ^ ((r >> 1) & 1), c ^ (r & 1)
        peers.append(((px, py, pc), 4 * px + 2 * py + pc))
    return me, peers


def _land_shape(a, scatter):
    return (NDEV,) + (a.shape[1:] if scatter else a.shape)


def _remote(src, dst, send_sems, recv_sems, r, k, n, dev):
    i = r * n + k
    return pltpu.make_async_remote_copy(src_ref=src, dst_ref=dst, send_sem=send_sems.at[i], recv_sem=recv_sems.at[i],
                                        device_id=dev, device_id_type=MESH)


def _exchange(arrays, scatter, name):
    n = len(arrays)
    out_shape = [SDS(_land_shape(a, scatter), a.dtype) for a in arrays]

    def body(*refs):
        ins, outs = refs[:n], refs[n:2 * n]
        send_sems, recv_sems, loc_sems = refs[2 * n:]
        me, peers = _me_and_peers()
        local = []
        for k in range(n):
            src = ins[k].at[me] if scatter else ins[k]
            cp = pltpu.make_async_copy(src, outs[k].at[me], loc_sems.at[k])
            cp.start()
            local.append(cp)
        sends = []
        for r, (dev, lin) in enumerate(peers):
            for k in range(n):
                src = ins[k].at[lin] if scatter else ins[k]
                cp = _remote(src, outs[k].at[me], send_sems, recv_sems, r, k, n, dev)
                cp.start()
                sends.append(cp)
        for r, (dev, lin) in enumerate(peers):
            for k in range(n):
                src = ins[k].at[me] if scatter else ins[k]
                _remote(src, outs[k].at[lin], send_sems, recv_sems, r, k, n, dev).wait_recv()
        for cp in sends:
            cp.wait_send()
        for cp in local:
            cp.wait()

    return pl.pallas_call(
        body, name=name, in_specs=[HBM_SPEC] * n, out_specs=[HBM_SPEC] * n, out_shape=out_shape,
        scratch_shapes=[pltpu.SemaphoreType.DMA(((NDEV - 1) * n,)), pltpu.SemaphoreType.DMA(((NDEV - 1) * n,)),
                        pltpu.SemaphoreType.DMA((n,))],
    )(*arrays)


def _exchange_start(arrays, scatter, name):
    n = len(arrays)
    lands = [lax.empty(_land_shape(a, scatter), a.dtype) for a in arrays]

    def body(*refs):
        ins, lnd = refs[:n], refs[n:2 * n]
        send_sems, recv_sems, own_sems = refs[2 * n:2 * n + 3]
        token = refs[-1]
        me, peers = _me_and_peers()
        for r, (dev, lin) in enumerate(peers):
            for k in range(n):
                src = ins[k].at[lin] if scatter else ins[k]
                _remote(src, lnd[k].at[me], send_sems, recv_sems, r, k, n, dev).start()
        if not scatter:
            for k in range(n):
                pltpu.make_async_copy(ins[k], lnd[k].at[me], own_sems.at[k]).start()
        token[...] = jnp.zeros_like(token)

    sem = pltpu.SemaphoreType.DMA(((NDEV - 1) * n,))
    outs = pl.pallas_call(
        body, name=name,
        out_shape=(sem, sem, pltpu.SemaphoreType.DMA((n,)), *[pltpu.HBM(a.shape, a.dtype) for a in arrays],
                   *[pltpu.HBM(l.shape, l.dtype) for l in lands], SDS((8, 128), F32)),
        in_specs=[HBM_SPEC] * (2 * n),
        out_specs=(SEM_SPEC, SEM_SPEC, SEM_SPEC, *[HBM_SPEC] * (2 * n), pl.BlockSpec(memory_space=pltpu.VMEM)),
        input_output_aliases={k: 3 + k for k in range(2 * n)},
        compiler_params=pltpu.CompilerParams(has_side_effects=EFFECT),
    )(*[pltpu.with_memory_space_constraint(a, pltpu.HBM) for a in arrays],
      *[pltpu.with_memory_space_constraint(l, pltpu.HBM) for l in lands])
    return (n, scatter, outs[0], outs[1], outs[2], outs[3:3 + n], outs[3 + n:3 + 2 * n]), outs[-1]


def _exchange_wait(handle, after, name):
    n, scatter, send_sems, recv_sems, own_sems, thru, lands = handle
    after = tuple(after)

    def body(*refs):
        ins, lnd = refs[:n], refs[n:2 * n]
        send_sems, recv_sems, own_sems = refs[2 * n:2 * n + 3]
        me, peers = _me_and_peers()
        for r, (dev, lin) in enumerate(peers):
            for k in range(n):
                src = ins[k].at[lin] if scatter else ins[k]
                cp = _remote(src, lnd[k].at[lin], send_sems, recv_sems, r, k, n, dev)
                cp.wait_send()
                cp.wait_recv()
        if not scatter:
            for k in range(n):
                pltpu.make_async_copy(ins[k], lnd[k].at[me], own_sems.at[k]).wait()

    outs = pl.pallas_call(
        body, name=name,
        out_shape=(*[pltpu.HBM(a.shape, a.dtype) for a in thru], *[pltpu.HBM(l.shape, l.dtype) for l in lands]),
        in_specs=[HBM_SPEC] * (2 * n) + [SEM_SPEC, SEM_SPEC, SEM_SPEC] + [ANY_SPEC] * len(after),
        out_specs=tuple([HBM_SPEC] * (2 * n)),
        input_output_aliases={k: k for k in range(2 * n)},
        compiler_params=pltpu.CompilerParams(has_side_effects=EFFECT),
    )(*thru, *lands, send_sems, recv_sems, own_sems, *after)
    return list(outs[:n]), list(outs[n:])


CHIP_RELATIONS = (2, 4, 6)


def _peer(r):
    x, y, c = lax.axis_index("x"), lax.axis_index("y"), lax.axis_index("c")
    px, py, pc = x ^ ((r >> 2) & 1), y ^ ((r >> 1) & 1), c ^ (r & 1)
    return (px, py, pc), 4 * px + 2 * py + pc


def _copy(src, dst, send_sems, recv_sems, i, dev):
    return pltpu.make_async_remote_copy(src_ref=src, dst_ref=dst, send_sem=send_sems.at[i], recv_sem=recv_sems.at[i],
                                        device_id=dev, device_id_type=MESH)


def _gather2_start(a, name):
    land = lax.empty((NDEV,) + a.shape, a.dtype)

    def body(own, lnd, send_sems, recv_sems, own_sem, own_thru, lnd_thru, token):
        me = _my_index()
        for i, r in enumerate((1,) + CHIP_RELATIONS):
            dev, _ = _peer(r)
            _copy(own, lnd.at[me], send_sems, recv_sems, i, dev).start()
        pltpu.make_async_copy(own, lnd.at[me], own_sem.at[0]).start()
        token[...] = jnp.zeros_like(token)

    sem4 = pltpu.SemaphoreType.DMA((4,))
    outs = pl.pallas_call(
        body, name=name,
        out_shape=(sem4, sem4, pltpu.SemaphoreType.DMA((1,)), pltpu.HBM(a.shape, a.dtype),
                   pltpu.HBM(land.shape, land.dtype), SDS((8, 128), F32)),
        in_specs=[HBM_SPEC, HBM_SPEC],
        out_specs=(SEM_SPEC, SEM_SPEC, SEM_SPEC, HBM_SPEC, HBM_SPEC, pl.BlockSpec(memory_space=pltpu.VMEM)),
        input_output_aliases={0: 3, 1: 4},
        compiler_params=pltpu.CompilerParams(has_side_effects=EFFECT),
    )(pltpu.with_memory_space_constraint(a, pltpu.HBM), pltpu.with_memory_space_constraint(land, pltpu.HBM))
    return outs[:5], outs[5]


def _gather2_forward(handle, after, name):
    send_sems, recv_sems, own_sem, own, land = handle
    after = tuple(after)

    def body(lnd, recv_sems, *rest):
        send2, recv2, lnd_thru = rest[len(after):]
        sib, _ = _peer(1)
        for k, r in enumerate(CHIP_RELATIONS):
            dev, lin = _peer(r)
            _copy(lnd.at[lin], lnd.at[lin], recv_sems, recv_sems, 1 + k, dev).wait_recv()
            _copy(lnd.at[lin], lnd.at[lin], send2, recv2, k, sib).start()

    sem3 = pltpu.SemaphoreType.DMA((3,))
    send2, recv2, land = pl.pallas_call(
        body, name=name,
        out_shape=(sem3, sem3, pltpu.HBM(land.shape, land.dtype)),
        in_specs=[HBM_SPEC, SEM_SPEC] + [ANY_SPEC] * len(after),
        out_specs=(SEM_SPEC, SEM_SPEC, HBM_SPEC),
        input_output_aliases={0: 2},
        compiler_params=pltpu.CompilerParams(has_side_effects=EFFECT),
    )(land, recv_sems, *after)
    return send_sems, recv_sems, own_sem, send2, recv2, own, land


def _gather2_wait(handle, name):
    send_sems, recv_sems, own_sem, send2, recv2, own, land = handle

    def body(own_ref, lnd, send_sems, recv_sems, own_sem, send2, recv2, own_thru, lnd_thru):
        me = _my_index()
        sib, sib_lin = _peer(1)
        for i, r in enumerate((1,) + CHIP_RELATIONS):
            dev, _ = _peer(r)
            _copy(own_ref, lnd.at[me], send_sems, recv_sems, i, dev).wait_send()
        _copy(own_ref, lnd.at[sib_lin], send_sems, recv_sems, 0, sib).wait_recv()
        for k, r in enumerate(CHIP_RELATIONS):
            _, lin = _peer(r)
            _, lin_other = _peer(r ^ 1)
            _copy(lnd.at[lin], lnd.at[lin], send2, recv2, k, sib).wait_send()
            _copy(lnd.at[lin_other], lnd.at[lin_other], send2, recv2, k, sib).wait_recv()
        pltpu.make_async_copy(own_ref, lnd.at[me], own_sem.at[0]).wait()

    outs = pl.pallas_call(
        body, name=name,
        out_shape=(pltpu.HBM(own.shape, own.dtype), pltpu.HBM(land.shape, land.dtype)),
        in_specs=[HBM_SPEC, HBM_SPEC] + [SEM_SPEC] * 5,
        out_specs=(HBM_SPEC, HBM_SPEC),
        input_output_aliases={0: 0, 1: 1},
        compiler_params=pltpu.CompilerParams(has_side_effects=EFFECT),
    )(own, land, send_sems, recv_sems, own_sem, send2, recv2)
    return outs[1]


def _local_step(x, tgt, norm_even, first_weight, lam_re, lam_im, log_dt, b_re, b_im, c_re, c_im, s5_d, bglu,
                ret_gain, wsp, bsp, fnorm, late_weights, emit, start_token=None):
    L = x.shape[0]
    lr3, li3 = lam_re.reshape(G, 1, P), lam_im.reshape(G, 1, P)
    dt3 = log_dt.reshape(G, 1, 1)
    br3, bi3 = jnp.swapaxes(b_re, 1, 2), jnp.swapaxes(b_im, 1, 2)
    abr3, abi3, bbr3, bbi3 = _s5_disc(lr3, li3, dt3, br3, bi3)
    bb = jnp.concatenate([_embed(bbr3), _embed(bbi3)], axis=2).astype(MXU)
    cm = jnp.concatenate([_embed(jnp.swapaxes(c_re, 1, 2)), -_embed(jnp.swapaxes(c_im, 1, 2))], axis=1).astype(MXU)
    abr, abi = abr3.reshape(1, NSTATE), abi3.reshape(1, NSTATE)
    pwr, pwi, _, _ = _s5_tables(abr, abi, NT, "s5_tables_step")
    par, pai, pbr, pbi = _s5_tables(pwr[NT - 1:NT], pwi[NT - 1:NT], CH, "s5_tables_chunk")
    inv = (ROPE_BASE ** (-jnp.arange(DK // 2, dtype=F32) / (DK // 2))).reshape(1, DK // 2)
    cos, sin = _rope_tables(L, inv)
    bsp3 = bsp.reshape(SG, CH, 1)

    def dep(token):
        return NO_DEPS if token is None else (token,)

    win_e = first_weight((cos, pbi, cm))
    p, u3 = _in_proj(x, norm_even, win_e, "in_even", dep(start_token), lane_blocks=True)
    ypre, s5_states = _s5_scan_fwd(u3, bb, cm, pwr, pwi, pwr.T, pwi.T, par, pai, s5_d)
    yb, ret_states, ret_scores, ret_o = _ret_fwd(p, cos, sin, ret_gain)
    wglu, wout_e, norm_odd, win_o, sgu_gain, wout_o = late_weights((ypre, yb))
    x1 = _out_even(x, ypre, p, yb, wglu, bglu, wout_e)
    p2 = _in_proj(x1, norm_odd, win_o, "in_odd")
    dp2, y_o, dx2, g_sgu_gain, g_wsp, g_bsp, g_fnorm, loss = _sgu_fwd_bwd(
        p2, x1, sgu_gain, wsp, bsp3, wout_o, fnorm, tgt)

    g_wout_o = _wgrad_rows([y_o], dx2, "wgrad_out_odd")
    g_win_o = _wgrad_cols(x1, norm_odd, dp2, "wgrad_in_odd")
    tok = emit("odd", dict(w_in_odd=g_win_o, w_out_odd=g_wout_o))
    dx1, g_norm_odd = _in_proj_bwd_x(dp2, x1, norm_odd, win_o, dx2, "in_odd_bwd", dep(tok))
    tok = emit("small_odd", dict(norm_odd=g_norm_odd, sgu_norm_gain=g_sgu_gain, sgu_w_spatial=g_wsp,
                                 sgu_b_spatial=g_bsp.reshape(SG, CH), final_norm=g_fnorm))

    dypre, daz, yg, dt, ya2, g_bglu = _s5_gate_bwd(ypre, p, dx1, wout_e, wglu, bglu, dep(tok))
    g_wglu = _wgrad_rows([yg], dt, "wgrad_glu")
    tok = emit("glu", dict(s5_w_glu=g_wglu))
    du, g_d, g_cm, g_bb, g_ar, g_ai = _s5_scan_bwd(u3, dypre, s5_states, bb, cm, jnp.swapaxes(cm, 1, 2), abr, abi,
                                                   pwr, pwi, par, pai, pbr, pbi, s5_d, dep(tok))
    dbbr3 = _diag_blocks(g_bb[:, :, :SW], HG, P)
    dbbi3 = _diag_blocks(g_bb[:, :, SW:], HG, P)
    g_c_re = jnp.swapaxes(_diag_blocks(g_cm[:, :SW, :], P, HG), 1, 2)
    g_c_im = -jnp.swapaxes(_diag_blocks(g_cm[:, SW:, :], P, HG), 1, 2)
    g_lr3, g_li3, g_dt3, g_br3, g_bi3 = _s5_disc_bwd(
        lr3, li3, dt3, br3, bi3, g_ar.reshape(G, 1, P), g_ai.reshape(G, 1, P), dbbr3, dbbi3)
    tok = emit("small_s5", dict(
        s5_lam_re=g_lr3.reshape(G, P), s5_lam_im=g_li3.reshape(G, P), s5_log_dt=g_dt3.reshape(1, G),
        s5_b_re=g_br3, s5_b_im=g_bi3, s5_c_re=g_c_re, s5_c_im=g_c_im, s5_d=g_d, s5_b_glu=g_bglu))
    dp, yb2, g_ret_gain = _ret_bwd(p, cos, sin, ret_gain, ret_states, ret_scores, ret_o, dx1, wout_e, du, daz,
                                   dep(tok))
    g_win_e = _wgrad_cols(x, norm_even, dp, "wgrad_in_even")
    tok = emit("even_cols", dict(w_in_even=g_win_e))
    g_wout_e = _wgrad_rows([ya2, yb2], dx1, "wgrad_out_even", dep(tok))
    tok = emit("even_rows", dict(w_out_even=g_wout_e))
    dx, g_norm_even = _in_proj_bwd_x(dp, x, norm_even, win_e, dx1, "in_even_bwd", dep(tok))
    emit("last", dict(ret_gn_gain=g_ret_gain, norm_even=g_norm_even, loss=loss))
    return dx


WEIGHTS = ['norm_even', 'w_in_even', 's5_lam_re', 's5_lam_im', 's5_log_dt', 's5_b_re', 's5_b_im', 's5_c_re',
           's5_c_im', 's5_d', 's5_w_glu', 's5_b_glu', 'ret_gn_gain', 'w_out_even', 'norm_odd', 'w_in_odd',
           'sgu_norm_gain', 'sgu_w_spatial', 'sgu_b_spatial', 'w_out_odd', 'final_norm']
BIG = ['w_in_even', 's5_w_glu', 'w_out_even', 'w_in_odd', 'w_out_odd']
SHARDED_SMALL = {'norm_odd': D // NDEV, 'sgu_norm_gain': DI // NDEV}
SCATTER_STAGES = ("odd", "glu", "even_cols", "even_rows")
GATHER_STAGES = ("small_odd", "small_s5")


def _view(n, a):
    if n in ('s5_b_re', 's5_b_im'):
        return jnp.swapaxes(a[0], 1, 2)
    if n == 'final_norm':
        return a.reshape(1, D)
    return a[0] if a.ndim >= 3 else a


def _unview(n, t, shape):
    if n in ('s5_b_re', 's5_b_im'):
        return jnp.swapaxes(t, 1, 2)[None]
    return t.reshape(shape)


def kernel(x, norm_even, w_in_even, s5_lam_re, s5_lam_im, s5_log_dt, s5_b_re, s5_b_im, s5_c_re, s5_c_im, s5_d, s5_w_glu, s5_b_glu, ret_gn_gain, w_out_even, norm_odd, w_in_odd, sgu_norm_gain, sgu_w_spatial, sgu_b_spatial, w_out_odd, final_norm, loss_target, m_norm_even, m_w_in_even, m_s5_lam_re, m_s5_lam_im, m_s5_log_dt, m_s5_b_re, m_s5_b_im, m_s5_c_re, m_s5_c_im, m_s5_d, m_s5_w_glu, m_s5_b_glu, m_ret_gn_gain, m_w_out_even, m_norm_odd, m_w_in_odd, m_sgu_norm_gain, m_sgu_w_spatial, m_sgu_b_spatial, m_w_out_odd, m_final_norm, v_norm_even, v_w_in_even, v_s5_lam_re, v_s5_lam_im, v_s5_log_dt, v_s5_b_re, v_s5_b_im, v_s5_c_re, v_s5_c_im, v_s5_d, v_s5_w_glu, v_s5_b_glu, v_ret_gn_gain, v_w_out_even, v_norm_odd, v_w_in_odd, v_sgu_norm_gain, v_sgu_w_spatial, v_sgu_b_spatial, v_w_out_odd, v_final_norm):
    args = dict(locals())
    w = {n: args[n] for n in WEIGHTS}
    m = {n: args["m_" + n] for n in WEIGHTS}
    v = {n: args["v_" + n] for n in WEIGHTS}
    me = _my_index()

    first_handle, _ = _gather2_start(w['w_in_even'][0].astype(MXU), "gather_first_start")

    def first_weight(after):
        return _gather2_wait(_gather2_forward(first_handle, after, "gather_first_forward"), "gather_first_wait")

    late_own = [w['s5_w_glu'][0].astype(MXU), w['w_out_even'][0].astype(MXU), w['norm_odd'],
                w['w_in_odd'][0].astype(MXU), w['sgu_norm_gain'], w['w_out_odd'][0].astype(MXU)]
    late_handle, start_token = _exchange_start(late_own, False, "gather_late_start")

    def late_weights(after):
        _, (wglu, wout_e, nodd, win_o, sgug, wout_o) = _exchange_wait(late_handle, after, "gather_late_wait")
        return (wglu.reshape(D, D), wout_e.reshape(DI, D), nodd.reshape(1, D), win_o, sgug.reshape(1, DI),
                wout_o.reshape(DI, D))

    pending = {}
    small_last = {}

    def emit(stage, grads):
        if stage == "last":
            small_last.update(grads)
            return None
        names = list(grads)
        handle, token = _exchange_start([grads[n] for n in names], stage in SCATTER_STAGES, stage + "_start")
        pending[stage] = (handle, names)
        return token

    dx = _local_step(
        x[0], loss_target[0], w['norm_even'], first_weight, w['s5_lam_re'][0], w['s5_lam_im'][0], w['s5_log_dt'][0],
        w['s5_b_re'][0], w['s5_b_im'][0], w['s5_c_re'][0], w['s5_c_im'][0], w['s5_d'], w['s5_b_glu'],
        w['ret_gn_gain'], w['sgu_w_spatial'][0], w['sgu_b_spatial'][0], w['final_norm'].reshape(1, D),
        late_weights, emit, start_token)

    out_g, out_d, out_m, out_v = {}, {}, {}, {}
    after = dx
    for stage in SCATTER_STAGES:
        handle, names = pending[stage]
        sent, lands = _exchange_wait(handle, (after,), stage + "_wait")
        for n, land, stack in zip(names, lands, sent):
            shp = w[n].shape
            r, c = shp[1], shp[2]
            own = lax.dynamic_index_in_dim(stack, me, 0, keepdims=False)
            res = _adamw(w[n].reshape(r, c), m[n].reshape(r, c), v[n].reshape(r, c), land, own, "adamw_" + n)
            out_g[n], out_d[n], out_m[n], out_v[n] = (t.reshape(shp) for t in res)
            after = res[0]

    names, owns, lands = [], [], []
    for stage in GATHER_STAGES:
        handle, stage_names = pending[stage]
        sent, got = _exchange_wait(handle, (after,), stage + "_wait")
        names, owns, lands = names + stage_names, owns + sent, lands + got
    last_names = list(small_last)
    last = _exchange([small_last[n] for n in last_names], False, "gather_last")
    loss_parts = last[last_names.index("loss")][:, 0, 0]
    for n, own, land in zip(last_names, (small_last[n] for n in last_names), last):
        if n != "loss":
            names, owns, lands = names + [n], owns + [own], lands + [land]
    for i, n in enumerate(names):
        if n in SHARDED_SMALL:
            width = SHARDED_SMALL[n]
            owns[i] = lax.dynamic_slice_in_dim(owns[i], me * width, width, axis=1)
            lands[i] = lax.dynamic_slice_in_dim(lands[i], me * width, width, axis=2)
    res = _adamw_many([_view(n, w[n]) for n in names], [_view(n, m[n]) for n in names],
                      [_view(n, v[n]) for n in names], lands, owns, "adamw_small")
    for dst, vals in zip((out_g, out_d, out_m, out_v), res):
        for n, t in zip(names, vals):
            dst[n] = _unview(n, t, w[n].shape)

    loss_total = jnp.sum(loss_parts)
    return (loss_total, dx[None], *[out_g[n] for n in WEIGHTS], *[out_d[n] for n in WEIGHTS],
            *[out_m[n] for n in WEIGHTS], *[out_v[n] for n in WEIGHTS])
```

```python
import math

import jax
import jax.numpy as jnp
from jax import lax
from jax.experimental import pallas as pl
from jax.experimental.pallas import tpu as pltpu

F32 = jnp.float32
MXU = jnp.bfloat16
AXES = ("x", "y", "c")
NDEV = 8
D = 1024
NIN = 6144
WIN_BLK = NIN // NDEV
DI = 2048
G, P, HG = 64, 64, 16
GB = 8
NJ = G // GB
SW = GB * P
UW = GB * HG
NSTATE = G * P
HEADS, DK = 4, 256
CH = 128
SG, SGD = 4, 512
EPS = 1e-6
ROPE_BASE = 10000.0
VMEM_CAP_V7X = 64 * 1024 * 1024
LOG_G = [math.log1p(-2.0 ** (-5.0 - h)) for h in range(HEADS)]
GELU_C = math.sqrt(2.0 / math.pi)

ADAM_LR, ADAM_B1, ADAM_B2, ADAM_EPS, ADAM_WD, ADAM_STEP = 0.001, 0.9, 0.999, 1e-08, 0.01, 10
BC1 = 1.0 - ADAM_B1 ** ADAM_STEP
BC2 = 1.0 - ADAM_B2 ** ADAM_STEP

SDS = jax.ShapeDtypeStruct
ARB2 = ("arbitrary", "arbitrary")


def _cp(vmem_mib, sem=None):
    kw = dict(vmem_limit_bytes=min(vmem_mib * 1024 * 1024, VMEM_CAP_V7X - 4 * 1024 * 1024))
    if sem is not None:
        kw["dimension_semantics"] = sem
    return pltpu.CompilerParams(**kw)


def _mm(a, b):
    return jnp.dot(a.astype(MXU), b.astype(MXU), preferred_element_type=F32)


def _mm_nt(a, b):
    return lax.dot_general(a.astype(MXU), b.astype(MXU), (((1,), (1,)), ((), ())), preferred_element_type=F32)


def _mm_tn(a, b):
    return lax.dot_general(a.astype(MXU), b.astype(MXU), (((0,), (0,)), ((), ())), preferred_element_type=F32)


def _gelu(x):
    return _gelu_and_grad(x)[0]


def _gelu_and_grad(x):
    x2 = x * x
    th = jnp.tanh(GELU_C * x * (1.0 + 0.044715 * x2))
    hp = 0.5 * (1.0 + th)
    return x * hp, hp + 0.5 * x * (1.0 - th * th) * GELU_C * (1.0 + 3.0 * 0.044715 * x2)


def _silu_and_grad(x):
    s = jax.nn.sigmoid(x)
    return x * s, s * (1.0 + x * (1.0 - s))


def _full(shape):
    nd = len(shape)
    return pl.BlockSpec(shape, lambda *_: (0,) * nd)


def _rms(xf):
    r = lax.rsqrt(jnp.mean(xf * xf, axis=-1, keepdims=True) + EPS)
    return xf * r, r


ANY_SPEC = pl.BlockSpec(memory_space=pl.ANY)
NO_DEPS = ()


def _load_once_start(src_hbm, dst_vmem, sems):
    @pl.when(pl.program_id(0) == 0)
    def _():
        for c in range(NDEV):
            pltpu.make_async_copy(src_hbm.at[c], dst_vmem.at[c], sems.at[c]).start()


def _load_once_wait(src_hbm, dst_vmem, sems, c):
    @pl.when(pl.program_id(0) == 0)
    def _():
        pltpu.make_async_copy(src_hbm.at[c], dst_vmem.at[c], sems.at[c]).wait()


def _lane_blocks(L):
    return SDS((NJ, L, UW), F32)


def _lane_block_spec(rows, index):
    return pl.BlockSpec((NJ, rows, UW), lambda i: (0, index(i), 0))


def _from_lane_blocks(ref):
    return jnp.concatenate([ref[j] for j in range(NJ)], axis=1)


def _to_lane_blocks(ref, v):
    for j in range(NJ):
        ref[j] = v[:, j * UW:(j + 1) * UW].astype(ref.dtype)


def _in_proj(x, gain, wst, name, deps=NO_DEPS, lane_blocks=False):
    L = x.shape[0]
    tm = min(512, L)

    def body(x_ref, g_ref, w_hbm, *rest):
        outs = rest[len(deps):]
        o_ref, w_scr, sems = outs[0], outs[-2], outs[-1]
        _load_once_start(w_hbm, w_scr, sems)
        xhat, _ = _rms(x_ref[...])
        h = (xhat * g_ref[...]).astype(MXU)
        for c in range(NDEV):
            _load_once_wait(w_hbm, w_scr, sems, c)
            o_ref[:, c * WIN_BLK:(c + 1) * WIN_BLK] = jnp.dot(h, w_scr[c], preferred_element_type=F32)
        if lane_blocks:
            _to_lane_blocks(outs[1], o_ref[:, 0:D])

    p_spec, p_shape = pl.BlockSpec((tm, NIN), lambda i: (i, 0)), SDS((L, NIN), F32)
    return pl.pallas_call(
        body, name=name, grid=(L // tm,),
        in_specs=[pl.BlockSpec((tm, D), lambda i: (i, 0)), _full((1, D)), ANY_SPEC] + [ANY_SPEC] * len(deps),
        out_specs=[p_spec, _lane_block_spec(tm, lambda i: i)] if lane_blocks else p_spec,
        out_shape=[p_shape, _lane_blocks(L)] if lane_blocks else p_shape,
        scratch_shapes=[pltpu.VMEM((NDEV, D, WIN_BLK), MXU), pltpu.SemaphoreType.DMA((NDEV,))],
        compiler_params=_cp(58, ("arbitrary",)),
    )(x, gain, wst, *deps)


def _in_proj_bwd_x(dp, x, gain, wst, dres, name, deps=NO_DEPS):
    L = x.shape[0]
    tm = min(512, L)

    def body(dp_ref, x_ref, g_ref, w_hbm, dres_ref, *rest):
        dx_ref, gg_ref, w_scr, sems = rest[len(deps):]
        _load_once_start(w_hbm, w_scr, sems)

        @pl.when(pl.program_id(0) == 0)
        def _():
            gg_ref[...] = jnp.zeros_like(gg_ref)

        _load_once_wait(w_hbm, w_scr, sems, 0)
        dh = _mm_nt(dp_ref[:, 0:WIN_BLK], w_scr[0])
        for c in range(1, NDEV):
            _load_once_wait(w_hbm, w_scr, sems, c)
            dh += _mm_nt(dp_ref[:, c * WIN_BLK:(c + 1) * WIN_BLK], w_scr[c])
        xhat, r = _rms(x_ref[...])
        dxhat = dh * g_ref[...]
        dx_ref[...] = dres_ref[...] + r * (dxhat - xhat * jnp.mean(dxhat * xhat, axis=-1, keepdims=True))
        gg_ref[...] += jnp.sum(dh * xhat, axis=0, keepdims=True)

    row = pl.BlockSpec((tm, D), lambda i: (i, 0))
    return pl.pallas_call(
        body, name=name, grid=(L // tm,),
        in_specs=[pl.BlockSpec((tm, NIN), lambda i: (i, 0)), row, _full((1, D)), ANY_SPEC, row]
        + [ANY_SPEC] * len(deps),
        out_specs=[row, _full((1, D))],
        out_shape=[SDS((L, D), F32), SDS((1, D), F32)],
        scratch_shapes=[pltpu.VMEM((NDEV, D, WIN_BLK), MXU), pltpu.SemaphoreType.DMA((NDEV,))],
        compiler_params=_cp(56, ("arbitrary",)),
    )(dp, x, gain, wst, dres, *deps)


def _wgrad_cols(x, gain, dp, name, deps=NO_DEPS):
    L = x.shape[0]
    tk = min(1024, L)
    nk = L // tk
    halves = 2
    nh = NDEV // halves

    def body(x_ref, g_ref, dp_ref, *rest):
        o_ref, acc = rest[len(deps):]
        k = pl.program_id(1)

        @pl.when(k == 0)
        def _():
            acc[...] = jnp.zeros_like(acc)

        xhat, _ = _rms(x_ref[...])
        acc[...] += _mm_tn(xhat * g_ref[...], dp_ref[...])

        @pl.when(k == nk - 1)
        def _():
            for c in range(nh):
                o_ref[c] = acc[:, c * WIN_BLK:(c + 1) * WIN_BLK].astype(o_ref.dtype)

    return pl.pallas_call(
        body, name=name, grid=(halves, nk),
        in_specs=[pl.BlockSpec((tk, D), lambda n, k: (k, 0)), _full((1, D)),
                  pl.BlockSpec((tk, nh * WIN_BLK), lambda n, k: (k, n))] + [ANY_SPEC] * len(deps),
        out_specs=pl.BlockSpec((nh, D, WIN_BLK), lambda n, k: (n, 0, 0)),
        out_shape=SDS((NDEV, D, WIN_BLK), MXU),
        scratch_shapes=[pltpu.VMEM((D, nh * WIN_BLK), F32)],
        compiler_params=_cp(56, ARB2),
    )(x, gain, dp, *deps)


def _wgrad_rows(a_parts, b, name, deps=NO_DEPS):
    L, N = b.shape
    na = len(a_parts)
    widths = [a.shape[1] for a in a_parts]
    M = sum(widths)
    tk = min(1024, L)
    nk = L // tk

    def body(*refs):
        a_refs, b_ref = refs[:na], refs[na]
        o_ref, acc = refs[na + 1 + len(deps):]
        k = pl.program_id(0)

        @pl.when(k == 0)
        def _():
            acc[...] = jnp.zeros_like(acc)

        bv = b_ref[...].astype(MXU)
        off = 0
        for a_ref, wd in zip(a_refs, widths):
            acc[off:off + wd, :] += _mm_tn(a_ref[...], bv)
            off += wd

        @pl.when(k == nk - 1)
        def _():
            o_ref[...] = acc[...].astype(o_ref.dtype).reshape(o_ref.shape)

    return pl.pallas_call(
        body, name=name, grid=(nk,),
        in_specs=[pl.BlockSpec((tk, wd), lambda k: (k, 0)) for wd in widths]
        + [pl.BlockSpec((tk, N), lambda k: (k, 0))] + [ANY_SPEC] * len(deps),
        out_specs=_full((NDEV, M // NDEV, N)),
        out_shape=SDS((NDEV, M // NDEV, N), MXU),
        scratch_shapes=[pltpu.VMEM((M, N), F32)],
        compiler_params=_cp(48, ("arbitrary",)),
    )(*a_parts, b, *deps)


def _s5_disc_fn(lr_raw, li, logdt, br, bi):
    lr = jnp.minimum(lr_raw, -1e-4)
    dt = jnp.exp(logdt)
    mag = jnp.exp(lr * dt)
    abr = mag * jnp.cos(li * dt)
    abi = mag * jnp.sin(li * dt)
    den = lr * lr + li * li
    nre = abr - 1.0
    nim = abi
    zr = (nre * lr + nim * li) / den
    zi = (nim * lr - nre * li) / den
    return abr, abi, zr * br - zi * bi, zr * bi + zi * br


def _s5_disc(lr, li, logdt, br, bi):
    def body(lr_ref, li_ref, dt_ref, br_ref, bi_ref, abr_ref, abi_ref, bbr_ref, bbi_ref):
        abr, abi, bbr, bbi = _s5_disc_fn(lr_ref[...], li_ref[...], dt_ref[...], br_ref[...], bi_ref[...])
        abr_ref[...] = abr
        abi_ref[...] = abi
        bbr_ref[...] = bbr
        bbi_ref[...] = bbi

    s1, s3 = SDS((G, 1, P), F32), SDS((G, HG, P), F32)
    return pl.pallas_call(body, name="s5_disc", out_shape=[s1, s1, s3, s3])(lr, li, logdt, br, bi)


def _s5_disc_bwd(lr, li, logdt, br, bi, dabr, dabi, dbbr, dbbi):
    def body(lr_ref, li_ref, dt_ref, br_ref, bi_ref, c0, c1, c2, c3, o0, o1, o2, o3, o4):
        _, vjp = jax.vjp(_s5_disc_fn, lr_ref[...], li_ref[...], dt_ref[...], br_ref[...], bi_ref[...])
        g = vjp((c0[...], c1[...], c2[...], c3[...]))
        for o, v in zip((o0, o1, o2, o3, o4), g):
            o[...] = v

    s1, s3 = SDS((G, 1, P), F32), SDS((G, HG, P), F32)
    return pl.pallas_call(body, name="s5_disc_bwd", out_shape=[s1, s1, SDS((G, 1, 1), F32), s3, s3])(
        lr, li, logdt, br, bi, dabr, dabi, dbbr, dbbi)


def _s5_tables(abr, abi, rows, name):
    def body(ar_ref, ai_ref, pfr, pfi, pbr, pbi):
        pfr[0:1, :] = ar_ref[...]
        pfi[0:1, :] = ai_ref[...]
        pbr[rows - 1:rows, :] = ar_ref[...]
        pbi[rows - 1:rows, :] = ai_ref[...]
        n = 1
        while n < rows:
            er, ei = pfr[n - 1:n, :], pfi[n - 1:n, :]
            xr, xi = pfr[0:n, :], pfi[0:n, :]
            pfr[n:2 * n, :] = er * xr - ei * xi
            pfi[n:2 * n, :] = er * xi + ei * xr
            yr, yi = pbr[rows - n:rows, :], pbi[rows - n:rows, :]
            pbr[rows - 2 * n:rows - n, :] = er * yr - ei * yi
            pbi[rows - 2 * n:rows - n, :] = er * yi + ei * yr
            n *= 2

    s = SDS((rows, NSTATE), F32)
    return pl.pallas_call(body, name=name, out_shape=[s, s, s, s], compiler_params=_cp(40))(abr, abi)


def _cscan(br, bi, pr_ref, pi_ref, reverse):
    T = br.shape[0]
    sign = -1.0 if reverse else 1.0
    row = lax.broadcasted_iota(jnp.int32, br.shape, 0)
    k = 1
    while k < T:
        akr = pr_ref[k - 1:k, :]
        aki = sign * pi_ref[k - 1:k, :]

        def shift(v):
            if k % 8 == 0:
                z = jnp.zeros((k, v.shape[1]), v.dtype)
                return jnp.concatenate([v[k:], z], 0) if reverse else jnp.concatenate([z, v[:T - k]], 0)
            if reverse:
                return jnp.where(row < T - k, pltpu.roll(v, T - k, 0), 0.0)
            return jnp.where(row >= k, pltpu.roll(v, k, 0), 0.0)

        sr, si = shift(br), shift(bi)
        br, bi = br + akr * sr - aki * si, bi + akr * si + aki * sr
        k *= 2
    return br, bi


def _embed(t):
    a, b = t.shape[1], t.shape[2]
    return jnp.einsum("jgab,gh->jgahb", t.reshape(NJ, GB, a, b), jnp.eye(GB, dtype=t.dtype)).reshape(NJ, GB * a, GB * b)


def _diag_blocks(t, a, b):
    return jnp.einsum("jgahb,gh->jgab", t.reshape(NJ, GB, a, GB, b), jnp.eye(GB, dtype=t.dtype)).reshape(G, a, b)


NT = 16


def _chunks(L):
    ncb = min(CH, L // NT)
    return ncb, NT * ncb


def _cmul_add(ar, ai, xr, xi, br, bi):
    return ar * xr - ai * xi + br, ar * xi + ai * xr + bi


def _pow_weights(w_ref, pwr_ref, pwi_ref, dst, adjoint):
    w = w_ref[...].astype(F32)
    wr, wi = w[:, :SW], w[:, SW:]
    for t in range(NT):
        k = t if adjoint else NT - 1 - t
        if k == 0:
            blk = w
        else:
            pr, pi = pwr_ref[k - 1:k, :], pwi_ref[k - 1:k, :]
            if adjoint:
                blk = jnp.concatenate([pr * wr + pi * wi, pr * wi - pi * wr], axis=1)
            else:
                blk = jnp.concatenate([pr * wr - pi * wi, pr * wi + pi * wr], axis=1)
        dst[t * UW:(t + 1) * UW, :] = blk.astype(dst.dtype)


def _slices(ref, ncb, axis):
    return jnp.concatenate([ref[pl.ds(t, ncb, stride=NT), :] for t in range(NT)], axis=axis)


def _s5_scan_fwd(u3, bb, cm, pwr, pwi, pwrt, pwit, par, pai, dskip):
    L = u3.shape[1]
    ncb, tb = _chunks(L)
    nb = L // tb

    def build(bb_ref, cm_ref, pwr_ref, pwi_ref, pct_ref, pit_ref, bbp_scr, ktp_scr, zc_scr):
        w = bb_ref[...].astype(F32)
        wr, wi = w[:, :SW], w[:, SW:]
        cmv = cm_ref[...].astype(F32)
        ct, cb = cmv[:SW, :], cmv[SW:, :]
        zero = jnp.zeros((UW, UW), MXU)
        for tau in range(NT):
            if tau == 0:
                blk = w
            else:
                pr, pi = pwr_ref[tau - 1:tau, :], pwi_ref[tau - 1:tau, :]
                blk = jnp.concatenate([pr * wr - pi * wi, pr * wi + pi * wr], axis=1)
            blk = blk.astype(MXU)
            bbp_scr[(NT - 1 - tau) * UW:(NT - tau) * UW, :] = blk
            k = jnp.dot(blk, cm_ref[...], preferred_element_type=F32).astype(MXU)
            for j in range(NT - tau):
                ktp_scr[j * UW:(j + 1) * UW, (j + tau) * UW:(j + tau + 1) * UW] = k
            pc, pic = pct_ref[:, tau:tau + 1], pit_ref[:, tau:tau + 1]
            zc_scr[:, tau * UW:(tau + 1) * UW] = jnp.concatenate(
                [pc * ct + pic * cb, pc * cb - pic * ct], axis=0).astype(MXU)
        for j in range(NT):
            for t in range(j):
                ktp_scr[j * UW:(j + 1) * UW, t * UW:(t + 1) * UW] = zero

    def body(u_ref, bb_ref, cm_ref, pwr_ref, pwi_ref, pct_ref, pit_ref, par_ref, pai_ref, d_ref, ypre_ref, cin_ref,
             carry, cin_scr, bbp_scr, ktp_scr, zc_scr):
        @pl.when(pl.program_id(1) == 0)
        def _():
            carry[...] = jnp.zeros_like(carry)
            build(bb_ref, cm_ref, pwr_ref, pwi_ref, pct_ref, pit_ref, bbp_scr, ktp_scr, zc_scr)

        c = carry[...]
        ucat = _slices(u_ref, ncb, 1)
        ub = ucat.astype(MXU)
        e = jnp.dot(ub, bbp_scr[...], preferred_element_type=F32)
        xr, xi = _cscan(e[:, :SW], e[:, SW:], par_ref, pai_ref, False)
        cr, ci = c[:, :SW], c[:, SW:]
        fr, fi = _cmul_add(par_ref[0:ncb, :], pai_ref[0:ncb, :], cr, ci, xr, xi)
        carry[...] = jnp.concatenate([fr[ncb - 1:ncb, :], fi[ncb - 1:ncb, :]], axis=1)
        row = lax.broadcasted_iota(jnp.int32, fr.shape, 0)
        cin_ref[:, :SW] = jnp.where(row >= 1, pltpu.roll(fr, 1, 0), cr)
        cin_ref[:, SW:] = jnp.where(row >= 1, pltpu.roll(fi, 1, 0), ci)
        cin_scr[...] = cin_ref[...].astype(MXU)
        y = (jnp.dot(ub, ktp_scr[...], preferred_element_type=F32)
             + jnp.dot(cin_scr[...], zc_scr[...], preferred_element_type=F32)
             + jnp.tile(d_ref[...], (1, NT)) * ucat)
        for t in range(NT):
            ypre_ref[pl.ds(t, ncb, stride=NT), :] = y[:, t * UW:(t + 1) * UW]

    tab = pl.BlockSpec((CH, SW), lambda j, i: (0, j))
    stp = pl.BlockSpec((NT, SW), lambda j, i: (0, j))
    stt = pl.BlockSpec((SW, NT), lambda j, i: (j, 0))
    vec = lambda w: pl.BlockSpec((1, w), lambda j, i: (0, j))
    return pl.pallas_call(
        body, name="s5_scan_fwd", grid=(NJ, nb),
        in_specs=[pl.BlockSpec((None, tb, UW), lambda j, i: (j, i, 0)),
                  pl.BlockSpec((None, UW, 2 * SW), lambda j, i: (j, 0, 0)),
                  pl.BlockSpec((None, 2 * SW, UW), lambda j, i: (j, 0, 0)),
                  stp, stp, stt, stt, tab, tab, vec(UW)],
        out_specs=[pl.BlockSpec((None, tb, UW), lambda j, i: (j, i, 0)),
                   pl.BlockSpec((None, ncb, 2 * SW), lambda j, i: (j, i, 0))],
        out_shape=[_lane_blocks(L), SDS((NJ, L // NT, 2 * SW), F32)],
        scratch_shapes=[pltpu.VMEM((1, 2 * SW), F32), pltpu.VMEM((ncb, 2 * SW), MXU),
                        pltpu.VMEM((NT * UW, 2 * SW), MXU), pltpu.VMEM((NT * UW, NT * UW), MXU),
                        pltpu.VMEM((2 * SW, NT * UW), MXU)],
        compiler_params=_cp(56, ARB2),
    )(u3, bb, cm, pwr, pwi, pwrt, pwit, par, pai, dskip)


def _s5_gate_bwd(ypre3, p, dx1, wout_e, wglu, bglu, deps=NO_DEPS):
    L = p.shape[0]
    tm = min(256, L)

    def body(y_ref, az_ref, dx1_ref, wo_ref, wg_ref, bg_ref, *rest):
        dyp_ref, daz_ref, yg_ref, dt_ref, ya_ref, gbg_ref = rest[len(deps):]

        @pl.when(pl.program_id(0) == 0)
        def _():
            gbg_ref[...] = jnp.zeros_like(gbg_ref)

        yg, dgelu = _gelu_and_grad(_from_lane_blocks(y_ref))
        sg = jax.nn.sigmoid(_mm(yg, wg_ref[...]) + bg_ref[...])
        act, dact = _silu_and_grad(az_ref[...])
        y2 = yg * sg
        dya = _mm_nt(dx1_ref[...], wo_ref[...])
        daz_ref[...] = (dya * y2 * dact).astype(daz_ref.dtype)
        dy2 = dya * act
        dt = dy2 * yg * sg * (1.0 - sg)
        dyg = dy2 * sg + _mm_nt(dt, wg_ref[...])
        _to_lane_blocks(dyp_ref, dyg * dgelu)
        yg_ref[...] = yg.astype(yg_ref.dtype)
        dt_ref[...] = dt.astype(dt_ref.dtype)
        ya_ref[...] = (y2 * act).astype(ya_ref.dtype)
        gbg_ref[...] += jnp.sum(dt, axis=0, keepdims=True)

    row = pl.BlockSpec((tm, D), lambda i: (i, 0))
    return pl.pallas_call(
        body, name="s5_gate_bwd", grid=(L // tm,),
        in_specs=[_lane_block_spec(tm, lambda i: i), pl.BlockSpec((tm, D), lambda i: (i, 1)), row,
                  pl.BlockSpec((D, D), lambda i: (0, 0)), _full((D, D)), _full((1, D))] + [ANY_SPEC] * len(deps),
        out_specs=[_lane_block_spec(tm, lambda i: i), row, row, row, row, _full((1, D))],
        out_shape=[_lane_blocks(L), SDS((L, D), MXU), SDS((L, D), MXU), SDS((L, D), MXU), SDS((L, D), MXU),
                   SDS((1, D), F32)],
        compiler_params=_cp(40, ("arbitrary",)),
    )(ypre3, p, dx1, wout_e, wglu, bglu, *deps)


def _s5_scan_bwd(u3, dypre3, states, bb, cm, cmt, abr, abi, pwr, pwi, par, pai, pbr, pbi, dskip, deps=NO_DEPS):
    L = u3.shape[1]
    ncb, tb = _chunks(L)
    nb = L // tb
    rev = lambda i: nb - 1 - i

    def body(u_ref, dy_ref, st_ref, bb_ref, cm_ref, cmt_ref, ar_ref, ai_ref, pwr_ref, pwi_ref, par_ref, pai_ref,
             pbr_ref, pbi_ref, d_ref, *rest):
        (du_ref, gd_ref, gcm_ref, gbb_ref, gar_ref, gai_ref,
         lcarry, s_scr, gs_scr, cmp_scr) = rest[len(deps):]
        del cm_ref

        @pl.when(pl.program_id(1) == 0)
        def _():
            _pow_weights(cmt_ref, pwr_ref, pwi_ref, cmp_scr, True)
            lcarry[...] = jnp.zeros_like(lcarry)
            gd_ref[...] = jnp.zeros_like(gd_ref)
            gcm_ref[...] = jnp.zeros_like(gcm_ref)
            gbb_ref[...] = jnp.zeros_like(gbb_ref)
            gar_ref[...] = jnp.zeros_like(gar_ref)
            gai_ref[...] = jnp.zeros_like(gai_ref)

        ar, ai = ar_ref[...], ai_ref[...]
        c = st_ref[...]
        s_scr[0] = c
        sr, si = c[:, :SW], c[:, SW:]
        for t in range(NT):
            bu = _mm(u_ref[pl.ds(t, ncb, stride=NT), :], bb_ref[...])
            sr, si = _cmul_add(ar, ai, sr, si, bu[:, :SW], bu[:, SW:])
            s_scr[t + 1] = jnp.concatenate([sr, si], axis=1)
        dys = [dy_ref[pl.ds(t, ncb, stride=NT), :] for t in range(NT)]
        for t in range(NT):
            gs_scr[t] = _mm(dys[t], cmt_ref[...])
        f = _mm(jnp.concatenate(dys, axis=1), cmp_scr[...])
        xr, xi = _cscan(f[:, :SW], f[:, SW:], par_ref, pai_ref, True)
        lc = lcarry[...]
        lcr, lci = lc[:, :SW], lc[:, SW:]
        hr, hi = _cmul_add(pbr_ref[CH - ncb:CH, :], -pbi_ref[CH - ncb:CH, :], lcr, lci, xr, xi)
        lcarry[...] = jnp.concatenate([hr[0:1, :], hi[0:1, :]], axis=1)
        row = lax.broadcasted_iota(jnp.int32, hr.shape, 0)
        lr_ = jnp.where(row < ncb - 1, pltpu.roll(hr, ncb - 1, 0), lcr)
        li_ = jnp.where(row < ncb - 1, pltpu.roll(hi, ncb - 1, 0), lci)
        gar = jnp.zeros((1, SW), F32)
        gai = jnp.zeros((1, SW), F32)
        for t in reversed(range(NT)):
            gs = gs_scr[t]
            lr_, li_ = _cmul_add(ar, -ai, lr_, li_, gs[:, :SW], gs[:, SW:])
            rows = pl.ds(t, ncb, stride=NT)
            u_t, dy_t = u_ref[rows, :], dy_ref[rows, :]
            lam = jnp.concatenate([lr_, li_], axis=1)
            gbb_ref[...] += _mm_tn(u_t, lam)
            du_ref[rows, :] = _mm_nt(lam, bb_ref[...]) + dy_t * d_ref[...]
            gd_ref[...] += jnp.sum(dy_t * u_t, axis=0, keepdims=True)
            gcm_ref[...] += _mm_tn(s_scr[t + 1], dy_t)
            sp = s_scr[t]
            spr, spi = sp[:, :SW], sp[:, SW:]
            gar += jnp.sum(lr_ * spr + li_ * spi, axis=0, keepdims=True)
            gai += jnp.sum(li_ * spr - lr_ * spi, axis=0, keepdims=True)
        gar_ref[...] += gar
        gai_ref[...] += gai

    tab = pl.BlockSpec((CH, SW), lambda j, i: (0, j))
    stp = pl.BlockSpec((NT, SW), lambda j, i: (0, j))
    colblk = pl.BlockSpec((None, tb, UW), lambda j, i: (j, rev(i), 0))
    vec = lambda w: pl.BlockSpec((1, w), lambda j, i: (0, j))
    return pl.pallas_call(
        body, name="s5_scan_bwd", grid=(NJ, nb),
        in_specs=[colblk, colblk,
                  pl.BlockSpec((None, ncb, 2 * SW), lambda j, i: (j, rev(i), 0)),
                  pl.BlockSpec((None, UW, 2 * SW), lambda j, i: (j, 0, 0)),
                  pl.BlockSpec((None, 2 * SW, UW), lambda j, i: (j, 0, 0)),
                  pl.BlockSpec((None, UW, 2 * SW), lambda j, i: (j, 0, 0)),
                  vec(SW), vec(SW), stp, stp, tab, tab, tab, tab, vec(UW)] + [ANY_SPEC] * len(deps),
        out_specs=[colblk, vec(UW),
                   pl.BlockSpec((None, 2 * SW, UW), lambda j, i: (j, 0, 0)),
                   pl.BlockSpec((None, UW, 2 * SW), lambda j, i: (j, 0, 0)),
                   vec(SW), vec(SW)],
        out_shape=[_lane_blocks(L), SDS((1, D), F32),
                   SDS((NJ, 2 * SW, UW), F32), SDS((NJ, UW, 2 * SW), F32),
                   SDS((1, NSTATE), F32), SDS((1, NSTATE), F32)],
        scratch_shapes=[pltpu.VMEM((1, 2 * SW), F32), pltpu.VMEM((NT + 1, ncb, 2 * SW), F32),
                        pltpu.VMEM((NT, ncb, 2 * SW), F32), pltpu.VMEM((NT * UW, 2 * SW), MXU)],
        compiler_params=_cp(56, ARB2),
    )(u3, dypre3, states, bb, cm, cmt, abr, abi, pwr, pwi, par, pai, pbr, pbi, dskip, *deps)


def _rope_tables(L, inv):
    tm = min(512, L)

    def body(inv_ref, cos_ref, sin_ref):
        pos = (lax.broadcasted_iota(jnp.int32, (tm, DK // 2), 0) + pl.program_id(0) * tm).astype(F32)
        ang = pos * inv_ref[...]
        cos_ref[...] = jnp.cos(ang)
        sin_ref[...] = jnp.sin(ang)

    blk = pl.BlockSpec((tm, DK // 2), lambda i: (i, 0))
    return pl.pallas_call(body, name="rope_tables", grid=(L // tm,), in_specs=[_full((1, DK // 2))],
                          out_specs=[blk, blk], out_shape=[SDS((L, DK // 2), F32)] * 2)(inv)


def _rot(x, cos, sin):
    x1, x2 = x[:, :DK // 2], x[:, DK // 2:]
    return jnp.concatenate([x1 * cos - x2 * sin, x1 * sin + x2 * cos], axis=1)


def _unrot(d, cos, sin):
    d1, d2 = d[:, :DK // 2], d[:, DK // 2:]
    return jnp.concatenate([d1 * cos + d2 * sin, d2 * cos - d1 * sin], axis=1)


def _ret_decays(h):
    lg = LOG_G[h]
    n = lax.broadcasted_iota(jnp.int32, (CH, CH), 0)
    m = lax.broadcasted_iota(jnp.int32, (CH, CH), 1)
    diff = (n - m).astype(F32)
    decay = jnp.where(n >= m, jnp.exp(lg * jnp.maximum(diff, 0.0)), 0.0)
    idx = lax.broadcasted_iota(jnp.int32, (CH, 1), 0).astype(F32)
    xi = jnp.exp(lg * (idx + 1.0))
    zeta = jnp.exp(lg * (CH - 1.0 - idx))
    return decay, xi, zeta, math.exp(lg * CH)


def _ret_tables(dec_scr, vec_scr):
    for h in range(HEADS):
        decay, xi, zeta, _ = _ret_decays(h)
        dec_scr[h] = decay
        vec_scr[h] = jnp.concatenate([jnp.broadcast_to(xi, (CH, 128)), jnp.broadcast_to(zeta, (CH, 128))], axis=1)


def _group_norm(o):
    mu = jnp.mean(o, axis=-1, keepdims=True)
    oc = o - mu
    rstd = lax.rsqrt(jnp.mean(oc * oc, axis=-1, keepdims=True) + EPS)
    return oc * rstd, rstd


def _ret_fwd(p, cos, sin, gain):
    L = p.shape[0]
    nb = L // CH

    def body(q_ref, k_ref, v_ref, bz_ref, cos_ref, sin_ref, g_ref, yb_ref, st_ref, sc_ref, o_ref,
             state, dec_scr, vec_scr):
        @pl.when(pl.program_id(0) == 0)
        def _():
            state[...] = jnp.zeros_like(state)
            _ret_tables(dec_scr, vec_scr)

        cos, sin = cos_ref[...], sin_ref[...]
        act, _ = _silu_and_grad(bz_ref[...])
        for h in range(HEADS):
            hs = slice(h * DK, (h + 1) * DK)
            xi, zeta = vec_scr[h, :, 0:1], vec_scr[h, :, 128:129]
            s_prev = state[h]
            s_prev_b = s_prev.astype(MXU)
            st_ref[h] = s_prev_b
            v = v_ref[:, hs]
            qr = _rot(q_ref[:, hs], cos, sin)
            kr = _rot(k_ref[:, hs], cos, sin) * (DK ** -0.5)
            scores = (_mm_nt(qr, kr) * dec_scr[h]).astype(MXU)
            o = _mm(scores, v) + _mm(qr * xi, s_prev_b)
            sc_ref[:, h * CH:(h + 1) * CH] = scores
            o_ref[:, hs] = o
            state[h] = s_prev * math.exp(LOG_G[h] * CH) + _mm_tn(kr * zeta, v)
            on, _ = _group_norm(o)
            yb_ref[:, hs] = (on * g_ref[:, hs] * act[:, hs]).astype(yb_ref.dtype)

    col = lambda c: pl.BlockSpec((CH, D), lambda i: (i, c))
    rope = pl.BlockSpec((CH, DK // 2), lambda i: (i, 0))
    return pl.pallas_call(
        body, name="ret_fwd", grid=(nb,),
        in_specs=[col(2), col(3), col(4), col(5), rope, rope, _full((1, D))],
        out_specs=[pl.BlockSpec((CH, D), lambda i: (i, 0)),
                   pl.BlockSpec((None, HEADS, DK, DK), lambda i: (i, 0, 0, 0)),
                   pl.BlockSpec((CH, HEADS * CH), lambda i: (i, 0)), pl.BlockSpec((CH, D), lambda i: (i, 0))],
        out_shape=[SDS((L, D), MXU), SDS((nb, HEADS, DK, DK), MXU), SDS((L, HEADS * CH), MXU), SDS((L, D), F32)],
        scratch_shapes=[pltpu.VMEM((HEADS, DK, DK), F32), pltpu.VMEM((HEADS, CH, CH), F32),
                        pltpu.VMEM((HEADS, CH, 256), F32)],
        compiler_params=_cp(40, ("arbitrary",)),
    )(p, p, p, p, cos, sin, gain)


def _ret_bwd(p, cos, sin, gain, states, scores, o, dx1, wout_e, du, daz, deps=NO_DEPS):
    L = p.shape[0]
    nb = L // CH
    rev = lambda i: nb - 1 - i

    def body(q_ref, k_ref, v_ref, bz_ref, cos_ref, sin_ref, g_ref, st_ref, sc_ref, o_ref, dx1_ref, wo_ref, du_ref,
             daz_ref, *rest):
        dp_ref, yb_ref, gg_ref, gstate, dec_scr, vec_scr = rest[len(deps):]

        @pl.when(pl.program_id(0) == 0)
        def _():
            gstate[...] = jnp.zeros_like(gstate)
            gg_ref[...] = jnp.zeros_like(gg_ref)
            _ret_tables(dec_scr, vec_scr)

        cos, sin = cos_ref[...], sin_ref[...]
        act, dact = _silu_and_grad(bz_ref[...])
        dyb = _mm_nt(dx1_ref[...], wo_ref[...])
        dp_ref[:, 0:D] = _from_lane_blocks(du_ref).astype(dp_ref.dtype)
        dp_ref[:, D:2 * D] = daz_ref[...]
        for h in range(HEADS):
            hs = slice(h * DK, (h + 1) * DK)
            col = lambda part: slice((2 + part) * D + h * DK, (2 + part) * D + (h + 1) * DK)
            decay = dec_scr[h]
            xi, zeta = vec_scr[h, :, 0:1], vec_scr[h, :, 128:129]
            v = v_ref[:, hs]
            s_prev_b = st_ref[h]
            qr = _rot(q_ref[:, hs], cos, sin)
            kr = _rot(k_ref[:, hs], cos, sin) * (DK ** -0.5)
            scores = sc_ref[:, h * CH:(h + 1) * CH]
            on, rstd = _group_norm(o_ref[:, hs])
            gain_h = g_ref[:, hs]
            out = on * gain_h
            yb_ref[:, hs] = (out * act[:, hs]).astype(yb_ref.dtype)
            dyb_h = dyb[:, hs]
            dp_ref[:, col(3)] = (dyb_h * out * dact[:, hs]).astype(dp_ref.dtype)
            dout = dyb_h * act[:, hs]
            gg_ref[:, hs] += jnp.sum(dout * on, axis=0, keepdims=True)
            don = dout * gain_h
            do = rstd * (don - jnp.mean(don, axis=-1, keepdims=True)
                         - on * jnp.mean(don * on, axis=-1, keepdims=True))
            gnext = gstate[h]
            gnext_b = gnext.astype(MXU)
            dscores = _mm_nt(do, v) * decay
            dp_ref[:, col(2)] = (_mm_tn(scores, do) + _mm(kr * zeta, gnext_b)).astype(dp_ref.dtype)
            dqr = _mm(dscores, kr) + _mm_nt(do, s_prev_b) * xi
            dkr = _mm_tn(dscores, qr) + _mm_nt(v, gnext_b) * zeta
            gstate[h] = gnext * math.exp(LOG_G[h] * CH) + _mm_tn(qr * xi, do)
            dp_ref[:, col(0)] = _unrot(dqr, cos, sin).astype(dp_ref.dtype)
            dp_ref[:, col(1)] = (_unrot(dkr, cos, sin) * (DK ** -0.5)).astype(dp_ref.dtype)

    col = lambda c: pl.BlockSpec((CH, D), lambda i: (rev(i), c))
    rope = pl.BlockSpec((CH, DK // 2), lambda i: (rev(i), 0))
    outc = col(0)
    act_out = SDS((L, D), MXU)
    return pl.pallas_call(
        body, name="ret_bwd", grid=(nb,),
        in_specs=[col(2), col(3), col(4), col(5), rope, rope, _full((1, D)),
                  pl.BlockSpec((None, HEADS, DK, DK), lambda i: (rev(i), 0, 0, 0)),
                  pl.BlockSpec((CH, HEADS * CH), lambda i: (rev(i), 0)), outc,
                  outc, pl.BlockSpec((D, D), lambda i: (1, 0)), _lane_block_spec(CH, rev), outc]
        + [ANY_SPEC] * len(deps),
        out_specs=[pl.BlockSpec((CH, NIN), lambda i: (rev(i), 0)), outc, _full((1, D))],
        out_shape=[SDS((L, NIN), MXU), act_out, SDS((1, D), F32)],
        scratch_shapes=[pltpu.VMEM((HEADS, DK, DK), F32), pltpu.VMEM((HEADS, CH, CH), F32),
                        pltpu.VMEM((HEADS, CH, 256), F32)],
        compiler_params=_cp(48, ("arbitrary",)),
    )(p, p, p, p, cos, sin, gain, states, scores, o, dx1, wout_e, du, daz, *deps)


def _out_even(x, ypre3, p, yb, wglu, bglu, wout):
    L = x.shape[0]
    tm = min(512, L)

    def body(x_ref, y_ref, az_ref, yb_ref, wg_ref, bg_ref, w_ref, o_ref):
        yg = _gelu(_from_lane_blocks(y_ref))
        t = _mm(yg, wg_ref[...]) + bg_ref[...]
        act, _ = _silu_and_grad(az_ref[...])
        ya = (yg * jax.nn.sigmoid(t) * act).astype(MXU)
        cat = jnp.concatenate([ya, yb_ref[...]], axis=1)
        o_ref[...] = x_ref[...] + jnp.dot(cat, w_ref[...], preferred_element_type=F32)

    row = pl.BlockSpec((tm, D), lambda i: (i, 0))
    return pl.pallas_call(
        body, name="out_even", grid=(L // tm,),
        in_specs=[row, _lane_block_spec(tm, lambda i: i), pl.BlockSpec((tm, D), lambda i: (i, 1)), row,
                  _full((D, D)), _full((1, D)), _full((DI, D))],
        out_specs=row, out_shape=SDS((L, D), F32), compiler_params=_cp(48, ("arbitrary",)),
    )(x, ypre3, p, yb, wglu, bglu, wout)


def _sgu_core(pv, gain, ws_ref, bs_ref):
    pu, pvv, z = pv[:, :DI], pv[:, DI:2 * DI], pv[:, 2 * DI:]
    u, gu = _gelu_and_grad(pu)
    v, gv = _gelu_and_grad(pvv)
    mu = jnp.mean(v, axis=-1, keepdims=True)
    vc = v - mu
    rstd = lax.rsqrt(jnp.mean(vc * vc, axis=-1, keepdims=True) + EPS)
    vhat = vc * rstd
    vn = vhat * gain
    t = lax.broadcasted_iota(jnp.int32, (CH, CH), 0)
    s_ = lax.broadcasted_iota(jnp.int32, (CH, CH), 1)
    mask = t >= s_
    wm = [jnp.where(mask, ws_ref[g], 0.0).astype(MXU) for g in range(SG)]
    s = jnp.concatenate([_mm(wm[g], vn[:, g * SGD:(g + 1) * SGD]) + bs_ref[g] for g in range(SG)], axis=1)
    return gu, gv, z, u, vhat, rstd, vn, mask, wm, s


def _sgu_fwd_bwd(p2, x1, gain, wsp, bsp, wout, fnorm, tgt):
    L = p2.shape[0]

    def body(p_ref, x1_ref, g_ref, ws_ref, bs_ref, wo_ref, fn_ref, t_ref,
             dp_ref, y_ref, dx2_ref, gg_ref, gws_ref, gbs_ref, gfn_ref, loss_ref):
        @pl.when(pl.program_id(0) == 0)
        def _():
            gg_ref[...] = jnp.zeros_like(gg_ref)
            gws_ref[...] = jnp.zeros_like(gws_ref)
            gbs_ref[...] = jnp.zeros_like(gbs_ref)
            gfn_ref[...] = jnp.zeros_like(gfn_ref)
            loss_ref[...] = jnp.zeros_like(loss_ref)

        gain = g_ref[...]
        gu, gv, z, u, vhat, rstd, vn, mask, wm, s = _sgu_core(p_ref[...], gain, ws_ref, bs_ref)
        act, dact = _silu_and_grad(z)
        y = (u * s * act).astype(MXU)
        y_ref[...] = y
        x2 = x1_ref[...] + jnp.dot(y, wo_ref[...], preferred_element_type=F32)
        xhat, r = _rms(x2)
        fn = fn_ref[...]
        e = xhat * fn - t_ref[...]
        loss_ref[...] += 0.5 * jnp.sum(jnp.mean(e * e, axis=-1, keepdims=True), axis=0, keepdims=True)
        do = e * (1.0 / D)
        gfn_ref[...] += jnp.sum(do * xhat, axis=0, keepdims=True)
        dxhat = do * fn
        dx2 = r * (dxhat - xhat * jnp.mean(dxhat * xhat, axis=-1, keepdims=True))
        dx2_ref[...] = dx2
        dy = _mm_nt(dx2, wo_ref[...])
        du = dy * s * act
        ds = dy * u * act
        dz = dy * u * s * dact
        dvn = []
        for g in range(SG):
            ds_g = ds[:, g * SGD:(g + 1) * SGD]
            vn_g = vn[:, g * SGD:(g + 1) * SGD]
            gbs_ref[g] += jnp.sum(ds_g, axis=1, keepdims=True)
            gws_ref[g] += jnp.where(mask, _mm_nt(ds_g, vn_g), 0.0)
            dvn.append(_mm_tn(wm[g], ds_g))
        dvn = jnp.concatenate(dvn, axis=1)
        gg_ref[...] += jnp.sum(dvn * vhat, axis=0, keepdims=True)
        dvhat = dvn * gain
        dv = rstd * (dvhat - jnp.mean(dvhat, axis=-1, keepdims=True)
                     - vhat * jnp.mean(dvhat * vhat, axis=-1, keepdims=True))
        dp_ref[...] = jnp.concatenate([du * gu, dv * gv, dz], axis=1).astype(dp_ref.dtype)

    row = pl.BlockSpec((CH, D), lambda i: (i, 0))
    wide = pl.BlockSpec((CH, NIN), lambda i: (i, 0))
    return pl.pallas_call(
        body, name="sgu_fwd_bwd", grid=(L // CH,),
        in_specs=[wide, row, _full((1, DI)), _full((SG, CH, CH)), _full((SG, CH, 1)), _full((DI, D)),
                  _full((1, D)), row],
        out_specs=[wide, pl.BlockSpec((CH, DI), lambda i: (i, 0)), row,
                   _full((1, DI)), _full((SG, CH, CH)), _full((SG, CH, 1)), _full((1, D)), _full((1, 128))],
        out_shape=[SDS((L, NIN), MXU), SDS((L, DI), MXU), SDS((L, D), F32), SDS((1, DI), F32),
                   SDS((SG, CH, CH), F32), SDS((SG, CH, 1), F32), SDS((1, D), F32), SDS((1, 128), F32)],
        compiler_params=_cp(48, ("arbitrary",)),
    )(p2, x1, gain, wsp, bsp, wout, fnorm, tgt)


def _my_index():
    return 4 * lax.axis_index("x") + 2 * lax.axis_index("y") + lax.axis_index("c")


def _ordered_sum(land_ref, own, me):
    g = None
    for s in range(NDEV):
        part = jnp.where(me == s, own, land_ref[s].astype(F32))
        g = part if g is None else g + part
    return g


def _adamw_math(w, m, v, g):
    mn = ADAM_B1 * m + (1.0 - ADAM_B1) * g
    vn = ADAM_B2 * v + (1.0 - ADAM_B2) * (g * g)
    mhat = mn / BC1
    vhat = vn / BC2
    return g, -ADAM_LR * (mhat / (jnp.sqrt(vhat) + ADAM_EPS) + ADAM_WD * w), mn, vn


def _adamw(w, m, v, land, own, name):
    R, C = w.shape
    tr = R
    for cand in (256, 128, 64, 32, 16, 8):
        if R % cand == 0 and R > cand:
            tr = cand
            break

    def body(w_ref, m_ref, v_ref, land_ref, own_ref, g_ref, d_ref, mo_ref, vo_ref):
        g = _ordered_sum(land_ref, own_ref[...].astype(F32), _my_index())
        for o, val in zip((g_ref, d_ref, mo_ref, vo_ref), _adamw_math(w_ref[...], m_ref[...], v_ref[...], g)):
            o[...] = val

    blk = pl.BlockSpec((tr, C), lambda i: (i, 0))
    out = SDS((R, C), F32)
    return pl.pallas_call(
        body, name=name, grid=(R // tr,),
        in_specs=[blk, blk, blk, pl.BlockSpec((NDEV, tr, C), lambda i: (0, i, 0)), blk],
        out_specs=[blk, blk, blk, blk], out_shape=[out, out, out, out],
        compiler_params=_cp(40, ("arbitrary",)),
    )(w, m, v, land, own)


def _adamw_many(ws, ms, vs, lands, owns, name):
    k = len(ws)

    def body(*refs):
        ins, outs = refs[:5 * k], refs[5 * k:]
        me = _my_index()
        for i in range(k):
            w_ref, m_ref, v_ref, land_ref, own_ref = (ins[j * k + i] for j in range(5))
            g = _ordered_sum(land_ref, own_ref[...], me)
            for j, val in enumerate(_adamw_math(w_ref[...], m_ref[...], v_ref[...], g)):
                outs[j * k + i][...] = val

    out_shape = [SDS(w.shape, F32) for _ in range(4) for w in ws]
    res = pl.pallas_call(body, name=name, out_shape=out_shape, compiler_params=_cp(60))(*ws, *ms, *vs, *lands, *owns)
    return [res[j * k:(j + 1) * k] for j in range(4)]


MESH = pl.DeviceIdType.MESH
HBM_SPEC = pl.BlockSpec(memory_space=pltpu.HBM)
SEM_SPEC = pl.BlockSpec(memory_space=pltpu.SEMAPHORE)
EFFECT = pltpu.SideEffectType.DATAFLOW_SIDE_EFFECTING


def _me_and_peers():
    x, y, c = lax.axis_index("x"), lax.axis_index("y"), lax.axis_index("c")
    me = 4 * x + 2 * y + c
    peers = []
    for r in range(1, NDEV):
        px, py, pc = x ^ ((r >> 2) & 1), y ^ ((r >> 1) & 1), c ^ (r & 1)
        peers.append(((px, py, pc), 4 * px + 2 * py + pc))
    return me, peers


def _land_shape(a, scatter):
    return (NDEV,) + (a.shape[1:] if scatter else a.shape)


def _remote(src, dst, send_sems, recv_sems, r, k, n, dev):
    i = r * n + k
    return pltpu.make_async_remote_copy(src_ref=src, dst_ref=dst, send_sem=send_sems.at[i], recv_sem=recv_sems.at[i],
                                        device_id=dev, device_id_type=MESH)


def _exchange(arrays, scatter, name):
    n = len(arrays)
    out_shape = [SDS(_land_shape(a, scatter), a.dtype) for a in arrays]

    def body(*refs):
        ins, outs = refs[:n], refs[n:2 * n]
        send_sems, recv_sems, loc_sems = refs[2 * n:]
        me, peers = _me_and_peers()
        local = []
        for k in range(n):
            src = ins[k].at[me] if scatter else ins[k]
            cp = pltpu.make_async_copy(src, outs[k].at[me], loc_sems.at[k])
            cp.start()
            local.append(cp)
        sends = []
        for r, (dev, lin) in enumerate(peers):
            for k in range(n):
                src = ins[k].at[lin] if scatter else ins[k]
                cp = _remote(src, outs[k].at[me], send_sems, recv_sems, r, k, n, dev)
                cp.start()
                sends.append(cp)
        for r, (dev, lin) in enumerate(peers):
            for k in range(n):
                src = ins[k].at[me] if scatter else ins[k]
                _remote(src, outs[k].at[lin], send_sems, recv_sems, r, k, n, dev).wait_recv()
        for cp in sends:
            cp.wait_send()
        for cp in local:
            cp.wait()

    return pl.pallas_call(
        body, name=name, in_specs=[HBM_SPEC] * n, out_specs=[HBM_SPEC] * n, out_shape=out_shape,
        scratch_shapes=[pltpu.SemaphoreType.DMA(((NDEV - 1) * n,)), pltpu.SemaphoreType.DMA(((NDEV - 1) * n,)),
                        pltpu.SemaphoreType.DMA((n,))],
    )(*arrays)


def _exchange_start(arrays, scatter, name):
    n = len(arrays)
    lands = [lax.empty(_land_shape(a, scatter), a.dtype) for a in arrays]

    def body(*refs):
        ins, lnd = refs[:n], refs[n:2 * n]
        send_sems, recv_sems, own_sems = refs[2 * n:2 * n + 3]
        token = refs[-1]
        me, peers = _me_and_peers()
        for r, (dev, lin) in enumerate(peers):
            for k in range(n):
                src = ins[k].at[lin] if scatter else ins[k]
                _remote(src, lnd[k].at[me], send_sems, recv_sems, r, k, n, dev).start()
        if not scatter:
            for k in range(n):
                pltpu.make_async_copy(ins[k], lnd[k].at[me], own_sems.at[k]).start()
        token[...] = jnp.zeros_like(token)

    sem = pltpu.SemaphoreType.DMA(((NDEV - 1) * n,))
    outs = pl.pallas_call(
        body, name=name,
        out_shape=(sem, sem, pltpu.SemaphoreType.DMA((n,)), *[pltpu.HBM(a.shape, a.dtype) for a in arrays],
                   *[pltpu.HBM(l.shape, l.dtype) for l in lands], SDS((8, 128), F32)),
        in_specs=[HBM_SPEC] * (2 * n),
        out_specs=(SEM_SPEC, SEM_SPEC, SEM_SPEC, *[HBM_SPEC] * (2 * n), pl.BlockSpec(memory_space=pltpu.VMEM)),
        input_output_aliases={k: 3 + k for k in range(2 * n)},
        compiler_params=pltpu.CompilerParams(has_side_effects=EFFECT),
    )(*[pltpu.with_memory_space_constraint(a, pltpu.HBM) for a in arrays],
      *[pltpu.with_memory_space_constraint(l, pltpu.HBM) for l in lands])
    return (n, scatter, outs[0], outs[1], outs[2], outs[3:3 + n], outs[3 + n:3 + 2 * n]), outs[-1]


def _exchange_wait(handle, after, name):
    n, scatter, send_sems, recv_sems, own_sems, thru, lands = handle
    after = tuple(after)

    def body(*refs):
        ins, lnd = refs[:n], refs[n:2 * n]
        send_sems, recv_sems, own_sems = refs[2 * n:2 * n + 3]
        me, peers = _me_and_peers()
        for r, (dev, lin) in enumerate(peers):
            for k in range(n):
                src = ins[k].at[lin] if scatter else ins[k]
                cp = _remote(src, lnd[k].at[lin], send_sems, recv_sems, r, k, n, dev)
                cp.wait_send()
                cp.wait_recv()
        if not scatter:
            for k in range(n):
                pltpu.make_async_copy(ins[k], lnd[k].at[me], own_sems.at[k]).wait()

    outs = pl.pallas_call(
        body, name=name,
        out_shape=(*[pltpu.HBM(a.shape, a.dtype) for a in thru], *[pltpu.HBM(l.shape, l.dtype) for l in lands]),
        in_specs=[HBM_SPEC] * (2 * n) + [SEM_SPEC, SEM_SPEC, SEM_SPEC] + [ANY_SPEC] * len(after),
        out_specs=tuple([HBM_SPEC] * (2 * n)),
        input_output_aliases={k: k for k in range(2 * n)},
        compiler_params=pltpu.CompilerParams(has_side_effects=EFFECT),
    )(*thru, *lands, send_sems, recv_sems, own_sems, *after)
    return list(outs[:n]), list(outs[n:])


CHIP_RELATIONS = (2, 4, 6)


def _peer(r):
    x, y, c = lax.axis_index("x"), lax.axis_index("y"), lax.axis_index("c")
    px, py, pc = x ^ ((r >> 2) & 1), y ^ ((r >> 1) & 1), c ^ (r & 1)
    return (px, py, pc), 4 * px + 2 * py + pc


def _copy(src, dst, send_sems, recv_sems, i, dev):
    return pltpu.make_async_remote_copy(src_ref=src, dst_ref=dst, send_sem=send_sems.at[i], recv_sem=recv_sems.at[i],
                                        device_id=dev, device_id_type=MESH)


def _gather2_start(a, name):
    land = lax.empty((NDEV,) + a.shape, a.dtype)

    def body(own, lnd, send_sems, recv_sems, own_sem, own_thru, lnd_thru, token):
        me = _my_index()
        for i, r in enumerate((1,) + CHIP_RELATIONS):
            dev, _ = _peer(r)
            _copy(own, lnd.at[me], send_sems, recv_sems, i, dev).start()
        pltpu.make_async_copy(own, lnd.at[me], own_sem.at[0]).start()
        token[...] = jnp.zeros_like(token)

    sem4 = pltpu.SemaphoreType.DMA((4,))
    outs = pl.pallas_call(
        body, name=name,
        out_shape=(sem4, sem4, pltpu.SemaphoreType.DMA((1,)), pltpu.HBM(a.shape, a.dtype),
                   pltpu.HBM(land.shape, land.dtype), SDS((8, 128), F32)),
        in_specs=[HBM_SPEC, HBM_SPEC],
        out_specs=(SEM_SPEC, SEM_SPEC, SEM_SPEC, HBM_SPEC, HBM_SPEC, pl.BlockSpec(memory_space=pltpu.VMEM)),
        input_output_aliases={0: 3, 1: 4},
        compiler_params=pltpu.CompilerParams(has_side_effects=EFFECT),
    )(pltpu.with_memory_space_constraint(a, pltpu.HBM), pltpu.with_memory_space_constraint(land, pltpu.HBM))
    return outs[:5], outs[5]


def _gather2_forward(handle, after, name):
    send_sems, recv_sems, own_sem, own, land = handle
    after = tuple(after)

    def body(lnd, recv_sems, *rest):
        send2, recv2, lnd_thru = rest[len(after):]
        sib, _ = _peer(1)
        for k, r in enumerate(CHIP_RELATIONS):
            dev, lin = _peer(r)
            _copy(lnd.at[lin], lnd.at[lin], recv_sems, recv_sems, 1 + k, dev).wait_recv()
            _copy(lnd.at[lin], lnd.at[lin], send2, recv2, k, sib).start()

    sem3 = pltpu.SemaphoreType.DMA((3,))
    send2, recv2, land = pl.pallas_call(
        body, name=name,
        out_shape=(sem3, sem3, pltpu.HBM(land.shape, land.dtype)),
        in_specs=[HBM_SPEC, SEM_SPEC] + [ANY_SPEC] * len(after),
        out_specs=(SEM_SPEC, SEM_SPEC, HBM_SPEC),
        input_output_aliases={0: 2},
        compiler_params=pltpu.CompilerParams(has_side_effects=EFFECT),
    )(land, recv_sems, *after)
    return send_sems, recv_sems, own_sem, send2, recv2, own, land


def _gather2_wait(handle, name):
    send_sems, recv_sems, own_sem, send2, recv2, own, land = handle

    def body(own_ref, lnd, send_sems, recv_sems, own_sem, send2, recv2, own_thru, lnd_thru):
        me = _my_index()
        sib, sib_lin = _peer(1)
        for i, r in enumerate((1,) + CHIP_RELATIONS):
            dev, _ = _peer(r)
            _copy(own_ref, lnd.at[me], send_sems, recv_sems, i, dev).wait_send()
        _copy(own_ref, lnd.at[sib_lin], send_sems, recv_sems, 0, sib).wait_recv()
        for k, r in enumerate(CHIP_RELATIONS):
            _, lin = _peer(r)
            _, lin_other = _peer(r ^ 1)
            _copy(lnd.at[lin], lnd.at[lin], send2, recv2, k, sib).wait_send()
            _copy(lnd.at[lin_other], lnd.at[lin_other], send2, recv2, k, sib).wait_recv()
        pltpu.make_async_copy(own_ref, lnd.at[me], own_sem.at[0]).wait()

    outs = pl.pallas_call(
        body, name=name,
        out_shape=(pltpu.HBM(own.shape, own.dtype), pltpu.HBM(land.shape, land.dtype)),
        in_specs=[HBM_SPEC, HBM_SPEC] + [SEM_SPEC] * 5,
        out_specs=(HBM_SPEC, HBM_SPEC),
        input_output_aliases={0: 0, 1: 1},
        compiler_params=pltpu.CompilerParams(has_side_effects=EFFECT),
    )(own, land, send_sems, recv_sems, own_sem, send2, recv2)
    return outs[1]


def _local_step(x, tgt, norm_even, first_weight, lam_re, lam_im, log_dt, b_re, b_im, c_re, c_im, s5_d, bglu,
                ret_gain, wsp, bsp, fnorm, late_weights, emit, start_token=None):
    L = x.shape[0]
    lr3, li3 = lam_re.reshape(G, 1, P), lam_im.reshape(G, 1, P)
    dt3 = log_dt.reshape(G, 1, 1)
    br3, bi3 = jnp.swapaxes(b_re, 1, 2), jnp.swapaxes(b_im, 1, 2)
    abr3, abi3, bbr3, bbi3 = _s5_disc(lr3, li3, dt3, br3, bi3)
    bb = jnp.concatenate([_embed(bbr3), _embed(bbi3)], axis=2).astype(MXU)
    cm = jnp.concatenate([_embed(jnp.swapaxes(c_re, 1, 2)), -_embed(jnp.swapaxes(c_im, 1, 2))], axis=1).astype(MXU)
    abr, abi = abr3.reshape(1, NSTATE), abi3.reshape(1, NSTATE)
    pwr, pwi, _, _ = _s5_tables(abr, abi, NT, "s5_tables_step")
    par, pai, pbr, pbi = _s5_tables(pwr[NT - 1:NT], pwi[NT - 1:NT], CH, "s5_tables_chunk")
    inv = (ROPE_BASE ** (-jnp.arange(DK // 2, dtype=F32) / (DK // 2))).reshape(1, DK // 2)
    cos, sin = _rope_tables(L, inv)
    bsp3 = bsp.reshape(SG, CH, 1)

    def dep(token):
        return NO_DEPS if token is None else (token,)

    win_e = first_weight((cos, pbi, cm))
    p, u3 = _in_proj(x, norm_even, win_e, "in_even", dep(start_token), lane_blocks=True)
    ypre, s5_states = _s5_scan_fwd(u3, bb, cm, pwr, pwi, pwr.T, pwi.T, par, pai, s5_d)
    yb, ret_states, ret_scores, ret_o = _ret_fwd(p, cos, sin, ret_gain)
    wglu, wout_e, norm_odd, win_o, sgu_gain, wout_o = late_weights((ypre, yb))
    x1 = _out_even(x, ypre, p, yb, wglu, bglu, wout_e)
    p2 = _in_proj(x1, norm_odd, win_o, "in_odd")
    dp2, y_o, dx2, g_sgu_gain, g_wsp, g_bsp, g_fnorm, loss = _sgu_fwd_bwd(
        p2, x1, sgu_gain, wsp, bsp3, wout_o, fnorm, tgt)

    g_wout_o = _wgrad_rows([y_o], dx2, "wgrad_out_odd")
    g_win_o = _wgrad_cols(x1, norm_odd, dp2, "wgrad_in_odd")
    tok = emit("odd", dict(w_in_odd=g_win_o, w_out_odd=g_wout_o))
    dx1, g_norm_odd = _in_proj_bwd_x(dp2, x1, norm_odd, win_o, dx2, "in_odd_bwd", dep(tok))
    tok = emit("small_odd", dict(norm_odd=g_norm_odd, sgu_norm_gain=g_sgu_gain, sgu_w_spatial=g_wsp,
                                 sgu_b_spatial=g_bsp.reshape(SG, CH), final_norm=g_fnorm))

    dypre, daz, yg, dt, ya2, g_bglu = _s5_gate_bwd(ypre, p, dx1, wout_e, wglu, bglu, dep(tok))
    g_wglu = _wgrad_rows([yg], dt, "wgrad_glu")
    tok = emit("glu", dict(s5_w_glu=g_wglu))
    du, g_d, g_cm, g_bb, g_ar, g_ai = _s5_scan_bwd(u3, dypre, s5_states, bb, cm, jnp.swapaxes(cm, 1, 2), abr, abi,
                                                   pwr, pwi, par, pai, pbr, pbi, s5_d, dep(tok))
    dbbr3 = _diag_blocks(g_bb[:, :, :SW], HG, P)
    dbbi3 = _diag_blocks(g_bb[:, :, SW:], HG, P)
    g_c_re = jnp.swapaxes(_diag_blocks(g_cm[:, :SW, :], P, HG), 1, 2)
    g_c_im = -jnp.swapaxes(_diag_blocks(g_cm[:, SW:, :], P, HG), 1, 2)
    g_lr3, g_li3, g_dt3, g_br3, g_bi3 = _s5_disc_bwd(
        lr3, li3, dt3, br3, bi3, g_ar.reshape(G, 1, P), g_ai.reshape(G, 1, P), dbbr3, dbbi3)
    tok = emit("small_s5", dict(
        s5_lam_re=g_lr3.reshape(G, P), s5_lam_im=g_li3.reshape(G, P), s5_log_dt=g_dt3.reshape(1, G),
        s5_b_re=g_br3, s5_b_im=g_bi3, s5_c_re=g_c_re, s5_c_im=g_c_im, s5_d=g_d, s5_b_glu=g_bglu))
    dp, yb2, g_ret_gain = _ret_bwd(p, cos, sin, ret_gain, ret_states, ret_scores, ret_o, dx1, wout_e, du, daz,
                                   dep(tok))
    g_win_e = _wgrad_cols(x, norm_even, dp, "wgrad_in_even")
    tok = emit("even_cols", dict(w_in_even=g_win_e))
    g_wout_e = _wgrad_rows([ya2, yb2], dx1, "wgrad_out_even", dep(tok))
    tok = emit("even_rows", dict(w_out_even=g_wout_e))
    dx, g_norm_even = _in_proj_bwd_x(dp, x, norm_even, win_e, dx1, "in_even_bwd", dep(tok))
    emit("last", dict(ret_gn_gain=g_ret_gain, norm_even=g_norm_even, loss=loss))
    return dx


WEIGHTS = ['norm_even', 'w_in_even', 's5_lam_re', 's5_lam_im', 's5_log_dt', 's5_b_re', 's5_b_im', 's5_c_re',
           's5_c_im', 's5_d', 's5_w_glu', 's5_b_glu', 'ret_gn_gain', 'w_out_even', 'norm_odd', 'w_in_odd',
           'sgu_norm_gain', 'sgu_w_spatial', 'sgu_b_spatial', 'w_out_odd', 'final_norm']
BIG = ['w_in_even', 's5_w_glu', 'w_out_even', 'w_in_odd', 'w_out_odd']
SHARDED_SMALL = {'norm_odd': D // NDEV, 'sgu_norm_gain': DI // NDEV}
SCATTER_STAGES = ("odd", "glu", "even_cols", "even_rows")
GATHER_STAGES = ("small_odd", "small_s5")


def _view(n, a):
    if n in ('s5_b_re', 's5_b_im'):
        return jnp.swapaxes(a[0], 1, 2)
    if n == 'final_norm':
        return a.reshape(1, D)
    return a[0] if a.ndim >= 3 else a


def _unview(n, t, shape):
    if n in ('s5_b_re', 's5_b_im'):
        return jnp.swapaxes(t, 1, 2)[None]
    return t.reshape(shape)


def kernel(x, norm_even, w_in_even, s5_lam_re, s5_lam_im, s5_log_dt, s5_b_re, s5_b_im, s5_c_re, s5_c_im, s5_d, s5_w_glu, s5_b_glu, ret_gn_gain, w_out_even, norm_odd, w_in_odd, sgu_norm_gain, sgu_w_spatial, sgu_b_spatial, w_out_odd, final_norm, loss_target, m_norm_even, m_w_in_even, m_s5_lam_re, m_s5_lam_im, m_s5_log_dt, m_s5_b_re, m_s5_b_im, m_s5_c_re, m_s5_c_im, m_s5_d, m_s5_w_glu, m_s5_b_glu, m_ret_gn_gain, m_w_out_even, m_norm_odd, m_w_in_odd, m_sgu_norm_gain, m_sgu_w_spatial, m_sgu_b_spatial, m_w_out_odd, m_final_norm, v_norm_even, v_w_in_even, v_s5_lam_re, v_s5_lam_im, v_s5_log_dt, v_s5_b_re, v_s5_b_im, v_s5_c_re, v_s5_c_im, v_s5_d, v_s5_w_glu, v_s5_b_glu, v_ret_gn_gain, v_w_out_even, v_norm_odd, v_w_in_odd, v_sgu_norm_gain, v_sgu_w_spatial, v_sgu_b_spatial, v_w_out_odd, v_final_norm):
    args = dict(locals())
    w = {n: args[n] for n in WEIGHTS}
    m = {n: args["m_" + n] for n in WEIGHTS}
    v = {n: args["v_" + n] for n in WEIGHTS}
    me = _my_index()

    first_handle, _ = _gather2_start(w['w_in_even'][0].astype(MXU), "gather_first_start")

    def first_weight(after):
        return _gather2_wait(_gather2_forward(first_handle, after, "gather_first_forward"), "gather_first_wait")

    late_own = [w['s5_w_glu'][0].astype(MXU), w['w_out_even'][0].astype(MXU), w['norm_odd'],
                w['w_in_odd'][0].astype(MXU), w['sgu_norm_gain'], w['w_out_odd'][0].astype(MXU)]
    late_handle, start_token = _exchange_start(late_own, False, "gather_late_start")

    def late_weights(after):
        _, (wglu, wout_e, nodd, win_o, sgug, wout_o) = _exchange_wait(late_handle, after, "gather_late_wait")
        return (wglu.reshape(D, D), wout_e.reshape(DI, D), nodd.reshape(1, D), win_o, sgug.reshape(1, DI),
                wout_o.reshape(DI, D))

    pending = {}
    small_last = {}

    def emit(stage, grads):
        if stage == "last":
            small_last.update(grads)
            return None
        names = list(grads)
        handle, token = _exchange_start([grads[n] for n in names], stage in SCATTER_STAGES, stage + "_start")
        pending[stage] = (handle, names)
        return token

    dx = _local_step(
        x[0], loss_target[0], w['norm_even'], first_weight, w['s5_lam_re'][0], w['s5_lam_im'][0], w['s5_log_dt'][0],
        w['s5_b_re'][0], w['s5_b_im'][0], w['s5_c_re'][0], w['s5_c_im'][0], w['s5_d'], w['s5_b_glu'],
        w['ret_gn_gain'], w['sgu_w_spatial'][0], w['sgu_b_spatial'][0], w['final_norm'].reshape(1, D),
        late_weights, emit, start_token)

    out_g, out_d, out_m, out_v = {}, {}, {}, {}
    after = dx
    for stage in SCATTER_STAGES:
        handle, names = pending[stage]
        sent, lands = _exchange_wait(handle, (after,), stage + "_wait")
        for n, land, stack in zip(names, lands, sent):
            shp = w[n].shape
            r, c = shp[1], shp[2]
            own = lax.dynamic_index_in_dim(stack, me, 0, keepdims=False)
            res = _adamw(w[n].reshape(r, c), m[n].reshape(r, c), v[n].reshape(r, c), land, own, "adamw_" + n)
            out_g[n], out_d[n], out_m[n], out_v[n] = (t.reshape(shp) for t in res)
            after = res[0]

    names, owns, lands = [], [], []
    for stage in GATHER_STAGES:
        handle, stage_names = pending[stage]
        sent, got = _exchange_wait(handle, (after,), stage + "_wait")
        names, owns, lands = names + stage_names, owns + sent, lands + got
    last_names = list(small_last)
    last = _exchange([small_last[n] for n in last_names], False, "gather_last")
    loss_parts = last[last_names.index("loss")][:, 0, 0]
    for n, own, land in zip(last_names, (small_last[n] for n in last_names), last):
        if n != "loss":
            names, owns, lands = names + [n], owns + [own], lands + [land]
    for i, n in enumerate(names):
        if n in SHARDED_SMALL:
            width = SHARDED_SMALL[n]
            owns[i] = lax.dynamic_slice_in_dim(owns[i], me * width, width, axis=1)
            lands[i] = lax.dynamic_slice_in_dim(lands[i], me * width, width, axis=2)
    res = _adamw_many([_view(n, w[n]) for n in names], [_view(n, m[n]) for n in names],
                      [_view(n, v[n]) for n in names], lands, owns, "adamw_small")
    for dst, vals in zip((out_g, out_d, out_m, out_v), res):
        for n, t in zip(names, vals):
            dst[n] = _unview(n, t, w[n].shape)

    loss_total = jnp.sum(loss_parts)
    return (loss_total, dx[None], *[out_g[n] for n in WEIGHTS], *[out_d[n] for n in WEIGHTS],
            *[out_m[n] for n in WEIGHTS], *[out_v[n] for n in WEIGHTS])
```

```python
import math

import jax
import jax.numpy as jnp
from jax import lax
from jax.experimental import pallas as pl
from jax.experimental.pallas import tpu as pltpu

F32 = jnp.float32
MXU = jnp.bfloat16
AXES = ("x", "y", "c")
NDEV = 8
D = 1024
NIN = 6144
WIN_BLK = NIN // NDEV
DI = 2048
G, P, HG = 64, 64, 16
GB = 8
NJ = G // GB
SW = GB * P
UW = GB * HG
NSTATE = G * P
HEADS, DK = 4, 256
CH = 128
SG, SGD = 4, 512
EPS = 1e-6
ROPE_BASE = 10000.0
VMEM_CAP_V7X = 64 * 1024 * 1024
LOG_G = [math.log1p(-2.0 ** (-5.0 - h)) for h in range(HEADS)]
GELU_C = math.sqrt(2.0 / math.pi)

ADAM_LR, ADAM_B1, ADAM_B2, ADAM_EPS, ADAM_WD, ADAM_STEP = 0.001, 0.9, 0.999, 1e-08, 0.01, 10
BC1 = 1.0 - ADAM_B1 ** ADAM_STEP
BC2 = 1.0 - ADAM_B2 ** ADAM_STEP

SDS = jax.ShapeDtypeStruct
ARB2 = ("arbitrary", "arbitrary")


def _cp(vmem_mib, sem=None):
    kw = dict(vmem_limit_bytes=min(vmem_mib * 1024 * 1024, VMEM_CAP_V7X - 4 * 1024 * 1024))
    if sem is not None:
        kw["dimension_semantics"] = sem
    return pltpu.CompilerParams(**kw)


def _mm(a, b):
    return jnp.dot(a.astype(MXU), b.astype(MXU), preferred_element_type=F32)


def _mm_nt(a, b):
    return lax.dot_general(a.astype(MXU), b.astype(MXU), (((1,), (1,)), ((), ())), preferred_element_type=F32)


def _mm_tn(a, b):
    return lax.dot_general(a.astype(MXU), b.astype(MXU), (((0,), (0,)), ((), ())), preferred_element_type=F32)


def _gelu(x):
    return _gelu_and_grad(x)[0]


def _gelu_and_grad(x):
    x2 = x * x
    th = jnp.tanh(GELU_C * x * (1.0 + 0.044715 * x2))
    hp = 0.5 * (1.0 + th)
    return x * hp, hp + 0.5 * x * (1.0 - th * th) * GELU_C * (1.0 + 3.0 * 0.044715 * x2)


def _silu_and_grad(x):
    s = jax.nn.sigmoid(x)
    return x * s, s * (1.0 + x * (1.0 - s))


def _full(shape):
    nd = len(shape)
    return pl.BlockSpec(shape, lambda *_: (0,) * nd)


def _rms(xf):
    r = lax.rsqrt(jnp.mean(xf * xf, axis=-1, keepdims=True) + EPS)
    return xf * r, r


ANY_SPEC = pl.BlockSpec(memory_space=pl.ANY)
NO_DEPS = ()


def _load_once(src_hbm, dst_vmem, sem):
    @pl.when(pl.program_id(0) == 0)
    def _():
        cp = pltpu.make_async_copy(src_hbm, dst_vmem, sem)
        cp.start()
        cp.wait()


def _lane_blocks(L):
    return SDS((NJ, L, UW), F32)


def _lane_block_spec(rows, index):
    return pl.BlockSpec((NJ, rows, UW), lambda i: (0, index(i), 0))


def _from_lane_blocks(ref):
    return jnp.concatenate([ref[j] for j in range(NJ)], axis=1)


def _to_lane_blocks(ref, v):
    for j in range(NJ):
        ref[j] = v[:, j * UW:(j + 1) * UW].astype(ref.dtype)


def _in_proj(x, gain, wst, name, deps=NO_DEPS, lane_blocks=False):
    L = x.shape[0]
    tm = min(512, L)

    def body(x_ref, g_ref, w_hbm, *rest):
        outs = rest[len(deps):]
        o_ref, w_scr, sem = outs[0], outs[-2], outs[-1]
        _load_once(w_hbm, w_scr, sem)
        xhat, _ = _rms(x_ref[...])
        h = (xhat * g_ref[...]).astype(MXU)
        for c in range(NDEV):
            o_ref[:, c * WIN_BLK:(c + 1) * WIN_BLK] = jnp.dot(h, w_scr[c], preferred_element_type=F32)
        if lane_blocks:
            _to_lane_blocks(outs[1], o_ref[:, 0:D])

    p_spec, p_shape = pl.BlockSpec((tm, NIN), lambda i: (i, 0)), SDS((L, NIN), F32)
    return pl.pallas_call(
        body, name=name, grid=(L // tm,),
        in_specs=[pl.BlockSpec((tm, D), lambda i: (i, 0)), _full((1, D)), ANY_SPEC] + [ANY_SPEC] * len(deps),
        out_specs=[p_spec, _lane_block_spec(tm, lambda i: i)] if lane_blocks else p_spec,
        out_shape=[p_shape, _lane_blocks(L)] if lane_blocks else p_shape,
        scratch_shapes=[pltpu.VMEM((NDEV, D, WIN_BLK), MXU), pltpu.SemaphoreType.DMA(())],
        compiler_params=_cp(58, ("arbitrary",)),
    )(x, gain, wst, *deps)


def _in_proj_bwd_x(dp, x, gain, wst, dres, name, deps=NO_DEPS):
    L = x.shape[0]
    tm = min(512, L)

    def body(dp_ref, x_ref, g_ref, w_hbm, dres_ref, *rest):
        dx_ref, gg_ref, w_scr, sem = rest[len(deps):]
        _load_once(w_hbm, w_scr, sem)

        @pl.when(pl.program_id(0) == 0)
        def _():
            gg_ref[...] = jnp.zeros_like(gg_ref)

        dh = _mm_nt(dp_ref[:, 0:WIN_BLK], w_scr[0])
        for c in range(1, NDEV):
            dh += _mm_nt(dp_ref[:, c * WIN_BLK:(c + 1) * WIN_BLK], w_scr[c])
        xhat, r = _rms(x_ref[...])
        dxhat = dh * g_ref[...]
        dx_ref[...] = dres_ref[...] + r * (dxhat - xhat * jnp.mean(dxhat * xhat, axis=-1, keepdims=True))
        gg_ref[...] += jnp.sum(dh * xhat, axis=0, keepdims=True)

    row = pl.BlockSpec((tm, D), lambda i: (i, 0))
    return pl.pallas_call(
        body, name=name, grid=(L // tm,),
        in_specs=[pl.BlockSpec((tm, NIN), lambda i: (i, 0)), row, _full((1, D)), ANY_SPEC, row]
        + [ANY_SPEC] * len(deps),
        out_specs=[row, _full((1, D))],
        out_shape=[SDS((L, D), F32), SDS((1, D), F32)],
        scratch_shapes=[pltpu.VMEM((NDEV, D, WIN_BLK), MXU), pltpu.SemaphoreType.DMA(())],
        compiler_params=_cp(56, ("arbitrary",)),
    )(dp, x, gain, wst, dres, *deps)


def _wgrad_cols(x, gain, dp, name, deps=NO_DEPS):
    L = x.shape[0]
    tk = min(1024, L)
    nk = L // tk
    halves = 2
    nh = NDEV // halves

    def body(x_ref, g_ref, dp_ref, *rest):
        o_ref, acc = rest[len(deps):]
        k = pl.program_id(1)

        @pl.when(k == 0)
        def _():
            acc[...] = jnp.zeros_like(acc)

        xhat, _ = _rms(x_ref[...])
        acc[...] += _mm_tn(xhat * g_ref[...], dp_ref[...])

        @pl.when(k == nk - 1)
        def _():
            for c in range(nh):
                o_ref[c] = acc[:, c * WIN_BLK:(c + 1) * WIN_BLK].astype(o_ref.dtype)

    return pl.pallas_call(
        body, name=name, grid=(halves, nk),
        in_specs=[pl.BlockSpec((tk, D), lambda n, k: (k, 0)), _full((1, D)),
                  pl.BlockSpec((tk, nh * WIN_BLK), lambda n, k: (k, n))] + [ANY_SPEC] * len(deps),
        out_specs=pl.BlockSpec((nh, D, WIN_BLK), lambda n, k: (n, 0, 0)),
        out_shape=SDS((NDEV, D, WIN_BLK), MXU),
        scratch_shapes=[pltpu.VMEM((D, nh * WIN_BLK), F32)],
        compiler_params=_cp(56, ARB2),
    )(x, gain, dp, *deps)


def _wgrad_rows(a_parts, b, name, deps=NO_DEPS):
    L, N = b.shape
    na = len(a_parts)
    widths = [a.shape[1] for a in a_parts]
    M = sum(widths)
    tk = min(1024, L)
    nk = L // tk

    def body(*refs):
        a_refs, b_ref = refs[:na], refs[na]
        o_ref, acc = refs[na + 1 + len(deps):]
        k = pl.program_id(0)

        @pl.when(k == 0)
        def _():
            acc[...] = jnp.zeros_like(acc)

        bv = b_ref[...].astype(MXU)
        off = 0
        for a_ref, wd in zip(a_refs, widths):
            acc[off:off + wd, :] += _mm_tn(a_ref[...], bv)
            off += wd

        @pl.when(k == nk - 1)
        def _():
            o_ref[...] = acc[...].astype(o_ref.dtype).reshape(o_ref.shape)

    return pl.pallas_call(
        body, name=name, grid=(nk,),
        in_specs=[pl.BlockSpec((tk, wd), lambda k: (k, 0)) for wd in widths]
        + [pl.BlockSpec((tk, N), lambda k: (k, 0))] + [ANY_SPEC] * len(deps),
        out_specs=_full((NDEV, M // NDEV, N)),
        out_shape=SDS((NDEV, M // NDEV, N), MXU),
        scratch_shapes=[pltpu.VMEM((M, N), F32)],
        compiler_params=_cp(48, ("arbitrary",)),
    )(*a_parts, b, *deps)


def _s5_disc_fn(lr_raw, li, logdt, br, bi):
    lr = jnp.minimum(lr_raw, -1e-4)
    dt = jnp.exp(logdt)
    mag = jnp.exp(lr * dt)
    abr = mag * jnp.cos(li * dt)
    abi = mag * jnp.sin(li * dt)
    den = lr * lr + li * li
    nre = abr - 1.0
    nim = abi
    zr = (nre * lr + nim * li) / den
    zi = (nim * lr - nre * li) / den
    return abr, abi, zr * br - zi * bi, zr * bi + zi * br


def _s5_disc(lr, li, logdt, br, bi):
    def body(lr_ref, li_ref, dt_ref, br_ref, bi_ref, abr_ref, abi_ref, bbr_ref, bbi_ref):
        abr, abi, bbr, bbi = _s5_disc_fn(lr_ref[...], li_ref[...], dt_ref[...], br_ref[...], bi_ref[...])
        abr_ref[...] = abr
        abi_ref[...] = abi
        bbr_ref[...] = bbr
        bbi_ref[...] = bbi

    s1, s3 = SDS((G, 1, P), F32), SDS((G, HG, P), F32)
    return pl.pallas_call(body, name="s5_disc", out_shape=[s1, s1, s3, s3])(lr, li, logdt, br, bi)


def _s5_disc_bwd(lr, li, logdt, br, bi, dabr, dabi, dbbr, dbbi):
    def body(lr_ref, li_ref, dt_ref, br_ref, bi_ref, c0, c1, c2, c3, o0, o1, o2, o3, o4):
        _, vjp = jax.vjp(_s5_disc_fn, lr_ref[...], li_ref[...], dt_ref[...], br_ref[...], bi_ref[...])
        g = vjp((c0[...], c1[...], c2[...], c3[...]))
        for o, v in zip((o0, o1, o2, o3, o4), g):
            o[...] = v

    s1, s3 = SDS((G, 1, P), F32), SDS((G, HG, P), F32)
    return pl.pallas_call(body, name="s5_disc_bwd", out_shape=[s1, s1, SDS((G, 1, 1), F32), s3, s3])(
        lr, li, logdt, br, bi, dabr, dabi, dbbr, dbbi)


def _s5_tables(abr, abi, rows, name):
    def body(ar_ref, ai_ref, pfr, pfi, pbr, pbi):
        pfr[0:1, :] = ar_ref[...]
        pfi[0:1, :] = ai_ref[...]
        pbr[rows - 1:rows, :] = ar_ref[...]
        pbi[rows - 1:rows, :] = ai_ref[...]
        n = 1
        while n < rows:
            er, ei = pfr[n - 1:n, :], pfi[n - 1:n, :]
            xr, xi = pfr[0:n, :], pfi[0:n, :]
            pfr[n:2 * n, :] = er * xr - ei * xi
            pfi[n:2 * n, :] = er * xi + ei * xr
            yr, yi = pbr[rows - n:rows, :], pbi[rows - n:rows, :]
            pbr[rows - 2 * n:rows - n, :] = er * yr - ei * yi
            pbi[rows - 2 * n:rows - n, :] = er * yi + ei * yr
            n *= 2

    s = SDS((rows, NSTATE), F32)
    return pl.pallas_call(body, name=name, out_shape=[s, s, s, s], compiler_params=_cp(40))(abr, abi)


def _cscan(br, bi, pr_ref, pi_ref, reverse):
    T = br.shape[0]
    sign = -1.0 if reverse else 1.0
    row = lax.broadcasted_iota(jnp.int32, br.shape, 0)
    k = 1
    while k < T:
        akr = pr_ref[k - 1:k, :]
        aki = sign * pi_ref[k - 1:k, :]

        def shift(v):
            if k % 8 == 0:
                z = jnp.zeros((k, v.shape[1]), v.dtype)
                return jnp.concatenate([v[k:], z], 0) if reverse else jnp.concatenate([z, v[:T - k]], 0)
            if reverse:
                return jnp.where(row < T - k, pltpu.roll(v, T - k, 0), 0.0)
            return jnp.where(row >= k, pltpu.roll(v, k, 0), 0.0)

        sr, si = shift(br), shift(bi)
        br, bi = br + akr * sr - aki * si, bi + akr * si + aki * sr
        k *= 2
    return br, bi


def _embed(t):
    a, b = t.shape[1], t.shape[2]
    return jnp.einsum("jgab,gh->jgahb", t.reshape(NJ, GB, a, b), jnp.eye(GB, dtype=t.dtype)).reshape(NJ, GB * a, GB * b)


def _diag_blocks(t, a, b):
    return jnp.einsum("jgahb,gh->jgab", t.reshape(NJ, GB, a, GB, b), jnp.eye(GB, dtype=t.dtype)).reshape(G, a, b)


NT = 16


def _chunks(L):
    ncb = min(CH, L // NT)
    return ncb, NT * ncb


def _cmul_add(ar, ai, xr, xi, br, bi):
    return ar * xr - ai * xi + br, ar * xi + ai * xr + bi


def _pow_weights(w_ref, pwr_ref, pwi_ref, dst, adjoint):
    w = w_ref[...].astype(F32)
    wr, wi = w[:, :SW], w[:, SW:]
    for t in range(NT):
        k = t if adjoint else NT - 1 - t
        if k == 0:
            blk = w
        else:
            pr, pi = pwr_ref[k - 1:k, :], pwi_ref[k - 1:k, :]
            if adjoint:
                blk = jnp.concatenate([pr * wr + pi * wi, pr * wi - pi * wr], axis=1)
            else:
                blk = jnp.concatenate([pr * wr - pi * wi, pr * wi + pi * wr], axis=1)
        dst[t * UW:(t + 1) * UW, :] = blk.astype(dst.dtype)


def _slices(ref, ncb, axis):
    return jnp.concatenate([ref[pl.ds(t, ncb, stride=NT), :] for t in range(NT)], axis=axis)


def _s5_scan_fwd(u3, bb, cm, pwr, pwi, pwrt, pwit, par, pai, dskip):
    L = u3.shape[1]
    ncb, tb = _chunks(L)
    nb = L // tb

    def build(bb_ref, cm_ref, pwr_ref, pwi_ref, pct_ref, pit_ref, bbp_scr, ktp_scr, zc_scr):
        w = bb_ref[...].astype(F32)
        wr, wi = w[:, :SW], w[:, SW:]
        cmv = cm_ref[...].astype(F32)
        ct, cb = cmv[:SW, :], cmv[SW:, :]
        zero = jnp.zeros((UW, UW), MXU)
        for tau in range(NT):
            if tau == 0:
                blk = w
            else:
                pr, pi = pwr_ref[tau - 1:tau, :], pwi_ref[tau - 1:tau, :]
                blk = jnp.concatenate([pr * wr - pi * wi, pr * wi + pi * wr], axis=1)
            blk = blk.astype(MXU)
            bbp_scr[(NT - 1 - tau) * UW:(NT - tau) * UW, :] = blk
            k = jnp.dot(blk, cm_ref[...], preferred_element_type=F32).astype(MXU)
            for j in range(NT - tau):
                ktp_scr[j * UW:(j + 1) * UW, (j + tau) * UW:(j + tau + 1) * UW] = k
            pc, pic = pct_ref[:, tau:tau + 1], pit_ref[:, tau:tau + 1]
            zc_scr[:, tau * UW:(tau + 1) * UW] = jnp.concatenate(
                [pc * ct + pic * cb, pc * cb - pic * ct], axis=0).astype(MXU)
        for j in range(NT):
            for t in range(j):
                ktp_scr[j * UW:(j + 1) * UW, t * UW:(t + 1) * UW] = zero

    def body(u_ref, bb_ref, cm_ref, pwr_ref, pwi_ref, pct_ref, pit_ref, par_ref, pai_ref, d_ref, ypre_ref, cin_ref,
             carry, cin_scr, bbp_scr, ktp_scr, zc_scr):
        @pl.when(pl.program_id(1) == 0)
        def _():
            carry[...] = jnp.zeros_like(carry)
            build(bb_ref, cm_ref, pwr_ref, pwi_ref, pct_ref, pit_ref, bbp_scr, ktp_scr, zc_scr)

        c = carry[...]
        ucat = _slices(u_ref, ncb, 1)
        ub = ucat.astype(MXU)
        e = jnp.dot(ub, bbp_scr[...], preferred_element_type=F32)
        xr, xi = _cscan(e[:, :SW], e[:, SW:], par_ref, pai_ref, False)
        cr, ci = c[:, :SW], c[:, SW:]
        fr, fi = _cmul_add(par_ref[0:ncb, :], pai_ref[0:ncb, :], cr, ci, xr, xi)
        carry[...] = jnp.concatenate([fr[ncb - 1:ncb, :], fi[ncb - 1:ncb, :]], axis=1)
        row = lax.broadcasted_iota(jnp.int32, fr.shape, 0)
        cin_ref[:, :SW] = jnp.where(row >= 1, pltpu.roll(fr, 1, 0), cr)
        cin_ref[:, SW:] = jnp.where(row >= 1, pltpu.roll(fi, 1, 0), ci)
        cin_scr[...] = cin_ref[...].astype(MXU)
        y = (jnp.dot(ub, ktp_scr[...], preferred_element_type=F32)
             + jnp.dot(cin_scr[...], zc_scr[...], preferred_element_type=F32)
             + jnp.tile(d_ref[...], (1, NT)) * ucat)
        for t in range(NT):
            ypre_ref[pl.ds(t, ncb, stride=NT), :] = y[:, t * UW:(t + 1) * UW]

    tab = pl.BlockSpec((CH, SW), lambda j, i: (0, j))
    stp = pl.BlockSpec((NT, SW), lambda j, i: (0, j))
    stt = pl.BlockSpec((SW, NT), lambda j, i: (j, 0))
    vec = lambda w: pl.BlockSpec((1, w), lambda j, i: (0, j))
    return pl.pallas_call(
        body, name="s5_scan_fwd", grid=(NJ, nb),
        in_specs=[pl.BlockSpec((None, tb, UW), lambda j, i: (j, i, 0)),
                  pl.BlockSpec((None, UW, 2 * SW), lambda j, i: (j, 0, 0)),
                  pl.BlockSpec((None, 2 * SW, UW), lambda j, i: (j, 0, 0)),
                  stp, stp, stt, stt, tab, tab, vec(UW)],
        out_specs=[pl.BlockSpec((None, tb, UW), lambda j, i: (j, i, 0)),
                   pl.BlockSpec((None, ncb, 2 * SW), lambda j, i: (j, i, 0))],
        out_shape=[_lane_blocks(L), SDS((NJ, L // NT, 2 * SW), F32)],
        scratch_shapes=[pltpu.VMEM((1, 2 * SW), F32), pltpu.VMEM((ncb, 2 * SW), MXU),
                        pltpu.VMEM((NT * UW, 2 * SW), MXU), pltpu.VMEM((NT * UW, NT * UW), MXU),
                        pltpu.VMEM((2 * SW, NT * UW), MXU)],
        compiler_params=_cp(56, ARB2),
    )(u3, bb, cm, pwr, pwi, pwrt, pwit, par, pai, dskip)


def _s5_gate_bwd(ypre3, p, dx1, wout_e, wglu, bglu, deps=NO_DEPS):
    L = p.shape[0]
    tm = min(256, L)

    def body(y_ref, az_ref, dx1_ref, wo_ref, wg_ref, bg_ref, *rest):
        dyp_ref, daz_ref, yg_ref, dt_ref, ya_ref, gbg_ref = rest[len(deps):]

        @pl.when(pl.program_id(0) == 0)
        def _():
            gbg_ref[...] = jnp.zeros_like(gbg_ref)

        yg, dgelu = _gelu_and_grad(_from_lane_blocks(y_ref))
        sg = jax.nn.sigmoid(_mm(yg, wg_ref[...]) + bg_ref[...])
        act, dact = _silu_and_grad(az_ref[...])
        y2 = yg * sg
        dya = _mm_nt(dx1_ref[...], wo_ref[...])
        daz_ref[...] = (dya * y2 * dact).astype(daz_ref.dtype)
        dy2 = dya * act
        dt = dy2 * yg * sg * (1.0 - sg)
        dyg = dy2 * sg + _mm_nt(dt, wg_ref[...])
        _to_lane_blocks(dyp_ref, dyg * dgelu)
        yg_ref[...] = yg.astype(yg_ref.dtype)
        dt_ref[...] = dt.astype(dt_ref.dtype)
        ya_ref[...] = (y2 * act).astype(ya_ref.dtype)
        gbg_ref[...] += jnp.sum(dt, axis=0, keepdims=True)

    row = pl.BlockSpec((tm, D), lambda i: (i, 0))
    return pl.pallas_call(
        body, name="s5_gate_bwd", grid=(L // tm,),
        in_specs=[_lane_block_spec(tm, lambda i: i), pl.BlockSpec((tm, D), lambda i: (i, 1)), row,
                  pl.BlockSpec((D, D), lambda i: (0, 0)), _full((D, D)), _full((1, D))] + [ANY_SPEC] * len(deps),
        out_specs=[_lane_block_spec(tm, lambda i: i), row, row, row, row, _full((1, D))],
        out_shape=[_lane_blocks(L), SDS((L, D), MXU), SDS((L, D), MXU), SDS((L, D), MXU), SDS((L, D), MXU),
                   SDS((1, D), F32)],
        compiler_params=_cp(40, ("arbitrary",)),
    )(ypre3, p, dx1, wout_e, wglu, bglu, *deps)


def _s5_scan_bwd(u3, dypre3, states, bb, cm, cmt, abr, abi, pwr, pwi, par, pai, pbr, pbi, dskip, deps=NO_DEPS):
    L = u3.shape[1]
    ncb, tb = _chunks(L)
    nb = L // tb
    rev = lambda i: nb - 1 - i

    def body(u_ref, dy_ref, st_ref, bb_ref, cm_ref, cmt_ref, ar_ref, ai_ref, pwr_ref, pwi_ref, par_ref, pai_ref,
             pbr_ref, pbi_ref, d_ref, *rest):
        (du_ref, gd_ref, gcm_ref, gbb_ref, gar_ref, gai_ref,
         lcarry, s_scr, gs_scr, cmp_scr) = rest[len(deps):]
        del cm_ref

        @pl.when(pl.program_id(1) == 0)
        def _():
            _pow_weights(cmt_ref, pwr_ref, pwi_ref, cmp_scr, True)
            lcarry[...] = jnp.zeros_like(lcarry)
            gd_ref[...] = jnp.zeros_like(gd_ref)
            gcm_ref[...] = jnp.zeros_like(gcm_ref)
            gbb_ref[...] = jnp.zeros_like(gbb_ref)
            gar_ref[...] = jnp.zeros_like(gar_ref)
            gai_ref[...] = jnp.zeros_like(gai_ref)

        ar, ai = ar_ref[...], ai_ref[...]
        c = st_ref[...]
        s_scr[0] = c
        sr, si = c[:, :SW], c[:, SW:]
        for t in range(NT):
            bu = _mm(u_ref[pl.ds(t, ncb, stride=NT), :], bb_ref[...])
            sr, si = _cmul_add(ar, ai, sr, si, bu[:, :SW], bu[:, SW:])
            s_scr[t + 1] = jnp.concatenate([sr, si], axis=1)
        dys = [dy_ref[pl.ds(t, ncb, stride=NT), :] for t in range(NT)]
        for t in range(NT):
            gs_scr[t] = _mm(dys[t], cmt_ref[...])
        f = _mm(jnp.concatenate(dys, axis=1), cmp_scr[...])
        xr, xi = _cscan(f[:, :SW], f[:, SW:], par_ref, pai_ref, True)
        lc = lcarry[...]
        lcr, lci = lc[:, :SW], lc[:, SW:]
        hr, hi = _cmul_add(pbr_ref[CH - ncb:CH, :], -pbi_ref[CH - ncb:CH, :], lcr, lci, xr, xi)
        lcarry[...] = jnp.concatenate([hr[0:1, :], hi[0:1, :]], axis=1)
        row = lax.broadcasted_iota(jnp.int32, hr.shape, 0)
        lr_ = jnp.where(row < ncb - 1, pltpu.roll(hr, ncb - 1, 0), lcr)
        li_ = jnp.where(row < ncb - 1, pltpu.roll(hi, ncb - 1, 0), lci)
        gar = jnp.zeros((1, SW), F32)
        gai = jnp.zeros((1, SW), F32)
        for t in reversed(range(NT)):
            gs = gs_scr[t]
            lr_, li_ = _cmul_add(ar, -ai, lr_, li_, gs[:, :SW], gs[:, SW:])
            rows = pl.ds(t, ncb, stride=NT)
            u_t, dy_t = u_ref[rows, :], dy_ref[rows, :]
            lam = jnp.concatenate([lr_, li_], axis=1)
            gbb_ref[...] += _mm_tn(u_t, lam)
            du_ref[rows, :] = _mm_nt(lam, bb_ref[...]) + dy_t * d_ref[...]
            gd_ref[...] += jnp.sum(dy_t * u_t, axis=0, keepdims=True)
            gcm_ref[...] += _mm_tn(s_scr[t + 1], dy_t)
            sp = s_scr[t]
            spr, spi = sp[:, :SW], sp[:, SW:]
            gar += jnp.sum(lr_ * spr + li_ * spi, axis=0, keepdims=True)
            gai += jnp.sum(li_ * spr - lr_ * spi, axis=0, keepdims=True)
        gar_ref[...] += gar
        gai_ref[...] += gai

    tab = pl.BlockSpec((CH, SW), lambda j, i: (0, j))
    stp = pl.BlockSpec((NT, SW), lambda j, i: (0, j))
    colblk = pl.BlockSpec((None, tb, UW), lambda j, i: (j, rev(i), 0))
    vec = lambda w: pl.BlockSpec((1, w), lambda j, i: (0, j))
    return pl.pallas_call(
        body, name="s5_scan_bwd", grid=(NJ, nb),
        in_specs=[colblk, colblk,
                  pl.BlockSpec((None, ncb, 2 * SW), lambda j, i: (j, rev(i), 0)),
                  pl.BlockSpec((None, UW, 2 * SW), lambda j, i: (j, 0, 0)),
                  pl.BlockSpec((None, 2 * SW, UW), lambda j, i: (j, 0, 0)),
                  pl.BlockSpec((None, UW, 2 * SW), lambda j, i: (j, 0, 0)),
                  vec(SW), vec(SW), stp, stp, tab, tab, tab, tab, vec(UW)] + [ANY_SPEC] * len(deps),
        out_specs=[colblk, vec(UW),
                   pl.BlockSpec((None, 2 * SW, UW), lambda j, i: (j, 0, 0)),
                   pl.BlockSpec((None, UW, 2 * SW), lambda j, i: (j, 0, 0)),
                   vec(SW), vec(SW)],
        out_shape=[_lane_blocks(L), SDS((1, D), F32),
                   SDS((NJ, 2 * SW, UW), F32), SDS((NJ, UW, 2 * SW), F32),
                   SDS((1, NSTATE), F32), SDS((1, NSTATE), F32)],
        scratch_shapes=[pltpu.VMEM((1, 2 * SW), F32), pltpu.VMEM((NT + 1, ncb, 2 * SW), F32),
                        pltpu.VMEM((NT, ncb, 2 * SW), F32), pltpu.VMEM((NT * UW, 2 * SW), MXU)],
        compiler_params=_cp(56, ARB2),
    )(u3, dypre3, states, bb, cm, cmt, abr, abi, pwr, pwi, par, pai, pbr, pbi, dskip, *deps)


def _rope_tables(L, inv):
    tm = min(512, L)

    def body(inv_ref, cos_ref, sin_ref):
        pos = (lax.broadcasted_iota(jnp.int32, (tm, DK // 2), 0) + pl.program_id(0) * tm).astype(F32)
        ang = pos * inv_ref[...]
        cos_ref[...] = jnp.cos(ang)
        sin_ref[...] = jnp.sin(ang)

    blk = pl.BlockSpec((tm, DK // 2), lambda i: (i, 0))
    return pl.pallas_call(body, name="rope_tables", grid=(L // tm,), in_specs=[_full((1, DK // 2))],
                          out_specs=[blk, blk], out_shape=[SDS((L, DK // 2), F32)] * 2)(inv)


def _rot(x, cos, sin):
    x1, x2 = x[:, :DK // 2], x[:, DK // 2:]
    return jnp.concatenate([x1 * cos - x2 * sin, x1 * sin + x2 * cos], axis=1)


def _unrot(d, cos, sin):
    d1, d2 = d[:, :DK // 2], d[:, DK // 2:]
    return jnp.concatenate([d1 * cos + d2 * sin, d2 * cos - d1 * sin], axis=1)


def _ret_decays(h):
    lg = LOG_G[h]
    n = lax.broadcasted_iota(jnp.int32, (CH, CH), 0)
    m = lax.broadcasted_iota(jnp.int32, (CH, CH), 1)
    diff = (n - m).astype(F32)
    decay = jnp.where(n >= m, jnp.exp(lg * jnp.maximum(diff, 0.0)), 0.0)
    idx = lax.broadcasted_iota(jnp.int32, (CH, 1), 0).astype(F32)
    xi = jnp.exp(lg * (idx + 1.0))
    zeta = jnp.exp(lg * (CH - 1.0 - idx))
    return decay, xi, zeta, math.exp(lg * CH)


def _ret_tables(dec_scr, vec_scr):
    for h in range(HEADS):
        decay, xi, zeta, _ = _ret_decays(h)
        dec_scr[h] = decay
        vec_scr[h] = jnp.concatenate([jnp.broadcast_to(xi, (CH, 128)), jnp.broadcast_to(zeta, (CH, 128))], axis=1)


def _group_norm(o):
    mu = jnp.mean(o, axis=-1, keepdims=True)
    oc = o - mu
    rstd = lax.rsqrt(jnp.mean(oc * oc, axis=-1, keepdims=True) + EPS)
    return oc * rstd, rstd


def _ret_fwd(p, cos, sin, gain):
    L = p.shape[0]
    nb = L // CH

    def body(q_ref, k_ref, v_ref, bz_ref, cos_ref, sin_ref, g_ref, yb_ref, st_ref, sc_ref, o_ref,
             state, dec_scr, vec_scr):
        @pl.when(pl.program_id(0) == 0)
        def _():
            state[...] = jnp.zeros_like(state)
            _ret_tables(dec_scr, vec_scr)

        cos, sin = cos_ref[...], sin_ref[...]
        act, _ = _silu_and_grad(bz_ref[...])
        for h in range(HEADS):
            hs = slice(h * DK, (h + 1) * DK)
            xi, zeta = vec_scr[h, :, 0:1], vec_scr[h, :, 128:129]
            s_prev = state[h]
            s_prev_b = s_prev.astype(MXU)
            st_ref[h] = s_prev_b
            v = v_ref[:, hs]
            qr = _rot(q_ref[:, hs], cos, sin)
            kr = _rot(k_ref[:, hs], cos, sin) * (DK ** -0.5)
            scores = (_mm_nt(qr, kr) * dec_scr[h]).astype(MXU)
            o = _mm(scores, v) + _mm(qr * xi, s_prev_b)
            sc_ref[:, h * CH:(h + 1) * CH] = scores
            o_ref[:, hs] = o
            state[h] = s_prev * math.exp(LOG_G[h] * CH) + _mm_tn(kr * zeta, v)
            on, _ = _group_norm(o)
            yb_ref[:, hs] = (on * g_ref[:, hs] * act[:, hs]).astype(yb_ref.dtype)

    col = lambda c: pl.BlockSpec((CH, D), lambda i: (i, c))
    rope = pl.BlockSpec((CH, DK // 2), lambda i: (i, 0))
    return pl.pallas_call(
        body, name="ret_fwd", grid=(nb,),
        in_specs=[col(2), col(3), col(4), col(5), rope, rope, _full((1, D))],
        out_specs=[pl.BlockSpec((CH, D), lambda i: (i, 0)),
                   pl.BlockSpec((None, HEADS, DK, DK), lambda i: (i, 0, 0, 0)),
                   pl.BlockSpec((CH, HEADS * CH), lambda i: (i, 0)), pl.BlockSpec((CH, D), lambda i: (i, 0))],
        out_shape=[SDS((L, D), MXU), SDS((nb, HEADS, DK, DK), MXU), SDS((L, HEADS * CH), MXU), SDS((L, D), F32)],
        scratch_shapes=[pltpu.VMEM((HEADS, DK, DK), F32), pltpu.VMEM((HEADS, CH, CH), F32),
                        pltpu.VMEM((HEADS, CH, 256), F32)],
        compiler_params=_cp(40, ("arbitrary",)),
    )(p, p, p, p, cos, sin, gain)


def _ret_bwd(p, cos, sin, gain, states, scores, o, dx1, wout_e, du, daz, deps=NO_DEPS):
    L = p.shape[0]
    nb = L // CH
    rev = lambda i: nb - 1 - i

    def body(q_ref, k_ref, v_ref, bz_ref, cos_ref, sin_ref, g_ref, st_ref, sc_ref, o_ref, dx1_ref, wo_ref, du_ref,
             daz_ref, *rest):
        dp_ref, yb_ref, gg_ref, gstate, dec_scr, vec_scr = rest[len(deps):]

        @pl.when(pl.program_id(0) == 0)
        def _():
            gstate[...] = jnp.zeros_like(gstate)
            gg_ref[...] = jnp.zeros_like(gg_ref)
            _ret_tables(dec_scr, vec_scr)

        cos, sin = cos_ref[...], sin_ref[...]
        act, dact = _silu_and_grad(bz_ref[...])
        dyb = _mm_nt(dx1_ref[...], wo_ref[...])
        dp_ref[:, 0:D] = _from_lane_blocks(du_ref).astype(dp_ref.dtype)
        dp_ref[:, D:2 * D] = daz_ref[...]
        for h in range(HEADS):
            hs = slice(h * DK, (h + 1) * DK)
            col = lambda part: slice((2 + part) * D + h * DK, (2 + part) * D + (h + 1) * DK)
            decay = dec_scr[h]
            xi, zeta = vec_scr[h, :, 0:1], vec_scr[h, :, 128:129]
            v = v_ref[:, hs]
            s_prev_b = st_ref[h]
            qr = _rot(q_ref[:, hs], cos, sin)
            kr = _rot(k_ref[:, hs], cos, sin) * (DK ** -0.5)
            scores = sc_ref[:, h * CH:(h + 1) * CH]
            on, rstd = _group_norm(o_ref[:, hs])
            gain_h = g_ref[:, hs]
            out = on * gain_h
            yb_ref[:, hs] = (out * act[:, hs]).astype(yb_ref.dtype)
            dyb_h = dyb[:, hs]
            dp_ref[:, col(3)] = (dyb_h * out * dact[:, hs]).astype(dp_ref.dtype)
            dout = dyb_h * act[:, hs]
            gg_ref[:, hs] += jnp.sum(dout * on, axis=0, keepdims=True)
            don = dout * gain_h
            do = rstd * (don - jnp.mean(don, axis=-1, keepdims=True)
                         - on * jnp.mean(don * on, axis=-1, keepdims=True))
            gnext = gstate[h]
            gnext_b = gnext.astype(MXU)
            dscores = _mm_nt(do, v) * decay
            dp_ref[:, col(2)] = (_mm_tn(scores, do) + _mm(kr * zeta, gnext_b)).astype(dp_ref.dtype)
            dqr = _mm(dscores, kr) + _mm_nt(do, s_prev_b) * xi
            dkr = _mm_tn(dscores, qr) + _mm_nt(v, gnext_b) * zeta
            gstate[h] = gnext * math.exp(LOG_G[h] * CH) + _mm_tn(qr * xi, do)
            dp_ref[:, col(0)] = _unrot(dqr, cos, sin).astype(dp_ref.dtype)
            dp_ref[:, col(1)] = (_unrot(dkr, cos, sin) * (DK ** -0.5)).astype(dp_ref.dtype)

    col = lambda c: pl.BlockSpec((CH, D), lambda i: (rev(i), c))
    rope = pl.BlockSpec((CH, DK // 2), lambda i: (rev(i), 0))
    outc = col(0)
    act_out = SDS((L, D), MXU)
    return pl.pallas_call(
        body, name="ret_bwd", grid=(nb,),
        in_specs=[col(2), col(3), col(4), col(5), rope, rope, _full((1, D)),
                  pl.BlockSpec((None, HEADS, DK, DK), lambda i: (rev(i), 0, 0, 0)),
                  pl.BlockSpec((CH, HEADS * CH), lambda i: (rev(i), 0)), outc,
                  outc, pl.BlockSpec((D, D), lambda i: (1, 0)), _lane_block_spec(CH, rev), outc]
        + [ANY_SPEC] * len(deps),
        out_specs=[pl.BlockSpec((CH, NIN), lambda i: (rev(i), 0)), outc, _full((1, D))],
        out_shape=[SDS((L, NIN), MXU), act_out, SDS((1, D), F32)],
        scratch_shapes=[pltpu.VMEM((HEADS, DK, DK), F32), pltpu.VMEM((HEADS, CH, CH), F32),
                        pltpu.VMEM((HEADS, CH, 256), F32)],
        compiler_params=_cp(48, ("arbitrary",)),
    )(p, p, p, p, cos, sin, gain, states, scores, o, dx1, wout_e, du, daz, *deps)


def _out_even(x, ypre3, p, yb, wglu, bglu, wout):
    L = x.shape[0]
    tm = min(512, L)

    def body(x_ref, y_ref, az_ref, yb_ref, wg_ref, bg_ref, w_ref, o_ref):
        yg = _gelu(_from_lane_blocks(y_ref))
        t = _mm(yg, wg_ref[...]) + bg_ref[...]
        act, _ = _silu_and_grad(az_ref[...])
        ya = (yg * jax.nn.sigmoid(t) * act).astype(MXU)
        cat = jnp.concatenate([ya, yb_ref[...]], axis=1)
        o_ref[...] = x_ref[...] + jnp.dot(cat, w_ref[...], preferred_element_type=F32)

    row = pl.BlockSpec((tm, D), lambda i: (i, 0))
    return pl.pallas_call(
        body, name="out_even", grid=(L // tm,),
        in_specs=[row, _lane_block_spec(tm, lambda i: i), pl.BlockSpec((tm, D), lambda i: (i, 1)), row,
                  _full((D, D)), _full((1, D)), _full((DI, D))],
        out_specs=row, out_shape=SDS((L, D), F32), compiler_params=_cp(48, ("arbitrary",)),
    )(x, ypre3, p, yb, wglu, bglu, wout)


def _sgu_core(pv, gain, ws_ref, bs_ref):
    pu, pvv, z = pv[:, :DI], pv[:, DI:2 * DI], pv[:, 2 * DI:]
    u, gu = _gelu_and_grad(pu)
    v, gv = _gelu_and_grad(pvv)
    mu = jnp.mean(v, axis=-1, keepdims=True)
    vc = v - mu
    rstd = lax.rsqrt(jnp.mean(vc * vc, axis=-1, keepdims=True) + EPS)
    vhat = vc * rstd
    vn = vhat * gain
    t = lax.broadcasted_iota(jnp.int32, (CH, CH), 0)
    s_ = lax.broadcasted_iota(jnp.int32, (CH, CH), 1)
    mask = t >= s_
    wm = [jnp.where(mask, ws_ref[g], 0.0).astype(MXU) for g in range(SG)]
    s = jnp.concatenate([_mm(wm[g], vn[:, g * SGD:(g + 1) * SGD]) + bs_ref[g] for g in range(SG)], axis=1)
    return gu, gv, z, u, vhat, rstd, vn, mask, wm, s


SGU_CHUNKS = 2


def _sgu_fwd_bwd(p2, x1, gain, wsp, bsp, wout, fnorm, tgt):
    L = p2.shape[0]
    nc = min(SGU_CHUNKS, L // CH)
    rb = nc * CH

    def body(p_ref, x1_ref, g_ref, ws_ref, bs_ref, wo_ref, fn_ref, t_ref,
             dp_ref, y_ref, dx2_ref, gg_ref, gws_ref, gbs_ref, gfn_ref, loss_ref):
        @pl.when(pl.program_id(0) == 0)
        def _():
            gg_ref[...] = jnp.zeros_like(gg_ref)
            gws_ref[...] = jnp.zeros_like(gws_ref)
            gbs_ref[...] = jnp.zeros_like(gbs_ref)
            gfn_ref[...] = jnp.zeros_like(gfn_ref)
            loss_ref[...] = jnp.zeros_like(loss_ref)

        gain = g_ref[...]
        kept = []
        for c in range(nc):
            rows = slice(c * CH, (c + 1) * CH)
            gu, gv, z, u, vhat, rstd, vn, mask, wm, s = _sgu_core(p_ref[rows, :], gain, ws_ref, bs_ref)
            act, dact = _silu_and_grad(z)
            y_ref[rows, :] = (u * s * act).astype(MXU)
            kept.append((gu, gv, u, vhat, rstd, vn, mask, wm, s, act, dact))
        x2 = x1_ref[...] + jnp.dot(y_ref[...], wo_ref[...], preferred_element_type=F32)
        xhat, r = _rms(x2)
        fn = fn_ref[...]
        e = xhat * fn - t_ref[...]
        loss_ref[...] += 0.5 * jnp.sum(jnp.mean(e * e, axis=-1, keepdims=True), axis=0, keepdims=True)
        do = e * (1.0 / D)
        gfn_ref[...] += jnp.sum(do * xhat, axis=0, keepdims=True)
        dxhat = do * fn
        dx2 = r * (dxhat - xhat * jnp.mean(dxhat * xhat, axis=-1, keepdims=True))
        dx2_ref[...] = dx2
        dy_all = _mm_nt(dx2, wo_ref[...])
        for c in range(nc):
            rows = slice(c * CH, (c + 1) * CH)
            gu, gv, u, vhat, rstd, vn, mask, wm, s, act, dact = kept[c]
            dy = dy_all[rows, :]
            du = dy * s * act
            ds = dy * u * act
            dz = dy * u * s * dact
            dvn = []
            for g in range(SG):
                ds_g = ds[:, g * SGD:(g + 1) * SGD]
                vn_g = vn[:, g * SGD:(g + 1) * SGD]
                gbs_ref[g] += jnp.sum(ds_g, axis=1, keepdims=True)
                gws_ref[g] += jnp.where(mask, _mm_nt(ds_g, vn_g), 0.0)
                dvn.append(_mm_tn(wm[g], ds_g))
            dvn = jnp.concatenate(dvn, axis=1)
            gg_ref[...] += jnp.sum(dvn * vhat, axis=0, keepdims=True)
            dvhat = dvn * gain
            dv = rstd * (dvhat - jnp.mean(dvhat, axis=-1, keepdims=True)
                         - vhat * jnp.mean(dvhat * vhat, axis=-1, keepdims=True))
            dp_ref[rows, :] = jnp.concatenate([du * gu, dv * gv, dz], axis=1).astype(dp_ref.dtype)

    row = pl.BlockSpec((rb, D), lambda i: (i, 0))
    wide = pl.BlockSpec((rb, NIN), lambda i: (i, 0))
    return pl.pallas_call(
        body, name="sgu_fwd_bwd", grid=(L // rb,),
        in_specs=[wide, row, _full((1, DI)), _full((SG, CH, CH)), _full((SG, CH, 1)), _full((DI, D)),
                  _full((1, D)), row],
        out_specs=[wide, pl.BlockSpec((rb, DI), lambda i: (i, 0)), row,
                   _full((1, DI)), _full((SG, CH, CH)), _full((SG, CH, 1)), _full((1, D)), _full((1, 128))],
        out_shape=[SDS((L, NIN), MXU), SDS((L, DI), MXU), SDS((L, D), F32), SDS((1, DI), F32),
                   SDS((SG, CH, CH), F32), SDS((SG, CH, 1), F32), SDS((1, D), F32), SDS((1, 128), F32)],
        compiler_params=_cp(60, ("arbitrary",)),
    )(p2, x1, gain, wsp, bsp, wout, fnorm, tgt)


def _my_index():
    return 4 * lax.axis_index("x") + 2 * lax.axis_index("y") + lax.axis_index("c")


def _ordered_sum(land_ref, own, me):
    g = None
    for s in range(NDEV):
        part = jnp.where(me == s, own, land_ref[s].astype(F32))
        g = part if g is None else g + part
    return g


def _adamw_math(w, m, v, g):
    mn = ADAM_B1 * m + (1.0 - ADAM_B1) * g
    vn = ADAM_B2 * v + (1.0 - ADAM_B2) * (g * g)
    mhat = mn / BC1
    vhat = vn / BC2
    return g, -ADAM_LR * (mhat / (jnp.sqrt(vhat) + ADAM_EPS) + ADAM_WD * w), mn, vn


def _adamw(w, m, v, land, own, name):
    R, C = w.shape
    tr = R
    for cand in (256, 128, 64, 32, 16, 8):
        if R % cand == 0 and R > cand:
            tr = cand
            break

    def body(w_ref, m_ref, v_ref, land_ref, own_ref, g_ref, d_ref, mo_ref, vo_ref):
        g = _ordered_sum(land_ref, own_ref[...].astype(F32), _my_index())
        for o, val in zip((g_ref, d_ref, mo_ref, vo_ref), _adamw_math(w_ref[...], m_ref[...], v_ref[...], g)):
            o[...] = val

    blk = pl.BlockSpec((tr, C), lambda i: (i, 0))
    out = SDS((R, C), F32)
    return pl.pallas_call(
        body, name=name, grid=(R // tr,),
        in_specs=[blk, blk, blk, pl.BlockSpec((NDEV, tr, C), lambda i: (0, i, 0)), blk],
        out_specs=[blk, blk, blk, blk], out_shape=[out, out, out, out],
        compiler_params=_cp(40, ("arbitrary",)),
    )(w, m, v, land, own)


def _adamw_many(ws, ms, vs, lands, owns, name):
    k = len(ws)

    def body(*refs):
        ins, outs = refs[:5 * k], refs[5 * k:]
        me = _my_index()
        for i in range(k):
            w_ref, m_ref, v_ref, land_ref, own_ref = (ins[j * k + i] for j in range(5))
            g = _ordered_sum(land_ref, own_ref[...], me)
            for j, val in enumerate(_adamw_math(w_ref[...], m_ref[...], v_ref[...], g)):
                outs[j * k + i][...] = val

    out_shape = [SDS(w.shape, F32) for _ in range(4) for w in ws]
    res = pl.pallas_call(body, name=name, out_shape=out_shape, compiler_params=_cp(60))(*ws, *ms, *vs, *lands, *owns)
    return [res[j * k:(j + 1) * k] for j in range(4)]


MESH = pl.DeviceIdType.MESH
HBM_SPEC = pl.BlockSpec(memory_space=pltpu.HBM)
SEM_SPEC = pl.BlockSpec(memory_space=pltpu.SEMAPHORE)
EFFECT = pltpu.SideEffectType.DATAFLOW_SIDE_EFFECTING


def _me_and_peers():
    x, y, c = lax.axis_index("x"), lax.axis_index("y"), lax.axis_index("c")
    me = 4 * x + 2 * y + c
    peers = []
    for r in range(1, NDEV):
        px, py, pc = x ^ ((r >> 2) & 1), y ^ ((r >> 1) & 1), c ^ (r & 1)
        peers.append(((px, py, pc), 4 * px + 2 * py + pc))
    return me, peers


def _land_shape(a, scatter):
    return (NDEV,) + (a.shape[1:] if scatter else a.shape)


def _remote(src, dst, send_sems, recv_sems, r, k, n, dev):
    i = r * n + k
    return pltpu.make_async_remote_copy(src_ref=src, dst_ref=dst, send_sem=send_sems.at[i], recv_sem=recv_sems.at[i],
                                        device_id=dev, device_id_type=MESH)


def _exchange(arrays, scatter, name):
    n = len(arrays)
    out_shape = [SDS(_land_shape(a, scatter), a.dtype) for a in arrays]

    def body(*refs):
        ins, outs = refs[:n], refs[n:2 * n]
        send_sems, recv_sems, loc_sems = refs[2 * n:]
        me, peers = _me_and_peers()
        local = []
        for k in range(n):
            src = ins[k].at[me] if scatter else ins[k]
            cp = pltpu.make_async_copy(src, outs[k].at[me], loc_sems.at[k])
            cp.start()
            local.append(cp)
        sends = []
        for r, (dev, lin) in enumerate(peers):
            for k in range(n):
                src = ins[k].at[lin] if scatter else ins[k]
                cp = _remote(src, outs[k].at[me], send_sems, recv_sems, r, k, n, dev)
                cp.start()
                sends.append(cp)
        for r, (dev, lin) in enumerate(peers):
            for k in range(n):
                src = ins[k].at[me] if scatter else ins[k]
                _remote(src, outs[k].at[lin], send_sems, recv_sems, r, k, n, dev).wait_recv()
        for cp in sends:
            cp.wait_send()
        for cp in local:
            cp.wait()

    return pl.pallas_call(
        body, name=name, in_specs=[HBM_SPEC] * n, out_specs=[HBM_SPEC] * n, out_shape=out_shape,
        scratch_shapes=[pltpu.SemaphoreType.DMA(((NDEV - 1) * n,)), pltpu.SemaphoreType.DMA(((NDEV - 1) * n,)),
                        pltpu.SemaphoreType.DMA((n,))],
    )(*arrays)


def _exchange_start(arrays, scatter, name):
    n = len(arrays)
    lands = [lax.empty(_land_shape(a, scatter), a.dtype) for a in arrays]

    def body(*refs):
        ins, lnd = refs[:n], refs[n:2 * n]
        send_sems, recv_sems, own_sems = refs[2 * n:2 * n + 3]
        token = refs[-1]
        me, peers = _me_and_peers()
        for r, (dev, lin) in enumerate(peers):
            for k in range(n):
                src = ins[k].at[lin] if scatter else ins[k]
                _remote(src, lnd[k].at[me], send_sems, recv_sems, r, k, n, dev).start()
        if not scatter:
            for k in range(n):
                pltpu.make_async_copy(ins[k], lnd[k].at[me], own_sems.at[k]).start()
        token[...] = jnp.zeros_like(token)

    sem = pltpu.SemaphoreType.DMA(((NDEV - 1) * n,))
    outs = pl.pallas_call(
        body, name=name,
        out_shape=(sem, sem, pltpu.SemaphoreType.DMA((n,)), *[pltpu.HBM(a.shape, a.dtype) for a in arrays],
                   *[pltpu.HBM(l.shape, l.dtype) for l in lands], SDS((8, 128), F32)),
        in_specs=[HBM_SPEC] * (2 * n),
        out_specs=(SEM_SPEC, SEM_SPEC, SEM_SPEC, *[HBM_SPEC] * (2 * n), pl.BlockSpec(memory_space=pltpu.VMEM)),
        input_output_aliases={k: 3 + k for k in range(2 * n)},
        compiler_params=pltpu.CompilerParams(has_side_effects=EFFECT),
    )(*[pltpu.with_memory_space_constraint(a, pltpu.HBM) for a in arrays],
      *[pltpu.with_memory_space_constraint(l, pltpu.HBM) for l in lands])
    return (n, scatter, outs[0], outs[1], outs[2], outs[3:3 + n], outs[3 + n:3 + 2 * n]), outs[-1]


def _exchange_wait(handle, after, name):
    n, scatter, send_sems, recv_sems, own_sems, thru, lands = handle
    after = tuple(after)

    def body(*refs):
        ins, lnd = refs[:n], refs[n:2 * n]
        send_sems, recv_sems, own_sems = refs[2 * n:2 * n + 3]
        me, peers = _me_and_peers()
        for r, (dev, lin) in enumerate(peers):
            for k in range(n):
                src = ins[k].at[lin] if scatter else ins[k]
                cp = _remote(src, lnd[k].at[lin], send_sems, recv_sems, r, k, n, dev)
                cp.wait_send()
                cp.wait_recv()
        if not scatter:
            for k in range(n):
                pltpu.make_async_copy(ins[k], lnd[k].at[me], own_sems.at[k]).wait()

    outs = pl.pallas_call(
        body, name=name,
        out_shape=(*[pltpu.HBM(a.shape, a.dtype) for a in thru], *[pltpu.HBM(l.shape, l.dtype) for l in lands]),
        in_specs=[HBM_SPEC] * (2 * n) + [SEM_SPEC, SEM_SPEC, SEM_SPEC] + [ANY_SPEC] * len(after),
        out_specs=tuple([HBM_SPEC] * (2 * n)),
        input_output_aliases={k: k for k in range(2 * n)},
        compiler_params=pltpu.CompilerParams(has_side_effects=EFFECT),
    )(*thru, *lands, send_sems, recv_sems, own_sems, *after)
    return list(outs[:n]), list(outs[n:])


CHIP_RELATIONS = (2, 4, 6)


def _peer(r):
    x, y, c = lax.axis_index("x"), lax.axis_index("y"), lax.axis_index("c")
    px, py, pc = x ^ ((r >> 2) & 1), y ^ ((r >> 1) & 1), c ^ (r & 1)
    return (px, py, pc), 4 * px + 2 * py + pc


def _copy(src, dst, send_sems, recv_sems, i, dev):
    return pltpu.make_async_remote_copy(src_ref=src, dst_ref=dst, send_sem=send_sems.at[i], recv_sem=recv_sems.at[i],
                                        device_id=dev, device_id_type=MESH)


def _gather2_start(a, name):
    land = lax.empty((NDEV,) + a.shape, a.dtype)

    def body(own, lnd, send_sems, recv_sems, own_sem, own_thru, lnd_thru, token):
        me = _my_index()
        for i, r in enumerate((1,) + CHIP_RELATIONS):
            dev, _ = _peer(r)
            _copy(own, lnd.at[me], send_sems, recv_sems, i, dev).start()
        pltpu.make_async_copy(own, lnd.at[me], own_sem.at[0]).start()
        token[...] = jnp.zeros_like(token)

    sem4 = pltpu.SemaphoreType.DMA((4,))
    outs = pl.pallas_call(
        body, name=name,
        out_shape=(sem4, sem4, pltpu.SemaphoreType.DMA((1,)), pltpu.HBM(a.shape, a.dtype),
                   pltpu.HBM(land.shape, land.dtype), SDS((8, 128), F32)),
        in_specs=[HBM_SPEC, HBM_SPEC],
        out_specs=(SEM_SPEC, SEM_SPEC, SEM_SPEC, HBM_SPEC, HBM_SPEC, pl.BlockSpec(memory_space=pltpu.VMEM)),
        input_output_aliases={0: 3, 1: 4},
        compiler_params=pltpu.CompilerParams(has_side_effects=EFFECT),
    )(pltpu.with_memory_space_constraint(a, pltpu.HBM), pltpu.with_memory_space_constraint(land, pltpu.HBM))
    return outs[:5], outs[5]


def _gather2_forward(handle, after, name):
    send_sems, recv_sems, own_sem, own, land = handle
    after = tuple(after)

    def body(lnd, recv_sems, *rest):
        send2, recv2, lnd_thru = rest[len(after):]
        sib, _ = _peer(1)
        for k, r in enumerate(CHIP_RELATIONS):
            dev, lin = _peer(r)
            _copy(lnd.at[lin], lnd.at[lin], recv_sems, recv_sems, 1 + k, dev).wait_recv()
            _copy(lnd.at[lin], lnd.at[lin], send2, recv2, k, sib).start()

    sem3 = pltpu.SemaphoreType.DMA((3,))
    send2, recv2, land = pl.pallas_call(
        body, name=name,
        out_shape=(sem3, sem3, pltpu.HBM(land.shape, land.dtype)),
        in_specs=[HBM_SPEC, SEM_SPEC] + [ANY_SPEC] * len(after),
        out_specs=(SEM_SPEC, SEM_SPEC, HBM_SPEC),
        input_output_aliases={0: 2},
        compiler_params=pltpu.CompilerParams(has_side_effects=EFFECT),
    )(land, recv_sems, *after)
    return send_sems, recv_sems, own_sem, send2, recv2, own, land


def _gather2_wait(handle, name):
    send_sems, recv_sems, own_sem, send2, recv2, own, land = handle

    def body(own_ref, lnd, send_sems, recv_sems, own_sem, send2, recv2, own_thru, lnd_thru):
        me = _my_index()
        sib, sib_lin = _peer(1)
        for i, r in enumerate((1,) + CHIP_RELATIONS):
            dev, _ = _peer(r)
            _copy(own_ref, lnd.at[me], send_sems, recv_sems, i, dev).wait_send()
        _copy(own_ref, lnd.at[sib_lin], send_sems, recv_sems, 0, sib).wait_recv()
        for k, r in enumerate(CHIP_RELATIONS):
            _, lin = _peer(r)
            _, lin_other = _peer(r ^ 1)
            _copy(lnd.at[lin], lnd.at[lin], send2, recv2, k, sib).wait_send()
            _copy(lnd.at[lin_other], lnd.at[lin_other], send2, recv2, k, sib).wait_recv()
        pltpu.make_async_copy(own_ref, lnd.at[me], own_sem.at[0]).wait()

    outs = pl.pallas_call(
        body, name=name,
        out_shape=(pltpu.HBM(own.shape, own.dtype), pltpu.HBM(land.shape, land.dtype)),
        in_specs=[HBM_SPEC, HBM_SPEC] + [SEM_SPEC] * 5,
        out_specs=(HBM_SPEC, HBM_SPEC),
        input_output_aliases={0: 0, 1: 1},
        compiler_params=pltpu.CompilerParams(has_side_effects=EFFECT),
    )(own, land, send_sems, recv_sems, own_sem, send2, recv2)
    return outs[1]


def _local_step(x, tgt, norm_even, first_weight, lam_re, lam_im, log_dt, b_re, b_im, c_re, c_im, s5_d, bglu,
                ret_gain, wsp, bsp, fnorm, late_weights, emit, start_token=None):
    L = x.shape[0]
    lr3, li3 = lam_re.reshape(G, 1, P), lam_im.reshape(G, 1, P)
    dt3 = log_dt.reshape(G, 1, 1)
    br3, bi3 = jnp.swapaxes(b_re, 1, 2), jnp.swapaxes(b_im, 1, 2)
    abr3, abi3, bbr3, bbi3 = _s5_disc(lr3, li3, dt3, br3, bi3)
    bb = jnp.concatenate([_embed(bbr3), _embed(bbi3)], axis=2).astype(MXU)
    cm = jnp.concatenate([_embed(jnp.swapaxes(c_re, 1, 2)), -_embed(jnp.swapaxes(c_im, 1, 2))], axis=1).astype(MXU)
    abr, abi = abr3.reshape(1, NSTATE), abi3.reshape(1, NSTATE)
    pwr, pwi, _, _ = _s5_tables(abr, abi, NT, "s5_tables_step")
    par, pai, pbr, pbi = _s5_tables(pwr[NT - 1:NT], pwi[NT - 1:NT], CH, "s5_tables_chunk")
    inv = (ROPE_BASE ** (-jnp.arange(DK // 2, dtype=F32) / (DK // 2))).reshape(1, DK // 2)
    cos, sin = _rope_tables(L, inv)
    bsp3 = bsp.reshape(SG, CH, 1)

    def dep(token):
        return NO_DEPS if token is None else (token,)

    win_e = first_weight((cos, pbi, cm))
    p, u3 = _in_proj(x, norm_even, win_e, "in_even", dep(start_token), lane_blocks=True)
    ypre, s5_states = _s5_scan_fwd(u3, bb, cm, pwr, pwi, pwr.T, pwi.T, par, pai, s5_d)
    yb, ret_states, ret_scores, ret_o = _ret_fwd(p, cos, sin, ret_gain)
    wglu, wout_e, norm_odd, win_o, sgu_gain, wout_o = late_weights((ypre, yb))
    x1 = _out_even(x, ypre, p, yb, wglu, bglu, wout_e)
    p2 = _in_proj(x1, norm_odd, win_o, "in_odd")
    dp2, y_o, dx2, g_sgu_gain, g_wsp, g_bsp, g_fnorm, loss = _sgu_fwd_bwd(
        p2, x1, sgu_gain, wsp, bsp3, wout_o, fnorm, tgt)

    g_wout_o = _wgrad_rows([y_o], dx2, "wgrad_out_odd")
    g_win_o = _wgrad_cols(x1, norm_odd, dp2, "wgrad_in_odd")
    tok = emit("odd", dict(w_in_odd=g_win_o, w_out_odd=g_wout_o))
    dx1, g_norm_odd = _in_proj_bwd_x(dp2, x1, norm_odd, win_o, dx2, "in_odd_bwd", dep(tok))
    tok = emit("small_odd", dict(norm_odd=g_norm_odd, sgu_norm_gain=g_sgu_gain, sgu_w_spatial=g_wsp,
                                 sgu_b_spatial=g_bsp.reshape(SG, CH), final_norm=g_fnorm))

    dypre, daz, yg, dt, ya2, g_bglu = _s5_gate_bwd(ypre, p, dx1, wout_e, wglu, bglu, dep(tok))
    g_wglu = _wgrad_rows([yg], dt, "wgrad_glu")
    tok = emit("glu", dict(s5_w_glu=g_wglu))
    du, g_d, g_cm, g_bb, g_ar, g_ai = _s5_scan_bwd(u3, dypre, s5_states, bb, cm, jnp.swapaxes(cm, 1, 2), abr, abi,
                                                   pwr, pwi, par, pai, pbr, pbi, s5_d, dep(tok))
    dbbr3 = _diag_blocks(g_bb[:, :, :SW], HG, P)
    dbbi3 = _diag_blocks(g_bb[:, :, SW:], HG, P)
    g_c_re = jnp.swapaxes(_diag_blocks(g_cm[:, :SW, :], P, HG), 1, 2)
    g_c_im = -jnp.swapaxes(_diag_blocks(g_cm[:, SW:, :], P, HG), 1, 2)
    g_lr3, g_li3, g_dt3, g_br3, g_bi3 = _s5_disc_bwd(
        lr3, li3, dt3, br3, bi3, g_ar.reshape(G, 1, P), g_ai.reshape(G, 1, P), dbbr3, dbbi3)
    tok = emit("small_s5", dict(
        s5_lam_re=g_lr3.reshape(G, P), s5_lam_im=g_li3.reshape(G, P), s5_log_dt=g_dt3.reshape(1, G),
        s5_b_re=g_br3, s5_b_im=g_bi3, s5_c_re=g_c_re, s5_c_im=g_c_im, s5_d=g_d, s5_b_glu=g_bglu))
    dp, yb2, g_ret_gain = _ret_bwd(p, cos, sin, ret_gain, ret_states, ret_scores, ret_o, dx1, wout_e, du, daz,
                                   dep(tok))
    g_win_e = _wgrad_cols(x, norm_even, dp, "wgrad_in_even")
    tok = emit("even_cols", dict(w_in_even=g_win_e))
    g_wout_e = _wgrad_rows([ya2, yb2], dx1, "wgrad_out_even", dep(tok))
    tok = emit("even_rows", dict(w_out_even=g_wout_e))
    dx, g_norm_even = _in_proj_bwd_x(dp, x, norm_even, win_e, dx1, "in_even_bwd", dep(tok))
    emit("last", dict(ret_gn_gain=g_ret_gain, norm_even=g_norm_even, loss=loss))
    return dx


WEIGHTS = ['norm_even', 'w_in_even', 's5_lam_re', 's5_lam_im', 's5_log_dt', 's5_b_re', 's5_b_im', 's5_c_re',
           's5_c_im', 's5_d', 's5_w_glu', 's5_b_glu', 'ret_gn_gain', 'w_out_even', 'norm_odd', 'w_in_odd',
           'sgu_norm_gain', 'sgu_w_spatial', 'sgu_b_spatial', 'w_out_odd', 'final_norm']
BIG = ['w_in_even', 's5_w_glu', 'w_out_even', 'w_in_odd', 'w_out_odd']
SHARDED_SMALL = {'norm_odd': D // NDEV, 'sgu_norm_gain': DI // NDEV}
SCATTER_STAGES = ("odd", "glu", "even_cols", "even_rows")
GATHER_STAGES = ("small_odd", "small_s5")


def _view(n, a):
    if n in ('s5_b_re', 's5_b_im'):
        return jnp.swapaxes(a[0], 1, 2)
    if n == 'final_norm':
        return a.reshape(1, D)
    return a[0] if a.ndim >= 3 else a


def _unview(n, t, shape):
    if n in ('s5_b_re', 's5_b_im'):
        return jnp.swapaxes(t, 1, 2)[None]
    return t.reshape(shape)


def kernel(x, norm_even, w_in_even, s5_lam_re, s5_lam_im, s5_log_dt, s5_b_re, s5_b_im, s5_c_re, s5_c_im, s5_d, s5_w_glu, s5_b_glu, ret_gn_gain, w_out_even, norm_odd, w_in_odd, sgu_norm_gain, sgu_w_spatial, sgu_b_spatial, w_out_odd, final_norm, loss_target, m_norm_even, m_w_in_even, m_s5_lam_re, m_s5_lam_im, m_s5_log_dt, m_s5_b_re, m_s5_b_im, m_s5_c_re, m_s5_c_im, m_s5_d, m_s5_w_glu, m_s5_b_glu, m_ret_gn_gain, m_w_out_even, m_norm_odd, m_w_in_odd, m_sgu_norm_gain, m_sgu_w_spatial, m_sgu_b_spatial, m_w_out_odd, m_final_norm, v_norm_even, v_w_in_even, v_s5_lam_re, v_s5_lam_im, v_s5_log_dt, v_s5_b_re, v_s5_b_im, v_s5_c_re, v_s5_c_im, v_s5_d, v_s5_w_glu, v_s5_b_glu, v_ret_gn_gain, v_w_out_even, v_norm_odd, v_w_in_odd, v_sgu_norm_gain, v_sgu_w_spatial, v_sgu_b_spatial, v_w_out_odd, v_final_norm):
    args = dict(locals())
    w = {n: args[n] for n in WEIGHTS}
    m = {n: args["m_" + n] for n in WEIGHTS}
    v = {n: args["v_" + n] for n in WEIGHTS}
    me = _my_index()

    first_handle, _ = _gather2_start(w['w_in_even'][0].astype(MXU), "gather_first_start")

    def first_weight(after):
        return _gather2_wait(_gather2_forward(first_handle, after, "gather_first_forward"), "gather_first_wait")

    late_own = [w['s5_w_glu'][0].astype(MXU), w['w_out_even'][0].astype(MXU), w['norm_odd'],
                w['w_in_odd'][0].astype(MXU), w['sgu_norm_gain'], w['w_out_odd'][0].astype(MXU)]
    late_handle, start_token = _exchange_start(late_own, False, "gather_late_start")

    def late_weights(after):
        _, (wglu, wout_e, nodd, win_o, sgug, wout_o) = _exchange_wait(late_handle, after, "gather_late_wait")
        return (wglu.reshape(D, D), wout_e.reshape(DI, D), nodd.reshape(1, D), win_o, sgug.reshape(1, DI),
                wout_o.reshape(DI, D))

    pending = {}
    small_last = {}

    def emit(stage, grads):
        if stage == "last":
            small_last.update(grads)
            return None
        names = list(grads)
        handle, token = _exchange_start([grads[n] for n in names], stage in SCATTER_STAGES, stage + "_start")
        pending[stage] = (handle, names)
        return token

    dx = _local_step(
        x[0], loss_target[0], w['norm_even'], first_weight, w['s5_lam_re'][0], w['s5_lam_im'][0], w['s5_log_dt'][0],
        w['s5_b_re'][0], w['s5_b_im'][0], w['s5_c_re'][0], w['s5_c_im'][0], w['s5_d'], w['s5_b_glu'],
        w['ret_gn_gain'], w['sgu_w_spatial'][0], w['sgu_b_spatial'][0], w['final_norm'].reshape(1, D),
        late_weights, emit, start_token)

    out_g, out_d, out_m, out_v = {}, {}, {}, {}
    after = dx
    for stage in SCATTER_STAGES:
        handle, names = pending[stage]
        sent, lands = _exchange_wait(handle, (after,), stage + "_wait")
        for n, land, stack in zip(names, lands, sent):
            shp = w[n].shape
            r, c = shp[1], shp[2]
            own = lax.dynamic_index_in_dim(stack, me, 0, keepdims=False)
            res = _adamw(w[n].reshape(r, c), m[n].reshape(r, c), v[n].reshape(r, c), land, own, "adamw_" + n)
            out_g[n], out_d[n], out_m[n], out_v[n] = (t.reshape(shp) for t in res)
            after = res[0]

    names, owns, lands = [], [], []
    for stage in GATHER_STAGES:
        handle, stage_names = pending[stage]
        sent, got = _exchange_wait(handle, (after,), stage + "_wait")
        names, owns, lands = names + stage_names, owns + sent, lands + got
    last_names = list(small_last)
    last = _exchange([small_last[n] for n in last_names], False, "gather_last")
    loss_parts = last[last_names.index("loss")][:, 0, 0]
    for n, own, land in zip(last_names, (small_last[n] for n in last_names), last):
        if n != "loss":
            names, owns, lands = names + [n], owns + [own], lands + [land]
    for i, n in enumerate(names):
        if n in SHARDED_SMALL:
            width = SHARDED_SMALL[n]
            owns[i] = lax.dynamic_slice_in_dim(owns[i], me * width, width, axis=1)
            lands[i] = lax.dynamic_slice_in_dim(lands[i], me * width, width, axis=2)
    res = _adamw_many([_view(n, w[n]) for n in names], [_view(n, m[n]) for n in names],
                      [_view(n, v[n]) for n in names], lands, owns, "adamw_small")
    for dst, vals in zip((out_g, out_d, out_m, out_v), res):
        for n, t in zip(names, vals):
            dst[n] = _unview(n, t, w[n].shape)

    loss_total = jnp.sum(loss_parts)
    return (loss_total, dx[None], *[out_g[n] for n in WEIGHTS], *[out_d[n] for n in WEIGHTS],
            *[out_m[n] for n in WEIGHTS], *[out_v[n] for n in WEIGHTS])
```

```python
import math

import jax
import jax.numpy as jnp
from jax import lax
from jax.experimental import pallas as pl
from jax.experimental.pallas import tpu as pltpu

F32 = jnp.float32
MXU = jnp.bfloat16
AXES = ("x", "y", "c")
NDEV = 8
D = 1024
NIN = 6144
WIN_BLK = NIN // NDEV
DI = 2048
G, P, HG = 64, 64, 16
GB = 8
NJ = G // GB
SW = GB * P
UW = GB * HG
NSTATE = G * P
HEADS, DK = 4, 256
CH = 128
SG, SGD = 4, 512
EPS = 1e-6
ROPE_BASE = 10000.0
VMEM_CAP_V7X = 64 * 1024 * 1024
LOG_G = [math.log1p(-2.0 ** (-5.0 - h)) for h in range(HEADS)]
GELU_C = math.sqrt(2.0 / math.pi)

ADAM_LR, ADAM_B1, ADAM_B2, ADAM_EPS, ADAM_WD, ADAM_STEP = 0.001, 0.9, 0.999, 1e-08, 0.01, 10
BC1 = 1.0 - ADAM_B1 ** ADAM_STEP
BC2 = 1.0 - ADAM_B2 ** ADAM_STEP

SDS = jax.ShapeDtypeStruct
ARB2 = ("arbitrary", "arbitrary")


def _cp(vmem_mib, sem=None):
    kw = dict(vmem_limit_bytes=min(vmem_mib * 1024 * 1024, VMEM_CAP_V7X - 4 * 1024 * 1024))
    if sem is not None:
        kw["dimension_semantics"] = sem
    return pltpu.CompilerParams(**kw)


def _mm(a, b):
    return jnp.dot(a.astype(MXU), b.astype(MXU), preferred_element_type=F32)


def _mm_nt(a, b):
    return lax.dot_general(a.astype(MXU), b.astype(MXU), (((1,), (1,)), ((), ())), preferred_element_type=F32)


def _mm_tn(a, b):
    return lax.dot_general(a.astype(MXU), b.astype(MXU), (((0,), (0,)), ((), ())), preferred_element_type=F32)


def _gelu(x):
    return _gelu_and_grad(x)[0]


def _gelu_and_grad(x):
    x2 = x * x
    th = jnp.tanh(GELU_C * x * (1.0 + 0.044715 * x2))
    hp = 0.5 * (1.0 + th)
    return x * hp, hp + 0.5 * x * (1.0 - th * th) * GELU_C * (1.0 + 3.0 * 0.044715 * x2)


def _silu_and_grad(x):
    s = jax.nn.sigmoid(x)
    return x * s, s * (1.0 + x * (1.0 - s))


def _full(shape):
    nd = len(shape)
    return pl.BlockSpec(shape, lambda *_: (0,) * nd)


def _rms(xf):
    r = lax.rsqrt(jnp.mean(xf * xf, axis=-1, keepdims=True) + EPS)
    return xf * r, r


ANY_SPEC = pl.BlockSpec(memory_space=pl.ANY)
NO_DEPS = ()


def _load_once(src_hbm, dst_vmem, sem):
    @pl.when(pl.program_id(0) == 0)
    def _():
        cp = pltpu.make_async_copy(src_hbm, dst_vmem, sem)
        cp.start()
        cp.wait()


def _lane_blocks(L):
    return SDS((NJ, L, UW), F32)


def _lane_block_spec(rows, index):
    return pl.BlockSpec((NJ, rows, UW), lambda i: (0, index(i), 0))


def _from_lane_blocks(ref):
    return jnp.concatenate([ref[j] for j in range(NJ)], axis=1)


def _to_lane_blocks(ref, v):
    for j in range(NJ):
        ref[j] = v[:, j * UW:(j + 1) * UW].astype(ref.dtype)


def _in_proj(x, gain, wst, name, deps=NO_DEPS, lane_blocks=False):
    L = x.shape[0]
    tm = min(512, L)

    def body(x_ref, g_ref, w_hbm, *rest):
        outs = rest[len(deps):]
        o_ref, w_scr, sem = outs[0], outs[-2], outs[-1]
        _load_once(w_hbm, w_scr, sem)
        xhat, _ = _rms(x_ref[...])
        h = (xhat * g_ref[...]).astype(MXU)
        for c in range(NDEV):
            o_ref[:, c * WIN_BLK:(c + 1) * WIN_BLK] = jnp.dot(h, w_scr[c], preferred_element_type=F32)
        if lane_blocks:
            _to_lane_blocks(outs[1], o_ref[:, 0:D])

    p_spec, p_shape = pl.BlockSpec((tm, NIN), lambda i: (i, 0)), SDS((L, NIN), F32)
    return pl.pallas_call(
        body, name=name, grid=(L // tm,),
        in_specs=[pl.BlockSpec((tm, D), lambda i: (i, 0)), _full((1, D)), ANY_SPEC] + [ANY_SPEC] * len(deps),
        out_specs=[p_spec, _lane_block_spec(tm, lambda i: i)] if lane_blocks else p_spec,
        out_shape=[p_shape, _lane_blocks(L)] if lane_blocks else p_shape,
        scratch_shapes=[pltpu.VMEM((NDEV, D, WIN_BLK), MXU), pltpu.SemaphoreType.DMA(())],
        compiler_params=_cp(58, ("arbitrary",)),
    )(x, gain, wst, *deps)


def _in_proj_bwd_x(dp, x, gain, wst, dres, name, deps=NO_DEPS):
    L = x.shape[0]
    tm = min(512, L)

    def body(dp_ref, x_ref, g_ref, w_hbm, dres_ref, *rest):
        dx_ref, gg_ref, w_scr, sem = rest[len(deps):]
        _load_once(w_hbm, w_scr, sem)

        @pl.when(pl.program_id(0) == 0)
        def _():
            gg_ref[...] = jnp.zeros_like(gg_ref)

        dh = _mm_nt(dp_ref[:, 0:WIN_BLK], w_scr[0])
        for c in range(1, NDEV):
            dh += _mm_nt(dp_ref[:, c * WIN_BLK:(c + 1) * WIN_BLK], w_scr[c])
        xhat, r = _rms(x_ref[...])
        dxhat = dh * g_ref[...]
        dx_ref[...] = dres_ref[...] + r * (dxhat - xhat * jnp.mean(dxhat * xhat, axis=-1, keepdims=True))
        gg_ref[...] += jnp.sum(dh * xhat, axis=0, keepdims=True)

    row = pl.BlockSpec((tm, D), lambda i: (i, 0))
    return pl.pallas_call(
        body, name=name, grid=(L // tm,),
        in_specs=[pl.BlockSpec((tm, NIN), lambda i: (i, 0)), row, _full((1, D)), ANY_SPEC, row]
        + [ANY_SPEC] * len(deps),
        out_specs=[row, _full((1, D))],
        out_shape=[SDS((L, D), F32), SDS((1, D), F32)],
        scratch_shapes=[pltpu.VMEM((NDEV, D, WIN_BLK), MXU), pltpu.SemaphoreType.DMA(())],
        compiler_params=_cp(56, ("arbitrary",)),
    )(dp, x, gain, wst, dres, *deps)


def _wgrad_cols(x, gain, dp, name, deps=NO_DEPS):
    L = x.shape[0]
    tk = min(1024, L)
    nk = L // tk
    halves = 2
    nh = NDEV // halves

    def body(x_ref, g_ref, dp_ref, *rest):
        o_ref, acc = rest[len(deps):]
        k = pl.program_id(1)

        @pl.when(k == 0)
        def _():
            acc[...] = jnp.zeros_like(acc)

        xhat, _ = _rms(x_ref[...])
        acc[...] += _mm_tn(xhat * g_ref[...], dp_ref[...])

        @pl.when(k == nk - 1)
        def _():
            for c in range(nh):
                o_ref[c] = acc[:, c * WIN_BLK:(c + 1) * WIN_BLK].astype(o_ref.dtype)

    return pl.pallas_call(
        body, name=name, grid=(halves, nk),
        in_specs=[pl.BlockSpec((tk, D), lambda n, k: (k, 0)), _full((1, D)),
                  pl.BlockSpec((tk, nh * WIN_BLK), lambda n, k: (k, n))] + [ANY_SPEC] * len(deps),
        out_specs=pl.BlockSpec((nh, D, WIN_BLK), lambda n, k: (n, 0, 0)),
        out_shape=SDS((NDEV, D, WIN_BLK), MXU),
        scratch_shapes=[pltpu.VMEM((D, nh * WIN_BLK), F32)],
        compiler_params=_cp(56, ARB2),
    )(x, gain, dp, *deps)


def _wgrad_rows(a_parts, b, name, deps=NO_DEPS):
    L, N = b.shape
    na = len(a_parts)
    widths = [a.shape[1] for a in a_parts]
    M = sum(widths)
    tk = min(1024, L)
    nk = L // tk

    def body(*refs):
        a_refs, b_ref = refs[:na], refs[na]
        o_ref, acc = refs[na + 1 + len(deps):]
        k = pl.program_id(0)

        @pl.when(k == 0)
        def _():
            acc[...] = jnp.zeros_like(acc)

        bv = b_ref[...].astype(MXU)
        off = 0
        for a_ref, wd in zip(a_refs, widths):
            acc[off:off + wd, :] += _mm_tn(a_ref[...], bv)
            off += wd

        @pl.when(k == nk - 1)
        def _():
            o_ref[...] = acc[...].astype(o_ref.dtype).reshape(o_ref.shape)

    return pl.pallas_call(
        body, name=name, grid=(nk,),
        in_specs=[pl.BlockSpec((tk, wd), lambda k: (k, 0)) for wd in widths]
        + [pl.BlockSpec((tk, N), lambda k: (k, 0))] + [ANY_SPEC] * len(deps),
        out_specs=_full((NDEV, M // NDEV, N)),
        out_shape=SDS((NDEV, M // NDEV, N), MXU),
        scratch_shapes=[pltpu.VMEM((M, N), F32)],
        compiler_params=_cp(48, ("arbitrary",)),
    )(*a_parts, b, *deps)


def _s5_disc_fn(lr_raw, li, logdt, br, bi):
    lr = jnp.minimum(lr_raw, -1e-4)
    dt = jnp.exp(logdt)
    mag = jnp.exp(lr * dt)
    abr = mag * jnp.cos(li * dt)
    abi = mag * jnp.sin(li * dt)
    den = lr * lr + li * li
    nre = abr - 1.0
    nim = abi
    zr = (nre * lr + nim * li) / den
    zi = (nim * lr - nre * li) / den
    return abr, abi, zr * br - zi * bi, zr * bi + zi * br


def _s5_disc(lr, li, logdt, br, bi):
    def body(lr_ref, li_ref, dt_ref, br_ref, bi_ref, abr_ref, abi_ref, bbr_ref, bbi_ref):
        abr, abi, bbr, bbi = _s5_disc_fn(lr_ref[...], li_ref[...], dt_ref[...], br_ref[...], bi_ref[...])
        abr_ref[...] = abr
        abi_ref[...] = abi
        bbr_ref[...] = bbr
        bbi_ref[...] = bbi

    s1, s3 = SDS((G, 1, P), F32), SDS((G, HG, P), F32)
    return pl.pallas_call(body, name="s5_disc", out_shape=[s1, s1, s3, s3])(lr, li, logdt, br, bi)


def _s5_disc_bwd(lr, li, logdt, br, bi, dabr, dabi, dbbr, dbbi):
    def body(lr_ref, li_ref, dt_ref, br_ref, bi_ref, c0, c1, c2, c3, o0, o1, o2, o3, o4):
        _, vjp = jax.vjp(_s5_disc_fn, lr_ref[...], li_ref[...], dt_ref[...], br_ref[...], bi_ref[...])
        g = vjp((c0[...], c1[...], c2[...], c3[...]))
        for o, v in zip((o0, o1, o2, o3, o4), g):
            o[...] = v

    s1, s3 = SDS((G, 1, P), F32), SDS((G, HG, P), F32)
    return pl.pallas_call(body, name="s5_disc_bwd", out_shape=[s1, s1, SDS((G, 1, 1), F32), s3, s3])(
        lr, li, logdt, br, bi, dabr, dabi, dbbr, dbbi)


def _s5_tables(abr, abi, rows, name):
    def body(ar_ref, ai_ref, pfr, pfi, pbr, pbi):
        pfr[0:1, :] = ar_ref[...]
        pfi[0:1, :] = ai_ref[...]
        pbr[rows - 1:rows, :] = ar_ref[...]
        pbi[rows - 1:rows, :] = ai_ref[...]
        n = 1
        while n < rows:
            er, ei = pfr[n - 1:n, :], pfi[n - 1:n, :]
            xr, xi = pfr[0:n, :], pfi[0:n, :]
            pfr[n:2 * n, :] = er * xr - ei * xi
            pfi[n:2 * n, :] = er * xi + ei * xr
            yr, yi = pbr[rows - n:rows, :], pbi[rows - n:rows, :]
            pbr[rows - 2 * n:rows - n, :] = er * yr - ei * yi
            pbi[rows - 2 * n:rows - n, :] = er * yi + ei * yr
            n *= 2

    s = SDS((rows, NSTATE), F32)
    return pl.pallas_call(body, name=name, out_shape=[s, s, s, s], compiler_params=_cp(40))(abr, abi)


def _cscan(br, bi, pr_ref, pi_ref, reverse):
    T = br.shape[0]
    sign = -1.0 if reverse else 1.0
    row = lax.broadcasted_iota(jnp.int32, br.shape, 0)
    k = 1
    while k < T:
        akr = pr_ref[k - 1:k, :]
        aki = sign * pi_ref[k - 1:k, :]

        def shift(v):
            if k % 8 == 0:
                z = jnp.zeros((k, v.shape[1]), v.dtype)
                return jnp.concatenate([v[k:], z], 0) if reverse else jnp.concatenate([z, v[:T - k]], 0)
            if reverse:
                return jnp.where(row < T - k, pltpu.roll(v, T - k, 0), 0.0)
            return jnp.where(row >= k, pltpu.roll(v, k, 0), 0.0)

        sr, si = shift(br), shift(bi)
        br, bi = br + akr * sr - aki * si, bi + akr * si + aki * sr
        k *= 2
    return br, bi


def _embed(t):
    a, b = t.shape[1], t.shape[2]
    return jnp.einsum("jgab,gh->jgahb", t.reshape(NJ, GB, a, b), jnp.eye(GB, dtype=t.dtype)).reshape(NJ, GB * a, GB * b)


def _diag_blocks(t, a, b):
    return jnp.einsum("jgahb,gh->jgab", t.reshape(NJ, GB, a, GB, b), jnp.eye(GB, dtype=t.dtype)).reshape(G, a, b)


NT = 16


def _chunks(L):
    ncb = min(CH, L // NT)
    return ncb, NT * ncb


def _cmul_add(ar, ai, xr, xi, br, bi):
    return ar * xr - ai * xi + br, ar * xi + ai * xr + bi


def _pow_weights(w_ref, pwr_ref, pwi_ref, dst, adjoint):
    w = w_ref[...].astype(F32)
    wr, wi = w[:, :SW], w[:, SW:]
    for t in range(NT):
        k = t if adjoint else NT - 1 - t
        if k == 0:
            blk = w
        else:
            pr, pi = pwr_ref[k - 1:k, :], pwi_ref[k - 1:k, :]
            if adjoint:
                blk = jnp.concatenate([pr * wr + pi * wi, pr * wi - pi * wr], axis=1)
            else:
                blk = jnp.concatenate([pr * wr - pi * wi, pr * wi + pi * wr], axis=1)
        dst[t * UW:(t + 1) * UW, :] = blk.astype(dst.dtype)


def _slices(ref, ncb, axis):
    return jnp.concatenate([ref[pl.ds(t, ncb, stride=NT), :] for t in range(NT)], axis=axis)


def _s5_scan_fwd(u3, bb, cm, pwr, pwi, pwrt, pwit, par, pai, dskip):
    L = u3.shape[1]
    ncb, tb = _chunks(L)
    nb = L // tb

    def build(bb_ref, cm_ref, pwr_ref, pwi_ref, pct_ref, pit_ref, bbp_scr, ktp_scr, zc_scr):
        w = bb_ref[...].astype(F32)
        wr, wi = w[:, :SW], w[:, SW:]
        cmv = cm_ref[...].astype(F32)
        ct, cb = cmv[:SW, :], cmv[SW:, :]
        zero = jnp.zeros((UW, UW), MXU)
        for tau in range(NT):
            if tau == 0:
                blk = w
            else:
                pr, pi = pwr_ref[tau - 1:tau, :], pwi_ref[tau - 1:tau, :]
                blk = jnp.concatenate([pr * wr - pi * wi, pr * wi + pi * wr], axis=1)
            blk = blk.astype(MXU)
            bbp_scr[(NT - 1 - tau) * UW:(NT - tau) * UW, :] = blk
            k = jnp.dot(blk, cm_ref[...], preferred_element_type=F32).astype(MXU)
            for j in range(NT - tau):
                ktp_scr[j * UW:(j + 1) * UW, (j + tau) * UW:(j + tau + 1) * UW] = k
            pc, pic = pct_ref[:, tau:tau + 1], pit_ref[:, tau:tau + 1]
            zc_scr[:, tau * UW:(tau + 1) * UW] = jnp.concatenate(
                [pc * ct + pic * cb, pc * cb - pic * ct], axis=0).astype(MXU)
        for j in range(NT):
            for t in range(j):
                ktp_scr[j * UW:(j + 1) * UW, t * UW:(t + 1) * UW] = zero

    def body(u_ref, bb_ref, cm_ref, pwr_ref, pwi_ref, pct_ref, pit_ref, par_ref, pai_ref, d_ref, ypre_ref, cin_ref,
             carry, cin_scr, bbp_scr, ktp_scr, zc_scr):
        @pl.when(pl.program_id(1) == 0)
        def _():
            carry[...] = jnp.zeros_like(carry)
            build(bb_ref, cm_ref, pwr_ref, pwi_ref, pct_ref, pit_ref, bbp_scr, ktp_scr, zc_scr)

        c = carry[...]
        ucat = _slices(u_ref, ncb, 1)
        ub = ucat.astype(MXU)
        e = jnp.dot(ub, bbp_scr[...], preferred_element_type=F32)
        xr, xi = _cscan(e[:, :SW], e[:, SW:], par_ref, pai_ref, False)
        cr, ci = c[:, :SW], c[:, SW:]
        fr, fi = _cmul_add(par_ref[0:ncb, :], pai_ref[0:ncb, :], cr, ci, xr, xi)
        carry[...] = jnp.concatenate([fr[ncb - 1:ncb, :], fi[ncb - 1:ncb, :]], axis=1)
        row = lax.broadcasted_iota(jnp.int32, fr.shape, 0)
        cin_ref[:, :SW] = jnp.where(row >= 1, pltpu.roll(fr, 1, 0), cr)
        cin_ref[:, SW:] = jnp.where(row >= 1, pltpu.roll(fi, 1, 0), ci)
        cin_scr[...] = cin_ref[...].astype(MXU)
        y = (jnp.dot(ub, ktp_scr[...], preferred_element_type=F32)
             + jnp.dot(cin_scr[...], zc_scr[...], preferred_element_type=F32)
             + jnp.tile(d_ref[...], (1, NT)) * ucat)
        for t in range(NT):
            ypre_ref[pl.ds(t, ncb, stride=NT), :] = y[:, t * UW:(t + 1) * UW]

    tab = pl.BlockSpec((CH, SW), lambda j, i: (0, j))
    stp = pl.BlockSpec((NT, SW), lambda j, i: (0, j))
    stt = pl.BlockSpec((SW, NT), lambda j, i: (j, 0))
    vec = lambda w: pl.BlockSpec((1, w), lambda j, i: (0, j))
    return pl.pallas_call(
        body, name="s5_scan_fwd", grid=(NJ, nb),
        in_specs=[pl.BlockSpec((None, tb, UW), lambda j, i: (j, i, 0)),
                  pl.BlockSpec((None, UW, 2 * SW), lambda j, i: (j, 0, 0)),
                  pl.BlockSpec((None, 2 * SW, UW), lambda j, i: (j, 0, 0)),
                  stp, stp, stt, stt, tab, tab, vec(UW)],
        out_specs=[pl.BlockSpec((None, tb, UW), lambda j, i: (j, i, 0)),
                   pl.BlockSpec((None, ncb, 2 * SW), lambda j, i: (j, i, 0))],
        out_shape=[_lane_blocks(L), SDS((NJ, L // NT, 2 * SW), F32)],
        scratch_shapes=[pltpu.VMEM((1, 2 * SW), F32), pltpu.VMEM((ncb, 2 * SW), MXU),
                        pltpu.VMEM((NT * UW, 2 * SW), MXU), pltpu.VMEM((NT * UW, NT * UW), MXU),
                        pltpu.VMEM((2 * SW, NT * UW), MXU)],
        compiler_params=_cp(56, ARB2),
    )(u3, bb, cm, pwr, pwi, pwrt, pwit, par, pai, dskip)


def _s5_gate_bwd(ypre3, p, dx1, wout_e, wglu, bglu, deps=NO_DEPS):
    L = p.shape[0]
    tm = min(256, L)

    def body(y_ref, az_ref, dx1_ref, wo_ref, wg_ref, bg_ref, *rest):
        dyp_ref, daz_ref, yg_ref, dt_ref, ya_ref, gbg_ref = rest[len(deps):]

        @pl.when(pl.program_id(0) == 0)
        def _():
            gbg_ref[...] = jnp.zeros_like(gbg_ref)

        yg, dgelu = _gelu_and_grad(_from_lane_blocks(y_ref))
        sg = jax.nn.sigmoid(_mm(yg, wg_ref[...]) + bg_ref[...])
        act, dact = _silu_and_grad(az_ref[...])
        y2 = yg * sg
        dya = _mm_nt(dx1_ref[...], wo_ref[...])
        daz_ref[...] = (dya * y2 * dact).astype(daz_ref.dtype)
        dy2 = dya * act
        dt = dy2 * yg * sg * (1.0 - sg)
        dyg = dy2 * sg + _mm_nt(dt, wg_ref[...])
        _to_lane_blocks(dyp_ref, dyg * dgelu)
        yg_ref[...] = yg.astype(yg_ref.dtype)
        dt_ref[...] = dt.astype(dt_ref.dtype)
        ya_ref[...] = (y2 * act).astype(ya_ref.dtype)
        gbg_ref[...] += jnp.sum(dt, axis=0, keepdims=True)

    row = pl.BlockSpec((tm, D), lambda i: (i, 0))
    return pl.pallas_call(
        body, name="s5_gate_bwd", grid=(L // tm,),
        in_specs=[_lane_block_spec(tm, lambda i: i), pl.BlockSpec((tm, D), lambda i: (i, 1)), row,
                  pl.BlockSpec((D, D), lambda i: (0, 0)), _full((D, D)), _full((1, D))] + [ANY_SPEC] * len(deps),
        out_specs=[_lane_block_spec(tm, lambda i: i), row, row, row, row, _full((1, D))],
        out_shape=[_lane_blocks(L), SDS((L, D), MXU), SDS((L, D), MXU), SDS((L, D), MXU), SDS((L, D), MXU),
                   SDS((1, D), F32)],
        compiler_params=_cp(40, ("arbitrary",)),
    )(ypre3, p, dx1, wout_e, wglu, bglu, *deps)


def _s5_scan_bwd(u3, dypre3, states, bb, cm, cmt, abr, abi, pwr, pwi, par, pai, pbr, pbi, dskip, deps=NO_DEPS):
    L = u3.shape[1]
    ncb, tb = _chunks(L)
    nb = L // tb
    rev = lambda i: nb - 1 - i

    def body(u_ref, dy_ref, st_ref, bb_ref, cm_ref, cmt_ref, ar_ref, ai_ref, pwr_ref, pwi_ref, par_ref, pai_ref,
             pbr_ref, pbi_ref, d_ref, *rest):
        (du_ref, gd_ref, gcm_ref, gbb_ref, gar_ref, gai_ref,
         lcarry, s_scr, gs_scr, cmp_scr) = rest[len(deps):]
        del cm_ref

        @pl.when(pl.program_id(1) == 0)
        def _():
            _pow_weights(cmt_ref, pwr_ref, pwi_ref, cmp_scr, True)
            lcarry[...] = jnp.zeros_like(lcarry)
            gd_ref[...] = jnp.zeros_like(gd_ref)
            gcm_ref[...] = jnp.zeros_like(gcm_ref)
            gbb_ref[...] = jnp.zeros_like(gbb_ref)
            gar_ref[...] = jnp.zeros_like(gar_ref)
            gai_ref[...] = jnp.zeros_like(gai_ref)

        ar, ai = ar_ref[...], ai_ref[...]
        c = st_ref[...]
        s_scr[0] = c
        sr, si = c[:, :SW], c[:, SW:]
        for t in range(NT):
            bu = _mm(u_ref[pl.ds(t, ncb, stride=NT), :], bb_ref[...])
            sr, si = _cmul_add(ar, ai, sr, si, bu[:, :SW], bu[:, SW:])
            s_scr[t + 1] = jnp.concatenate([sr, si], axis=1)
        dys = [dy_ref[pl.ds(t, ncb, stride=NT), :] for t in range(NT)]
        for t in range(NT):
            gs_scr[t] = _mm(dys[t], cmt_ref[...])
        f = _mm(jnp.concatenate(dys, axis=1), cmp_scr[...])
        xr, xi = _cscan(f[:, :SW], f[:, SW:], par_ref, pai_ref, True)
        lc = lcarry[...]
        lcr, lci = lc[:, :SW], lc[:, SW:]
        hr, hi = _cmul_add(pbr_ref[CH - ncb:CH, :], -pbi_ref[CH - ncb:CH, :], lcr, lci, xr, xi)
        lcarry[...] = jnp.concatenate([hr[0:1, :], hi[0:1, :]], axis=1)
        row = lax.broadcasted_iota(jnp.int32, hr.shape, 0)
        lr_ = jnp.where(row < ncb - 1, pltpu.roll(hr, ncb - 1, 0), lcr)
        li_ = jnp.where(row < ncb - 1, pltpu.roll(hi, ncb - 1, 0), lci)
        gar = jnp.zeros((1, SW), F32)
        gai = jnp.zeros((1, SW), F32)
        for t in reversed(range(NT)):
            gs = gs_scr[t]
            lr_, li_ = _cmul_add(ar, -ai, lr_, li_, gs[:, :SW], gs[:, SW:])
            rows = pl.ds(t, ncb, stride=NT)
            u_t, dy_t = u_ref[rows, :], dy_ref[rows, :]
            lam = jnp.concatenate([lr_, li_], axis=1)
            gbb_ref[...] += _mm_tn(u_t, lam)
            du_ref[rows, :] = _mm_nt(lam, bb_ref[...]) + dy_t * d_ref[...]
            gd_ref[...] += jnp.sum(dy_t * u_t, axis=0, keepdims=True)
            gcm_ref[...] += _mm_tn(s_scr[t + 1], dy_t)
            sp = s_scr[t]
            spr, spi = sp[:, :SW], sp[:, SW:]
            gar += jnp.sum(lr_ * spr + li_ * spi, axis=0, keepdims=True)
            gai += jnp.sum(li_ * spr - lr_ * spi, axis=0, keepdims=True)
        gar_ref[...] += gar
        gai_ref[...] += gai

    tab = pl.BlockSpec((CH, SW), lambda j, i: (0, j))
    stp = pl.BlockSpec((NT, SW), lambda j, i: (0, j))
    colblk = pl.BlockSpec((None, tb, UW), lambda j, i: (j, rev(i), 0))
    vec = lambda w: pl.BlockSpec((1, w), lambda j, i: (0, j))
    return pl.pallas_call(
        body, name="s5_scan_bwd", grid=(NJ, nb),
        in_specs=[colblk, colblk,
                  pl.BlockSpec((None, ncb, 2 * SW), lambda j, i: (j, rev(i), 0)),
                  pl.BlockSpec((None, UW, 2 * SW), lambda j, i: (j, 0, 0)),
                  pl.BlockSpec((None, 2 * SW, UW), lambda j, i: (j, 0, 0)),
                  pl.BlockSpec((None, UW, 2 * SW), lambda j, i: (j, 0, 0)),
                  vec(SW), vec(SW), stp, stp, tab, tab, tab, tab, vec(UW)] + [ANY_SPEC] * len(deps),
        out_specs=[colblk, vec(UW),
                   pl.BlockSpec((None, 2 * SW, UW), lambda j, i: (j, 0, 0)),
                   pl.BlockSpec((None, UW, 2 * SW), lambda j, i: (j, 0, 0)),
                   vec(SW), vec(SW)],
        out_shape=[_lane_blocks(L), SDS((1, D), F32),
                   SDS((NJ, 2 * SW, UW), F32), SDS((NJ, UW, 2 * SW), F32),
                   SDS((1, NSTATE), F32), SDS((1, NSTATE), F32)],
        scratch_shapes=[pltpu.VMEM((1, 2 * SW), F32), pltpu.VMEM((NT + 1, ncb, 2 * SW), F32),
                        pltpu.VMEM((NT, ncb, 2 * SW), F32), pltpu.VMEM((NT * UW, 2 * SW), MXU)],
        compiler_params=_cp(56, ARB2),
    )(u3, dypre3, states, bb, cm, cmt, abr, abi, pwr, pwi, par, pai, pbr, pbi, dskip, *deps)


def _rope_tables(L, inv):
    tm = min(512, L)

    def body(inv_ref, cos_ref, sin_ref):
        pos = (lax.broadcasted_iota(jnp.int32, (tm, DK // 2), 0) + pl.program_id(0) * tm).astype(F32)
        ang = pos * inv_ref[...]
        cos_ref[...] = jnp.cos(ang)
        sin_ref[...] = jnp.sin(ang)

    blk = pl.BlockSpec((tm, DK // 2), lambda i: (i, 0))
    return pl.pallas_call(body, name="rope_tables", grid=(L // tm,), in_specs=[_full((1, DK // 2))],
                          out_specs=[blk, blk], out_shape=[SDS((L, DK // 2), F32)] * 2)(inv)


def _rot(x, cos, sin):
    x1, x2 = x[:, :DK // 2], x[:, DK // 2:]
    return jnp.concatenate([x1 * cos - x2 * sin, x1 * sin + x2 * cos], axis=1)


def _unrot(d, cos, sin):
    d1, d2 = d[:, :DK // 2], d[:, DK // 2:]
    return jnp.concatenate([d1 * cos + d2 * sin, d2 * cos - d1 * sin], axis=1)


def _ret_decays(h):
    lg = LOG_G[h]
    n = lax.broadcasted_iota(jnp.int32, (CH, CH), 0)
    m = lax.broadcasted_iota(jnp.int32, (CH, CH), 1)
    diff = (n - m).astype(F32)
    decay = jnp.where(n >= m, jnp.exp(lg * jnp.maximum(diff, 0.0)), 0.0)
    idx = lax.broadcasted_iota(jnp.int32, (CH, 1), 0).astype(F32)
    xi = jnp.exp(lg * (idx + 1.0))
    zeta = jnp.exp(lg * (CH - 1.0 - idx))
    return decay, xi, zeta, math.exp(lg * CH)


def _ret_tables(dec_scr, vec_scr):
    for h in range(HEADS):
        decay, xi, zeta, _ = _ret_decays(h)
        dec_scr[h] = decay
        vec_scr[h] = jnp.concatenate([jnp.broadcast_to(xi, (CH, 128)), jnp.broadcast_to(zeta, (CH, 128))], axis=1)


def _group_norm(o):
    mu = jnp.mean(o, axis=-1, keepdims=True)
    oc = o - mu
    rstd = lax.rsqrt(jnp.mean(oc * oc, axis=-1, keepdims=True) + EPS)
    return oc * rstd, rstd


def _ret_fwd(p, cos, sin, gain):
    L = p.shape[0]
    nb = L // CH

    def body(q_ref, k_ref, v_ref, bz_ref, cos_ref, sin_ref, g_ref, yb_ref, st_ref, sc_ref, o_ref,
             state, dec_scr, vec_scr):
        @pl.when(pl.program_id(0) == 0)
        def _():
            state[...] = jnp.zeros_like(state)
            _ret_tables(dec_scr, vec_scr)

        cos, sin = cos_ref[...], sin_ref[...]
        act, _ = _silu_and_grad(bz_ref[...])
        for h in range(HEADS):
            hs = slice(h * DK, (h + 1) * DK)
            xi, zeta = vec_scr[h, :, 0:1], vec_scr[h, :, 128:129]
            s_prev = state[h]
            s_prev_b = s_prev.astype(MXU)
            st_ref[h] = s_prev_b
            v = v_ref[:, hs]
            qr = _rot(q_ref[:, hs], cos, sin)
            kr = _rot(k_ref[:, hs], cos, sin) * (DK ** -0.5)
            scores = (_mm_nt(qr, kr) * dec_scr[h]).astype(MXU)
            o = _mm(scores, v) + _mm(qr * xi, s_prev_b)
            sc_ref[:, h * CH:(h + 1) * CH] = scores
            o_ref[:, hs] = o
            state[h] = s_prev * math.exp(LOG_G[h] * CH) + _mm_tn(kr * zeta, v)
            on, _ = _group_norm(o)
            yb_ref[:, hs] = (on * g_ref[:, hs] * act[:, hs]).astype(yb_ref.dtype)

    col = lambda c: pl.BlockSpec((CH, D), lambda i: (i, c))
    rope = pl.BlockSpec((CH, DK // 2), lambda i: (i, 0))
    return pl.pallas_call(
        body, name="ret_fwd", grid=(nb,),
        in_specs=[col(2), col(3), col(4), col(5), rope, rope, _full((1, D))],
        out_specs=[pl.BlockSpec((CH, D), lambda i: (i, 0)),
                   pl.BlockSpec((None, HEADS, DK, DK), lambda i: (i, 0, 0, 0)),
                   pl.BlockSpec((CH, HEADS * CH), lambda i: (i, 0)), pl.BlockSpec((CH, D), lambda i: (i, 0))],
        out_shape=[SDS((L, D), MXU), SDS((nb, HEADS, DK, DK), MXU), SDS((L, HEADS * CH), MXU), SDS((L, D), F32)],
        scratch_shapes=[pltpu.VMEM((HEADS, DK, DK), F32), pltpu.VMEM((HEADS, CH, CH), F32),
                        pltpu.VMEM((HEADS, CH, 256), F32)],
        compiler_params=_cp(40, ("arbitrary",)),
    )(p, p, p, p, cos, sin, gain)


RET_BWD_CHUNKS = 2


def _ret_bwd(p, cos, sin, gain, states, scores, o, dx1, wout_e, du, daz, deps=NO_DEPS):
    L = p.shape[0]
    nc = min(RET_BWD_CHUNKS, L // CH)
    rb = nc * CH
    nb = L // rb
    rev = lambda i: nb - 1 - i

    def body(q_ref, k_ref, v_ref, bz_ref, cos_ref, sin_ref, g_ref, st_ref, sc_ref, o_ref, dx1_ref, wo_ref, du_ref,
             daz_ref, *rest):
        dp_ref, yb_ref, gg_ref, gstate, dec_scr, vec_scr = rest[len(deps):]

        @pl.when(pl.program_id(0) == 0)
        def _():
            gstate[...] = jnp.zeros_like(gstate)
            gg_ref[...] = jnp.zeros_like(gg_ref)
            _ret_tables(dec_scr, vec_scr)

        act_all, dact_all = _silu_and_grad(bz_ref[...])
        dyb_all = _mm_nt(dx1_ref[...], wo_ref[...])
        dp_ref[:, 0:D] = _from_lane_blocks(du_ref).astype(dp_ref.dtype)
        dp_ref[:, D:2 * D] = daz_ref[...]
        for c in reversed(range(nc)):
            rows = slice(c * CH, (c + 1) * CH)
            cos, sin = cos_ref[rows, :], sin_ref[rows, :]
            act, dact, dyb = act_all[rows, :], dact_all[rows, :], dyb_all[rows, :]
            for h in range(HEADS):
                hs = slice(h * DK, (h + 1) * DK)
                col = lambda part: slice((2 + part) * D + h * DK, (2 + part) * D + (h + 1) * DK)
                decay = dec_scr[h]
                xi, zeta = vec_scr[h, :, 0:1], vec_scr[h, :, 128:129]
                v = v_ref[rows, hs]
                s_prev_b = st_ref[c, h]
                qr = _rot(q_ref[rows, hs], cos, sin)
                kr = _rot(k_ref[rows, hs], cos, sin) * (DK ** -0.5)
                scores = sc_ref[rows, h * CH:(h + 1) * CH]
                on, rstd = _group_norm(o_ref[rows, hs])
                gain_h = g_ref[:, hs]
                out = on * gain_h
                yb_ref[rows, hs] = (out * act[:, hs]).astype(yb_ref.dtype)
                dyb_h = dyb[:, hs]
                dp_ref[rows, col(3)] = (dyb_h * out * dact[:, hs]).astype(dp_ref.dtype)
                dout = dyb_h * act[:, hs]
                gg_ref[:, hs] += jnp.sum(dout * on, axis=0, keepdims=True)
                don = dout * gain_h
                do = rstd * (don - jnp.mean(don, axis=-1, keepdims=True)
                             - on * jnp.mean(don * on, axis=-1, keepdims=True))
                gnext = gstate[h]
                gnext_b = gnext.astype(MXU)
                dscores = _mm_nt(do, v) * decay
                dp_ref[rows, col(2)] = (_mm_tn(scores, do) + _mm(kr * zeta, gnext_b)).astype(dp_ref.dtype)
                dqr = _mm(dscores, kr) + _mm_nt(do, s_prev_b) * xi
                dkr = _mm_tn(dscores, qr) + _mm_nt(v, gnext_b) * zeta
                gstate[h] = gnext * math.exp(LOG_G[h] * CH) + _mm_tn(qr * xi, do)
                dp_ref[rows, col(0)] = _unrot(dqr, cos, sin).astype(dp_ref.dtype)
                dp_ref[rows, col(1)] = (_unrot(dkr, cos, sin) * (DK ** -0.5)).astype(dp_ref.dtype)

    col = lambda c: pl.BlockSpec((rb, D), lambda i: (rev(i), c))
    rope = pl.BlockSpec((rb, DK // 2), lambda i: (rev(i), 0))
    outc = col(0)
    act_out = SDS((L, D), MXU)
    return pl.pallas_call(
        body, name="ret_bwd", grid=(nb,),
        in_specs=[col(2), col(3), col(4), col(5), rope, rope, _full((1, D)),
                  pl.BlockSpec((nc, HEADS, DK, DK), lambda i: (rev(i), 0, 0, 0)),
                  pl.BlockSpec((rb, HEADS * CH), lambda i: (rev(i), 0)), outc,
                  outc, pl.BlockSpec((D, D), lambda i: (1, 0)), _lane_block_spec(rb, rev), outc]
        + [ANY_SPEC] * len(deps),
        out_specs=[pl.BlockSpec((rb, NIN), lambda i: (rev(i), 0)), outc, _full((1, D))],
        out_shape=[SDS((L, NIN), MXU), act_out, SDS((1, D), F32)],
        scratch_shapes=[pltpu.VMEM((HEADS, DK, DK), F32), pltpu.VMEM((HEADS, CH, CH), F32),
                        pltpu.VMEM((HEADS, CH, 256), F32)],
        compiler_params=_cp(56, ("arbitrary",)),
    )(p, p, p, p, cos, sin, gain, states, scores, o, dx1, wout_e, du, daz, *deps)


def _out_even(x, ypre3, p, yb, wglu, bglu, wout):
    L = x.shape[0]
    tm = min(512, L)

    def body(x_ref, y_ref, az_ref, yb_ref, wg_ref, bg_ref, w_ref, o_ref):
        yg = _gelu(_from_lane_blocks(y_ref))
        t = _mm(yg, wg_ref[...]) + bg_ref[...]
        act, _ = _silu_and_grad(az_ref[...])
        ya = (yg * jax.nn.sigmoid(t) * act).astype(MXU)
        cat = jnp.concatenate([ya, yb_ref[...]], axis=1)
        o_ref[...] = x_ref[...] + jnp.dot(cat, w_ref[...], preferred_element_type=F32)

    row = pl.BlockSpec((tm, D), lambda i: (i, 0))
    return pl.pallas_call(
        body, name="out_even", grid=(L // tm,),
        in_specs=[row, _lane_block_spec(tm, lambda i: i), pl.BlockSpec((tm, D), lambda i: (i, 1)), row,
                  _full((D, D)), _full((1, D)), _full((DI, D))],
        out_specs=row, out_shape=SDS((L, D), F32), compiler_params=_cp(48, ("arbitrary",)),
    )(x, ypre3, p, yb, wglu, bglu, wout)


def _sgu_core(pv, gain, ws_ref, bs_ref):
    pu, pvv, z = pv[:, :DI], pv[:, DI:2 * DI], pv[:, 2 * DI:]
    u, gu = _gelu_and_grad(pu)
    v, gv = _gelu_and_grad(pvv)
    mu = jnp.mean(v, axis=-1, keepdims=True)
    vc = v - mu
    rstd = lax.rsqrt(jnp.mean(vc * vc, axis=-1, keepdims=True) + EPS)
    vhat = vc * rstd
    vn = vhat * gain
    t = lax.broadcasted_iota(jnp.int32, (CH, CH), 0)
    s_ = lax.broadcasted_iota(jnp.int32, (CH, CH), 1)
    mask = t >= s_
    wm = [jnp.where(mask, ws_ref[g], 0.0).astype(MXU) for g in range(SG)]
    s = jnp.concatenate([_mm(wm[g], vn[:, g * SGD:(g + 1) * SGD]) + bs_ref[g] for g in range(SG)], axis=1)
    return gu, gv, z, u, vhat, rstd, vn, mask, wm, s


SGU_CHUNKS = 2


def _sgu_fwd_bwd(p2, x1, gain, wsp, bsp, wout, fnorm, tgt):
    L = p2.shape[0]
    nc = min(SGU_CHUNKS, L // CH)
    rb = nc * CH

    def body(p_ref, x1_ref, g_ref, ws_ref, bs_ref, wo_ref, fn_ref, t_ref,
             dp_ref, y_ref, dx2_ref, gg_ref, gws_ref, gbs_ref, gfn_ref, loss_ref):
        @pl.when(pl.program_id(0) == 0)
        def _():
            gg_ref[...] = jnp.zeros_like(gg_ref)
            gws_ref[...] = jnp.zeros_like(gws_ref)
            gbs_ref[...] = jnp.zeros_like(gbs_ref)
            gfn_ref[...] = jnp.zeros_like(gfn_ref)
            loss_ref[...] = jnp.zeros_like(loss_ref)

        gain = g_ref[...]
        kept = []
        for c in range(nc):
            rows = slice(c * CH, (c + 1) * CH)
            gu, gv, z, u, vhat, rstd, vn, mask, wm, s = _sgu_core(p_ref[rows, :], gain, ws_ref, bs_ref)
            act, dact = _silu_and_grad(z)
            y_ref[rows, :] = (u * s * act).astype(MXU)
            kept.append((gu, gv, u, vhat, rstd, vn, mask, wm, s, act, dact))
        x2 = x1_ref[...] + jnp.dot(y_ref[...], wo_ref[...], preferred_element_type=F32)
        xhat, r = _rms(x2)
        fn = fn_ref[...]
        e = xhat * fn - t_ref[...]
        loss_ref[...] += 0.5 * jnp.sum(jnp.mean(e * e, axis=-1, keepdims=True), axis=0, keepdims=True)
        do = e * (1.0 / D)
        gfn_ref[...] += jnp.sum(do * xhat, axis=0, keepdims=True)
        dxhat = do * fn
        dx2 = r * (dxhat - xhat * jnp.mean(dxhat * xhat, axis=-1, keepdims=True))
        dx2_ref[...] = dx2
        dy_all = _mm_nt(dx2, wo_ref[...])
        for c in range(nc):
            rows = slice(c * CH, (c + 1) * CH)
            gu, gv, u, vhat, rstd, vn, mask, wm, s, act, dact = kept[c]
            dy = dy_all[rows, :]
            du = dy * s * act
            ds = dy * u * act
            dz = dy * u * s * dact
            dvn = []
            for g in range(SG):
                ds_g = ds[:, g * SGD:(g + 1) * SGD]
                vn_g = vn[:, g * SGD:(g + 1) * SGD]
                gbs_ref[g] += jnp.sum(ds_g, axis=1, keepdims=True)
                gws_ref[g] += jnp.where(mask, _mm_nt(ds_g, vn_g), 0.0)
                dvn.append(_mm_tn(wm[g], ds_g))
            dvn = jnp.concatenate(dvn, axis=1)
            gg_ref[...] += jnp.sum(dvn * vhat, axis=0, keepdims=True)
            dvhat = dvn * gain
            dv = rstd * (dvhat - jnp.mean(dvhat, axis=-1, keepdims=True)
                         - vhat * jnp.mean(dvhat * vhat, axis=-1, keepdims=True))
            dp_ref[rows, :] = jnp.concatenate([du * gu, dv * gv, dz], axis=1).astype(dp_ref.dtype)

    row = pl.BlockSpec((rb, D), lambda i: (i, 0))
    wide = pl.BlockSpec((rb, NIN), lambda i: (i, 0))
    return pl.pallas_call(
        body, name="sgu_fwd_bwd", grid=(L // rb,),
        in_specs=[wide, row, _full((1, DI)), _full((SG, CH, CH)), _full((SG, CH, 1)), _full((DI, D)),
                  _full((1, D)), row],
        out_specs=[wide, pl.BlockSpec((rb, DI), lambda i: (i, 0)), row,
                   _full((1, DI)), _full((SG, CH, CH)), _full((SG, CH, 1)), _full((1, D)), _full((1, 128))],
        out_shape=[SDS((L, NIN), MXU), SDS((L, DI), MXU), SDS((L, D), F32), SDS((1, DI), F32),
                   SDS((SG, CH, CH), F32), SDS((SG, CH, 1), F32), SDS((1, D), F32), SDS((1, 128), F32)],
        compiler_params=_cp(60, ("arbitrary",)),
    )(p2, x1, gain, wsp, bsp, wout, fnorm, tgt)


def _my_index():
    return 4 * lax.axis_index("x") + 2 * lax.axis_index("y") + lax.axis_index("c")


def _ordered_sum(land_ref, own, me):
    g = None
    for s in range(NDEV):
        part = jnp.where(me == s, own, land_ref[s].astype(F32))
        g = part if g is None else g + part
    return g


def _adamw_math(w, m, v, g):
    mn = ADAM_B1 * m + (1.0 - ADAM_B1) * g
    vn = ADAM_B2 * v + (1.0 - ADAM_B2) * (g * g)
    mhat = mn / BC1
    vhat = vn / BC2
    return g, -ADAM_LR * (mhat / (jnp.sqrt(vhat) + ADAM_EPS) + ADAM_WD * w), mn, vn


def _adamw(w, m, v, land, own, name):
    R, C = w.shape
    tr = R
    for cand in (256, 128, 64, 32, 16, 8):
        if R % cand == 0 and R > cand:
            tr = cand
            break

    def body(w_ref, m_ref, v_ref, land_ref, own_ref, g_ref, d_ref, mo_ref, vo_ref):
        g = _ordered_sum(land_ref, own_ref[...].astype(F32), _my_index())
        for o, val in zip((g_ref, d_ref, mo_ref, vo_ref), _adamw_math(w_ref[...], m_ref[...], v_ref[...], g)):
            o[...] = val

    blk = pl.BlockSpec((tr, C), lambda i: (i, 0))
    out = SDS((R, C), F32)
    return pl.pallas_call(
        body, name=name, grid=(R // tr,),
        in_specs=[blk, blk, blk, pl.BlockSpec((NDEV, tr, C), lambda i: (0, i, 0)), blk],
        out_specs=[blk, blk, blk, blk], out_shape=[out, out, out, out],
        compiler_params=_cp(40, ("arbitrary",)),
    )(w, m, v, land, own)


def _adamw_many(ws, ms, vs, lands, owns, name):
    k = len(ws)

    def body(*refs):
        ins, outs = refs[:5 * k], refs[5 * k:]
        me = _my_index()
        for i in range(k):
            w_ref, m_ref, v_ref, land_ref, own_ref = (ins[j * k + i] for j in range(5))
            g = _ordered_sum(land_ref, own_ref[...], me)
            for j, val in enumerate(_adamw_math(w_ref[...], m_ref[...], v_ref[...], g)):
                outs[j * k + i][...] = val

    out_shape = [SDS(w.shape, F32) for _ in range(4) for w in ws]
    res = pl.pallas_call(body, name=name, out_shape=out_shape, compiler_params=_cp(60))(*ws, *ms, *vs, *lands, *owns)
    return [res[j * k:(j + 1) * k] for j in range(4)]


MESH = pl.DeviceIdType.MESH
HBM_SPEC = pl.BlockSpec(memory_space=pltpu.HBM)
SEM_SPEC = pl.BlockSpec(memory_space=pltpu.SEMAPHORE)
EFFECT = pltpu.SideEffectType.DATAFLOW_SIDE_EFFECTING


def _me_and_peers():
    x, y, c = lax.axis_index("x"), lax.axis_index("y"), lax.axis_index("c")
    me = 4 * x + 2 * y + c
    peers = []
    for r in range(1, NDEV):
        px, py, pc = x ^ ((r >> 2) & 1), y ^ ((r >> 1) & 1), c ^ (r & 1)
        peers.append(((px, py, pc), 4 * px + 2 * py + pc))
    return me, peers


def _land_shape(a, scatter):
    return (NDEV,) + (a.shape[1:] if scatter else a.shape)


def _remote(src, dst, send_sems, recv_sems, r, k, n, dev):
    i = r * n + k
    return pltpu.make_async_remote_copy(src_ref=src, dst_ref=dst, send_sem=send_sems.at[i], recv_sem=recv_sems.at[i],
                                        device_id=dev, device_id_type=MESH)


def _exchange(arrays, scatter, name):
    n = len(arrays)
    out_shape = [SDS(_land_shape(a, scatter), a.dtype) for a in arrays]

    def body(*refs):
        ins, outs = refs[:n], refs[n:2 * n]
        send_sems, recv_sems, loc_sems = refs[2 * n:]
        me, peers = _me_and_peers()
        local = []
        for k in range(n):
            src = ins[k].at[me] if scatter else ins[k]
            cp = pltpu.make_async_copy(src, outs[k].at[me], loc_sems.at[k])
            cp.start()
            local.append(cp)
        sends = []
        for r, (dev, lin) in enumerate(peers):
            for k in range(n):
                src = ins[k].at[lin] if scatter else ins[k]
                cp = _remote(src, outs[k].at[me], send_sems, recv_sems, r, k, n, dev)
                cp.start()
                sends.append(cp)
        for r, (dev, lin) in enumerate(peers):
            for k in range(n):
                src = ins[k].at[me] if scatter else ins[k]
                _remote(src, outs[k].at[lin], send_sems, recv_sems, r, k, n, dev).wait_recv()
        for cp in sends:
            cp.wait_send()
        for cp in local:
            cp.wait()

    return pl.pallas_call(
        body, name=name, in_specs=[HBM_SPEC] * n, out_specs=[HBM_SPEC] * n, out_shape=out_shape,
        scratch_shapes=[pltpu.SemaphoreType.DMA(((NDEV - 1) * n,)), pltpu.SemaphoreType.DMA(((NDEV - 1) * n,)),
                        pltpu.SemaphoreType.DMA((n,))],
    )(*arrays)


def _exchange_start(arrays, scatter, name):
    n = len(arrays)
    lands = [lax.empty(_land_shape(a, scatter), a.dtype) for a in arrays]

    def body(*refs):
        ins, lnd = refs[:n], refs[n:2 * n]
        send_sems, recv_sems, own_sems = refs[2 * n:2 * n + 3]
        token = refs[-1]
        me, peers = _me_and_peers()
        for r, (dev, lin) in enumerate(peers):
            for k in range(n):
                src = ins[k].at[lin] if scatter else ins[k]
                _remote(src, lnd[k].at[me], send_sems, recv_sems, r, k, n, dev).start()
        if not scatter:
            for k in range(n):
                pltpu.make_async_copy(ins[k], lnd[k].at[me], own_sems.at[k]).start()
        token[...] = jnp.zeros_like(token)

    sem = pltpu.SemaphoreType.DMA(((NDEV - 1) * n,))
    outs = pl.pallas_call(
        body, name=name,
        out_shape=(sem, sem, pltpu.SemaphoreType.DMA((n,)), *[pltpu.HBM(a.shape, a.dtype) for a in arrays],
                   *[pltpu.HBM(l.shape, l.dtype) for l in lands], SDS((8, 128), F32)),
        in_specs=[HBM_SPEC] * (2 * n),
        out_specs=(SEM_SPEC, SEM_SPEC, SEM_SPEC, *[HBM_SPEC] * (2 * n), pl.BlockSpec(memory_space=pltpu.VMEM)),
        input_output_aliases={k: 3 + k for k in range(2 * n)},
        compiler_params=pltpu.CompilerParams(has_side_effects=EFFECT),
    )(*[pltpu.with_memory_space_constraint(a, pltpu.HBM) for a in arrays],
      *[pltpu.with_memory_space_constraint(l, pltpu.HBM) for l in lands])
    return (n, scatter, outs[0], outs[1], outs[2], outs[3:3 + n], outs[3 + n:3 + 2 * n]), outs[-1]


def _exchange_wait(handle, after, name):
    n, scatter, send_sems, recv_sems, own_sems, thru, lands = handle
    after = tuple(after)

    def body(*refs):
        ins, lnd = refs[:n], refs[n:2 * n]
        send_sems, recv_sems, own_sems = refs[2 * n:2 * n + 3]
        me, peers = _me_and_peers()
        for r, (dev, lin) in enumerate(peers):
            for k in range(n):
                src = ins[k].at[lin] if scatter else ins[k]
                cp = _remote(src, lnd[k].at[lin], send_sems, recv_sems, r, k, n, dev)
                cp.wait_send()
                cp.wait_recv()
        if not scatter:
            for k in range(n):
                pltpu.make_async_copy(ins[k], lnd[k].at[me], own_sems.at[k]).wait()

    outs = pl.pallas_call(
        body, name=name,
        out_shape=(*[pltpu.HBM(a.shape, a.dtype) for a in thru], *[pltpu.HBM(l.shape, l.dtype) for l in lands]),
        in_specs=[HBM_SPEC] * (2 * n) + [SEM_SPEC, SEM_SPEC, SEM_SPEC] + [ANY_SPEC] * len(after),
        out_specs=tuple([HBM_SPEC] * (2 * n)),
        input_output_aliases={k: k for k in range(2 * n)},
        compiler_params=pltpu.CompilerParams(has_side_effects=EFFECT),
    )(*thru, *lands, send_sems, recv_sems, own_sems, *after)
    return list(outs[:n]), list(outs[n:])


CHIP_RELATIONS = (2, 4, 6)


def _peer(r):
    x, y, c = lax.axis_index("x"), lax.axis_index("y"), lax.axis_index("c")
    px, py, pc = x ^ ((r >> 2) & 1), y ^ ((r >> 1) & 1), c ^ (r & 1)
    return (px, py, pc), 4 * px + 2 * py + pc


def _copy(src, dst, send_sems, recv_sems, i, dev):
    return pltpu.make_async_remote_copy(src_ref=src, dst_ref=dst, send_sem=send_sems.at[i], recv_sem=recv_sems.at[i],
                                        device_id=dev, device_id_type=MESH)


def _gather2_start(a, name):
    land = lax.empty((NDEV,) + a.shape, a.dtype)

    def body(own, lnd, send_sems, recv_sems, own_sem, own_thru, lnd_thru, token):
        me = _my_index()
        for i, r in enumerate((1,) + CHIP_RELATIONS):
            dev, _ = _peer(r)
            _copy(own, lnd.at[me], send_sems, recv_sems, i, dev).start()
        pltpu.make_async_copy(own, lnd.at[me], own_sem.at[0]).start()
        token[...] = jnp.zeros_like(token)

    sem4 = pltpu.SemaphoreType.DMA((4,))
    outs = pl.pallas_call(
        body, name=name,
        out_shape=(sem4, sem4, pltpu.SemaphoreType.DMA((1,)), pltpu.HBM(a.shape, a.dtype),
                   pltpu.HBM(land.shape, land.dtype), SDS((8, 128), F32)),
        in_specs=[HBM_SPEC, HBM_SPEC],
        out_specs=(SEM_SPEC, SEM_SPEC, SEM_SPEC, HBM_SPEC, HBM_SPEC, pl.BlockSpec(memory_space=pltpu.VMEM)),
        input_output_aliases={0: 3, 1: 4},
        compiler_params=pltpu.CompilerParams(has_side_effects=EFFECT),
    )(pltpu.with_memory_space_constraint(a, pltpu.HBM), pltpu.with_memory_space_constraint(land, pltpu.HBM))
    return outs[:5], outs[5]


def _gather2_forward(handle, after, name):
    send_sems, recv_sems, own_sem, own, land = handle
    after = tuple(after)

    def body(lnd, recv_sems, *rest):
        send2, recv2, lnd_thru = rest[len(after):]
        sib, _ = _peer(1)
        for k, r in enumerate(CHIP_RELATIONS):
            dev, lin = _peer(r)
            _copy(lnd.at[lin], lnd.at[lin], recv_sems, recv_sems, 1 + k, dev).wait_recv()
            _copy(lnd.at[lin], lnd.at[lin], send2, recv2, k, sib).start()

    sem3 = pltpu.SemaphoreType.DMA((3,))
    send2, recv2, land = pl.pallas_call(
        body, name=name,
        out_shape=(sem3, sem3, pltpu.HBM(land.shape, land.dtype)),
        in_specs=[HBM_SPEC, SEM_SPEC] + [ANY_SPEC] * len(after),
        out_specs=(SEM_SPEC, SEM_SPEC, HBM_SPEC),
        input_output_aliases={0: 2},
        compiler_params=pltpu.CompilerParams(has_side_effects=EFFECT),
    )(land, recv_sems, *after)
    return send_sems, recv_sems, own_sem, send2, recv2, own, land


def _gather2_wait(handle, name):
    send_sems, recv_sems, own_sem, send2, recv2, own, land = handle

    def body(own_ref, lnd, send_sems, recv_sems, own_sem, send2, recv2, own_thru, lnd_thru):
        me = _my_index()
        sib, sib_lin = _peer(1)
        for i, r in enumerate((1,) + CHIP_RELATIONS):
            dev, _ = _peer(r)
            _copy(own_ref, lnd.at[me], send_sems, recv_sems, i, dev).wait_send()
        _copy(own_ref, lnd.at[sib_lin], send_sems, recv_sems, 0, sib).wait_recv()
        for k, r in enumerate(CHIP_RELATIONS):
            _, lin = _peer(r)
            _, lin_other = _peer(r ^ 1)
            _copy(lnd.at[lin], lnd.at[lin], send2, recv2, k, sib).wait_send()
            _copy(lnd.at[lin_other], lnd.at[lin_other], send2, recv2, k, sib).wait_recv()
        pltpu.make_async_copy(own_ref, lnd.at[me], own_sem.at[0]).wait()

    outs = pl.pallas_call(
        body, name=name,
        out_shape=(pltpu.HBM(own.shape, own.dtype), pltpu.HBM(land.shape, land.dtype)),
        in_specs=[HBM_SPEC, HBM_SPEC] + [SEM_SPEC] * 5,
        out_specs=(HBM_SPEC, HBM_SPEC),
        input_output_aliases={0: 0, 1: 1},
        compiler_params=pltpu.CompilerParams(has_side_effects=EFFECT),
    )(own, land, send_sems, recv_sems, own_sem, send2, recv2)
    return outs[1]


def _local_step(x, tgt, norm_even, first_weight, lam_re, lam_im, log_dt, b_re, b_im, c_re, c_im, s5_d, bglu,
                ret_gain, wsp, bsp, fnorm, late_weights, emit, start_token=None):
    L = x.shape[0]
    lr3, li3 = lam_re.reshape(G, 1, P), lam_im.reshape(G, 1, P)
    dt3 = log_dt.reshape(G, 1, 1)
    br3, bi3 = jnp.swapaxes(b_re, 1, 2), jnp.swapaxes(b_im, 1, 2)
    abr3, abi3, bbr3, bbi3 = _s5_disc(lr3, li3, dt3, br3, bi3)
    bb = jnp.concatenate([_embed(bbr3), _embed(bbi3)], axis=2).astype(MXU)
    cm = jnp.concatenate([_embed(jnp.swapaxes(c_re, 1, 2)), -_embed(jnp.swapaxes(c_im, 1, 2))], axis=1).astype(MXU)
    abr, abi = abr3.reshape(1, NSTATE), abi3.reshape(1, NSTATE)
    pwr, pwi, _, _ = _s5_tables(abr, abi, NT, "s5_tables_step")
    par, pai, pbr, pbi = _s5_tables(pwr[NT - 1:NT], pwi[NT - 1:NT], CH, "s5_tables_chunk")
    inv = (ROPE_BASE ** (-jnp.arange(DK // 2, dtype=F32) / (DK // 2))).reshape(1, DK // 2)
    cos, sin = _rope_tables(L, inv)
    bsp3 = bsp.reshape(SG, CH, 1)

    def dep(token):
        return NO_DEPS if token is None else (token,)

    win_e = first_weight((cos, pbi, cm))
    p, u3 = _in_proj(x, norm_even, win_e, "in_even", dep(start_token), lane_blocks=True)
    ypre, s5_states = _s5_scan_fwd(u3, bb, cm, pwr, pwi, pwr.T, pwi.T, par, pai, s5_d)
    yb, ret_states, ret_scores, ret_o = _ret_fwd(p, cos, sin, ret_gain)
    wglu, wout_e, norm_odd, win_o, sgu_gain, wout_o = late_weights((ypre, yb))
    x1 = _out_even(x, ypre, p, yb, wglu, bglu, wout_e)
    p2 = _in_proj(x1, norm_odd, win_o, "in_odd")
    dp2, y_o, dx2, g_sgu_gain, g_wsp, g_bsp, g_fnorm, loss = _sgu_fwd_bwd(
        p2, x1, sgu_gain, wsp, bsp3, wout_o, fnorm, tgt)

    g_wout_o = _wgrad_rows([y_o], dx2, "wgrad_out_odd")
    g_win_o = _wgrad_cols(x1, norm_odd, dp2, "wgrad_in_odd")
    tok = emit("odd", dict(w_in_odd=g_win_o, w_out_odd=g_wout_o))
    dx1, g_norm_odd = _in_proj_bwd_x(dp2, x1, norm_odd, win_o, dx2, "in_odd_bwd", dep(tok))
    tok = emit("small_odd", dict(norm_odd=g_norm_odd, sgu_norm_gain=g_sgu_gain, sgu_w_spatial=g_wsp,
                                 sgu_b_spatial=g_bsp.reshape(SG, CH), final_norm=g_fnorm))

    dypre, daz, yg, dt, ya2, g_bglu = _s5_gate_bwd(ypre, p, dx1, wout_e, wglu, bglu, dep(tok))
    g_wglu = _wgrad_rows([yg], dt, "wgrad_glu")
    tok = emit("glu", dict(s5_w_glu=g_wglu))
    du, g_d, g_cm, g_bb, g_ar, g_ai = _s5_scan_bwd(u3, dypre, s5_states, bb, cm, jnp.swapaxes(cm, 1, 2), abr, abi,
                                                   pwr, pwi, par, pai, pbr, pbi, s5_d, dep(tok))
    dbbr3 = _diag_blocks(g_bb[:, :, :SW], HG, P)
    dbbi3 = _diag_blocks(g_bb[:, :, SW:], HG, P)
    g_c_re = jnp.swapaxes(_diag_blocks(g_cm[:, :SW, :], P, HG), 1, 2)
    g_c_im = -jnp.swapaxes(_diag_blocks(g_cm[:, SW:, :], P, HG), 1, 2)
    g_lr3, g_li3, g_dt3, g_br3, g_bi3 = _s5_disc_bwd(
        lr3, li3, dt3, br3, bi3, g_ar.reshape(G, 1, P), g_ai.reshape(G, 1, P), dbbr3, dbbi3)
    tok = emit("small_s5", dict(
        s5_lam_re=g_lr3.reshape(G, P), s5_lam_im=g_li3.reshape(G, P), s5_log_dt=g_dt3.reshape(1, G),
        s5_b_re=g_br3, s5_b_im=g_bi3, s5_c_re=g_c_re, s5_c_im=g_c_im, s5_d=g_d, s5_b_glu=g_bglu))
    dp, yb2, g_ret_gain = _ret_bwd(p, cos, sin, ret_gain, ret_states, ret_scores, ret_o, dx1, wout_e, du, daz,
                                   dep(tok))
    g_win_e = _wgrad_cols(x, norm_even, dp, "wgrad_in_even")
    tok = emit("even_cols", dict(w_in_even=g_win_e))
    g_wout_e = _wgrad_rows([ya2, yb2], dx1, "wgrad_out_even", dep(tok))
    tok = emit("even_rows", dict(w_out_even=g_wout_e))
    dx, g_norm_even = _in_proj_bwd_x(dp, x, norm_even, win_e, dx1, "in_even_bwd", dep(tok))
    emit("last", dict(ret_gn_gain=g_ret_gain, norm_even=g_norm_even, loss=loss))
    return dx


WEIGHTS = ['norm_even', 'w_in_even', 's5_lam_re', 's5_lam_im', 's5_log_dt', 's5_b_re', 's5_b_im', 's5_c_re',
           's5_c_im', 's5_d', 's5_w_glu', 's5_b_glu', 'ret_gn_gain', 'w_out_even', 'norm_odd', 'w_in_odd',
           'sgu_norm_gain', 'sgu_w_spatial', 'sgu_b_spatial', 'w_out_odd', 'final_norm']
BIG = ['w_in_even', 's5_w_glu', 'w_out_even', 'w_in_odd', 'w_out_odd']
SHARDED_SMALL = {'norm_odd': D // NDEV, 'sgu_norm_gain': DI // NDEV}
SCATTER_STAGES = ("odd", "glu", "even_cols", "even_rows")
GATHER_STAGES = ("small_odd", "small_s5")


def _view(n, a):
    if n in ('s5_b_re', 's5_b_im'):
        return jnp.swapaxes(a[0], 1, 2)
    if n == 'final_norm':
        return a.reshape(1, D)
    return a[0] if a.ndim >= 3 else a


def _unview(n, t, shape):
    if n in ('s5_b_re', 's5_b_im'):
        return jnp.swapaxes(t, 1, 2)[None]
    return t.reshape(shape)


def kernel(x, norm_even, w_in_even, s5_lam_re, s5_lam_im, s5_log_dt, s5_b_re, s5_b_im, s5_c_re, s5_c_im, s5_d, s5_w_glu, s5_b_glu, ret_gn_gain, w_out_even, norm_odd, w_in_odd, sgu_norm_gain, sgu_w_spatial, sgu_b_spatial, w_out_odd, final_norm, loss_target, m_norm_even, m_w_in_even, m_s5_lam_re, m_s5_lam_im, m_s5_log_dt, m_s5_b_re, m_s5_b_im, m_s5_c_re, m_s5_c_im, m_s5_d, m_s5_w_glu, m_s5_b_glu, m_ret_gn_gain, m_w_out_even, m_norm_odd, m_w_in_odd, m_sgu_norm_gain, m_sgu_w_spatial, m_sgu_b_spatial, m_w_out_odd, m_final_norm, v_norm_even, v_w_in_even, v_s5_lam_re, v_s5_lam_im, v_s5_log_dt, v_s5_b_re, v_s5_b_im, v_s5_c_re, v_s5_c_im, v_s5_d, v_s5_w_glu, v_s5_b_glu, v_ret_gn_gain, v_w_out_even, v_norm_odd, v_w_in_odd, v_sgu_norm_gain, v_sgu_w_spatial, v_sgu_b_spatial, v_w_out_odd, v_final_norm):
    args = dict(locals())
    w = {n: args[n] for n in WEIGHTS}
    m = {n: args["m_" + n] for n in WEIGHTS}
    v = {n: args["v_" + n] for n in WEIGHTS}
    me = _my_index()

    first_handle, _ = _gather2_start(w['w_in_even'][0].astype(MXU), "gather_first_start")

    def first_weight(after):
        return _gather2_wait(_gather2_forward(first_handle, after, "gather_first_forward"), "gather_first_wait")

    late_own = [w['s5_w_glu'][0].astype(MXU), w['w_out_even'][0].astype(MXU), w['norm_odd'],
                w['w_in_odd'][0].astype(MXU), w['sgu_norm_gain'], w['w_out_odd'][0].astype(MXU)]
    late_handle, start_token = _exchange_start(late_own, False, "gather_late_start")

    def late_weights(after):
        _, (wglu, wout_e, nodd, win_o, sgug, wout_o) = _exchange_wait(late_handle, after, "gather_late_wait")
        return (wglu.reshape(D, D), wout_e.reshape(DI, D), nodd.reshape(1, D), win_o, sgug.reshape(1, DI),
                wout_o.reshape(DI, D))

    pending = {}
    small_last = {}

    def emit(stage, grads):
        if stage == "last":
            small_last.update(grads)
            return None
        names = list(grads)
        handle, token = _exchange_start([grads[n] for n in names], stage in SCATTER_STAGES, stage + "_start")
        pending[stage] = (handle, names)
        return token

    dx = _local_step(
        x[0], loss_target[0], w['norm_even'], first_weight, w['s5_lam_re'][0], w['s5_lam_im'][0], w['s5_log_dt'][0],
        w['s5_b_re'][0], w['s5_b_im'][0], w['s5_c_re'][0], w['s5_c_im'][0], w['s5_d'], w['s5_b_glu'],
        w['ret_gn_gain'], w['sgu_w_spatial'][0], w['sgu_b_spatial'][0], w['final_norm'].reshape(1, D),
        late_weights, emit, start_token)

    out_g, out_d, out_m, out_v = {}, {}, {}, {}
    after = dx
    for stage in SCATTER_STAGES:
        handle, names = pending[stage]
        sent, lands = _exchange_wait(handle, (after,), stage + "_wait")
        for n, land, stack in zip(names, lands, sent):
            shp = w[n].shape
            r, c = shp[1], shp[2]
            own = lax.dynamic_index_in_dim(stack, me, 0, keepdims=False)
            res = _adamw(w[n].reshape(r, c), m[n].reshape(r, c), v[n].reshape(r, c), land, own, "adamw_" + n)
            out_g[n], out_d[n], out_m[n], out_v[n] = (t.reshape(shp) for t in res)
            after = res[0]

    names, owns, lands = [], [], []
    for stage in GATHER_STAGES:
        handle, stage_names = pending[stage]
        sent, got = _exchange_wait(handle, (after,), stage + "_wait")
        names, owns, lands = names + stage_names, owns + sent, lands + got
    last_names = list(small_last)
    last = _exchange([small_last[n] for n in last_names], False, "gather_last")
    loss_parts = last[last_names.index("loss")][:, 0, 0]
    for n, own, land in zip(last_names, (small_last[n] for n in last_names), last):
        if n != "loss":
            names, owns, lands = names + [n], owns + [own], lands + [land]
    for i, n in enumerate(names):
        if n in SHARDED_SMALL:
            width = SHARDED_SMALL[n]
            owns[i] = lax.dynamic_slice_in_dim(owns[i], me * width, width, axis=1)
            lands[i] = lax.dynamic_slice_in_dim(lands[i], me * width, width, axis=2)
    res = _adamw_many([_view(n, w[n]) for n in names], [_view(n, m[n]) for n in names],
                      [_view(n, v[n]) for n in names], lands, owns, "adamw_small")
    for dst, vals in zip((out_g, out_d, out_m, out_v), res):
        for n, t in zip(names, vals):
            dst[n] = _unview(n, t, w[n].shape)

    loss_total = jnp.sum(loss_parts)
    return (loss_total, dx[None], *[out_g[n] for n in WEIGHTS], *[out_d[n] for n in WEIGHTS],
            *[out_m[n] for n in WEIGHTS], *[out_v[n] for n in WEIGHTS])
```

```python
import math

import jax
import jax.numpy as jnp
from jax import lax
from jax.experimental import pallas as pl
from jax.experimental.pallas import tpu as pltpu

F32 = jnp.float32
MXU = jnp.bfloat16
AXES = ("x", "y", "c")
NDEV = 8
D = 1024
NIN = 6144
WIN_BLK = NIN // NDEV
DI = 2048
G, P, HG = 64, 64, 16
GB = 8
NJ = G // GB
SW = GB * P
UW = GB * HG
NSTATE = G * P
HEADS, DK = 4, 256
CH = 128
SG, SGD = 4, 512
EPS = 1e-6
ROPE_BASE = 10000.0
VMEM_CAP_V7X = 64 * 1024 * 1024
LOG_G = [math.log1p(-2.0 ** (-5.0 - h)) for h in range(HEADS)]
GELU_C = math.sqrt(2.0 / math.pi)

ADAM_LR, ADAM_B1, ADAM_B2, ADAM_EPS, ADAM_WD, ADAM_STEP = 0.001, 0.9, 0.999, 1e-08, 0.01, 10
BC1 = 1.0 - ADAM_B1 ** ADAM_STEP
BC2 = 1.0 - ADAM_B2 ** ADAM_STEP

SDS = jax.ShapeDtypeStruct
ARB2 = ("arbitrary", "arbitrary")


def _cp(vmem_mib, sem=None):
    kw = dict(vmem_limit_bytes=min(vmem_mib * 1024 * 1024, VMEM_CAP_V7X - 4 * 1024 * 1024))
    if sem is not None:
        kw["dimension_semantics"] = sem
    return pltpu.CompilerParams(**kw)


def _mm(a, b):
    return jnp.dot(a.astype(MXU), b.astype(MXU), preferred_element_type=F32)


def _mm_nt(a, b):
    return lax.dot_general(a.astype(MXU), b.astype(MXU), (((1,), (1,)), ((), ())), preferred_element_type=F32)


def _mm_tn(a, b):
    return lax.dot_general(a.astype(MXU), b.astype(MXU), (((0,), (0,)), ((), ())), preferred_element_type=F32)


def _gelu(x):
    return _gelu_and_grad(x)[0]


def _gelu_and_grad(x):
    x2 = x * x
    th = jnp.tanh(GELU_C * x * (1.0 + 0.044715 * x2))
    hp = 0.5 * (1.0 + th)
    return x * hp, hp + 0.5 * x * (1.0 - th * th) * GELU_C * (1.0 + 3.0 * 0.044715 * x2)


def _silu_and_grad(x):
    s = jax.nn.sigmoid(x)
    return x * s, s * (1.0 + x * (1.0 - s))


def _full(shape):
    nd = len(shape)
    return pl.BlockSpec(shape, lambda *_: (0,) * nd)


def _rms(xf):
    r = lax.rsqrt(jnp.mean(xf * xf, axis=-1, keepdims=True) + EPS)
    return xf * r, r


ANY_SPEC = pl.BlockSpec(memory_space=pl.ANY)
NO_DEPS = ()


def _load_once(src_hbm, dst_vmem, sem):
    @pl.when(pl.program_id(0) == 0)
    def _():
        cp = pltpu.make_async_copy(src_hbm, dst_vmem, sem)
        cp.start()
        cp.wait()


def _lane_blocks(L):
    return SDS((NJ, L, UW), F32)


def _lane_block_spec(rows, index):
    return pl.BlockSpec((NJ, rows, UW), lambda i: (0, index(i), 0))


def _from_lane_blocks(ref):
    return jnp.concatenate([ref[j] for j in range(NJ)], axis=1)


def _to_lane_blocks(ref, v):
    for j in range(NJ):
        ref[j] = v[:, j * UW:(j + 1) * UW].astype(ref.dtype)


def _in_proj(x, gain, wst, name, deps=NO_DEPS, lane_blocks=False):
    L = x.shape[0]
    tm = min(512, L)

    def body(x_ref, g_ref, w_hbm, *rest):
        outs = rest[len(deps):]
        o_ref, w_scr, sem = outs[0], outs[-2], outs[-1]
        _load_once(w_hbm, w_scr, sem)
        xhat, _ = _rms(x_ref[...])
        h = (xhat * g_ref[...]).astype(MXU)
        for c in range(NDEV):
            o_ref[:, c * WIN_BLK:(c + 1) * WIN_BLK] = jnp.dot(h, w_scr[c], preferred_element_type=F32)
        if lane_blocks:
            _to_lane_blocks(outs[1], o_ref[:, 0:D])

    p_spec, p_shape = pl.BlockSpec((tm, NIN), lambda i: (i, 0)), SDS((L, NIN), F32)
    return pl.pallas_call(
        body, name=name, grid=(L // tm,),
        in_specs=[pl.BlockSpec((tm, D), lambda i: (i, 0)), _full((1, D)), ANY_SPEC] + [ANY_SPEC] * len(deps),
        out_specs=[p_spec, _lane_block_spec(tm, lambda i: i)] if lane_blocks else p_spec,
        out_shape=[p_shape, _lane_blocks(L)] if lane_blocks else p_shape,
        scratch_shapes=[pltpu.VMEM((NDEV, D, WIN_BLK), MXU), pltpu.SemaphoreType.DMA(())],
        compiler_params=_cp(58, ("arbitrary",)),
    )(x, gain, wst, *deps)


def _in_proj_bwd_x(dp, x, gain, wst, dres, name, deps=NO_DEPS):
    L = x.shape[0]
    tm = min(512, L)

    def body(dp_ref, x_ref, g_ref, w_hbm, dres_ref, *rest):
        dx_ref, gg_ref, w_scr, sem = rest[len(deps):]
        _load_once(w_hbm, w_scr, sem)

        @pl.when(pl.program_id(0) == 0)
        def _():
            gg_ref[...] = jnp.zeros_like(gg_ref)

        dh = _mm_nt(dp_ref[:, 0:WIN_BLK], w_scr[0])
        for c in range(1, NDEV):
            dh += _mm_nt(dp_ref[:, c * WIN_BLK:(c + 1) * WIN_BLK], w_scr[c])
        xhat, r = _rms(x_ref[...])
        dxhat = dh * g_ref[...]
        dx_ref[...] = dres_ref[...] + r * (dxhat - xhat * jnp.mean(dxhat * xhat, axis=-1, keepdims=True))
        gg_ref[...] += jnp.sum(dh * xhat, axis=0, keepdims=True)

    row = pl.BlockSpec((tm, D), lambda i: (i, 0))
    return pl.pallas_call(
        body, name=name, grid=(L // tm,),
        in_specs=[pl.BlockSpec((tm, NIN), lambda i: (i, 0)), row, _full((1, D)), ANY_SPEC, row]
        + [ANY_SPEC] * len(deps),
        out_specs=[row, _full((1, D))],
        out_shape=[SDS((L, D), F32), SDS((1, D), F32)],
        scratch_shapes=[pltpu.VMEM((NDEV, D, WIN_BLK), MXU), pltpu.SemaphoreType.DMA(())],
        compiler_params=_cp(56, ("arbitrary",)),
    )(dp, x, gain, wst, dres, *deps)


def _wgrad_cols(x, gain, dp, name, deps=NO_DEPS):
    L = x.shape[0]
    tk = min(1024, L)
    nk = L // tk
    halves = 2
    nh = NDEV // halves

    def body(x_ref, g_ref, dp_ref, *rest):
        o_ref, acc = rest[len(deps):]
        k = pl.program_id(1)

        @pl.when(k == 0)
        def _():
            acc[...] = jnp.zeros_like(acc)

        xhat, _ = _rms(x_ref[...])
        acc[...] += _mm_tn(xhat * g_ref[...], dp_ref[...])

        @pl.when(k == nk - 1)
        def _():
            for c in range(nh):
                o_ref[c] = acc[:, c * WIN_BLK:(c + 1) * WIN_BLK].astype(o_ref.dtype)

    return pl.pallas_call(
        body, name=name, grid=(halves, nk),
        in_specs=[pl.BlockSpec((tk, D), lambda n, k: (k, 0)), _full((1, D)),
                  pl.BlockSpec((tk, nh * WIN_BLK), lambda n, k: (k, n))] + [ANY_SPEC] * len(deps),
        out_specs=pl.BlockSpec((nh, D, WIN_BLK), lambda n, k: (n, 0, 0)),
        out_shape=SDS((NDEV, D, WIN_BLK), MXU),
        scratch_shapes=[pltpu.VMEM((D, nh * WIN_BLK), F32)],
        compiler_params=_cp(56, ARB2),
    )(x, gain, dp, *deps)


def _wgrad_rows(a_parts, b, name, deps=NO_DEPS):
    L, N = b.shape
    na = len(a_parts)
    widths = [a.shape[1] for a in a_parts]
    M = sum(widths)
    tk = min(1024, L)
    nk = L // tk

    def body(*refs):
        a_refs, b_ref = refs[:na], refs[na]
        o_ref, acc = refs[na + 1 + len(deps):]
        k = pl.program_id(0)

        @pl.when(k == 0)
        def _():
            acc[...] = jnp.zeros_like(acc)

        bv = b_ref[...].astype(MXU)
        off = 0
        for a_ref, wd in zip(a_refs, widths):
            acc[off:off + wd, :] += _mm_tn(a_ref[...], bv)
            off += wd

        @pl.when(k == nk - 1)
        def _():
            o_ref[...] = acc[...].astype(o_ref.dtype).reshape(o_ref.shape)

    return pl.pallas_call(
        body, name=name, grid=(nk,),
        in_specs=[pl.BlockSpec((tk, wd), lambda k: (k, 0)) for wd in widths]
        + [pl.BlockSpec((tk, N), lambda k: (k, 0))] + [ANY_SPEC] * len(deps),
        out_specs=_full((NDEV, M // NDEV, N)),
        out_shape=SDS((NDEV, M // NDEV, N), MXU),
        scratch_shapes=[pltpu.VMEM((M, N), F32)],
        compiler_params=_cp(48, ("arbitrary",)),
    )(*a_parts, b, *deps)


def _s5_disc_fn(lr_raw, li, logdt, br, bi):
    lr = jnp.minimum(lr_raw, -1e-4)
    dt = jnp.exp(logdt)
    mag = jnp.exp(lr * dt)
    abr = mag * jnp.cos(li * dt)
    abi = mag * jnp.sin(li * dt)
    den = lr * lr + li * li
    nre = abr - 1.0
    nim = abi
    zr = (nre * lr + nim * li) / den
    zi = (nim * lr - nre * li) / den
    return abr, abi, zr * br - zi * bi, zr * bi + zi * br


def _s5_disc(lr, li, logdt, br, bi):
    def body(lr_ref, li_ref, dt_ref, br_ref, bi_ref, abr_ref, abi_ref, bbr_ref, bbi_ref):
        abr, abi, bbr, bbi = _s5_disc_fn(lr_ref[...], li_ref[...], dt_ref[...], br_ref[...], bi_ref[...])
        abr_ref[...] = abr
        abi_ref[...] = abi
        bbr_ref[...] = bbr
        bbi_ref[...] = bbi

    s1, s3 = SDS((G, 1, P), F32), SDS((G, HG, P), F32)
    return pl.pallas_call(body, name="s5_disc", out_shape=[s1, s1, s3, s3])(lr, li, logdt, br, bi)


def _s5_disc_bwd(lr, li, logdt, br, bi, dabr, dabi, dbbr, dbbi):
    def body(lr_ref, li_ref, dt_ref, br_ref, bi_ref, c0, c1, c2, c3, o0, o1, o2, o3, o4):
        _, vjp = jax.vjp(_s5_disc_fn, lr_ref[...], li_ref[...], dt_ref[...], br_ref[...], bi_ref[...])
        g = vjp((c0[...], c1[...], c2[...], c3[...]))
        for o, v in zip((o0, o1, o2, o3, o4), g):
            o[...] = v

    s1, s3 = SDS((G, 1, P), F32), SDS((G, HG, P), F32)
    return pl.pallas_call(body, name="s5_disc_bwd", out_shape=[s1, s1, SDS((G, 1, 1), F32), s3, s3])(
        lr, li, logdt, br, bi, dabr, dabi, dbbr, dbbi)


def _s5_tables(abr, abi, rows, name):
    def body(ar_ref, ai_ref, pfr, pfi, pbr, pbi):
        pfr[0:1, :] = ar_ref[...]
        pfi[0:1, :] = ai_ref[...]
        pbr[rows - 1:rows, :] = ar_ref[...]
        pbi[rows - 1:rows, :] = ai_ref[...]
        n = 1
        while n < rows:
            er, ei = pfr[n - 1:n, :], pfi[n - 1:n, :]
            xr, xi = pfr[0:n, :], pfi[0:n, :]
            pfr[n:2 * n, :] = er * xr - ei * xi
            pfi[n:2 * n, :] = er * xi + ei * xr
            yr, yi = pbr[rows - n:rows, :], pbi[rows - n:rows, :]
            pbr[rows - 2 * n:rows - n, :] = er * yr - ei * yi
            pbi[rows - 2 * n:rows - n, :] = er * yi + ei * yr
            n *= 2

    s = SDS((rows, NSTATE), F32)
    return pl.pallas_call(body, name=name, out_shape=[s, s, s, s], compiler_params=_cp(40))(abr, abi)


def _cscan(br, bi, pr_ref, pi_ref, reverse):
    T = br.shape[0]
    sign = -1.0 if reverse else 1.0
    row = lax.broadcasted_iota(jnp.int32, br.shape, 0)
    k = 1
    while k < T:
        akr = pr_ref[k - 1:k, :]
        aki = sign * pi_ref[k - 1:k, :]

        def shift(v):
            if k % 8 == 0:
                z = jnp.zeros((k, v.shape[1]), v.dtype)
                return jnp.concatenate([v[k:], z], 0) if reverse else jnp.concatenate([z, v[:T - k]], 0)
            if reverse:
                return jnp.where(row < T - k, pltpu.roll(v, T - k, 0), 0.0)
            return jnp.where(row >= k, pltpu.roll(v, k, 0), 0.0)

        sr, si = shift(br), shift(bi)
        br, bi = br + akr * sr - aki * si, bi + akr * si + aki * sr
        k *= 2
    return br, bi


def _embed(t):
    a, b = t.shape[1], t.shape[2]
    return jnp.einsum("jgab,gh->jgahb", t.reshape(NJ, GB, a, b), jnp.eye(GB, dtype=t.dtype)).reshape(NJ, GB * a, GB * b)


def _diag_blocks(t, a, b):
    return jnp.einsum("jgahb,gh->jgab", t.reshape(NJ, GB, a, GB, b), jnp.eye(GB, dtype=t.dtype)).reshape(G, a, b)


NT = 16


def _chunks(L):
    ncb = min(CH, L // NT)
    return ncb, NT * ncb


def _cmul_add(ar, ai, xr, xi, br, bi):
    return ar * xr - ai * xi + br, ar * xi + ai * xr + bi


def _pow_weights(w_ref, pwr_ref, pwi_ref, dst, adjoint):
    w = w_ref[...].astype(F32)
    wr, wi = w[:, :SW], w[:, SW:]
    for t in range(NT):
        k = t if adjoint else NT - 1 - t
        if k == 0:
            blk = w
        else:
            pr, pi = pwr_ref[k - 1:k, :], pwi_ref[k - 1:k, :]
            if adjoint:
                blk = jnp.concatenate([pr * wr + pi * wi, pr * wi - pi * wr], axis=1)
            else:
                blk = jnp.concatenate([pr * wr - pi * wi, pr * wi + pi * wr], axis=1)
        dst[t * UW:(t + 1) * UW, :] = blk.astype(dst.dtype)


def _slices(ref, ncb, axis):
    return jnp.concatenate([ref[pl.ds(t, ncb, stride=NT), :] for t in range(NT)], axis=axis)


def _s5_scan_fwd(u3, bb, cm, pwr, pwi, pwrt, pwit, par, pai, dskip):
    L = u3.shape[1]
    ncb, tb = _chunks(L)
    nb = L // tb

    def build(bb_ref, cm_ref, pwr_ref, pwi_ref, pct_ref, pit_ref, bbp_scr, ktp_scr, zc_scr):
        w = bb_ref[...].astype(F32)
        wr, wi = w[:, :SW], w[:, SW:]
        cmv = cm_ref[...].astype(F32)
        ct, cb = cmv[:SW, :], cmv[SW:, :]
        zero = jnp.zeros((UW, UW), MXU)
        for tau in range(NT):
            if tau == 0:
                blk = w
            else:
                pr, pi = pwr_ref[tau - 1:tau, :], pwi_ref[tau - 1:tau, :]
                blk = jnp.concatenate([pr * wr - pi * wi, pr * wi + pi * wr], axis=1)
            blk = blk.astype(MXU)
            bbp_scr[(NT - 1 - tau) * UW:(NT - tau) * UW, :] = blk
            k = jnp.dot(blk, cm_ref[...], preferred_element_type=F32).astype(MXU)
            for j in range(NT - tau):
                ktp_scr[j * UW:(j + 1) * UW, (j + tau) * UW:(j + tau + 1) * UW] = k
            pc, pic = pct_ref[:, tau:tau + 1], pit_ref[:, tau:tau + 1]
            zc_scr[:, tau * UW:(tau + 1) * UW] = jnp.concatenate(
                [pc * ct + pic * cb, pc * cb - pic * ct], axis=0).astype(MXU)
        for j in range(NT):
            for t in range(j):
                ktp_scr[j * UW:(j + 1) * UW, t * UW:(t + 1) * UW] = zero

    def body(u_ref, bb_ref, cm_ref, pwr_ref, pwi_ref, pct_ref, pit_ref, par_ref, pai_ref, d_ref, ypre_ref, cin_ref,
             carry, cin_scr, bbp_scr, ktp_scr, zc_scr):
        @pl.when(pl.program_id(1) == 0)
        def _():
            carry[...] = jnp.zeros_like(carry)
            build(bb_ref, cm_ref, pwr_ref, pwi_ref, pct_ref, pit_ref, bbp_scr, ktp_scr, zc_scr)

        c = carry[...]
        ucat = _slices(u_ref, ncb, 1)
        ub = ucat.astype(MXU)
        e = jnp.dot(ub, bbp_scr[...], preferred_element_type=F32)
        xr, xi = _cscan(e[:, :SW], e[:, SW:], par_ref, pai_ref, False)
        cr, ci = c[:, :SW], c[:, SW:]
        fr, fi = _cmul_add(par_ref[0:ncb, :], pai_ref[0:ncb, :], cr, ci, xr, xi)
        carry[...] = jnp.concatenate([fr[ncb - 1:ncb, :], fi[ncb - 1:ncb, :]], axis=1)
        row = lax.broadcasted_iota(jnp.int32, fr.shape, 0)
        cin_ref[:, :SW] = jnp.where(row >= 1, pltpu.roll(fr, 1, 0), cr)
        cin_ref[:, SW:] = jnp.where(row >= 1, pltpu.roll(fi, 1, 0), ci)
        cin_scr[...] = cin_ref[...].astype(MXU)
        y = (jnp.dot(ub, ktp_scr[...], preferred_element_type=F32)
             + jnp.dot(cin_scr[...], zc_scr[...], preferred_element_type=F32)
             + jnp.tile(d_ref[...], (1, NT)) * ucat)
        for t in range(NT):
            ypre_ref[pl.ds(t, ncb, stride=NT), :] = y[:, t * UW:(t + 1) * UW]

    tab = pl.BlockSpec((CH, SW), lambda j, i: (0, j))
    stp = pl.BlockSpec((NT, SW), lambda j, i: (0, j))
    stt = pl.BlockSpec((SW, NT), lambda j, i: (j, 0))
    vec = lambda w: pl.BlockSpec((1, w), lambda j, i: (0, j))
    return pl.pallas_call(
        body, name="s5_scan_fwd", grid=(NJ, nb),
        in_specs=[pl.BlockSpec((None, tb, UW), lambda j, i: (j, i, 0)),
                  pl.BlockSpec((None, UW, 2 * SW), lambda j, i: (j, 0, 0)),
                  pl.BlockSpec((None, 2 * SW, UW), lambda j, i: (j, 0, 0)),
                  stp, stp, stt, stt, tab, tab, vec(UW)],
        out_specs=[pl.BlockSpec((None, tb, UW), lambda j, i: (j, i, 0)),
                   pl.BlockSpec((None, ncb, 2 * SW), lambda j, i: (j, i, 0))],
        out_shape=[_lane_blocks(L), SDS((NJ, L // NT, 2 * SW), F32)],
        scratch_shapes=[pltpu.VMEM((1, 2 * SW), F32), pltpu.VMEM((ncb, 2 * SW), MXU),
                        pltpu.VMEM((NT * UW, 2 * SW), MXU), pltpu.VMEM((NT * UW, NT * UW), MXU),
                        pltpu.VMEM((2 * SW, NT * UW), MXU)],
        compiler_params=_cp(56, ARB2),
    )(u3, bb, cm, pwr, pwi, pwrt, pwit, par, pai, dskip)


def _s5_gate_bwd(ypre3, p, dx1, wout_e, wglu, bglu, deps=NO_DEPS):
    L = p.shape[0]
    sub = min(256, L)
    nsub = 2 if L % (2 * sub) == 0 else 1
    tm = nsub * sub

    def body(y_ref, az_ref, dx1_ref, wo_ref, wg_ref, bg_ref, *rest):
        dyp_ref, daz_ref, yg_ref, dt_ref, ya_ref, gbg_ref = rest[len(deps):]

        @pl.when(pl.program_id(0) == 0)
        def _():
            gbg_ref[...] = jnp.zeros_like(gbg_ref)

        for c in range(nsub):
            rows = slice(c * sub, (c + 1) * sub)
            ypre = jnp.concatenate([y_ref[j, rows, :] for j in range(NJ)], axis=1)
            yg, dgelu = _gelu_and_grad(ypre)
            sg = jax.nn.sigmoid(_mm(yg, wg_ref[...]) + bg_ref[...])
            act, dact = _silu_and_grad(az_ref[rows, :])
            y2 = yg * sg
            dya = _mm_nt(dx1_ref[rows, :], wo_ref[...])
            daz_ref[rows, :] = (dya * y2 * dact).astype(daz_ref.dtype)
            dy2 = dya * act
            dt = dy2 * yg * sg * (1.0 - sg)
            dyp = (dy2 * sg + _mm_nt(dt, wg_ref[...])) * dgelu
            for j in range(NJ):
                dyp_ref[j, rows, :] = dyp[:, j * UW:(j + 1) * UW]
            yg_ref[rows, :] = yg.astype(yg_ref.dtype)
            dt_ref[rows, :] = dt.astype(dt_ref.dtype)
            ya_ref[rows, :] = (y2 * act).astype(ya_ref.dtype)
            gbg_ref[...] += jnp.sum(dt, axis=0, keepdims=True)

    row = pl.BlockSpec((tm, D), lambda i: (i, 0))
    return pl.pallas_call(
        body, name="s5_gate_bwd", grid=(L // tm,),
        in_specs=[_lane_block_spec(tm, lambda i: i), pl.BlockSpec((tm, D), lambda i: (i, 1)), row,
                  pl.BlockSpec((D, D), lambda i: (0, 0)), _full((D, D)), _full((1, D))] + [ANY_SPEC] * len(deps),
        out_specs=[_lane_block_spec(tm, lambda i: i), row, row, row, row, _full((1, D))],
        out_shape=[_lane_blocks(L), SDS((L, D), MXU), SDS((L, D), MXU), SDS((L, D), MXU), SDS((L, D), MXU),
                   SDS((1, D), F32)],
        compiler_params=_cp(56, ("arbitrary",)),
    )(ypre3, p, dx1, wout_e, wglu, bglu, *deps)


def _s5_scan_bwd(u3, dypre3, states, bb, cm, cmt, abr, abi, pwr, pwi, par, pai, pbr, pbi, dskip, deps=NO_DEPS):
    L = u3.shape[1]
    ncb, tb = _chunks(L)
    nb = L // tb
    rev = lambda i: nb - 1 - i

    def body(u_ref, dy_ref, st_ref, bb_ref, cm_ref, cmt_ref, ar_ref, ai_ref, pwr_ref, pwi_ref, par_ref, pai_ref,
             pbr_ref, pbi_ref, d_ref, *rest):
        (du_ref, gd_ref, gcm_ref, gbb_ref, gar_ref, gai_ref,
         lcarry, s_scr, gs_scr, cmp_scr) = rest[len(deps):]
        del cm_ref

        @pl.when(pl.program_id(1) == 0)
        def _():
            _pow_weights(cmt_ref, pwr_ref, pwi_ref, cmp_scr, True)
            lcarry[...] = jnp.zeros_like(lcarry)
            gd_ref[...] = jnp.zeros_like(gd_ref)
            gcm_ref[...] = jnp.zeros_like(gcm_ref)
            gbb_ref[...] = jnp.zeros_like(gbb_ref)
            gar_ref[...] = jnp.zeros_like(gar_ref)
            gai_ref[...] = jnp.zeros_like(gai_ref)

        ar, ai = ar_ref[...], ai_ref[...]
        c = st_ref[...]
        s_scr[0] = c
        sr, si = c[:, :SW], c[:, SW:]
        for t in range(NT):
            bu = _mm(u_ref[pl.ds(t, ncb, stride=NT), :], bb_ref[...])
            sr, si = _cmul_add(ar, ai, sr, si, bu[:, :SW], bu[:, SW:])
            s_scr[t + 1] = jnp.concatenate([sr, si], axis=1)
        dys = [dy_ref[pl.ds(t, ncb, stride=NT), :] for t in range(NT)]
        for t in range(NT):
            gs_scr[t] = _mm(dys[t], cmt_ref[...])
        f = _mm(jnp.concatenate(dys, axis=1), cmp_scr[...])
        xr, xi = _cscan(f[:, :SW], f[:, SW:], par_ref, pai_ref, True)
        lc = lcarry[...]
        lcr, lci = lc[:, :SW], lc[:, SW:]
        hr, hi = _cmul_add(pbr_ref[CH - ncb:CH, :], -pbi_ref[CH - ncb:CH, :], lcr, lci, xr, xi)
        lcarry[...] = jnp.concatenate([hr[0:1, :], hi[0:1, :]], axis=1)
        row = lax.broadcasted_iota(jnp.int32, hr.shape, 0)
        lr_ = jnp.where(row < ncb - 1, pltpu.roll(hr, ncb - 1, 0), lcr)
        li_ = jnp.where(row < ncb - 1, pltpu.roll(hi, ncb - 1, 0), lci)
        gar = jnp.zeros((1, SW), F32)
        gai = jnp.zeros((1, SW), F32)
        for t in reversed(range(NT)):
            gs = gs_scr[t]
            lr_, li_ = _cmul_add(ar, -ai, lr_, li_, gs[:, :SW], gs[:, SW:])
            rows = pl.ds(t, ncb, stride=NT)
            u_t, dy_t = u_ref[rows, :], dy_ref[rows, :]
            lam = jnp.concatenate([lr_, li_], axis=1)
            gbb_ref[...] += _mm_tn(u_t, lam)
            du_ref[rows, :] = _mm_nt(lam, bb_ref[...]) + dy_t * d_ref[...]
            gd_ref[...] += jnp.sum(dy_t * u_t, axis=0, keepdims=True)
            gcm_ref[...] += _mm_tn(s_scr[t + 1], dy_t)
            sp = s_scr[t]
            spr, spi = sp[:, :SW], sp[:, SW:]
            gar += jnp.sum(lr_ * spr + li_ * spi, axis=0, keepdims=True)
            gai += jnp.sum(li_ * spr - lr_ * spi, axis=0, keepdims=True)
        gar_ref[...] += gar
        gai_ref[...] += gai

    tab = pl.BlockSpec((CH, SW), lambda j, i: (0, j))
    stp = pl.BlockSpec((NT, SW), lambda j, i: (0, j))
    colblk = pl.BlockSpec((None, tb, UW), lambda j, i: (j, rev(i), 0))
    vec = lambda w: pl.BlockSpec((1, w), lambda j, i: (0, j))
    return pl.pallas_call(
        body, name="s5_scan_bwd", grid=(NJ, nb),
        in_specs=[colblk, colblk,
                  pl.BlockSpec((None, ncb, 2 * SW), lambda j, i: (j, rev(i), 0)),
                  pl.BlockSpec((None, UW, 2 * SW), lambda j, i: (j, 0, 0)),
                  pl.BlockSpec((None, 2 * SW, UW), lambda j, i: (j, 0, 0)),
                  pl.BlockSpec((None, UW, 2 * SW), lambda j, i: (j, 0, 0)),
                  vec(SW), vec(SW), stp, stp, tab, tab, tab, tab, vec(UW)] + [ANY_SPEC] * len(deps),
        out_specs=[colblk, vec(UW),
                   pl.BlockSpec((None, 2 * SW, UW), lambda j, i: (j, 0, 0)),
                   pl.BlockSpec((None, UW, 2 * SW), lambda j, i: (j, 0, 0)),
                   vec(SW), vec(SW)],
        out_shape=[_lane_blocks(L), SDS((1, D), F32),
                   SDS((NJ, 2 * SW, UW), F32), SDS((NJ, UW, 2 * SW), F32),
                   SDS((1, NSTATE), F32), SDS((1, NSTATE), F32)],
        scratch_shapes=[pltpu.VMEM((1, 2 * SW), F32), pltpu.VMEM((NT + 1, ncb, 2 * SW), F32),
                        pltpu.VMEM((NT, ncb, 2 * SW), F32), pltpu.VMEM((NT * UW, 2 * SW), MXU)],
        compiler_params=_cp(56, ARB2),
    )(u3, dypre3, states, bb, cm, cmt, abr, abi, pwr, pwi, par, pai, pbr, pbi, dskip, *deps)


def _rope_tables(L, inv):
    tm = min(512, L)

    def body(inv_ref, cos_ref, sin_ref):
        pos = (lax.broadcasted_iota(jnp.int32, (tm, DK // 2), 0) + pl.program_id(0) * tm).astype(F32)
        ang = pos * inv_ref[...]
        cos_ref[...] = jnp.cos(ang)
        sin_ref[...] = jnp.sin(ang)

    blk = pl.BlockSpec((tm, DK // 2), lambda i: (i, 0))
    return pl.pallas_call(body, name="rope_tables", grid=(L // tm,), in_specs=[_full((1, DK // 2))],
                          out_specs=[blk, blk], out_shape=[SDS((L, DK // 2), F32)] * 2)(inv)


def _rot(x, cos, sin):
    x1, x2 = x[:, :DK // 2], x[:, DK // 2:]
    return jnp.concatenate([x1 * cos - x2 * sin, x1 * sin + x2 * cos], axis=1)


def _unrot(d, cos, sin):
    d1, d2 = d[:, :DK // 2], d[:, DK // 2:]
    return jnp.concatenate([d1 * cos + d2 * sin, d2 * cos - d1 * sin], axis=1)


def _ret_decays(h):
    lg = LOG_G[h]
    n = lax.broadcasted_iota(jnp.int32, (CH, CH), 0)
    m = lax.broadcasted_iota(jnp.int32, (CH, CH), 1)
    diff = (n - m).astype(F32)
    decay = jnp.where(n >= m, jnp.exp(lg * jnp.maximum(diff, 0.0)), 0.0)
    idx = lax.broadcasted_iota(jnp.int32, (CH, 1), 0).astype(F32)
    xi = jnp.exp(lg * (idx + 1.0))
    zeta = jnp.exp(lg * (CH - 1.0 - idx))
    return decay, xi, zeta, math.exp(lg * CH)


def _ret_tables(dec_scr, vec_scr):
    for h in range(HEADS):
        decay, xi, zeta, _ = _ret_decays(h)
        dec_scr[h] = decay
        vec_scr[h] = jnp.concatenate([jnp.broadcast_to(xi, (CH, 128)), jnp.broadcast_to(zeta, (CH, 128))], axis=1)


def _group_norm(o):
    mu = jnp.mean(o, axis=-1, keepdims=True)
    oc = o - mu
    rstd = lax.rsqrt(jnp.mean(oc * oc, axis=-1, keepdims=True) + EPS)
    return oc * rstd, rstd


def _ret_fwd(p, cos, sin, gain):
    L = p.shape[0]
    nb = L // CH

    def body(q_ref, k_ref, v_ref, bz_ref, cos_ref, sin_ref, g_ref, yb_ref, st_ref, sc_ref, o_ref,
             state, dec_scr, vec_scr):
        @pl.when(pl.program_id(0) == 0)
        def _():
            state[...] = jnp.zeros_like(state)
            _ret_tables(dec_scr, vec_scr)

        cos, sin = cos_ref[...], sin_ref[...]
        act, _ = _silu_and_grad(bz_ref[...])
        for h in range(HEADS):
            hs = slice(h * DK, (h + 1) * DK)
            xi, zeta = vec_scr[h, :, 0:1], vec_scr[h, :, 128:129]
            s_prev = state[h]
            s_prev_b = s_prev.astype(MXU)
            st_ref[h] = s_prev_b
            v = v_ref[:, hs]
            qr = _rot(q_ref[:, hs], cos, sin)
            kr = _rot(k_ref[:, hs], cos, sin) * (DK ** -0.5)
            scores = (_mm_nt(qr, kr) * dec_scr[h]).astype(MXU)
            o = _mm(scores, v) + _mm(qr * xi, s_prev_b)
            sc_ref[:, h * CH:(h + 1) * CH] = scores
            o_ref[:, hs] = o
            state[h] = s_prev * math.exp(LOG_G[h] * CH) + _mm_tn(kr * zeta, v)
            on, _ = _group_norm(o)
            yb_ref[:, hs] = (on * g_ref[:, hs] * act[:, hs]).astype(yb_ref.dtype)

    col = lambda c: pl.BlockSpec((CH, D), lambda i: (i, c))
    rope = pl.BlockSpec((CH, DK // 2), lambda i: (i, 0))
    return pl.pallas_call(
        body, name="ret_fwd", grid=(nb,),
        in_specs=[col(2), col(3), col(4), col(5), rope, rope, _full((1, D))],
        out_specs=[pl.BlockSpec((CH, D), lambda i: (i, 0)),
                   pl.BlockSpec((None, HEADS, DK, DK), lambda i: (i, 0, 0, 0)),
                   pl.BlockSpec((CH, HEADS * CH), lambda i: (i, 0)), pl.BlockSpec((CH, D), lambda i: (i, 0))],
        out_shape=[SDS((L, D), MXU), SDS((nb, HEADS, DK, DK), MXU), SDS((L, HEADS * CH), MXU), SDS((L, D), F32)],
        scratch_shapes=[pltpu.VMEM((HEADS, DK, DK), F32), pltpu.VMEM((HEADS, CH, CH), F32),
                        pltpu.VMEM((HEADS, CH, 256), F32)],
        compiler_params=_cp(40, ("arbitrary",)),
    )(p, p, p, p, cos, sin, gain)


RET_BWD_CHUNKS = 2


def _ret_bwd(p, cos, sin, gain, states, scores, o, dx1, wout_e, du, daz, deps=NO_DEPS):
    L = p.shape[0]
    nc = min(RET_BWD_CHUNKS, L // CH)
    rb = nc * CH
    nb = L // rb
    rev = lambda i: nb - 1 - i

    def body(q_ref, k_ref, v_ref, bz_ref, cos_ref, sin_ref, g_ref, st_ref, sc_ref, o_ref, dx1_ref, wo_ref, du_ref,
             daz_ref, *rest):
        dp_ref, yb_ref, gg_ref, gstate, dec_scr, vec_scr = rest[len(deps):]

        @pl.when(pl.program_id(0) == 0)
        def _():
            gstate[...] = jnp.zeros_like(gstate)
            gg_ref[...] = jnp.zeros_like(gg_ref)
            _ret_tables(dec_scr, vec_scr)

        act_all, dact_all = _silu_and_grad(bz_ref[...])
        dyb_all = _mm_nt(dx1_ref[...], wo_ref[...])
        dp_ref[:, 0:D] = _from_lane_blocks(du_ref).astype(dp_ref.dtype)
        dp_ref[:, D:2 * D] = daz_ref[...]
        for c in reversed(range(nc)):
            rows = slice(c * CH, (c + 1) * CH)
            cos, sin = cos_ref[rows, :], sin_ref[rows, :]
            act, dact, dyb = act_all[rows, :], dact_all[rows, :], dyb_all[rows, :]
            for h in range(HEADS):
                hs = slice(h * DK, (h + 1) * DK)
                col = lambda part: slice((2 + part) * D + h * DK, (2 + part) * D + (h + 1) * DK)
                decay = dec_scr[h]
                xi, zeta = vec_scr[h, :, 0:1], vec_scr[h, :, 128:129]
                v = v_ref[rows, hs]
                s_prev_b = st_ref[c, h]
                qr = _rot(q_ref[rows, hs], cos, sin)
                kr = _rot(k_ref[rows, hs], cos, sin) * (DK ** -0.5)
                scores = sc_ref[rows, h * CH:(h + 1) * CH]
                on, rstd = _group_norm(o_ref[rows, hs])
                gain_h = g_ref[:, hs]
                out = on * gain_h
                yb_ref[rows, hs] = (out * act[:, hs]).astype(yb_ref.dtype)
                dyb_h = dyb[:, hs]
                dp_ref[rows, col(3)] = (dyb_h * out * dact[:, hs]).astype(dp_ref.dtype)
                dout = dyb_h * act[:, hs]
                gg_ref[:, hs] += jnp.sum(dout * on, axis=0, keepdims=True)
                don = dout * gain_h
                do = rstd * (don - jnp.mean(don, axis=-1, keepdims=True)
                             - on * jnp.mean(don * on, axis=-1, keepdims=True))
                gnext = gstate[h]
                gnext_b = gnext.astype(MXU)
                dscores = _mm_nt(do, v) * decay
                dp_ref[rows, col(2)] = (_mm_tn(scores, do) + _mm(kr * zeta, gnext_b)).astype(dp_ref.dtype)
                dqr = _mm(dscores, kr) + _mm_nt(do, s_prev_b) * xi
                dkr = _mm_tn(dscores, qr) + _mm_nt(v, gnext_b) * zeta
                gstate[h] = gnext * math.exp(LOG_G[h] * CH) + _mm_tn(qr * xi, do)
                dp_ref[rows, col(0)] = _unrot(dqr, cos, sin).astype(dp_ref.dtype)
                dp_ref[rows, col(1)] = (_unrot(dkr, cos, sin) * (DK ** -0.5)).astype(dp_ref.dtype)

    col = lambda c: pl.BlockSpec((rb, D), lambda i: (rev(i), c))
    rope = pl.BlockSpec((rb, DK // 2), lambda i: (rev(i), 0))
    outc = col(0)
    act_out = SDS((L, D), MXU)
    return pl.pallas_call(
        body, name="ret_bwd", grid=(nb,),
        in_specs=[col(2), col(3), col(4), col(5), rope, rope, _full((1, D)),
                  pl.BlockSpec((nc, HEADS, DK, DK), lambda i: (rev(i), 0, 0, 0)),
                  pl.BlockSpec((rb, HEADS * CH), lambda i: (rev(i), 0)), outc,
                  outc, pl.BlockSpec((D, D), lambda i: (1, 0)), _lane_block_spec(rb, rev), outc]
        + [ANY_SPEC] * len(deps),
        out_specs=[pl.BlockSpec((rb, NIN), lambda i: (rev(i), 0)), outc, _full((1, D))],
        out_shape=[SDS((L, NIN), MXU), act_out, SDS((1, D), F32)],
        scratch_shapes=[pltpu.VMEM((HEADS, DK, DK), F32), pltpu.VMEM((HEADS, CH, CH), F32),
                        pltpu.VMEM((HEADS, CH, 256), F32)],
        compiler_params=_cp(56, ("arbitrary",)),
    )(p, p, p, p, cos, sin, gain, states, scores, o, dx1, wout_e, du, daz, *deps)


def _out_even(x, ypre3, p, yb, wglu, bglu, wout):
    L = x.shape[0]
    tm = min(512, L)

    def body(x_ref, y_ref, az_ref, yb_ref, wg_ref, bg_ref, w_ref, o_ref):
        yg = _gelu(_from_lane_blocks(y_ref))
        t = _mm(yg, wg_ref[...]) + bg_ref[...]
        act, _ = _silu_and_grad(az_ref[...])
        ya = (yg * jax.nn.sigmoid(t) * act).astype(MXU)
        cat = jnp.concatenate([ya, yb_ref[...]], axis=1)
        o_ref[...] = x_ref[...] + jnp.dot(cat, w_ref[...], preferred_element_type=F32)

    row = pl.BlockSpec((tm, D), lambda i: (i, 0))
    return pl.pallas_call(
        body, name="out_even", grid=(L // tm,),
        in_specs=[row, _lane_block_spec(tm, lambda i: i), pl.BlockSpec((tm, D), lambda i: (i, 1)), row,
                  _full((D, D)), _full((1, D)), _full((DI, D))],
        out_specs=row, out_shape=SDS((L, D), F32), compiler_params=_cp(48, ("arbitrary",)),
    )(x, ypre3, p, yb, wglu, bglu, wout)


def _sgu_core(pv, gain, ws_ref, bs_ref):
    pu, pvv, z = pv[:, :DI], pv[:, DI:2 * DI], pv[:, 2 * DI:]
    u, gu = _gelu_and_grad(pu)
    v, gv = _gelu_and_grad(pvv)
    mu = jnp.mean(v, axis=-1, keepdims=True)
    vc = v - mu
    rstd = lax.rsqrt(jnp.mean(vc * vc, axis=-1, keepdims=True) + EPS)
    vhat = vc * rstd
    vn = vhat * gain
    t = lax.broadcasted_iota(jnp.int32, (CH, CH), 0)
    s_ = lax.broadcasted_iota(jnp.int32, (CH, CH), 1)
    mask = t >= s_
    wm = [jnp.where(mask, ws_ref[g], 0.0).astype(MXU) for g in range(SG)]
    s = jnp.concatenate([_mm(wm[g], vn[:, g * SGD:(g + 1) * SGD]) + bs_ref[g] for g in range(SG)], axis=1)
    return gu, gv, z, u, vhat, rstd, vn, mask, wm, s


SGU_CHUNKS = 2


def _sgu_fwd_bwd(p2, x1, gain, wsp, bsp, wout, fnorm, tgt):
    L = p2.shape[0]
    nc = min(SGU_CHUNKS, L // CH)
    rb = nc * CH

    def body(p_ref, x1_ref, g_ref, ws_ref, bs_ref, wo_ref, fn_ref, t_ref,
             dp_ref, y_ref, dx2_ref, gg_ref, gws_ref, gbs_ref, gfn_ref, loss_ref):
        @pl.when(pl.program_id(0) == 0)
        def _():
            gg_ref[...] = jnp.zeros_like(gg_ref)
            gws_ref[...] = jnp.zeros_like(gws_ref)
            gbs_ref[...] = jnp.zeros_like(gbs_ref)
            gfn_ref[...] = jnp.zeros_like(gfn_ref)
            loss_ref[...] = jnp.zeros_like(loss_ref)

        gain = g_ref[...]
        kept = []
        for c in range(nc):
            rows = slice(c * CH, (c + 1) * CH)
            gu, gv, z, u, vhat, rstd, vn, mask, wm, s = _sgu_core(p_ref[rows, :], gain, ws_ref, bs_ref)
            act, dact = _silu_and_grad(z)
            y_ref[rows, :] = (u * s * act).astype(MXU)
            kept.append((gu, gv, u, vhat, rstd, vn, mask, wm, s, act, dact))
        x2 = x1_ref[...] + jnp.dot(y_ref[...], wo_ref[...], preferred_element_type=F32)
        xhat, r = _rms(x2)
        fn = fn_ref[...]
        e = xhat * fn - t_ref[...]
        loss_ref[...] += 0.5 * jnp.sum(jnp.mean(e * e, axis=-1, keepdims=True), axis=0, keepdims=True)
        do = e * (1.0 / D)
        gfn_ref[...] += jnp.sum(do * xhat, axis=0, keepdims=True)
        dxhat = do * fn
        dx2 = r * (dxhat - xhat * jnp.mean(dxhat * xhat, axis=-1, keepdims=True))
        dx2_ref[...] = dx2
        dy_all = _mm_nt(dx2, wo_ref[...])
        for c in range(nc):
            rows = slice(c * CH, (c + 1) * CH)
            gu, gv, u, vhat, rstd, vn, mask, wm, s, act, dact = kept[c]
            dy = dy_all[rows, :]
            du = dy * s * act
            ds = dy * u * act
            dz = dy * u * s * dact
            dvn = []
            for g in range(SG):
                ds_g = ds[:, g * SGD:(g + 1) * SGD]
                vn_g = vn[:, g * SGD:(g + 1) * SGD]
                gbs_ref[g] += jnp.sum(ds_g, axis=1, keepdims=True)
                gws_ref[g] += jnp.where(mask, _mm_nt(ds_g, vn_g), 0.0)
                dvn.append(_mm_tn(wm[g], ds_g))
            dvn = jnp.concatenate(dvn, axis=1)
            gg_ref[...] += jnp.sum(dvn * vhat, axis=0, keepdims=True)
            dvhat = dvn * gain
            dv = rstd * (dvhat - jnp.mean(dvhat, axis=-1, keepdims=True)
                         - vhat * jnp.mean(dvhat * vhat, axis=-1, keepdims=True))
            dp_ref[rows, :] = jnp.concatenate([du * gu, dv * gv, dz], axis=1).astype(dp_ref.dtype)

    row = pl.BlockSpec((rb, D), lambda i: (i, 0))
    wide = pl.BlockSpec((rb, NIN), lambda i: (i, 0))
    return pl.pallas_call(
        body, name="sgu_fwd_bwd", grid=(L // rb,),
        in_specs=[wide, row, _full((1, DI)), _full((SG, CH, CH)), _full((SG, CH, 1)), _full((DI, D)),
                  _full((1, D)), row],
        out_specs=[wide, pl.BlockSpec((rb, DI), lambda i: (i, 0)), row,
                   _full((1, DI)), _full((SG, CH, CH)), _full((SG, CH, 1)), _full((1, D)), _full((1, 128))],
        out_shape=[SDS((L, NIN), MXU), SDS((L, DI), MXU), SDS((L, D), F32), SDS((1, DI), F32),
                   SDS((SG, CH, CH), F32), SDS((SG, CH, 1), F32), SDS((1, D), F32), SDS((1, 128), F32)],
        compiler_params=_cp(60, ("arbitrary",)),
    )(p2, x1, gain, wsp, bsp, wout, fnorm, tgt)


def _my_index():
    return 4 * lax.axis_index("x") + 2 * lax.axis_index("y") + lax.axis_index("c")


def _ordered_sum(land_ref, own, me):
    g = None
    for s in range(NDEV):
        part = jnp.where(me == s, own, land_ref[s].astype(F32))
        g = part if g is None else g + part
    return g


def _adamw_math(w, m, v, g):
    mn = ADAM_B1 * m + (1.0 - ADAM_B1) * g
    vn = ADAM_B2 * v + (1.0 - ADAM_B2) * (g * g)
    mhat = mn / BC1
    vhat = vn / BC2
    return g, -ADAM_LR * (mhat / (jnp.sqrt(vhat) + ADAM_EPS) + ADAM_WD * w), mn, vn


def _adamw(w, m, v, land, own, name):
    R, C = w.shape
    tr = R
    for cand in (256, 128, 64, 32, 16, 8):
        if R % cand == 0 and R > cand:
            tr = cand
            break

    def body(w_ref, m_ref, v_ref, land_ref, own_ref, g_ref, d_ref, mo_ref, vo_ref):
        g = _ordered_sum(land_ref, own_ref[...].astype(F32), _my_index())
        for o, val in zip((g_ref, d_ref, mo_ref, vo_ref), _adamw_math(w_ref[...], m_ref[...], v_ref[...], g)):
            o[...] = val

    blk = pl.BlockSpec((tr, C), lambda i: (i, 0))
    out = SDS((R, C), F32)
    return pl.pallas_call(
        body, name=name, grid=(R // tr,),
        in_specs=[blk, blk, blk, pl.BlockSpec((NDEV, tr, C), lambda i: (0, i, 0)), blk],
        out_specs=[blk, blk, blk, blk], out_shape=[out, out, out, out],
        compiler_params=_cp(40, ("arbitrary",)),
    )(w, m, v, land, own)


def _adamw_many(ws, ms, vs, lands, owns, name):
    k = len(ws)

    def body(*refs):
        ins, outs = refs[:5 * k], refs[5 * k:]
        me = _my_index()
        for i in range(k):
            w_ref, m_ref, v_ref, land_ref, own_ref = (ins[j * k + i] for j in range(5))
            g = _ordered_sum(land_ref, own_ref[...], me)
            for j, val in enumerate(_adamw_math(w_ref[...], m_ref[...], v_ref[...], g)):
                outs[j * k + i][...] = val

    out_shape = [SDS(w.shape, F32) for _ in range(4) for w in ws]
    res = pl.pallas_call(body, name=name, out_shape=out_shape, compiler_params=_cp(60))(*ws, *ms, *vs, *lands, *owns)
    return [res[j * k:(j + 1) * k] for j in range(4)]


MESH = pl.DeviceIdType.MESH
HBM_SPEC = pl.BlockSpec(memory_space=pltpu.HBM)
SEM_SPEC = pl.BlockSpec(memory_space=pltpu.SEMAPHORE)
EFFECT = pltpu.SideEffectType.DATAFLOW_SIDE_EFFECTING


def _me_and_peers():
    x, y, c = lax.axis_index("x"), lax.axis_index("y"), lax.axis_index("c")
    me = 4 * x + 2 * y + c
    peers = []
    for r in range(1, NDEV):
        px, py, pc = x ^ ((r >> 2) & 1), y ^ ((r >> 1) & 1), c ^ (r & 1)
        peers.append(((px, py, pc), 4 * px + 2 * py + pc))
    return me, peers


def _land_shape(a, scatter):
    return (NDEV,) + (a.shape[1:] if scatter else a.shape)


def _remote(src, dst, send_sems, recv_sems, r, k, n, dev):
    i = r * n + k
    return pltpu.make_async_remote_copy(src_ref=src, dst_ref=dst, send_sem=send_sems.at[i], recv_sem=recv_sems.at[i],
                                        device_id=dev, device_id_type=MESH)


def _exchange(arrays, scatter, name):
    n = len(arrays)
    out_shape = [SDS(_land_shape(a, scatter), a.dtype) for a in arrays]

    def body(*refs):
        ins, outs = refs[:n], refs[n:2 * n]
        send_sems, recv_sems, loc_sems = refs[2 * n:]
        me, peers = _me_and_peers()
        local = []
        for k in range(n):
            src = ins[k].at[me] if scatter else ins[k]
            cp = pltpu.make_async_copy(src, outs[k].at[me], loc_sems.at[k])
            cp.start()
            local.append(cp)
        sends = []
        for r, (dev, lin) in enumerate(peers):
            for k in range(n):
                src = ins[k].at[lin] if scatter else ins[k]
                cp = _remote(src, outs[k].at[me], send_sems, recv_sems, r, k, n, dev)
                cp.start()
                sends.append(cp)
        for r, (dev, lin) in enumerate(peers):
            for k in range(n):
                src = ins[k].at[me] if scatter else ins[k]
                _remote(src, outs[k].at[lin], send_sems, recv_sems, r, k, n, dev).wait_recv()
        for cp in sends:
            cp.wait_send()
        for cp in local:
            cp.wait()

    return pl.pallas_call(
        body, name=name, in_specs=[HBM_SPEC] * n, out_specs=[HBM_SPEC] * n, out_shape=out_shape,
        scratch_shapes=[pltpu.SemaphoreType.DMA(((NDEV - 1) * n,)), pltpu.SemaphoreType.DMA(((NDEV - 1) * n,)),
                        pltpu.SemaphoreType.DMA((n,))],
    )(*arrays)


def _exchange_start(arrays, scatter, name):
    n = len(arrays)
    lands = [lax.empty(_land_shape(a, scatter), a.dtype) for a in arrays]

    def body(*refs):
        ins, lnd = refs[:n], refs[n:2 * n]
        send_sems, recv_sems, own_sems = refs[2 * n:2 * n + 3]
        token = refs[-1]
        me, peers = _me_and_peers()
        for r, (dev, lin) in enumerate(peers):
            for k in range(n):
                src = ins[k].at[lin] if scatter else ins[k]
                _remote(src, lnd[k].at[me], send_sems, recv_sems, r, k, n, dev).start()
        if not scatter:
            for k in range(n):
                pltpu.make_async_copy(ins[k], lnd[k].at[me], own_sems.at[k]).start()
        token[...] = jnp.zeros_like(token)

    sem = pltpu.SemaphoreType.DMA(((NDEV - 1) * n,))
    outs = pl.pallas_call(
        body, name=name,
        out_shape=(sem, sem, pltpu.SemaphoreType.DMA((n,)), *[pltpu.HBM(a.shape, a.dtype) for a in arrays],
                   *[pltpu.HBM(l.shape, l.dtype) for l in lands], SDS((8, 128), F32)),
        in_specs=[HBM_SPEC] * (2 * n),
        out_specs=(SEM_SPEC, SEM_SPEC, SEM_SPEC, *[HBM_SPEC] * (2 * n), pl.BlockSpec(memory_space=pltpu.VMEM)),
        input_output_aliases={k: 3 + k for k in range(2 * n)},
        compiler_params=pltpu.CompilerParams(has_side_effects=EFFECT),
    )(*[pltpu.with_memory_space_constraint(a, pltpu.HBM) for a in arrays],
      *[pltpu.with_memory_space_constraint(l, pltpu.HBM) for l in lands])
    return (n, scatter, outs[0], outs[1], outs[2], outs[3:3 + n], outs[3 + n:3 + 2 * n]), outs[-1]


def _exchange_wait(handle, after, name):
    n, scatter, send_sems, recv_sems, own_sems, thru, lands = handle
    after = tuple(after)

    def body(*refs):
        ins, lnd = refs[:n], refs[n:2 * n]
        send_sems, recv_sems, own_sems = refs[2 * n:2 * n + 3]
        me, peers = _me_and_peers()
        for r, (dev, lin) in enumerate(peers):
            for k in range(n):
                src = ins[k].at[lin] if scatter else ins[k]
                cp = _remote(src, lnd[k].at[lin], send_sems, recv_sems, r, k, n, dev)
                cp.wait_send()
                cp.wait_recv()
        if not scatter:
            for k in range(n):
                pltpu.make_async_copy(ins[k], lnd[k].at[me], own_sems.at[k]).wait()

    outs = pl.pallas_call(
        body, name=name,
        out_shape=(*[pltpu.HBM(a.shape, a.dtype) for a in thru], *[pltpu.HBM(l.shape, l.dtype) for l in lands]),
        in_specs=[HBM_SPEC] * (2 * n) + [SEM_SPEC, SEM_SPEC, SEM_SPEC] + [ANY_SPEC] * len(after),
        out_specs=tuple([HBM_SPEC] * (2 * n)),
        input_output_aliases={k: k for k in range(2 * n)},
        compiler_params=pltpu.CompilerParams(has_side_effects=EFFECT),
    )(*thru, *lands, send_sems, recv_sems, own_sems, *after)
    return list(outs[:n]), list(outs[n:])


CHIP_RELATIONS = (2, 4, 6)


def _peer(r):
    x, y, c = lax.axis_index("x"), lax.axis_index("y"), lax.axis_index("c")
    px, py, pc = x ^ ((r >> 2) & 1), y ^ ((r >> 1) & 1), c ^ (r & 1)
    return (px, py, pc), 4 * px + 2 * py + pc


def _copy(src, dst, send_sems, recv_sems, i, dev):
    return pltpu.make_async_remote_copy(src_ref=src, dst_ref=dst, send_sem=send_sems.at[i], recv_sem=recv_sems.at[i],
                                        device_id=dev, device_id_type=MESH)


def _gather2_start(a, name):
    land = lax.empty((NDEV,) + a.shape, a.dtype)

    def body(own, lnd, send_sems, recv_sems, own_sem, own_thru, lnd_thru, token):
        me = _my_index()
        for i, r in enumerate((1,) + CHIP_RELATIONS):
            dev, _ = _peer(r)
            _copy(own, lnd.at[me], send_sems, recv_sems, i, dev).start()
        pltpu.make_async_copy(own, lnd.at[me], own_sem.at[0]).start()
        token[...] = jnp.zeros_like(token)

    sem4 = pltpu.SemaphoreType.DMA((4,))
    outs = pl.pallas_call(
        body, name=name,
        out_shape=(sem4, sem4, pltpu.SemaphoreType.DMA((1,)), pltpu.HBM(a.shape, a.dtype),
                   pltpu.HBM(land.shape, land.dtype), SDS((8, 128), F32)),
        in_specs=[HBM_SPEC, HBM_SPEC],
        out_specs=(SEM_SPEC, SEM_SPEC, SEM_SPEC, HBM_SPEC, HBM_SPEC, pl.BlockSpec(memory_space=pltpu.VMEM)),
        input_output_aliases={0: 3, 1: 4},
        compiler_params=pltpu.CompilerParams(has_side_effects=EFFECT),
    )(pltpu.with_memory_space_constraint(a, pltpu.HBM), pltpu.with_memory_space_constraint(land, pltpu.HBM))
    return outs[:5], outs[5]


def _gather2_forward(handle, after, name):
    send_sems, recv_sems, own_sem, own, land = handle
    after = tuple(after)

    def body(lnd, recv_sems, *rest):
        send2, recv2, lnd_thru = rest[len(after):]
        sib, _ = _peer(1)
        for k, r in enumerate(CHIP_RELATIONS):
            dev, lin = _peer(r)
            _copy(lnd.at[lin], lnd.at[lin], recv_sems, recv_sems, 1 + k, dev).wait_recv()
            _copy(lnd.at[lin], lnd.at[lin], send2, recv2, k, sib).start()

    sem3 = pltpu.SemaphoreType.DMA((3,))
    send2, recv2, land = pl.pallas_call(
        body, name=name,
        out_shape=(sem3, sem3, pltpu.HBM(land.shape, land.dtype)),
        in_specs=[HBM_SPEC, SEM_SPEC] + [ANY_SPEC] * len(after),
        out_specs=(SEM_SPEC, SEM_SPEC, HBM_SPEC),
        input_output_aliases={0: 2},
        compiler_params=pltpu.CompilerParams(has_side_effects=EFFECT),
    )(land, recv_sems, *after)
    return send_sems, recv_sems, own_sem, send2, recv2, own, land


def _gather2_wait(handle, name):
    send_sems, recv_sems, own_sem, send2, recv2, own, land = handle

    def body(own_ref, lnd, send_sems, recv_sems, own_sem, send2, recv2, own_thru, lnd_thru):
        me = _my_index()
        sib, sib_lin = _peer(1)
        for i, r in enumerate((1,) + CHIP_RELATIONS):
            dev, _ = _peer(r)
            _copy(own_ref, lnd.at[me], send_sems, recv_sems, i, dev).wait_send()
        _copy(own_ref, lnd.at[sib_lin], send_sems, recv_sems, 0, sib).wait_recv()
        for k, r in enumerate(CHIP_RELATIONS):
            _, lin = _peer(r)
            _, lin_other = _peer(r ^ 1)
            _copy(lnd.at[lin], lnd.at[lin], send2, recv2, k, sib).wait_send()
            _copy(lnd.at[lin_other], lnd.at[lin_other], send2, recv2, k, sib).wait_recv()
        pltpu.make_async_copy(own_ref, lnd.at[me], own_sem.at[0]).wait()

    outs = pl.pallas_call(
        body, name=name,
        out_shape=(pltpu.HBM(own.shape, own.dtype), pltpu.HBM(land.shape, land.dtype)),
        in_specs=[HBM_SPEC, HBM_SPEC] + [SEM_SPEC] * 5,
        out_specs=(HBM_SPEC, HBM_SPEC),
        input_output_aliases={0: 0, 1: 1},
        compiler_params=pltpu.CompilerParams(has_side_effects=EFFECT),
    )(own, land, send_sems, recv_sems, own_sem, send2, recv2)
    return outs[1]


def _local_step(x, tgt, norm_even, first_weight, lam_re, lam_im, log_dt, b_re, b_im, c_re, c_im, s5_d, bglu,
                ret_gain, wsp, bsp, fnorm, late_weights, emit, start_token=None):
    L = x.shape[0]
    lr3, li3 = lam_re.reshape(G, 1, P), lam_im.reshape(G, 1, P)
    dt3 = log_dt.reshape(G, 1, 1)
    br3, bi3 = jnp.swapaxes(b_re, 1, 2), jnp.swapaxes(b_im, 1, 2)
    abr3, abi3, bbr3, bbi3 = _s5_disc(lr3, li3, dt3, br3, bi3)
    bb = jnp.concatenate([_embed(bbr3), _embed(bbi3)], axis=2).astype(MXU)
    cm = jnp.concatenate([_embed(jnp.swapaxes(c_re, 1, 2)), -_embed(jnp.swapaxes(c_im, 1, 2))], axis=1).astype(MXU)
    abr, abi = abr3.reshape(1, NSTATE), abi3.reshape(1, NSTATE)
    pwr, pwi, _, _ = _s5_tables(abr, abi, NT, "s5_tables_step")
    par, pai, pbr, pbi = _s5_tables(pwr[NT - 1:NT], pwi[NT - 1:NT], CH, "s5_tables_chunk")
    inv = (ROPE_BASE ** (-jnp.arange(DK // 2, dtype=F32) / (DK // 2))).reshape(1, DK // 2)
    cos, sin = _rope_tables(L, inv)
    bsp3 = bsp.reshape(SG, CH, 1)

    def dep(token):
        return NO_DEPS if token is None else (token,)

    win_e = first_weight((cos, pbi, cm))
    p, u3 = _in_proj(x, norm_even, win_e, "in_even", dep(start_token), lane_blocks=True)
    ypre, s5_states = _s5_scan_fwd(u3, bb, cm, pwr, pwi, pwr.T, pwi.T, par, pai, s5_d)
    yb, ret_states, ret_scores, ret_o = _ret_fwd(p, cos, sin, ret_gain)
    wglu, wout_e, norm_odd, win_o, sgu_gain, wout_o = late_weights((ypre, yb))
    x1 = _out_even(x, ypre, p, yb, wglu, bglu, wout_e)
    p2 = _in_proj(x1, norm_odd, win_o, "in_odd")
    dp2, y_o, dx2, g_sgu_gain, g_wsp, g_bsp, g_fnorm, loss = _sgu_fwd_bwd(
        p2, x1, sgu_gain, wsp, bsp3, wout_o, fnorm, tgt)

    g_wout_o = _wgrad_rows([y_o], dx2, "wgrad_out_odd")
    g_win_o = _wgrad_cols(x1, norm_odd, dp2, "wgrad_in_odd")
    tok = emit("odd", dict(w_in_odd=g_win_o, w_out_odd=g_wout_o))
    dx1, g_norm_odd = _in_proj_bwd_x(dp2, x1, norm_odd, win_o, dx2, "in_odd_bwd", dep(tok))
    tok = emit("small_odd", dict(norm_odd=g_norm_odd, sgu_norm_gain=g_sgu_gain, sgu_w_spatial=g_wsp,
                                 sgu_b_spatial=g_bsp.reshape(SG, CH), final_norm=g_fnorm))

    dypre, daz, yg, dt, ya2, g_bglu = _s5_gate_bwd(ypre, p, dx1, wout_e, wglu, bglu, dep(tok))
    g_wglu = _wgrad_rows([yg], dt, "wgrad_glu")
    tok = emit("glu", dict(s5_w_glu=g_wglu))
    du, g_d, g_cm, g_bb, g_ar, g_ai = _s5_scan_bwd(u3, dypre, s5_states, bb, cm, jnp.swapaxes(cm, 1, 2), abr, abi,
                                                   pwr, pwi, par, pai, pbr, pbi, s5_d, dep(tok))
    dbbr3 = _diag_blocks(g_bb[:, :, :SW], HG, P)
    dbbi3 = _diag_blocks(g_bb[:, :, SW:], HG, P)
    g_c_re = jnp.swapaxes(_diag_blocks(g_cm[:, :SW, :], P, HG), 1, 2)
    g_c_im = -jnp.swapaxes(_diag_blocks(g_cm[:, SW:, :], P, HG), 1, 2)
    g_lr3, g_li3, g_dt3, g_br3, g_bi3 = _s5_disc_bwd(
        lr3, li3, dt3, br3, bi3, g_ar.reshape(G, 1, P), g_ai.reshape(G, 1, P), dbbr3, dbbi3)
    tok = emit("small_s5", dict(
        s5_lam_re=g_lr3.reshape(G, P), s5_lam_im=g_li3.reshape(G, P), s5_log_dt=g_dt3.reshape(1, G),
        s5_b_re=g_br3, s5_b_im=g_bi3, s5_c_re=g_c_re, s5_c_im=g_c_im, s5_d=g_d, s5_b_glu=g_bglu))
    dp, yb2, g_ret_gain = _ret_bwd(p, cos, sin, ret_gain, ret_states, ret_scores, ret_o, dx1, wout_e, du, daz,
                                   dep(tok))
    g_win_e = _wgrad_cols(x, norm_even, dp, "wgrad_in_even")
    tok = emit("even_cols", dict(w_in_even=g_win_e))
    g_wout_e = _wgrad_rows([ya2, yb2], dx1, "wgrad_out_even", dep(tok))
    tok = emit("even_rows", dict(w_out_even=g_wout_e))
    dx, g_norm_even = _in_proj_bwd_x(dp, x, norm_even, win_e, dx1, "in_even_bwd", dep(tok))
    emit("last", dict(ret_gn_gain=g_ret_gain, norm_even=g_norm_even, loss=loss))
    return dx


WEIGHTS = ['norm_even', 'w_in_even', 's5_lam_re', 's5_lam_im', 's5_log_dt', 's5_b_re', 's5_b_im', 's5_c_re',
           's5_c_im', 's5_d', 's5_w_glu', 's5_b_glu', 'ret_gn_gain', 'w_out_even', 'norm_odd', 'w_in_odd',
           'sgu_norm_gain', 'sgu_w_spatial', 'sgu_b_spatial', 'w_out_odd', 'final_norm']
BIG = ['w_in_even', 's5_w_glu', 'w_out_even', 'w_in_odd', 'w_out_odd']
SHARDED_SMALL = {'norm_odd': D // NDEV, 'sgu_norm_gain': DI // NDEV}
SCATTER_STAGES = ("odd", "glu", "even_cols", "even_rows")
GATHER_STAGES = ("small_odd", "small_s5")


def _view(n, a):
    if n in ('s5_b_re', 's5_b_im'):
        return jnp.swapaxes(a[0], 1, 2)
    if n == 'final_norm':
        return a.reshape(1, D)
    return a[0] if a.ndim >= 3 else a


def _unview(n, t, shape):
    if n in ('s5_b_re', 's5_b_im'):
        return jnp.swapaxes(t, 1, 2)[None]
    return t.reshape(shape)


def kernel(x, norm_even, w_in_even, s5_lam_re, s5_lam_im, s5_log_dt, s5_b_re, s5_b_im, s5_c_re, s5_c_im, s5_d, s5_w_glu, s5_b_glu, ret_gn_gain, w_out_even, norm_odd, w_in_odd, sgu_norm_gain, sgu_w_spatial, sgu_b_spatial, w_out_odd, final_norm, loss_target, m_norm_even, m_w_in_even, m_s5_lam_re, m_s5_lam_im, m_s5_log_dt, m_s5_b_re, m_s5_b_im, m_s5_c_re, m_s5_c_im, m_s5_d, m_s5_w_glu, m_s5_b_glu, m_ret_gn_gain, m_w_out_even, m_norm_odd, m_w_in_odd, m_sgu_norm_gain, m_sgu_w_spatial, m_sgu_b_spatial, m_w_out_odd, m_final_norm, v_norm_even, v_w_in_even, v_s5_lam_re, v_s5_lam_im, v_s5_log_dt, v_s5_b_re, v_s5_b_im, v_s5_c_re, v_s5_c_im, v_s5_d, v_s5_w_glu, v_s5_b_glu, v_ret_gn_gain, v_w_out_even, v_norm_odd, v_w_in_odd, v_sgu_norm_gain, v_sgu_w_spatial, v_sgu_b_spatial, v_w_out_odd, v_final_norm):
    args = dict(locals())
    w = {n: args[n] for n in WEIGHTS}
    m = {n: args["m_" + n] for n in WEIGHTS}
    v = {n: args["v_" + n] for n in WEIGHTS}
    me = _my_index()

    first_handle, _ = _gather2_start(w['w_in_even'][0].astype(MXU), "gather_first_start")

    def first_weight(after):
        return _gather2_wait(_gather2_forward(first_handle, after, "gather_first_forward"), "gather_first_wait")

    late_own = [w['s5_w_glu'][0].astype(MXU), w['w_out_even'][0].astype(MXU), w['norm_odd'],
                w['w_in_odd'][0].astype(MXU), w['sgu_norm_gain'], w['w_out_odd'][0].astype(MXU)]
    late_handle, start_token = _exchange_start(late_own, False, "gather_late_start")

    def late_weights(after):
        _, (wglu, wout_e, nodd, win_o, sgug, wout_o) = _exchange_wait(late_handle, after, "gather_late_wait")
        return (wglu.reshape(D, D), wout_e.reshape(DI, D), nodd.reshape(1, D), win_o, sgug.reshape(1, DI),
                wout_o.reshape(DI, D))

    pending = {}
    small_last = {}

    def emit(stage, grads):
        if stage == "last":
            small_last.update(grads)
            return None
        names = list(grads)
        handle, token = _exchange_start([grads[n] for n in names], stage in SCATTER_STAGES, stage + "_start")
        pending[stage] = (handle, names)
        return token

    dx = _local_step(
        x[0], loss_target[0], w['norm_even'], first_weight, w['s5_lam_re'][0], w['s5_lam_im'][0], w['s5_log_dt'][0],
        w['s5_b_re'][0], w['s5_b_im'][0], w['s5_c_re'][0], w['s5_c_im'][0], w['s5_d'], w['s5_b_glu'],
        w['ret_gn_gain'], w['sgu_w_spatial'][0], w['sgu_b_spatial'][0], w['final_norm'].reshape(1, D),
        late_weights, emit, start_token)

    out_g, out_d, out_m, out_v = {}, {}, {}, {}
    after = dx
    for stage in SCATTER_STAGES:
        handle, names = pending[stage]
        sent, lands = _exchange_wait(handle, (after,), stage + "_wait")
        for n, land, stack in zip(names, lands, sent):
            shp = w[n].shape
            r, c = shp[1], shp[2]
            own = lax.dynamic_index_in_dim(stack, me, 0, keepdims=False)
            res = _adamw(w[n].reshape(r, c), m[n].reshape(r, c), v[n].reshape(r, c), land, own, "adamw_" + n)
            out_g[n], out_d[n], out_m[n], out_v[n] = (t.reshape(shp) for t in res)
            after = res[0]

    names, owns, lands = [], [], []
    for stage in GATHER_STAGES:
        handle, stage_names = pending[stage]
        sent, got = _exchange_wait(handle, (after,), stage + "_wait")
        names, owns, lands = names + stage_names, owns + sent, lands + got
    last_names = list(small_last)
    last = _exchange([small_last[n] for n in last_names], False, "gather_last")
    loss_parts = last[last_names.index("loss")][:, 0, 0]
    for n, own, land in zip(last_names, (small_last[n] for n in last_names), last):
        if n != "loss":
            names, owns, lands = names + [n], owns + [own], lands + [land]
    for i, n in enumerate(names):
        if n in SHARDED_SMALL:
            width = SHARDED_SMALL[n]
            owns[i] = lax.dynamic_slice_in_dim(owns[i], me * width, width, axis=1)
            lands[i] = lax.dynamic_slice_in_dim(lands[i], me * width, width, axis=2)
    res = _adamw_many([_view(n, w[n]) for n in names], [_view(n, m[n]) for n in names],
                      [_view(n, v[n]) for n in names], lands, owns, "adamw_small")
    for dst, vals in zip((out_g, out_d, out_m, out_v), res):
        for n, t in zip(names, vals):
            dst[n] = _unview(n, t, w[n].shape)

    loss_total = jnp.sum(loss_parts)
    return (loss_total, dx[None], *[out_g[n] for n in WEIGHTS], *[out_d[n] for n in WEIGHTS],
            *[out_m[n] for n in WEIGHTS], *[out_v[n] for n in WEIGHTS])
```
